```python
import math
import jax, jax.numpy as jnp
from jax import lax
import numpy as np

D_MODEL = 1024
BATCH = 8
SEQ = 16384
DEPTH = 2

N_META = 16
N_MIXERS = 2
N_A_LAYERS = (DEPTH + 1) // 2
N_B_LAYERS = DEPTH // 2
SC_WIDTH = 3
D_RNN = 1280
RG_BLOCKS = 10
RG_BLOCK_DIM = D_RNN // RG_BLOCKS
RG_CONV_WIDTH = 4
RG_C = 8.0
D_FF = 2816
FFN_CONV_WIDTH = 3
RMS_EPS = 1e-6

kernel_name = "hybrid_shortconv_rglru_convffn"


def rms_norm(x, g):
    xf = x.astype(jnp.float32)
    var = jnp.mean(xf * xf, axis=-1, keepdims=True)
    return (xf * lax.rsqrt(var + RMS_EPS) * g.astype(jnp.float32)).astype(x.dtype)


def causal_dwconv(x, w):
    k_width = w.shape[0]
    t_len = x.shape[1]
    xp = jnp.pad(x, ((0, 0), (k_width - 1, 0), (0, 0)))
    y = xp[:, 0:t_len] * w[0]
    for k in range(1, k_width):
        y = y + xp[:, k:k + t_len] * w[k]
    return y


def short_conv_mixer(x, w_in, conv_w, w_out):
    h = jnp.einsum('btd,de->bte', x, w_in)
    b_gate, c_gate, v = jnp.split(h, 3, axis=-1)
    u = causal_dwconv(c_gate * v, conv_w)
    return jnp.einsum('btd,de->bte', b_gate * u, w_out)


def _lin_rec_combine(left, right):
    a_l, b_l = left
    a_r, b_r = right
    return a_l * a_r, a_r * b_l + b_r


def rglru_block(x, w_in, conv_w, conv_b, w_gate_a, b_gate_a, w_gate_x, b_gate_x, lam, w_out):
    bsz, t_len, _ = x.shape
    h = jnp.einsum('btd,de->bte', x, w_in)
    g_branch, r_branch = jnp.split(h, 2, axis=-1)
    gate = jax.nn.gelu(g_branch, approximate=True)
    u = causal_dwconv(r_branch, conv_w) + conv_b
    ub = u.reshape(bsz, t_len, RG_BLOCKS, RG_BLOCK_DIM)
    r = jax.nn.sigmoid(jnp.einsum('btki,kij->btkj', ub, w_gate_a).reshape(bsz, t_len, D_RNN) + b_gate_a)
    i = jax.nn.sigmoid(jnp.einsum('btki,kij->btkj', ub, w_gate_x).reshape(bsz, t_len, D_RNN) + b_gate_x)
    log_a = -RG_C * r.astype(jnp.float32) * jax.nn.softplus(-lam.astype(jnp.float32))
    a = jnp.exp(log_a)
    mult = jnp.sqrt(-jnp.expm1(2.0 * log_a))
    b = mult * (i * u).astype(jnp.float32)
    _, hs = lax.associative_scan(_lin_rec_combine, (a, b), axis=1)
    y = hs.astype(x.dtype) * gate
    return jnp.einsum('bte,ed->btd', y, w_out)


def conv_gated_mlp(x, w_up, conv_w, w_down):
    h = jnp.einsum('btd,df->btf', x, w_up)
    h = causal_dwconv(h, conv_w)
    g, v = jnp.split(h, 2, axis=-1)
    return jnp.einsum('btf,fd->btd', jax.nn.silu(g) * v, w_down)


def _fwd_setup_inputs(seed: int = 0) -> dict:
    key = jax.random.key(seed)
    ks = jax.random.split(key, 24)
    f32 = jnp.float32
    D = D_MODEL

    def nrm(k, shape, scale):
        return jax.random.normal(k, shape, f32) * scale

    x = jax.random.normal(ks[0], (BATCH, SEQ, D), f32)
    meta_tokens = nrm(ks[1], (N_META, D), 1.0)
    norm_mix_g = 1.0 + nrm(ks[2], (DEPTH, D), 0.01)
    norm_ffn_g = 1.0 + nrm(ks[3], (DEPTH, D), 0.01)
    final_norm_g = 1.0 + nrm(ks[4], (D,), 0.01)

    sc_w_in = nrm(ks[5], (N_A_LAYERS, D, 3 * D), D ** -0.5)
    sc_conv_w = nrm(ks[6], (N_A_LAYERS, SC_WIDTH, D), SC_WIDTH ** -0.5)
    sc_w_out = nrm(ks[7], (N_A_LAYERS, D, D), D ** -0.5)

    rg_w_in = nrm(ks[8], (N_B_LAYERS, D, 2 * D_RNN), D ** -0.5)
    rg_conv_w = nrm(ks[9], (N_B_LAYERS, RG_CONV_WIDTH, D_RNN), RG_CONV_WIDTH ** -0.5)
    rg_conv_b = nrm(ks[10], (N_B_LAYERS, D_RNN), 0.01)
    rg_w_gate_a = nrm(ks[11], (N_B_LAYERS, RG_BLOCKS, RG_BLOCK_DIM, RG_BLOCK_DIM), RG_BLOCK_DIM ** -0.5)
    rg_b_gate_a = nrm(ks[12], (N_B_LAYERS, D_RNN), 0.01)
    rg_w_gate_x = nrm(ks[13], (N_B_LAYERS, RG_BLOCKS, RG_BLOCK_DIM, RG_BLOCK_DIM), RG_BLOCK_DIM ** -0.5)
    rg_b_gate_x = nrm(ks[14], (N_B_LAYERS, D_RNN), 0.01)
    a_c = jax.random.uniform(ks[15], (N_B_LAYERS, D_RNN), f32, 0.9, 0.999)
    a_base = a_c ** (1.0 / RG_C)
    rg_lambda = jnp.log(a_base) - jnp.log1p(-a_base)
    rg_w_out = nrm(ks[16], (N_B_LAYERS, D_RNN, D), D_RNN ** -0.5)

    ffn_w_up = nrm(ks[17], (DEPTH, D, 2 * D_FF), D ** -0.5)
    ffn_conv_w = nrm(ks[18], (DEPTH, FFN_CONV_WIDTH, 2 * D_FF), FFN_CONV_WIDTH ** -0.5)
    ffn_w_down = nrm(ks[19], (DEPTH, D_FF, D), D_FF ** -0.5)

    return {"x": x, "meta_tokens": meta_tokens, "norm_mix_g": norm_mix_g,
            "norm_ffn_g": norm_ffn_g, "final_norm_g": final_norm_g,
            "sc_w_in": sc_w_in, "sc_conv_w": sc_conv_w, "sc_w_out": sc_w_out,
            "rg_w_in": rg_w_in, "rg_conv_w": rg_conv_w, "rg_conv_b": rg_conv_b,
            "rg_w_gate_a": rg_w_gate_a, "rg_b_gate_a": rg_b_gate_a,
            "rg_w_gate_x": rg_w_gate_x, "rg_b_gate_x": rg_b_gate_x,
            "rg_lambda": rg_lambda, "rg_w_out": rg_w_out,
            "ffn_w_up": ffn_w_up, "ffn_conv_w": ffn_conv_w, "ffn_w_down": ffn_w_down}


def _fwd_reference(x, meta_tokens, norm_mix_g, norm_ffn_g, final_norm_g,
              sc_w_in, sc_conv_w, sc_w_out,
              rg_w_in, rg_conv_w, rg_conv_b, rg_w_gate_a, rg_b_gate_a,
              rg_w_gate_x, rg_b_gate_x, rg_lambda, rg_w_out,
              ffn_w_up, ffn_conv_w, ffn_w_down):
    bsz = x.shape[0]
    meta = jnp.broadcast_to(meta_tokens.astype(x.dtype)[None], (bsz, N_META, x.shape[-1]))
    h = jnp.concatenate([meta, x], axis=1)
    for layer in range(DEPTH):
        hn = rms_norm(h, norm_mix_g[layer])
        j = layer // N_MIXERS
        if layer % N_MIXERS == 0:
            mix = short_conv_mixer(hn, sc_w_in[j], sc_conv_w[j], sc_w_out[j])
        else:
            mix = rglru_block(hn, rg_w_in[j], rg_conv_w[j], rg_conv_b[j],
                              rg_w_gate_a[j], rg_b_gate_a[j], rg_w_gate_x[j], rg_b_gate_x[j],
                              rg_lambda[j], rg_w_out[j])
        h = h + mix
        h = h + conv_gated_mlp(rms_norm(h, norm_ffn_g[layer]), ffn_w_up[layer],
                               ffn_conv_w[layer], ffn_w_down[layer])
    out = rms_norm(h, final_norm_g)
    return out[:, N_META:]


import jax as _jax
import jax.numpy as _jnp

TWIN_FORMAT = 'train_step'
FWD_PARAMS = ['x', 'meta_tokens', 'norm_mix_g', 'norm_ffn_g', 'final_norm_g', 'sc_w_in', 'sc_conv_w', 'sc_w_out', 'rg_w_in', 'rg_conv_w', 'rg_conv_b', 'rg_w_gate_a', 'rg_b_gate_a', 'rg_w_gate_x', 'rg_b_gate_x', 'rg_lambda', 'rg_w_out', 'ffn_w_up', 'ffn_conv_w', 'ffn_w_down']
TWIN_WEIGHTS = ['meta_tokens', 'norm_mix_g', 'norm_ffn_g', 'final_norm_g', 'sc_w_in', 'sc_conv_w', 'sc_w_out', 'rg_w_in', 'rg_conv_w', 'rg_conv_b', 'rg_w_gate_a', 'rg_b_gate_a', 'rg_w_gate_x', 'rg_b_gate_x', 'rg_lambda', 'rg_w_out', 'ffn_w_up', 'ffn_conv_w', 'ffn_w_down']
TWIN_DIFF_INPUT = 'x'
TWIN_INPUTS = ['x', 'meta_tokens', 'norm_mix_g', 'norm_ffn_g', 'final_norm_g', 'sc_w_in', 'sc_conv_w', 'sc_w_out', 'rg_w_in', 'rg_conv_w', 'rg_conv_b', 'rg_w_gate_a', 'rg_b_gate_a', 'rg_w_gate_x', 'rg_b_gate_x', 'rg_lambda', 'rg_w_out', 'ffn_w_up', 'ffn_conv_w', 'ffn_w_down', 'loss_target', 'm_meta_tokens', 'm_norm_mix_g', 'm_norm_ffn_g', 'm_final_norm_g', 'm_sc_w_in', 'm_sc_conv_w', 'm_sc_w_out', 'm_rg_w_in', 'm_rg_conv_w', 'm_rg_conv_b', 'm_rg_w_gate_a', 'm_rg_b_gate_a', 'm_rg_w_gate_x', 'm_rg_b_gate_x', 'm_rg_lambda', 'm_rg_w_out', 'm_ffn_w_up', 'm_ffn_conv_w', 'm_ffn_w_down', 'v_meta_tokens', 'v_norm_mix_g', 'v_norm_ffn_g', 'v_final_norm_g', 'v_sc_w_in', 'v_sc_conv_w', 'v_sc_w_out', 'v_rg_w_in', 'v_rg_conv_w', 'v_rg_conv_b', 'v_rg_w_gate_a', 'v_rg_b_gate_a', 'v_rg_w_gate_x', 'v_rg_b_gate_x', 'v_rg_lambda', 'v_rg_w_out', 'v_ffn_w_up', 'v_ffn_conv_w', 'v_ffn_w_down']
TWIN_OUTPUTS = ['loss', 'grad_x', 'grad_meta_tokens', 'grad_norm_mix_g', 'grad_norm_ffn_g', 'grad_final_norm_g', 'grad_sc_w_in', 'grad_sc_conv_w', 'grad_sc_w_out', 'grad_rg_w_in', 'grad_rg_conv_w', 'grad_rg_conv_b', 'grad_rg_w_gate_a', 'grad_rg_b_gate_a', 'grad_rg_w_gate_x', 'grad_rg_b_gate_x', 'grad_rg_lambda', 'grad_rg_w_out', 'grad_ffn_w_up', 'grad_ffn_conv_w', 'grad_ffn_w_down', 'delta_meta_tokens', 'delta_norm_mix_g', 'delta_norm_ffn_g', 'delta_final_norm_g', 'delta_sc_w_in', 'delta_sc_conv_w', 'delta_sc_w_out', 'delta_rg_w_in', 'delta_rg_conv_w', 'delta_rg_conv_b', 'delta_rg_w_gate_a', 'delta_rg_b_gate_a', 'delta_rg_w_gate_x', 'delta_rg_b_gate_x', 'delta_rg_lambda', 'delta_rg_w_out', 'delta_ffn_w_up', 'delta_ffn_conv_w', 'delta_ffn_w_down', 'new_m_meta_tokens', 'new_m_norm_mix_g', 'new_m_norm_ffn_g', 'new_m_final_norm_g', 'new_m_sc_w_in', 'new_m_sc_conv_w', 'new_m_sc_w_out', 'new_m_rg_w_in', 'new_m_rg_conv_w', 'new_m_rg_conv_b', 'new_m_rg_w_gate_a', 'new_m_rg_b_gate_a', 'new_m_rg_w_gate_x', 'new_m_rg_b_gate_x', 'new_m_rg_lambda', 'new_m_rg_w_out', 'new_m_ffn_w_up', 'new_m_ffn_conv_w', 'new_m_ffn_w_down', 'new_v_meta_tokens', 'new_v_norm_mix_g', 'new_v_norm_ffn_g', 'new_v_final_norm_g', 'new_v_sc_w_in', 'new_v_sc_conv_w', 'new_v_sc_w_out', 'new_v_rg_w_in', 'new_v_rg_conv_w', 'new_v_rg_conv_b', 'new_v_rg_w_gate_a', 'new_v_rg_b_gate_a', 'new_v_rg_w_gate_x', 'new_v_rg_b_gate_x', 'new_v_rg_lambda', 'new_v_rg_w_out', 'new_v_ffn_w_up', 'new_v_ffn_conv_w', 'new_v_ffn_w_down']
TWIN_LEAF_KINDS = {'loss': 'loss', 'grad_x': 'grad_x', 'grad_meta_tokens': 'grad_w', 'grad_norm_mix_g': 'grad_w', 'grad_norm_ffn_g': 'grad_w', 'grad_final_norm_g': 'grad_w', 'grad_sc_w_in': 'grad_w', 'grad_sc_conv_w': 'grad_w', 'grad_sc_w_out': 'grad_w', 'grad_rg_w_in': 'grad_w', 'grad_rg_conv_w': 'grad_w', 'grad_rg_conv_b': 'grad_w', 'grad_rg_w_gate_a': 'grad_w', 'grad_rg_b_gate_a': 'grad_w', 'grad_rg_w_gate_x': 'grad_w', 'grad_rg_b_gate_x': 'grad_w', 'grad_rg_lambda': 'grad_w', 'grad_rg_w_out': 'grad_w', 'grad_ffn_w_up': 'grad_w', 'grad_ffn_conv_w': 'grad_w', 'grad_ffn_w_down': 'grad_w', 'delta_meta_tokens': 'delta_w', 'delta_norm_mix_g': 'delta_w', 'delta_norm_ffn_g': 'delta_w', 'delta_final_norm_g': 'delta_w', 'delta_sc_w_in': 'delta_w', 'delta_sc_conv_w': 'delta_w', 'delta_sc_w_out': 'delta_w', 'delta_rg_w_in': 'delta_w', 'delta_rg_conv_w': 'delta_w', 'delta_rg_conv_b': 'delta_w', 'delta_rg_w_gate_a': 'delta_w', 'delta_rg_b_gate_a': 'delta_w', 'delta_rg_w_gate_x': 'delta_w', 'delta_rg_b_gate_x': 'delta_w', 'delta_rg_lambda': 'delta_w', 'delta_rg_w_out': 'delta_w', 'delta_ffn_w_up': 'delta_w', 'delta_ffn_conv_w': 'delta_w', 'delta_ffn_w_down': 'delta_w', 'new_m_meta_tokens': 'new_m', 'new_m_norm_mix_g': 'new_m', 'new_m_norm_ffn_g': 'new_m', 'new_m_final_norm_g': 'new_m', 'new_m_sc_w_in': 'new_m', 'new_m_sc_conv_w': 'new_m', 'new_m_sc_w_out': 'new_m', 'new_m_rg_w_in': 'new_m', 'new_m_rg_conv_w': 'new_m', 'new_m_rg_conv_b': 'new_m', 'new_m_rg_w_gate_a': 'new_m', 'new_m_rg_b_gate_a': 'new_m', 'new_m_rg_w_gate_x': 'new_m', 'new_m_rg_b_gate_x': 'new_m', 'new_m_rg_lambda': 'new_m', 'new_m_rg_w_out': 'new_m', 'new_m_ffn_w_up': 'new_m', 'new_m_ffn_conv_w': 'new_m', 'new_m_ffn_w_down': 'new_m', 'new_v_meta_tokens': 'new_v', 'new_v_norm_mix_g': 'new_v', 'new_v_norm_ffn_g': 'new_v', 'new_v_final_norm_g': 'new_v', 'new_v_sc_w_in': 'new_v', 'new_v_sc_conv_w': 'new_v', 'new_v_sc_w_out': 'new_v', 'new_v_rg_w_in': 'new_v', 'new_v_rg_conv_w': 'new_v', 'new_v_rg_conv_b': 'new_v', 'new_v_rg_w_gate_a': 'new_v', 'new_v_rg_b_gate_a': 'new_v', 'new_v_rg_w_gate_x': 'new_v', 'new_v_rg_b_gate_x': 'new_v', 'new_v_rg_lambda': 'new_v', 'new_v_rg_w_out': 'new_v', 'new_v_ffn_w_up': 'new_v', 'new_v_ffn_conv_w': 'new_v', 'new_v_ffn_w_down': 'new_v'}


def _forward(args):
    return _fwd_reference(*[args[k] for k in FWD_PARAMS])


def _output_shape():
    def fwd():
        inp = _fwd_setup_inputs(0)
        return _fwd_reference(*[inp[k] for k in FWD_PARAMS])
    out = _jax.eval_shape(fwd)
    return out.shape, out.dtype

N_MICROBATCH = 1
ADAM_LR = 0.001
ADAM_B1 = 0.9
ADAM_B2 = 0.999
ADAM_EPS = 1e-08
ADAM_WD = 0.01
ADAM_STEP = 10
PER_EXAMPLE_BATCH_AXIS = {'x': 0, 'loss_target': 0}
SHARED_INPUTS = []
_WEIGHT_DTYPES = {'meta_tokens': _jnp.float32, 'norm_mix_g': _jnp.float32, 'norm_ffn_g': _jnp.float32, 'final_norm_g': _jnp.float32, 'sc_w_in': _jnp.float32, 'sc_conv_w': _jnp.float32, 'sc_w_out': _jnp.float32, 'rg_w_in': _jnp.float32, 'rg_conv_w': _jnp.float32, 'rg_conv_b': _jnp.float32, 'rg_w_gate_a': _jnp.float32, 'rg_b_gate_a': _jnp.float32, 'rg_w_gate_x': _jnp.float32, 'rg_b_gate_x': _jnp.float32, 'rg_lambda': _jnp.float32, 'rg_w_out': _jnp.float32, 'ffn_w_up': _jnp.float32, 'ffn_conv_w': _jnp.float32, 'ffn_w_down': _jnp.float32}
MOMENT_SCALE = {'meta_tokens': 1.204723e-02, 'norm_mix_g': 4.349932e-01, 'norm_ffn_g': 2.009557e-01, 'final_norm_g': 1.278660e+02, 'sc_w_in': 2.957523e-01, 'sc_conv_w': 2.979141e-01, 'sc_w_out': 2.972296e-01, 'rg_w_in': 9.451037e-02, 'rg_conv_w': 9.736550e-02, 'rg_conv_b': 1.307992e+00, 'rg_w_gate_a': 3.345986e-02, 'rg_b_gate_a': 2.729978e-02, 'rg_w_gate_x': 6.055280e-02, 'rg_b_gate_x': 2.904728e-02, 'rg_lambda': 5.176102e-02, 'rg_w_out': 1.099865e-01, 'ffn_w_up': 8.587234e-02, 'ffn_conv_w': 8.688155e-02, 'ffn_w_down': 1.400662e-01}


def _to_microbatches(a, axis):
    t = _jnp.moveaxis(a, axis, 0)
    t = t.reshape((N_MICROBATCH, t.shape[0] // N_MICROBATCH) + t.shape[1:])
    return _jnp.moveaxis(t, 1, axis + 1)


def setup_inputs(seed: int = 0) -> dict:
    inp = _fwd_setup_inputs(seed)
    key = _jax.random.fold_in(_jax.random.key(seed), 7919)
    shape, _ = _output_shape()
    out = dict(inp)
    out["loss_target"] = _jax.random.normal(_jax.random.fold_in(key, 0), shape, _jnp.float32)
    for i, name in enumerate(TWIN_WEIGHTS):
        w = inp[name].astype(_jnp.float32)
        if MOMENT_SCALE is None:
            s = _jnp.sqrt(_jnp.mean(_jnp.square(w)) + 1e-30)
        else:
            s = MOMENT_SCALE[name]
        km, kv = _jax.random.split(_jax.random.fold_in(key, i + 1))
        out[name] = w
        out["m_" + name] = s * _jax.random.normal(km, w.shape, _jnp.float32)
        out["v_" + name] = (s * s) * _jax.random.uniform(kv, w.shape, _jnp.float32, 0.5, 1.5)
    if N_MICROBATCH > 1:
        for name, axis in PER_EXAMPLE_BATCH_AXIS.items():
            out[name] = _to_microbatches(out[name], axis)
    return {'x': out['x'], 'meta_tokens': out['meta_tokens'], 'norm_mix_g': out['norm_mix_g'], 'norm_ffn_g': out['norm_ffn_g'], 'final_norm_g': out['final_norm_g'], 'sc_w_in': out['sc_w_in'], 'sc_conv_w': out['sc_conv_w'], 'sc_w_out': out['sc_w_out'], 'rg_w_in': out['rg_w_in'], 'rg_conv_w': out['rg_conv_w'], 'rg_conv_b': out['rg_conv_b'], 'rg_w_gate_a': out['rg_w_gate_a'], 'rg_b_gate_a': out['rg_b_gate_a'], 'rg_w_gate_x': out['rg_w_gate_x'], 'rg_b_gate_x': out['rg_b_gate_x'], 'rg_lambda': out['rg_lambda'], 'rg_w_out': out['rg_w_out'], 'ffn_w_up': out['ffn_w_up'], 'ffn_conv_w': out['ffn_conv_w'], 'ffn_w_down': out['ffn_w_down'], 'loss_target': out['loss_target'], 'm_meta_tokens': out['m_meta_tokens'], 'm_norm_mix_g': out['m_norm_mix_g'], 'm_norm_ffn_g': out['m_norm_ffn_g'], 'm_final_norm_g': out['m_final_norm_g'], 'm_sc_w_in': out['m_sc_w_in'], 'm_sc_conv_w': out['m_sc_conv_w'], 'm_sc_w_out': out['m_sc_w_out'], 'm_rg_w_in': out['m_rg_w_in'], 'm_rg_conv_w': out['m_rg_conv_w'], 'm_rg_conv_b': out['m_rg_conv_b'], 'm_rg_w_gate_a': out['m_rg_w_gate_a'], 'm_rg_b_gate_a': out['m_rg_b_gate_a'], 'm_rg_w_gate_x': out['m_rg_w_gate_x'], 'm_rg_b_gate_x': out['m_rg_b_gate_x'], 'm_rg_lambda': out['m_rg_lambda'], 'm_rg_w_out': out['m_rg_w_out'], 'm_ffn_w_up': out['m_ffn_w_up'], 'm_ffn_conv_w': out['m_ffn_conv_w'], 'm_ffn_w_down': out['m_ffn_w_down'], 'v_meta_tokens': out['v_meta_tokens'], 'v_norm_mix_g': out['v_norm_mix_g'], 'v_norm_ffn_g': out['v_norm_ffn_g'], 'v_final_norm_g': out['v_final_norm_g'], 'v_sc_w_in': out['v_sc_w_in'], 'v_sc_conv_w': out['v_sc_conv_w'], 'v_sc_w_out': out['v_sc_w_out'], 'v_rg_w_in': out['v_rg_w_in'], 'v_rg_conv_w': out['v_rg_conv_w'], 'v_rg_conv_b': out['v_rg_conv_b'], 'v_rg_w_gate_a': out['v_rg_w_gate_a'], 'v_rg_b_gate_a': out['v_rg_b_gate_a'], 'v_rg_w_gate_x': out['v_rg_w_gate_x'], 'v_rg_b_gate_x': out['v_rg_b_gate_x'], 'v_rg_lambda': out['v_rg_lambda'], 'v_rg_w_out': out['v_rg_w_out'], 'v_ffn_w_up': out['v_ffn_w_up'], 'v_ffn_conv_w': out['v_ffn_conv_w'], 'v_ffn_w_down': out['v_ffn_w_down']}


def _loss(weights, diff, rest, loss_target):
    with _jax.named_scope("forward"):
        args = {**rest, TWIN_DIFF_INPUT: diff, **{k: w.astype(_WEIGHT_DTYPES[k]) for k, w in weights.items()}}
        y = _forward(args)
    with _jax.named_scope("loss_head"):
        err = _jnp.square(y.astype(_jnp.float32) - loss_target)
        return 0.5 * _jnp.sum(_jnp.mean(err, axis=-1)) if err.ndim else 0.5 * err


def _adamw(w, g, m, v):
    m = ADAM_B1 * m + (1.0 - ADAM_B1) * g
    v = ADAM_B2 * v + (1.0 - ADAM_B2) * _jnp.square(g)
    m_hat = m / (1.0 - ADAM_B1 ** ADAM_STEP)
    v_hat = v / (1.0 - ADAM_B2 ** ADAM_STEP)
    delta = -ADAM_LR * (m_hat / (_jnp.sqrt(v_hat) + ADAM_EPS) + ADAM_WD * w)
    return delta, m, v


def reference(x, meta_tokens, norm_mix_g, norm_ffn_g, final_norm_g, sc_w_in, sc_conv_w, sc_w_out, rg_w_in, rg_conv_w, rg_conv_b, rg_w_gate_a, rg_b_gate_a, rg_w_gate_x, rg_b_gate_x, rg_lambda, rg_w_out, ffn_w_up, ffn_conv_w, ffn_w_down, loss_target, m_meta_tokens, m_norm_mix_g, m_norm_ffn_g, m_final_norm_g, m_sc_w_in, m_sc_conv_w, m_sc_w_out, m_rg_w_in, m_rg_conv_w, m_rg_conv_b, m_rg_w_gate_a, m_rg_b_gate_a, m_rg_w_gate_x, m_rg_b_gate_x, m_rg_lambda, m_rg_w_out, m_ffn_w_up, m_ffn_conv_w, m_ffn_w_down, v_meta_tokens, v_norm_mix_g, v_norm_ffn_g, v_final_norm_g, v_sc_w_in, v_sc_conv_w, v_sc_w_out, v_rg_w_in, v_rg_conv_w, v_rg_conv_b, v_rg_w_gate_a, v_rg_b_gate_a, v_rg_w_gate_x, v_rg_b_gate_x, v_rg_lambda, v_rg_w_out, v_ffn_w_up, v_ffn_conv_w, v_ffn_w_down):
    given = dict(x=x, meta_tokens=meta_tokens, norm_mix_g=norm_mix_g, norm_ffn_g=norm_ffn_g, final_norm_g=final_norm_g, sc_w_in=sc_w_in, sc_conv_w=sc_conv_w, sc_w_out=sc_w_out, rg_w_in=rg_w_in, rg_conv_w=rg_conv_w, rg_conv_b=rg_conv_b, rg_w_gate_a=rg_w_gate_a, rg_b_gate_a=rg_b_gate_a, rg_w_gate_x=rg_w_gate_x, rg_b_gate_x=rg_b_gate_x, rg_lambda=rg_lambda, rg_w_out=rg_w_out, ffn_w_up=ffn_w_up, ffn_conv_w=ffn_conv_w, ffn_w_down=ffn_w_down, loss_target=loss_target, m_meta_tokens=m_meta_tokens, m_norm_mix_g=m_norm_mix_g, m_norm_ffn_g=m_norm_ffn_g, m_final_norm_g=m_final_norm_g, m_sc_w_in=m_sc_w_in, m_sc_conv_w=m_sc_conv_w, m_sc_w_out=m_sc_w_out, m_rg_w_in=m_rg_w_in, m_rg_conv_w=m_rg_conv_w, m_rg_conv_b=m_rg_conv_b, m_rg_w_gate_a=m_rg_w_gate_a, m_rg_b_gate_a=m_rg_b_gate_a, m_rg_w_gate_x=m_rg_w_gate_x, m_rg_b_gate_x=m_rg_b_gate_x, m_rg_lambda=m_rg_lambda, m_rg_w_out=m_rg_w_out, m_ffn_w_up=m_ffn_w_up, m_ffn_conv_w=m_ffn_conv_w, m_ffn_w_down=m_ffn_w_down, v_meta_tokens=v_meta_tokens, v_norm_mix_g=v_norm_mix_g, v_norm_ffn_g=v_norm_ffn_g, v_final_norm_g=v_final_norm_g, v_sc_w_in=v_sc_w_in, v_sc_conv_w=v_sc_conv_w, v_sc_w_out=v_sc_w_out, v_rg_w_in=v_rg_w_in, v_rg_conv_w=v_rg_conv_w, v_rg_conv_b=v_rg_conv_b, v_rg_w_gate_a=v_rg_w_gate_a, v_rg_b_gate_a=v_rg_b_gate_a, v_rg_w_gate_x=v_rg_w_gate_x, v_rg_b_gate_x=v_rg_b_gate_x, v_rg_lambda=v_rg_lambda, v_rg_w_out=v_rg_w_out, v_ffn_w_up=v_ffn_w_up, v_ffn_conv_w=v_ffn_conv_w, v_ffn_w_down=v_ffn_w_down)
    weights = {n: given[n] for n in TWIN_WEIGHTS}
    shared = {n: given[n] for n in SHARED_INPUTS}
    per_example = {n: given[n] for n in ['x']}
    grad_fn = _jax.value_and_grad(_loss, argnums=(0, 1))

    def one_microbatch(ex, loss_target):
        ex = dict(ex)
        diff = ex.pop(TWIN_DIFF_INPUT)
        return grad_fn(weights, diff, {**shared, **ex}, loss_target)

    if N_MICROBATCH == 1:
        loss, (grad_w, grad_x) = one_microbatch(per_example, given["loss_target"])
    else:
        def body(carry, xs):
            loss_sum, grad_sum = carry
            l_k, (gw_k, gx_k) = one_microbatch(xs[0], xs[1])
            with _jax.named_scope("update"):
                return (loss_sum + l_k, _jax.tree.map(_jnp.add, grad_sum, gw_k)), gx_k

        init = (_jnp.zeros((), _jnp.float32), _jax.tree.map(_jnp.zeros_like, weights))
        (loss, grad_w), grad_x = _jax.lax.scan(body, init, (per_example, given["loss_target"]))
    with _jax.named_scope("update"):
        delta_w, new_m, new_v = {}, {}, {}
        for n in TWIN_WEIGHTS:
            delta_w[n], new_m[n], new_v[n] = _adamw(weights[n], grad_w[n], given["m_" + n], given["v_" + n])
    return (loss, grad_x, *[grad_w[n] for n in TWIN_WEIGHTS], *[delta_w[n] for n in TWIN_WEIGHTS],
            *[new_m[n] for n in TWIN_WEIGHTS], *[new_v[n] for n in TWIN_WEIGHTS])
```

```python
import functools

import jax
import jax.numpy as jnp
from jax import lax
from jax.experimental import pallas as pl
from jax.experimental.pallas import tpu as pltpu

F32 = jnp.float32
MXU_DT = jnp.bfloat16
ACT_DT = jnp.bfloat16
WIRE_DT = jnp.bfloat16
MESH_ID = pl.DeviceIdType.MESH

N_META = 16
RMS_EPS = 1e-6
RG_C = 8.0
ADAM_LR, ADAM_B1, ADAM_B2, ADAM_EPS, ADAM_WD, ADAM_STEP = 0.001, 0.9, 0.999, 1e-08, 0.01, 10
N_CHIPS = 4
VMEM_LIMIT = 60 * 1024 * 1024
F32_ROWS = 8
ACT_ROWS = 16
LANES = 128


def _row_tile(seq):
    for tm in (256, 128, 64, 32, 16):
        if seq % tm == 0:
            return tm
    raise ValueError(f"sequence length {seq} is not a multiple of 16")


def _params(n_axes=1, **kw):
    return pltpu.CompilerParams(dimension_semantics=("arbitrary",) * n_axes, vmem_limit_bytes=VMEM_LIMIT, **kw)


def _const(shape):
    return pl.BlockSpec(shape, lambda *_: (0,) * len(shape), pipeline_mode=pl.Buffered(1))


def _dot(a, b):
    return jnp.dot(a, b, preferred_element_type=F32)


def _dot_nt(a, b):
    return lax.dot_general(a, b, (((1,), (1,)), ((), ())), preferred_element_type=F32)


def _dot_tn(a, b):
    return lax.dot_general(a, b, (((0,), (0,)), ((), ())), preferred_element_type=F32)


def _sigmoid(x):
    return 1.0 / (1.0 + jnp.exp(-x))


def _rms(h, g):
    rstd = lax.rsqrt(jnp.mean(h * h, axis=-1, keepdims=True) + RMS_EPS)
    xhat = h * rstd
    return xhat * g, xhat, rstd


def _rms_bwd(dhn, xhat, rstd, g):
    dx = dhn * g
    return rstd * (dx - xhat * jnp.mean(dx * xhat, axis=-1, keepdims=True))


def _gelu(x):
    k = 0.7978845608028654
    t = jnp.tanh(k * (x + 0.044715 * x * x * x))
    return 0.5 * x * (1.0 + t), t


def _gelu_grad(x, t):
    k = 0.7978845608028654
    return 0.5 * (1.0 + t) + 0.5 * x * (1.0 - t * t) * k * (1.0 + 3 * 0.044715 * x * x)


def _softplus(x):
    e = jnp.exp(-jnp.abs(x))
    return jnp.maximum(x, 0.0) + jnp.where(e < 1e-4, e - 0.5 * e * e, jnp.log(1.0 + e))


def _expm1_neg(z):
    series = z * (1.0 + z * (0.5 + z * (1.0 / 6 + z * (1.0 / 24 + z * (1.0 / 120)))))
    return jnp.where(z > -0.1, series, jnp.exp(z) - 1.0)


def _valid_rows(tile, tm):
    row = lax.broadcasted_iota(jnp.int32, (tm, 1), 0) + tile * tm
    return row >= tm - N_META


def _conv_taps(buf, cols, width, tm, first):
    return [buf[pl.ds(first + k, tm), cols] for k in range(width)]


def _sc_fwd(x, first, g, w_in, cw, w_out, *, tm):
    seq, d = x.shape
    nt = seq // tm + 1
    nq, _, n = w_in.shape
    width = cw.shape[0]

    def body(x_ref, first_ref, g_ref, win_ref, cw_ref, wout_ref, h1_ref, hh_ref, hh_scr, cbuf):
        i = pl.program_id(0)

        @pl.when(i == 0)
        def _():
            cbuf[pl.ds(0, F32_ROWS), :] = jnp.zeros((F32_ROWS, d), F32)

        h = jnp.where(i == 0, first_ref[...], x_ref[...])
        hn = _rms(h, g_ref[...])[0].astype(MXU_DT)
        for q in range(nq):
            hh_scr[:, q * n:(q + 1) * n] = _dot(hn, win_ref[q])
        hh_ref[...] = hh_scr[...].astype(hh_ref.dtype)
        b = hh_scr[:, 0:d]
        cbuf[pl.ds(F32_ROWS, tm), :] = hh_scr[:, d:2 * d] * hh_scr[:, 2 * d:3 * d]
        taps = _conv_taps(cbuf, slice(None), width, tm, F32_ROWS - width + 1)
        u = sum(cw_ref[k:k + 1, :] * taps[k] for k in range(width))
        cbuf[pl.ds(0, F32_ROWS), :] = cbuf[pl.ds(tm, F32_ROWS), :]
        h1_ref[...] = h + _dot((b * u).astype(MXU_DT), wout_ref[...])

    return pl.pallas_call(
        body, name="sc_fwd", grid=(nt,),
        in_specs=[pl.BlockSpec((tm, d), lambda i: (jnp.maximum(i - 1, 0), 0)), _const((tm, d)), _const((1, d)),
                  _const(w_in.shape), _const(cw.shape), _const(w_out.shape)],
        out_specs=[pl.BlockSpec((tm, d), lambda i: (i, 0)), pl.BlockSpec((tm, nq * n), lambda i: (i, 0))],
        out_shape=[jax.ShapeDtypeStruct((nt * tm, d), F32), jax.ShapeDtypeStruct((nt * tm, nq * n), ACT_DT)],
        scratch_shapes=[pltpu.VMEM((tm, nq * n), F32), pltpu.VMEM((F32_ROWS + tm, d), F32)],
        compiler_params=_params(),
    )(x, first, g, w_in, cw, w_out)


def _sc_bwd(dh, hh, cw, w_out, *, tm):
    t_len, d = dh.shape
    nt = t_len // tm
    width = cw.shape[0]
    halo = tm // ACT_ROWS

    def body(dh_ref, hh_ref, hhp_ref, cw_ref, wout_ref, dhh_ref, z_ref, dcw_ref, cbuf, dbuf):
        i = pl.program_id(0)
        r = nt - 1 - i

        @pl.when(i == 0)
        def _():
            dbuf[pl.ds(tm, F32_ROWS), :] = jnp.zeros((F32_ROWS, d), F32)
            dcw_ref[...] = jnp.zeros_like(dcw_ref)

        b = hh_ref[:, 0:d].astype(F32)
        c = hh_ref[:, d:2 * d].astype(F32)
        v = hh_ref[:, 2 * d:3 * d].astype(F32)
        prev = hhp_ref[...].astype(F32)[ACT_ROWS - F32_ROWS:, :]
        cbuf[pl.ds(0, F32_ROWS), :] = jnp.where(r > 0, prev[:, d:2 * d] * prev[:, 2 * d:3 * d], 0.0)
        cbuf[pl.ds(F32_ROWS, tm), :] = c * v
        taps = _conv_taps(cbuf, slice(None), width, tm, F32_ROWS - width + 1)
        u = sum(cw_ref[k:k + 1, :] * taps[k] for k in range(width))
        z_ref[...] = (b * u).astype(z_ref.dtype)
        dz = _dot_nt(dh_ref[...].astype(MXU_DT), wout_ref[...])
        du = dz * b
        for k in range(width):
            dcw_ref[k:k + 1, :] += jnp.sum(taps[k] * du, axis=0, keepdims=True)
        dbuf[pl.ds(0, tm), :] = du
        dcv = sum(cw_ref[k:k + 1, :] * dbuf[pl.ds(width - 1 - k, tm), :] for k in range(width))
        dbuf[pl.ds(tm, F32_ROWS), :] = dbuf[pl.ds(0, F32_ROWS), :]
        dhh_ref[:, 0:d] = (dz * u).astype(dhh_ref.dtype)
        dhh_ref[:, d:2 * d] = (dcv * v).astype(dhh_ref.dtype)
        dhh_ref[:, 2 * d:3 * d] = (dcv * c).astype(dhh_ref.dtype)

    rev = lambda i: (nt - 1 - i, 0)
    return pl.pallas_call(
        body, name="sc_bwd", grid=(nt,),
        in_specs=[pl.BlockSpec((tm, d), rev), pl.BlockSpec((tm, 3 * d), rev),
                  pl.BlockSpec((ACT_ROWS, 3 * d), lambda i: (jnp.maximum((nt - 1 - i) * halo - 1, 0), 0)),
                  _const(cw.shape), _const(w_out.shape)],
        out_specs=[pl.BlockSpec((tm, 3 * d), rev), pl.BlockSpec((tm, d), rev), _const((F32_ROWS, d))],
        out_shape=[jax.ShapeDtypeStruct((t_len, 3 * d), ACT_DT), jax.ShapeDtypeStruct((t_len, d), ACT_DT),
                   jax.ShapeDtypeStruct((F32_ROWS, d), F32)],
        scratch_shapes=[pltpu.VMEM((F32_ROWS + tm, d), F32), pltpu.VMEM((tm + F32_ROWS, d), F32)],
        compiler_params=_params(),
    )(dh, hh, hh, cw, w_out)


def _ffn_fwd(h, g, w_up, cw, w_down, *, tm):
    t_len, d = h.shape
    nt = t_len // tm
    nq, _, n = w_up.shape
    width = cw.shape[0]

    def body(h_ref, g_ref, wup_ref, cw_ref, wdn_ref, out_ref, hu_ref, ubuf):
        i = pl.program_id(0)

        @pl.when(i == 0)
        def _():
            ubuf[pl.ds(0, F32_ROWS), :] = jnp.zeros((F32_ROWS, nq * n), F32)

        h_in = h_ref[...]
        hn = _rms(h_in, g_ref[...])[0].astype(MXU_DT)
        for q in range(nq):
            ubuf[pl.ds(F32_ROWS, tm), q * n:(q + 1) * n] = _dot(hn, wup_ref[q])
        hu_ref[...] = ubuf[pl.ds(F32_ROWS, tm), :].astype(hu_ref.dtype)
        acc = h_in
        for j in range(nq // 2):
            gcol, vcol = slice(j * n, (j + 1) * n), slice((nq // 2 + j) * n, (nq // 2 + j + 1) * n)
            conv = lambda cols: sum(cw_ref[k:k + 1, cols] * tap for k, tap in
                                    enumerate(_conv_taps(ubuf, cols, width, tm, F32_ROWS - width + 1)))
            gj, vj = conv(gcol), conv(vcol)
            acc = acc + _dot((gj * _sigmoid(gj) * vj).astype(MXU_DT), wdn_ref[j * n:(j + 1) * n, :])
        ubuf[pl.ds(0, F32_ROWS), :] = ubuf[pl.ds(tm, F32_ROWS), :]
        out_ref[...] = acc

    row = lambda i: (i, 0)
    return pl.pallas_call(
        body, name="ffn_fwd", grid=(nt,),
        in_specs=[pl.BlockSpec((tm, d), row), _const((1, d)), _const(w_up.shape), _const(cw.shape), _const(w_down.shape)],
        out_specs=[pl.BlockSpec((tm, d), row), pl.BlockSpec((tm, nq * n), row)],
        out_shape=[jax.ShapeDtypeStruct((t_len, d), F32), jax.ShapeDtypeStruct((t_len, nq * n), ACT_DT)],
        scratch_shapes=[pltpu.VMEM((F32_ROWS + tm, nq * n), F32)],
        compiler_params=_params(),
    )(h, g, w_up, cw, w_down)


def _ffn_bwd(dh, hu, cw, w_down, *, tm):
    t_len, d = dh.shape
    nt = t_len // tm
    ff = hu.shape[1]
    n = ff // 4
    width = cw.shape[0]
    halo = tm // ACT_ROWS

    def body(dh_ref, hu_ref, hup_ref, cw_ref, wdn_ref, a_ref, dhu_ref, dcw_ref, ubuf, dbuf):
        i = pl.program_id(0)
        r = nt - 1 - i

        @pl.when(i == 0)
        def _():
            dbuf[pl.ds(tm, F32_ROWS), :] = jnp.zeros((F32_ROWS, ff), F32)
            dcw_ref[...] = jnp.zeros_like(dcw_ref)

        prev = hup_ref[...].astype(F32)[ACT_ROWS - F32_ROWS:, :]
        ubuf[pl.ds(0, F32_ROWS), :] = jnp.where(r > 0, prev, 0.0)
        ubuf[pl.ds(F32_ROWS, tm), :] = hu_ref[...].astype(F32)
        dhb = dh_ref[...].astype(MXU_DT)
        conv = lambda cols: sum(cw_ref[k:k + 1, cols] * tap for k, tap in
                                enumerate(_conv_taps(ubuf, cols, width, tm, F32_ROWS - width + 1)))
        for j in range(2):
            gcol, vcol = slice(j * n, (j + 1) * n), slice((2 + j) * n, (3 + j) * n)
            gj, vj = conv(gcol), conv(vcol)
            sg = _sigmoid(gj)
            s = gj * sg
            a_ref[:, gcol] = (s * vj).astype(a_ref.dtype)
            da = _dot_nt(dhb, wdn_ref[j * n:(j + 1) * n, :])
            dbuf[pl.ds(0, tm), vcol] = da * s
            dbuf[pl.ds(0, tm), gcol] = da * vj * (sg * (1.0 + gj * (1.0 - sg)))
        for q in range(4):
            cols = slice(q * n, (q + 1) * n)
            dy = dbuf[pl.ds(0, tm), cols]
            for k, tap in enumerate(_conv_taps(ubuf, cols, width, tm, F32_ROWS - width + 1)):
                dcw_ref[k:k + 1, cols] += jnp.sum(tap * dy, axis=0, keepdims=True)
            dhu = sum(cw_ref[k:k + 1, cols] * dbuf[pl.ds(width - 1 - k, tm), cols] for k in range(width))
            dhu_ref[:, cols] = dhu.astype(dhu_ref.dtype)
        dbuf[pl.ds(tm, F32_ROWS), :] = dbuf[pl.ds(0, F32_ROWS), :]

    rev = lambda i: (nt - 1 - i, 0)
    return pl.pallas_call(
        body, name="ffn_bwd", grid=(nt,),
        in_specs=[pl.BlockSpec((tm, d), rev), pl.BlockSpec((tm, ff), rev),
                  pl.BlockSpec((ACT_ROWS, ff), lambda i: (jnp.maximum((nt - 1 - i) * halo - 1, 0), 0)),
                  _const(cw.shape), _const(w_down.shape)],
        out_specs=[pl.BlockSpec((tm, 2 * n), rev), pl.BlockSpec((tm, ff), rev), _const((F32_ROWS, ff))],
        out_shape=[jax.ShapeDtypeStruct((t_len, 2 * n), ACT_DT), jax.ShapeDtypeStruct((t_len, ff), ACT_DT),
                   jax.ShapeDtypeStruct((F32_ROWS, ff), F32)],
        scratch_shapes=[pltpu.VMEM((F32_ROWS + tm, ff), F32), pltpu.VMEM((tm + F32_ROWS, ff), F32)],
        compiler_params=_params(),
    )(dh, hu, hu, cw, w_down)


V_CONV_B, V_B_A, V_B_X, V_LAMBDA = 0, 1, 2, 3
G_CONV_W, G_CONV_B, G_B_A, G_B_X, G_LAMBDA = 0, 4, 5, 6, 7


def _scan_fwd(a, b, tm):
    row = lax.broadcasted_iota(jnp.int32, (tm, 1), 0)
    s = 1
    while s < tm:
        keep = row >= s
        b = jnp.where(keep, b + a * pltpu.roll(b, s, 0), b)
        a = jnp.where(keep, a * pltpu.roll(a, s, 0), a)
        s *= 2
    return b, a


def _scan_bwd(a, b, tm):
    row = lax.broadcasted_iota(jnp.int32, (tm, 1), 0)
    s = 1
    while s < tm:
        keep = row < tm - s
        b = jnp.where(keep, b + a * pltpu.roll(b, tm - s, 0), b)
        a = jnp.where(keep, a * pltpu.roll(a, tm - s, 0), a)
        s *= 2
    return b, a


def _rg_gates(u, vec_ref, wa_ref, wx_ref, pre_scr, nb, bd):
    ub = u.astype(MXU_DT)
    for k in range(nb):
        blk = slice(k * bd, (k + 1) * bd)
        pre_scr[0, :, blk] = _dot(ub[:, blk], wa_ref[k])
        pre_scr[1, :, blk] = _dot(ub[:, blk], wx_ref[k])
    r_gate = _sigmoid(pre_scr[0] + vec_ref[V_B_A:V_B_A + 1, :])
    i_gate = _sigmoid(pre_scr[1] + vec_ref[V_B_X:V_B_X + 1, :])
    sp = _softplus(-vec_ref[V_LAMBDA:V_LAMBDA + 1, :])
    log_a = -RG_C * r_gate * sp
    a = jnp.exp(log_a)
    mult = jnp.sqrt(-_expm1_neg(2.0 * log_a))
    return r_gate, i_gate, a, mult, sp, ub


def _rg_fwd(h, g, w_in, cw, vec, wa, wx, w_out, *, tm):
    t_len, d = h.shape
    nt = t_len // tm
    nq, _, n = w_in.shape
    dr = 2 * n
    width = cw.shape[0]
    nb, bd, _ = wa.shape

    def body(h_ref, g_ref, win_ref, cw_ref, vec_ref, wa_ref, wx_ref, wout_ref, out_ref, hh_ref, hs_ref,
             gbuf, rbuf, pre_scr, carry):
        i = pl.program_id(0)

        @pl.when(i == 0)
        def _():
            rbuf[pl.ds(0, F32_ROWS), :] = jnp.zeros((F32_ROWS, dr), F32)
            carry[...] = jnp.zeros_like(carry)

        h_in = h_ref[...]
        hn = _rms(h_in, g_ref[...])[0].astype(MXU_DT)
        for q in range(2):
            gbuf[:, q * n:(q + 1) * n] = _dot(hn, win_ref[q])
            rbuf[pl.ds(F32_ROWS, tm), q * n:(q + 1) * n] = _dot(hn, win_ref[2 + q])
        hh_ref[:, 0:dr] = gbuf[...].astype(hh_ref.dtype)
        hh_ref[:, dr:2 * dr] = rbuf[pl.ds(F32_ROWS, tm), :].astype(hh_ref.dtype)
        taps = _conv_taps(rbuf, slice(None), width, tm, F32_ROWS - width + 1)
        u = sum(cw_ref[k:k + 1, :] * taps[k] for k in range(width)) + vec_ref[V_CONV_B:V_CONV_B + 1, :]
        rbuf[pl.ds(0, F32_ROWS), :] = rbuf[pl.ds(tm, F32_ROWS), :]
        _, i_gate, a, mult, _, _ = _rg_gates(u, vec_ref, wa_ref, wx_ref, pre_scr, nb, bd)
        b = jnp.where(_valid_rows(i, tm), mult * (i_gate * u), 0.0)
        hs, cum = _scan_fwd(a, b, tm)
        hs = hs + cum * carry[0:1, :]
        carry[0:1, :] = hs[tm - 1:tm, :]
        hs_ref[...] = hs
        y = hs * _gelu(gbuf[...])[0]
        out_ref[...] = h_in + _dot(y.astype(MXU_DT), wout_ref[...])

    row = lambda i: (i, 0)
    return pl.pallas_call(
        body, name="rg_fwd", grid=(nt,),
        in_specs=[pl.BlockSpec((tm, d), row), _const((1, d)), _const(w_in.shape), _const(cw.shape), _const(vec.shape),
                  _const(wa.shape), _const(wx.shape), _const(w_out.shape)],
        out_specs=[pl.BlockSpec((tm, d), row), pl.BlockSpec((tm, 2 * dr), row), pl.BlockSpec((tm, dr), row)],
        out_shape=[jax.ShapeDtypeStruct((t_len, d), F32), jax.ShapeDtypeStruct((t_len, 2 * dr), ACT_DT),
                   jax.ShapeDtypeStruct((t_len, dr), F32)],
        scratch_shapes=[pltpu.VMEM((tm, dr), F32), pltpu.VMEM((F32_ROWS + tm, dr), F32), pltpu.VMEM((2, tm, dr), F32),
                        pltpu.VMEM((F32_ROWS, dr), F32)],
        compiler_params=_params(),
    )(h, g, w_in, cw, vec, wa, wx, w_out)


def _rg_bwd(dh, hh, hs, cw, vec, wa, wx, w_out, *, tm):
    t_len, d = dh.shape
    nt = t_len // tm
    dr = hs.shape[1]
    width = cw.shape[0]
    nb, bd, _ = wa.shape
    halo_act, halo_f32 = tm // ACT_ROWS, tm // F32_ROWS

    def body(dh_ref, hh_ref, hhp_ref, hs_ref, hsp_ref, cw_ref, vec_ref, wa_ref, wx_ref, wout_ref,
             dhh_ref, y_ref, dvec_ref, dwa_ref, dwx_ref, rbuf, dbuf, pre_scr, carry):
        i = pl.program_id(0)
        r = nt - 1 - i

        @pl.when(i == 0)
        def _():
            dbuf[pl.ds(tm, F32_ROWS), :] = jnp.zeros((F32_ROWS, dr), F32)
            carry[...] = jnp.zeros_like(carry)
            dvec_ref[...] = jnp.zeros_like(dvec_ref)
            dwa_ref[...] = jnp.zeros_like(dwa_ref)
            dwx_ref[...] = jnp.zeros_like(dwx_ref)

        row = lax.broadcasted_iota(jnp.int32, (tm, 1), 0)
        gb = hh_ref[:, 0:dr].astype(F32)
        prev = hhp_ref[...].astype(F32)[ACT_ROWS - F32_ROWS:, dr:2 * dr]
        rbuf[pl.ds(0, F32_ROWS), :] = jnp.where(r > 0, prev, 0.0)
        rbuf[pl.ds(F32_ROWS, tm), :] = hh_ref[:, dr:2 * dr].astype(F32)
        taps = _conv_taps(rbuf, slice(None), width, tm, F32_ROWS - width + 1)
        u = sum(cw_ref[k:k + 1, :] * taps[k] for k in range(width)) + vec_ref[V_CONV_B:V_CONV_B + 1, :]
        r_gate, i_gate, a, mult, sp, ub = _rg_gates(u, vec_ref, wa_ref, wx_ref, pre_scr, nb, bd)
        hs_t = hs_ref[...]
        h_last = jnp.where(r > 0, hsp_ref[F32_ROWS - 1:F32_ROWS, :], 0.0)
        h_prev = jnp.where(row == 0, h_last, pltpu.roll(hs_t, 1, 0))
        gate, th = _gelu(gb)
        y_ref[...] = (hs_t * gate).astype(y_ref.dtype)
        dy = _dot_nt(dh_ref[...].astype(MXU_DT), wout_ref[...])
        a_next = jnp.where(row == tm - 1, carry[0:1, :], pltpu.roll(a, tm - 1, 0))
        d_hs, cum = _scan_bwd(a_next, dy * gate, tm)
        d_hs = d_hs + cum * carry[1:2, :]
        carry[0:1, :] = a[0:1, :]
        carry[1:2, :] = d_hs[0:1, :]
        d_b = jnp.where(_valid_rows(r, tm), d_hs, 0.0)
        d_iu = d_b * mult
        d_log_a = d_hs * h_prev * a - d_b * (i_gate * u) * (a * a) / jnp.maximum(mult, 1e-30)
        dvec_ref[G_LAMBDA:G_LAMBDA + 1, :] += jnp.sum(d_log_a * r_gate, axis=0, keepdims=True) * (-RG_C)
        d_pre_r = d_log_a * (-RG_C * sp) * r_gate * (1.0 - r_gate)
        d_pre_i = d_iu * u * i_gate * (1.0 - i_gate)
        dvec_ref[G_B_A:G_B_A + 1, :] += jnp.sum(d_pre_r, axis=0, keepdims=True)
        dvec_ref[G_B_X:G_B_X + 1, :] += jnp.sum(d_pre_i, axis=0, keepdims=True)
        dbuf[pl.ds(0, tm), :] = d_iu * i_gate
        d_pre_r = d_pre_r.astype(MXU_DT)
        d_pre_i = d_pre_i.astype(MXU_DT)
        for k in range(nb):
            blk = slice(k * bd, (k + 1) * bd)
            dwa_ref[k] += _dot_tn(ub[:, blk], d_pre_r[:, blk])
            dwx_ref[k] += _dot_tn(ub[:, blk], d_pre_i[:, blk])
            dbuf[pl.ds(0, tm), blk] += _dot_nt(d_pre_r[:, blk], wa_ref[k]) + _dot_nt(d_pre_i[:, blk], wx_ref[k])
        du = dbuf[pl.ds(0, tm), :]
        dvec_ref[G_CONV_B:G_CONV_B + 1, :] += jnp.sum(du, axis=0, keepdims=True)
        for k in range(width):
            dvec_ref[G_CONV_W + k:G_CONV_W + k + 1, :] += jnp.sum(taps[k] * du, axis=0, keepdims=True)
        d_rb = sum(cw_ref[k:k + 1, :] * dbuf[pl.ds(width - 1 - k, tm), :] for k in range(width))
        dbuf[pl.ds(tm, F32_ROWS), :] = dbuf[pl.ds(0, F32_ROWS), :]
        dhh_ref[:, 0:dr] = (dy * hs_t * _gelu_grad(gb, th)).astype(dhh_ref.dtype)
        dhh_ref[:, dr:2 * dr] = d_rb.astype(dhh_ref.dtype)

        @pl.when(i == nt - 1)
        def _():
            lam = vec_ref[V_LAMBDA:V_LAMBDA + 1, :]
            dvec_ref[G_LAMBDA:G_LAMBDA + 1, :] = dvec_ref[G_LAMBDA:G_LAMBDA + 1, :] * (-_sigmoid(-lam))

    rev = lambda i: (nt - 1 - i, 0)
    return pl.pallas_call(
        body, name="rg_bwd", grid=(nt,),
        in_specs=[pl.BlockSpec((tm, d), rev), pl.BlockSpec((tm, 2 * dr), rev),
                  pl.BlockSpec((ACT_ROWS, 2 * dr), lambda i: (jnp.maximum((nt - 1 - i) * halo_act - 1, 0), 0)),
                  pl.BlockSpec((tm, dr), rev),
                  pl.BlockSpec((F32_ROWS, dr), lambda i: (jnp.maximum((nt - 1 - i) * halo_f32 - 1, 0), 0)),
                  _const(cw.shape), _const(vec.shape), _const(wa.shape), _const(wx.shape), _const(w_out.shape)],
        out_specs=[pl.BlockSpec((tm, 2 * dr), rev), pl.BlockSpec((tm, dr), rev), _const((F32_ROWS, dr)),
                   _const(wa.shape), _const(wx.shape)],
        out_shape=[jax.ShapeDtypeStruct((t_len, 2 * dr), ACT_DT), jax.ShapeDtypeStruct((t_len, dr), ACT_DT),
                   jax.ShapeDtypeStruct((F32_ROWS, dr), F32), jax.ShapeDtypeStruct(wa.shape, F32),
                   jax.ShapeDtypeStruct(wx.shape, F32)],
        scratch_shapes=[pltpu.VMEM((F32_ROWS + tm, dr), F32), pltpu.VMEM((tm + F32_ROWS, dr), F32),
                        pltpu.VMEM((2, tm, dr), F32), pltpu.VMEM((F32_ROWS, dr), F32)],
        compiler_params=_params(),
    )(dh, hh, hh, hs, hs, cw, vec, wa, wx, w_out)


def _inproj_bwd(dpre, h, dh, w, g, *, tm, first=None):
    t_len, d = dh.shape
    nt = t_len // tm
    nq, _, n = w.shape

    def body(dpre_ref, h_ref, *rest):
        first_ref = rest[0] if first is not None else None
        dh_ref, w_ref, g_ref, out_ref, hn_ref, dg_ref = rest[first is not None:]
        i = pl.program_id(0)

        @pl.when(i == 0)
        def _():
            dg_ref[...] = jnp.zeros_like(dg_ref)

        h_in = h_ref[...] if first is None else jnp.where(i == 0, first_ref[...], h_ref[...])
        gain = g_ref[...]
        hn, xhat, rstd = _rms(h_in, gain)
        hn_ref[...] = hn.astype(hn_ref.dtype)
        dhn = sum(_dot_nt(dpre_ref[:, q * n:(q + 1) * n], w_ref[q]) for q in range(nq))
        dg_ref[0:1, :] += jnp.sum(dhn * xhat, axis=0, keepdims=True)
        out_ref[...] = jnp.where(_valid_rows(i, tm), dh_ref[...] + _rms_bwd(dhn, xhat, rstd, gain), 0.0)

    row = lambda i: (i, 0)
    h_spec = pl.BlockSpec((tm, d), row if first is None else (lambda i: (jnp.maximum(i - 1, 0), 0)))
    operands = [dpre, h] + ([first] if first is not None else []) + [dh, w, g]
    return pl.pallas_call(
        body, name="inproj_bwd", grid=(nt,),
        in_specs=[pl.BlockSpec((tm, nq * n), row), h_spec] + ([_const((tm, d))] if first is not None else [])
        + [pl.BlockSpec((tm, d), row), _const(w.shape), _const((1, d))],
        out_specs=[pl.BlockSpec((tm, d), row), pl.BlockSpec((tm, d), row), _const((F32_ROWS, d))],
        out_shape=[jax.ShapeDtypeStruct((t_len, d), F32), jax.ShapeDtypeStruct((t_len, d), ACT_DT),
                   jax.ShapeDtypeStruct((F32_ROWS, d), F32)],
        compiler_params=_params(),
    )(*operands)


def _weight_grad(a, b, nb, *, rows):
    t_len, k_dim = a.shape
    n = b.shape[1] // nb
    nt = t_len // rows

    def body(a_ref, b_ref, out_ref):
        @pl.when(pl.program_id(1) == 0)
        def _():
            out_ref[...] = jnp.zeros_like(out_ref)

        out_ref[0] += _dot_tn(a_ref[...].astype(MXU_DT), b_ref[...].astype(MXU_DT))

    return pl.pallas_call(
        body, name="weight_grad", grid=(nb, nt),
        in_specs=[pl.BlockSpec((rows, k_dim), lambda j, i: (i, 0)), pl.BlockSpec((rows, n), lambda j, i: (i, j))],
        out_specs=pl.BlockSpec((1, k_dim, n), lambda j, i: (j, 0, 0)),
        out_shape=jax.ShapeDtypeStruct((nb, k_dim, n), F32),
        compiler_params=_params(2),
    )(a, b)


def _loss_head(h, target, g, *, tm):
    t_len, d = h.shape
    nt = t_len // tm

    def body(h_ref, t_ref, g_ref, dh_ref, sq_ref, dg_ref):
        i = pl.program_id(0)

        @pl.when(i == 0)
        def _():
            sq_ref[...] = jnp.zeros_like(sq_ref)
            dg_ref[...] = jnp.zeros_like(dg_ref)
            dh_ref[...] = jnp.zeros_like(dh_ref)

        @pl.when(i > 0)
        def _():
            gain = g_ref[...]
            out, xhat, rstd = _rms(h_ref[...], gain)
            err = out - t_ref[...]
            sq_ref[0:1, :] += jnp.sum(err * err, axis=0, keepdims=True)
            dout = err * (1.0 / d)
            dg_ref[0:1, :] += jnp.sum(dout * xhat, axis=0, keepdims=True)
            dh_ref[...] = _rms_bwd(dout, xhat, rstd, gain)

    row = lambda i: (i, 0)
    return pl.pallas_call(
        body, name="loss_head", grid=(nt,),
        in_specs=[pl.BlockSpec((tm, d), row), pl.BlockSpec((tm, d), lambda i: (jnp.maximum(i - 1, 0), 0)), _const((1, d))],
        out_specs=[pl.BlockSpec((tm, d), row), _const((F32_ROWS, d)), _const((F32_ROWS, d))],
        out_shape=[jax.ShapeDtypeStruct((t_len, d), F32), jax.ShapeDtypeStruct((F32_ROWS, d), F32),
                   jax.ShapeDtypeStruct((F32_ROWS, d), F32)],
        compiler_params=_params(),
    )(h, target, g)


def _adamw(w, m, v, parts, *, rows):
    n_rows, n_cols = w.shape
    nt = n_rows // rows

    def body(w_ref, m_ref, v_ref, *rest):
        part_refs, (g_ref, d_ref, nm_ref, nv_ref) = rest[:len(parts)], rest[len(parts):]
        grad = part_refs[0][...].astype(F32)
        for p in part_refs[1:]:
            grad = grad + p[...].astype(F32)
        new_m = ADAM_B1 * m_ref[...] + (1.0 - ADAM_B1) * grad
        new_v = ADAM_B2 * v_ref[...] + (1.0 - ADAM_B2) * (grad * grad)
        m_hat = new_m / (1.0 - ADAM_B1 ** ADAM_STEP)
        v_hat = new_v / (1.0 - ADAM_B2 ** ADAM_STEP)
        g_ref[...] = grad
        d_ref[...] = -ADAM_LR * (m_hat / (jnp.sqrt(v_hat) + ADAM_EPS) + ADAM_WD * w_ref[...])
        nm_ref[...] = new_m
        nv_ref[...] = new_v

    spec = pl.BlockSpec((rows, n_cols), lambda i: (i, 0))
    return pl.pallas_call(
        body, name="adamw", grid=(nt,),
        in_specs=[spec] * (3 + len(parts)), out_specs=[spec] * 4,
        out_shape=[jax.ShapeDtypeStruct(w.shape, F32)] * 4,
        compiler_params=_params(),
    )(w, m, v, *parts)


def _sum_parts(own, recv, *, rows):
    n_rows, n_cols = own.shape
    n_recv = recv.shape[0]

    def body(own_ref, recv_ref, out_ref):
        acc = own_ref[...].astype(F32)
        for j in range(n_recv):
            acc = acc + recv_ref[j].astype(F32)
        out_ref[...] = acc

    return pl.pallas_call(
        body, name="sum_parts", grid=(n_rows // rows,),
        in_specs=[pl.BlockSpec((rows, n_cols), lambda i: (i, 0)), pl.BlockSpec((n_recv, rows, n_cols), lambda i: (0, i, 0))],
        out_specs=pl.BlockSpec((rows, n_cols), lambda i: (i, 0)),
        out_shape=jax.ShapeDtypeStruct(own.shape, F32),
        compiler_params=_params(),
    )(own, recv)


ANY = pl.BlockSpec(memory_space=pl.ANY)


def _place():
    return lax.axis_index("x"), lax.axis_index("y"), lax.axis_index("c")


def _other_chips(x, y):
    return [(1 - x, y), (x, 1 - y), (1 - x, 1 - y)]


def _gather_chips(shards):
    nk = len(shards)

    def body(*refs):
        ins, outs, (send_sems, recv_sems, local_sems) = refs[:nk], refs[nk:2 * nk], refs[2 * nk:]
        x, y, c = _place()
        mine = 2 * x + y
        local = [pltpu.make_async_copy(ins[k], outs[k].at[mine], local_sems.at[k]) for k in range(nk)]
        for cp in local:
            cp.start()
        sends = []
        for k in range(nk):
            for j, (px, py) in enumerate(_other_chips(x, y)):
                sends.append(pltpu.make_async_remote_copy(
                    src_ref=ins[k], dst_ref=outs[k].at[mine], send_sem=send_sems.at[k, j], recv_sem=recv_sems.at[k, j],
                    device_id=(px, py, c), device_id_type=MESH_ID))
                sends[-1].start()
        for k in range(nk):
            for j, (px, py) in enumerate(_other_chips(x, y)):
                pltpu.make_async_remote_copy(
                    src_ref=ins[k], dst_ref=outs[k].at[2 * px + py], send_sem=send_sems.at[k, j],
                    recv_sem=recv_sems.at[k, j], device_id=(px, py, c), device_id_type=MESH_ID).wait_recv()
        for cp in sends:
            cp.wait_send()
        for cp in local:
            cp.wait()

    return pl.pallas_call(
        body, name="gather_chips", in_specs=[ANY] * nk, out_specs=[ANY] * nk,
        out_shape=[jax.ShapeDtypeStruct((N_CHIPS,) + s.shape, s.dtype) for s in shards],
        scratch_shapes=[pltpu.SemaphoreType.DMA((nk, 3)), pltpu.SemaphoreType.DMA((nk, 3)), pltpu.SemaphoreType.DMA((nk,))],
    )(*shards)


def _scatter_chips(parts):
    nk = len(parts)

    def body(*refs):
        ins, outs, (send_sems, recv_sems) = refs[:nk], refs[nk:2 * nk], refs[2 * nk:]
        x, y, c = _place()
        sends = []
        for k in range(nk):
            for j, (px, py) in enumerate(_other_chips(x, y)):
                sends.append(pltpu.make_async_remote_copy(
                    src_ref=ins[k].at[2 * px + py], dst_ref=outs[k].at[j], send_sem=send_sems.at[k, j],
                    recv_sem=recv_sems.at[k, j], device_id=(px, py, c), device_id_type=MESH_ID))
                sends[-1].start()
        for cp in sends:
            cp.wait_recv()
        for cp in sends:
            cp.wait_send()

    return pl.pallas_call(
        body, name="scatter_chips", in_specs=[ANY] * nk, out_specs=[ANY] * nk,
        out_shape=[jax.ShapeDtypeStruct((3,) + p.shape[1:], p.dtype) for p in parts],
        scratch_shapes=[pltpu.SemaphoreType.DMA((nk, 3)), pltpu.SemaphoreType.DMA((nk, 3))],
    )(*parts)


def _swap_cores(arrays):
    nk = len(arrays)

    def body(*refs):
        ins, outs, (send_sems, recv_sems) = refs[:nk], refs[nk:2 * nk], refs[2 * nk:]
        x, y, c = _place()
        sends = [pltpu.make_async_remote_copy(
            src_ref=ins[k], dst_ref=outs[k], send_sem=send_sems.at[k], recv_sem=recv_sems.at[k],
            device_id=(x, y, 1 - c), device_id_type=MESH_ID) for k in range(nk)]
        for cp in sends:
            cp.start()
        for cp in sends:
            cp.wait_recv()
        for cp in sends:
            cp.wait_send()

    return pl.pallas_call(
        body, name="swap_cores", in_specs=[ANY] * nk, out_specs=[ANY] * nk,
        out_shape=[jax.ShapeDtypeStruct(a.shape, a.dtype) for a in arrays],
        scratch_shapes=[pltpu.SemaphoreType.DMA((nk,)), pltpu.SemaphoreType.DMA((nk,))],
    )(*arrays)


def _gather_devices(v):
    def body(v_ref, out_ref, send_sems, recv_sems, local_sem):
        x, y, c = _place()
        local = pltpu.make_async_copy(v_ref, out_ref.at[4 * x + 2 * y + c], local_sem)
        local.start()
        sends = []
        for flip in range(1, 8):
            px, py, pc = x ^ (flip >> 2), y ^ ((flip >> 1) & 1), c ^ (flip & 1)
            sends.append(pltpu.make_async_remote_copy(
                src_ref=v_ref, dst_ref=out_ref.at[4 * x + 2 * y + c], send_sem=send_sems.at[flip - 1],
                recv_sem=recv_sems.at[flip - 1], device_id=(px, py, pc), device_id_type=MESH_ID))
            sends[-1].start()
        for flip in range(1, 8):
            px, py, pc = x ^ (flip >> 2), y ^ ((flip >> 1) & 1), c ^ (flip & 1)
            pltpu.make_async_remote_copy(
                src_ref=v_ref, dst_ref=out_ref.at[4 * px + 2 * py + pc], send_sem=send_sems.at[flip - 1],
                recv_sem=recv_sems.at[flip - 1], device_id=(px, py, pc), device_id_type=MESH_ID).wait_recv()
        for cp in sends:
            cp.wait_send()
        local.wait()

    return pl.pallas_call(
        body, name="gather_devices", in_specs=[ANY], out_specs=ANY,
        out_shape=jax.ShapeDtypeStruct((8,) + v.shape, v.dtype),
        scratch_shapes=[pltpu.SemaphoreType.DMA((7,)), pltpu.SemaphoreType.DMA((7,)), pltpu.SemaphoreType.DMA],
    )(v)


def _pack(arrays, pad_rows=F32_ROWS):
    flat = jnp.concatenate([a.reshape(-1).astype(F32) for a in arrays])
    rows = -(-flat.shape[0] // (LANES * pad_rows)) * pad_rows
    return jnp.pad(flat, (0, rows * LANES - flat.shape[0])).reshape(rows, LANES)


def _unpack(packed, shapes):
    flat, out, off = packed.reshape(-1), [], 0
    for s in shapes:
        size = 1
        for dim in s:
            size *= dim
        out.append(flat[off:off + size].reshape(s))
        off += size
    return out


def _divisor_rows(n_rows, most=256):
    best = None
    for r in range(ACT_ROWS, most + 1, ACT_ROWS):
        if n_rows % r == 0:
            best = r
    return best or n_rows


def kernel(x, meta_tokens, norm_mix_g, norm_ffn_g, final_norm_g, sc_w_in, sc_conv_w, sc_w_out, rg_w_in, rg_conv_w, rg_conv_b, rg_w_gate_a, rg_b_gate_a, rg_w_gate_x, rg_b_gate_x, rg_lambda, rg_w_out, ffn_w_up, ffn_conv_w, ffn_w_down, loss_target, m_meta_tokens, m_norm_mix_g, m_norm_ffn_g, m_final_norm_g, m_sc_w_in, m_sc_conv_w, m_sc_w_out, m_rg_w_in, m_rg_conv_w, m_rg_conv_b, m_rg_w_gate_a, m_rg_b_gate_a, m_rg_w_gate_x, m_rg_b_gate_x, m_rg_lambda, m_rg_w_out, m_ffn_w_up, m_ffn_conv_w, m_ffn_w_down, v_meta_tokens, v_norm_mix_g, v_norm_ffn_g, v_final_norm_g, v_sc_w_in, v_sc_conv_w, v_sc_w_out, v_rg_w_in, v_rg_conv_w, v_rg_conv_b, v_rg_w_gate_a, v_rg_b_gate_a, v_rg_w_gate_x, v_rg_b_gate_x, v_rg_lambda, v_rg_w_out, v_ffn_w_up, v_ffn_conv_w, v_ffn_w_down):
    weights = dict(meta_tokens=meta_tokens, norm_mix_g=norm_mix_g, norm_ffn_g=norm_ffn_g, final_norm_g=final_norm_g, sc_w_in=sc_w_in, sc_conv_w=sc_conv_w, sc_w_out=sc_w_out, rg_w_in=rg_w_in, rg_conv_w=rg_conv_w, rg_conv_b=rg_conv_b, rg_w_gate_a=rg_w_gate_a, rg_b_gate_a=rg_b_gate_a, rg_w_gate_x=rg_w_gate_x, rg_b_gate_x=rg_b_gate_x, rg_lambda=rg_lambda, rg_w_out=rg_w_out, ffn_w_up=ffn_w_up, ffn_conv_w=ffn_conv_w, ffn_w_down=ffn_w_down)
    m_in = dict(meta_tokens=m_meta_tokens, norm_mix_g=m_norm_mix_g, norm_ffn_g=m_norm_ffn_g, final_norm_g=m_final_norm_g, sc_w_in=m_sc_w_in, sc_conv_w=m_sc_conv_w, sc_w_out=m_sc_w_out, rg_w_in=m_rg_w_in, rg_conv_w=m_rg_conv_w, rg_conv_b=m_rg_conv_b, rg_w_gate_a=m_rg_w_gate_a, rg_b_gate_a=m_rg_b_gate_a, rg_w_gate_x=m_rg_w_gate_x, rg_b_gate_x=m_rg_b_gate_x, rg_lambda=m_rg_lambda, rg_w_out=m_rg_w_out, ffn_w_up=m_ffn_w_up, ffn_conv_w=m_ffn_conv_w, ffn_w_down=m_ffn_w_down)
    v_in = dict(meta_tokens=v_meta_tokens, norm_mix_g=v_norm_mix_g, norm_ffn_g=v_norm_ffn_g, final_norm_g=v_final_norm_g, sc_w_in=v_sc_w_in, sc_conv_w=v_sc_conv_w, sc_w_out=v_sc_w_out, rg_w_in=v_rg_w_in, rg_conv_w=v_rg_conv_w, rg_conv_b=v_rg_conv_b, rg_w_gate_a=v_rg_w_gate_a, rg_b_gate_a=v_rg_b_gate_a, rg_w_gate_x=v_rg_w_gate_x, rg_b_gate_x=v_rg_b_gate_x, rg_lambda=v_rg_lambda, rg_w_out=v_rg_w_out, ffn_w_up=v_ffn_w_up, ffn_conv_w=v_ffn_conv_w, ffn_w_down=v_ffn_w_down)
    names = list(weights)

    tokens, target = x[0], loss_target[0]
    seq, d = tokens.shape
    tm = _row_tile(seq)
    t_len = seq + tm
    wg_rows = 5 * tm if t_len % (5 * tm) == 0 else tm
    xi, yi, _ = _place()
    chip = 2 * xi + yi
    mesh_axes = ("x", "y", "c")

    big_names = ["sc_w_in", "sc_w_out", "rg_w_in", "rg_w_out", "ffn_w_up", "ffn_w_down"]
    big_shards = {"sc_w_in": [sc_w_in[0]], "sc_w_out": [sc_w_out[0]], "rg_w_in": [rg_w_in[0]], "rg_w_out": [rg_w_out[0]],
                  "ffn_w_up": [ffn_w_up[0], ffn_w_up[1]], "ffn_w_down": [ffn_w_down[0], ffn_w_down[1]]}
    small_sharded = ["meta_tokens", "sc_conv_w", "rg_conv_w", "rg_conv_b", "rg_b_gate_a", "rg_b_gate_x", "rg_lambda", "ffn_conv_w"]
    small_2d = {n: weights[n].reshape(-1, weights[n].shape[-1]) for n in small_sharded}
    flat_shards = [s for n in big_names for s in big_shards[n]]
    gathered = _gather_chips([s.astype(WIRE_DT) for s in flat_shards] + [_pack([small_2d[n] for n in small_sharded])])
    w_sc_in, w_sc_out, w_rg_in, w_rg_out, w_up0, w_up1, w_dn0, w_dn1 = gathered[:-1]
    w_sc_out = w_sc_out.reshape(-1, d)
    w_rg_out = w_rg_out.reshape(-1, d)
    w_up, w_dn = [w_up0, w_up1], [w_dn0.reshape(-1, d), w_dn1.reshape(-1, d)]
    small_len = sum(a.size for a in small_2d.values())
    by_chip = gathered[-1].reshape(N_CHIPS, -1)[:, :small_len]
    full, off = {}, 0
    for n in small_sharded:
        rows, width = small_2d[n].shape
        full[n] = by_chip[:, off:off + rows * width].reshape(N_CHIPS, rows, width).transpose(1, 0, 2).reshape(rows, N_CHIPS * width)
        off += rows * width
    sc_cw, rg_cw = full["sc_conv_w"], full["rg_conv_w"]
    ffn_cw = [full["ffn_conv_w"][0:3], full["ffn_conv_w"][3:6]]
    d_rnn = rg_cw.shape[1]
    vec = jnp.concatenate([full["rg_conv_b"], full["rg_b_gate_a"], full["rg_b_gate_x"], full["rg_lambda"],
                           jnp.zeros((F32_ROWS - 4, d_rnn), F32)])
    wa, wx = rg_w_gate_a[0].astype(MXU_DT), rg_w_gate_x[0].astype(MXU_DT)
    first = jnp.concatenate([jnp.zeros((tm - N_META, d), F32), full["meta_tokens"]])
    g_mix = [norm_mix_g[0:1], norm_mix_g[1:2]]
    g_ffn = [norm_ffn_g[0:1], norm_ffn_g[1:2]]

    h1, hh0 = _sc_fwd(tokens, first, g_mix[0], w_sc_in, sc_cw, w_sc_out, tm=tm)
    h2, hu0 = _ffn_fwd(h1, g_ffn[0], w_up[0], ffn_cw[0], w_dn[0], tm=tm)
    h3, hh1, hs = _rg_fwd(h2, g_mix[1], w_rg_in, rg_cw, vec, wa, wx, w_rg_out, tm=tm)
    h4, hu1 = _ffn_fwd(h3, g_ffn[1], w_up[1], ffn_cw[1], w_dn[1], tm=tm)
    dh4, sq, d_final = _loss_head(h4, target, final_norm_g.reshape(1, d), tm=tm)
    loss = lax.psum(jnp.sum(sq[0]) * (0.5 / d), mesh_axes)

    def ffn_backward(dh_out, h_in, hu, layer):
        act, dhu, dcw = _ffn_bwd(dh_out, hu, ffn_cw[layer], w_dn[layer], tm=tm)
        dh_in, hn, dg = _inproj_bwd(dhu, h_in, dh_out, w_up[layer], g_ffn[layer], tm=tm)
        d_up = _weight_grad(hn, dhu, N_CHIPS, rows=wg_rows)
        d_dn = _weight_grad(act, dh_out, 1, rows=wg_rows)
        return dh_in, d_up, d_dn.reshape(N_CHIPS, -1, d), dcw[0:3], dg[0]

    dh3, d_up1, d_dn1, d_fcw1, d_gf1 = ffn_backward(dh4, h3, hu1, 1)
    dhh1, y_rg, d_vec, d_wa, d_wx = _rg_bwd(dh3, hh1, hs, rg_cw, vec, wa, wx, w_rg_out, tm=tm)
    dh2, hn_rg, d_gm1 = _inproj_bwd(dhh1, h2, dh3, w_rg_in, g_mix[1], tm=tm)
    d_rg_in = _weight_grad(hn_rg, dhh1, N_CHIPS, rows=wg_rows)
    d_rg_out = _weight_grad(y_rg, dh3, 1, rows=wg_rows).reshape(N_CHIPS, -1, d)
    dh1, d_up0, d_dn0, d_fcw0, d_gf0 = ffn_backward(dh2, h1, hu0, 0)
    dhh0, z_sc, d_sccw = _sc_bwd(dh1, hh0, sc_cw, w_sc_out, tm=tm)
    dh0, hn_sc, d_gm0 = _inproj_bwd(dhh0, tokens, dh1, w_sc_in, g_mix[0], tm=tm, first=first)
    d_sc_in = _weight_grad(hn_sc, dhh0, N_CHIPS, rows=wg_rows)
    d_sc_out = _weight_grad(z_sc, dh1, 1, rows=wg_rows).reshape(N_CHIPS, -1, d)
    grad_x = dh0[tm:][None]

    partial = [d_sc_in, d_sc_out, d_rg_in, d_rg_out, d_up0, d_up1, d_dn0, d_dn1]
    received = _scatter_chips([p.astype(WIRE_DT) for p in partial])
    core_sum = []
    for p, r in zip(partial, received):
        own = lax.dynamic_index_in_dim(p, chip, 0, keepdims=False)
        core_sum.append(_sum_parts(own, r, rows=_divisor_rows(own.shape[0])))
    other_sum = _swap_cores(core_sum)
    out = {k: {} for k in ("grad", "delta", "m", "v")}
    per_shard = {}
    for j, (mine, theirs) in enumerate(zip(core_sum, other_sum)):
        n, layer = [("sc_w_in", 0), ("sc_w_out", 0), ("rg_w_in", 0), ("rg_w_out", 0), ("ffn_w_up", 0), ("ffn_w_up", 1),
                    ("ffn_w_down", 0), ("ffn_w_down", 1)][j]
        res = _adamw(weights[n][layer], m_in[n][layer], v_in[n][layer], [mine, theirs], rows=_divisor_rows(mine.shape[0]))
        per_shard.setdefault(n, []).append(res)
    for n, res in per_shard.items():
        for k, key in enumerate(("grad", "delta", "m", "v")):
            out[key][n] = jnp.stack([r[k] for r in res])

    d_meta = dh0[tm - N_META:tm]
    small_full = {"meta_tokens": d_meta, "sc_conv_w": d_sccw[0:3], "rg_conv_w": d_vec[G_CONV_W:G_CONV_W + 4],
                  "rg_conv_b": d_vec[G_CONV_B:G_CONV_B + 1], "rg_b_gate_a": d_vec[G_B_A:G_B_A + 1],
                  "rg_b_gate_x": d_vec[G_B_X:G_B_X + 1], "rg_lambda": d_vec[G_LAMBDA:G_LAMBDA + 1],
                  "ffn_conv_w": jnp.concatenate([d_fcw0, d_fcw1])}
    replicated = {"norm_mix_g": jnp.stack([d_gm0[0], d_gm1[0]]), "norm_ffn_g": jnp.stack([d_gf0, d_gf1]),
                  "final_norm_g": d_final[0], "rg_w_gate_a": d_wa[None], "rg_w_gate_x": d_wx[None]}
    small_names = small_sharded + list(replicated)
    partial_small = [small_full[n] for n in small_sharded] + [replicated[n] for n in replicated]
    packed = _pack(partial_small)
    by_device = _gather_devices(packed)
    total = _sum_parts(by_device[0], by_device[1:], rows=packed.shape[0])
    summed = dict(zip(small_names, _unpack(total, [p.shape for p in partial_small])))
    grads = {}
    for n in small_sharded:
        width = small_2d[n].shape[1]
        grads[n] = lax.dynamic_slice_in_dim(summed[n], chip * width, width, axis=1).reshape(weights[n].shape)
    for n in replicated:
        grads[n] = summed[n].reshape(weights[n].shape)
    shapes = [weights[n].shape for n in small_names]
    res = _adamw(_pack([weights[n] for n in small_names]), _pack([m_in[n] for n in small_names]),
                 _pack([v_in[n] for n in small_names]), [_pack([grads[n] for n in small_names])],
                 rows=_pack([weights[n] for n in small_names]).shape[0])
    for k, key in enumerate(("grad", "delta", "m", "v")):
        out[key].update(dict(zip(small_names, _unpack(res[k], shapes))))

    return (loss, grad_x, *[out["grad"][n] for n in names], *[out["delta"][n] for n in names],
            *[out["m"][n] for n in names], *[out["v"][n] for n in names])
```

```python
import functools

import jax
import jax.numpy as jnp
from jax import lax
from jax.experimental import pallas as pl
from jax.experimental.pallas import tpu as pltpu

F32 = jnp.float32
MXU_DT = jnp.bfloat16
ACT_DT = jnp.bfloat16
WIRE_DT = jnp.bfloat16
MESH_ID = pl.DeviceIdType.MESH

N_META = 16
RMS_EPS = 1e-6
RG_C = 8.0
ADAM_LR, ADAM_B1, ADAM_B2, ADAM_EPS, ADAM_WD, ADAM_STEP = 0.001, 0.9, 0.999, 1e-08, 0.01, 10
N_CHIPS = 4
VMEM_LIMIT = 60 * 1024 * 1024
F32_ROWS = 8
ACT_ROWS = 16
LANES = 128


def _row_tile(seq):
    for tm in (256, 128, 64, 32, 16):
        if seq % tm == 0:
            return tm
    raise ValueError(f"sequence length {seq} is not a multiple of 16")


def _params(n_axes=1, **kw):
    return pltpu.CompilerParams(dimension_semantics=("arbitrary",) * n_axes, vmem_limit_bytes=VMEM_LIMIT, **kw)


def _const(shape):
    return pl.BlockSpec(shape, lambda *_: (0,) * len(shape), pipeline_mode=pl.Buffered(1))


def _dot(a, b):
    return jnp.dot(a, b, preferred_element_type=F32)


def _dot_nt(a, b):
    return lax.dot_general(a, b, (((1,), (1,)), ((), ())), preferred_element_type=F32)


def _dot_tn(a, b):
    return lax.dot_general(a, b, (((0,), (0,)), ((), ())), preferred_element_type=F32)


def _sigmoid(x):
    return 1.0 / (1.0 + jnp.exp(-x))


def _rms(h, g):
    rstd = lax.rsqrt(jnp.mean(h * h, axis=-1, keepdims=True) + RMS_EPS)
    xhat = h * rstd
    return xhat * g, xhat, rstd


def _rms_bwd(dhn, xhat, rstd, g):
    dx = dhn * g
    return rstd * (dx - xhat * jnp.mean(dx * xhat, axis=-1, keepdims=True))


def _gelu(x):
    k = 0.7978845608028654
    t = jnp.tanh(k * (x + 0.044715 * x * x * x))
    return 0.5 * x * (1.0 + t), t


def _gelu_grad(x, t):
    k = 0.7978845608028654
    return 0.5 * (1.0 + t) + 0.5 * x * (1.0 - t * t) * k * (1.0 + 3 * 0.044715 * x * x)


def _softplus(x):
    e = jnp.exp(-jnp.abs(x))
    return jnp.maximum(x, 0.0) + jnp.where(e < 1e-4, e - 0.5 * e * e, jnp.log(1.0 + e))


def _expm1_neg(z):
    series = z * (1.0 + z * (0.5 + z * (1.0 / 6 + z * (1.0 / 24 + z * (1.0 / 120)))))
    return jnp.where(z > -0.1, series, jnp.exp(z) - 1.0)


def _valid_rows(tile, tm):
    row = lax.broadcasted_iota(jnp.int32, (tm, 1), 0) + tile * tm
    return row >= tm - N_META


def _conv_taps(buf, cols, width, tm, first):
    return [buf[pl.ds(first + k, tm), cols] for k in range(width)]


ANY = pl.BlockSpec(memory_space=pl.ANY)


def _place():
    return lax.axis_index("x"), lax.axis_index("y"), lax.axis_index("c")


def _other_chips(x, y):
    return [(1 - x, y), (x, 1 - y), (1 - x, 1 - y)]


class _Gather:
    def __init__(self, shards):
        nk = len(shards)
        self.arrays = list(shards)
        self.out_shape = [jax.ShapeDtypeStruct((N_CHIPS,) + s.shape, s.dtype) for s in shards]
        self.scratch = [pltpu.SemaphoreType.DMA((nk, 3)), pltpu.SemaphoreType.DMA((nk, 3)), pltpu.SemaphoreType.DMA((nk,))]

    def run(self, ins, outs, sems, start):
        send_sems, recv_sems, local_sems = sems
        x, y, c = _place()
        mine = 2 * x + y
        for k in range(len(ins)):
            local = pltpu.make_async_copy(ins[k], outs[k].at[mine], local_sems.at[k])
            local.start() if start else local.wait()
            for j, (px, py) in enumerate(_other_chips(x, y)):
                sems_kj = dict(send_sem=send_sems.at[k, j], recv_sem=recv_sems.at[k, j], device_id=(px, py, c),
                               device_id_type=MESH_ID)
                send = pltpu.make_async_remote_copy(src_ref=ins[k], dst_ref=outs[k].at[mine], **sems_kj)
                if start:
                    send.start()
                else:
                    pltpu.make_async_remote_copy(src_ref=ins[k], dst_ref=outs[k].at[2 * px + py], **sems_kj).wait_recv()
                    send.wait_send()


class _Scatter:
    def __init__(self, parts):
        nk = len(parts)
        self.arrays = list(parts)
        self.out_shape = [jax.ShapeDtypeStruct((3,) + p.shape[1:], p.dtype) for p in parts]
        self.scratch = [pltpu.SemaphoreType.DMA((nk, 3)), pltpu.SemaphoreType.DMA((nk, 3))]

    def run(self, ins, outs, sems, start):
        send_sems, recv_sems = sems
        x, y, c = _place()
        for k in range(len(ins)):
            for j, (px, py) in enumerate(_other_chips(x, y)):
                send = pltpu.make_async_remote_copy(
                    src_ref=ins[k].at[2 * px + py], dst_ref=outs[k].at[j], send_sem=send_sems.at[k, j],
                    recv_sem=recv_sems.at[k, j], device_id=(px, py, c), device_id_type=MESH_ID)
                if start:
                    send.start()
                else:
                    send.wait_recv()
                    send.wait_send()


def _exchange(ride, name):
    n_in, n_out = len(ride.arrays), len(ride.out_shape)

    def body(*refs):
        ride.run(refs[:n_in], refs[n_in:n_in + n_out], refs[n_in + n_out:], start=True)
        ride.run(refs[:n_in], refs[n_in:n_in + n_out], refs[n_in + n_out:], start=False)

    return pl.pallas_call(body, name=name, in_specs=[ANY] * n_in, out_specs=[ANY] * n_out, out_shape=ride.out_shape,
                          scratch_shapes=ride.scratch)(*ride.arrays)


def _launch(body, operands, *, name, grid, in_specs, out_specs, out_shape, scratch_shapes=(), ride=None):
    common = dict(name=name, grid=grid, compiler_params=_params(len(grid)))
    if ride is None:
        return pl.pallas_call(body, in_specs=in_specs, out_specs=out_specs, out_shape=out_shape,
                              scratch_shapes=list(scratch_shapes), **common)(*operands)
    n_in, n_out, n_scr = len(operands), len(out_shape), len(scratch_shapes)
    r_in, r_out = len(ride.arrays), len(ride.out_shape)
    last = grid[0] - 1

    def riding(*refs):
        ins, refs = refs[:n_in], refs[n_in:]
        r_ins, refs = refs[:r_in], refs[r_in:]
        outs, refs = refs[:n_out], refs[n_out:]
        r_outs, refs = refs[:r_out], refs[r_out:]
        scr, r_sems = refs[:n_scr], refs[n_scr:]
        i = pl.program_id(0)

        @pl.when(i == 0)
        def _():
            ride.run(r_ins, r_outs, r_sems, start=True)

        body(*ins, *outs, *scr)

        @pl.when(i == last)
        def _():
            ride.run(r_ins, r_outs, r_sems, start=False)

    return pl.pallas_call(
        riding, in_specs=list(in_specs) + [ANY] * r_in, out_specs=list(out_specs) + [ANY] * r_out,
        out_shape=list(out_shape) + ride.out_shape, scratch_shapes=list(scratch_shapes) + ride.scratch, **common,
    )(*operands, *ride.arrays)


def _sc_fwd(x, first, g, w_in, cw, w_out, *, tm, ride=None):
    seq, d = x.shape
    nt = seq // tm + 1
    nq, _, n = w_in.shape
    width = cw.shape[0]

    def body(x_ref, first_ref, g_ref, win_ref, cw_ref, wout_ref, h1_ref, hh_ref, hh_scr, cbuf):
        i = pl.program_id(0)

        @pl.when(i == 0)
        def _():
            cbuf[pl.ds(0, F32_ROWS), :] = jnp.zeros((F32_ROWS, d), F32)

        h = jnp.where(i == 0, first_ref[...], x_ref[...])
        hn = _rms(h, g_ref[...])[0].astype(MXU_DT)
        for q in range(nq):
            hh_scr[:, q * n:(q + 1) * n] = _dot(hn, win_ref[q])
        hh_ref[...] = hh_scr[...].astype(hh_ref.dtype)
        b = hh_scr[:, 0:d]
        cbuf[pl.ds(F32_ROWS, tm), :] = hh_scr[:, d:2 * d] * hh_scr[:, 2 * d:3 * d]
        taps = _conv_taps(cbuf, slice(None), width, tm, F32_ROWS - width + 1)
        u = sum(cw_ref[k:k + 1, :] * taps[k] for k in range(width))
        cbuf[pl.ds(0, F32_ROWS), :] = cbuf[pl.ds(tm, F32_ROWS), :]
        h1_ref[...] = h + _dot((b * u).astype(MXU_DT), wout_ref[...])

    return _launch(
        body, [x, first, g, w_in, cw, w_out], name="sc_fwd", grid=(nt,),
        in_specs=[pl.BlockSpec((tm, d), lambda i: (jnp.maximum(i - 1, 0), 0)), _const((tm, d)), _const((1, d)),
                  _const(w_in.shape), _const(cw.shape), _const(w_out.shape)],
        out_specs=[pl.BlockSpec((tm, d), lambda i: (i, 0)), pl.BlockSpec((tm, nq * n), lambda i: (i, 0))],
        out_shape=[jax.ShapeDtypeStruct((nt * tm, d), F32), jax.ShapeDtypeStruct((nt * tm, nq * n), ACT_DT)],
        scratch_shapes=[pltpu.VMEM((tm, nq * n), F32), pltpu.VMEM((F32_ROWS + tm, d), F32)],
        ride=ride,
    )


def _sc_bwd(dh, hh, cw, w_out, *, tm, ride=None):
    t_len, d = dh.shape
    nt = t_len // tm
    width = cw.shape[0]
    halo = tm // ACT_ROWS

    def body(dh_ref, hh_ref, hhp_ref, cw_ref, wout_ref, dhh_ref, z_ref, dcw_ref, cbuf, dbuf):
        i = pl.program_id(0)
        r = nt - 1 - i

        @pl.when(i == 0)
        def _():
            dbuf[pl.ds(tm, F32_ROWS), :] = jnp.zeros((F32_ROWS, d), F32)
            dcw_ref[...] = jnp.zeros_like(dcw_ref)

        b = hh_ref[:, 0:d].astype(F32)
        c = hh_ref[:, d:2 * d].astype(F32)
        v = hh_ref[:, 2 * d:3 * d].astype(F32)
        prev = hhp_ref[...].astype(F32)[ACT_ROWS - F32_ROWS:, :]
        cbuf[pl.ds(0, F32_ROWS), :] = jnp.where(r > 0, prev[:, d:2 * d] * prev[:, 2 * d:3 * d], 0.0)
        cbuf[pl.ds(F32_ROWS, tm), :] = c * v
        taps = _conv_taps(cbuf, slice(None), width, tm, F32_ROWS - width + 1)
        u = sum(cw_ref[k:k + 1, :] * taps[k] for k in range(width))
        z_ref[...] = (b * u).astype(z_ref.dtype)
        dz = _dot_nt(dh_ref[...].astype(MXU_DT), wout_ref[...])
        du = dz * b
        for k in range(width):
            dcw_ref[k:k + 1, :] += jnp.sum(taps[k] * du, axis=0, keepdims=True)
        dbuf[pl.ds(0, tm), :] = du
        dcv = sum(cw_ref[k:k + 1, :] * dbuf[pl.ds(width - 1 - k, tm), :] for k in range(width))
        dbuf[pl.ds(tm, F32_ROWS), :] = dbuf[pl.ds(0, F32_ROWS), :]
        dhh_ref[:, 0:d] = (dz * u).astype(dhh_ref.dtype)
        dhh_ref[:, d:2 * d] = (dcv * v).astype(dhh_ref.dtype)
        dhh_ref[:, 2 * d:3 * d] = (dcv * c).astype(dhh_ref.dtype)

    rev = lambda i: (nt - 1 - i, 0)
    return _launch(
        body, [dh, hh, hh, cw, w_out], name="sc_bwd", grid=(nt,),
        in_specs=[pl.BlockSpec((tm, d), rev), pl.BlockSpec((tm, 3 * d), rev),
                  pl.BlockSpec((ACT_ROWS, 3 * d), lambda i: (jnp.maximum((nt - 1 - i) * halo - 1, 0), 0)),
                  _const(cw.shape), _const(w_out.shape)],
        out_specs=[pl.BlockSpec((tm, 3 * d), rev), pl.BlockSpec((tm, d), rev), _const((F32_ROWS, d))],
        out_shape=[jax.ShapeDtypeStruct((t_len, 3 * d), ACT_DT), jax.ShapeDtypeStruct((t_len, d), ACT_DT),
                   jax.ShapeDtypeStruct((F32_ROWS, d), F32)],
        scratch_shapes=[pltpu.VMEM((F32_ROWS + tm, d), F32), pltpu.VMEM((tm + F32_ROWS, d), F32)],
        ride=ride,
    )


def _ffn_fwd(h, g, w_up, cw, w_down, *, tm, ride=None):
    t_len, d = h.shape
    nt = t_len // tm
    nq, _, n = w_up.shape
    width = cw.shape[0]

    def body(h_ref, g_ref, wup_ref, cw_ref, wdn_ref, out_ref, hu_ref, ubuf):
        i = pl.program_id(0)

        @pl.when(i == 0)
        def _():
            ubuf[pl.ds(0, F32_ROWS), :] = jnp.zeros((F32_ROWS, nq * n), F32)

        h_in = h_ref[...]
        hn = _rms(h_in, g_ref[...])[0].astype(MXU_DT)
        for q in range(nq):
            ubuf[pl.ds(F32_ROWS, tm), q * n:(q + 1) * n] = _dot(hn, wup_ref[q])
        hu_ref[...] = ubuf[pl.ds(F32_ROWS, tm), :].astype(hu_ref.dtype)
        acc = h_in
        for j in range(nq // 2):
            gcol, vcol = slice(j * n, (j + 1) * n), slice((nq // 2 + j) * n, (nq // 2 + j + 1) * n)
            conv = lambda cols: sum(cw_ref[k:k + 1, cols] * tap for k, tap in
                                    enumerate(_conv_taps(ubuf, cols, width, tm, F32_ROWS - width + 1)))
            gj, vj = conv(gcol), conv(vcol)
            acc = acc + _dot((gj * _sigmoid(gj) * vj).astype(MXU_DT), wdn_ref[j * n:(j + 1) * n, :])
        ubuf[pl.ds(0, F32_ROWS), :] = ubuf[pl.ds(tm, F32_ROWS), :]
        out_ref[...] = acc

    row = lambda i: (i, 0)
    return _launch(
        body, [h, g, w_up, cw, w_down], name="ffn_fwd", grid=(nt,),
        in_specs=[pl.BlockSpec((tm, d), row), _const((1, d)), _const(w_up.shape), _const(cw.shape), _const(w_down.shape)],
        out_specs=[pl.BlockSpec((tm, d), row), pl.BlockSpec((tm, nq * n), row)],
        out_shape=[jax.ShapeDtypeStruct((t_len, d), F32), jax.ShapeDtypeStruct((t_len, nq * n), ACT_DT)],
        scratch_shapes=[pltpu.VMEM((F32_ROWS + tm, nq * n), F32)],
        ride=ride,
    )


def _ffn_bwd(dh, hu, cw, w_down, *, tm, ride=None):
    t_len, d = dh.shape
    nt = t_len // tm
    ff = hu.shape[1]
    n = ff // 4
    width = cw.shape[0]
    halo = tm // ACT_ROWS

    def body(dh_ref, hu_ref, hup_ref, cw_ref, wdn_ref, a_ref, dhu_ref, dcw_ref, ubuf, dbuf):
        i = pl.program_id(0)
        r = nt - 1 - i

        @pl.when(i == 0)
        def _():
            dbuf[pl.ds(tm, F32_ROWS), :] = jnp.zeros((F32_ROWS, ff), F32)
            dcw_ref[...] = jnp.zeros_like(dcw_ref)

        prev = hup_ref[...].astype(F32)[ACT_ROWS - F32_ROWS:, :]
        ubuf[pl.ds(0, F32_ROWS), :] = jnp.where(r > 0, prev, 0.0)
        ubuf[pl.ds(F32_ROWS, tm), :] = hu_ref[...].astype(F32)
        dhb = dh_ref[...].astype(MXU_DT)
        conv = lambda cols: sum(cw_ref[k:k + 1, cols] * tap for k, tap in
                                enumerate(_conv_taps(ubuf, cols, width, tm, F32_ROWS - width + 1)))
        for j in range(2):
            gcol, vcol = slice(j * n, (j + 1) * n), slice((2 + j) * n, (3 + j) * n)
            gj, vj = conv(gcol), conv(vcol)
            sg = _sigmoid(gj)
            s = gj * sg
            a_ref[:, gcol] = (s * vj).astype(a_ref.dtype)
            da = _dot_nt(dhb, wdn_ref[j * n:(j + 1) * n, :])
            dbuf[pl.ds(0, tm), vcol] = da * s
            dbuf[pl.ds(0, tm), gcol] = da * vj * (sg * (1.0 + gj * (1.0 - sg)))
        for q in range(4):
            cols = slice(q * n, (q + 1) * n)
            dy = dbuf[pl.ds(0, tm), cols]
            for k, tap in enumerate(_conv_taps(ubuf, cols, width, tm, F32_ROWS - width + 1)):
                dcw_ref[k:k + 1, cols] += jnp.sum(tap * dy, axis=0, keepdims=True)
            dhu = sum(cw_ref[k:k + 1, cols] * dbuf[pl.ds(width - 1 - k, tm), cols] for k in range(width))
            dhu_ref[:, cols] = dhu.astype(dhu_ref.dtype)
        dbuf[pl.ds(tm, F32_ROWS), :] = dbuf[pl.ds(0, F32_ROWS), :]

    rev = lambda i: (nt - 1 - i, 0)
    return _launch(
        body, [dh, hu, hu, cw, w_down], name="ffn_bwd", grid=(nt,),
        in_specs=[pl.BlockSpec((tm, d), rev), pl.BlockSpec((tm, ff), rev),
                  pl.BlockSpec((ACT_ROWS, ff), lambda i: (jnp.maximum((nt - 1 - i) * halo - 1, 0), 0)),
                  _const(cw.shape), _const(w_down.shape)],
        out_specs=[pl.BlockSpec((tm, 2 * n), rev), pl.BlockSpec((tm, ff), rev), _const((F32_ROWS, ff))],
        out_shape=[jax.ShapeDtypeStruct((t_len, 2 * n), ACT_DT), jax.ShapeDtypeStruct((t_len, ff), ACT_DT),
                   jax.ShapeDtypeStruct((F32_ROWS, ff), F32)],
        scratch_shapes=[pltpu.VMEM((F32_ROWS + tm, ff), F32), pltpu.VMEM((tm + F32_ROWS, ff), F32)],
        ride=ride,
    )


V_CONV_B, V_B_A, V_B_X, V_LAMBDA = 0, 1, 2, 3
G_CONV_W, G_CONV_B, G_B_A, G_B_X, G_LAMBDA = 0, 4, 5, 6, 7


def _scan_fwd(a, b, tm):
    row = lax.broadcasted_iota(jnp.int32, (tm, 1), 0)
    s = 1
    while s < tm:
        keep = row >= s
        b = jnp.where(keep, b + a * pltpu.roll(b, s, 0), b)
        a = jnp.where(keep, a * pltpu.roll(a, s, 0), a)
        s *= 2
    return b, a


def _scan_bwd(a, b, tm):
    row = lax.broadcasted_iota(jnp.int32, (tm, 1), 0)
    s = 1
    while s < tm:
        keep = row < tm - s
        b = jnp.where(keep, b + a * pltpu.roll(b, tm - s, 0), b)
        a = jnp.where(keep, a * pltpu.roll(a, tm - s, 0), a)
        s *= 2
    return b, a


def _rg_gates(u, vec_ref, wa_ref, wx_ref, pre_scr, nb, bd):
    ub = u.astype(MXU_DT)
    for k in range(nb):
        blk = slice(k * bd, (k + 1) * bd)
        pre_scr[0, :, blk] = _dot(ub[:, blk], wa_ref[k])
        pre_scr[1, :, blk] = _dot(ub[:, blk], wx_ref[k])
    r_gate = _sigmoid(pre_scr[0] + vec_ref[V_B_A:V_B_A + 1, :])
    i_gate = _sigmoid(pre_scr[1] + vec_ref[V_B_X:V_B_X + 1, :])
    sp = _softplus(-vec_ref[V_LAMBDA:V_LAMBDA + 1, :])
    log_a = -RG_C * r_gate * sp
    a = jnp.exp(log_a)
    mult = jnp.sqrt(-_expm1_neg(2.0 * log_a))
    return r_gate, i_gate, a, mult, sp, ub


def _rg_fwd(h, g, w_in, cw, vec, wa, wx, w_out, *, tm):
    t_len, d = h.shape
    nt = t_len // tm
    nq, _, n = w_in.shape
    dr = 2 * n
    width = cw.shape[0]
    nb, bd, _ = wa.shape

    def body(h_ref, g_ref, win_ref, cw_ref, vec_ref, wa_ref, wx_ref, wout_ref, out_ref, hh_ref, hs_ref,
             gbuf, rbuf, pre_scr, carry):
        i = pl.program_id(0)

        @pl.when(i == 0)
        def _():
            rbuf[pl.ds(0, F32_ROWS), :] = jnp.zeros((F32_ROWS, dr), F32)
            carry[...] = jnp.zeros_like(carry)

        h_in = h_ref[...]
        hn = _rms(h_in, g_ref[...])[0].astype(MXU_DT)
        for q in range(2):
            gbuf[:, q * n:(q + 1) * n] = _dot(hn, win_ref[q])
            rbuf[pl.ds(F32_ROWS, tm), q * n:(q + 1) * n] = _dot(hn, win_ref[2 + q])
        hh_ref[:, 0:dr] = gbuf[...].astype(hh_ref.dtype)
        hh_ref[:, dr:2 * dr] = rbuf[pl.ds(F32_ROWS, tm), :].astype(hh_ref.dtype)
        taps = _conv_taps(rbuf, slice(None), width, tm, F32_ROWS - width + 1)
        u = sum(cw_ref[k:k + 1, :] * taps[k] for k in range(width)) + vec_ref[V_CONV_B:V_CONV_B + 1, :]
        rbuf[pl.ds(0, F32_ROWS), :] = rbuf[pl.ds(tm, F32_ROWS), :]
        _, i_gate, a, mult, _, _ = _rg_gates(u, vec_ref, wa_ref, wx_ref, pre_scr, nb, bd)
        b = jnp.where(_valid_rows(i, tm), mult * (i_gate * u), 0.0)
        hs, cum = _scan_fwd(a, b, tm)
        hs = hs + cum * carry[0:1, :]
        carry[0:1, :] = hs[tm - 1:tm, :]
        hs_ref[...] = hs
        y = hs * _gelu(gbuf[...])[0]
        out_ref[...] = h_in + _dot(y.astype(MXU_DT), wout_ref[...])

    row = lambda i: (i, 0)
    return pl.pallas_call(
        body, name="rg_fwd", grid=(nt,),
        in_specs=[pl.BlockSpec((tm, d), row), _const((1, d)), _const(w_in.shape), _const(cw.shape), _const(vec.shape),
                  _const(wa.shape), _const(wx.shape), _const(w_out.shape)],
        out_specs=[pl.BlockSpec((tm, d), row), pl.BlockSpec((tm, 2 * dr), row), pl.BlockSpec((tm, dr), row)],
        out_shape=[jax.ShapeDtypeStruct((t_len, d), F32), jax.ShapeDtypeStruct((t_len, 2 * dr), ACT_DT),
                   jax.ShapeDtypeStruct((t_len, dr), F32)],
        scratch_shapes=[pltpu.VMEM((tm, dr), F32), pltpu.VMEM((F32_ROWS + tm, dr), F32), pltpu.VMEM((2, tm, dr), F32),
                        pltpu.VMEM((F32_ROWS, dr), F32)],
        compiler_params=_params(),
    )(h, g, w_in, cw, vec, wa, wx, w_out)


def _rg_bwd(dh, hh, hs, cw, vec, wa, wx, w_out, *, tm, ride=None):
    t_len, d = dh.shape
    nt = t_len // tm
    dr = hs.shape[1]
    width = cw.shape[0]
    nb, bd, _ = wa.shape
    halo_act, halo_f32 = tm // ACT_ROWS, tm // F32_ROWS

    def body(dh_ref, hh_ref, hhp_ref, hs_ref, hsp_ref, cw_ref, vec_ref, wa_ref, wx_ref, wout_ref,
             dhh_ref, y_ref, dvec_ref, dwa_ref, dwx_ref, rbuf, dbuf, pre_scr, carry):
        i = pl.program_id(0)
        r = nt - 1 - i

        @pl.when(i == 0)
        def _():
            dbuf[pl.ds(tm, F32_ROWS), :] = jnp.zeros((F32_ROWS, dr), F32)
            carry[...] = jnp.zeros_like(carry)
            dvec_ref[...] = jnp.zeros_like(dvec_ref)
            dwa_ref[...] = jnp.zeros_like(dwa_ref)
            dwx_ref[...] = jnp.zeros_like(dwx_ref)

        row = lax.broadcasted_iota(jnp.int32, (tm, 1), 0)
        gb = hh_ref[:, 0:dr].astype(F32)
        prev = hhp_ref[...].astype(F32)[ACT_ROWS - F32_ROWS:, dr:2 * dr]
        rbuf[pl.ds(0, F32_ROWS), :] = jnp.where(r > 0, prev, 0.0)
        rbuf[pl.ds(F32_ROWS, tm), :] = hh_ref[:, dr:2 * dr].astype(F32)
        taps = _conv_taps(rbuf, slice(None), width, tm, F32_ROWS - width + 1)
        u = sum(cw_ref[k:k + 1, :] * taps[k] for k in range(width)) + vec_ref[V_CONV_B:V_CONV_B + 1, :]
        r_gate, i_gate, a, mult, sp, ub = _rg_gates(u, vec_ref, wa_ref, wx_ref, pre_scr, nb, bd)
        hs_t = hs_ref[...]
        h_last = jnp.where(r > 0, hsp_ref[F32_ROWS - 1:F32_ROWS, :], 0.0)
        h_prev = jnp.where(row == 0, h_last, pltpu.roll(hs_t, 1, 0))
        gate, th = _gelu(gb)
        y_ref[...] = (hs_t * gate).astype(y_ref.dtype)
        dy = _dot_nt(dh_ref[...].astype(MXU_DT), wout_ref[...])
        a_next = jnp.where(row == tm - 1, carry[0:1, :], pltpu.roll(a, tm - 1, 0))
        d_hs, cum = _scan_bwd(a_next, dy * gate, tm)
        d_hs = d_hs + cum * carry[1:2, :]
        carry[0:1, :] = a[0:1, :]
        carry[1:2, :] = d_hs[0:1, :]
        d_b = jnp.where(_valid_rows(r, tm), d_hs, 0.0)
        d_iu = d_b * mult
        d_log_a = d_hs * h_prev * a - d_b * (i_gate * u) * (a * a) / jnp.maximum(mult, 1e-30)
        dvec_ref[G_LAMBDA:G_LAMBDA + 1, :] += jnp.sum(d_log_a * r_gate, axis=0, keepdims=True) * (-RG_C)
        d_pre_r = d_log_a * (-RG_C * sp) * r_gate * (1.0 - r_gate)
        d_pre_i = d_iu * u * i_gate * (1.0 - i_gate)
        dvec_ref[G_B_A:G_B_A + 1, :] += jnp.sum(d_pre_r, axis=0, keepdims=True)
        dvec_ref[G_B_X:G_B_X + 1, :] += jnp.sum(d_pre_i, axis=0, keepdims=True)
        dbuf[pl.ds(0, tm), :] = d_iu * i_gate
        d_pre_r = d_pre_r.astype(MXU_DT)
        d_pre_i = d_pre_i.astype(MXU_DT)
        for k in range(nb):
            blk = slice(k * bd, (k + 1) * bd)
            dwa_ref[k] += _dot_tn(ub[:, blk], d_pre_r[:, blk])
            dwx_ref[k] += _dot_tn(ub[:, blk], d_pre_i[:, blk])
            dbuf[pl.ds(0, tm), blk] += _dot_nt(d_pre_r[:, blk], wa_ref[k]) + _dot_nt(d_pre_i[:, blk], wx_ref[k])
        du = dbuf[pl.ds(0, tm), :]
        dvec_ref[G_CONV_B:G_CONV_B + 1, :] += jnp.sum(du, axis=0, keepdims=True)
        for k in range(width):
            dvec_ref[G_CONV_W + k:G_CONV_W + k + 1, :] += jnp.sum(taps[k] * du, axis=0, keepdims=True)
        d_rb = sum(cw_ref[k:k + 1, :] * dbuf[pl.ds(width - 1 - k, tm), :] for k in range(width))
        dbuf[pl.ds(tm, F32_ROWS), :] = dbuf[pl.ds(0, F32_ROWS), :]
        dhh_ref[:, 0:dr] = (dy * hs_t * _gelu_grad(gb, th)).astype(dhh_ref.dtype)
        dhh_ref[:, dr:2 * dr] = d_rb.astype(dhh_ref.dtype)

        @pl.when(i == nt - 1)
        def _():
            lam = vec_ref[V_LAMBDA:V_LAMBDA + 1, :]
            dvec_ref[G_LAMBDA:G_LAMBDA + 1, :] = dvec_ref[G_LAMBDA:G_LAMBDA + 1, :] * (-_sigmoid(-lam))

    rev = lambda i: (nt - 1 - i, 0)
    return _launch(
        body, [dh, hh, hh, hs, hs, cw, vec, wa, wx, w_out], name="rg_bwd", grid=(nt,),
        in_specs=[pl.BlockSpec((tm, d), rev), pl.BlockSpec((tm, 2 * dr), rev),
                  pl.BlockSpec((ACT_ROWS, 2 * dr), lambda i: (jnp.maximum((nt - 1 - i) * halo_act - 1, 0), 0)),
                  pl.BlockSpec((tm, dr), rev),
                  pl.BlockSpec((F32_ROWS, dr), lambda i: (jnp.maximum((nt - 1 - i) * halo_f32 - 1, 0), 0)),
                  _const(cw.shape), _const(vec.shape), _const(wa.shape), _const(wx.shape), _const(w_out.shape)],
        out_specs=[pl.BlockSpec((tm, 2 * dr), rev), pl.BlockSpec((tm, dr), rev), _const((F32_ROWS, dr)),
                   _const(wa.shape), _const(wx.shape)],
        out_shape=[jax.ShapeDtypeStruct((t_len, 2 * dr), ACT_DT), jax.ShapeDtypeStruct((t_len, dr), ACT_DT),
                   jax.ShapeDtypeStruct((F32_ROWS, dr), F32), jax.ShapeDtypeStruct(wa.shape, F32),
                   jax.ShapeDtypeStruct(wx.shape, F32)],
        scratch_shapes=[pltpu.VMEM((F32_ROWS + tm, dr), F32), pltpu.VMEM((tm + F32_ROWS, dr), F32),
                        pltpu.VMEM((2, tm, dr), F32), pltpu.VMEM((F32_ROWS, dr), F32)],
        ride=ride,
    )


def _inproj_bwd(dpre, h, dh, w, g, *, tm, first=None):
    t_len, d = dh.shape
    nt = t_len // tm
    nq, _, n = w.shape

    def body(dpre_ref, h_ref, *rest):
        first_ref = rest[0] if first is not None else None
        dh_ref, w_ref, g_ref, out_ref, hn_ref, dg_ref = rest[first is not None:][:6]
        i = pl.program_id(0)

        @pl.when(i == 0)
        def _():
            dg_ref[...] = jnp.zeros_like(dg_ref)

        h_in = h_ref[...] if first is None else jnp.where(i == 0, first_ref[...], h_ref[...])
        gain = g_ref[...]
        hn, xhat, rstd = _rms(h_in, gain)
        hn_ref[...] = hn.astype(hn_ref.dtype)
        dhn = sum(_dot_nt(dpre_ref[:, q * n:(q + 1) * n], w_ref[q]) for q in range(nq))
        dg_ref[0:1, :] += jnp.sum(dhn * xhat, axis=0, keepdims=True)
        dh_in = jnp.where(_valid_rows(i, tm), dh_ref[...] + _rms_bwd(dhn, xhat, rstd, gain), 0.0)
        if first is None:
            out_ref[...] = dh_in
        else:
            dfirst_ref = rest[-1]

            @pl.when(i == 0)
            def _():
                dfirst_ref[...] = dh_in

            @pl.when(i > 0)
            def _():
                out_ref[...] = dh_in

    row = lambda i: (i, 0)
    after_first = lambda i: (jnp.maximum(i - 1, 0), 0)
    operands = [dpre, h] + ([first] if first is not None else []) + [dh, w, g]
    out_specs = [pl.BlockSpec((tm, d), row), pl.BlockSpec((tm, d), row), _const((F32_ROWS, d))]
    out_shape = [jax.ShapeDtypeStruct((t_len, d), F32), jax.ShapeDtypeStruct((t_len, d), ACT_DT),
                 jax.ShapeDtypeStruct((F32_ROWS, d), F32)]
    if first is not None:
        out_specs = [pl.BlockSpec((tm, d), after_first)] + out_specs[1:] + [_const((tm, d))]
        out_shape = [jax.ShapeDtypeStruct((t_len - tm, d), F32)] + out_shape[1:] + [jax.ShapeDtypeStruct((tm, d), F32)]
    return pl.pallas_call(
        body, name="inproj_bwd", grid=(nt,),
        in_specs=[pl.BlockSpec((tm, nq * n), row), pl.BlockSpec((tm, d), row if first is None else after_first)]
        + ([_const((tm, d))] if first is not None else [])
        + [pl.BlockSpec((tm, d), row), _const(w.shape), _const((1, d))],
        out_specs=out_specs, out_shape=out_shape, compiler_params=_params(),
    )(*operands)


def _weight_grad(a, b, nb, *, rows):
    t_len, k_dim = a.shape
    n = b.shape[1] // nb
    nt = t_len // rows

    def body(a_ref, b_ref, out_ref, wire_ref):
        @pl.when(pl.program_id(1) == 0)
        def _():
            out_ref[...] = jnp.zeros_like(out_ref)

        out_ref[0] += _dot_tn(a_ref[...].astype(MXU_DT), b_ref[...].astype(MXU_DT))

        @pl.when(pl.program_id(1) == nt - 1)
        def _():
            wire_ref[...] = out_ref[...].astype(wire_ref.dtype)

    block = pl.BlockSpec((1, k_dim, n), lambda j, i: (j, 0, 0))
    return pl.pallas_call(
        body, name="weight_grad", grid=(nb, nt),
        in_specs=[pl.BlockSpec((rows, k_dim), lambda j, i: (i, 0)), pl.BlockSpec((rows, n), lambda j, i: (i, j))],
        out_specs=[block, block],
        out_shape=[jax.ShapeDtypeStruct((nb, k_dim, n), F32), jax.ShapeDtypeStruct((nb, k_dim, n), WIRE_DT)],
        compiler_params=_params(2),
    )(a, b)


def _loss_head(h, target, g, *, tm):
    t_len, d = h.shape
    nt = t_len // tm

    def body(h_ref, t_ref, g_ref, dh_ref, sq_ref, dg_ref):
        i = pl.program_id(0)

        @pl.when(i == 0)
        def _():
            sq_ref[...] = jnp.zeros_like(sq_ref)
            dg_ref[...] = jnp.zeros_like(dg_ref)
            dh_ref[...] = jnp.zeros_like(dh_ref)

        @pl.when(i > 0)
        def _():
            gain = g_ref[...]
            out, xhat, rstd = _rms(h_ref[...], gain)
            err = out - t_ref[...]
            sq_ref[0:1, :] += jnp.sum(err * err, axis=0, keepdims=True)
            dout = err * (1.0 / d)
            dg_ref[0:1, :] += jnp.sum(dout * xhat, axis=0, keepdims=True)
            dh_ref[...] = _rms_bwd(dout, xhat, rstd, gain)

    row = lambda i: (i, 0)
    return pl.pallas_call(
        body, name="loss_head", grid=(nt,),
        in_specs=[pl.BlockSpec((tm, d), row), pl.BlockSpec((tm, d), lambda i: (jnp.maximum(i - 1, 0), 0)), _const((1, d))],
        out_specs=[pl.BlockSpec((tm, d), row), _const((F32_ROWS, d)), _const((F32_ROWS, d))],
        out_shape=[jax.ShapeDtypeStruct((t_len, d), F32), jax.ShapeDtypeStruct((F32_ROWS, d), F32),
                   jax.ShapeDtypeStruct((F32_ROWS, d), F32)],
        compiler_params=_params(),
    )(h, target, g)


def _adamw(w, m, v, parts, *, rows, layer=0, into=None):
    n_layers, n_rows, n_cols = w.shape
    nt = n_rows // rows
    n_parts = len(parts)

    def body(w_ref, m_ref, v_ref, *rest):
        part_refs, (g_ref, d_ref, nm_ref, nv_ref) = rest[:n_parts], rest[-4:]
        w_ref, m_ref, v_ref, g_ref, d_ref, nm_ref, nv_ref = (r.at[0] for r in (w_ref, m_ref, v_ref, g_ref, d_ref, nm_ref, nv_ref))
        grad = part_refs[0][...].astype(F32)
        for p in part_refs[1:]:
            grad = grad + p[...].astype(F32)
        new_m = ADAM_B1 * m_ref[...] + (1.0 - ADAM_B1) * grad
        new_v = ADAM_B2 * v_ref[...] + (1.0 - ADAM_B2) * (grad * grad)
        m_hat = new_m / (1.0 - ADAM_B1 ** ADAM_STEP)
        v_hat = new_v / (1.0 - ADAM_B2 ** ADAM_STEP)
        g_ref[...] = grad
        d_ref[...] = -ADAM_LR * (m_hat / (jnp.sqrt(v_hat) + ADAM_EPS) + ADAM_WD * w_ref[...])
        nm_ref[...] = new_m
        nv_ref[...] = new_v

    spec = pl.BlockSpec((rows, n_cols), lambda i: (i, 0))
    layer_spec = pl.BlockSpec((1, rows, n_cols), lambda i: (layer, i, 0))
    into = list(into or [])
    return pl.pallas_call(
        body, name="adamw", grid=(nt,),
        in_specs=[layer_spec] * 3 + [spec] * n_parts + [ANY] * len(into), out_specs=[layer_spec] * 4,
        out_shape=[jax.ShapeDtypeStruct(w.shape, F32)] * 4,
        input_output_aliases={3 + n_parts + k: k for k in range(len(into))},
        compiler_params=_params(),
    )(w, m, v, *parts, *into)


def _sum_parts(own, recv, *, rows):
    n_rows, n_cols = own.shape
    n_recv = recv.shape[0]

    def body(own_ref, recv_ref, out_ref):
        acc = own_ref[...].astype(F32)
        for j in range(n_recv):
            acc = acc + recv_ref[j].astype(F32)
        out_ref[...] = acc

    return pl.pallas_call(
        body, name="sum_parts", grid=(n_rows // rows,),
        in_specs=[pl.BlockSpec((rows, n_cols), lambda i: (i, 0)), pl.BlockSpec((n_recv, rows, n_cols), lambda i: (0, i, 0))],
        out_specs=pl.BlockSpec((rows, n_cols), lambda i: (i, 0)),
        out_shape=jax.ShapeDtypeStruct(own.shape, F32),
        compiler_params=_params(),
    )(own, recv)


def _swap_cores(arrays):
    nk = len(arrays)

    def body(*refs):
        ins, outs, (send_sems, recv_sems) = refs[:nk], refs[nk:2 * nk], refs[2 * nk:]
        x, y, c = _place()
        sends = [pltpu.make_async_remote_copy(
            src_ref=ins[k], dst_ref=outs[k], send_sem=send_sems.at[k], recv_sem=recv_sems.at[k],
            device_id=(x, y, 1 - c), device_id_type=MESH_ID) for k in range(nk)]
        for cp in sends:
            cp.start()
        for cp in sends:
            cp.wait_recv()
        for cp in sends:
            cp.wait_send()

    return pl.pallas_call(
        body, name="swap_cores", in_specs=[ANY] * nk, out_specs=[ANY] * nk,
        out_shape=[jax.ShapeDtypeStruct(a.shape, a.dtype) for a in arrays],
        scratch_shapes=[pltpu.SemaphoreType.DMA((nk,)), pltpu.SemaphoreType.DMA((nk,))],
    )(*arrays)


def _gather_devices(v):
    def body(v_ref, out_ref, send_sems, recv_sems, local_sem):
        x, y, c = _place()
        local = pltpu.make_async_copy(v_ref, out_ref.at[4 * x + 2 * y + c], local_sem)
        local.start()
        sends = []
        for flip in range(1, 8):
            px, py, pc = x ^ (flip >> 2), y ^ ((flip >> 1) & 1), c ^ (flip & 1)
            sends.append(pltpu.make_async_remote_copy(
                src_ref=v_ref, dst_ref=out_ref.at[4 * x + 2 * y + c], send_sem=send_sems.at[flip - 1],
                recv_sem=recv_sems.at[flip - 1], device_id=(px, py, pc), device_id_type=MESH_ID))
            sends[-1].start()
        for flip in range(1, 8):
            px, py, pc = x ^ (flip >> 2), y ^ ((flip >> 1) & 1), c ^ (flip & 1)
            pltpu.make_async_remote_copy(
                src_ref=v_ref, dst_ref=out_ref.at[4 * px + 2 * py + pc], send_sem=send_sems.at[flip - 1],
                recv_sem=recv_sems.at[flip - 1], device_id=(px, py, pc), device_id_type=MESH_ID).wait_recv()
        for cp in sends:
            cp.wait_send()
        local.wait()

    return pl.pallas_call(
        body, name="gather_devices", in_specs=[ANY], out_specs=ANY,
        out_shape=jax.ShapeDtypeStruct((8,) + v.shape, v.dtype),
        scratch_shapes=[pltpu.SemaphoreType.DMA((7,)), pltpu.SemaphoreType.DMA((7,)), pltpu.SemaphoreType.DMA],
    )(v)


def _pack(arrays, pad_rows=F32_ROWS):
    flat = jnp.concatenate([a.reshape(-1).astype(F32) for a in arrays])
    rows = -(-flat.shape[0] // (LANES * pad_rows)) * pad_rows
    return jnp.pad(flat, (0, rows * LANES - flat.shape[0])).reshape(rows, LANES)


def _unpack(packed, shapes):
    flat, out, off = packed.reshape(-1), [], 0
    for s in shapes:
        size = 1
        for dim in s:
            size *= dim
        out.append(flat[off:off + size].reshape(s))
        off += size
    return out


def _divisor_rows(n_rows, most=256):
    best = None
    for r in range(ACT_ROWS, most + 1, ACT_ROWS):
        if n_rows % r == 0:
            best = r
    return best or n_rows


def kernel(x, meta_tokens, norm_mix_g, norm_ffn_g, final_norm_g, sc_w_in, sc_conv_w, sc_w_out, rg_w_in, rg_conv_w, rg_conv_b, rg_w_gate_a, rg_b_gate_a, rg_w_gate_x, rg_b_gate_x, rg_lambda, rg_w_out, ffn_w_up, ffn_conv_w, ffn_w_down, loss_target, m_meta_tokens, m_norm_mix_g, m_norm_ffn_g, m_final_norm_g, m_sc_w_in, m_sc_conv_w, m_sc_w_out, m_rg_w_in, m_rg_conv_w, m_rg_conv_b, m_rg_w_gate_a, m_rg_b_gate_a, m_rg_w_gate_x, m_rg_b_gate_x, m_rg_lambda, m_rg_w_out, m_ffn_w_up, m_ffn_conv_w, m_ffn_w_down, v_meta_tokens, v_norm_mix_g, v_norm_ffn_g, v_final_norm_g, v_sc_w_in, v_sc_conv_w, v_sc_w_out, v_rg_w_in, v_rg_conv_w, v_rg_conv_b, v_rg_w_gate_a, v_rg_b_gate_a, v_rg_w_gate_x, v_rg_b_gate_x, v_rg_lambda, v_rg_w_out, v_ffn_w_up, v_ffn_conv_w, v_ffn_w_down):
    weights = dict(meta_tokens=meta_tokens, norm_mix_g=norm_mix_g, norm_ffn_g=norm_ffn_g, final_norm_g=final_norm_g, sc_w_in=sc_w_in, sc_conv_w=sc_conv_w, sc_w_out=sc_w_out, rg_w_in=rg_w_in, rg_conv_w=rg_conv_w, rg_conv_b=rg_conv_b, rg_w_gate_a=rg_w_gate_a, rg_b_gate_a=rg_b_gate_a, rg_w_gate_x=rg_w_gate_x, rg_b_gate_x=rg_b_gate_x, rg_lambda=rg_lambda, rg_w_out=rg_w_out, ffn_w_up=ffn_w_up, ffn_conv_w=ffn_conv_w, ffn_w_down=ffn_w_down)
    m_in = dict(meta_tokens=m_meta_tokens, norm_mix_g=m_norm_mix_g, norm_ffn_g=m_norm_ffn_g, final_norm_g=m_final_norm_g, sc_w_in=m_sc_w_in, sc_conv_w=m_sc_conv_w, sc_w_out=m_sc_w_out, rg_w_in=m_rg_w_in, rg_conv_w=m_rg_conv_w, rg_conv_b=m_rg_conv_b, rg_w_gate_a=m_rg_w_gate_a, rg_b_gate_a=m_rg_b_gate_a, rg_w_gate_x=m_rg_w_gate_x, rg_b_gate_x=m_rg_b_gate_x, rg_lambda=m_rg_lambda, rg_w_out=m_rg_w_out, ffn_w_up=m_ffn_w_up, ffn_conv_w=m_ffn_conv_w, ffn_w_down=m_ffn_w_down)
    v_in = dict(meta_tokens=v_meta_tokens, norm_mix_g=v_norm_mix_g, norm_ffn_g=v_norm_ffn_g, final_norm_g=v_final_norm_g, sc_w_in=v_sc_w_in, sc_conv_w=v_sc_conv_w, sc_w_out=v_sc_w_out, rg_w_in=v_rg_w_in, rg_conv_w=v_rg_conv_w, rg_conv_b=v_rg_conv_b, rg_w_gate_a=v_rg_w_gate_a, rg_b_gate_a=v_rg_b_gate_a, rg_w_gate_x=v_rg_w_gate_x, rg_b_gate_x=v_rg_b_gate_x, rg_lambda=v_rg_lambda, rg_w_out=v_rg_w_out, ffn_w_up=v_ffn_w_up, ffn_conv_w=v_ffn_conv_w, ffn_w_down=v_ffn_w_down)
    names = list(weights)

    tokens, target = x[0], loss_target[0]
    seq, d = tokens.shape
    tm = _row_tile(seq)
    t_len = seq + tm
    wg_rows = 5 * tm if t_len % (5 * tm) == 0 else tm
    xi, yi, _ = _place()
    chip = 2 * xi + yi
    mesh_axes = ("x", "y", "c")

    wire = lambda w: w.astype(WIRE_DT)
    small_sharded = ["meta_tokens", "sc_conv_w", "rg_conv_w", "rg_conv_b", "rg_b_gate_a", "rg_b_gate_x", "rg_lambda", "ffn_conv_w"]
    small_2d = {n: weights[n].reshape(-1, weights[n].shape[-1]) for n in small_sharded}
    w_sc_in, w_sc_out, small_by_chip = _exchange(
        _Gather([wire(sc_w_in[0]), wire(sc_w_out[0]), _pack([small_2d[n] for n in small_sharded])]), "gather_first")
    w_sc_out = w_sc_out.reshape(-1, d)
    gather_ffn0 = _Gather([wire(ffn_w_up[0]), wire(ffn_w_down[0])])
    gather_rest = _Gather([wire(rg_w_in[0]), wire(rg_w_out[0]), wire(ffn_w_up[1]), wire(ffn_w_down[1])])
    small_len = sum(a.size for a in small_2d.values())
    by_chip = small_by_chip.reshape(N_CHIPS, -1)[:, :small_len]
    full, off = {}, 0
    for n in small_sharded:
        rows, width = small_2d[n].shape
        full[n] = by_chip[:, off:off + rows * width].reshape(N_CHIPS, rows, width).transpose(1, 0, 2).reshape(rows, N_CHIPS * width)
        off += rows * width
    sc_cw, rg_cw = full["sc_conv_w"], full["rg_conv_w"]
    ffn_cw = [full["ffn_conv_w"][0:3], full["ffn_conv_w"][3:6]]
    d_rnn = rg_cw.shape[1]
    vec = jnp.concatenate([full["rg_conv_b"], full["rg_b_gate_a"], full["rg_b_gate_x"], full["rg_lambda"],
                           jnp.zeros((F32_ROWS - 4, d_rnn), F32)])
    wa, wx = rg_w_gate_a[0].astype(MXU_DT), rg_w_gate_x[0].astype(MXU_DT)
    first = jnp.concatenate([jnp.zeros((tm - N_META, d), F32), full["meta_tokens"]])
    g_mix = [norm_mix_g[0:1], norm_mix_g[1:2]]
    g_ffn = [norm_ffn_g[0:1], norm_ffn_g[1:2]]

    h1, hh0, w_up0, w_dn0 = _sc_fwd(tokens, first, g_mix[0], w_sc_in, sc_cw, w_sc_out, tm=tm, ride=gather_ffn0)
    h2, hu0, w_rg_in, w_rg_out, w_up1, w_dn1 = _ffn_fwd(h1, g_ffn[0], w_up0, ffn_cw[0], w_dn0.reshape(-1, d), tm=tm,
                                                         ride=gather_rest)
    w_up, w_dn, w_rg_out = [w_up0, w_up1], [w_dn0.reshape(-1, d), w_dn1.reshape(-1, d)], w_rg_out.reshape(-1, d)
    h3, hh1, hs = _rg_fwd(h2, g_mix[1], w_rg_in, rg_cw, vec, wa, wx, w_rg_out, tm=tm)
    h4, hu1 = _ffn_fwd(h3, g_ffn[1], w_up[1], ffn_cw[1], w_dn[1], tm=tm)
    dh4, sq, d_final = _loss_head(h4, target, final_norm_g.reshape(1, d), tm=tm)
    loss = lax.psum(jnp.sum(sq[0]) * (0.5 / d), mesh_axes)

    def by_chip_rows(pair):
        return [p.reshape(N_CHIPS, -1, d) for p in pair]

    def ffn_backward(dh_out, h_in, hu, layer, ride):
        act, dhu, dcw, *landed = _ffn_bwd(dh_out, hu, ffn_cw[layer], w_dn[layer], tm=tm, ride=ride)
        dh_in, hn, dg = _inproj_bwd(dhu, h_in, dh_out, w_up[layer], g_ffn[layer], tm=tm)
        d_up = _weight_grad(hn, dhu, N_CHIPS, rows=wg_rows)
        d_dn = by_chip_rows(_weight_grad(act, dh_out, 1, rows=wg_rows))
        return dh_in, d_up, d_dn, dcw[0:3], dg[0], landed

    dh3, d_up1, d_dn1, d_fcw1, d_gf1, _ = ffn_backward(dh4, h3, hu1, 1, None)
    dhh1, y_rg, d_vec, d_wa, d_wx, *landed_ffn1 = _rg_bwd(dh3, hh1, hs, rg_cw, vec, wa, wx, w_rg_out, tm=tm,
                                                          ride=_Scatter([d_up1[1], d_dn1[1]]))
    dh2, hn_rg, d_gm1 = _inproj_bwd(dhh1, h2, dh3, w_rg_in, g_mix[1], tm=tm)
    d_rg_in = _weight_grad(hn_rg, dhh1, N_CHIPS, rows=wg_rows)
    d_rg_out = by_chip_rows(_weight_grad(y_rg, dh3, 1, rows=wg_rows))
    dh1, d_up0, d_dn0, d_fcw0, d_gf0, landed_rg = ffn_backward(dh2, h1, hu0, 0, _Scatter([d_rg_in[1], d_rg_out[1]]))
    dhh0, z_sc, d_sccw, *landed_ffn0 = _sc_bwd(dh1, hh0, sc_cw, w_sc_out, tm=tm, ride=_Scatter([d_up0[1], d_dn0[1]]))
    grad_x, hn_sc, d_gm0, d_first = _inproj_bwd(dhh0, tokens, dh1, w_sc_in, g_mix[0], tm=tm, first=first)
    d_sc_in = _weight_grad(hn_sc, dhh0, N_CHIPS, rows=wg_rows)
    d_sc_out = by_chip_rows(_weight_grad(z_sc, dh1, 1, rows=wg_rows))
    landed_sc = _exchange(_Scatter([d_sc_in[1], d_sc_out[1]]), "scatter_last")
    grad_x = grad_x[None]

    big = [("sc_w_in", 0, d_sc_in, landed_sc[0]), ("sc_w_out", 0, d_sc_out, landed_sc[1]),
           ("rg_w_in", 0, d_rg_in, landed_rg[0]), ("rg_w_out", 0, d_rg_out, landed_rg[1]),
           ("ffn_w_up", 0, d_up0, landed_ffn0[0]), ("ffn_w_up", 1, d_up1, landed_ffn1[0]),
           ("ffn_w_down", 0, d_dn0, landed_ffn0[1]), ("ffn_w_down", 1, d_dn1, landed_ffn1[1])]
    core_sum = []
    for _, _, (partial, _), received in big:
        own = lax.dynamic_index_in_dim(partial, chip, 0, keepdims=False)
        core_sum.append(_sum_parts(own, received, rows=_divisor_rows(own.shape[0])))
    other_sum = _swap_cores(core_sum)
    out = {k: {} for k in ("grad", "delta", "m", "v")}
    stacked = {}
    for (n, layer, _, _), mine, theirs in zip(big, core_sum, other_sum):
        stacked[n] = _adamw(weights[n], m_in[n], v_in[n], [mine, theirs], rows=_divisor_rows(mine.shape[0]), layer=layer,
                            into=stacked.get(n))
    for n, res in stacked.items():
        for k, key in enumerate(("grad", "delta", "m", "v")):
            out[key][n] = res[k]

    d_meta = d_first[tm - N_META:]
    small_full = {"meta_tokens": d_meta, "sc_conv_w": d_sccw[0:3], "rg_conv_w": d_vec[G_CONV_W:G_CONV_W + 4],
                  "rg_conv_b": d_vec[G_CONV_B:G_CONV_B + 1], "rg_b_gate_a": d_vec[G_B_A:G_B_A + 1],
                  "rg_b_gate_x": d_vec[G_B_X:G_B_X + 1], "rg_lambda": d_vec[G_LAMBDA:G_LAMBDA + 1],
                  "ffn_conv_w": jnp.concatenate([d_fcw0, d_fcw1])}
    replicated = {"norm_mix_g": jnp.stack([d_gm0[0], d_gm1[0]]), "norm_ffn_g": jnp.stack([d_gf0, d_gf1]),
                  "final_norm_g": d_final[0], "rg_w_gate_a": d_wa[None], "rg_w_gate_x": d_wx[None]}
    small_names = small_sharded + list(replicated)
    partial_small = [small_full[n] for n in small_sharded] + [replicated[n] for n in replicated]
    packed = _pack(partial_small)
    by_device = _gather_devices(packed)
    total = _sum_parts(by_device[0], by_device[1:], rows=packed.shape[0])
    summed = dict(zip(small_names, _unpack(total, [p.shape for p in partial_small])))
    grads = {}
    for n in small_sharded:
        width = small_2d[n].shape[1]
        grads[n] = lax.dynamic_slice_in_dim(summed[n], chip * width, width, axis=1).reshape(weights[n].shape)
    for n in replicated:
        grads[n] = summed[n].reshape(weights[n].shape)
    shapes = [weights[n].shape for n in small_names]
    packed_w = _pack([weights[n] for n in small_names])
    res = _adamw(packed_w[None], _pack([m_in[n] for n in small_names])[None], _pack([v_in[n] for n in small_names])[None],
                 [_pack([grads[n] for n in small_names])], rows=packed_w.shape[0])
    for k, key in enumerate(("grad", "delta", "m", "v")):
        out[key].update(dict(zip(small_names, _unpack(res[k][0], shapes))))

    return (loss, grad_x, *[out["grad"][n] for n in names], *[out["delta"][n] for n in names],
            *[out["m"][n] for n in names], *[out["v"][n] for n in names])
```

```python
import functools

import jax
import jax.numpy as jnp
from jax import lax
from jax.experimental import pallas as pl
from jax.experimental.pallas import tpu as pltpu

F32 = jnp.float32
MXU_DT = jnp.bfloat16
ACT_DT = jnp.bfloat16
WIRE_DT = jnp.bfloat16
MESH_ID = pl.DeviceIdType.MESH

N_META = 16
RMS_EPS = 1e-6
RG_C = 8.0
ADAM_LR, ADAM_B1, ADAM_B2, ADAM_EPS, ADAM_WD, ADAM_STEP = 0.001, 0.9, 0.999, 1e-08, 0.01, 10
N_CHIPS = 4
VMEM_LIMIT = 60 * 1024 * 1024
F32_ROWS = 8
ACT_ROWS = 16
LANES = 128


def _row_tile(seq):
    for tm in (256, 128, 64, 32, 16):
        if seq % tm == 0:
            return tm
    raise ValueError(f"sequence length {seq} is not a multiple of 16")


def _params(n_axes=1, **kw):
    return pltpu.CompilerParams(dimension_semantics=("arbitrary",) * n_axes, vmem_limit_bytes=VMEM_LIMIT, **kw)


def _const(shape):
    return pl.BlockSpec(shape, lambda *_: (0,) * len(shape), pipeline_mode=pl.Buffered(1))


def _dot(a, b):
    return jnp.dot(a, b, preferred_element_type=F32)


def _dot_nt(a, b):
    return lax.dot_general(a, b, (((1,), (1,)), ((), ())), preferred_element_type=F32)


def _dot_tn(a, b):
    return lax.dot_general(a, b, (((0,), (0,)), ((), ())), preferred_element_type=F32)


def _sigmoid(x):
    return 1.0 / (1.0 + jnp.exp(-x))


def _rms(h, g):
    rstd = lax.rsqrt(jnp.mean(h * h, axis=-1, keepdims=True) + RMS_EPS)
    xhat = h * rstd
    return xhat * g, xhat, rstd


def _rms_bwd(dhn, xhat, rstd, g):
    dx = dhn * g
    return rstd * (dx - xhat * jnp.mean(dx * xhat, axis=-1, keepdims=True))


def _gelu(x):
    k = 0.7978845608028654
    t = jnp.tanh(k * (x + 0.044715 * x * x * x))
    return 0.5 * x * (1.0 + t), t


def _gelu_grad(x, t):
    k = 0.7978845608028654
    return 0.5 * (1.0 + t) + 0.5 * x * (1.0 - t * t) * k * (1.0 + 3 * 0.044715 * x * x)


def _softplus(x):
    e = jnp.exp(-jnp.abs(x))
    return jnp.maximum(x, 0.0) + jnp.where(e < 1e-4, e - 0.5 * e * e, jnp.log(1.0 + e))


def _expm1_neg(z):
    series = z * (1.0 + z * (0.5 + z * (1.0 / 6 + z * (1.0 / 24 + z * (1.0 / 120)))))
    return jnp.where(z > -0.1, series, jnp.exp(z) - 1.0)


def _tile_order(a, tm):
    return a.reshape(-1, F32_ROWS, tm // F32_ROWS, a.shape[-1]).swapaxes(1, 2).reshape(a.shape)


def _time_order(a, tm):
    return a.reshape(-1, tm // F32_ROWS, F32_ROWS, a.shape[-1]).swapaxes(1, 2).reshape(a.shape)


def _valid_rows(tile, tm):
    row = lax.broadcasted_iota(jnp.int32, (tm, 1), 0)
    time = (row & (F32_ROWS - 1)) * (tm // F32_ROWS) + (row >> 3) + tile * tm
    return time >= tm - N_META


def _sublane():
    return lax.broadcasted_iota(jnp.int32, (F32_ROWS, 1), 0)


def _past_rows(width):
    return (width - 1) * F32_ROWS


def _halo_block(past, tm, nt):
    rows = -(-past // ACT_ROWS) * ACT_ROWS
    return rows, lambda i: (jnp.maximum((nt - 1 - i) * (tm // rows) - 1, 0), 0)


def _link_past(buf, cols, width, tm):
    past = _past_rows(width)
    for k in range(1, width):
        rows = pl.ds(past - F32_ROWS * k, F32_ROWS)
        before = pltpu.roll(buf[rows, cols], 1, 0)
        mine = pltpu.roll(buf[pl.ds(past + tm - F32_ROWS * k, F32_ROWS), cols], 1, 0)
        buf[rows, cols] = jnp.where(_sublane() == 0, before, mine)


def _link_future(buf, cols, width, tm):
    for k in range(1, width):
        rows = pl.ds(tm + F32_ROWS * (k - 1), F32_ROWS)
        after = pltpu.roll(buf[rows, cols], F32_ROWS - 1, 0)
        mine = pltpu.roll(buf[pl.ds(F32_ROWS * (k - 1), F32_ROWS), cols], F32_ROWS - 1, 0)
        buf[rows, cols] = jnp.where(_sublane() == F32_ROWS - 1, after, mine)


def _conv_taps(buf, cols, width, tm):
    return [buf[pl.ds(F32_ROWS * k, tm), cols] for k in range(width)]


def _conv_back(buf, cw_ref, cols, width, tm):
    return sum(cw_ref[k:k + 1, cols] * buf[pl.ds(F32_ROWS * (width - 1 - k), tm), cols] for k in range(width))


ANY = pl.BlockSpec(memory_space=pl.ANY)


def _place():
    return lax.axis_index("x"), lax.axis_index("y"), lax.axis_index("c")


def _other_chips(x, y):
    return [(1 - x, y), (x, 1 - y), (1 - x, 1 - y)]


class _Gather:
    def __init__(self, shards):
        nk = len(shards)
        self.arrays = list(shards)
        self.out_shape = [jax.ShapeDtypeStruct((N_CHIPS,) + s.shape, s.dtype) for s in shards]
        self.scratch = [pltpu.SemaphoreType.DMA((nk, 3)), pltpu.SemaphoreType.DMA((nk, 3)), pltpu.SemaphoreType.DMA((nk,))]

    def run(self, ins, outs, sems, start):
        send_sems, recv_sems, local_sems = sems
        x, y, c = _place()
        mine = 2 * x + y
        for k in range(len(ins)):
            local = pltpu.make_async_copy(ins[k], outs[k].at[mine], local_sems.at[k])
            local.start() if start else local.wait()
            for j, (px, py) in enumerate(_other_chips(x, y)):
                sems_kj = dict(send_sem=send_sems.at[k, j], recv_sem=recv_sems.at[k, j], device_id=(px, py, c),
                               device_id_type=MESH_ID)
                send = pltpu.make_async_remote_copy(src_ref=ins[k], dst_ref=outs[k].at[mine], **sems_kj)
                if start:
                    send.start()
                else:
                    pltpu.make_async_remote_copy(src_ref=ins[k], dst_ref=outs[k].at[2 * px + py], **sems_kj).wait_recv()
                    send.wait_send()


class _Scatter:
    def __init__(self, parts):
        nk = len(parts)
        self.arrays = list(parts)
        self.out_shape = [jax.ShapeDtypeStruct((3,) + p.shape[1:], p.dtype) for p in parts]
        self.scratch = [pltpu.SemaphoreType.DMA((nk, 3)), pltpu.SemaphoreType.DMA((nk, 3))]

    def run(self, ins, outs, sems, start):
        send_sems, recv_sems = sems
        x, y, c = _place()
        for k in range(len(ins)):
            for j, (px, py) in enumerate(_other_chips(x, y)):
                send = pltpu.make_async_remote_copy(
                    src_ref=ins[k].at[2 * px + py], dst_ref=outs[k].at[j], send_sem=send_sems.at[k, j],
                    recv_sem=recv_sems.at[k, j], device_id=(px, py, c), device_id_type=MESH_ID)
                if start:
                    send.start()
                else:
                    send.wait_recv()
                    send.wait_send()


def _exchange(ride, name):
    n_in, n_out = len(ride.arrays), len(ride.out_shape)

    def body(*refs):
        ride.run(refs[:n_in], refs[n_in:n_in + n_out], refs[n_in + n_out:], start=True)
        ride.run(refs[:n_in], refs[n_in:n_in + n_out], refs[n_in + n_out:], start=False)

    return pl.pallas_call(body, name=name, in_specs=[ANY] * n_in, out_specs=[ANY] * n_out, out_shape=ride.out_shape,
                          scratch_shapes=ride.scratch)(*ride.arrays)


def _launch(body, operands, *, name, grid, in_specs, out_specs, out_shape, scratch_shapes=(), ride=None):
    common = dict(name=name, grid=grid, compiler_params=_params(len(grid)))
    if ride is None:
        return pl.pallas_call(body, in_specs=in_specs, out_specs=out_specs, out_shape=out_shape,
                              scratch_shapes=list(scratch_shapes), **common)(*operands)
    n_in, n_out, n_scr = len(operands), len(out_shape), len(scratch_shapes)
    r_in, r_out = len(ride.arrays), len(ride.out_shape)
    last = grid[0] - 1

    def riding(*refs):
        ins, refs = refs[:n_in], refs[n_in:]
        r_ins, refs = refs[:r_in], refs[r_in:]
        outs, refs = refs[:n_out], refs[n_out:]
        r_outs, refs = refs[:r_out], refs[r_out:]
        scr, r_sems = refs[:n_scr], refs[n_scr:]
        i = pl.program_id(0)

        @pl.when(i == 0)
        def _():
            ride.run(r_ins, r_outs, r_sems, start=True)

        body(*ins, *outs, *scr)

        @pl.when(i == last)
        def _():
            ride.run(r_ins, r_outs, r_sems, start=False)

    return pl.pallas_call(
        riding, in_specs=list(in_specs) + [ANY] * r_in, out_specs=list(out_specs) + [ANY] * r_out,
        out_shape=list(out_shape) + ride.out_shape, scratch_shapes=list(scratch_shapes) + ride.scratch, **common,
    )(*operands, *ride.arrays)


def _sc_fwd(x, first, g, w_in, cw, w_out, *, tm, ride=None):
    seq, d = x.shape
    nt = seq // tm + 1
    nq, _, n = w_in.shape
    width = cw.shape[0]
    past = _past_rows(width)

    def body(x_ref, first_ref, g_ref, win_ref, cw_ref, wout_ref, h1_ref, hh_ref, hh_scr, cbuf):
        i = pl.program_id(0)

        @pl.when(i == 0)
        def _():
            cbuf[pl.ds(0, past), :] = jnp.zeros((past, d), F32)

        h = jnp.where(i == 0, first_ref[...], x_ref[...])
        hn = _rms(h, g_ref[...])[0].astype(MXU_DT)
        for q in range(nq):
            hh_scr[:, q * n:(q + 1) * n] = _dot(hn, win_ref[q])
        hh_ref[...] = hh_scr[...].astype(hh_ref.dtype)
        b = hh_scr[:, 0:d]
        cbuf[pl.ds(past, tm), :] = hh_scr[:, d:2 * d] * hh_scr[:, 2 * d:3 * d]
        last = cbuf[pl.ds(tm, past), :]
        _link_past(cbuf, slice(None), width, tm)
        u = sum(cw_ref[k:k + 1, :] * tap for k, tap in enumerate(_conv_taps(cbuf, slice(None), width, tm)))
        cbuf[pl.ds(0, past), :] = last
        h1_ref[...] = h + _dot((b * u).astype(MXU_DT), wout_ref[...])

    return _launch(
        body, [x, first, g, w_in, cw, w_out], name="sc_fwd", grid=(nt,),
        in_specs=[pl.BlockSpec((tm, d), lambda i: (jnp.maximum(i - 1, 0), 0)), _const((tm, d)), _const((1, d)),
                  _const(w_in.shape), _const(cw.shape), _const(w_out.shape)],
        out_specs=[pl.BlockSpec((tm, d), lambda i: (i, 0)), pl.BlockSpec((tm, nq * n), lambda i: (i, 0))],
        out_shape=[jax.ShapeDtypeStruct((nt * tm, d), F32), jax.ShapeDtypeStruct((nt * tm, nq * n), ACT_DT)],
        scratch_shapes=[pltpu.VMEM((tm, nq * n), F32), pltpu.VMEM((past + tm, d), F32)],
        ride=ride,
    )


def _sc_bwd(dh, hh, cw, w_out, *, tm, ride=None):
    t_len, d = dh.shape
    nt = t_len // tm
    width = cw.shape[0]
    past = _past_rows(width)
    halo_rows, halo_index = _halo_block(past, tm, nt)

    def body(dh_ref, hh_ref, hhp_ref, cw_ref, wout_ref, dhh_ref, z_ref, dcw_ref, cbuf, dbuf):
        i = pl.program_id(0)
        r = nt - 1 - i

        @pl.when(i == 0)
        def _():
            dbuf[pl.ds(tm, past), :] = jnp.zeros((past, d), F32)
            dcw_ref[...] = jnp.zeros_like(dcw_ref)

        b = hh_ref[:, 0:d].astype(F32)
        c = hh_ref[:, d:2 * d].astype(F32)
        v = hh_ref[:, 2 * d:3 * d].astype(F32)
        prev = hhp_ref[...].astype(F32)[halo_rows - past:, :]
        cbuf[pl.ds(0, past), :] = jnp.where(r > 0, prev[:, d:2 * d] * prev[:, 2 * d:3 * d], 0.0)
        cbuf[pl.ds(past, tm), :] = c * v
        _link_past(cbuf, slice(None), width, tm)
        taps = _conv_taps(cbuf, slice(None), width, tm)
        u = sum(cw_ref[k:k + 1, :] * taps[k] for k in range(width))
        z_ref[...] = (b * u).astype(z_ref.dtype)
        dz = _dot_nt(dh_ref[...].astype(MXU_DT), wout_ref[...])
        du = dz * b
        for k in range(width):
            dcw_ref[k:k + 1, :] += jnp.sum(taps[k] * du, axis=0, keepdims=True)
        dbuf[pl.ds(0, tm), :] = du
        _link_future(dbuf, slice(None), width, tm)
        dcv = _conv_back(dbuf, cw_ref, slice(None), width, tm)
        dbuf[pl.ds(tm, past), :] = dbuf[pl.ds(0, past), :]
        dhh_ref[:, 0:d] = (dz * u).astype(dhh_ref.dtype)
        dhh_ref[:, d:2 * d] = (dcv * v).astype(dhh_ref.dtype)
        dhh_ref[:, 2 * d:3 * d] = (dcv * c).astype(dhh_ref.dtype)

    rev = lambda i: (nt - 1 - i, 0)
    return _launch(
        body, [dh, hh, hh, cw, w_out], name="sc_bwd", grid=(nt,),
        in_specs=[pl.BlockSpec((tm, d), rev), pl.BlockSpec((tm, 3 * d), rev),
                  pl.BlockSpec((halo_rows, 3 * d), halo_index), _const(cw.shape), _const(w_out.shape)],
        out_specs=[pl.BlockSpec((tm, 3 * d), rev), pl.BlockSpec((tm, d), rev), _const((F32_ROWS, d))],
        out_shape=[jax.ShapeDtypeStruct((t_len, 3 * d), ACT_DT), jax.ShapeDtypeStruct((t_len, d), ACT_DT),
                   jax.ShapeDtypeStruct((F32_ROWS, d), F32)],
        scratch_shapes=[pltpu.VMEM((past + tm, d), F32), pltpu.VMEM((tm + past, d), F32)],
        ride=ride,
    )


def _ffn_fwd(h, g, w_up, cw, w_down, *, tm, ride=None):
    t_len, d = h.shape
    nt = t_len // tm
    nq, _, n = w_up.shape
    width = cw.shape[0]
    past = _past_rows(width)

    def body(h_ref, g_ref, wup_ref, cw_ref, wdn_ref, out_ref, hu_ref, ubuf, tail):
        i = pl.program_id(0)

        @pl.when(i == 0)
        def _():
            tail[...] = jnp.zeros_like(tail)

        h_in = h_ref[...]
        hn = _rms(h_in, g_ref[...])[0].astype(MXU_DT)
        ubuf[pl.ds(0, past), :] = tail[...]
        for q in range(nq):
            ubuf[pl.ds(past, tm), q * n:(q + 1) * n] = _dot(hn, wup_ref[q])
        hu_ref[...] = ubuf[pl.ds(past, tm), :].astype(hu_ref.dtype)
        tail[...] = ubuf[pl.ds(tm, past), :]
        _link_past(ubuf, slice(None), width, tm)
        acc = h_in
        for j in range(nq // 2):
            gcol, vcol = slice(j * n, (j + 1) * n), slice((nq // 2 + j) * n, (nq // 2 + j + 1) * n)
            conv = lambda cols: sum(cw_ref[k:k + 1, cols] * tap for k, tap in enumerate(_conv_taps(ubuf, cols, width, tm)))
            gj, vj = conv(gcol), conv(vcol)
            acc = acc + _dot((gj * _sigmoid(gj) * vj).astype(MXU_DT), wdn_ref[j * n:(j + 1) * n, :])
        out_ref[...] = acc

    row = lambda i: (i, 0)
    return _launch(
        body, [h, g, w_up, cw, w_down], name="ffn_fwd", grid=(nt,),
        in_specs=[pl.BlockSpec((tm, d), row), _const((1, d)), _const(w_up.shape), _const(cw.shape), _const(w_down.shape)],
        out_specs=[pl.BlockSpec((tm, d), row), pl.BlockSpec((tm, nq * n), row)],
        out_shape=[jax.ShapeDtypeStruct((t_len, d), F32), jax.ShapeDtypeStruct((t_len, nq * n), ACT_DT)],
        scratch_shapes=[pltpu.VMEM((past + tm, nq * n), F32), pltpu.VMEM((past, nq * n), F32)],
        ride=ride,
    )


def _ffn_bwd(dh, hu, cw, w_down, *, tm, ride=None):
    t_len, d = dh.shape
    nt = t_len // tm
    ff = hu.shape[1]
    n = ff // 4
    width = cw.shape[0]
    past = _past_rows(width)
    halo_rows, halo_index = _halo_block(past, tm, nt)

    def body(dh_ref, hu_ref, hup_ref, cw_ref, wdn_ref, a_ref, dhu_ref, dcw_ref, ubuf, dbuf):
        i = pl.program_id(0)
        r = nt - 1 - i

        @pl.when(i == 0)
        def _():
            dbuf[pl.ds(tm, past), :] = jnp.zeros((past, ff), F32)
            dcw_ref[...] = jnp.zeros_like(dcw_ref)

        prev = hup_ref[...].astype(F32)[halo_rows - past:, :]
        ubuf[pl.ds(0, past), :] = jnp.where(r > 0, prev, 0.0)
        ubuf[pl.ds(past, tm), :] = hu_ref[...].astype(F32)
        _link_past(ubuf, slice(None), width, tm)
        dhb = dh_ref[...].astype(MXU_DT)
        conv = lambda cols: sum(cw_ref[k:k + 1, cols] * tap for k, tap in enumerate(_conv_taps(ubuf, cols, width, tm)))
        for j in range(2):
            gcol, vcol = slice(j * n, (j + 1) * n), slice((2 + j) * n, (3 + j) * n)
            gj, vj = conv(gcol), conv(vcol)
            sg = _sigmoid(gj)
            s = gj * sg
            a_ref[:, gcol] = (s * vj).astype(a_ref.dtype)
            da = _dot_nt(dhb, wdn_ref[j * n:(j + 1) * n, :])
            dbuf[pl.ds(0, tm), vcol] = da * s
            dbuf[pl.ds(0, tm), gcol] = da * vj * (sg * (1.0 + gj * (1.0 - sg)))
        _link_future(dbuf, slice(None), width, tm)
        for q in range(4):
            cols = slice(q * n, (q + 1) * n)
            dy = dbuf[pl.ds(0, tm), cols]
            for k, tap in enumerate(_conv_taps(ubuf, cols, width, tm)):
                dcw_ref[k:k + 1, cols] += jnp.sum(tap * dy, axis=0, keepdims=True)
            dhu_ref[:, cols] = _conv_back(dbuf, cw_ref, cols, width, tm).astype(dhu_ref.dtype)
        dbuf[pl.ds(tm, past), :] = dbuf[pl.ds(0, past), :]

    rev = lambda i: (nt - 1 - i, 0)
    return _launch(
        body, [dh, hu, hu, cw, w_down], name="ffn_bwd", grid=(nt,),
        in_specs=[pl.BlockSpec((tm, d), rev), pl.BlockSpec((tm, ff), rev), pl.BlockSpec((halo_rows, ff), halo_index),
                  _const(cw.shape), _const(w_down.shape)],
        out_specs=[pl.BlockSpec((tm, 2 * n), rev), pl.BlockSpec((tm, ff), rev), _const((F32_ROWS, ff))],
        out_shape=[jax.ShapeDtypeStruct((t_len, 2 * n), ACT_DT), jax.ShapeDtypeStruct((t_len, ff), ACT_DT),
                   jax.ShapeDtypeStruct((F32_ROWS, ff), F32)],
        scratch_shapes=[pltpu.VMEM((past + tm, ff), F32), pltpu.VMEM((tm + past, ff), F32)],
        ride=ride,
    )


V_CONV_B, V_B_A, V_B_X, V_LAMBDA = 0, 1, 2, 3
G_CONV_W, G_CONV_B, G_B_A, G_B_X, G_LAMBDA = 0, 4, 5, 6, 7


def _scan(a_ref, b_ref, edge, tm, reverse):
    nj = tm // F32_ROWS
    order = range(nj - 1, -1, -1) if reverse else range(nj)
    slab = lambda ref, j: ref[pl.ds(F32_ROWS * j, F32_ROWS), :]
    a_run = b_run = None
    for j in order:
        a_j, b_j = slab(a_ref, j), slab(b_ref, j)
        if a_run is not None:
            b_j = b_j + a_j * b_run
            a_j = a_j * a_run
            b_ref[pl.ds(F32_ROWS * j, F32_ROWS), :] = b_j
            a_ref[pl.ds(F32_ROWS * j, F32_ROWS), :] = a_j
        a_run, b_run = a_j, b_j
    sub = _sublane()
    shift = 1
    while shift < F32_ROWS:
        amount = F32_ROWS - shift if reverse else shift
        keep = (sub < F32_ROWS - shift) if reverse else (sub >= shift)
        b_run = jnp.where(keep, b_run + a_run * pltpu.roll(b_run, amount, 0), b_run)
        a_run = jnp.where(keep, a_run * pltpu.roll(a_run, amount, 0), a_run)
        shift *= 2
    outer = edge[0:1, :] if reverse else edge[F32_ROWS - 1:F32_ROWS, :]
    ends = b_run + a_run * outer
    if reverse:
        carry = jnp.where(sub == F32_ROWS - 1, outer, pltpu.roll(ends, F32_ROWS - 1, 0))
    else:
        carry = jnp.where(sub == 0, outer, pltpu.roll(ends, 1, 0))
    for j in range(nj):
        b_ref[pl.ds(F32_ROWS * j, F32_ROWS), :] = slab(b_ref, j) + slab(a_ref, j) * carry
    return slab(b_ref, 0 if reverse else nj - 1)


def _rg_gates(u, vec_ref, wa_ref, wx_ref, pre_scr, nb, bd):
    ub = u.astype(MXU_DT)
    for k in range(nb):
        blk = slice(k * bd, (k + 1) * bd)
        pre_scr[0, :, blk] = _dot(ub[:, blk], wa_ref[k])
        pre_scr[1, :, blk] = _dot(ub[:, blk], wx_ref[k])
    r_gate = _sigmoid(pre_scr[0] + vec_ref[V_B_A:V_B_A + 1, :])
    i_gate = _sigmoid(pre_scr[1] + vec_ref[V_B_X:V_B_X + 1, :])
    sp = _softplus(-vec_ref[V_LAMBDA:V_LAMBDA + 1, :])
    log_a = -RG_C * r_gate * sp
    a = jnp.exp(log_a)
    mult = jnp.sqrt(-_expm1_neg(2.0 * log_a))
    return r_gate, i_gate, a, mult, sp, ub


def _rg_fwd(h, g, w_in, cw, vec, wa, wx, w_out, *, tm):
    t_len, d = h.shape
    nt = t_len // tm
    nq, _, n = w_in.shape
    dr = 2 * n
    width = cw.shape[0]
    past = _past_rows(width)
    nb, bd, _ = wa.shape

    def body(h_ref, g_ref, win_ref, cw_ref, vec_ref, wa_ref, wx_ref, wout_ref, out_ref, hh_ref, hs_ref,
             gbuf, rbuf, pre_scr, tail, edge):
        i = pl.program_id(0)

        @pl.when(i == 0)
        def _():
            tail[...] = jnp.zeros_like(tail)
            edge[...] = jnp.zeros_like(edge)

        h_in = h_ref[...]
        hn = _rms(h_in, g_ref[...])[0].astype(MXU_DT)
        rbuf[pl.ds(0, past), :] = tail[...]
        for q in range(2):
            gbuf[:, q * n:(q + 1) * n] = _dot(hn, win_ref[q])
            rbuf[pl.ds(past, tm), q * n:(q + 1) * n] = _dot(hn, win_ref[2 + q])
        hh_ref[:, 0:dr] = gbuf[...].astype(hh_ref.dtype)
        hh_ref[:, dr:2 * dr] = rbuf[pl.ds(past, tm), :].astype(hh_ref.dtype)
        tail[...] = rbuf[pl.ds(tm, past), :]
        _link_past(rbuf, slice(None), width, tm)
        taps = _conv_taps(rbuf, slice(None), width, tm)
        u = sum(cw_ref[k:k + 1, :] * taps[k] for k in range(width)) + vec_ref[V_CONV_B:V_CONV_B + 1, :]
        _, i_gate, a, mult, _, _ = _rg_gates(u, vec_ref, wa_ref, wx_ref, pre_scr, nb, bd)
        pre_scr[0] = a
        pre_scr[1] = jnp.where(_valid_rows(i, tm), mult * (i_gate * u), 0.0)
        edge[...] = _scan(pre_scr.at[0], pre_scr.at[1], edge[...], tm, reverse=False)
        hs = pre_scr[1]
        hs_ref[...] = hs
        y = hs * _gelu(gbuf[...])[0]
        out_ref[...] = h_in + _dot(y.astype(MXU_DT), wout_ref[...])

    row = lambda i: (i, 0)
    return pl.pallas_call(
        body, name="rg_fwd", grid=(nt,),
        in_specs=[pl.BlockSpec((tm, d), row), _const((1, d)), _const(w_in.shape), _const(cw.shape), _const(vec.shape),
                  _const(wa.shape), _const(wx.shape), _const(w_out.shape)],
        out_specs=[pl.BlockSpec((tm, d), row), pl.BlockSpec((tm, 2 * dr), row), pl.BlockSpec((tm, dr), row)],
        out_shape=[jax.ShapeDtypeStruct((t_len, d), F32), jax.ShapeDtypeStruct((t_len, 2 * dr), ACT_DT),
                   jax.ShapeDtypeStruct((t_len, dr), F32)],
        scratch_shapes=[pltpu.VMEM((tm, dr), F32), pltpu.VMEM((past + tm, dr), F32), pltpu.VMEM((2, tm, dr), F32),
                        pltpu.VMEM((past, dr), F32), pltpu.VMEM((F32_ROWS, dr), F32)],
        compiler_params=_params(),
    )(h, g, w_in, cw, vec, wa, wx, w_out)


def _rg_bwd(dh, hh, hs, cw, vec, wa, wx, w_out, *, tm, ride=None):
    t_len, d = dh.shape
    nt = t_len // tm
    dr = hs.shape[1]
    width = cw.shape[0]
    nb, bd, _ = wa.shape
    past = _past_rows(width)
    halo_rows, halo_index = _halo_block(past, tm, nt)
    one = F32_ROWS

    def body(dh_ref, hh_ref, hhp_ref, hs_ref, hsp_ref, cw_ref, vec_ref, wa_ref, wx_ref, wout_ref,
             dhh_ref, y_ref, dvec_ref, dwa_ref, dwx_ref, rbuf, dbuf, pre_scr, hbuf, abuf, edge):
        i = pl.program_id(0)
        r = nt - 1 - i

        @pl.when(i == 0)
        def _():
            dbuf[pl.ds(tm, past), :] = jnp.zeros((past, dr), F32)
            abuf[pl.ds(tm, one), :] = jnp.zeros((one, dr), F32)
            edge[...] = jnp.zeros_like(edge)
            dvec_ref[...] = jnp.zeros_like(dvec_ref)
            dwa_ref[...] = jnp.zeros_like(dwa_ref)
            dwx_ref[...] = jnp.zeros_like(dwx_ref)

        gb = hh_ref[:, 0:dr].astype(F32)
        prev = hhp_ref[...].astype(F32)[halo_rows - past:, dr:2 * dr]
        rbuf[pl.ds(0, past), :] = jnp.where(r > 0, prev, 0.0)
        rbuf[pl.ds(past, tm), :] = hh_ref[:, dr:2 * dr].astype(F32)
        _link_past(rbuf, slice(None), width, tm)
        taps = _conv_taps(rbuf, slice(None), width, tm)
        u = sum(cw_ref[k:k + 1, :] * taps[k] for k in range(width)) + vec_ref[V_CONV_B:V_CONV_B + 1, :]
        r_gate, i_gate, a, mult, sp, ub = _rg_gates(u, vec_ref, wa_ref, wx_ref, pre_scr, nb, bd)
        hs_t = hs_ref[...]
        hbuf[pl.ds(0, one), :] = jnp.where(r > 0, hsp_ref[...], 0.0)
        hbuf[pl.ds(one, tm), :] = hs_t
        _link_past(hbuf, slice(None), 2, tm)
        h_prev = hbuf[pl.ds(0, tm), :]
        gate, th = _gelu(gb)
        y_ref[...] = (hs_t * gate).astype(y_ref.dtype)
        dy = _dot_nt(dh_ref[...].astype(MXU_DT), wout_ref[...])
        abuf[pl.ds(0, tm), :] = a
        _link_future(abuf, slice(None), 2, tm)
        pre_scr[0] = abuf[pl.ds(one, tm), :]
        pre_scr[1] = dy * gate
        edge[...] = _scan(pre_scr.at[0], pre_scr.at[1], edge[...], tm, reverse=True)
        abuf[pl.ds(tm, one), :] = abuf[pl.ds(0, one), :]
        d_hs = pre_scr[1]
        d_b = jnp.where(_valid_rows(r, tm), d_hs, 0.0)
        d_iu = d_b * mult
        d_log_a = d_hs * h_prev * a - d_b * (i_gate * u) * (a * a) / jnp.maximum(mult, 1e-30)
        dvec_ref[G_LAMBDA:G_LAMBDA + 1, :] += jnp.sum(d_log_a * r_gate, axis=0, keepdims=True) * (-RG_C)
        d_pre_r = d_log_a * (-RG_C * sp) * r_gate * (1.0 - r_gate)
        d_pre_i = d_iu * u * i_gate * (1.0 - i_gate)
        dvec_ref[G_B_A:G_B_A + 1, :] += jnp.sum(d_pre_r, axis=0, keepdims=True)
        dvec_ref[G_B_X:G_B_X + 1, :] += jnp.sum(d_pre_i, axis=0, keepdims=True)
        dbuf[pl.ds(0, tm), :] = d_iu * i_gate
        d_pre_r = d_pre_r.astype(MXU_DT)
        d_pre_i = d_pre_i.astype(MXU_DT)
        for k in range(nb):
            blk = slice(k * bd, (k + 1) * bd)
            dwa_ref[k] += _dot_tn(ub[:, blk], d_pre_r[:, blk])
            dwx_ref[k] += _dot_tn(ub[:, blk], d_pre_i[:, blk])
            dbuf[pl.ds(0, tm), blk] += _dot_nt(d_pre_r[:, blk], wa_ref[k]) + _dot_nt(d_pre_i[:, blk], wx_ref[k])
        du = dbuf[pl.ds(0, tm), :]
        dvec_ref[G_CONV_B:G_CONV_B + 1, :] += jnp.sum(du, axis=0, keepdims=True)
        for k in range(width):
            dvec_ref[G_CONV_W + k:G_CONV_W + k + 1, :] += jnp.sum(taps[k] * du, axis=0, keepdims=True)
        _link_future(dbuf, slice(None), width, tm)
        d_rb = _conv_back(dbuf, cw_ref, slice(None), width, tm)
        dbuf[pl.ds(tm, past), :] = dbuf[pl.ds(0, past), :]
        dhh_ref[:, 0:dr] = (dy * hs_t * _gelu_grad(gb, th)).astype(dhh_ref.dtype)
        dhh_ref[:, dr:2 * dr] = d_rb.astype(dhh_ref.dtype)

        @pl.when(i == nt - 1)
        def _():
            lam = vec_ref[V_LAMBDA:V_LAMBDA + 1, :]
            dvec_ref[G_LAMBDA:G_LAMBDA + 1, :] = dvec_ref[G_LAMBDA:G_LAMBDA + 1, :] * (-_sigmoid(-lam))

    rev = lambda i: (nt - 1 - i, 0)
    return _launch(
        body, [dh, hh, hh, hs, hs, cw, vec, wa, wx, w_out], name="rg_bwd", grid=(nt,),
        in_specs=[pl.BlockSpec((tm, d), rev), pl.BlockSpec((tm, 2 * dr), rev), pl.BlockSpec((halo_rows, 2 * dr), halo_index),
                  pl.BlockSpec((tm, dr), rev),
                  pl.BlockSpec((one, dr), lambda i: (jnp.maximum((nt - 1 - i) * (tm // one) - 1, 0), 0)),
                  _const(cw.shape), _const(vec.shape), _const(wa.shape), _const(wx.shape), _const(w_out.shape)],
        out_specs=[pl.BlockSpec((tm, 2 * dr), rev), pl.BlockSpec((tm, dr), rev), _const((F32_ROWS, dr)),
                   _const(wa.shape), _const(wx.shape)],
        out_shape=[jax.ShapeDtypeStruct((t_len, 2 * dr), ACT_DT), jax.ShapeDtypeStruct((t_len, dr), ACT_DT),
                   jax.ShapeDtypeStruct((F32_ROWS, dr), F32), jax.ShapeDtypeStruct(wa.shape, F32),
                   jax.ShapeDtypeStruct(wx.shape, F32)],
        scratch_shapes=[pltpu.VMEM((past + tm, dr), F32), pltpu.VMEM((tm + past, dr), F32), pltpu.VMEM((2, tm, dr), F32),
                        pltpu.VMEM((one + tm, dr), F32), pltpu.VMEM((tm + one, dr), F32), pltpu.VMEM((F32_ROWS, dr), F32)],
        ride=ride,
    )


def _inproj_bwd(dpre, h, dh, w, g, *, tm, first=None):
    t_len, d = dh.shape
    nt = t_len // tm
    nq, _, n = w.shape

    def body(dpre_ref, h_ref, *rest):
        first_ref = rest[0] if first is not None else None
        dh_ref, w_ref, g_ref, out_ref, hn_ref, dg_ref = rest[first is not None:][:6]
        i = pl.program_id(0)

        @pl.when(i == 0)
        def _():
            dg_ref[...] = jnp.zeros_like(dg_ref)

        h_in = h_ref[...] if first is None else jnp.where(i == 0, first_ref[...], h_ref[...])
        gain = g_ref[...]
        hn, xhat, rstd = _rms(h_in, gain)
        hn_ref[...] = hn.astype(hn_ref.dtype)
        dhn = sum(_dot_nt(dpre_ref[:, q * n:(q + 1) * n], w_ref[q]) for q in range(nq))
        dg_ref[0:1, :] += jnp.sum(dhn * xhat, axis=0, keepdims=True)
        dh_in = jnp.where(_valid_rows(i, tm), dh_ref[...] + _rms_bwd(dhn, xhat, rstd, gain), 0.0)
        if first is None:
            out_ref[...] = dh_in
        else:
            dfirst_ref = rest[-1]

            @pl.when(i == 0)
            def _():
                dfirst_ref[...] = dh_in

            @pl.when(i > 0)
            def _():
                out_ref[...] = dh_in

    row = lambda i: (i, 0)
    after_first = lambda i: (jnp.maximum(i - 1, 0), 0)
    operands = [dpre, h] + ([first] if first is not None else []) + [dh, w, g]
    out_specs = [pl.BlockSpec((tm, d), row), pl.BlockSpec((tm, d), row), _const((F32_ROWS, d))]
    out_shape = [jax.ShapeDtypeStruct((t_len, d), F32), jax.ShapeDtypeStruct((t_len, d), ACT_DT),
                 jax.ShapeDtypeStruct((F32_ROWS, d), F32)]
    if first is not None:
        out_specs = [pl.BlockSpec((tm, d), after_first)] + out_specs[1:] + [_const((tm, d))]
        out_shape = [jax.ShapeDtypeStruct((t_len - tm, d), F32)] + out_shape[1:] + [jax.ShapeDtypeStruct((tm, d), F32)]
    return pl.pallas_call(
        body, name="inproj_bwd", grid=(nt,),
        in_specs=[pl.BlockSpec((tm, nq * n), row), pl.BlockSpec((tm, d), row if first is None else after_first)]
        + ([_const((tm, d))] if first is not None else [])
        + [pl.BlockSpec((tm, d), row), _const(w.shape), _const((1, d))],
        out_specs=out_specs, out_shape=out_shape, compiler_params=_params(),
    )(*operands)


def _weight_grad(a, b, nb, *, rows):
    t_len, k_dim = a.shape
    n = b.shape[1] // nb
    nt = t_len // rows

    def body(a_ref, b_ref, out_ref, wire_ref):
        @pl.when(pl.program_id(1) == 0)
        def _():
            out_ref[...] = jnp.zeros_like(out_ref)

        out_ref[0] += _dot_tn(a_ref[...].astype(MXU_DT), b_ref[...].astype(MXU_DT))

        @pl.when(pl.program_id(1) == nt - 1)
        def _():
            wire_ref[...] = out_ref[...].astype(wire_ref.dtype)

    block = pl.BlockSpec((1, k_dim, n), lambda j, i: (j, 0, 0))
    return pl.pallas_call(
        body, name="weight_grad", grid=(nb, nt),
        in_specs=[pl.BlockSpec((rows, k_dim), lambda j, i: (i, 0)), pl.BlockSpec((rows, n), lambda j, i: (i, j))],
        out_specs=[block, block],
        out_shape=[jax.ShapeDtypeStruct((nb, k_dim, n), F32), jax.ShapeDtypeStruct((nb, k_dim, n), WIRE_DT)],
        compiler_params=_params(2),
    )(a, b)


def _loss_head(h, target, g, *, tm):
    t_len, d = h.shape
    nt = t_len // tm

    def body(h_ref, t_ref, g_ref, dh_ref, sq_ref, dg_ref):
        i = pl.program_id(0)

        @pl.when(i == 0)
        def _():
            sq_ref[...] = jnp.zeros_like(sq_ref)
            dg_ref[...] = jnp.zeros_like(dg_ref)
            dh_ref[...] = jnp.zeros_like(dh_ref)

        @pl.when(i > 0)
        def _():
            gain = g_ref[...]
            out, xhat, rstd = _rms(h_ref[...], gain)
            err = out - t_ref[...]
            sq_ref[0:1, :] += jnp.sum(err * err, axis=0, keepdims=True)
            dout = err * (1.0 / d)
            dg_ref[0:1, :] += jnp.sum(dout * xhat, axis=0, keepdims=True)
            dh_ref[...] = _rms_bwd(dout, xhat, rstd, gain)

    row = lambda i: (i, 0)
    return pl.pallas_call(
        body, name="loss_head", grid=(nt,),
        in_specs=[pl.BlockSpec((tm, d), row), pl.BlockSpec((tm, d), lambda i: (jnp.maximum(i - 1, 0), 0)), _const((1, d))],
        out_specs=[pl.BlockSpec((tm, d), row), _const((F32_ROWS, d)), _const((F32_ROWS, d))],
        out_shape=[jax.ShapeDtypeStruct((t_len, d), F32), jax.ShapeDtypeStruct((F32_ROWS, d), F32),
                   jax.ShapeDtypeStruct((F32_ROWS, d), F32)],
        compiler_params=_params(),
    )(h, target, g)


def _adamw(w, m, v, parts, *, rows, layer=0, into=None):
    n_layers, n_rows, n_cols = w.shape
    nt = n_rows // rows
    n_parts = len(parts)

    def body(w_ref, m_ref, v_ref, *rest):
        part_refs, (g_ref, d_ref, nm_ref, nv_ref) = rest[:n_parts], rest[-4:]
        w_ref, m_ref, v_ref, g_ref, d_ref, nm_ref, nv_ref = (r.at[0] for r in (w_ref, m_ref, v_ref, g_ref, d_ref, nm_ref, nv_ref))
        grad = part_refs[0][...].astype(F32)
        for p in part_refs[1:]:
            grad = grad + p[...].astype(F32)
        new_m = ADAM_B1 * m_ref[...] + (1.0 - ADAM_B1) * grad
        new_v = ADAM_B2 * v_ref[...] + (1.0 - ADAM_B2) * (grad * grad)
        m_hat = new_m / (1.0 - ADAM_B1 ** ADAM_STEP)
        v_hat = new_v / (1.0 - ADAM_B2 ** ADAM_STEP)
        g_ref[...] = grad
        d_ref[...] = -ADAM_LR * (m_hat / (jnp.sqrt(v_hat) + ADAM_EPS) + ADAM_WD * w_ref[...])
        nm_ref[...] = new_m
        nv_ref[...] = new_v

    spec = pl.BlockSpec((rows, n_cols), lambda i: (i, 0))
    layer_spec = pl.BlockSpec((1, rows, n_cols), lambda i: (layer, i, 0))
    into = list(into or [])
    return pl.pallas_call(
        body, name="adamw", grid=(nt,),
        in_specs=[layer_spec] * 3 + [spec] * n_parts + [ANY] * len(into), out_specs=[layer_spec] * 4,
        out_shape=[jax.ShapeDtypeStruct(w.shape, F32)] * 4,
        input_output_aliases={3 + n_parts + k: k for k in range(len(into))},
        compiler_params=_params(),
    )(w, m, v, *parts, *into)


def _sum_parts(own, recv, *, rows):
    n_rows, n_cols = own.shape
    n_recv = recv.shape[0]

    def body(own_ref, recv_ref, out_ref):
        acc = own_ref[...].astype(F32)
        for j in range(n_recv):
            acc = acc + recv_ref[j].astype(F32)
        out_ref[...] = acc

    return pl.pallas_call(
        body, name="sum_parts", grid=(n_rows // rows,),
        in_specs=[pl.BlockSpec((rows, n_cols), lambda i: (i, 0)), pl.BlockSpec((n_recv, rows, n_cols), lambda i: (0, i, 0))],
        out_specs=pl.BlockSpec((rows, n_cols), lambda i: (i, 0)),
        out_shape=jax.ShapeDtypeStruct(own.shape, F32),
        compiler_params=_params(),
    )(own, recv)


def _swap_cores(arrays):
    nk = len(arrays)

    def body(*refs):
        ins, outs, (send_sems, recv_sems) = refs[:nk], refs[nk:2 * nk], refs[2 * nk:]
        x, y, c = _place()
        sends = [pltpu.make_async_remote_copy(
            src_ref=ins[k], dst_ref=outs[k], send_sem=send_sems.at[k], recv_sem=recv_sems.at[k],
            device_id=(x, y, 1 - c), device_id_type=MESH_ID) for k in range(nk)]
        for cp in sends:
            cp.start()
        for cp in sends:
            cp.wait_recv()
        for cp in sends:
            cp.wait_send()

    return pl.pallas_call(
        body, name="swap_cores", in_specs=[ANY] * nk, out_specs=[ANY] * nk,
        out_shape=[jax.ShapeDtypeStruct(a.shape, a.dtype) for a in arrays],
        scratch_shapes=[pltpu.SemaphoreType.DMA((nk,)), pltpu.SemaphoreType.DMA((nk,))],
    )(*arrays)


def _gather_devices(v):
    def body(v_ref, out_ref, send_sems, recv_sems, local_sem):
        x, y, c = _place()
        local = pltpu.make_async_copy(v_ref, out_ref.at[4 * x + 2 * y + c], local_sem)
        local.start()
        sends = []
        for flip in range(1, 8):
            px, py, pc = x ^ (flip >> 2), y ^ ((flip >> 1) & 1), c ^ (flip & 1)
            sends.append(pltpu.make_async_remote_copy(
                src_ref=v_ref, dst_ref=out_ref.at[4 * x + 2 * y + c], send_sem=send_sems.at[flip - 1],
                recv_sem=recv_sems.at[flip - 1], device_id=(px, py, pc), device_id_type=MESH_ID))
            sends[-1].start()
        for flip in range(1, 8):
            px, py, pc = x ^ (flip >> 2), y ^ ((flip >> 1) & 1), c ^ (flip & 1)
            pltpu.make_async_remote_copy(
                src_ref=v_ref, dst_ref=out_ref.at[4 * px + 2 * py + pc], send_sem=send_sems.at[flip - 1],
                recv_sem=recv_sems.at[flip - 1], device_id=(px, py, pc), device_id_type=MESH_ID).wait_recv()
        for cp in sends:
            cp.wait_send()
        local.wait()

    return pl.pallas_call(
        body, name="gather_devices", in_specs=[ANY], out_specs=ANY,
        out_shape=jax.ShapeDtypeStruct((8,) + v.shape, v.dtype),
        scratch_shapes=[pltpu.SemaphoreType.DMA((7,)), pltpu.SemaphoreType.DMA((7,)), pltpu.SemaphoreType.DMA],
    )(v)


def _pack(arrays, pad_rows=F32_ROWS):
    flat = jnp.concatenate([a.reshape(-1).astype(F32) for a in arrays])
    rows = -(-flat.shape[0] // (LANES * pad_rows)) * pad_rows
    return jnp.pad(flat, (0, rows * LANES - flat.shape[0])).reshape(rows, LANES)


def _unpack(packed, shapes):
    flat, out, off = packed.reshape(-1), [], 0
    for s in shapes:
        size = 1
        for dim in s:
            size *= dim
        out.append(flat[off:off + size].reshape(s))
        off += size
    return out


def _divisor_rows(n_rows, most=256):
    best = None
    for r in range(ACT_ROWS, most + 1, ACT_ROWS):
        if n_rows % r == 0:
            best = r
    return best or n_rows


def kernel(x, meta_tokens, norm_mix_g, norm_ffn_g, final_norm_g, sc_w_in, sc_conv_w, sc_w_out, rg_w_in, rg_conv_w, rg_conv_b, rg_w_gate_a, rg_b_gate_a, rg_w_gate_x, rg_b_gate_x, rg_lambda, rg_w_out, ffn_w_up, ffn_conv_w, ffn_w_down, loss_target, m_meta_tokens, m_norm_mix_g, m_norm_ffn_g, m_final_norm_g, m_sc_w_in, m_sc_conv_w, m_sc_w_out, m_rg_w_in, m_rg_conv_w, m_rg_conv_b, m_rg_w_gate_a, m_rg_b_gate_a, m_rg_w_gate_x, m_rg_b_gate_x, m_rg_lambda, m_rg_w_out, m_ffn_w_up, m_ffn_conv_w, m_ffn_w_down, v_meta_tokens, v_norm_mix_g, v_norm_ffn_g, v_final_norm_g, v_sc_w_in, v_sc_conv_w, v_sc_w_out, v_rg_w_in, v_rg_conv_w, v_rg_conv_b, v_rg_w_gate_a, v_rg_b_gate_a, v_rg_w_gate_x, v_rg_b_gate_x, v_rg_lambda, v_rg_w_out, v_ffn_w_up, v_ffn_conv_w, v_ffn_w_down):
    weights = dict(meta_tokens=meta_tokens, norm_mix_g=norm_mix_g, norm_ffn_g=norm_ffn_g, final_norm_g=final_norm_g, sc_w_in=sc_w_in, sc_conv_w=sc_conv_w, sc_w_out=sc_w_out, rg_w_in=rg_w_in, rg_conv_w=rg_conv_w, rg_conv_b=rg_conv_b, rg_w_gate_a=rg_w_gate_a, rg_b_gate_a=rg_b_gate_a, rg_w_gate_x=rg_w_gate_x, rg_b_gate_x=rg_b_gate_x, rg_lambda=rg_lambda, rg_w_out=rg_w_out, ffn_w_up=ffn_w_up, ffn_conv_w=ffn_conv_w, ffn_w_down=ffn_w_down)
    m_in = dict(meta_tokens=m_meta_tokens, norm_mix_g=m_norm_mix_g, norm_ffn_g=m_norm_ffn_g, final_norm_g=m_final_norm_g, sc_w_in=m_sc_w_in, sc_conv_w=m_sc_conv_w, sc_w_out=m_sc_w_out, rg_w_in=m_rg_w_in, rg_conv_w=m_rg_conv_w, rg_conv_b=m_rg_conv_b, rg_w_gate_a=m_rg_w_gate_a, rg_b_gate_a=m_rg_b_gate_a, rg_w_gate_x=m_rg_w_gate_x, rg_b_gate_x=m_rg_b_gate_x, rg_lambda=m_rg_lambda, rg_w_out=m_rg_w_out, ffn_w_up=m_ffn_w_up, ffn_conv_w=m_ffn_conv_w, ffn_w_down=m_ffn_w_down)
    v_in = dict(meta_tokens=v_meta_tokens, norm_mix_g=v_norm_mix_g, norm_ffn_g=v_norm_ffn_g, final_norm_g=v_final_norm_g, sc_w_in=v_sc_w_in, sc_conv_w=v_sc_conv_w, sc_w_out=v_sc_w_out, rg_w_in=v_rg_w_in, rg_conv_w=v_rg_conv_w, rg_conv_b=v_rg_conv_b, rg_w_gate_a=v_rg_w_gate_a, rg_b_gate_a=v_rg_b_gate_a, rg_w_gate_x=v_rg_w_gate_x, rg_b_gate_x=v_rg_b_gate_x, rg_lambda=v_rg_lambda, rg_w_out=v_rg_w_out, ffn_w_up=v_ffn_w_up, ffn_conv_w=v_ffn_conv_w, ffn_w_down=v_ffn_w_down)
    names = list(weights)

    seq, d = x.shape[1:]
    tm = _row_tile(seq)
    tokens, target = _tile_order(x[0], tm), _tile_order(loss_target[0], tm)
    t_len = seq + tm
    wg_rows = 5 * tm if t_len % (5 * tm) == 0 else tm
    xi, yi, _ = _place()
    chip = 2 * xi + yi
    mesh_axes = ("x", "y", "c")

    wire = lambda w: w.astype(WIRE_DT)
    small_sharded = ["meta_tokens", "sc_conv_w", "rg_conv_w", "rg_conv_b", "rg_b_gate_a", "rg_b_gate_x", "rg_lambda", "ffn_conv_w"]
    small_2d = {n: weights[n].reshape(-1, weights[n].shape[-1]) for n in small_sharded}
    w_sc_in, w_sc_out, small_by_chip = _exchange(
        _Gather([wire(sc_w_in[0]), wire(sc_w_out[0]), _pack([small_2d[n] for n in small_sharded])]), "gather_first")
    w_sc_out = w_sc_out.reshape(-1, d)
    gather_ffn0 = _Gather([wire(ffn_w_up[0]), wire(ffn_w_down[0])])
    gather_rest = _Gather([wire(rg_w_in[0]), wire(rg_w_out[0]), wire(ffn_w_up[1]), wire(ffn_w_down[1])])
    small_len = sum(a.size for a in small_2d.values())
    by_chip = small_by_chip.reshape(N_CHIPS, -1)[:, :small_len]
    full, off = {}, 0
    for n in small_sharded:
        rows, width = small_2d[n].shape
        full[n] = by_chip[:, off:off + rows * width].reshape(N_CHIPS, rows, width).transpose(1, 0, 2).reshape(rows, N_CHIPS * width)
        off += rows * width
    sc_cw, rg_cw = full["sc_conv_w"], full["rg_conv_w"]
    ffn_cw = [full["ffn_conv_w"][0:3], full["ffn_conv_w"][3:6]]
    d_rnn = rg_cw.shape[1]
    vec = jnp.concatenate([full["rg_conv_b"], full["rg_b_gate_a"], full["rg_b_gate_x"], full["rg_lambda"],
                           jnp.zeros((F32_ROWS - 4, d_rnn), F32)])
    wa, wx = rg_w_gate_a[0].astype(MXU_DT), rg_w_gate_x[0].astype(MXU_DT)
    first = _tile_order(jnp.concatenate([jnp.zeros((tm - N_META, d), F32), full["meta_tokens"]]), tm)
    g_mix = [norm_mix_g[0:1], norm_mix_g[1:2]]
    g_ffn = [norm_ffn_g[0:1], norm_ffn_g[1:2]]

    h1, hh0, w_up0, w_dn0 = _sc_fwd(tokens, first, g_mix[0], w_sc_in, sc_cw, w_sc_out, tm=tm, ride=gather_ffn0)
    h2, hu0, w_rg_in, w_rg_out, w_up1, w_dn1 = _ffn_fwd(h1, g_ffn[0], w_up0, ffn_cw[0], w_dn0.reshape(-1, d), tm=tm,
                                                         ride=gather_rest)
    w_up, w_dn, w_rg_out = [w_up0, w_up1], [w_dn0.reshape(-1, d), w_dn1.reshape(-1, d)], w_rg_out.reshape(-1, d)
    h3, hh1, hs = _rg_fwd(h2, g_mix[1], w_rg_in, rg_cw, vec, wa, wx, w_rg_out, tm=tm)
    h4, hu1 = _ffn_fwd(h3, g_ffn[1], w_up[1], ffn_cw[1], w_dn[1], tm=tm)
    dh4, sq, d_final = _loss_head(h4, target, final_norm_g.reshape(1, d), tm=tm)
    loss = lax.psum(jnp.sum(sq[0]) * (0.5 / d), mesh_axes)

    def by_chip_rows(pair):
        return [p.reshape(N_CHIPS, -1, d) for p in pair]

    def ffn_backward(dh_out, h_in, hu, layer, ride):
        act, dhu, dcw, *landed = _ffn_bwd(dh_out, hu, ffn_cw[layer], w_dn[layer], tm=tm, ride=ride)
        dh_in, hn, dg = _inproj_bwd(dhu, h_in, dh_out, w_up[layer], g_ffn[layer], tm=tm)
        d_up = _weight_grad(hn, dhu, N_CHIPS, rows=wg_rows)
        d_dn = by_chip_rows(_weight_grad(act, dh_out, 1, rows=wg_rows))
        return dh_in, d_up, d_dn, dcw[0:3], dg[0], landed

    dh3, d_up1, d_dn1, d_fcw1, d_gf1, _ = ffn_backward(dh4, h3, hu1, 1, None)
    dhh1, y_rg, d_vec, d_wa, d_wx, *landed_ffn1 = _rg_bwd(dh3, hh1, hs, rg_cw, vec, wa, wx, w_rg_out, tm=tm,
                                                          ride=_Scatter([d_up1[1], d_dn1[1]]))
    dh2, hn_rg, d_gm1 = _inproj_bwd(dhh1, h2, dh3, w_rg_in, g_mix[1], tm=tm)
    d_rg_in = _weight_grad(hn_rg, dhh1, N_CHIPS, rows=wg_rows)
    d_rg_out = by_chip_rows(_weight_grad(y_rg, dh3, 1, rows=wg_rows))
    dh1, d_up0, d_dn0, d_fcw0, d_gf0, landed_rg = ffn_backward(dh2, h1, hu0, 0, _Scatter([d_rg_in[1], d_rg_out[1]]))
    dhh0, z_sc, d_sccw, *landed_ffn0 = _sc_bwd(dh1, hh0, sc_cw, w_sc_out, tm=tm, ride=_Scatter([d_up0[1], d_dn0[1]]))
    grad_x, hn_sc, d_gm0, d_first = _inproj_bwd(dhh0, tokens, dh1, w_sc_in, g_mix[0], tm=tm, first=first)
    d_sc_in = _weight_grad(hn_sc, dhh0, N_CHIPS, rows=wg_rows)
    d_sc_out = by_chip_rows(_weight_grad(z_sc, dh1, 1, rows=wg_rows))
    landed_sc = _exchange(_Scatter([d_sc_in[1], d_sc_out[1]]), "scatter_last")
    grad_x = _time_order(grad_x, tm)[None]

    big = [("sc_w_in", 0, d_sc_in, landed_sc[0]), ("sc_w_out", 0, d_sc_out, landed_sc[1]),
           ("rg_w_in", 0, d_rg_in, landed_rg[0]), ("rg_w_out", 0, d_rg_out, landed_rg[1]),
           ("ffn_w_up", 0, d_up0, landed_ffn0[0]), ("ffn_w_up", 1, d_up1, landed_ffn1[0]),
           ("ffn_w_down", 0, d_dn0, landed_ffn0[1]), ("ffn_w_down", 1, d_dn1, landed_ffn1[1])]
    core_sum = []
    for _, _, (partial, _), received in big:
        own = lax.dynamic_index_in_dim(partial, chip, 0, keepdims=False)
        core_sum.append(_sum_parts(own, received, rows=_divisor_rows(own.shape[0])))
    other_sum = _swap_cores(core_sum)
    out = {k: {} for k in ("grad", "delta", "m", "v")}
    stacked = {}
    for (n, layer, _, _), mine, theirs in zip(big, core_sum, other_sum):
        stacked[n] = _adamw(weights[n], m_in[n], v_in[n], [mine, theirs], rows=_divisor_rows(mine.shape[0]), layer=layer,
                            into=stacked.get(n))
    for n, res in stacked.items():
        for k, key in enumerate(("grad", "delta", "m", "v")):
            out[key][n] = res[k]

    d_meta = _time_order(d_first, tm)[tm - N_META:]
    small_full = {"meta_tokens": d_meta, "sc_conv_w": d_sccw[0:3], "rg_conv_w": d_vec[G_CONV_W:G_CONV_W + 4],
                  "rg_conv_b": d_vec[G_CONV_B:G_CONV_B + 1], "rg_b_gate_a": d_vec[G_B_A:G_B_A + 1],
                  "rg_b_gate_x": d_vec[G_B_X:G_B_X + 1], "rg_lambda": d_vec[G_LAMBDA:G_LAMBDA + 1],
                  "ffn_conv_w": jnp.concatenate([d_fcw0, d_fcw1])}
    replicated = {"norm_mix_g": jnp.stack([d_gm0[0], d_gm1[0]]), "norm_ffn_g": jnp.stack([d_gf0, d_gf1]),
                  "final_norm_g": d_final[0], "rg_w_gate_a": d_wa[None], "rg_w_gate_x": d_wx[None]}
    small_names = small_sharded + list(replicated)
    partial_small = [small_full[n] for n in small_sharded] + [replicated[n] for n in replicated]
    packed = _pack(partial_small)
    by_device = _gather_devices(packed)
    total = _sum_parts(by_device[0], by_device[1:], rows=packed.shape[0])
    summed = dict(zip(small_names, _unpack(total, [p.shape for p in partial_small])))
    grads = {}
    for n in small_sharded:
        width = small_2d[n].shape[1]
        grads[n] = lax.dynamic_slice_in_dim(summed[n], chip * width, width, axis=1).reshape(weights[n].shape)
    for n in replicated:
        grads[n] = summed[n].reshape(weights[n].shape)
    shapes = [weights[n].shape for n in small_names]
    packed_w = _pack([weights[n] for n in small_names])
    res = _adamw(packed_w[None], _pack([m_in[n] for n in small_names])[None], _pack([v_in[n] for n in small_names])[None],
                 [_pack([grads[n] for n in small_names])], rows=packed_w.shape[0])
    for k, key in enumerate(("grad", "delta", "m", "v")):
        out[key].update(dict(zip(small_names, _unpack(res[k][0], shapes))))

    return (loss, grad_x, *[out["grad"][n] for n in names], *[out["delta"][n] for n in names],
            *[out["m"][n] for n in names], *[out["v"][n] for n in names])
```

```python
import functools

import jax
import jax.numpy as jnp
from jax import lax
from jax.experimental import pallas as pl
from jax.experimental.pallas import tpu as pltpu

F32 = jnp.float32
MXU_DT = jnp.bfloat16
ACT_DT = jnp.bfloat16
WIRE_DT = jnp.bfloat16
MESH_ID = pl.DeviceIdType.MESH

N_META = 16
RMS_EPS = 1e-6
RG_C = 8.0
ADAM_LR, ADAM_B1, ADAM_B2, ADAM_EPS, ADAM_WD, ADAM_STEP = 0.001, 0.9, 0.999, 1e-08, 0.01, 10
N_CHIPS = 4
VMEM_LIMIT = 60 * 1024 * 1024
F32_ROWS = 8
ACT_ROWS = 16
LANES = 128


def _row_tile(seq):
    for tm in (256, 128, 64, 32, 16):
        if seq % tm == 0:
            return tm
    raise ValueError(f"sequence length {seq} is not a multiple of 16")


def _params(n_axes=1, **kw):
    return pltpu.CompilerParams(dimension_semantics=("arbitrary",) * n_axes, vmem_limit_bytes=VMEM_LIMIT, **kw)


def _const(shape):
    return pl.BlockSpec(shape, lambda *_: (0,) * len(shape), pipeline_mode=pl.Buffered(1))


def _dot(a, b):
    return jnp.dot(a, b, preferred_element_type=F32)


def _dot_nt(a, b):
    return lax.dot_general(a, b, (((1,), (1,)), ((), ())), preferred_element_type=F32)


def _dot_tn(a, b):
    return lax.dot_general(a, b, (((0,), (0,)), ((), ())), preferred_element_type=F32)


def _sigmoid(x):
    return 1.0 / (1.0 + jnp.exp(-x))


def _rms(h, g):
    rstd = lax.rsqrt(jnp.mean(h * h, axis=-1, keepdims=True) + RMS_EPS)
    xhat = h * rstd
    return xhat * g, xhat, rstd


def _rms_bwd(dhn, xhat, rstd, g):
    dx = dhn * g
    return rstd * (dx - xhat * jnp.mean(dx * xhat, axis=-1, keepdims=True))


def _gelu(x):
    k = 0.7978845608028654
    t = jnp.tanh(k * (x + 0.044715 * x * x * x))
    return 0.5 * x * (1.0 + t), t


def _gelu_grad(x, t):
    k = 0.7978845608028654
    return 0.5 * (1.0 + t) + 0.5 * x * (1.0 - t * t) * k * (1.0 + 3 * 0.044715 * x * x)


def _softplus(x):
    e = jnp.exp(-jnp.abs(x))
    return jnp.maximum(x, 0.0) + jnp.where(e < 1e-4, e - 0.5 * e * e, jnp.log(1.0 + e))


def _expm1_neg(z):
    series = z * (1.0 + z * (0.5 + z * (1.0 / 6 + z * (1.0 / 24 + z * (1.0 / 120)))))
    return jnp.where(z > -0.1, series, jnp.exp(z) - 1.0)


def _tile_order(a, tm):
    return a.reshape(-1, F32_ROWS, tm // F32_ROWS, a.shape[-1]).swapaxes(1, 2).reshape(a.shape)


def _time_order(a, tm):
    return a.reshape(-1, tm // F32_ROWS, F32_ROWS, a.shape[-1]).swapaxes(1, 2).reshape(a.shape)


def _valid_rows(tile, tm):
    row = lax.broadcasted_iota(jnp.int32, (tm, 1), 0)
    time = (row & (F32_ROWS - 1)) * (tm // F32_ROWS) + (row >> 3) + tile * tm
    return time >= tm - N_META


def _sublane():
    return lax.broadcasted_iota(jnp.int32, (F32_ROWS, 1), 0)


def _past_rows(width):
    return (width - 1) * F32_ROWS


def _halo_block(past, tm, nt):
    rows = -(-past // ACT_ROWS) * ACT_ROWS
    return rows, lambda i: (jnp.maximum((nt - 1 - i) * (tm // rows) - 1, 0), 0)


def _link_past(buf, cols, width, tm):
    past = _past_rows(width)
    for k in range(1, width):
        rows = pl.ds(past - F32_ROWS * k, F32_ROWS)
        before = pltpu.roll(buf[rows, cols], 1, 0)
        mine = pltpu.roll(buf[pl.ds(past + tm - F32_ROWS * k, F32_ROWS), cols], 1, 0)
        buf[rows, cols] = jnp.where(_sublane() == 0, before, mine)


def _link_future(buf, cols, width, tm):
    for k in range(1, width):
        rows = pl.ds(tm + F32_ROWS * (k - 1), F32_ROWS)
        after = pltpu.roll(buf[rows, cols], F32_ROWS - 1, 0)
        mine = pltpu.roll(buf[pl.ds(F32_ROWS * (k - 1), F32_ROWS), cols], F32_ROWS - 1, 0)
        buf[rows, cols] = jnp.where(_sublane() == F32_ROWS - 1, after, mine)


def _conv_taps(buf, cols, width, tm):
    return [buf[pl.ds(F32_ROWS * k, tm), cols] for k in range(width)]


def _conv_back(buf, cw_ref, cols, width, tm):
    return sum(cw_ref[k:k + 1, cols] * buf[pl.ds(F32_ROWS * (width - 1 - k), tm), cols] for k in range(width))


ANY = pl.BlockSpec(memory_space=pl.ANY)


def _place():
    return lax.axis_index("x"), lax.axis_index("y"), lax.axis_index("c")


def _other_chips(x, y):
    return [(1 - x, y), (x, 1 - y), (1 - x, 1 - y)]


class _Gather:
    def __init__(self, shards):
        nk = len(shards)
        self.arrays = list(shards)
        self.out_shape = [jax.ShapeDtypeStruct((N_CHIPS,) + s.shape, s.dtype) for s in shards]
        self.scratch = [pltpu.SemaphoreType.DMA((nk, 3)), pltpu.SemaphoreType.DMA((nk, 3)), pltpu.SemaphoreType.DMA((nk,))]

    def run(self, ins, outs, sems, start):
        send_sems, recv_sems, local_sems = sems
        x, y, c = _place()
        mine = 2 * x + y
        for k in range(len(ins)):
            local = pltpu.make_async_copy(ins[k], outs[k].at[mine], local_sems.at[k])
            local.start() if start else local.wait()
            for j, (px, py) in enumerate(_other_chips(x, y)):
                sems_kj = dict(send_sem=send_sems.at[k, j], recv_sem=recv_sems.at[k, j], device_id=(px, py, c),
                               device_id_type=MESH_ID)
                send = pltpu.make_async_remote_copy(src_ref=ins[k], dst_ref=outs[k].at[mine], **sems_kj)
                if start:
                    send.start()
                else:
                    pltpu.make_async_remote_copy(src_ref=ins[k], dst_ref=outs[k].at[2 * px + py], **sems_kj).wait_recv()
                    send.wait_send()


class _Scatter:
    def __init__(self, parts):
        nk = len(parts)
        self.arrays = list(parts)
        self.out_shape = [jax.ShapeDtypeStruct((3,) + p.shape[1:], p.dtype) for p in parts]
        self.scratch = [pltpu.SemaphoreType.DMA((nk, 3)), pltpu.SemaphoreType.DMA((nk, 3))]

    def run(self, ins, outs, sems, start):
        send_sems, recv_sems = sems
        x, y, c = _place()
        for k in range(len(ins)):
            for j, (px, py) in enumerate(_other_chips(x, y)):
                send = pltpu.make_async_remote_copy(
                    src_ref=ins[k].at[2 * px + py], dst_ref=outs[k].at[j], send_sem=send_sems.at[k, j],
                    recv_sem=recv_sems.at[k, j], device_id=(px, py, c), device_id_type=MESH_ID)
                if start:
                    send.start()
                else:
                    send.wait_recv()
                    send.wait_send()


def _exchange(ride, name):
    n_in, n_out = len(ride.arrays), len(ride.out_shape)

    def body(*refs):
        ride.run(refs[:n_in], refs[n_in:n_in + n_out], refs[n_in + n_out:], start=True)
        ride.run(refs[:n_in], refs[n_in:n_in + n_out], refs[n_in + n_out:], start=False)

    return pl.pallas_call(body, name=name, in_specs=[ANY] * n_in, out_specs=[ANY] * n_out, out_shape=ride.out_shape,
                          scratch_shapes=ride.scratch)(*ride.arrays)


def _launch(body, operands, *, name, grid, in_specs, out_specs, out_shape, scratch_shapes=(), ride=None):
    common = dict(name=name, grid=grid, compiler_params=_params(len(grid)))
    if ride is None:
        return pl.pallas_call(body, in_specs=in_specs, out_specs=out_specs, out_shape=out_shape,
                              scratch_shapes=list(scratch_shapes), **common)(*operands)
    n_in, n_out, n_scr = len(operands), len(out_shape), len(scratch_shapes)
    r_in, r_out = len(ride.arrays), len(ride.out_shape)
    last = grid[0] - 1

    def riding(*refs):
        ins, refs = refs[:n_in], refs[n_in:]
        r_ins, refs = refs[:r_in], refs[r_in:]
        outs, refs = refs[:n_out], refs[n_out:]
        r_outs, refs = refs[:r_out], refs[r_out:]
        scr, r_sems = refs[:n_scr], refs[n_scr:]
        i = pl.program_id(0)

        @pl.when(i == 0)
        def _():
            ride.run(r_ins, r_outs, r_sems, start=True)

        body(*ins, *outs, *scr)

        @pl.when(i == last)
        def _():
            ride.run(r_ins, r_outs, r_sems, start=False)

    return pl.pallas_call(
        riding, in_specs=list(in_specs) + [ANY] * r_in, out_specs=list(out_specs) + [ANY] * r_out,
        out_shape=list(out_shape) + ride.out_shape, scratch_shapes=list(scratch_shapes) + ride.scratch, **common,
    )(*operands, *ride.arrays)


def _sc_fwd(x, first, g, w_in, cw, w_out, *, tm, ride=None):
    seq, d = x.shape
    nt = seq // tm + 1
    nq, _, n = w_in.shape
    width = cw.shape[0]
    past = _past_rows(width)

    def body(x_ref, first_ref, g_ref, win_ref, cw_ref, wout_ref, h1_ref, hh_ref, hh_scr, cbuf):
        i = pl.program_id(0)

        @pl.when(i == 0)
        def _():
            cbuf[pl.ds(0, past), :] = jnp.zeros((past, d), F32)

        h = jnp.where(i == 0, first_ref[...], x_ref[...])
        hn = _rms(h, g_ref[...])[0].astype(MXU_DT)
        for q in range(nq):
            hh_scr[:, q * n:(q + 1) * n] = _dot(hn, win_ref[q])
        hh_ref[...] = hh_scr[...].astype(hh_ref.dtype)
        b = hh_scr[:, 0:d]
        cbuf[pl.ds(past, tm), :] = hh_scr[:, d:2 * d] * hh_scr[:, 2 * d:3 * d]
        last = cbuf[pl.ds(tm, past), :]
        _link_past(cbuf, slice(None), width, tm)
        u = sum(cw_ref[k:k + 1, :] * tap for k, tap in enumerate(_conv_taps(cbuf, slice(None), width, tm)))
        cbuf[pl.ds(0, past), :] = last
        h1_ref[...] = h + _dot((b * u).astype(MXU_DT), wout_ref[...])

    return _launch(
        body, [x, first, g, w_in, cw, w_out], name="sc_fwd", grid=(nt,),
        in_specs=[pl.BlockSpec((tm, d), lambda i: (jnp.maximum(i - 1, 0), 0)), _const((tm, d)), _const((1, d)),
                  _const(w_in.shape), _const(cw.shape), _const(w_out.shape)],
        out_specs=[pl.BlockSpec((tm, d), lambda i: (i, 0)), pl.BlockSpec((tm, nq * n), lambda i: (i, 0))],
        out_shape=[jax.ShapeDtypeStruct((nt * tm, d), F32), jax.ShapeDtypeStruct((nt * tm, nq * n), ACT_DT)],
        scratch_shapes=[pltpu.VMEM((tm, nq * n), F32), pltpu.VMEM((past + tm, d), F32)],
        ride=ride,
    )


def _sc_bwd(dh, hh, x, first, g, w_in, cw, w_out, *, tm, ride=None):
    t_len, d = dh.shape
    nt = t_len // tm
    nq, _, n = w_in.shape
    width = cw.shape[0]
    past = _past_rows(width)
    halo_rows, halo_index = _halo_block(past, tm, nt)

    def body(dh_ref, hh_ref, hhp_ref, x_ref, first_ref, g_ref, win_ref, cw_ref, wout_ref,
             dx_ref, dhh_ref, z_ref, hn_ref, dcw_ref, dg_ref, dfirst_ref, cbuf, dbuf):
        i = pl.program_id(0)
        r = nt - 1 - i

        @pl.when(i == 0)
        def _():
            dbuf[pl.ds(tm, past), :] = jnp.zeros((past, d), F32)
            dcw_ref[...] = jnp.zeros_like(dcw_ref)
            dg_ref[...] = jnp.zeros_like(dg_ref)

        dh_out = dh_ref[...]
        b = hh_ref[:, 0:d].astype(F32)
        c = hh_ref[:, d:2 * d].astype(F32)
        v = hh_ref[:, 2 * d:3 * d].astype(F32)
        prev = hhp_ref[...].astype(F32)[halo_rows - past:, :]
        cbuf[pl.ds(0, past), :] = jnp.where(r > 0, prev[:, d:2 * d] * prev[:, 2 * d:3 * d], 0.0)
        cbuf[pl.ds(past, tm), :] = c * v
        _link_past(cbuf, slice(None), width, tm)
        taps = _conv_taps(cbuf, slice(None), width, tm)
        u = sum(cw_ref[k:k + 1, :] * taps[k] for k in range(width))
        z_ref[...] = (b * u).astype(z_ref.dtype)
        dz = _dot_nt(dh_out.astype(MXU_DT), wout_ref[...])
        dhh_ref[:, 0:d] = (dz * u).astype(dhh_ref.dtype)
        du = dz * b
        for k in range(width):
            dcw_ref[k:k + 1, :] += jnp.sum(taps[k] * du, axis=0, keepdims=True)
        dbuf[pl.ds(0, tm), :] = du
        _link_future(dbuf, slice(None), width, tm)
        dcv = _conv_back(dbuf, cw_ref, slice(None), width, tm)
        dbuf[pl.ds(tm, past), :] = dbuf[pl.ds(0, past), :]
        dhh_ref[:, d:2 * d] = (dcv * v).astype(dhh_ref.dtype)
        dhh_ref[:, 2 * d:3 * d] = (dcv * c).astype(dhh_ref.dtype)
        dhn = sum(_dot_nt(dhh_ref[:, q * n:(q + 1) * n], win_ref[q]) for q in range(nq))
        h_in = jnp.where(r == 0, first_ref[...], x_ref[...])
        dh_in = _norm_bwd_tile(dhn, h_in, dh_out, g_ref[...], _valid_rows(r, tm), hn_ref, dg_ref)

        @pl.when(r == 0)
        def _():
            dfirst_ref[...] = dh_in

        @pl.when(r > 0)
        def _():
            dx_ref[...] = dh_in

    rev = lambda i: (nt - 1 - i, 0)
    rev_x = lambda i: (jnp.maximum(nt - 2 - i, 0), 0)
    return _launch(
        body, [dh, hh, hh, x, first, g, w_in, cw, w_out], name="sc_bwd", grid=(nt,),
        in_specs=[pl.BlockSpec((tm, d), rev), pl.BlockSpec((tm, 3 * d), rev), pl.BlockSpec((halo_rows, 3 * d), halo_index),
                  pl.BlockSpec((tm, d), rev_x), _const((tm, d)), _const((1, d)), _const(w_in.shape), _const(cw.shape),
                  _const(w_out.shape)],
        out_specs=[pl.BlockSpec((tm, d), rev_x), pl.BlockSpec((tm, 3 * d), rev), pl.BlockSpec((tm, d), rev),
                   pl.BlockSpec((tm, d), rev), _const((F32_ROWS, d)), _const((F32_ROWS, d)), _const((tm, d))],
        out_shape=[jax.ShapeDtypeStruct((t_len - tm, d), F32), jax.ShapeDtypeStruct((t_len, 3 * d), ACT_DT),
                   jax.ShapeDtypeStruct((t_len, d), ACT_DT), jax.ShapeDtypeStruct((t_len, d), ACT_DT),
                   jax.ShapeDtypeStruct((F32_ROWS, d), F32), jax.ShapeDtypeStruct((F32_ROWS, d), F32),
                   jax.ShapeDtypeStruct((tm, d), F32)],
        scratch_shapes=[pltpu.VMEM((past + tm, d), F32), pltpu.VMEM((tm + past, d), F32)],
        ride=ride,
    )


def _ffn_fwd(h, g, w_up, cw, w_down, *, tm, ride=None):
    t_len, d = h.shape
    nt = t_len // tm
    nq, _, n = w_up.shape
    width = cw.shape[0]
    past = _past_rows(width)

    def body(h_ref, g_ref, wup_ref, cw_ref, wdn_ref, out_ref, hu_ref, ubuf, tail):
        i = pl.program_id(0)

        @pl.when(i == 0)
        def _():
            tail[...] = jnp.zeros_like(tail)

        h_in = h_ref[...]
        hn = _rms(h_in, g_ref[...])[0].astype(MXU_DT)
        ubuf[pl.ds(0, past), :] = tail[...]
        for q in range(nq):
            ubuf[pl.ds(past, tm), q * n:(q + 1) * n] = _dot(hn, wup_ref[q])
        hu_ref[...] = ubuf[pl.ds(past, tm), :].astype(hu_ref.dtype)
        tail[...] = ubuf[pl.ds(tm, past), :]
        _link_past(ubuf, slice(None), width, tm)
        acc = h_in
        for j in range(nq // 2):
            gcol, vcol = slice(j * n, (j + 1) * n), slice((nq // 2 + j) * n, (nq // 2 + j + 1) * n)
            conv = lambda cols: sum(cw_ref[k:k + 1, cols] * tap for k, tap in enumerate(_conv_taps(ubuf, cols, width, tm)))
            gj, vj = conv(gcol), conv(vcol)
            acc = acc + _dot((gj * _sigmoid(gj) * vj).astype(MXU_DT), wdn_ref[j * n:(j + 1) * n, :])
        out_ref[...] = acc

    row = lambda i: (i, 0)
    return _launch(
        body, [h, g, w_up, cw, w_down], name="ffn_fwd", grid=(nt,),
        in_specs=[pl.BlockSpec((tm, d), row), _const((1, d)), _const(w_up.shape), _const(cw.shape), _const(w_down.shape)],
        out_specs=[pl.BlockSpec((tm, d), row), pl.BlockSpec((tm, nq * n), row)],
        out_shape=[jax.ShapeDtypeStruct((t_len, d), F32), jax.ShapeDtypeStruct((t_len, nq * n), ACT_DT)],
        scratch_shapes=[pltpu.VMEM((past + tm, nq * n), F32), pltpu.VMEM((past, nq * n), F32)],
        ride=ride,
    )


def _norm_bwd_tile(dhn, h_in, dh, gain, valid, hn_ref, dg_ref):
    hn, xhat, rstd = _rms(h_in, gain)
    hn_ref[...] = hn.astype(hn_ref.dtype)
    dg_ref[0:1, :] += jnp.sum(dhn * xhat, axis=0, keepdims=True)
    return jnp.where(valid, dh + _rms_bwd(dhn, xhat, rstd, gain), 0.0)


def _ffn_bwd(dh, hu, h, g, w_up, cw, w_down, *, tm, ride=None):
    t_len, d = dh.shape
    nt = t_len // tm
    ff = hu.shape[1]
    n = ff // 4
    width = cw.shape[0]
    past = _past_rows(width)
    halo_rows, halo_index = _halo_block(past, tm, nt)

    def body(dh_ref, hu_ref, hup_ref, h_ref, g_ref, wup_ref, cw_ref, wdn_ref,
             dhin_ref, a_ref, dhu_ref, hn_ref, dcw_ref, dg_ref, ubuf, dbuf, head):
        i = pl.program_id(0)
        r = nt - 1 - i

        @pl.when(i == 0)
        def _():
            head[...] = jnp.zeros_like(head)
            dcw_ref[...] = jnp.zeros_like(dcw_ref)
            dg_ref[...] = jnp.zeros_like(dg_ref)

        dh_out = dh_ref[...]
        dhb = dh_out.astype(MXU_DT)
        dhn = jnp.zeros((tm, d), F32)
        for j in range(2):
            mine = slice(0, n), slice(n, 2 * n)
            full = slice(j * n, (j + 1) * n), slice((2 + j) * n, (3 + j) * n)
            for here, there in zip(mine, full):
                prev = hup_ref[:, there].astype(F32)[halo_rows - past:, :]
                ubuf[pl.ds(0, past), here] = jnp.where(r > 0, prev, 0.0)
                ubuf[pl.ds(past, tm), here] = hu_ref[:, there].astype(F32)
                dbuf[pl.ds(tm, past), here] = head[:, there]
            _link_past(ubuf, slice(None), width, tm)
            conv = lambda here, there: sum(cw_ref[k:k + 1, there] * tap
                                           for k, tap in enumerate(_conv_taps(ubuf, here, width, tm)))
            gj, vj = conv(mine[0], full[0]), conv(mine[1], full[1])
            sg = _sigmoid(gj)
            s = gj * sg
            a_ref[:, full[0]] = (s * vj).astype(a_ref.dtype)
            da = _dot_nt(dhb, wdn_ref[j * n:(j + 1) * n, :])
            dbuf[pl.ds(0, tm), mine[1]] = da * s
            dbuf[pl.ds(0, tm), mine[0]] = da * vj * (sg * (1.0 + gj * (1.0 - sg)))
            for here, there in zip(mine, full):
                head[:, there] = dbuf[pl.ds(0, past), here]
            _link_future(dbuf, slice(None), width, tm)
            for here, there in zip(mine, full):
                dy = dbuf[pl.ds(0, tm), here]
                for k, tap in enumerate(_conv_taps(ubuf, here, width, tm)):
                    dcw_ref[k:k + 1, there] += jnp.sum(tap * dy, axis=0, keepdims=True)
                dhu = sum(cw_ref[k:k + 1, there] * dbuf[pl.ds(F32_ROWS * (width - 1 - k), tm), here]
                          for k in range(width)).astype(dhu_ref.dtype)
                dhu_ref[:, there] = dhu
                dhn = dhn + _dot_nt(dhu, wup_ref[there.start // n])
        dhin_ref[...] = _norm_bwd_tile(dhn, h_ref[...], dh_out, g_ref[...], _valid_rows(r, tm), hn_ref, dg_ref)

    rev = lambda i: (nt - 1 - i, 0)
    return _launch(
        body, [dh, hu, hu, h, g, w_up, cw, w_down], name="ffn_bwd", grid=(nt,),
        in_specs=[pl.BlockSpec((tm, d), rev), pl.BlockSpec((tm, ff), rev), pl.BlockSpec((halo_rows, ff), halo_index),
                  pl.BlockSpec((tm, d), rev), _const((1, d)), _const(w_up.shape), _const(cw.shape), _const(w_down.shape)],
        out_specs=[pl.BlockSpec((tm, d), rev), pl.BlockSpec((tm, 2 * n), rev), pl.BlockSpec((tm, ff), rev),
                   pl.BlockSpec((tm, d), rev), _const((F32_ROWS, ff)), _const((F32_ROWS, d))],
        out_shape=[jax.ShapeDtypeStruct((t_len, d), F32), jax.ShapeDtypeStruct((t_len, 2 * n), ACT_DT),
                   jax.ShapeDtypeStruct((t_len, ff), ACT_DT), jax.ShapeDtypeStruct((t_len, d), ACT_DT),
                   jax.ShapeDtypeStruct((F32_ROWS, ff), F32), jax.ShapeDtypeStruct((F32_ROWS, d), F32)],
        scratch_shapes=[pltpu.VMEM((past + tm, 2 * n), F32), pltpu.VMEM((tm + past, 2 * n), F32), pltpu.VMEM((past, ff), F32)],
        ride=ride,
    )


V_CONV_B, V_B_A, V_B_X, V_LAMBDA = 0, 1, 2, 3
G_CONV_W, G_CONV_B, G_B_A, G_B_X, G_LAMBDA = 0, 4, 5, 6, 7


def _scan(a_ref, b_ref, edge, tm, reverse):
    nj = tm // F32_ROWS
    order = range(nj - 1, -1, -1) if reverse else range(nj)
    slab = lambda ref, j: ref[pl.ds(F32_ROWS * j, F32_ROWS), :]
    a_run = b_run = None
    for j in order:
        a_j, b_j = slab(a_ref, j), slab(b_ref, j)
        if a_run is not None:
            b_j = b_j + a_j * b_run
            a_j = a_j * a_run
            b_ref[pl.ds(F32_ROWS * j, F32_ROWS), :] = b_j
            a_ref[pl.ds(F32_ROWS * j, F32_ROWS), :] = a_j
        a_run, b_run = a_j, b_j
    sub = _sublane()
    shift = 1
    while shift < F32_ROWS:
        amount = F32_ROWS - shift if reverse else shift
        keep = (sub < F32_ROWS - shift) if reverse else (sub >= shift)
        b_run = jnp.where(keep, b_run + a_run * pltpu.roll(b_run, amount, 0), b_run)
        a_run = jnp.where(keep, a_run * pltpu.roll(a_run, amount, 0), a_run)
        shift *= 2
    outer = edge[0:1, :] if reverse else edge[F32_ROWS - 1:F32_ROWS, :]
    ends = b_run + a_run * outer
    if reverse:
        carry = jnp.where(sub == F32_ROWS - 1, outer, pltpu.roll(ends, F32_ROWS - 1, 0))
    else:
        carry = jnp.where(sub == 0, outer, pltpu.roll(ends, 1, 0))
    for j in range(nj):
        b_ref[pl.ds(F32_ROWS * j, F32_ROWS), :] = slab(b_ref, j) + slab(a_ref, j) * carry
    return slab(b_ref, 0 if reverse else nj - 1)


def _rg_gates(u, vec_ref, wa_ref, wx_ref, pre_scr, nb, bd):
    ub = u.astype(MXU_DT)
    for k in range(nb):
        blk = slice(k * bd, (k + 1) * bd)
        pre_scr[0, :, blk] = _dot(ub[:, blk], wa_ref[k])
        pre_scr[1, :, blk] = _dot(ub[:, blk], wx_ref[k])
    r_gate = _sigmoid(pre_scr[0] + vec_ref[V_B_A:V_B_A + 1, :])
    i_gate = _sigmoid(pre_scr[1] + vec_ref[V_B_X:V_B_X + 1, :])
    sp = _softplus(-vec_ref[V_LAMBDA:V_LAMBDA + 1, :])
    log_a = -RG_C * r_gate * sp
    a = jnp.exp(log_a)
    mult = jnp.sqrt(-_expm1_neg(2.0 * log_a))
    return r_gate, i_gate, a, mult, sp, ub


def _rg_fwd(h, g, w_in, cw, vec, wa, wx, w_out, *, tm):
    t_len, d = h.shape
    nt = t_len // tm
    nq, _, n = w_in.shape
    dr = 2 * n
    width = cw.shape[0]
    past = _past_rows(width)
    nb, bd, _ = wa.shape

    def body(h_ref, g_ref, win_ref, cw_ref, vec_ref, wa_ref, wx_ref, wout_ref, out_ref, hh_ref, hs_ref,
             gbuf, rbuf, pre_scr, tail, edge):
        i = pl.program_id(0)

        @pl.when(i == 0)
        def _():
            tail[...] = jnp.zeros_like(tail)
            edge[...] = jnp.zeros_like(edge)

        h_in = h_ref[...]
        hn = _rms(h_in, g_ref[...])[0].astype(MXU_DT)
        rbuf[pl.ds(0, past), :] = tail[...]
        for q in range(2):
            gbuf[:, q * n:(q + 1) * n] = _dot(hn, win_ref[q])
            rbuf[pl.ds(past, tm), q * n:(q + 1) * n] = _dot(hn, win_ref[2 + q])
        hh_ref[:, 0:dr] = gbuf[...].astype(hh_ref.dtype)
        hh_ref[:, dr:2 * dr] = rbuf[pl.ds(past, tm), :].astype(hh_ref.dtype)
        tail[...] = rbuf[pl.ds(tm, past), :]
        _link_past(rbuf, slice(None), width, tm)
        taps = _conv_taps(rbuf, slice(None), width, tm)
        u = sum(cw_ref[k:k + 1, :] * taps[k] for k in range(width)) + vec_ref[V_CONV_B:V_CONV_B + 1, :]
        _, i_gate, a, mult, _, _ = _rg_gates(u, vec_ref, wa_ref, wx_ref, pre_scr, nb, bd)
        pre_scr[0] = a
        pre_scr[1] = jnp.where(_valid_rows(i, tm), mult * (i_gate * u), 0.0)
        edge[...] = _scan(pre_scr.at[0], pre_scr.at[1], edge[...], tm, reverse=False)
        hs = pre_scr[1]
        hs_ref[...] = hs
        y = hs * _gelu(gbuf[...])[0]
        out_ref[...] = h_in + _dot(y.astype(MXU_DT), wout_ref[...])

    row = lambda i: (i, 0)
    return pl.pallas_call(
        body, name="rg_fwd", grid=(nt,),
        in_specs=[pl.BlockSpec((tm, d), row), _const((1, d)), _const(w_in.shape), _const(cw.shape), _const(vec.shape),
                  _const(wa.shape), _const(wx.shape), _const(w_out.shape)],
        out_specs=[pl.BlockSpec((tm, d), row), pl.BlockSpec((tm, 2 * dr), row), pl.BlockSpec((tm, dr), row)],
        out_shape=[jax.ShapeDtypeStruct((t_len, d), F32), jax.ShapeDtypeStruct((t_len, 2 * dr), ACT_DT),
                   jax.ShapeDtypeStruct((t_len, dr), F32)],
        scratch_shapes=[pltpu.VMEM((tm, dr), F32), pltpu.VMEM((past + tm, dr), F32), pltpu.VMEM((2, tm, dr), F32),
                        pltpu.VMEM((past, dr), F32), pltpu.VMEM((F32_ROWS, dr), F32)],
        compiler_params=_params(),
    )(h, g, w_in, cw, vec, wa, wx, w_out)


def _rg_bwd(dh, hh, hs, h, g, w_in, cw, vec, wa, wx, w_out, *, tm, ride=None):
    t_len, d = dh.shape
    nt = t_len // tm
    dr = hs.shape[1]
    n = w_in.shape[2]
    width = cw.shape[0]
    nb, bd, _ = wa.shape
    past = _past_rows(width)
    halo_rows, halo_index = _halo_block(past, tm, nt)
    one = F32_ROWS

    def body(dh_ref, hh_ref, hhp_ref, hs_ref, hsp_ref, h_ref, g_ref, win_ref, cw_ref, vec_ref, wa_ref, wx_ref, wout_ref,
             dhin_ref, dhh_ref, y_ref, hn_ref, dvec_ref, dwa_ref, dwx_ref, dg_ref, rbuf, dbuf, pre_scr, hbuf, abuf, edge):
        i = pl.program_id(0)
        r = nt - 1 - i

        @pl.when(i == 0)
        def _():
            dbuf[pl.ds(tm, past), :] = jnp.zeros((past, dr), F32)
            abuf[pl.ds(tm, one), :] = jnp.zeros((one, dr), F32)
            edge[...] = jnp.zeros_like(edge)
            dvec_ref[...] = jnp.zeros_like(dvec_ref)
            dwa_ref[...] = jnp.zeros_like(dwa_ref)
            dwx_ref[...] = jnp.zeros_like(dwx_ref)
            dg_ref[...] = jnp.zeros_like(dg_ref)

        dh_out = dh_ref[...]
        gb = hh_ref[:, 0:dr].astype(F32)
        prev = hhp_ref[...].astype(F32)[halo_rows - past:, dr:2 * dr]
        rbuf[pl.ds(0, past), :] = jnp.where(r > 0, prev, 0.0)
        rbuf[pl.ds(past, tm), :] = hh_ref[:, dr:2 * dr].astype(F32)
        _link_past(rbuf, slice(None), width, tm)
        taps = _conv_taps(rbuf, slice(None), width, tm)
        u = sum(cw_ref[k:k + 1, :] * taps[k] for k in range(width)) + vec_ref[V_CONV_B:V_CONV_B + 1, :]
        r_gate, i_gate, a, mult, sp, ub = _rg_gates(u, vec_ref, wa_ref, wx_ref, pre_scr, nb, bd)
        hs_t = hs_ref[...]
        hbuf[pl.ds(0, one), :] = jnp.where(r > 0, hsp_ref[...], 0.0)
        hbuf[pl.ds(one, tm), :] = hs_t
        _link_past(hbuf, slice(None), 2, tm)
        h_prev = hbuf[pl.ds(0, tm), :]
        gate, th = _gelu(gb)
        y_ref[...] = (hs_t * gate).astype(y_ref.dtype)
        dy = _dot_nt(dh_out.astype(MXU_DT), wout_ref[...])
        d_gb = (dy * hs_t * _gelu_grad(gb, th)).astype(dhh_ref.dtype)
        dhh_ref[:, 0:dr] = d_gb
        dhn = sum(_dot_nt(d_gb[:, q * n:(q + 1) * n], win_ref[q]) for q in range(2))
        abuf[pl.ds(0, tm), :] = a
        _link_future(abuf, slice(None), 2, tm)
        pre_scr[0] = abuf[pl.ds(one, tm), :]
        pre_scr[1] = dy * gate
        edge[...] = _scan(pre_scr.at[0], pre_scr.at[1], edge[...], tm, reverse=True)
        abuf[pl.ds(tm, one), :] = abuf[pl.ds(0, one), :]
        d_hs = pre_scr[1]
        d_b = jnp.where(_valid_rows(r, tm), d_hs, 0.0)
        d_iu = d_b * mult
        d_log_a = d_hs * h_prev * a - d_b * (i_gate * u) * (a * a) / jnp.maximum(mult, 1e-30)
        dvec_ref[G_LAMBDA:G_LAMBDA + 1, :] += jnp.sum(d_log_a * r_gate, axis=0, keepdims=True) * (-RG_C)
        d_pre_r = d_log_a * (-RG_C * sp) * r_gate * (1.0 - r_gate)
        d_pre_i = d_iu * u * i_gate * (1.0 - i_gate)
        dvec_ref[G_B_A:G_B_A + 1, :] += jnp.sum(d_pre_r, axis=0, keepdims=True)
        dvec_ref[G_B_X:G_B_X + 1, :] += jnp.sum(d_pre_i, axis=0, keepdims=True)
        dbuf[pl.ds(0, tm), :] = d_iu * i_gate
        d_pre_r = d_pre_r.astype(MXU_DT)
        d_pre_i = d_pre_i.astype(MXU_DT)
        for k in range(nb):
            blk = slice(k * bd, (k + 1) * bd)
            dwa_ref[k] += _dot_tn(ub[:, blk], d_pre_r[:, blk])
            dwx_ref[k] += _dot_tn(ub[:, blk], d_pre_i[:, blk])
            dbuf[pl.ds(0, tm), blk] += _dot_nt(d_pre_r[:, blk], wa_ref[k]) + _dot_nt(d_pre_i[:, blk], wx_ref[k])
        du = dbuf[pl.ds(0, tm), :]
        dvec_ref[G_CONV_B:G_CONV_B + 1, :] += jnp.sum(du, axis=0, keepdims=True)
        for k in range(width):
            dvec_ref[G_CONV_W + k:G_CONV_W + k + 1, :] += jnp.sum(taps[k] * du, axis=0, keepdims=True)
        _link_future(dbuf, slice(None), width, tm)
        d_rb = _conv_back(dbuf, cw_ref, slice(None), width, tm)
        dbuf[pl.ds(tm, past), :] = dbuf[pl.ds(0, past), :]
        d_rb = d_rb.astype(dhh_ref.dtype)
        dhh_ref[:, dr:2 * dr] = d_rb
        dhn = dhn + sum(_dot_nt(d_rb[:, q * n:(q + 1) * n], win_ref[2 + q]) for q in range(2))
        dhin_ref[...] = _norm_bwd_tile(dhn, h_ref[...], dh_out, g_ref[...], _valid_rows(r, tm), hn_ref, dg_ref)

        @pl.when(i == nt - 1)
        def _():
            lam = vec_ref[V_LAMBDA:V_LAMBDA + 1, :]
            dvec_ref[G_LAMBDA:G_LAMBDA + 1, :] = dvec_ref[G_LAMBDA:G_LAMBDA + 1, :] * (-_sigmoid(-lam))

    rev = lambda i: (nt - 1 - i, 0)
    return _launch(
        body, [dh, hh, hh, hs, hs, h, g, w_in, cw, vec, wa, wx, w_out], name="rg_bwd", grid=(nt,),
        in_specs=[pl.BlockSpec((tm, d), rev), pl.BlockSpec((tm, 2 * dr), rev), pl.BlockSpec((halo_rows, 2 * dr), halo_index),
                  pl.BlockSpec((tm, dr), rev),
                  pl.BlockSpec((one, dr), lambda i: (jnp.maximum((nt - 1 - i) * (tm // one) - 1, 0), 0)),
                  pl.BlockSpec((tm, d), rev), _const((1, d)), _const(w_in.shape),
                  _const(cw.shape), _const(vec.shape), _const(wa.shape), _const(wx.shape), _const(w_out.shape)],
        out_specs=[pl.BlockSpec((tm, d), rev), pl.BlockSpec((tm, 2 * dr), rev), pl.BlockSpec((tm, dr), rev),
                   pl.BlockSpec((tm, d), rev), _const((F32_ROWS, dr)), _const(wa.shape), _const(wx.shape),
                   _const((F32_ROWS, d))],
        out_shape=[jax.ShapeDtypeStruct((t_len, d), F32), jax.ShapeDtypeStruct((t_len, 2 * dr), ACT_DT),
                   jax.ShapeDtypeStruct((t_len, dr), ACT_DT), jax.ShapeDtypeStruct((t_len, d), ACT_DT),
                   jax.ShapeDtypeStruct((F32_ROWS, dr), F32), jax.ShapeDtypeStruct(wa.shape, F32),
                   jax.ShapeDtypeStruct(wx.shape, F32), jax.ShapeDtypeStruct((F32_ROWS, d), F32)],
        scratch_shapes=[pltpu.VMEM((past + tm, dr), F32), pltpu.VMEM((tm + past, dr), F32), pltpu.VMEM((2, tm, dr), F32),
                        pltpu.VMEM((one + tm, dr), F32), pltpu.VMEM((tm + one, dr), F32), pltpu.VMEM((F32_ROWS, dr), F32)],
        ride=ride,
    )


def _weight_grad(a, b, nb, *, rows):
    t_len, k_dim = a.shape
    n = b.shape[1] // nb
    nt = t_len // rows

    def body(a_ref, b_ref, out_ref, wire_ref):
        @pl.when(pl.program_id(1) == 0)
        def _():
            out_ref[...] = jnp.zeros_like(out_ref)

        out_ref[0] += _dot_tn(a_ref[...].astype(MXU_DT), b_ref[...].astype(MXU_DT))

        @pl.when(pl.program_id(1) == nt - 1)
        def _():
            wire_ref[...] = out_ref[...].astype(wire_ref.dtype)

    block = pl.BlockSpec((1, k_dim, n), lambda j, i: (j, 0, 0))
    return pl.pallas_call(
        body, name="weight_grad", grid=(nb, nt),
        in_specs=[pl.BlockSpec((rows, k_dim), lambda j, i: (i, 0)), pl.BlockSpec((rows, n), lambda j, i: (i, j))],
        out_specs=[block, block],
        out_shape=[jax.ShapeDtypeStruct((nb, k_dim, n), F32), jax.ShapeDtypeStruct((nb, k_dim, n), WIRE_DT)],
        compiler_params=_params(2),
    )(a, b)


def _loss_head(h, target, g, *, tm):
    t_len, d = h.shape
    nt = t_len // tm

    def body(h_ref, t_ref, g_ref, dh_ref, sq_ref, dg_ref):
        i = pl.program_id(0)

        @pl.when(i == 0)
        def _():
            sq_ref[...] = jnp.zeros_like(sq_ref)
            dg_ref[...] = jnp.zeros_like(dg_ref)
            dh_ref[...] = jnp.zeros_like(dh_ref)

        @pl.when(i > 0)
        def _():
            gain = g_ref[...]
            out, xhat, rstd = _rms(h_ref[...], gain)
            err = out - t_ref[...]
            sq_ref[0:1, :] += jnp.sum(err * err, axis=0, keepdims=True)
            dout = err * (1.0 / d)
            dg_ref[0:1, :] += jnp.sum(dout * xhat, axis=0, keepdims=True)
            dh_ref[...] = _rms_bwd(dout, xhat, rstd, gain)

    row = lambda i: (i, 0)
    return pl.pallas_call(
        body, name="loss_head", grid=(nt,),
        in_specs=[pl.BlockSpec((tm, d), row), pl.BlockSpec((tm, d), lambda i: (jnp.maximum(i - 1, 0), 0)), _const((1, d))],
        out_specs=[pl.BlockSpec((tm, d), row), _const((F32_ROWS, d)), _const((F32_ROWS, d))],
        out_shape=[jax.ShapeDtypeStruct((t_len, d), F32), jax.ShapeDtypeStruct((F32_ROWS, d), F32),
                   jax.ShapeDtypeStruct((F32_ROWS, d), F32)],
        compiler_params=_params(),
    )(h, target, g)


def _adamw(w, m, v, parts, *, rows, layer=0, into=None):
    n_layers, n_rows, n_cols = w.shape
    nt = n_rows // rows
    n_parts = len(parts)

    def body(w_ref, m_ref, v_ref, *rest):
        part_refs, (g_ref, d_ref, nm_ref, nv_ref) = rest[:n_parts], rest[-4:]
        w_ref, m_ref, v_ref, g_ref, d_ref, nm_ref, nv_ref = (r.at[0] for r in (w_ref, m_ref, v_ref, g_ref, d_ref, nm_ref, nv_ref))
        grad = part_refs[0][...].astype(F32)
        for p in part_refs[1:]:
            grad = grad + p[...].astype(F32)
        new_m = ADAM_B1 * m_ref[...] + (1.0 - ADAM_B1) * grad
        new_v = ADAM_B2 * v_ref[...] + (1.0 - ADAM_B2) * (grad * grad)
        m_hat = new_m / (1.0 - ADAM_B1 ** ADAM_STEP)
        v_hat = new_v / (1.0 - ADAM_B2 ** ADAM_STEP)
        g_ref[...] = grad
        d_ref[...] = -ADAM_LR * (m_hat / (jnp.sqrt(v_hat) + ADAM_EPS) + ADAM_WD * w_ref[...])
        nm_ref[...] = new_m
        nv_ref[...] = new_v

    spec = pl.BlockSpec((rows, n_cols), lambda i: (i, 0))
    layer_spec = pl.BlockSpec((1, rows, n_cols), lambda i: (layer, i, 0))
    into = list(into or [])
    return pl.pallas_call(
        body, name="adamw", grid=(nt,),
        in_specs=[layer_spec] * 3 + [spec] * n_parts + [ANY] * len(into), out_specs=[layer_spec] * 4,
        out_shape=[jax.ShapeDtypeStruct(w.shape, F32)] * 4,
        input_output_aliases={3 + n_parts + k: k for k in range(len(into))},
        compiler_params=_params(),
    )(w, m, v, *parts, *into)


def _sum_parts(own, recv, *, rows):
    n_rows, n_cols = own.shape
    n_recv = recv.shape[0]

    def body(own_ref, recv_ref, out_ref):
        acc = own_ref[...].astype(F32)
        for j in range(n_recv):
            acc = acc + recv_ref[j].astype(F32)
        out_ref[...] = acc

    return pl.pallas_call(
        body, name="sum_parts", grid=(n_rows // rows,),
        in_specs=[pl.BlockSpec((rows, n_cols), lambda i: (i, 0)), pl.BlockSpec((n_recv, rows, n_cols), lambda i: (0, i, 0))],
        out_specs=pl.BlockSpec((rows, n_cols), lambda i: (i, 0)),
        out_shape=jax.ShapeDtypeStruct(own.shape, F32),
        compiler_params=_params(),
    )(own, recv)


def _swap_cores(arrays):
    nk = len(arrays)

    def body(*refs):
        ins, outs, (send_sems, recv_sems) = refs[:nk], refs[nk:2 * nk], refs[2 * nk:]
        x, y, c = _place()
        sends = [pltpu.make_async_remote_copy(
            src_ref=ins[k], dst_ref=outs[k], send_sem=send_sems.at[k], recv_sem=recv_sems.at[k],
            device_id=(x, y, 1 - c), device_id_type=MESH_ID) for k in range(nk)]
        for cp in sends:
            cp.start()
        for cp in sends:
            cp.wait_recv()
        for cp in sends:
            cp.wait_send()

    return pl.pallas_call(
        body, name="swap_cores", in_specs=[ANY] * nk, out_specs=[ANY] * nk,
        out_shape=[jax.ShapeDtypeStruct(a.shape, a.dtype) for a in arrays],
        scratch_shapes=[pltpu.SemaphoreType.DMA((nk,)), pltpu.SemaphoreType.DMA((nk,))],
    )(*arrays)


class _AllDevices:
    def __init__(self, v):
        self.arrays = [v]
        self.out_shape = [jax.ShapeDtypeStruct((8,) + v.shape, v.dtype)]
        self.scratch = [pltpu.SemaphoreType.DMA((7,)), pltpu.SemaphoreType.DMA((7,)), pltpu.SemaphoreType.DMA((1,))]

    def run(self, ins, outs, sems, start):
        (v_ref,), (out_ref,), (send_sems, recv_sems, local_sems) = ins, outs, sems
        x, y, c = _place()
        local = pltpu.make_async_copy(v_ref, out_ref.at[4 * x + 2 * y + c], local_sems.at[0])
        local.start() if start else local.wait()
        for flip in range(1, 8):
            px, py, pc = x ^ (flip >> 2), y ^ ((flip >> 1) & 1), c ^ (flip & 1)
            sems_f = dict(send_sem=send_sems.at[flip - 1], recv_sem=recv_sems.at[flip - 1], device_id=(px, py, pc),
                          device_id_type=MESH_ID)
            send = pltpu.make_async_remote_copy(src_ref=v_ref, dst_ref=out_ref.at[4 * x + 2 * y + c], **sems_f)
            if start:
                send.start()
            else:
                pltpu.make_async_remote_copy(src_ref=v_ref, dst_ref=out_ref.at[4 * px + 2 * py + pc], **sems_f).wait_recv()
                send.wait_send()


class _Both:
    def __init__(self, first, second):
        self.rides = (first, second)
        self.arrays = first.arrays + second.arrays
        self.out_shape = first.out_shape + second.out_shape
        self.scratch = first.scratch + second.scratch

    def run(self, ins, outs, sems, start):
        for ride in self.rides:
            n_in, n_out, n_sem = len(ride.arrays), len(ride.out_shape), len(ride.scratch)
            ride.run(ins[:n_in], outs[:n_out], sems[:n_sem], start)
            ins, outs, sems = ins[n_in:], outs[n_out:], sems[n_sem:]


def _pack(arrays, pad_rows=F32_ROWS):
    flat = jnp.concatenate([a.reshape(-1).astype(F32) for a in arrays])
    rows = -(-flat.shape[0] // (LANES * pad_rows)) * pad_rows
    return jnp.pad(flat, (0, rows * LANES - flat.shape[0])).reshape(rows, LANES)


def _unpack(packed, shapes):
    flat, out, off = packed.reshape(-1), [], 0
    for s in shapes:
        size = 1
        for dim in s:
            size *= dim
        out.append(flat[off:off + size].reshape(s))
        off += size
    return out


def _divisor_rows(n_rows, most=256):
    best = None
    for r in range(ACT_ROWS, most + 1, ACT_ROWS):
        if n_rows % r == 0:
            best = r
    return best or n_rows


def kernel(x, meta_tokens, norm_mix_g, norm_ffn_g, final_norm_g, sc_w_in, sc_conv_w, sc_w_out, rg_w_in, rg_conv_w, rg_conv_b, rg_w_gate_a, rg_b_gate_a, rg_w_gate_x, rg_b_gate_x, rg_lambda, rg_w_out, ffn_w_up, ffn_conv_w, ffn_w_down, loss_target, m_meta_tokens, m_norm_mix_g, m_norm_ffn_g, m_final_norm_g, m_sc_w_in, m_sc_conv_w, m_sc_w_out, m_rg_w_in, m_rg_conv_w, m_rg_conv_b, m_rg_w_gate_a, m_rg_b_gate_a, m_rg_w_gate_x, m_rg_b_gate_x, m_rg_lambda, m_rg_w_out, m_ffn_w_up, m_ffn_conv_w, m_ffn_w_down, v_meta_tokens, v_norm_mix_g, v_norm_ffn_g, v_final_norm_g, v_sc_w_in, v_sc_conv_w, v_sc_w_out, v_rg_w_in, v_rg_conv_w, v_rg_conv_b, v_rg_w_gate_a, v_rg_b_gate_a, v_rg_w_gate_x, v_rg_b_gate_x, v_rg_lambda, v_rg_w_out, v_ffn_w_up, v_ffn_conv_w, v_ffn_w_down):
    weights = dict(meta_tokens=meta_tokens, norm_mix_g=norm_mix_g, norm_ffn_g=norm_ffn_g, final_norm_g=final_norm_g, sc_w_in=sc_w_in, sc_conv_w=sc_conv_w, sc_w_out=sc_w_out, rg_w_in=rg_w_in, rg_conv_w=rg_conv_w, rg_conv_b=rg_conv_b, rg_w_gate_a=rg_w_gate_a, rg_b_gate_a=rg_b_gate_a, rg_w_gate_x=rg_w_gate_x, rg_b_gate_x=rg_b_gate_x, rg_lambda=rg_lambda, rg_w_out=rg_w_out, ffn_w_up=ffn_w_up, ffn_conv_w=ffn_conv_w, ffn_w_down=ffn_w_down)
    m_in = dict(meta_tokens=m_meta_tokens, norm_mix_g=m_norm_mix_g, norm_ffn_g=m_norm_ffn_g, final_norm_g=m_final_norm_g, sc_w_in=m_sc_w_in, sc_conv_w=m_sc_conv_w, sc_w_out=m_sc_w_out, rg_w_in=m_rg_w_in, rg_conv_w=m_rg_conv_w, rg_conv_b=m_rg_conv_b, rg_w_gate_a=m_rg_w_gate_a, rg_b_gate_a=m_rg_b_gate_a, rg_w_gate_x=m_rg_w_gate_x, rg_b_gate_x=m_rg_b_gate_x, rg_lambda=m_rg_lambda, rg_w_out=m_rg_w_out, ffn_w_up=m_ffn_w_up, ffn_conv_w=m_ffn_conv_w, ffn_w_down=m_ffn_w_down)
    v_in = dict(meta_tokens=v_meta_tokens, norm_mix_g=v_norm_mix_g, norm_ffn_g=v_norm_ffn_g, final_norm_g=v_final_norm_g, sc_w_in=v_sc_w_in, sc_conv_w=v_sc_conv_w, sc_w_out=v_sc_w_out, rg_w_in=v_rg_w_in, rg_conv_w=v_rg_conv_w, rg_conv_b=v_rg_conv_b, rg_w_gate_a=v_rg_w_gate_a, rg_b_gate_a=v_rg_b_gate_a, rg_w_gate_x=v_rg_w_gate_x, rg_b_gate_x=v_rg_b_gate_x, rg_lambda=v_rg_lambda, rg_w_out=v_rg_w_out, ffn_w_up=v_ffn_w_up, ffn_conv_w=v_ffn_conv_w, ffn_w_down=v_ffn_w_down)
    names = list(weights)

    seq, d = x.shape[1:]
    tm = _row_tile(seq)
    tokens, target = _tile_order(x[0], tm), _tile_order(loss_target[0], tm)
    t_len = seq + tm
    wg_rows = 5 * tm if t_len % (5 * tm) == 0 else tm
    xi, yi, _ = _place()
    chip = 2 * xi + yi
    mesh_axes = ("x", "y", "c")

    wire = lambda w: w.astype(WIRE_DT)
    small_sharded = ["meta_tokens", "sc_conv_w", "rg_conv_w", "rg_conv_b", "rg_b_gate_a", "rg_b_gate_x", "rg_lambda", "ffn_conv_w"]
    small_2d = {n: weights[n].reshape(-1, weights[n].shape[-1]) for n in small_sharded}
    w_sc_in, w_sc_out, small_by_chip = _exchange(
        _Gather([wire(sc_w_in[0]), wire(sc_w_out[0]), _pack([small_2d[n] for n in small_sharded])]), "gather_first")
    w_sc_out = w_sc_out.reshape(-1, d)
    gather_ffn0 = _Gather([wire(ffn_w_up[0]), wire(ffn_w_down[0])])
    gather_rest = _Gather([wire(rg_w_in[0]), wire(rg_w_out[0]), wire(ffn_w_up[1]), wire(ffn_w_down[1])])
    small_len = sum(a.size for a in small_2d.values())
    by_chip = small_by_chip.reshape(N_CHIPS, -1)[:, :small_len]
    full, off = {}, 0
    for n in small_sharded:
        rows, width = small_2d[n].shape
        full[n] = by_chip[:, off:off + rows * width].reshape(N_CHIPS, rows, width).transpose(1, 0, 2).reshape(rows, N_CHIPS * width)
        off += rows * width
    sc_cw, rg_cw = full["sc_conv_w"], full["rg_conv_w"]
    ffn_cw = [full["ffn_conv_w"][0:3], full["ffn_conv_w"][3:6]]
    d_rnn = rg_cw.shape[1]
    vec = jnp.concatenate([full["rg_conv_b"], full["rg_b_gate_a"], full["rg_b_gate_x"], full["rg_lambda"],
                           jnp.zeros((F32_ROWS - 4, d_rnn), F32)])
    wa, wx = rg_w_gate_a[0].astype(MXU_DT), rg_w_gate_x[0].astype(MXU_DT)
    first = _tile_order(jnp.concatenate([jnp.zeros((tm - N_META, d), F32), full["meta_tokens"]]), tm)
    g_mix = [norm_mix_g[0:1], norm_mix_g[1:2]]
    g_ffn = [norm_ffn_g[0:1], norm_ffn_g[1:2]]

    h1, hh0, w_up0, w_dn0 = _sc_fwd(tokens, first, g_mix[0], w_sc_in, sc_cw, w_sc_out, tm=tm, ride=gather_ffn0)
    h2, hu0, w_rg_in, w_rg_out, w_up1, w_dn1 = _ffn_fwd(h1, g_ffn[0], w_up0, ffn_cw[0], w_dn0.reshape(-1, d), tm=tm,
                                                         ride=gather_rest)
    w_up, w_dn, w_rg_out = [w_up0, w_up1], [w_dn0.reshape(-1, d), w_dn1.reshape(-1, d)], w_rg_out.reshape(-1, d)
    h3, hh1, hs = _rg_fwd(h2, g_mix[1], w_rg_in, rg_cw, vec, wa, wx, w_rg_out, tm=tm)
    h4, hu1 = _ffn_fwd(h3, g_ffn[1], w_up[1], ffn_cw[1], w_dn[1], tm=tm)
    dh4, sq, d_final = _loss_head(h4, target, final_norm_g.reshape(1, d), tm=tm)
    loss = lax.psum(jnp.sum(sq[0]) * (0.5 / d), mesh_axes)

    def by_chip_rows(pair):
        return [p.reshape(N_CHIPS, -1, d) for p in pair]

    def ffn_backward(dh_out, h_in, hu, layer, ride):
        dh_in, act, dhu, hn, dcw, dg, *landed = _ffn_bwd(dh_out, hu, h_in, g_ffn[layer], w_up[layer], ffn_cw[layer],
                                                         w_dn[layer], tm=tm, ride=ride)
        d_up = _weight_grad(hn, dhu, N_CHIPS, rows=wg_rows)
        d_dn = by_chip_rows(_weight_grad(act, dh_out, 1, rows=wg_rows))
        return dh_in, d_up, d_dn, dcw[0:3], dg[0], landed

    dh3, d_up1, d_dn1, d_fcw1, d_gf1, _ = ffn_backward(dh4, h3, hu1, 1, None)
    dh2, dhh1, y_rg, hn_rg, d_vec, d_wa, d_wx, d_gm1, *landed_ffn1 = _rg_bwd(
        dh3, hh1, hs, h2, g_mix[1], w_rg_in, rg_cw, vec, wa, wx, w_rg_out, tm=tm, ride=_Scatter([d_up1[1], d_dn1[1]]))
    d_rg_in = _weight_grad(hn_rg, dhh1, N_CHIPS, rows=wg_rows)
    d_rg_out = by_chip_rows(_weight_grad(y_rg, dh3, 1, rows=wg_rows))
    early = {"rg_conv_w": d_vec[G_CONV_W:G_CONV_W + 4], "rg_conv_b": d_vec[G_CONV_B:G_CONV_B + 1],
             "rg_b_gate_a": d_vec[G_B_A:G_B_A + 1], "rg_b_gate_x": d_vec[G_B_X:G_B_X + 1],
             "rg_lambda": d_vec[G_LAMBDA:G_LAMBDA + 1], "ffn_conv_w.1": d_fcw1, "norm_mix_g.1": d_gm1[0:1],
             "norm_ffn_g.1": d_gf1[None], "final_norm_g": d_final[0], "rg_w_gate_a": d_wa[None], "rg_w_gate_x": d_wx[None]}
    early_packed = _pack(list(early.values()))
    dh1, d_up0, d_dn0, d_fcw0, d_gf0, (*landed_rg, early_by_device) = ffn_backward(
        dh2, h1, hu0, 0, _Both(_Scatter([d_rg_in[1], d_rg_out[1]]), _AllDevices(early_packed)))
    grad_x, dhh0, z_sc, hn_sc, d_sccw, d_gm0, d_first, *landed_ffn0 = _sc_bwd(
        dh1, hh0, tokens, first, g_mix[0], w_sc_in, sc_cw, w_sc_out, tm=tm, ride=_Scatter([d_up0[1], d_dn0[1]]))
    d_sc_in = _weight_grad(hn_sc, dhh0, N_CHIPS, rows=wg_rows)
    d_sc_out = by_chip_rows(_weight_grad(z_sc, dh1, 1, rows=wg_rows))
    landed_sc = _exchange(_Scatter([d_sc_in[1], d_sc_out[1]]), "scatter_last")
    grad_x = _time_order(grad_x, tm)[None]

    big = [("sc_w_in", 0, d_sc_in, landed_sc[0]), ("sc_w_out", 0, d_sc_out, landed_sc[1]),
           ("rg_w_in", 0, d_rg_in, landed_rg[0]), ("rg_w_out", 0, d_rg_out, landed_rg[1]),
           ("ffn_w_up", 0, d_up0, landed_ffn0[0]), ("ffn_w_up", 1, d_up1, landed_ffn1[0]),
           ("ffn_w_down", 0, d_dn0, landed_ffn0[1]), ("ffn_w_down", 1, d_dn1, landed_ffn1[1])]
    core_sum = []
    for _, _, (partial, _), received in big:
        own = lax.dynamic_index_in_dim(partial, chip, 0, keepdims=False)
        core_sum.append(_sum_parts(own, received, rows=_divisor_rows(own.shape[0])))
    other_sum = _swap_cores(core_sum)
    out = {k: {} for k in ("grad", "delta", "m", "v")}
    stacked = {}
    for (n, layer, _, _), mine, theirs in zip(big, core_sum, other_sum):
        stacked[n] = _adamw(weights[n], m_in[n], v_in[n], [mine, theirs], rows=_divisor_rows(mine.shape[0]), layer=layer,
                            into=stacked.get(n))
    for n, res in stacked.items():
        for k, key in enumerate(("grad", "delta", "m", "v")):
            out[key][n] = res[k]

    late = {"meta_tokens": _time_order(d_first, tm)[tm - N_META:], "sc_conv_w": d_sccw[0:3], "ffn_conv_w.0": d_fcw0,
            "norm_mix_g.0": d_gm0[0:1], "norm_ffn_g.0": d_gf0[None]}
    late_packed = _pack(list(late.values()))
    late_by_device, = _exchange(_AllDevices(late_packed), "gather_devices")
    summed = {}
    for parts, packed, by_device in ((early, early_packed, early_by_device), (late, late_packed, late_by_device)):
        total = _sum_parts(by_device[0], by_device[1:], rows=packed.shape[0])
        summed.update(zip(parts, _unpack(total, [p.shape for p in parts.values()])))
    for n in ("ffn_conv_w", "norm_mix_g", "norm_ffn_g"):
        summed[n] = jnp.concatenate([summed.pop(n + ".0"), summed.pop(n + ".1")])
    replicated = ["norm_mix_g", "norm_ffn_g", "final_norm_g", "rg_w_gate_a", "rg_w_gate_x"]
    small_names = small_sharded + replicated
    grads = {}
    for n in small_sharded:
        width = small_2d[n].shape[1]
        grads[n] = lax.dynamic_slice_in_dim(summed[n], chip * width, width, axis=1).reshape(weights[n].shape)
    for n in replicated:
        grads[n] = summed[n].reshape(weights[n].shape)
    shapes = [weights[n].shape for n in small_names]
    packed_w = _pack([weights[n] for n in small_names])
    res = _adamw(packed_w[None], _pack([m_in[n] for n in small_names])[None], _pack([v_in[n] for n in small_names])[None],
                 [_pack([grads[n] for n in small_names])], rows=packed_w.shape[0])
    for k, key in enumerate(("grad", "delta", "m", "v")):
        out[key].update(dict(zip(small_names, _unpack(res[k][0], shapes))))

    return (loss, grad_x, *[out["grad"][n] for n in names], *[out["delta"][n] for n in names],
            *[out["m"][n] for n in names], *[out["v"][n] for n in names])
```

```python
import functools

import jax
import jax.numpy as jnp
from jax import lax
from jax.experimental import pallas as pl
from jax.experimental.pallas import tpu as pltpu

F32 = jnp.float32
MXU_DT = jnp.bfloat16
ACT_DT = jnp.bfloat16
WIRE_DT = jnp.bfloat16
MESH_ID = pl.DeviceIdType.MESH

N_META = 16
RMS_EPS = 1e-6
RG_C = 8.0
ADAM_LR, ADAM_B1, ADAM_B2, ADAM_EPS, ADAM_WD, ADAM_STEP = 0.001, 0.9, 0.999, 1e-08, 0.01, 10
N_CHIPS = 4
VMEM_LIMIT = 60 * 1024 * 1024
F32_ROWS = 8
ACT_ROWS = 16
LANES = 128


def _row_tile(seq):
    for tm in (256, 128, 64, 32, 16):
        if seq % tm == 0:
            return tm
    raise ValueError(f"sequence length {seq} is not a multiple of 16")


def _params(n_axes=1, **kw):
    return pltpu.CompilerParams(dimension_semantics=("arbitrary",) * n_axes, vmem_limit_bytes=VMEM_LIMIT, **kw)


def _const(shape):
    return pl.BlockSpec(shape, lambda *_: (0,) * len(shape), pipeline_mode=pl.Buffered(1))


def _dot(a, b):
    return jnp.dot(a, b, preferred_element_type=F32)


def _dot_nt(a, b):
    return lax.dot_general(a, b, (((1,), (1,)), ((), ())), preferred_element_type=F32)


def _dot_tn(a, b):
    return lax.dot_general(a, b, (((0,), (0,)), ((), ())), preferred_element_type=F32)


def _sigmoid(x):
    return 1.0 / (1.0 + jnp.exp(-x))


def _rms(h, g):
    rstd = lax.rsqrt(jnp.mean(h * h, axis=-1, keepdims=True) + RMS_EPS)
    xhat = h * rstd
    return xhat * g, xhat, rstd


def _rms_bwd(dhn, xhat, rstd, g):
    dx = dhn * g
    return rstd * (dx - xhat * jnp.mean(dx * xhat, axis=-1, keepdims=True))


def _gelu(x):
    k = 0.7978845608028654
    t = jnp.tanh(k * (x + 0.044715 * x * x * x))
    return 0.5 * x * (1.0 + t), t


def _gelu_grad(x, t):
    k = 0.7978845608028654
    return 0.5 * (1.0 + t) + 0.5 * x * (1.0 - t * t) * k * (1.0 + 3 * 0.044715 * x * x)


def _softplus(x):
    e = jnp.exp(-jnp.abs(x))
    return jnp.maximum(x, 0.0) + jnp.where(e < 1e-4, e - 0.5 * e * e, jnp.log(1.0 + e))


def _expm1_neg(z):
    series = z * (1.0 + z * (0.5 + z * (1.0 / 6 + z * (1.0 / 24 + z * (1.0 / 120)))))
    return jnp.where(z > -0.1, series, jnp.exp(z) - 1.0)


def _tile_order(a, tm):
    return a.reshape(-1, F32_ROWS, tm // F32_ROWS, a.shape[-1]).swapaxes(1, 2).reshape(a.shape)


def _time_order(a, tm):
    return a.reshape(-1, tm // F32_ROWS, F32_ROWS, a.shape[-1]).swapaxes(1, 2).reshape(a.shape)


def _valid_rows(tile, tm):
    row = lax.broadcasted_iota(jnp.int32, (tm, 1), 0)
    time = (row & (F32_ROWS - 1)) * (tm // F32_ROWS) + (row >> 3) + tile * tm
    return time >= tm - N_META


def _sublane():
    return lax.broadcasted_iota(jnp.int32, (F32_ROWS, 1), 0)


def _past_rows(width):
    return (width - 1) * F32_ROWS


def _halo_block(past, tm, nt):
    rows = -(-past // ACT_ROWS) * ACT_ROWS
    return rows, lambda i: (jnp.maximum((nt - 1 - i) * (tm // rows) - 1, 0), 0)


def _link_past(buf, cols, width, tm):
    past = _past_rows(width)
    for k in range(1, width):
        rows = pl.ds(past - F32_ROWS * k, F32_ROWS)
        before = pltpu.roll(buf[rows, cols], 1, 0)
        mine = pltpu.roll(buf[pl.ds(past + tm - F32_ROWS * k, F32_ROWS), cols], 1, 0)
        buf[rows, cols] = jnp.where(_sublane() == 0, before, mine)


def _link_future(buf, cols, width, tm):
    for k in range(1, width):
        rows = pl.ds(tm + F32_ROWS * (k - 1), F32_ROWS)
        after = pltpu.roll(buf[rows, cols], F32_ROWS - 1, 0)
        mine = pltpu.roll(buf[pl.ds(F32_ROWS * (k - 1), F32_ROWS), cols], F32_ROWS - 1, 0)
        buf[rows, cols] = jnp.where(_sublane() == F32_ROWS - 1, after, mine)


def _conv_taps(buf, cols, width, tm):
    return [buf[pl.ds(F32_ROWS * k, tm), cols] for k in range(width)]


def _conv_back(buf, cw_ref, cols, width, tm):
    return sum(cw_ref[k:k + 1, cols] * buf[pl.ds(F32_ROWS * (width - 1 - k), tm), cols] for k in range(width))


ANY = pl.BlockSpec(memory_space=pl.ANY)


def _place():
    return lax.axis_index("x"), lax.axis_index("y"), lax.axis_index("c")


def _other_chips(x, y):
    return [(1 - x, y), (x, 1 - y), (1 - x, 1 - y)]


class _Gather:
    def __init__(self, shards):
        nk = len(shards)
        self.arrays = list(shards)
        self.out_shape = [jax.ShapeDtypeStruct((N_CHIPS,) + s.shape, s.dtype) for s in shards]
        self.scratch = [pltpu.SemaphoreType.DMA((nk, 3)), pltpu.SemaphoreType.DMA((nk, 3)), pltpu.SemaphoreType.DMA((nk,))]

    def run(self, ins, outs, sems, start):
        send_sems, recv_sems, local_sems = sems
        x, y, c = _place()
        mine = 2 * x + y
        for k in range(len(ins)):
            local = pltpu.make_async_copy(ins[k], outs[k].at[mine], local_sems.at[k])
            local.start() if start else local.wait()
            for j, (px, py) in enumerate(_other_chips(x, y)):
                sems_kj = dict(send_sem=send_sems.at[k, j], recv_sem=recv_sems.at[k, j], device_id=(px, py, c),
                               device_id_type=MESH_ID)
                send = pltpu.make_async_remote_copy(src_ref=ins[k], dst_ref=outs[k].at[mine], **sems_kj)
                if start:
                    send.start()
                else:
                    pltpu.make_async_remote_copy(src_ref=ins[k], dst_ref=outs[k].at[2 * px + py], **sems_kj).wait_recv()
                    send.wait_send()


class _Scatter:
    def __init__(self, parts):
        nk = len(parts)
        self.arrays = list(parts)
        self.out_shape = [jax.ShapeDtypeStruct((3,) + p.shape[1:], p.dtype) for p in parts]
        self.scratch = [pltpu.SemaphoreType.DMA((nk, 3)), pltpu.SemaphoreType.DMA((nk, 3))]

    def run(self, ins, outs, sems, start):
        send_sems, recv_sems = sems
        x, y, c = _place()
        for k in range(len(ins)):
            for j, (px, py) in enumerate(_other_chips(x, y)):
                send = pltpu.make_async_remote_copy(
                    src_ref=ins[k].at[2 * px + py], dst_ref=outs[k].at[j], send_sem=send_sems.at[k, j],
                    recv_sem=recv_sems.at[k, j], device_id=(px, py, c), device_id_type=MESH_ID)
                if start:
                    send.start()
                else:
                    send.wait_recv()
                    send.wait_send()


def _exchange(ride, name):
    n_in, n_out = len(ride.arrays), len(ride.out_shape)

    def body(*refs):
        ride.run(refs[:n_in], refs[n_in:n_in + n_out], refs[n_in + n_out:], start=True)
        ride.run(refs[:n_in], refs[n_in:n_in + n_out], refs[n_in + n_out:], start=False)

    return pl.pallas_call(body, name=name, in_specs=[ANY] * n_in, out_specs=[ANY] * n_out, out_shape=ride.out_shape,
                          scratch_shapes=ride.scratch)(*ride.arrays)


def _launch(body, operands, *, name, grid, in_specs, out_specs, out_shape, scratch_shapes=(), ride=None):
    common = dict(name=name, grid=grid, compiler_params=_params(len(grid)))
    if ride is None:
        return pl.pallas_call(body, in_specs=in_specs, out_specs=out_specs, out_shape=out_shape,
                              scratch_shapes=list(scratch_shapes), **common)(*operands)
    n_in, n_out, n_scr = len(operands), len(out_shape), len(scratch_shapes)
    r_in, r_out = len(ride.arrays), len(ride.out_shape)
    last = grid[0] - 1

    def riding(*refs):
        ins, refs = refs[:n_in], refs[n_in:]
        r_ins, refs = refs[:r_in], refs[r_in:]
        outs, refs = refs[:n_out], refs[n_out:]
        r_outs, refs = refs[:r_out], refs[r_out:]
        scr, r_sems = refs[:n_scr], refs[n_scr:]
        i = pl.program_id(0)

        @pl.when(i == 0)
        def _():
            ride.run(r_ins, r_outs, r_sems, start=True)

        body(*ins, *outs, *scr)

        @pl.when(i == last)
        def _():
            ride.run(r_ins, r_outs, r_sems, start=False)

    return pl.pallas_call(
        riding, in_specs=list(in_specs) + [ANY] * r_in, out_specs=list(out_specs) + [ANY] * r_out,
        out_shape=list(out_shape) + ride.out_shape, scratch_shapes=list(scratch_shapes) + ride.scratch, **common,
    )(*operands, *ride.arrays)


def _sc_fwd(x, first, g, w_in, cw, w_out, *, tm, ride=None):
    seq, d = x.shape
    nt = seq // tm + 1
    nq, _, n = w_in.shape
    width = cw.shape[0]
    past = _past_rows(width)

    def body(x_ref, first_ref, g_ref, win_ref, cw_ref, wout_ref, h1_ref, hh_ref, hh_scr, cbuf):
        i = pl.program_id(0)

        @pl.when(i == 0)
        def _():
            cbuf[pl.ds(0, past), :] = jnp.zeros((past, d), F32)

        h = jnp.where(i == 0, first_ref[...], x_ref[...])
        hn = _rms(h, g_ref[...])[0].astype(MXU_DT)
        for q in range(nq):
            hh_scr[:, q * n:(q + 1) * n] = _dot(hn, win_ref[q])
        hh_ref[...] = hh_scr[...].astype(hh_ref.dtype)
        b = hh_scr[:, 0:d]
        cbuf[pl.ds(past, tm), :] = hh_scr[:, d:2 * d] * hh_scr[:, 2 * d:3 * d]
        last = cbuf[pl.ds(tm, past), :]
        _link_past(cbuf, slice(None), width, tm)
        u = sum(cw_ref[k:k + 1, :] * tap for k, tap in enumerate(_conv_taps(cbuf, slice(None), width, tm)))
        cbuf[pl.ds(0, past), :] = last
        h1_ref[...] = h + _dot((b * u).astype(MXU_DT), wout_ref[...])

    return _launch(
        body, [x, first, g, w_in, cw, w_out], name="sc_fwd", grid=(nt,),
        in_specs=[pl.BlockSpec((tm, d), lambda i: (jnp.maximum(i - 1, 0), 0)), _const((tm, d)), _const((1, d)),
                  _const(w_in.shape), _const(cw.shape), _const(w_out.shape)],
        out_specs=[pl.BlockSpec((tm, d), lambda i: (i, 0)), pl.BlockSpec((tm, nq * n), lambda i: (i, 0))],
        out_shape=[jax.ShapeDtypeStruct((nt * tm, d), F32), jax.ShapeDtypeStruct((nt * tm, nq * n), ACT_DT)],
        scratch_shapes=[pltpu.VMEM((tm, nq * n), F32), pltpu.VMEM((past + tm, d), F32)],
        ride=ride,
    )


def _sc_bwd(dh, hh, x, first, g, w_in, cw, w_out, *, tm, ride=None):
    t_len, d = dh.shape
    nt = t_len // tm
    nq, _, n = w_in.shape
    width = cw.shape[0]
    past = _past_rows(width)
    halo_rows, halo_index = _halo_block(past, tm, nt)

    def body(dh_ref, hh_ref, hhp_ref, x_ref, first_ref, g_ref, win_ref, cw_ref, wout_ref,
             dx_ref, dhh_ref, z_ref, hn_ref, dcw_ref, dg_ref, dfirst_ref, cbuf, dbuf):
        i = pl.program_id(0)
        r = nt - 1 - i

        @pl.when(i == 0)
        def _():
            dbuf[pl.ds(tm, past), :] = jnp.zeros((past, d), F32)
            dcw_ref[...] = jnp.zeros_like(dcw_ref)
            dg_ref[...] = jnp.zeros_like(dg_ref)

        dh_out = dh_ref[...]
        b = hh_ref[:, 0:d].astype(F32)
        c = hh_ref[:, d:2 * d].astype(F32)
        v = hh_ref[:, 2 * d:3 * d].astype(F32)
        prev = hhp_ref[...].astype(F32)[halo_rows - past:, :]
        cbuf[pl.ds(0, past), :] = jnp.where(r > 0, prev[:, d:2 * d] * prev[:, 2 * d:3 * d], 0.0)
        cbuf[pl.ds(past, tm), :] = c * v
        _link_past(cbuf, slice(None), width, tm)
        taps = _conv_taps(cbuf, slice(None), width, tm)
        u = sum(cw_ref[k:k + 1, :] * taps[k] for k in range(width))
        z_ref[...] = (b * u).astype(z_ref.dtype)
        dz = _dot_nt(dh_out.astype(MXU_DT), wout_ref[...])
        dhh_ref[:, 0:d] = (dz * u).astype(dhh_ref.dtype)
        du = dz * b
        for k in range(width):
            dcw_ref[k:k + 1, :] += jnp.sum(taps[k] * du, axis=0, keepdims=True)
        dbuf[pl.ds(0, tm), :] = du
        _link_future(dbuf, slice(None), width, tm)
        dcv = _conv_back(dbuf, cw_ref, slice(None), width, tm)
        dbuf[pl.ds(tm, past), :] = dbuf[pl.ds(0, past), :]
        dhh_ref[:, d:2 * d] = (dcv * v).astype(dhh_ref.dtype)
        dhh_ref[:, 2 * d:3 * d] = (dcv * c).astype(dhh_ref.dtype)
        dhn = sum(_dot_nt(dhh_ref[:, q * n:(q + 1) * n], win_ref[q]) for q in range(nq))
        h_in = jnp.where(r == 0, first_ref[...], x_ref[...])
        dh_in = _norm_bwd_tile(dhn, h_in, dh_out, g_ref[...], _valid_rows(r, tm), hn_ref, dg_ref)

        @pl.when(r == 0)
        def _():
            dfirst_ref[...] = dh_in

        @pl.when(r > 0)
        def _():
            dx_ref[...] = dh_in

    rev = lambda i: (nt - 1 - i, 0)
    rev_x = lambda i: (jnp.maximum(nt - 2 - i, 0), 0)
    return _launch(
        body, [dh, hh, hh, x, first, g, w_in, cw, w_out], name="sc_bwd", grid=(nt,),
        in_specs=[pl.BlockSpec((tm, d), rev), pl.BlockSpec((tm, 3 * d), rev), pl.BlockSpec((halo_rows, 3 * d), halo_index),
                  pl.BlockSpec((tm, d), rev_x), _const((tm, d)), _const((1, d)), _const(w_in.shape), _const(cw.shape),
                  _const(w_out.shape)],
        out_specs=[pl.BlockSpec((tm, d), rev_x), pl.BlockSpec((tm, 3 * d), rev), pl.BlockSpec((tm, d), rev),
                   pl.BlockSpec((tm, d), rev), _const((F32_ROWS, d)), _const((F32_ROWS, d)), _const((tm, d))],
        out_shape=[jax.ShapeDtypeStruct((t_len - tm, d), F32), jax.ShapeDtypeStruct((t_len, 3 * d), ACT_DT),
                   jax.ShapeDtypeStruct((t_len, d), ACT_DT), jax.ShapeDtypeStruct((t_len, d), ACT_DT),
                   jax.ShapeDtypeStruct((F32_ROWS, d), F32), jax.ShapeDtypeStruct((F32_ROWS, d), F32),
                   jax.ShapeDtypeStruct((tm, d), F32)],
        scratch_shapes=[pltpu.VMEM((past + tm, d), F32), pltpu.VMEM((tm + past, d), F32)],
        ride=ride,
    )


def _ffn_fwd(h, g, w_up, cw, w_down, *, tm, ride=None, loss=None):
    t_len, d = h.shape
    nt = t_len // tm
    nq, _, n = w_up.shape
    width = cw.shape[0]
    past = _past_rows(width)

    def body(h_ref, g_ref, wup_ref, cw_ref, wdn_ref, *rest):
        if loss is None:
            out_ref, hu_ref, ubuf, tail = rest
        else:
            t_ref, gf_ref, out_ref, hu_ref, sq_ref, dgf_ref, ubuf, tail = rest
        i = pl.program_id(0)

        @pl.when(i == 0)
        def _():
            tail[...] = jnp.zeros_like(tail)

        h_in = h_ref[...]
        hn = _rms(h_in, g_ref[...])[0].astype(MXU_DT)
        ubuf[pl.ds(0, past), :] = tail[...]
        for q in range(nq):
            ubuf[pl.ds(past, tm), q * n:(q + 1) * n] = _dot(hn, wup_ref[q])
        hu_ref[...] = ubuf[pl.ds(past, tm), :].astype(hu_ref.dtype)
        tail[...] = ubuf[pl.ds(tm, past), :]
        _link_past(ubuf, slice(None), width, tm)
        acc = h_in
        for j in range(nq // 2):
            gcol, vcol = slice(j * n, (j + 1) * n), slice((nq // 2 + j) * n, (nq // 2 + j + 1) * n)
            conv = lambda cols: sum(cw_ref[k:k + 1, cols] * tap for k, tap in enumerate(_conv_taps(ubuf, cols, width, tm)))
            gj, vj = conv(gcol), conv(vcol)
            acc = acc + _dot((gj * _sigmoid(gj) * vj).astype(MXU_DT), wdn_ref[j * n:(j + 1) * n, :])
        if loss is None:
            out_ref[...] = acc
            return

        @pl.when(i == 0)
        def _():
            sq_ref[...] = jnp.zeros_like(sq_ref)
            dgf_ref[...] = jnp.zeros_like(dgf_ref)
            out_ref[...] = jnp.zeros_like(out_ref)

        @pl.when(i > 0)
        def _():
            gain = gf_ref[...]
            out, xhat, rstd = _rms(acc, gain)
            err = out - t_ref[...]
            sq_ref[0:1, :] += jnp.sum(err * err, axis=0, keepdims=True)
            dout = err * (1.0 / d)
            dgf_ref[0:1, :] += jnp.sum(dout * xhat, axis=0, keepdims=True)
            out_ref[...] = _rms_bwd(dout, xhat, rstd, gain)

    row = lambda i: (i, 0)
    stat = jax.ShapeDtypeStruct((F32_ROWS, d), F32)
    return _launch(
        body, [h, g, w_up, cw, w_down] + list(loss or ()), name="ffn_fwd", grid=(nt,),
        in_specs=[pl.BlockSpec((tm, d), row), _const((1, d)), _const(w_up.shape), _const(cw.shape), _const(w_down.shape)]
        + ([pl.BlockSpec((tm, d), lambda i: (jnp.maximum(i - 1, 0), 0)), _const((1, d))] if loss else []),
        out_specs=[pl.BlockSpec((tm, d), row), pl.BlockSpec((tm, nq * n), row)] + ([_const(stat.shape)] * 2 if loss else []),
        out_shape=[jax.ShapeDtypeStruct((t_len, d), F32), jax.ShapeDtypeStruct((t_len, nq * n), ACT_DT)]
        + ([stat, stat] if loss else []),
        scratch_shapes=[pltpu.VMEM((past + tm, nq * n), F32), pltpu.VMEM((past, nq * n), F32)],
        ride=ride,
    )


def _norm_bwd_tile(dhn, h_in, dh, gain, valid, hn_ref, dg_ref):
    hn, xhat, rstd = _rms(h_in, gain)
    hn_ref[...] = hn.astype(hn_ref.dtype)
    dg_ref[0:1, :] += jnp.sum(dhn * xhat, axis=0, keepdims=True)
    return jnp.where(valid, dh + _rms_bwd(dhn, xhat, rstd, gain), 0.0)


def _ffn_bwd(dh, hu, h, g, w_up, cw, w_down, *, tm, ride=None):
    t_len, d = dh.shape
    nt = t_len // tm
    ff = hu.shape[1]
    n = ff // 4
    width = cw.shape[0]
    past = _past_rows(width)
    halo_rows, halo_index = _halo_block(past, tm, nt)

    def body(dh_ref, hu_ref, hup_ref, h_ref, g_ref, wup_ref, cw_ref, wdn_ref,
             dhin_ref, a_ref, dhu_ref, hn_ref, dcw_ref, dg_ref, ubuf, dbuf, head):
        i = pl.program_id(0)
        r = nt - 1 - i

        @pl.when(i == 0)
        def _():
            head[...] = jnp.zeros_like(head)
            dcw_ref[...] = jnp.zeros_like(dcw_ref)
            dg_ref[...] = jnp.zeros_like(dg_ref)

        dh_out = dh_ref[...]
        dhb = dh_out.astype(MXU_DT)
        dhn = jnp.zeros((tm, d), F32)
        for j in range(2):
            mine = slice(0, n), slice(n, 2 * n)
            full = slice(j * n, (j + 1) * n), slice((2 + j) * n, (3 + j) * n)
            for here, there in zip(mine, full):
                prev = hup_ref[:, there].astype(F32)[halo_rows - past:, :]
                ubuf[pl.ds(0, past), here] = jnp.where(r > 0, prev, 0.0)
                ubuf[pl.ds(past, tm), here] = hu_ref[:, there].astype(F32)
                dbuf[pl.ds(tm, past), here] = head[:, there]
            _link_past(ubuf, slice(None), width, tm)
            conv = lambda here, there: sum(cw_ref[k:k + 1, there] * tap
                                           for k, tap in enumerate(_conv_taps(ubuf, here, width, tm)))
            gj, vj = conv(mine[0], full[0]), conv(mine[1], full[1])
            sg = _sigmoid(gj)
            s = gj * sg
            a_ref[:, full[0]] = (s * vj).astype(a_ref.dtype)
            da = _dot_nt(dhb, wdn_ref[j * n:(j + 1) * n, :])
            dbuf[pl.ds(0, tm), mine[1]] = da * s
            dbuf[pl.ds(0, tm), mine[0]] = da * vj * (sg * (1.0 + gj * (1.0 - sg)))
            for here, there in zip(mine, full):
                head[:, there] = dbuf[pl.ds(0, past), here]
            _link_future(dbuf, slice(None), width, tm)
            for here, there in zip(mine, full):
                dy = dbuf[pl.ds(0, tm), here]
                for k, tap in enumerate(_conv_taps(ubuf, here, width, tm)):
                    dcw_ref[k:k + 1, there] += jnp.sum(tap * dy, axis=0, keepdims=True)
                dhu = sum(cw_ref[k:k + 1, there] * dbuf[pl.ds(F32_ROWS * (width - 1 - k), tm), here]
                          for k in range(width)).astype(dhu_ref.dtype)
                dhu_ref[:, there] = dhu
                dhn = dhn + _dot_nt(dhu, wup_ref[there.start // n])
        dhin_ref[...] = _norm_bwd_tile(dhn, h_ref[...], dh_out, g_ref[...], _valid_rows(r, tm), hn_ref, dg_ref)

    rev = lambda i: (nt - 1 - i, 0)
    return _launch(
        body, [dh, hu, hu, h, g, w_up, cw, w_down], name="ffn_bwd", grid=(nt,),
        in_specs=[pl.BlockSpec((tm, d), rev), pl.BlockSpec((tm, ff), rev), pl.BlockSpec((halo_rows, ff), halo_index),
                  pl.BlockSpec((tm, d), rev), _const((1, d)), _const(w_up.shape), _const(cw.shape), _const(w_down.shape)],
        out_specs=[pl.BlockSpec((tm, d), rev), pl.BlockSpec((tm, 2 * n), rev), pl.BlockSpec((tm, ff), rev),
                   pl.BlockSpec((tm, d), rev), _const((F32_ROWS, ff)), _const((F32_ROWS, d))],
        out_shape=[jax.ShapeDtypeStruct((t_len, d), F32), jax.ShapeDtypeStruct((t_len, 2 * n), ACT_DT),
                   jax.ShapeDtypeStruct((t_len, ff), ACT_DT), jax.ShapeDtypeStruct((t_len, d), ACT_DT),
                   jax.ShapeDtypeStruct((F32_ROWS, ff), F32), jax.ShapeDtypeStruct((F32_ROWS, d), F32)],
        scratch_shapes=[pltpu.VMEM((past + tm, 2 * n), F32), pltpu.VMEM((tm + past, 2 * n), F32), pltpu.VMEM((past, ff), F32)],
        ride=ride,
    )


V_CONV_B, V_B_A, V_B_X, V_LAMBDA = 0, 1, 2, 3
G_CONV_W, G_CONV_B, G_B_A, G_B_X, G_LAMBDA = 0, 4, 5, 6, 7


def _scan(a_ref, b_ref, edge, tm, reverse):
    nj = tm // F32_ROWS
    order = range(nj - 1, -1, -1) if reverse else range(nj)
    slab = lambda ref, j: ref[pl.ds(F32_ROWS * j, F32_ROWS), :]
    a_run = b_run = None
    for j in order:
        a_j, b_j = slab(a_ref, j), slab(b_ref, j)
        if a_run is not None:
            b_j = b_j + a_j * b_run
            a_j = a_j * a_run
            b_ref[pl.ds(F32_ROWS * j, F32_ROWS), :] = b_j
            a_ref[pl.ds(F32_ROWS * j, F32_ROWS), :] = a_j
        a_run, b_run = a_j, b_j
    sub = _sublane()
    shift = 1
    while shift < F32_ROWS:
        amount = F32_ROWS - shift if reverse else shift
        keep = (sub < F32_ROWS - shift) if reverse else (sub >= shift)
        b_run = jnp.where(keep, b_run + a_run * pltpu.roll(b_run, amount, 0), b_run)
        a_run = jnp.where(keep, a_run * pltpu.roll(a_run, amount, 0), a_run)
        shift *= 2
    outer = edge[0:1, :] if reverse else edge[F32_ROWS - 1:F32_ROWS, :]
    ends = b_run + a_run * outer
    if reverse:
        carry = jnp.where(sub == F32_ROWS - 1, outer, pltpu.roll(ends, F32_ROWS - 1, 0))
    else:
        carry = jnp.where(sub == 0, outer, pltpu.roll(ends, 1, 0))
    for j in range(nj):
        b_ref[pl.ds(F32_ROWS * j, F32_ROWS), :] = slab(b_ref, j) + slab(a_ref, j) * carry
    return slab(b_ref, 0 if reverse else nj - 1)


def _rg_gates(u, vec_ref, wa_ref, wx_ref, pre_scr, nb, bd):
    ub = u.astype(MXU_DT)
    for k in range(nb):
        blk = slice(k * bd, (k + 1) * bd)
        pre_scr[0, :, blk] = _dot(ub[:, blk], wa_ref[k])
        pre_scr[1, :, blk] = _dot(ub[:, blk], wx_ref[k])
    r_gate = _sigmoid(pre_scr[0] + vec_ref[V_B_A:V_B_A + 1, :])
    i_gate = _sigmoid(pre_scr[1] + vec_ref[V_B_X:V_B_X + 1, :])
    sp = _softplus(-vec_ref[V_LAMBDA:V_LAMBDA + 1, :])
    log_a = -RG_C * r_gate * sp
    a = jnp.exp(log_a)
    mult = jnp.sqrt(-_expm1_neg(2.0 * log_a))
    return r_gate, i_gate, a, mult, sp, ub


def _rg_fwd(h, g, w_in, cw, vec, wa, wx, w_out, *, tm):
    t_len, d = h.shape
    nt = t_len // tm
    nq, _, n = w_in.shape
    dr = 2 * n
    width = cw.shape[0]
    past = _past_rows(width)
    nb, bd, _ = wa.shape

    def body(h_ref, g_ref, win_ref, cw_ref, vec_ref, wa_ref, wx_ref, wout_ref, out_ref, hh_ref, hs_ref,
             gbuf, rbuf, pre_scr, tail, edge):
        i = pl.program_id(0)

        @pl.when(i == 0)
        def _():
            tail[...] = jnp.zeros_like(tail)
            edge[...] = jnp.zeros_like(edge)

        h_in = h_ref[...]
        hn = _rms(h_in, g_ref[...])[0].astype(MXU_DT)
        rbuf[pl.ds(0, past), :] = tail[...]
        for q in range(2):
            gbuf[:, q * n:(q + 1) * n] = _dot(hn, win_ref[q])
            rbuf[pl.ds(past, tm), q * n:(q + 1) * n] = _dot(hn, win_ref[2 + q])
        hh_ref[:, 0:dr] = gbuf[...].astype(hh_ref.dtype)
        hh_ref[:, dr:2 * dr] = rbuf[pl.ds(past, tm), :].astype(hh_ref.dtype)
        tail[...] = rbuf[pl.ds(tm, past), :]
        _link_past(rbuf, slice(None), width, tm)
        taps = _conv_taps(rbuf, slice(None), width, tm)
        u = sum(cw_ref[k:k + 1, :] * taps[k] for k in range(width)) + vec_ref[V_CONV_B:V_CONV_B + 1, :]
        _, i_gate, a, mult, _, _ = _rg_gates(u, vec_ref, wa_ref, wx_ref, pre_scr, nb, bd)
        pre_scr[0] = a
        pre_scr[1] = jnp.where(_valid_rows(i, tm), mult * (i_gate * u), 0.0)
        edge[...] = _scan(pre_scr.at[0], pre_scr.at[1], edge[...], tm, reverse=False)
        hs = pre_scr[1]
        hs_ref[...] = hs
        y = hs * _gelu(gbuf[...])[0]
        out_ref[...] = h_in + _dot(y.astype(MXU_DT), wout_ref[...])

    row = lambda i: (i, 0)
    return pl.pallas_call(
        body, name="rg_fwd", grid=(nt,),
        in_specs=[pl.BlockSpec((tm, d), row), _const((1, d)), _const(w_in.shape), _const(cw.shape), _const(vec.shape),
                  _const(wa.shape), _const(wx.shape), _const(w_out.shape)],
        out_specs=[pl.BlockSpec((tm, d), row), pl.BlockSpec((tm, 2 * dr), row), pl.BlockSpec((tm, dr), row)],
        out_shape=[jax.ShapeDtypeStruct((t_len, d), F32), jax.ShapeDtypeStruct((t_len, 2 * dr), ACT_DT),
                   jax.ShapeDtypeStruct((t_len, dr), F32)],
        scratch_shapes=[pltpu.VMEM((tm, dr), F32), pltpu.VMEM((past + tm, dr), F32), pltpu.VMEM((2, tm, dr), F32),
                        pltpu.VMEM((past, dr), F32), pltpu.VMEM((F32_ROWS, dr), F32)],
        compiler_params=_params(),
    )(h, g, w_in, cw, vec, wa, wx, w_out)


def _rg_bwd(dh, hh, hs, h, g, w_in, cw, vec, wa, wx, w_out, *, tm, ride=None):
    t_len, d = dh.shape
    nt = t_len // tm
    dr = hs.shape[1]
    n = w_in.shape[2]
    width = cw.shape[0]
    nb, bd, _ = wa.shape
    past = _past_rows(width)
    halo_rows, halo_index = _halo_block(past, tm, nt)
    one = F32_ROWS

    def body(dh_ref, hh_ref, hhp_ref, hs_ref, hsp_ref, h_ref, g_ref, win_ref, cw_ref, vec_ref, wa_ref, wx_ref, wout_ref,
             dhin_ref, dhh_ref, y_ref, hn_ref, dvec_ref, dwa_ref, dwx_ref, dg_ref, rbuf, dbuf, pre_scr, hbuf, abuf, edge):
        i = pl.program_id(0)
        r = nt - 1 - i

        @pl.when(i == 0)
        def _():
            dbuf[pl.ds(tm, past), :] = jnp.zeros((past, dr), F32)
            abuf[pl.ds(tm, one), :] = jnp.zeros((one, dr), F32)
            edge[...] = jnp.zeros_like(edge)
            dvec_ref[...] = jnp.zeros_like(dvec_ref)
            dwa_ref[...] = jnp.zeros_like(dwa_ref)
            dwx_ref[...] = jnp.zeros_like(dwx_ref)
            dg_ref[...] = jnp.zeros_like(dg_ref)

        dh_out = dh_ref[...]
        gb = hh_ref[:, 0:dr].astype(F32)
        prev = hhp_ref[...].astype(F32)[halo_rows - past:, dr:2 * dr]
        rbuf[pl.ds(0, past), :] = jnp.where(r > 0, prev, 0.0)
        rbuf[pl.ds(past, tm), :] = hh_ref[:, dr:2 * dr].astype(F32)
        _link_past(rbuf, slice(None), width, tm)
        taps = _conv_taps(rbuf, slice(None), width, tm)
        u = sum(cw_ref[k:k + 1, :] * taps[k] for k in range(width)) + vec_ref[V_CONV_B:V_CONV_B + 1, :]
        r_gate, i_gate, a, mult, sp, ub = _rg_gates(u, vec_ref, wa_ref, wx_ref, pre_scr, nb, bd)
        hs_t = hs_ref[...]
        hbuf[pl.ds(0, one), :] = jnp.where(r > 0, hsp_ref[...], 0.0)
        hbuf[pl.ds(one, tm), :] = hs_t
        _link_past(hbuf, slice(None), 2, tm)
        h_prev = hbuf[pl.ds(0, tm), :]
        gate, th = _gelu(gb)
        y_ref[...] = (hs_t * gate).astype(y_ref.dtype)
        dy = _dot_nt(dh_out.astype(MXU_DT), wout_ref[...])
        d_gb = (dy * hs_t * _gelu_grad(gb, th)).astype(dhh_ref.dtype)
        dhh_ref[:, 0:dr] = d_gb
        dhn = sum(_dot_nt(d_gb[:, q * n:(q + 1) * n], win_ref[q]) for q in range(2))
        abuf[pl.ds(0, tm), :] = a
        _link_future(abuf, slice(None), 2, tm)
        pre_scr[0] = abuf[pl.ds(one, tm), :]
        pre_scr[1] = dy * gate
        edge[...] = _scan(pre_scr.at[0], pre_scr.at[1], edge[...], tm, reverse=True)
        abuf[pl.ds(tm, one), :] = abuf[pl.ds(0, one), :]
        d_hs = pre_scr[1]
        d_b = jnp.where(_valid_rows(r, tm), d_hs, 0.0)
        d_iu = d_b * mult
        d_log_a = d_hs * h_prev * a - d_b * (i_gate * u) * (a * a) / jnp.maximum(mult, 1e-30)
        dvec_ref[G_LAMBDA:G_LAMBDA + 1, :] += jnp.sum(d_log_a * r_gate, axis=0, keepdims=True) * (-RG_C)
        d_pre_r = d_log_a * (-RG_C * sp) * r_gate * (1.0 - r_gate)
        d_pre_i = d_iu * u * i_gate * (1.0 - i_gate)
        dvec_ref[G_B_A:G_B_A + 1, :] += jnp.sum(d_pre_r, axis=0, keepdims=True)
        dvec_ref[G_B_X:G_B_X + 1, :] += jnp.sum(d_pre_i, axis=0, keepdims=True)
        dbuf[pl.ds(0, tm), :] = d_iu * i_gate
        d_pre_r = d_pre_r.astype(MXU_DT)
        d_pre_i = d_pre_i.astype(MXU_DT)
        for k in range(nb):
            blk = slice(k * bd, (k + 1) * bd)
            dwa_ref[k] += _dot_tn(ub[:, blk], d_pre_r[:, blk])
            dwx_ref[k] += _dot_tn(ub[:, blk], d_pre_i[:, blk])
            dbuf[pl.ds(0, tm), blk] += _dot_nt(d_pre_r[:, blk], wa_ref[k]) + _dot_nt(d_pre_i[:, blk], wx_ref[k])
        du = dbuf[pl.ds(0, tm), :]
        dvec_ref[G_CONV_B:G_CONV_B + 1, :] += jnp.sum(du, axis=0, keepdims=True)
        for k in range(width):
            dvec_ref[G_CONV_W + k:G_CONV_W + k + 1, :] += jnp.sum(taps[k] * du, axis=0, keepdims=True)
        _link_future(dbuf, slice(None), width, tm)
        d_rb = _conv_back(dbuf, cw_ref, slice(None), width, tm)
        dbuf[pl.ds(tm, past), :] = dbuf[pl.ds(0, past), :]
        d_rb = d_rb.astype(dhh_ref.dtype)
        dhh_ref[:, dr:2 * dr] = d_rb
        dhn = dhn + sum(_dot_nt(d_rb[:, q * n:(q + 1) * n], win_ref[2 + q]) for q in range(2))
        dhin_ref[...] = _norm_bwd_tile(dhn, h_ref[...], dh_out, g_ref[...], _valid_rows(r, tm), hn_ref, dg_ref)

        @pl.when(i == nt - 1)
        def _():
            lam = vec_ref[V_LAMBDA:V_LAMBDA + 1, :]
            dvec_ref[G_LAMBDA:G_LAMBDA + 1, :] = dvec_ref[G_LAMBDA:G_LAMBDA + 1, :] * (-_sigmoid(-lam))

    rev = lambda i: (nt - 1 - i, 0)
    return _launch(
        body, [dh, hh, hh, hs, hs, h, g, w_in, cw, vec, wa, wx, w_out], name="rg_bwd", grid=(nt,),
        in_specs=[pl.BlockSpec((tm, d), rev), pl.BlockSpec((tm, 2 * dr), rev), pl.BlockSpec((halo_rows, 2 * dr), halo_index),
                  pl.BlockSpec((tm, dr), rev),
                  pl.BlockSpec((one, dr), lambda i: (jnp.maximum((nt - 1 - i) * (tm // one) - 1, 0), 0)),
                  pl.BlockSpec((tm, d), rev), _const((1, d)), _const(w_in.shape),
                  _const(cw.shape), _const(vec.shape), _const(wa.shape), _const(wx.shape), _const(w_out.shape)],
        out_specs=[pl.BlockSpec((tm, d), rev), pl.BlockSpec((tm, 2 * dr), rev), pl.BlockSpec((tm, dr), rev),
                   pl.BlockSpec((tm, d), rev), _const((F32_ROWS, dr)), _const(wa.shape), _const(wx.shape),
                   _const((F32_ROWS, d))],
        out_shape=[jax.ShapeDtypeStruct((t_len, d), F32), jax.ShapeDtypeStruct((t_len, 2 * dr), ACT_DT),
                   jax.ShapeDtypeStruct((t_len, dr), ACT_DT), jax.ShapeDtypeStruct((t_len, d), ACT_DT),
                   jax.ShapeDtypeStruct((F32_ROWS, dr), F32), jax.ShapeDtypeStruct(wa.shape, F32),
                   jax.ShapeDtypeStruct(wx.shape, F32), jax.ShapeDtypeStruct((F32_ROWS, d), F32)],
        scratch_shapes=[pltpu.VMEM((past + tm, dr), F32), pltpu.VMEM((tm + past, dr), F32), pltpu.VMEM((2, tm, dr), F32),
                        pltpu.VMEM((one + tm, dr), F32), pltpu.VMEM((tm + one, dr), F32), pltpu.VMEM((F32_ROWS, dr), F32)],
        ride=ride,
    )


def _weight_grad(a, b, nb, *, rows):
    t_len, k_dim = a.shape
    n = b.shape[1] // nb
    nt = t_len // rows

    def body(a_ref, b_ref, out_ref, wire_ref):
        @pl.when(pl.program_id(1) == 0)
        def _():
            out_ref[...] = jnp.zeros_like(out_ref)

        out_ref[0] += _dot_tn(a_ref[...].astype(MXU_DT), b_ref[...].astype(MXU_DT))

        @pl.when(pl.program_id(1) == nt - 1)
        def _():
            wire_ref[...] = out_ref[...].astype(wire_ref.dtype)

    block = pl.BlockSpec((1, k_dim, n), lambda j, i: (j, 0, 0))
    return pl.pallas_call(
        body, name="weight_grad", grid=(nb, nt),
        in_specs=[pl.BlockSpec((rows, k_dim), lambda j, i: (i, 0)), pl.BlockSpec((rows, n), lambda j, i: (i, j))],
        out_specs=[block, block],
        out_shape=[jax.ShapeDtypeStruct((nb, k_dim, n), F32), jax.ShapeDtypeStruct((nb, k_dim, n), WIRE_DT)],
        compiler_params=_params(2),
    )(a, b)


def _adamw(w, m, v, parts, *, rows, layer=0, into=None):
    n_layers, n_rows, n_cols = w.shape
    nt = n_rows // rows
    n_parts = len(parts)

    def body(w_ref, m_ref, v_ref, *rest):
        part_refs, (g_ref, d_ref, nm_ref, nv_ref) = rest[:n_parts], rest[-4:]
        w_ref, m_ref, v_ref, g_ref, d_ref, nm_ref, nv_ref = (r.at[0] for r in (w_ref, m_ref, v_ref, g_ref, d_ref, nm_ref, nv_ref))
        grad = part_refs[0][...].astype(F32)
        for p in part_refs[1:]:
            grad = grad + p[...].astype(F32)
        new_m = ADAM_B1 * m_ref[...] + (1.0 - ADAM_B1) * grad
        new_v = ADAM_B2 * v_ref[...] + (1.0 - ADAM_B2) * (grad * grad)
        m_hat = new_m / (1.0 - ADAM_B1 ** ADAM_STEP)
        v_hat = new_v / (1.0 - ADAM_B2 ** ADAM_STEP)
        g_ref[...] = grad
        d_ref[...] = -ADAM_LR * (m_hat / (jnp.sqrt(v_hat) + ADAM_EPS) + ADAM_WD * w_ref[...])
        nm_ref[...] = new_m
        nv_ref[...] = new_v

    spec = pl.BlockSpec((rows, n_cols), lambda i: (i, 0))
    layer_spec = pl.BlockSpec((1, rows, n_cols), lambda i: (layer, i, 0))
    into = list(into or [])
    return pl.pallas_call(
        body, name="adamw", grid=(nt,),
        in_specs=[layer_spec] * 3 + [spec] * n_parts + [ANY] * len(into), out_specs=[layer_spec] * 4,
        out_shape=[jax.ShapeDtypeStruct(w.shape, F32)] * 4,
        input_output_aliases={3 + n_parts + k: k for k in range(len(into))},
        compiler_params=_params(),
    )(w, m, v, *parts, *into)


def _sum_stack(stack, *, rows):
    n_stack, n_rows, n_cols = stack.shape

    def body(stack_ref, out_ref):
        acc = stack_ref[0]
        for j in range(1, n_stack):
            acc = acc + stack_ref[j]
        out_ref[...] = acc

    return pl.pallas_call(
        body, name="sum_stack", grid=(n_rows // rows,),
        in_specs=[pl.BlockSpec((n_stack, rows, n_cols), lambda i: (0, i, 0))],
        out_specs=pl.BlockSpec((rows, n_cols), lambda i: (i, 0)),
        out_shape=jax.ShapeDtypeStruct((n_rows, n_cols), F32),
        compiler_params=_params(),
    )(stack)


def _sum_parts(own, recv, *, rows):
    n_rows, n_cols = own.shape
    n_recv = recv.shape[0]

    def body(own_ref, recv_ref, out_ref):
        acc = own_ref[...].astype(F32)
        for j in range(n_recv):
            acc = acc + recv_ref[j].astype(F32)
        out_ref[...] = acc

    return pl.pallas_call(
        body, name="sum_parts", grid=(n_rows // rows,),
        in_specs=[pl.BlockSpec((rows, n_cols), lambda i: (i, 0)), pl.BlockSpec((n_recv, rows, n_cols), lambda i: (0, i, 0))],
        out_specs=pl.BlockSpec((rows, n_cols), lambda i: (i, 0)),
        out_shape=jax.ShapeDtypeStruct(own.shape, F32),
        compiler_params=_params(),
    )(own, recv)


def _swap_cores(arrays):
    nk = len(arrays)

    def body(*refs):
        ins, outs, (send_sems, recv_sems) = refs[:nk], refs[nk:2 * nk], refs[2 * nk:]
        x, y, c = _place()
        sends = [pltpu.make_async_remote_copy(
            src_ref=ins[k], dst_ref=outs[k], send_sem=send_sems.at[k], recv_sem=recv_sems.at[k],
            device_id=(x, y, 1 - c), device_id_type=MESH_ID) for k in range(nk)]
        for cp in sends:
            cp.start()
        for cp in sends:
            cp.wait_recv()
        for cp in sends:
            cp.wait_send()

    return pl.pallas_call(
        body, name="swap_cores", in_specs=[ANY] * nk, out_specs=[ANY] * nk,
        out_shape=[jax.ShapeDtypeStruct(a.shape, a.dtype) for a in arrays],
        scratch_shapes=[pltpu.SemaphoreType.DMA((nk,)), pltpu.SemaphoreType.DMA((nk,))],
    )(*arrays)


class _AllDevices:
    def __init__(self, arrays):
        nk = len(arrays)
        self.arrays = list(arrays)
        self.out_shape = [jax.ShapeDtypeStruct((8,) + a.shape, a.dtype) for a in arrays]
        self.scratch = [pltpu.SemaphoreType.DMA((nk, 7)), pltpu.SemaphoreType.DMA((nk, 7)), pltpu.SemaphoreType.DMA((nk,))]

    def run(self, ins, outs, sems, start):
        send_sems, recv_sems, local_sems = sems
        x, y, c = _place()
        mine = 4 * x + 2 * y + c
        for k in range(len(ins)):
            local = pltpu.make_async_copy(ins[k], outs[k].at[mine], local_sems.at[k])
            local.start() if start else local.wait()
            for flip in range(1, 8):
                px, py, pc = x ^ (flip >> 2), y ^ ((flip >> 1) & 1), c ^ (flip & 1)
                sems_f = dict(send_sem=send_sems.at[k, flip - 1], recv_sem=recv_sems.at[k, flip - 1],
                              device_id=(px, py, pc), device_id_type=MESH_ID)
                send = pltpu.make_async_remote_copy(src_ref=ins[k], dst_ref=outs[k].at[mine], **sems_f)
                if start:
                    send.start()
                else:
                    pltpu.make_async_remote_copy(src_ref=ins[k], dst_ref=outs[k].at[4 * px + 2 * py + pc], **sems_f).wait_recv()
                    send.wait_send()


class _Both:
    def __init__(self, first, second):
        self.rides = (first, second)
        self.arrays = first.arrays + second.arrays
        self.out_shape = first.out_shape + second.out_shape
        self.scratch = first.scratch + second.scratch

    def run(self, ins, outs, sems, start):
        for ride in self.rides:
            n_in, n_out, n_sem = len(ride.arrays), len(ride.out_shape), len(ride.scratch)
            ride.run(ins[:n_in], outs[:n_out], sems[:n_sem], start)
            ins, outs, sems = ins[n_in:], outs[n_out:], sems[n_sem:]


def _pack(arrays, pad_rows=F32_ROWS):
    flat = jnp.concatenate([a.reshape(-1).astype(F32) for a in arrays])
    rows = -(-flat.shape[0] // (LANES * pad_rows)) * pad_rows
    return jnp.pad(flat, (0, rows * LANES - flat.shape[0])).reshape(rows, LANES)


def _unpack(packed, shapes):
    flat, out, off = packed.reshape(-1), [], 0
    for s in shapes:
        size = 1
        for dim in s:
            size *= dim
        out.append(flat[off:off + size].reshape(s))
        off += size
    return out


def _divisor_rows(n_rows, most=256):
    best = None
    for r in range(ACT_ROWS, most + 1, ACT_ROWS):
        if n_rows % r == 0:
            best = r
    return best or n_rows


def kernel(x, meta_tokens, norm_mix_g, norm_ffn_g, final_norm_g, sc_w_in, sc_conv_w, sc_w_out, rg_w_in, rg_conv_w, rg_conv_b, rg_w_gate_a, rg_b_gate_a, rg_w_gate_x, rg_b_gate_x, rg_lambda, rg_w_out, ffn_w_up, ffn_conv_w, ffn_w_down, loss_target, m_meta_tokens, m_norm_mix_g, m_norm_ffn_g, m_final_norm_g, m_sc_w_in, m_sc_conv_w, m_sc_w_out, m_rg_w_in, m_rg_conv_w, m_rg_conv_b, m_rg_w_gate_a, m_rg_b_gate_a, m_rg_w_gate_x, m_rg_b_gate_x, m_rg_lambda, m_rg_w_out, m_ffn_w_up, m_ffn_conv_w, m_ffn_w_down, v_meta_tokens, v_norm_mix_g, v_norm_ffn_g, v_final_norm_g, v_sc_w_in, v_sc_conv_w, v_sc_w_out, v_rg_w_in, v_rg_conv_w, v_rg_conv_b, v_rg_w_gate_a, v_rg_b_gate_a, v_rg_w_gate_x, v_rg_b_gate_x, v_rg_lambda, v_rg_w_out, v_ffn_w_up, v_ffn_conv_w, v_ffn_w_down):
    weights = dict(meta_tokens=meta_tokens, norm_mix_g=norm_mix_g, norm_ffn_g=norm_ffn_g, final_norm_g=final_norm_g, sc_w_in=sc_w_in, sc_conv_w=sc_conv_w, sc_w_out=sc_w_out, rg_w_in=rg_w_in, rg_conv_w=rg_conv_w, rg_conv_b=rg_conv_b, rg_w_gate_a=rg_w_gate_a, rg_b_gate_a=rg_b_gate_a, rg_w_gate_x=rg_w_gate_x, rg_b_gate_x=rg_b_gate_x, rg_lambda=rg_lambda, rg_w_out=rg_w_out, ffn_w_up=ffn_w_up, ffn_conv_w=ffn_conv_w, ffn_w_down=ffn_w_down)
    m_in = dict(meta_tokens=m_meta_tokens, norm_mix_g=m_norm_mix_g, norm_ffn_g=m_norm_ffn_g, final_norm_g=m_final_norm_g, sc_w_in=m_sc_w_in, sc_conv_w=m_sc_conv_w, sc_w_out=m_sc_w_out, rg_w_in=m_rg_w_in, rg_conv_w=m_rg_conv_w, rg_conv_b=m_rg_conv_b, rg_w_gate_a=m_rg_w_gate_a, rg_b_gate_a=m_rg_b_gate_a, rg_w_gate_x=m_rg_w_gate_x, rg_b_gate_x=m_rg_b_gate_x, rg_lambda=m_rg_lambda, rg_w_out=m_rg_w_out, ffn_w_up=m_ffn_w_up, ffn_conv_w=m_ffn_conv_w, ffn_w_down=m_ffn_w_down)
    v_in = dict(meta_tokens=v_meta_tokens, norm_mix_g=v_norm_mix_g, norm_ffn_g=v_norm_ffn_g, final_norm_g=v_final_norm_g, sc_w_in=v_sc_w_in, sc_conv_w=v_sc_conv_w, sc_w_out=v_sc_w_out, rg_w_in=v_rg_w_in, rg_conv_w=v_rg_conv_w, rg_conv_b=v_rg_conv_b, rg_w_gate_a=v_rg_w_gate_a, rg_b_gate_a=v_rg_b_gate_a, rg_w_gate_x=v_rg_w_gate_x, rg_b_gate_x=v_rg_b_gate_x, rg_lambda=v_rg_lambda, rg_w_out=v_rg_w_out, ffn_w_up=v_ffn_w_up, ffn_conv_w=v_ffn_conv_w, ffn_w_down=v_ffn_w_down)
    names = list(weights)

    seq, d = x.shape[1:]
    tm = _row_tile(seq)
    tokens, target = _tile_order(x[0], tm), _tile_order(loss_target[0], tm)
    t_len = seq + tm
    wg_rows = 5 * tm if t_len % (5 * tm) == 0 else tm
    wg_rows_in = 13 * tm if t_len % (13 * tm) == 0 else wg_rows
    xi, yi, _ = _place()
    chip = 2 * xi + yi
    mesh_axes = ("x", "y", "c")

    wire = lambda w: w.astype(WIRE_DT)
    small_sharded = ["meta_tokens", "sc_conv_w", "rg_conv_w", "rg_conv_b", "rg_b_gate_a", "rg_b_gate_x", "rg_lambda", "ffn_conv_w"]
    small_2d = {n: weights[n].reshape(-1, weights[n].shape[-1]) for n in small_sharded}
    w_sc_in, w_sc_out, small_by_chip = _exchange(
        _Gather([wire(sc_w_in[0]), wire(sc_w_out[0]), _pack([small_2d[n] for n in small_sharded])]), "gather_first")
    w_sc_out = w_sc_out.reshape(-1, d)
    gather_ffn0 = _Gather([wire(ffn_w_up[0]), wire(ffn_w_down[0])])
    gather_rest = _Gather([wire(rg_w_in[0]), wire(rg_w_out[0]), wire(ffn_w_up[1]), wire(ffn_w_down[1])])
    small_len = sum(a.size for a in small_2d.values())
    by_chip = small_by_chip.reshape(N_CHIPS, -1)[:, :small_len]
    full, off = {}, 0
    for n in small_sharded:
        rows, width = small_2d[n].shape
        full[n] = by_chip[:, off:off + rows * width].reshape(N_CHIPS, rows, width).transpose(1, 0, 2).reshape(rows, N_CHIPS * width)
        off += rows * width
    sc_cw, rg_cw = full["sc_conv_w"], full["rg_conv_w"]
    ffn_cw = [full["ffn_conv_w"][0:3], full["ffn_conv_w"][3:6]]
    d_rnn = rg_cw.shape[1]
    vec = jnp.concatenate([full["rg_conv_b"], full["rg_b_gate_a"], full["rg_b_gate_x"], full["rg_lambda"],
                           jnp.zeros((F32_ROWS - 4, d_rnn), F32)])
    wa, wx = rg_w_gate_a[0].astype(MXU_DT), rg_w_gate_x[0].astype(MXU_DT)
    first = _tile_order(jnp.concatenate([jnp.zeros((tm - N_META, d), F32), full["meta_tokens"]]), tm)
    g_mix = [norm_mix_g[0:1], norm_mix_g[1:2]]
    g_ffn = [norm_ffn_g[0:1], norm_ffn_g[1:2]]

    h1, hh0, w_up0, w_dn0 = _sc_fwd(tokens, first, g_mix[0], w_sc_in, sc_cw, w_sc_out, tm=tm, ride=gather_ffn0)
    h2, hu0, w_rg_in, w_rg_out, w_up1, w_dn1 = _ffn_fwd(h1, g_ffn[0], w_up0, ffn_cw[0], w_dn0.reshape(-1, d), tm=tm,
                                                         ride=gather_rest)
    w_up, w_dn, w_rg_out = [w_up0, w_up1], [w_dn0.reshape(-1, d), w_dn1.reshape(-1, d)], w_rg_out.reshape(-1, d)
    h3, hh1, hs = _rg_fwd(h2, g_mix[1], w_rg_in, rg_cw, vec, wa, wx, w_rg_out, tm=tm)
    dh4, hu1, sq, d_final = _ffn_fwd(h3, g_ffn[1], w_up[1], ffn_cw[1], w_dn[1], tm=tm,
                                     loss=(target, final_norm_g.reshape(1, d)))
    loss = lax.psum(jnp.sum(sq[0]) * (0.5 / d), mesh_axes)

    def by_chip_rows(pair):
        return [p.reshape(N_CHIPS, -1, d) for p in pair]

    def ffn_backward(dh_out, h_in, hu, layer, ride):
        dh_in, act, dhu, hn, dcw, dg, *landed = _ffn_bwd(dh_out, hu, h_in, g_ffn[layer], w_up[layer], ffn_cw[layer],
                                                         w_dn[layer], tm=tm, ride=ride)
        d_up = _weight_grad(hn, dhu, N_CHIPS, rows=wg_rows_in)
        d_dn = by_chip_rows(_weight_grad(act, dh_out, 1, rows=wg_rows))
        return dh_in, d_up, d_dn, dcw[0:3], dg[0], landed

    dh3, d_up1, d_dn1, d_fcw1, d_gf1, _ = ffn_backward(dh4, h3, hu1, 1, None)
    dh2, dhh1, y_rg, hn_rg, d_vec, d_wa, d_wx, d_gm1, *landed_ffn1 = _rg_bwd(
        dh3, hh1, hs, h2, g_mix[1], w_rg_in, rg_cw, vec, wa, wx, w_rg_out, tm=tm, ride=_Scatter([d_up1[1], d_dn1[1]]))
    d_rg_in = _weight_grad(hn_rg, dhh1, N_CHIPS, rows=wg_rows_in)
    d_rg_out = by_chip_rows(_weight_grad(y_rg, dh3, 1, rows=wg_rows))
    early = {"rg_conv_w": d_vec[G_CONV_W:G_CONV_W + 4], "rg_conv_b": d_vec[G_CONV_B:G_CONV_B + 1],
             "rg_b_gate_a": d_vec[G_B_A:G_B_A + 1], "rg_b_gate_x": d_vec[G_B_X:G_B_X + 1],
             "rg_lambda": d_vec[G_LAMBDA:G_LAMBDA + 1], "ffn_conv_w.1": d_fcw1, "norm_mix_g.1": d_gm1[0:1],
             "norm_ffn_g.1": d_gf1[None], "final_norm_g": d_final[0]}
    early_packed = _pack(list(early.values()))
    gate_names = ["rg_w_gate_a", "rg_w_gate_x"]
    to_all = _AllDevices([early_packed, d_wa.reshape(-1, LANES), d_wx.reshape(-1, LANES)])
    dh1, d_up0, d_dn0, d_fcw0, d_gf0, landed = ffn_backward(
        dh2, h1, hu0, 0, _Both(_Scatter([d_rg_in[1], d_rg_out[1]]), to_all))
    landed_rg, early_by_device, gates_by_device = landed[0:2], landed[2], landed[3:]
    grad_x, dhh0, z_sc, hn_sc, d_sccw, d_gm0, d_first, *landed_ffn0 = _sc_bwd(
        dh1, hh0, tokens, first, g_mix[0], w_sc_in, sc_cw, w_sc_out, tm=tm, ride=_Scatter([d_up0[1], d_dn0[1]]))
    d_sc_in = _weight_grad(hn_sc, dhh0, N_CHIPS, rows=wg_rows_in)
    d_sc_out = by_chip_rows(_weight_grad(z_sc, dh1, 1, rows=wg_rows))
    landed_sc = _exchange(_Scatter([d_sc_in[1], d_sc_out[1]]), "scatter_last")
    grad_x = _time_order(grad_x, tm)[None]

    big = [("sc_w_in", 0, d_sc_in, landed_sc[0]), ("sc_w_out", 0, d_sc_out, landed_sc[1]),
           ("rg_w_in", 0, d_rg_in, landed_rg[0]), ("rg_w_out", 0, d_rg_out, landed_rg[1]),
           ("ffn_w_up", 0, d_up0, landed_ffn0[0]), ("ffn_w_up", 1, d_up1, landed_ffn1[0]),
           ("ffn_w_down", 0, d_dn0, landed_ffn0[1]), ("ffn_w_down", 1, d_dn1, landed_ffn1[1])]
    core_sum = []
    for _, _, (partial, _), received in big:
        own = lax.dynamic_index_in_dim(partial, chip, 0, keepdims=False)
        core_sum.append(_sum_parts(own, received, rows=_divisor_rows(own.shape[0])))
    other_sum = _swap_cores(core_sum)
    out = {k: {} for k in ("grad", "delta", "m", "v")}
    stacked = {}
    for (n, layer, _, _), mine, theirs in zip(big, core_sum, other_sum):
        stacked[n] = _adamw(weights[n], m_in[n], v_in[n], [mine, theirs], rows=_divisor_rows(mine.shape[0]), layer=layer,
                            into=stacked.get(n))
    for n, res in stacked.items():
        for k, key in enumerate(("grad", "delta", "m", "v")):
            out[key][n] = res[k]

    late = {"meta_tokens": _time_order(d_first, tm)[tm - N_META:], "sc_conv_w": d_sccw[0:3], "ffn_conv_w.0": d_fcw0,
            "norm_mix_g.0": d_gm0[0:1], "norm_ffn_g.0": d_gf0[None]}
    late_packed = _pack(list(late.values()))
    late_by_device, = _exchange(_AllDevices([late_packed]), "gather_devices")
    summed = {}
    for parts, packed, by_device in ((early, early_packed, early_by_device), (late, late_packed, late_by_device)):
        total = _sum_stack(by_device, rows=packed.shape[0])
        summed.update(zip(parts, _unpack(total, [p.shape for p in parts.values()])))
    for n in ("ffn_conv_w", "norm_mix_g", "norm_ffn_g"):
        summed[n] = jnp.concatenate([summed.pop(n + ".0"), summed.pop(n + ".1")])
    for n, by_device in zip(gate_names, gates_by_device):
        as_rows = lambda a: a.reshape(1, -1, LANES)
        res = _adamw(as_rows(weights[n]), as_rows(m_in[n]), as_rows(v_in[n]), [_sum_stack(by_device, rows=256)], rows=256)
        for k, key in enumerate(("grad", "delta", "m", "v")):
            out[key][n] = res[k].reshape(weights[n].shape)
    replicated = ["norm_mix_g", "norm_ffn_g", "final_norm_g"]
    small_names = small_sharded + replicated
    grads = {}
    for n in small_sharded:
        width = small_2d[n].shape[1]
        grads[n] = lax.dynamic_slice_in_dim(summed[n], chip * width, width, axis=1).reshape(weights[n].shape)
    for n in replicated:
        grads[n] = summed[n].reshape(weights[n].shape)
    shapes = [weights[n].shape for n in small_names]
    packed_w = _pack([weights[n] for n in small_names])
    res = _adamw(packed_w[None], _pack([m_in[n] for n in small_names])[None], _pack([v_in[n] for n in small_names])[None],
                 [_pack([grads[n] for n in small_names])], rows=packed_w.shape[0])
    for k, key in enumerate(("grad", "delta", "m", "v")):
        out[key].update(dict(zip(small_names, _unpack(res[k][0], shapes))))

    return (loss, grad_x, *[out["grad"][n] for n in names], *[out["delta"][n] for n in names],
            *[out["m"][n] for n in names], *[out["v"][n] for n in names])
```

```python
import functools

import jax
import jax.numpy as jnp
from jax import lax
from jax.experimental import pallas as pl
from jax.experimental.pallas import tpu as pltpu

F32 = jnp.float32
MXU_DT = jnp.bfloat16
ACT_DT = jnp.bfloat16
WIRE_DT = jnp.bfloat16
MESH_ID = pl.DeviceIdType.MESH

N_META = 16
RMS_EPS = 1e-6
RG_C = 8.0
ADAM_LR, ADAM_B1, ADAM_B2, ADAM_EPS, ADAM_WD, ADAM_STEP = 0.001, 0.9, 0.999, 1e-08, 0.01, 10
N_CHIPS = 4
VMEM_LIMIT = 60 * 1024 * 1024
F32_ROWS = 8
ACT_ROWS = 16
LANES = 128


def _row_tile(seq):
    for tm in (256, 128, 64, 32, 16):
        if seq % tm == 0:
            return tm
    raise ValueError(f"sequence length {seq} is not a multiple of 16")


def _params(n_axes=1, **kw):
    return pltpu.CompilerParams(dimension_semantics=("arbitrary",) * n_axes, vmem_limit_bytes=VMEM_LIMIT, **kw)


def _const(shape):
    return pl.BlockSpec(shape, lambda *_: (0,) * len(shape), pipeline_mode=pl.Buffered(1))


def _dot(a, b):
    return jnp.dot(a, b, preferred_element_type=F32)


def _dot_nt(a, b):
    return lax.dot_general(a, b, (((1,), (1,)), ((), ())), preferred_element_type=F32)


def _dot_tn(a, b):
    return lax.dot_general(a, b, (((0,), (0,)), ((), ())), preferred_element_type=F32)


def _sigmoid(x):
    return 1.0 / (1.0 + jnp.exp(-x))


def _rms(h, g):
    rstd = lax.rsqrt(jnp.mean(h * h, axis=-1, keepdims=True) + RMS_EPS)
    xhat = h * rstd
    return xhat * g, xhat, rstd


def _rms_bwd(dhn, xhat, rstd, g):
    dx = dhn * g
    return rstd * (dx - xhat * jnp.mean(dx * xhat, axis=-1, keepdims=True))


def _gelu(x):
    k = 0.7978845608028654
    t = jnp.tanh(k * (x + 0.044715 * x * x * x))
    return 0.5 * x * (1.0 + t), t


def _gelu_grad(x, t):
    k = 0.7978845608028654
    return 0.5 * (1.0 + t) + 0.5 * x * (1.0 - t * t) * k * (1.0 + 3 * 0.044715 * x * x)


def _softplus(x):
    e = jnp.exp(-jnp.abs(x))
    return jnp.maximum(x, 0.0) + jnp.where(e < 1e-4, e - 0.5 * e * e, jnp.log(1.0 + e))


def _expm1_neg(z):
    series = z * (1.0 + z * (0.5 + z * (1.0 / 6)))
    return jnp.where(z > -0.02, series, jnp.exp(z) - 1.0)


def _tile_order(a, tm):
    return a.reshape(-1, F32_ROWS, tm // F32_ROWS, a.shape[-1]).swapaxes(1, 2).reshape(a.shape)


def _time_order(a, tm):
    return a.reshape(-1, tm // F32_ROWS, F32_ROWS, a.shape[-1]).swapaxes(1, 2).reshape(a.shape)


def _valid_rows(tile, tm):
    row = lax.broadcasted_iota(jnp.int32, (tm, 1), 0)
    time = (row & (F32_ROWS - 1)) * (tm // F32_ROWS) + (row >> 3) + tile * tm
    return time >= tm - N_META


def _sublane():
    return lax.broadcasted_iota(jnp.int32, (F32_ROWS, 1), 0)


def _past_rows(width):
    return (width - 1) * F32_ROWS


def _halo_block(past, tm, nt):
    rows = -(-past // ACT_ROWS) * ACT_ROWS
    return rows, lambda i: (jnp.maximum((nt - 1 - i) * (tm // rows) - 1, 0), 0)


def _link_past(buf, cols, width, tm):
    past = _past_rows(width)
    for k in range(1, width):
        rows = pl.ds(past - F32_ROWS * k, F32_ROWS)
        before = pltpu.roll(buf[rows, cols], 1, 0)
        mine = pltpu.roll(buf[pl.ds(past + tm - F32_ROWS * k, F32_ROWS), cols], 1, 0)
        buf[rows, cols] = jnp.where(_sublane() == 0, before, mine)


def _link_future(buf, cols, width, tm):
    for k in range(1, width):
        rows = pl.ds(tm + F32_ROWS * (k - 1), F32_ROWS)
        after = pltpu.roll(buf[rows, cols], F32_ROWS - 1, 0)
        mine = pltpu.roll(buf[pl.ds(F32_ROWS * (k - 1), F32_ROWS), cols], F32_ROWS - 1, 0)
        buf[rows, cols] = jnp.where(_sublane() == F32_ROWS - 1, after, mine)


def _conv_taps(buf, cols, width, tm):
    return [buf[pl.ds(F32_ROWS * k, tm), cols] for k in range(width)]


def _conv_back(buf, cw_ref, cols, width, tm):
    return sum(cw_ref[k:k + 1, cols] * buf[pl.ds(F32_ROWS * (width - 1 - k), tm), cols] for k in range(width))


ANY = pl.BlockSpec(memory_space=pl.ANY)


def _place():
    return lax.axis_index("x"), lax.axis_index("y"), lax.axis_index("c")


def _other_chips(x, y):
    return [(1 - x, y), (x, 1 - y), (1 - x, 1 - y)]


class _Gather:
    def __init__(self, shards):
        nk = len(shards)
        self.arrays = list(shards)
        self.out_shape = [jax.ShapeDtypeStruct((N_CHIPS,) + s.shape, s.dtype) for s in shards]
        self.scratch = [pltpu.SemaphoreType.DMA((nk, 3)), pltpu.SemaphoreType.DMA((nk, 3)), pltpu.SemaphoreType.DMA((nk,))]

    def run(self, ins, outs, sems, start):
        send_sems, recv_sems, local_sems = sems
        x, y, c = _place()
        mine = 2 * x + y
        for k in range(len(ins)):
            local = pltpu.make_async_copy(ins[k], outs[k].at[mine], local_sems.at[k])
            local.start() if start else local.wait()
            for j, (px, py) in enumerate(_other_chips(x, y)):
                sems_kj = dict(send_sem=send_sems.at[k, j], recv_sem=recv_sems.at[k, j], device_id=(px, py, c),
                               device_id_type=MESH_ID)
                send = pltpu.make_async_remote_copy(src_ref=ins[k], dst_ref=outs[k].at[mine], **sems_kj)
                if start:
                    send.start()
                else:
                    pltpu.make_async_remote_copy(src_ref=ins[k], dst_ref=outs[k].at[2 * px + py], **sems_kj).wait_recv()
                    send.wait_send()


class _Scatter:
    def __init__(self, parts):
        nk = len(parts)
        self.arrays = list(parts)
        self.out_shape = [jax.ShapeDtypeStruct((3,) + p.shape[1:], p.dtype) for p in parts]
        self.scratch = [pltpu.SemaphoreType.DMA((nk, 3)), pltpu.SemaphoreType.DMA((nk, 3))]

    def run(self, ins, outs, sems, start):
        send_sems, recv_sems = sems
        x, y, c = _place()
        for k in range(len(ins)):
            for j, (px, py) in enumerate(_other_chips(x, y)):
                send = pltpu.make_async_remote_copy(
                    src_ref=ins[k].at[2 * px + py], dst_ref=outs[k].at[j], send_sem=send_sems.at[k, j],
                    recv_sem=recv_sems.at[k, j], device_id=(px, py, c), device_id_type=MESH_ID)
                if start:
                    send.start()
                else:
                    send.wait_recv()
                    send.wait_send()


def _exchange(ride, name):
    n_in, n_out = len(ride.arrays), len(ride.out_shape)

    def body(*refs):
        ride.run(refs[:n_in], refs[n_in:n_in + n_out], refs[n_in + n_out:], start=True)
        ride.run(refs[:n_in], refs[n_in:n_in + n_out], refs[n_in + n_out:], start=False)

    return pl.pallas_call(body, name=name, in_specs=[ANY] * n_in, out_specs=[ANY] * n_out, out_shape=ride.out_shape,
                          scratch_shapes=ride.scratch)(*ride.arrays)


def _launch(body, operands, *, name, grid, in_specs, out_specs, out_shape, scratch_shapes=(), ride=None):
    common = dict(name=name, grid=grid, compiler_params=_params(len(grid)))
    if ride is None:
        return pl.pallas_call(body, in_specs=in_specs, out_specs=out_specs, out_shape=out_shape,
                              scratch_shapes=list(scratch_shapes), **common)(*operands)
    n_in, n_out, n_scr = len(operands), len(out_shape), len(scratch_shapes)
    r_in, r_out = len(ride.arrays), len(ride.out_shape)

    def riding(*refs):
        ins, refs = refs[:n_in], refs[n_in:]
        r_ins, refs = refs[:r_in], refs[r_in:]
        outs, refs = refs[:n_out], refs[n_out:]
        r_outs, refs = refs[:r_out], refs[r_out:]
        scr, r_sems = refs[:n_scr], refs[n_scr:]
        step = [pl.program_id(axis) for axis in range(len(grid))]
        first = functools.reduce(jnp.logical_and, [s == 0 for s in step])
        last = functools.reduce(jnp.logical_and, [s == size - 1 for s, size in zip(step, grid)])

        @pl.when(first)
        def _():
            ride.run(r_ins, r_outs, r_sems, start=True)

        body(*ins, *outs, *scr)

        @pl.when(last)
        def _():
            ride.run(r_ins, r_outs, r_sems, start=False)

    return pl.pallas_call(
        riding, in_specs=list(in_specs) + [ANY] * r_in, out_specs=list(out_specs) + [ANY] * r_out,
        out_shape=list(out_shape) + ride.out_shape, scratch_shapes=list(scratch_shapes) + ride.scratch, **common,
    )(*operands, *ride.arrays)


def _sc_fwd(x, first, g, w_in, cw, w_out, *, tm, ride=None):
    seq, d = x.shape
    nt = seq // tm + 1
    nq, _, n = w_in.shape
    width = cw.shape[0]
    past = _past_rows(width)

    def body(x_ref, first_ref, g_ref, win_ref, cw_ref, wout_ref, h1_ref, hh_ref, hh_scr, cbuf):
        i = pl.program_id(0)

        @pl.when(i == 0)
        def _():
            cbuf[pl.ds(0, past), :] = jnp.zeros((past, d), F32)

        h = jnp.where(i == 0, first_ref[...], x_ref[...])
        hn = _rms(h, g_ref[...])[0].astype(MXU_DT)
        for q in range(nq):
            hh_scr[:, q * n:(q + 1) * n] = _dot(hn, win_ref[q])
        hh_ref[...] = hh_scr[...].astype(hh_ref.dtype)
        b = hh_scr[:, 0:d]
        cbuf[pl.ds(past, tm), :] = hh_scr[:, d:2 * d] * hh_scr[:, 2 * d:3 * d]
        last = cbuf[pl.ds(tm, past), :]
        _link_past(cbuf, slice(None), width, tm)
        u = sum(cw_ref[k:k + 1, :] * tap for k, tap in enumerate(_conv_taps(cbuf, slice(None), width, tm)))
        cbuf[pl.ds(0, past), :] = last
        h1_ref[...] = h + _dot((b * u).astype(MXU_DT), wout_ref[...])

    return _launch(
        body, [x, first, g, w_in, cw, w_out], name="sc_fwd", grid=(nt,),
        in_specs=[pl.BlockSpec((tm, d), lambda i: (jnp.maximum(i - 1, 0), 0)), _const((tm, d)), _const((1, d)),
                  _const(w_in.shape), _const(cw.shape), _const(w_out.shape)],
        out_specs=[pl.BlockSpec((tm, d), lambda i: (i, 0)), pl.BlockSpec((tm, nq * n), lambda i: (i, 0))],
        out_shape=[jax.ShapeDtypeStruct((nt * tm, d), F32), jax.ShapeDtypeStruct((nt * tm, nq * n), ACT_DT)],
        scratch_shapes=[pltpu.VMEM((tm, nq * n), F32), pltpu.VMEM((past + tm, d), F32)],
        ride=ride,
    )


def _sc_bwd(dh, hh, x, first, g, w_in, cw, w_out, *, tm, ride=None):
    t_len, d = dh.shape
    nt = t_len // tm
    nq, _, n = w_in.shape
    width = cw.shape[0]
    past = _past_rows(width)
    halo_rows, halo_index = _halo_block(past, tm, nt)

    def body(dh_ref, hh_ref, hhp_ref, x_ref, first_ref, g_ref, win_ref, cw_ref, wout_ref,
             dx_ref, dhh_ref, z_ref, hn_ref, dcw_ref, dg_ref, dfirst_ref, cbuf, dbuf):
        i = pl.program_id(0)
        r = nt - 1 - i

        @pl.when(i == 0)
        def _():
            dbuf[pl.ds(tm, past), :] = jnp.zeros((past, d), F32)
            dcw_ref[...] = jnp.zeros_like(dcw_ref)
            dg_ref[...] = jnp.zeros_like(dg_ref)

        dh_out = dh_ref[...]
        b = hh_ref[:, 0:d].astype(F32)
        c = hh_ref[:, d:2 * d].astype(F32)
        v = hh_ref[:, 2 * d:3 * d].astype(F32)
        prev = hhp_ref[...].astype(F32)[halo_rows - past:, :]
        cbuf[pl.ds(0, past), :] = jnp.where(r > 0, prev[:, d:2 * d] * prev[:, 2 * d:3 * d], 0.0)
        cbuf[pl.ds(past, tm), :] = c * v
        _link_past(cbuf, slice(None), width, tm)
        taps = _conv_taps(cbuf, slice(None), width, tm)
        u = sum(cw_ref[k:k + 1, :] * taps[k] for k in range(width))
        z_ref[...] = (b * u).astype(z_ref.dtype)
        dz = _dot_nt(dh_out.astype(MXU_DT), wout_ref[...])
        dhh_ref[:, 0:d] = (dz * u).astype(dhh_ref.dtype)
        du = dz * b
        for k in range(width):
            dcw_ref[k:k + 1, :] += jnp.sum(taps[k] * du, axis=0, keepdims=True)
        dbuf[pl.ds(0, tm), :] = du
        _link_future(dbuf, slice(None), width, tm)
        dcv = _conv_back(dbuf, cw_ref, slice(None), width, tm)
        dbuf[pl.ds(tm, past), :] = dbuf[pl.ds(0, past), :]
        dhh_ref[:, d:2 * d] = (dcv * v).astype(dhh_ref.dtype)
        dhh_ref[:, 2 * d:3 * d] = (dcv * c).astype(dhh_ref.dtype)
        dhn = sum(_dot_nt(dhh_ref[:, q * n:(q + 1) * n], win_ref[q]) for q in range(nq))
        h_in = jnp.where(r == 0, first_ref[...], x_ref[...])
        dh_in = _norm_bwd_tile(dhn, h_in, dh_out, g_ref[...], _valid_rows(r, tm), hn_ref, dg_ref)

        @pl.when(r == 0)
        def _():
            dfirst_ref[...] = dh_in

        @pl.when(r > 0)
        def _():
            dx_ref[...] = dh_in

    rev = lambda i: (nt - 1 - i, 0)
    rev_x = lambda i: (jnp.maximum(nt - 2 - i, 0), 0)
    return _launch(
        body, [dh, hh, hh, x, first, g, w_in, cw, w_out], name="sc_bwd", grid=(nt,),
        in_specs=[pl.BlockSpec((tm, d), rev), pl.BlockSpec((tm, 3 * d), rev), pl.BlockSpec((halo_rows, 3 * d), halo_index),
                  pl.BlockSpec((tm, d), rev_x), _const((tm, d)), _const((1, d)), _const(w_in.shape), _const(cw.shape),
                  _const(w_out.shape)],
        out_specs=[pl.BlockSpec((tm, d), rev_x), pl.BlockSpec((tm, 3 * d), rev), pl.BlockSpec((tm, d), rev),
                   pl.BlockSpec((tm, d), rev), _const((F32_ROWS, d)), _const((F32_ROWS, d)), _const((tm, d))],
        out_shape=[jax.ShapeDtypeStruct((t_len - tm, d), F32), jax.ShapeDtypeStruct((t_len, 3 * d), ACT_DT),
                   jax.ShapeDtypeStruct((t_len, d), ACT_DT), jax.ShapeDtypeStruct((t_len, d), ACT_DT),
                   jax.ShapeDtypeStruct((F32_ROWS, d), F32), jax.ShapeDtypeStruct((F32_ROWS, d), F32),
                   jax.ShapeDtypeStruct((tm, d), F32)],
        scratch_shapes=[pltpu.VMEM((past + tm, d), F32), pltpu.VMEM((tm + past, d), F32)],
        ride=ride,
    )


def _ffn_fwd(h, g, w_up, cw, w_down, *, tm, ride=None, loss=None):
    t_len, d = h.shape
    nt = t_len // tm
    nq, _, n = w_up.shape
    width = cw.shape[0]
    past = _past_rows(width)

    def body(h_ref, g_ref, wup_ref, cw_ref, wdn_ref, *rest):
        if loss is None:
            out_ref, hu_ref, ubuf, tail = rest
        else:
            t_ref, gf_ref, out_ref, hu_ref, sq_ref, dgf_ref, ubuf, tail = rest
        i = pl.program_id(0)

        @pl.when(i == 0)
        def _():
            tail[...] = jnp.zeros_like(tail)

        h_in = h_ref[...]
        hn = _rms(h_in, g_ref[...])[0].astype(MXU_DT)
        ubuf[pl.ds(0, past), :] = tail[...]
        for q in range(nq):
            ubuf[pl.ds(past, tm), q * n:(q + 1) * n] = _dot(hn, wup_ref[q])
        hu_ref[...] = ubuf[pl.ds(past, tm), :].astype(hu_ref.dtype)
        tail[...] = ubuf[pl.ds(tm, past), :]
        _link_past(ubuf, slice(None), width, tm)
        acc = h_in
        for j in range(nq // 2):
            gcol, vcol = slice(j * n, (j + 1) * n), slice((nq // 2 + j) * n, (nq // 2 + j + 1) * n)
            conv = lambda cols: sum(cw_ref[k:k + 1, cols] * tap for k, tap in enumerate(_conv_taps(ubuf, cols, width, tm)))
            gj, vj = conv(gcol), conv(vcol)
            acc = acc + _dot((gj * _sigmoid(gj) * vj).astype(MXU_DT), wdn_ref[j * n:(j + 1) * n, :])
        if loss is None:
            out_ref[...] = acc
            return

        @pl.when(i == 0)
        def _():
            sq_ref[...] = jnp.zeros_like(sq_ref)
            dgf_ref[...] = jnp.zeros_like(dgf_ref)
            out_ref[...] = jnp.zeros_like(out_ref)

        @pl.when(i > 0)
        def _():
            gain = gf_ref[...]
            out, xhat, rstd = _rms(acc, gain)
            err = out - t_ref[...]
            sq_ref[0:1, :] += jnp.sum(err * err, axis=0, keepdims=True)
            dout = err * (1.0 / d)
            dgf_ref[0:1, :] += jnp.sum(dout * xhat, axis=0, keepdims=True)
            out_ref[...] = _rms_bwd(dout, xhat, rstd, gain)

    row = lambda i: (i, 0)
    stat = jax.ShapeDtypeStruct((F32_ROWS, d), F32)
    return _launch(
        body, [h, g, w_up, cw, w_down] + list(loss or ()), name="ffn_fwd", grid=(nt,),
        in_specs=[pl.BlockSpec((tm, d), row), _const((1, d)), _const(w_up.shape), _const(cw.shape), _const(w_down.shape)]
        + ([pl.BlockSpec((tm, d), lambda i: (jnp.maximum(i - 1, 0), 0)), _const((1, d))] if loss else []),
        out_specs=[pl.BlockSpec((tm, d), row), pl.BlockSpec((tm, nq * n), row)] + ([_const(stat.shape)] * 2 if loss else []),
        out_shape=[jax.ShapeDtypeStruct((t_len, d), F32), jax.ShapeDtypeStruct((t_len, nq * n), ACT_DT)]
        + ([stat, stat] if loss else []),
        scratch_shapes=[pltpu.VMEM((past + tm, nq * n), F32), pltpu.VMEM((past, nq * n), F32)],
        ride=ride,
    )


def _norm_bwd_tile(dhn, h_in, dh, gain, valid, hn_ref, dg_ref):
    hn, xhat, rstd = _rms(h_in, gain)
    hn_ref[...] = hn.astype(hn_ref.dtype)
    dg_ref[0:1, :] += jnp.sum(dhn * xhat, axis=0, keepdims=True)
    return jnp.where(valid, dh + _rms_bwd(dhn, xhat, rstd, gain), 0.0)


def _ffn_bwd(dh, hu, h, g, w_up, cw, w_down, *, tm, ride=None):
    t_len, d = dh.shape
    nt = t_len // tm
    ff = hu.shape[1]
    n = ff // 4
    width = cw.shape[0]
    past = _past_rows(width)
    halo_rows, halo_index = _halo_block(past, tm, nt)

    def body(dh_ref, hu_ref, hup_ref, h_ref, g_ref, wup_ref, cw_ref, wdn_ref,
             dhin_ref, a_ref, dhu_ref, hn_ref, dcw_ref, dg_ref, ubuf, dbuf, head):
        i = pl.program_id(0)
        r = nt - 1 - i

        @pl.when(i == 0)
        def _():
            head[...] = jnp.zeros_like(head)
            dcw_ref[...] = jnp.zeros_like(dcw_ref)
            dg_ref[...] = jnp.zeros_like(dg_ref)

        dh_out = dh_ref[...]
        dhb = dh_out.astype(MXU_DT)
        dhn_parts = []
        d_act =[_dot_nt(dhb, wdn_ref[j * n:(j + 1) * n, :]) for j in range(2)]
        for j in range(2):
            mine = slice(0, n), slice(n, 2 * n)
            full = slice(j * n, (j + 1) * n), slice((2 + j) * n, (3 + j) * n)
            for here, there in zip(mine, full):
                prev = hup_ref[:, there].astype(F32)[halo_rows - past:, :]
                ubuf[pl.ds(0, past), here] = jnp.where(r > 0, prev, 0.0)
                ubuf[pl.ds(past, tm), here] = hu_ref[:, there].astype(F32)
                dbuf[pl.ds(tm, past), here] = head[:, there]
            _link_past(ubuf, slice(None), width, tm)
            conv = lambda here, there: sum(cw_ref[k:k + 1, there] * tap
                                           for k, tap in enumerate(_conv_taps(ubuf, here, width, tm)))
            gj, vj = conv(mine[0], full[0]), conv(mine[1], full[1])
            sg = _sigmoid(gj)
            s = gj * sg
            a_ref[:, full[0]] = (s * vj).astype(a_ref.dtype)
            da = d_act[j]
            dbuf[pl.ds(0, tm), mine[1]] = da * s
            dbuf[pl.ds(0, tm), mine[0]] = da * vj * (sg * (1.0 + gj * (1.0 - sg)))
            for here, there in zip(mine, full):
                head[:, there] = dbuf[pl.ds(0, past), here]
            _link_future(dbuf, slice(None), width, tm)
            for here, there in zip(mine, full):
                dy = dbuf[pl.ds(0, tm), here]
                for k, tap in enumerate(_conv_taps(ubuf, here, width, tm)):
                    dcw_ref[k:k + 1, there] += jnp.sum(tap * dy, axis=0, keepdims=True)
                dhu = sum(cw_ref[k:k + 1, there] * dbuf[pl.ds(F32_ROWS * (width - 1 - k), tm), here]
                          for k in range(width)).astype(dhu_ref.dtype)
                dhu_ref[:, there] = dhu
                dhn_parts.append(_dot_nt(dhu, wup_ref[there.start // n]))
        dhn = (dhn_parts[0] + dhn_parts[1]) + (dhn_parts[2] + dhn_parts[3])
        dhin_ref[...] = _norm_bwd_tile(dhn, h_ref[...], dh_out, g_ref[...], _valid_rows(r, tm), hn_ref, dg_ref)

    rev = lambda i: (nt - 1 - i, 0)
    return _launch(
        body, [dh, hu, hu, h, g, w_up, cw, w_down], name="ffn_bwd", grid=(nt,),
        in_specs=[pl.BlockSpec((tm, d), rev), pl.BlockSpec((tm, ff), rev), pl.BlockSpec((halo_rows, ff), halo_index),
                  pl.BlockSpec((tm, d), rev), _const((1, d)), _const(w_up.shape), _const(cw.shape), _const(w_down.shape)],
        out_specs=[pl.BlockSpec((tm, d), rev), pl.BlockSpec((tm, 2 * n), rev), pl.BlockSpec((tm, ff), rev),
                   pl.BlockSpec((tm, d), rev), _const((F32_ROWS, ff)), _const((F32_ROWS, d))],
        out_shape=[jax.ShapeDtypeStruct((t_len, d), F32), jax.ShapeDtypeStruct((t_len, 2 * n), ACT_DT),
                   jax.ShapeDtypeStruct((t_len, ff), ACT_DT), jax.ShapeDtypeStruct((t_len, d), ACT_DT),
                   jax.ShapeDtypeStruct((F32_ROWS, ff), F32), jax.ShapeDtypeStruct((F32_ROWS, d), F32)],
        scratch_shapes=[pltpu.VMEM((past + tm, 2 * n), F32), pltpu.VMEM((tm + past, 2 * n), F32), pltpu.VMEM((past, ff), F32)],
        ride=ride,
    )


V_CONV_B, V_B_A, V_B_X, V_LAMBDA = 0, 1, 2, 3
G_CONV_W, G_CONV_B, G_B_A, G_B_X, G_LAMBDA = 0, 4, 5, 6, 7


def _scan(a_ref, b_ref, edge, tm, reverse):
    nj = tm // F32_ROWS
    order = range(nj - 1, -1, -1) if reverse else range(nj)
    slab = lambda ref, j: ref[pl.ds(F32_ROWS * j, F32_ROWS), :]
    a_run = b_run = None
    for j in order:
        a_j, b_j = slab(a_ref, j), slab(b_ref, j)
        if a_run is not None:
            b_j = b_j + a_j * b_run
            a_j = a_j * a_run
            b_ref[pl.ds(F32_ROWS * j, F32_ROWS), :] = b_j
            a_ref[pl.ds(F32_ROWS * j, F32_ROWS), :] = a_j
        a_run, b_run = a_j, b_j
    sub = _sublane()
    shift = 1
    while shift < F32_ROWS:
        amount = F32_ROWS - shift if reverse else shift
        keep = (sub < F32_ROWS - shift) if reverse else (sub >= shift)
        b_run = jnp.where(keep, b_run + a_run * pltpu.roll(b_run, amount, 0), b_run)
        a_run = jnp.where(keep, a_run * pltpu.roll(a_run, amount, 0), a_run)
        shift *= 2
    outer = edge[0:1, :] if reverse else edge[F32_ROWS - 1:F32_ROWS, :]
    ends = b_run + a_run * outer
    if reverse:
        carry = jnp.where(sub == F32_ROWS - 1, outer, pltpu.roll(ends, F32_ROWS - 1, 0))
    else:
        carry = jnp.where(sub == 0, outer, pltpu.roll(ends, 1, 0))
    for j in range(nj):
        b_ref[pl.ds(F32_ROWS * j, F32_ROWS), :] = slab(b_ref, j) + slab(a_ref, j) * carry
    return slab(b_ref, 0 if reverse else nj - 1)


def _rg_gates(u, vec_ref, wa_ref, wx_ref, pre_scr, nb, bd):
    ub = u.astype(MXU_DT)
    for k in range(nb):
        blk = slice(k * bd, (k + 1) * bd)
        pre_scr[0, :, blk] = _dot(ub[:, blk], wa_ref[k])
        pre_scr[1, :, blk] = _dot(ub[:, blk], wx_ref[k])
    r_gate = _sigmoid(pre_scr[0] + vec_ref[V_B_A:V_B_A + 1, :])
    i_gate = _sigmoid(pre_scr[1] + vec_ref[V_B_X:V_B_X + 1, :])
    sp = _softplus(-vec_ref[V_LAMBDA:V_LAMBDA + 1, :])
    log_a = -RG_C * r_gate * sp
    a = jnp.exp(log_a)
    one_minus_a2 = jnp.maximum(-_expm1_neg(2.0 * log_a), 1e-30)
    inv_mult = lax.rsqrt(one_minus_a2)
    return r_gate, i_gate, a, one_minus_a2 * inv_mult, inv_mult, sp, ub


def _rg_fwd(h, g, w_in, cw, vec, wa, wx, w_out, *, tm):
    t_len, d = h.shape
    nt = t_len // tm
    nq, _, n = w_in.shape
    dr = 2 * n
    width = cw.shape[0]
    past = _past_rows(width)
    nb, bd, _ = wa.shape

    def body(h_ref, g_ref, win_ref, cw_ref, vec_ref, wa_ref, wx_ref, wout_ref, out_ref, hh_ref, hs_ref,
             gbuf, rbuf, pre_scr, tail, edge):
        i = pl.program_id(0)

        @pl.when(i == 0)
        def _():
            tail[...] = jnp.zeros_like(tail)
            edge[...] = jnp.zeros_like(edge)

        h_in = h_ref[...]
        hn = _rms(h_in, g_ref[...])[0].astype(MXU_DT)
        rbuf[pl.ds(0, past), :] = tail[...]
        for q in range(2):
            gbuf[:, q * n:(q + 1) * n] = _dot(hn, win_ref[q])
            rbuf[pl.ds(past, tm), q * n:(q + 1) * n] = _dot(hn, win_ref[2 + q])
        hh_ref[:, 0:dr] = gbuf[...].astype(hh_ref.dtype)
        hh_ref[:, dr:2 * dr] = rbuf[pl.ds(past, tm), :].astype(hh_ref.dtype)
        tail[...] = rbuf[pl.ds(tm, past), :]
        _link_past(rbuf, slice(None), width, tm)
        taps = _conv_taps(rbuf, slice(None), width, tm)
        u = sum(cw_ref[k:k + 1, :] * taps[k] for k in range(width)) + vec_ref[V_CONV_B:V_CONV_B + 1, :]
        _, i_gate, a, mult, _, _, _ = _rg_gates(u, vec_ref, wa_ref, wx_ref, pre_scr, nb, bd)
        pre_scr[0] = a
        pre_scr[1] = jnp.where(_valid_rows(i, tm), mult * (i_gate * u), 0.0)
        edge[...] = _scan(pre_scr.at[0], pre_scr.at[1], edge[...], tm, reverse=False)
        hs = pre_scr[1]
        hs_ref[...] = hs
        y = hs * _gelu(gbuf[...])[0]
        out_ref[...] = h_in + _dot(y.astype(MXU_DT), wout_ref[...])

    row = lambda i: (i, 0)
    return pl.pallas_call(
        body, name="rg_fwd", grid=(nt,),
        in_specs=[pl.BlockSpec((tm, d), row), _const((1, d)), _const(w_in.shape), _const(cw.shape), _const(vec.shape),
                  _const(wa.shape), _const(wx.shape), _const(w_out.shape)],
        out_specs=[pl.BlockSpec((tm, d), row), pl.BlockSpec((tm, 2 * dr), row), pl.BlockSpec((tm, dr), row)],
        out_shape=[jax.ShapeDtypeStruct((t_len, d), F32), jax.ShapeDtypeStruct((t_len, 2 * dr), ACT_DT),
                   jax.ShapeDtypeStruct((t_len, dr), F32)],
        scratch_shapes=[pltpu.VMEM((tm, dr), F32), pltpu.VMEM((past + tm, dr), F32), pltpu.VMEM((2, tm, dr), F32),
                        pltpu.VMEM((past, dr), F32), pltpu.VMEM((F32_ROWS, dr), F32)],
        compiler_params=_params(),
    )(h, g, w_in, cw, vec, wa, wx, w_out)


def _rg_bwd(dh, hh, hs, h, g, w_in, cw, vec, wa, wx, w_out, *, tm, ride=None):
    t_len, d = dh.shape
    nt = t_len // tm
    dr = hs.shape[1]
    n = w_in.shape[2]
    width = cw.shape[0]
    nb, bd, _ = wa.shape
    past = _past_rows(width)
    halo_rows, halo_index = _halo_block(past, tm, nt)
    one = F32_ROWS

    def body(dh_ref, hh_ref, hhp_ref, hs_ref, hsp_ref, h_ref, g_ref, win_ref, cw_ref, vec_ref, wa_ref, wx_ref, wout_ref,
             dhin_ref, dhh_ref, y_ref, hn_ref, dvec_ref, dwa_ref, dwx_ref, dg_ref, rbuf, dbuf, pre_scr, hbuf, abuf, edge):
        i = pl.program_id(0)
        r = nt - 1 - i

        @pl.when(i == 0)
        def _():
            dbuf[pl.ds(tm, past), :] = jnp.zeros((past, dr), F32)
            abuf[pl.ds(tm, one), :] = jnp.zeros((one, dr), F32)
            edge[...] = jnp.zeros_like(edge)
            dvec_ref[...] = jnp.zeros_like(dvec_ref)
            dwa_ref[...] = jnp.zeros_like(dwa_ref)
            dwx_ref[...] = jnp.zeros_like(dwx_ref)
            dg_ref[...] = jnp.zeros_like(dg_ref)

        dh_out = dh_ref[...]
        gb = hh_ref[:, 0:dr].astype(F32)
        prev = hhp_ref[...].astype(F32)[halo_rows - past:, dr:2 * dr]
        rbuf[pl.ds(0, past), :] = jnp.where(r > 0, prev, 0.0)
        rbuf[pl.ds(past, tm), :] = hh_ref[:, dr:2 * dr].astype(F32)
        _link_past(rbuf, slice(None), width, tm)
        taps = _conv_taps(rbuf, slice(None), width, tm)
        u = sum(cw_ref[k:k + 1, :] * taps[k] for k in range(width)) + vec_ref[V_CONV_B:V_CONV_B + 1, :]
        r_gate, i_gate, a, mult, inv_mult, sp, ub = _rg_gates(u, vec_ref, wa_ref, wx_ref, pre_scr, nb, bd)
        hs_t = hs_ref[...]
        hbuf[pl.ds(0, one), :] = jnp.where(r > 0, hsp_ref[...], 0.0)
        hbuf[pl.ds(one, tm), :] = hs_t
        _link_past(hbuf, slice(None), 2, tm)
        h_prev = hbuf[pl.ds(0, tm), :]
        gate, th = _gelu(gb)
        y_ref[...] = (hs_t * gate).astype(y_ref.dtype)
        dy = _dot_nt(dh_out.astype(MXU_DT), wout_ref[...])
        d_gb = (dy * hs_t * _gelu_grad(gb, th)).astype(dhh_ref.dtype)
        dhh_ref[:, 0:dr] = d_gb
        dhn = sum(_dot_nt(d_gb[:, q * n:(q + 1) * n], win_ref[q]) for q in range(2))
        abuf[pl.ds(0, tm), :] = a
        _link_future(abuf, slice(None), 2, tm)
        pre_scr[0] = abuf[pl.ds(one, tm), :]
        pre_scr[1] = dy * gate
        edge[...] = _scan(pre_scr.at[0], pre_scr.at[1], edge[...], tm, reverse=True)
        abuf[pl.ds(tm, one), :] = abuf[pl.ds(0, one), :]
        d_hs = pre_scr[1]
        d_b = jnp.where(_valid_rows(r, tm), d_hs, 0.0)
        d_iu = d_b * mult
        d_log_a = d_hs * h_prev * a - d_b * (i_gate * u) * (a * a) * inv_mult
        dvec_ref[G_LAMBDA:G_LAMBDA + 1, :] += jnp.sum(d_log_a * r_gate, axis=0, keepdims=True) * (-RG_C)
        d_pre_r = d_log_a * (-RG_C * sp) * r_gate * (1.0 - r_gate)
        d_pre_i = d_iu * u * i_gate * (1.0 - i_gate)
        dvec_ref[G_B_A:G_B_A + 1, :] += jnp.sum(d_pre_r, axis=0, keepdims=True)
        dvec_ref[G_B_X:G_B_X + 1, :] += jnp.sum(d_pre_i, axis=0, keepdims=True)
        dbuf[pl.ds(0, tm), :] = d_iu * i_gate
        d_pre_r = d_pre_r.astype(MXU_DT)
        d_pre_i = d_pre_i.astype(MXU_DT)
        for k in range(nb):
            blk = slice(k * bd, (k + 1) * bd)
            dwa_ref[k] += _dot_tn(ub[:, blk], d_pre_r[:, blk])
            dwx_ref[k] += _dot_tn(ub[:, blk], d_pre_i[:, blk])
            dbuf[pl.ds(0, tm), blk] += _dot_nt(d_pre_r[:, blk], wa_ref[k]) + _dot_nt(d_pre_i[:, blk], wx_ref[k])
        du = dbuf[pl.ds(0, tm), :]
        dvec_ref[G_CONV_B:G_CONV_B + 1, :] += jnp.sum(du, axis=0, keepdims=True)
        for k in range(width):
            dvec_ref[G_CONV_W + k:G_CONV_W + k + 1, :] += jnp.sum(taps[k] * du, axis=0, keepdims=True)
        _link_future(dbuf, slice(None), width, tm)
        d_rb = _conv_back(dbuf, cw_ref, slice(None), width, tm)
        dbuf[pl.ds(tm, past), :] = dbuf[pl.ds(0, past), :]
        d_rb = d_rb.astype(dhh_ref.dtype)
        dhh_ref[:, dr:2 * dr] = d_rb
        dhn = dhn + sum(_dot_nt(d_rb[:, q * n:(q + 1) * n], win_ref[2 + q]) for q in range(2))
        dhin_ref[...] = _norm_bwd_tile(dhn, h_ref[...], dh_out, g_ref[...], _valid_rows(r, tm), hn_ref, dg_ref)

        @pl.when(i == nt - 1)
        def _():
            lam = vec_ref[V_LAMBDA:V_LAMBDA + 1, :]
            dvec_ref[G_LAMBDA:G_LAMBDA + 1, :] = dvec_ref[G_LAMBDA:G_LAMBDA + 1, :] * (-_sigmoid(-lam))

    rev = lambda i: (nt - 1 - i, 0)
    return _launch(
        body, [dh, hh, hh, hs, hs, h, g, w_in, cw, vec, wa, wx, w_out], name="rg_bwd", grid=(nt,),
        in_specs=[pl.BlockSpec((tm, d), rev), pl.BlockSpec((tm, 2 * dr), rev), pl.BlockSpec((halo_rows, 2 * dr), halo_index),
                  pl.BlockSpec((tm, dr), rev),
                  pl.BlockSpec((one, dr), lambda i: (jnp.maximum((nt - 1 - i) * (tm // one) - 1, 0), 0)),
                  pl.BlockSpec((tm, d), rev), _const((1, d)), _const(w_in.shape),
                  _const(cw.shape), _const(vec.shape), _const(wa.shape), _const(wx.shape), _const(w_out.shape)],
        out_specs=[pl.BlockSpec((tm, d), rev), pl.BlockSpec((tm, 2 * dr), rev), pl.BlockSpec((tm, dr), rev),
                   pl.BlockSpec((tm, d), rev), _const((F32_ROWS, dr)), _const(wa.shape), _const(wx.shape),
                   _const((F32_ROWS, d))],
        out_shape=[jax.ShapeDtypeStruct((t_len, d), F32), jax.ShapeDtypeStruct((t_len, 2 * dr), ACT_DT),
                   jax.ShapeDtypeStruct((t_len, dr), ACT_DT), jax.ShapeDtypeStruct((t_len, d), ACT_DT),
                   jax.ShapeDtypeStruct((F32_ROWS, dr), F32), jax.ShapeDtypeStruct(wa.shape, F32),
                   jax.ShapeDtypeStruct(wx.shape, F32), jax.ShapeDtypeStruct((F32_ROWS, d), F32)],
        scratch_shapes=[pltpu.VMEM((past + tm, dr), F32), pltpu.VMEM((tm + past, dr), F32), pltpu.VMEM((2, tm, dr), F32),
                        pltpu.VMEM((one + tm, dr), F32), pltpu.VMEM((tm + one, dr), F32), pltpu.VMEM((F32_ROWS, dr), F32)],
        ride=ride,
    )


def _weight_grad(a, b, nb, *, rows, ride=None):
    t_len, k_dim = a.shape
    n = b.shape[1] // nb
    nt = t_len // rows

    def body(a_ref, b_ref, out_ref, wire_ref):
        @pl.when(pl.program_id(1) == 0)
        def _():
            out_ref[...] = jnp.zeros_like(out_ref)

        out_ref[0] += _dot_tn(a_ref[...].astype(MXU_DT), b_ref[...].astype(MXU_DT))

        @pl.when(pl.program_id(1) == nt - 1)
        def _():
            wire_ref[...] = out_ref[...].astype(wire_ref.dtype)

    block = pl.BlockSpec((1, k_dim, n), lambda j, i: (j, 0, 0))
    return _launch(
        body, [a, b], name="weight_grad", grid=(nb, nt),
        in_specs=[pl.BlockSpec((rows, k_dim), lambda j, i: (i, 0)), pl.BlockSpec((rows, n), lambda j, i: (i, j))],
        out_specs=[block, block],
        out_shape=[jax.ShapeDtypeStruct((nb, k_dim, n), F32), jax.ShapeDtypeStruct((nb, k_dim, n), WIRE_DT)],
        ride=ride,
    )


def _adamw(w, m, v, parts, *, rows, layer=0, into=None):
    n_layers, n_rows, n_cols = w.shape
    nt = n_rows // rows
    n_parts = len(parts)

    def body(w_ref, m_ref, v_ref, *rest):
        part_refs, (g_ref, d_ref, nm_ref, nv_ref) = rest[:n_parts], rest[-4:]
        w_ref, m_ref, v_ref, g_ref, d_ref, nm_ref, nv_ref = (r.at[0] for r in (w_ref, m_ref, v_ref, g_ref, d_ref, nm_ref, nv_ref))
        grad = part_refs[0][...].astype(F32)
        for p in part_refs[1:]:
            grad = grad + p[...].astype(F32)
        new_m = ADAM_B1 * m_ref[...] + (1.0 - ADAM_B1) * grad
        new_v = ADAM_B2 * v_ref[...] + (1.0 - ADAM_B2) * (grad * grad)
        m_hat = new_m / (1.0 - ADAM_B1 ** ADAM_STEP)
        v_hat = new_v / (1.0 - ADAM_B2 ** ADAM_STEP)
        g_ref[...] = grad
        d_ref[...] = -ADAM_LR * (m_hat / (jnp.sqrt(v_hat) + ADAM_EPS) + ADAM_WD * w_ref[...])
        nm_ref[...] = new_m
        nv_ref[...] = new_v

    spec = pl.BlockSpec((rows, n_cols), lambda i: (i, 0))
    layer_spec = pl.BlockSpec((1, rows, n_cols), lambda i: (layer, i, 0))
    into = list(into or [])
    return pl.pallas_call(
        body, name="adamw", grid=(nt,),
        in_specs=[layer_spec] * 3 + [spec] * n_parts + [ANY] * len(into), out_specs=[layer_spec] * 4,
        out_shape=[jax.ShapeDtypeStruct(w.shape, F32)] * 4,
        input_output_aliases={3 + n_parts + k: k for k in range(len(into))},
        compiler_params=_params(),
    )(w, m, v, *parts, *into)


def _sum_stack(stack, *, rows):
    n_stack, n_rows, n_cols = stack.shape

    def body(stack_ref, out_ref):
        acc = stack_ref[0]
        for j in range(1, n_stack):
            acc = acc + stack_ref[j]
        out_ref[...] = acc

    return pl.pallas_call(
        body, name="sum_stack", grid=(n_rows // rows,),
        in_specs=[pl.BlockSpec((n_stack, rows, n_cols), lambda i: (0, i, 0))],
        out_specs=pl.BlockSpec((rows, n_cols), lambda i: (i, 0)),
        out_shape=jax.ShapeDtypeStruct((n_rows, n_cols), F32),
        compiler_params=_params(),
    )(stack)


def _sum_parts(own, recv, *, rows):
    n_rows, n_cols = own.shape
    n_recv = recv.shape[0]

    def body(own_ref, recv_ref, out_ref):
        acc = own_ref[...].astype(F32)
        for j in range(n_recv):
            acc = acc + recv_ref[j].astype(F32)
        out_ref[...] = acc

    return pl.pallas_call(
        body, name="sum_parts", grid=(n_rows // rows,),
        in_specs=[pl.BlockSpec((rows, n_cols), lambda i: (i, 0)), pl.BlockSpec((n_recv, rows, n_cols), lambda i: (0, i, 0))],
        out_specs=pl.BlockSpec((rows, n_cols), lambda i: (i, 0)),
        out_shape=jax.ShapeDtypeStruct(own.shape, F32),
        compiler_params=_params(),
    )(own, recv)


def _swap_cores(arrays):
    nk = len(arrays)

    def body(*refs):
        ins, outs, (send_sems, recv_sems) = refs[:nk], refs[nk:2 * nk], refs[2 * nk:]
        x, y, c = _place()
        sends = [pltpu.make_async_remote_copy(
            src_ref=ins[k], dst_ref=outs[k], send_sem=send_sems.at[k], recv_sem=recv_sems.at[k],
            device_id=(x, y, 1 - c), device_id_type=MESH_ID) for k in range(nk)]
        for cp in sends:
            cp.start()
        for cp in sends:
            cp.wait_recv()
        for cp in sends:
            cp.wait_send()

    return pl.pallas_call(
        body, name="swap_cores", in_specs=[ANY] * nk, out_specs=[ANY] * nk,
        out_shape=[jax.ShapeDtypeStruct(a.shape, a.dtype) for a in arrays],
        scratch_shapes=[pltpu.SemaphoreType.DMA((nk,)), pltpu.SemaphoreType.DMA((nk,))],
    )(*arrays)


class _AllDevices:
    def __init__(self, arrays):
        nk = len(arrays)
        self.arrays = list(arrays)
        self.out_shape = [jax.ShapeDtypeStruct((8,) + a.shape, a.dtype) for a in arrays]
        self.scratch = [pltpu.SemaphoreType.DMA((nk, 7)), pltpu.SemaphoreType.DMA((nk, 7)), pltpu.SemaphoreType.DMA((nk,))]

    def run(self, ins, outs, sems, start):
        send_sems, recv_sems, local_sems = sems
        x, y, c = _place()
        mine = 4 * x + 2 * y + c
        for k in range(len(ins)):
            local = pltpu.make_async_copy(ins[k], outs[k].at[mine], local_sems.at[k])
            local.start() if start else local.wait()
            for flip in range(1, 8):
                px, py, pc = x ^ (flip >> 2), y ^ ((flip >> 1) & 1), c ^ (flip & 1)
                sems_f = dict(send_sem=send_sems.at[k, flip - 1], recv_sem=recv_sems.at[k, flip - 1],
                              device_id=(px, py, pc), device_id_type=MESH_ID)
                send = pltpu.make_async_remote_copy(src_ref=ins[k], dst_ref=outs[k].at[mine], **sems_f)
                if start:
                    send.start()
                else:
                    pltpu.make_async_remote_copy(src_ref=ins[k], dst_ref=outs[k].at[4 * px + 2 * py + pc], **sems_f).wait_recv()
                    send.wait_send()


class _Both:
    def __init__(self, first, second):
        self.rides = (first, second)
        self.arrays = first.arrays + second.arrays
        self.out_shape = first.out_shape + second.out_shape
        self.scratch = first.scratch + second.scratch

    def run(self, ins, outs, sems, start):
        for ride in self.rides:
            n_in, n_out, n_sem = len(ride.arrays), len(ride.out_shape), len(ride.scratch)
            ride.run(ins[:n_in], outs[:n_out], sems[:n_sem], start)
            ins, outs, sems = ins[n_in:], outs[n_out:], sems[n_sem:]


def _pack(arrays, pad_rows=F32_ROWS):
    flat = jnp.concatenate([a.reshape(-1).astype(F32) for a in arrays])
    rows = -(-flat.shape[0] // (LANES * pad_rows)) * pad_rows
    return jnp.pad(flat, (0, rows * LANES - flat.shape[0])).reshape(rows, LANES)


def _unpack(packed, shapes):
    flat, out, off = packed.reshape(-1), [], 0
    for s in shapes:
        size = 1
        for dim in s:
            size *= dim
        out.append(flat[off:off + size].reshape(s))
        off += size
    return out


def _divisor_rows(n_rows, most=256):
    best = None
    for r in range(ACT_ROWS, most + 1, ACT_ROWS):
        if n_rows % r == 0:
            best = r
    return best or n_rows


def kernel(x, meta_tokens, norm_mix_g, norm_ffn_g, final_norm_g, sc_w_in, sc_conv_w, sc_w_out, rg_w_in, rg_conv_w, rg_conv_b, rg_w_gate_a, rg_b_gate_a, rg_w_gate_x, rg_b_gate_x, rg_lambda, rg_w_out, ffn_w_up, ffn_conv_w, ffn_w_down, loss_target, m_meta_tokens, m_norm_mix_g, m_norm_ffn_g, m_final_norm_g, m_sc_w_in, m_sc_conv_w, m_sc_w_out, m_rg_w_in, m_rg_conv_w, m_rg_conv_b, m_rg_w_gate_a, m_rg_b_gate_a, m_rg_w_gate_x, m_rg_b_gate_x, m_rg_lambda, m_rg_w_out, m_ffn_w_up, m_ffn_conv_w, m_ffn_w_down, v_meta_tokens, v_norm_mix_g, v_norm_ffn_g, v_final_norm_g, v_sc_w_in, v_sc_conv_w, v_sc_w_out, v_rg_w_in, v_rg_conv_w, v_rg_conv_b, v_rg_w_gate_a, v_rg_b_gate_a, v_rg_w_gate_x, v_rg_b_gate_x, v_rg_lambda, v_rg_w_out, v_ffn_w_up, v_ffn_conv_w, v_ffn_w_down):
    weights = dict(meta_tokens=meta_tokens, norm_mix_g=norm_mix_g, norm_ffn_g=norm_ffn_g, final_norm_g=final_norm_g, sc_w_in=sc_w_in, sc_conv_w=sc_conv_w, sc_w_out=sc_w_out, rg_w_in=rg_w_in, rg_conv_w=rg_conv_w, rg_conv_b=rg_conv_b, rg_w_gate_a=rg_w_gate_a, rg_b_gate_a=rg_b_gate_a, rg_w_gate_x=rg_w_gate_x, rg_b_gate_x=rg_b_gate_x, rg_lambda=rg_lambda, rg_w_out=rg_w_out, ffn_w_up=ffn_w_up, ffn_conv_w=ffn_conv_w, ffn_w_down=ffn_w_down)
    m_in = dict(meta_tokens=m_meta_tokens, norm_mix_g=m_norm_mix_g, norm_ffn_g=m_norm_ffn_g, final_norm_g=m_final_norm_g, sc_w_in=m_sc_w_in, sc_conv_w=m_sc_conv_w, sc_w_out=m_sc_w_out, rg_w_in=m_rg_w_in, rg_conv_w=m_rg_conv_w, rg_conv_b=m_rg_conv_b, rg_w_gate_a=m_rg_w_gate_a, rg_b_gate_a=m_rg_b_gate_a, rg_w_gate_x=m_rg_w_gate_x, rg_b_gate_x=m_rg_b_gate_x, rg_lambda=m_rg_lambda, rg_w_out=m_rg_w_out, ffn_w_up=m_ffn_w_up, ffn_conv_w=m_ffn_conv_w, ffn_w_down=m_ffn_w_down)
    v_in = dict(meta_tokens=v_meta_tokens, norm_mix_g=v_norm_mix_g, norm_ffn_g=v_norm_ffn_g, final_norm_g=v_final_norm_g, sc_w_in=v_sc_w_in, sc_conv_w=v_sc_conv_w, sc_w_out=v_sc_w_out, rg_w_in=v_rg_w_in, rg_conv_w=v_rg_conv_w, rg_conv_b=v_rg_conv_b, rg_w_gate_a=v_rg_w_gate_a, rg_b_gate_a=v_rg_b_gate_a, rg_w_gate_x=v_rg_w_gate_x, rg_b_gate_x=v_rg_b_gate_x, rg_lambda=v_rg_lambda, rg_w_out=v_rg_w_out, ffn_w_up=v_ffn_w_up, ffn_conv_w=v_ffn_conv_w, ffn_w_down=v_ffn_w_down)
    names = list(weights)

    seq, d = x.shape[1:]
    tm = _row_tile(seq)
    tokens, target = _tile_order(x[0], tm), _tile_order(loss_target[0], tm)
    t_len = seq + tm
    wg_rows = 5 * tm if t_len % (5 * tm) == 0 else tm
    wg_rows_in = 13 * tm if t_len % (13 * tm) == 0 else wg_rows
    xi, yi, _ = _place()
    chip = 2 * xi + yi
    mesh_axes = ("x", "y", "c")

    wire = lambda w: w.astype(WIRE_DT)
    small_sharded = ["meta_tokens", "sc_conv_w", "rg_conv_w", "rg_conv_b", "rg_b_gate_a", "rg_b_gate_x", "rg_lambda", "ffn_conv_w"]
    small_2d = {n: weights[n].reshape(-1, weights[n].shape[-1]) for n in small_sharded}
    w_sc_in, w_sc_out, small_by_chip = _exchange(
        _Gather([wire(sc_w_in[0]), wire(sc_w_out[0]), _pack([small_2d[n] for n in small_sharded])]), "gather_first")
    w_sc_out = w_sc_out.reshape(-1, d)
    gather_ffn0 = _Gather([wire(ffn_w_up[0]), wire(ffn_w_down[0])])
    gather_rest = _Gather([wire(rg_w_in[0]), wire(rg_w_out[0]), wire(ffn_w_up[1]), wire(ffn_w_down[1])])
    small_len = sum(a.size for a in small_2d.values())
    by_chip = small_by_chip.reshape(N_CHIPS, -1)[:, :small_len]
    full, off = {}, 0
    for n in small_sharded:
        rows, width = small_2d[n].shape
        full[n] = by_chip[:, off:off + rows * width].reshape(N_CHIPS, rows, width).transpose(1, 0, 2).reshape(rows, N_CHIPS * width)
        off += rows * width
    sc_cw, rg_cw = full["sc_conv_w"], full["rg_conv_w"]
    ffn_cw = [full["ffn_conv_w"][0:3], full["ffn_conv_w"][3:6]]
    d_rnn = rg_cw.shape[1]
    vec = jnp.concatenate([full["rg_conv_b"], full["rg_b_gate_a"], full["rg_b_gate_x"], full["rg_lambda"],
                           jnp.zeros((F32_ROWS - 4, d_rnn), F32)])
    wa, wx = rg_w_gate_a[0].astype(MXU_DT), rg_w_gate_x[0].astype(MXU_DT)
    first = _tile_order(jnp.concatenate([jnp.zeros((tm - N_META, d), F32), full["meta_tokens"]]), tm)
    g_mix = [norm_mix_g[0:1], norm_mix_g[1:2]]
    g_ffn = [norm_ffn_g[0:1], norm_ffn_g[1:2]]

    h1, hh0, w_up0, w_dn0 = _sc_fwd(tokens, first, g_mix[0], w_sc_in, sc_cw, w_sc_out, tm=tm, ride=gather_ffn0)
    h2, hu0, w_rg_in, w_rg_out, w_up1, w_dn1 = _ffn_fwd(h1, g_ffn[0], w_up0, ffn_cw[0], w_dn0.reshape(-1, d), tm=tm,
                                                         ride=gather_rest)
    w_up, w_dn, w_rg_out = [w_up0, w_up1], [w_dn0.reshape(-1, d), w_dn1.reshape(-1, d)], w_rg_out.reshape(-1, d)
    h3, hh1, hs = _rg_fwd(h2, g_mix[1], w_rg_in, rg_cw, vec, wa, wx, w_rg_out, tm=tm)
    dh4, hu1, sq, d_final = _ffn_fwd(h3, g_ffn[1], w_up[1], ffn_cw[1], w_dn[1], tm=tm,
                                     loss=(target, final_norm_g.reshape(1, d)))
    loss = lax.psum(jnp.sum(sq[0]) * (0.5 / d), mesh_axes)

    def by_chip_rows(pair):
        return [p.reshape(N_CHIPS, -1, d) for p in pair]

    def ffn_backward(dh_out, h_in, hu, layer, ride):
        dh_in, act, dhu, hn, dcw, dg, *landed = _ffn_bwd(dh_out, hu, h_in, g_ffn[layer], w_up[layer], ffn_cw[layer],
                                                         w_dn[layer], tm=tm, ride=ride)
        d_up = _weight_grad(hn, dhu, N_CHIPS, rows=wg_rows_in)
        d_dn = by_chip_rows(_weight_grad(act, dh_out, 1, rows=wg_rows))
        return dh_in, d_up, d_dn, dcw[0:3], dg[0], landed

    dh3, d_up1, d_dn1, d_fcw1, d_gf1, _ = ffn_backward(dh4, h3, hu1, 1, None)
    dh2, dhh1, y_rg, hn_rg, d_vec, d_wa, d_wx, d_gm1, *landed_ffn1 = _rg_bwd(
        dh3, hh1, hs, h2, g_mix[1], w_rg_in, rg_cw, vec, wa, wx, w_rg_out, tm=tm, ride=_Scatter([d_up1[1], d_dn1[1]]))
    d_rg_in = _weight_grad(hn_rg, dhh1, N_CHIPS, rows=wg_rows_in)
    d_rg_out = by_chip_rows(_weight_grad(y_rg, dh3, 1, rows=wg_rows))
    early = {"rg_conv_w": d_vec[G_CONV_W:G_CONV_W + 4], "rg_conv_b": d_vec[G_CONV_B:G_CONV_B + 1],
             "rg_b_gate_a": d_vec[G_B_A:G_B_A + 1], "rg_b_gate_x": d_vec[G_B_X:G_B_X + 1],
             "rg_lambda": d_vec[G_LAMBDA:G_LAMBDA + 1], "ffn_conv_w.1": d_fcw1, "norm_mix_g.1": d_gm1[0:1],
             "norm_ffn_g.1": d_gf1[None], "final_norm_g": d_final[0]}
    early_packed = _pack(list(early.values()))
    gate_names = ["rg_w_gate_a", "rg_w_gate_x"]
    to_all = _AllDevices([early_packed, d_wa.reshape(-1, LANES), d_wx.reshape(-1, LANES)])
    dh1, d_up0, d_dn0, d_fcw0, d_gf0, landed = ffn_backward(
        dh2, h1, hu0, 0, _Both(_Scatter([d_rg_in[1], d_rg_out[1]]), to_all))
    landed_rg, early_by_device, gates_by_device = landed[0:2], landed[2], landed[3:]
    grad_x, dhh0, z_sc, hn_sc, d_sccw, d_gm0, d_first, *landed_ffn0 = _sc_bwd(
        dh1, hh0, tokens, first, g_mix[0], w_sc_in, sc_cw, w_sc_out, tm=tm, ride=_Scatter([d_up0[1], d_dn0[1]]))
    d_sc_in = _weight_grad(hn_sc, dhh0, N_CHIPS, rows=wg_rows_in)
    *d_sc_out, landed_sc_in = _weight_grad(z_sc, dh1, 1, rows=wg_rows, ride=_Scatter([d_sc_in[1]]))
    d_sc_out = by_chip_rows(d_sc_out)
    landed_sc = [landed_sc_in, *_exchange(_Scatter([d_sc_out[1]]), "scatter_last")]
    grad_x = _time_order(grad_x, tm)[None]

    big = [("sc_w_in", 0, d_sc_in, landed_sc[0]), ("sc_w_out", 0, d_sc_out, landed_sc[1]),
           ("rg_w_in", 0, d_rg_in, landed_rg[0]), ("rg_w_out", 0, d_rg_out, landed_rg[1]),
           ("ffn_w_up", 0, d_up0, landed_ffn0[0]), ("ffn_w_up", 1, d_up1, landed_ffn1[0]),
           ("ffn_w_down", 0, d_dn0, landed_ffn0[1]), ("ffn_w_down", 1, d_dn1, landed_ffn1[1])]
    core_sum = []
    for _, _, (partial, _), received in big:
        own = lax.dynamic_index_in_dim(partial, chip, 0, keepdims=False)
        core_sum.append(_sum_parts(own, received, rows=_divisor_rows(own.shape[0])))
    other_sum = _swap_cores(core_sum)
    out = {k: {} for k in ("grad", "delta", "m", "v")}
    stacked = {}
    for (n, layer, _, _), mine, theirs in zip(big, core_sum, other_sum):
        stacked[n] = _adamw(weights[n], m_in[n], v_in[n], [mine, theirs], rows=_divisor_rows(mine.shape[0]), layer=layer,
                            into=stacked.get(n))
    for n, res in stacked.items():
        for k, key in enumerate(("grad", "delta", "m", "v")):
            out[key][n] = res[k]

    late = {"meta_tokens": _time_order(d_first, tm)[tm - N_META:], "sc_conv_w": d_sccw[0:3], "ffn_conv_w.0": d_fcw0,
            "norm_mix_g.0": d_gm0[0:1], "norm_ffn_g.0": d_gf0[None]}
    late_packed = _pack(list(late.values()))
    late_by_device, = _exchange(_AllDevices([late_packed]), "gather_devices")
    summed = {}
    for parts, packed, by_device in ((early, early_packed, early_by_device), (late, late_packed, late_by_device)):
        total = _sum_stack(by_device, rows=packed.shape[0])
        summed.update(zip(parts, _unpack(total, [p.shape for p in parts.values()])))
    for n in ("ffn_conv_w", "norm_mix_g", "norm_ffn_g"):
        summed[n] = jnp.concatenate([summed.pop(n + ".0"), summed.pop(n + ".1")])
    for n, by_device in zip(gate_names, gates_by_device):
        as_rows = lambda a: a.reshape(1, -1, LANES)
        res = _adamw(as_rows(weights[n]), as_rows(m_in[n]), as_rows(v_in[n]), [_sum_stack(by_device, rows=256)], rows=256)
        for k, key in enumerate(("grad", "delta", "m", "v")):
            out[key][n] = res[k].reshape(weights[n].shape)
    replicated = ["norm_mix_g", "norm_ffn_g", "final_norm_g"]
    small_names = small_sharded + replicated
    grads = {}
    for n in small_sharded:
        width = small_2d[n].shape[1]
        grads[n] = lax.dynamic_slice_in_dim(summed[n], chip * width, width, axis=1).reshape(weights[n].shape)
    for n in replicated:
        grads[n] = summed[n].reshape(weights[n].shape)
    shapes = [weights[n].shape for n in small_names]
    packed_w = _pack([weights[n] for n in small_names])
    res = _adamw(packed_w[None], _pack([m_in[n] for n in small_names])[None], _pack([v_in[n] for n in small_names])[None],
                 [_pack([grads[n] for n in small_names])], rows=packed_w.shape[0])
    for k, key in enumerate(("grad", "delta", "m", "v")):
        out[key].update(dict(zip(small_names, _unpack(res[k][0], shapes))))

    return (loss, grad_x, *[out["grad"][n] for n in names], *[out["delta"][n] for n in names],
            *[out["m"][n] for n in names], *[out["v"][n] for n in names])
```

```python
import functools

import jax
import jax.numpy as jnp
from jax import lax
from jax.experimental import pallas as pl
from jax.experimental.pallas import tpu as pltpu

F32 = jnp.float32
MXU_DT = jnp.bfloat16
ACT_DT = jnp.bfloat16
WIRE_DT = jnp.bfloat16
MESH_ID = pl.DeviceIdType.MESH

N_META = 16
RMS_EPS = 1e-6
RG_C = 8.0
ADAM_LR, ADAM_B1, ADAM_B2, ADAM_EPS, ADAM_WD, ADAM_STEP = 0.001, 0.9, 0.999, 1e-08, 0.01, 10
N_CHIPS = 4
VMEM_LIMIT = 60 * 1024 * 1024
F32_ROWS = 8
ACT_ROWS = 16
LANES = 128


def _row_tile(seq):
    for tm in (256, 128, 64, 32, 16):
        if seq % tm == 0:
            return tm
    raise ValueError(f"sequence length {seq} is not a multiple of 16")


def _params(n_axes=1, **kw):
    return pltpu.CompilerParams(dimension_semantics=("arbitrary",) * n_axes, vmem_limit_bytes=VMEM_LIMIT, **kw)


def _const(shape):
    return pl.BlockSpec(shape, lambda *_: (0,) * len(shape), pipeline_mode=pl.Buffered(1))


def _dot(a, b):
    return jnp.dot(a, b, preferred_element_type=F32)


def _dot_nt(a, b):
    return lax.dot_general(a, b, (((1,), (1,)), ((), ())), preferred_element_type=F32)


def _dot_tn(a, b):
    return lax.dot_general(a, b, (((0,), (0,)), ((), ())), preferred_element_type=F32)


def _sigmoid(x):
    return 1.0 / (1.0 + jnp.exp(-x))


def _rms(h, g):
    rstd = lax.rsqrt(jnp.mean(h * h, axis=-1, keepdims=True) + RMS_EPS)
    xhat = h * rstd
    return xhat * g, xhat, rstd


def _rms_bwd(dhn, xhat, rstd, g):
    dx = dhn * g
    return rstd * (dx - xhat * jnp.mean(dx * xhat, axis=-1, keepdims=True))


def _gelu(x):
    k = 0.7978845608028654
    t = jnp.tanh(k * (x + 0.044715 * x * x * x))
    return 0.5 * x * (1.0 + t), t


def _gelu_grad(x, t):
    k = 0.7978845608028654
    return 0.5 * (1.0 + t) + 0.5 * x * (1.0 - t * t) * k * (1.0 + 3 * 0.044715 * x * x)


def _softplus(x):
    e = jnp.exp(-jnp.abs(x))
    return jnp.maximum(x, 0.0) + jnp.where(e < 1e-4, e - 0.5 * e * e, jnp.log(1.0 + e))


def _expm1_neg(z):
    series = z * (1.0 + z * (0.5 + z * (1.0 / 6)))
    return jnp.where(z > -0.02, series, jnp.exp(z) - 1.0)


def _tile_order(a, tm):
    return a.reshape(-1, F32_ROWS, tm // F32_ROWS, a.shape[-1]).swapaxes(1, 2).reshape(a.shape)


def _time_order(a, tm):
    return a.reshape(-1, tm // F32_ROWS, F32_ROWS, a.shape[-1]).swapaxes(1, 2).reshape(a.shape)


def _valid_rows(tile, tm):
    row = lax.broadcasted_iota(jnp.int32, (tm, 1), 0)
    time = (row & (F32_ROWS - 1)) * (tm // F32_ROWS) + (row >> 3) + tile * tm
    return time >= tm - N_META


def _sublane():
    return lax.broadcasted_iota(jnp.int32, (F32_ROWS, 1), 0)


def _past_rows(width):
    return (width - 1) * F32_ROWS


def _halo_block(past, tm, nt):
    rows = -(-past // ACT_ROWS) * ACT_ROWS
    return rows, lambda i: (jnp.maximum((nt - 1 - i) * (tm // rows) - 1, 0), 0)


def _link_past(buf, cols, width, tm):
    past = _past_rows(width)
    for k in range(1, width):
        rows = pl.ds(past - F32_ROWS * k, F32_ROWS)
        before = pltpu.roll(buf[rows, cols], 1, 0)
        mine = pltpu.roll(buf[pl.ds(past + tm - F32_ROWS * k, F32_ROWS), cols], 1, 0)
        buf[rows, cols] = jnp.where(_sublane() == 0, before, mine)


def _link_future(buf, cols, width, tm):
    for k in range(1, width):
        rows = pl.ds(tm + F32_ROWS * (k - 1), F32_ROWS)
        after = pltpu.roll(buf[rows, cols], F32_ROWS - 1, 0)
        mine = pltpu.roll(buf[pl.ds(F32_ROWS * (k - 1), F32_ROWS), cols], F32_ROWS - 1, 0)
        buf[rows, cols] = jnp.where(_sublane() == F32_ROWS - 1, after, mine)


def _conv_taps(buf, cols, width, tm):
    return [buf[pl.ds(F32_ROWS * k, tm), cols] for k in range(width)]


def _conv_back(buf, cw_ref, cols, width, tm):
    return sum(cw_ref[k:k + 1, cols] * buf[pl.ds(F32_ROWS * (width - 1 - k), tm), cols] for k in range(width))


ANY = pl.BlockSpec(memory_space=pl.ANY)


def _place():
    return lax.axis_index("x"), lax.axis_index("y"), lax.axis_index("c")


def _other_chips(x, y):
    return [(1 - x, y), (x, 1 - y), (1 - x, 1 - y)]


class _Gather:
    def __init__(self, shards):
        nk = len(shards)
        self.arrays = list(shards)
        self.out_shape = [jax.ShapeDtypeStruct((N_CHIPS,) + s.shape, s.dtype) for s in shards]
        self.scratch = [pltpu.SemaphoreType.DMA((nk, 3)), pltpu.SemaphoreType.DMA((nk, 3)), pltpu.SemaphoreType.DMA((nk,))]

    def run(self, ins, outs, sems, start):
        send_sems, recv_sems, local_sems = sems
        x, y, c = _place()
        mine = 2 * x + y
        for k in range(len(ins)):
            local = pltpu.make_async_copy(ins[k], outs[k].at[mine], local_sems.at[k])
            local.start() if start else local.wait()
            for j, (px, py) in enumerate(_other_chips(x, y)):
                sems_kj = dict(send_sem=send_sems.at[k, j], recv_sem=recv_sems.at[k, j], device_id=(px, py, c),
                               device_id_type=MESH_ID)
                send = pltpu.make_async_remote_copy(src_ref=ins[k], dst_ref=outs[k].at[mine], **sems_kj)
                if start:
                    send.start()
                else:
                    pltpu.make_async_remote_copy(src_ref=ins[k], dst_ref=outs[k].at[2 * px + py], **sems_kj).wait_recv()
                    send.wait_send()


class _GatherHalves:
    def __init__(self, shards):
        nk = len(shards)
        self.arrays = list(shards)
        self.out_shape = [jax.ShapeDtypeStruct((N_CHIPS,) + s.shape, s.dtype) for s in shards]
        self.scratch = [pltpu.SemaphoreType.DMA((nk, 3)) for _ in range(4)] + [pltpu.SemaphoreType.DMA((nk,))]

    def run(self, ins, outs, sems, start):
        far_send, far_recv, near_send, near_recv, local_sems = sems
        x, y, c = _place()
        mine = 2 * x + y
        for k in range(len(ins)):
            half = ins[k].shape[0] // 2
            my_half = pl.ds(pl.multiple_of(c * half, ACT_ROWS), half)
            other_half = pl.ds(pl.multiple_of((1 - c) * half, ACT_ROWS), half)
            local = pltpu.make_async_copy(ins[k], outs[k].at[mine], local_sems.at[k])
            local.start() if start else local.wait()
            for j, (px, py) in enumerate(_other_chips(x, y)):
                theirs = 2 * px + py
                far = dict(send_sem=far_send.at[k, j], recv_sem=far_recv.at[k, j], device_id=(px, py, c), device_id_type=MESH_ID)
                near = dict(send_sem=near_send.at[k, j], recv_sem=near_recv.at[k, j], device_id=(x, y, 1 - c),
                            device_id_type=MESH_ID)
                send = pltpu.make_async_remote_copy(src_ref=ins[k].at[my_half], dst_ref=outs[k].at[mine, my_half], **far)
                if start:
                    send.start()
                    continue
                landed = outs[k].at[theirs, my_half]
                pltpu.make_async_remote_copy(src_ref=ins[k].at[my_half], dst_ref=landed, **far).wait_recv()
                passed_on = pltpu.make_async_remote_copy(src_ref=landed, dst_ref=landed, **near)
                passed_on.start()
                pltpu.make_async_remote_copy(src_ref=landed, dst_ref=outs[k].at[theirs, other_half], **near).wait_recv()
                passed_on.wait_send()
                send.wait_send()


class _Scatter:
    def __init__(self, parts):
        nk = len(parts)
        self.arrays = list(parts)
        self.out_shape = [jax.ShapeDtypeStruct((3,) + p.shape[1:], p.dtype) for p in parts]
        self.scratch = [pltpu.SemaphoreType.DMA((nk, 3)), pltpu.SemaphoreType.DMA((nk, 3))]

    def run(self, ins, outs, sems, start):
        send_sems, recv_sems = sems
        x, y, c = _place()
        for k in range(len(ins)):
            for j, (px, py) in enumerate(_other_chips(x, y)):
                send = pltpu.make_async_remote_copy(
                    src_ref=ins[k].at[2 * px + py], dst_ref=outs[k].at[j], send_sem=send_sems.at[k, j],
                    recv_sem=recv_sems.at[k, j], device_id=(px, py, c), device_id_type=MESH_ID)
                if start:
                    send.start()
                else:
                    send.wait_recv()
                    send.wait_send()


def _exchange(ride, name):
    n_in, n_out = len(ride.arrays), len(ride.out_shape)

    def body(*refs):
        ride.run(refs[:n_in], refs[n_in:n_in + n_out], refs[n_in + n_out:], start=True)
        ride.run(refs[:n_in], refs[n_in:n_in + n_out], refs[n_in + n_out:], start=False)

    return pl.pallas_call(body, name=name, in_specs=[ANY] * n_in, out_specs=[ANY] * n_out, out_shape=ride.out_shape,
                          scratch_shapes=ride.scratch)(*ride.arrays)


def _launch(body, operands, *, name, grid, in_specs, out_specs, out_shape, scratch_shapes=(), ride=None):
    common = dict(name=name, grid=grid, compiler_params=_params(len(grid)))
    if ride is None:
        return pl.pallas_call(body, in_specs=in_specs, out_specs=out_specs, out_shape=out_shape,
                              scratch_shapes=list(scratch_shapes), **common)(*operands)
    n_in, n_out, n_scr = len(operands), len(out_shape), len(scratch_shapes)
    r_in, r_out = len(ride.arrays), len(ride.out_shape)

    def riding(*refs):
        ins, refs = refs[:n_in], refs[n_in:]
        r_ins, refs = refs[:r_in], refs[r_in:]
        outs, refs = refs[:n_out], refs[n_out:]
        r_outs, refs = refs[:r_out], refs[r_out:]
        scr, r_sems = refs[:n_scr], refs[n_scr:]
        step = [pl.program_id(axis) for axis in range(len(grid))]
        first = functools.reduce(jnp.logical_and, [s == 0 for s in step])
        last = functools.reduce(jnp.logical_and, [s == size - 1 for s, size in zip(step, grid)])

        @pl.when(first)
        def _():
            ride.run(r_ins, r_outs, r_sems, start=True)

        body(*ins, *outs, *scr)

        @pl.when(last)
        def _():
            ride.run(r_ins, r_outs, r_sems, start=False)

    return pl.pallas_call(
        riding, in_specs=list(in_specs) + [ANY] * r_in, out_specs=list(out_specs) + [ANY] * r_out,
        out_shape=list(out_shape) + ride.out_shape, scratch_shapes=list(scratch_shapes) + ride.scratch, **common,
    )(*operands, *ride.arrays)


def _sc_fwd(x, first, g, w_in, cw, w_out, *, tm, ride=None):
    seq, d = x.shape
    nt = seq // tm + 1
    nq, _, n = w_in.shape
    width = cw.shape[0]
    past = _past_rows(width)

    def body(x_ref, first_ref, g_ref, win_ref, cw_ref, wout_ref, h1_ref, hh_ref, hh_scr, cbuf):
        i = pl.program_id(0)

        @pl.when(i == 0)
        def _():
            cbuf[pl.ds(0, past), :] = jnp.zeros((past, d), F32)

        h = jnp.where(i == 0, first_ref[...], x_ref[...])
        hn = _rms(h, g_ref[...])[0].astype(MXU_DT)
        for q in range(nq):
            hh_scr[:, q * n:(q + 1) * n] = _dot(hn, win_ref[q])
        hh_ref[...] = hh_scr[...].astype(hh_ref.dtype)
        b = hh_scr[:, 0:d]
        cbuf[pl.ds(past, tm), :] = hh_scr[:, d:2 * d] * hh_scr[:, 2 * d:3 * d]
        last = cbuf[pl.ds(tm, past), :]
        _link_past(cbuf, slice(None), width, tm)
        u = sum(cw_ref[k:k + 1, :] * tap for k, tap in enumerate(_conv_taps(cbuf, slice(None), width, tm)))
        cbuf[pl.ds(0, past), :] = last
        h1_ref[...] = h + _dot((b * u).astype(MXU_DT), wout_ref[...])

    return _launch(
        body, [x, first, g, w_in, cw, w_out], name="sc_fwd", grid=(nt,),
        in_specs=[pl.BlockSpec((tm, d), lambda i: (jnp.maximum(i - 1, 0), 0)), _const((tm, d)), _const((1, d)),
                  _const(w_in.shape), _const(cw.shape), _const(w_out.shape)],
        out_specs=[pl.BlockSpec((tm, d), lambda i: (i, 0)), pl.BlockSpec((tm, nq * n), lambda i: (i, 0))],
        out_shape=[jax.ShapeDtypeStruct((nt * tm, d), F32), jax.ShapeDtypeStruct((nt * tm, nq * n), ACT_DT)],
        scratch_shapes=[pltpu.VMEM((tm, nq * n), F32), pltpu.VMEM((past + tm, d), F32)],
        ride=ride,
    )


def _sc_bwd(dh, hh, x, first, g, w_in, cw, w_out, *, tm, ride=None):
    t_len, d = dh.shape
    nt = t_len // tm
    nq, _, n = w_in.shape
    width = cw.shape[0]
    past = _past_rows(width)
    halo_rows, halo_index = _halo_block(past, tm, nt)

    def body(dh_ref, hh_ref, hhp_ref, x_ref, first_ref, g_ref, win_ref, cw_ref, wout_ref,
             dx_ref, dhh_ref, z_ref, hn_ref, dcw_ref, dg_ref, dfirst_ref, cbuf, dbuf):
        i = pl.program_id(0)
        r = nt - 1 - i

        @pl.when(i == 0)
        def _():
            dbuf[pl.ds(tm, past), :] = jnp.zeros((past, d), F32)
            dcw_ref[...] = jnp.zeros_like(dcw_ref)
            dg_ref[...] = jnp.zeros_like(dg_ref)

        dh_out = dh_ref[...]
        b = hh_ref[:, 0:d].astype(F32)
        c = hh_ref[:, d:2 * d].astype(F32)
        v = hh_ref[:, 2 * d:3 * d].astype(F32)
        prev = hhp_ref[...].astype(F32)[halo_rows - past:, :]
        cbuf[pl.ds(0, past), :] = jnp.where(r > 0, prev[:, d:2 * d] * prev[:, 2 * d:3 * d], 0.0)
        cbuf[pl.ds(past, tm), :] = c * v
        _link_past(cbuf, slice(None), width, tm)
        taps = _conv_taps(cbuf, slice(None), width, tm)
        u = sum(cw_ref[k:k + 1, :] * taps[k] for k in range(width))
        z_ref[...] = (b * u).astype(z_ref.dtype)
        dz = _dot_nt(dh_out.astype(MXU_DT), wout_ref[...])
        dhh_ref[:, 0:d] = (dz * u).astype(dhh_ref.dtype)
        du = dz * b
        for k in range(width):
            dcw_ref[k:k + 1, :] += jnp.sum(taps[k] * du, axis=0, keepdims=True)
        dbuf[pl.ds(0, tm), :] = du
        _link_future(dbuf, slice(None), width, tm)
        dcv = _conv_back(dbuf, cw_ref, slice(None), width, tm)
        dbuf[pl.ds(tm, past), :] = dbuf[pl.ds(0, past), :]
        dhh_ref[:, d:2 * d] = (dcv * v).astype(dhh_ref.dtype)
        dhh_ref[:, 2 * d:3 * d] = (dcv * c).astype(dhh_ref.dtype)
        parts = [_dot_nt(dhh_ref[:, q * n:(q + 1) * n], win_ref[q]) for q in range(nq)]
        dhn = functools.reduce(lambda a, b: a + b, parts)
        h_in = jnp.where(r == 0, first_ref[...], x_ref[...])
        dh_in = _norm_bwd_tile(dhn, h_in, dh_out, g_ref[...], _valid_rows(r, tm), hn_ref, dg_ref)

        @pl.when(r == 0)
        def _():
            dfirst_ref[...] = dh_in

        @pl.when(r > 0)
        def _():
            dx_ref[...] = dh_in

    rev = lambda i: (nt - 1 - i, 0)
    rev_x = lambda i: (jnp.maximum(nt - 2 - i, 0), 0)
    return _launch(
        body, [dh, hh, hh, x, first, g, w_in, cw, w_out], name="sc_bwd", grid=(nt,),
        in_specs=[pl.BlockSpec((tm, d), rev), pl.BlockSpec((tm, 3 * d), rev), pl.BlockSpec((halo_rows, 3 * d), halo_index),
                  pl.BlockSpec((tm, d), rev_x), _const((tm, d)), _const((1, d)), _const(w_in.shape), _const(cw.shape),
                  _const(w_out.shape)],
        out_specs=[pl.BlockSpec((tm, d), rev_x), pl.BlockSpec((tm, 3 * d), rev), pl.BlockSpec((tm, d), rev),
                   pl.BlockSpec((tm, d), rev), _const((F32_ROWS, d)), _const((F32_ROWS, d)), _const((tm, d))],
        out_shape=[jax.ShapeDtypeStruct((t_len - tm, d), F32), jax.ShapeDtypeStruct((t_len, 3 * d), ACT_DT),
                   jax.ShapeDtypeStruct((t_len, d), ACT_DT), jax.ShapeDtypeStruct((t_len, d), ACT_DT),
                   jax.ShapeDtypeStruct((F32_ROWS, d), F32), jax.ShapeDtypeStruct((F32_ROWS, d), F32),
                   jax.ShapeDtypeStruct((tm, d), F32)],
        scratch_shapes=[pltpu.VMEM((past + tm, d), F32), pltpu.VMEM((tm + past, d), F32)],
        ride=ride,
    )


def _ffn_fwd(h, g, w_up, cw, w_down, *, tm, ride=None, loss=None):
    t_len, d = h.shape
    nt = t_len // tm
    nq, _, n = w_up.shape
    width = cw.shape[0]
    past = _past_rows(width)

    def body(h_ref, g_ref, wup_ref, cw_ref, wdn_ref, *rest):
        if loss is None:
            out_ref, hu_ref, ubuf, tail = rest
        else:
            t_ref, gf_ref, out_ref, hu_ref, sq_ref, dgf_ref, ubuf, tail = rest
        i = pl.program_id(0)

        @pl.when(i == 0)
        def _():
            tail[...] = jnp.zeros_like(tail)

        h_in = h_ref[...]
        hn = _rms(h_in, g_ref[...])[0].astype(MXU_DT)
        ubuf[pl.ds(0, past), :] = tail[...]
        for q in range(nq):
            ubuf[pl.ds(past, tm), q * n:(q + 1) * n] = _dot(hn, wup_ref[q])
        hu_ref[...] = ubuf[pl.ds(past, tm), :].astype(hu_ref.dtype)
        tail[...] = ubuf[pl.ds(tm, past), :]
        _link_past(ubuf, slice(None), width, tm)
        acc = h_in
        for j in range(nq // 2):
            gcol, vcol = slice(j * n, (j + 1) * n), slice((nq // 2 + j) * n, (nq // 2 + j + 1) * n)
            conv = lambda cols: sum(cw_ref[k:k + 1, cols] * tap for k, tap in enumerate(_conv_taps(ubuf, cols, width, tm)))
            gj, vj = conv(gcol), conv(vcol)
            acc = acc + _dot((gj * _sigmoid(gj) * vj).astype(MXU_DT), wdn_ref[j * n:(j + 1) * n, :])
        if loss is None:
            out_ref[...] = acc
            return

        @pl.when(i == 0)
        def _():
            sq_ref[...] = jnp.zeros_like(sq_ref)
            dgf_ref[...] = jnp.zeros_like(dgf_ref)
            out_ref[...] = jnp.zeros_like(out_ref)

        @pl.when(i > 0)
        def _():
            gain = gf_ref[...]
            out, xhat, rstd = _rms(acc, gain)
            err = out - t_ref[...]
            sq_ref[0:1, :] += jnp.sum(err * err, axis=0, keepdims=True)
            dout = err * (1.0 / d)
            dgf_ref[0:1, :] += jnp.sum(dout * xhat, axis=0, keepdims=True)
            out_ref[...] = _rms_bwd(dout, xhat, rstd, gain)

    row = lambda i: (i, 0)
    stat = jax.ShapeDtypeStruct((F32_ROWS, d), F32)
    return _launch(
        body, [h, g, w_up, cw, w_down] + list(loss or ()), name="ffn_fwd", grid=(nt,),
        in_specs=[pl.BlockSpec((tm, d), row), _const((1, d)), _const(w_up.shape), _const(cw.shape), _const(w_down.shape)]
        + ([pl.BlockSpec((tm, d), lambda i: (jnp.maximum(i - 1, 0), 0)), _const((1, d))] if loss else []),
        out_specs=[pl.BlockSpec((tm, d), row), pl.BlockSpec((tm, nq * n), row)] + ([_const(stat.shape)] * 2 if loss else []),
        out_shape=[jax.ShapeDtypeStruct((t_len, d), F32), jax.ShapeDtypeStruct((t_len, nq * n), ACT_DT)]
        + ([stat, stat] if loss else []),
        scratch_shapes=[pltpu.VMEM((past + tm, nq * n), F32), pltpu.VMEM((past, nq * n), F32)],
        ride=ride,
    )


def _norm_bwd_tile(dhn, h_in, dh, gain, valid, hn_ref, dg_ref):
    hn, xhat, rstd = _rms(h_in, gain)
    hn_ref[...] = hn.astype(hn_ref.dtype)
    dg_ref[0:1, :] += jnp.sum(dhn * xhat, axis=0, keepdims=True)
    return jnp.where(valid, dh + _rms_bwd(dhn, xhat, rstd, gain), 0.0)


def _ffn_bwd(dh, hu, h, g, w_up, cw, w_down, *, tm, ride=None):
    t_len, d = dh.shape
    nt = t_len // tm
    ff = hu.shape[1]
    n = ff // 4
    width = cw.shape[0]
    past = _past_rows(width)
    halo_rows, halo_index = _halo_block(past, tm, nt)

    def body(dh_ref, hu_ref, hup_ref, h_ref, g_ref, wup_ref, cw_ref, wdn_ref,
             dhin_ref, a_ref, dhu_ref, hn_ref, dcw_ref, dg_ref, ubuf, dbuf, head):
        i = pl.program_id(0)
        r = nt - 1 - i

        @pl.when(i == 0)
        def _():
            head[...] = jnp.zeros_like(head)
            dcw_ref[...] = jnp.zeros_like(dcw_ref)
            dg_ref[...] = jnp.zeros_like(dg_ref)

        dh_out = dh_ref[...]
        dhb = dh_out.astype(MXU_DT)
        dhn_parts = []
        d_act = [_dot_nt(dhb, wdn_ref[j * n:(j + 1) * n, :]) for j in range(2)]
        for j in range(2):
            mine = slice(0, n), slice(n, 2 * n)
            full = slice(j * n, (j + 1) * n), slice((2 + j) * n, (3 + j) * n)
            for here, there in zip(mine, full):
                prev = hup_ref[:, there].astype(F32)[halo_rows - past:, :]
                ubuf[pl.ds(0, past), here] = jnp.where(r > 0, prev, 0.0)
                ubuf[pl.ds(past, tm), here] = hu_ref[:, there].astype(F32)
                dbuf[pl.ds(tm, past), here] = head[:, there]
            _link_past(ubuf, slice(None), width, tm)
            conv = lambda here, there: sum(cw_ref[k:k + 1, there] * tap
                                           for k, tap in enumerate(_conv_taps(ubuf, here, width, tm)))
            gj, vj = conv(mine[0], full[0]), conv(mine[1], full[1])
            sg = _sigmoid(gj)
            s = gj * sg
            a_ref[:, full[0]] = (s * vj).astype(a_ref.dtype)
            da = d_act[j]
            dbuf[pl.ds(0, tm), mine[1]] = da * s
            dbuf[pl.ds(0, tm), mine[0]] = da * vj * (sg * (1.0 + gj * (1.0 - sg)))
            for here, there in zip(mine, full):
                head[:, there] = dbuf[pl.ds(0, past), here]
            _link_future(dbuf, slice(None), width, tm)
            for here, there in zip(mine, full):
                dy = dbuf[pl.ds(0, tm), here]
                for k, tap in enumerate(_conv_taps(ubuf, here, width, tm)):
                    dcw_ref[k:k + 1, there] += jnp.sum(tap * dy, axis=0, keepdims=True)
                dhu = sum(cw_ref[k:k + 1, there] * dbuf[pl.ds(F32_ROWS * (width - 1 - k), tm), here]
                          for k in range(width)).astype(dhu_ref.dtype)
                dhu_ref[:, there] = dhu
                dhn_parts.append(_dot_nt(dhu, wup_ref[there.start // n]))
        dhn = (dhn_parts[0] + dhn_parts[1]) + (dhn_parts[2] + dhn_parts[3])
        dhin_ref[...] = _norm_bwd_tile(dhn, h_ref[...], dh_out, g_ref[...], _valid_rows(r, tm), hn_ref, dg_ref)

    rev = lambda i: (nt - 1 - i, 0)
    return _launch(
        body, [dh, hu, hu, h, g, w_up, cw, w_down], name="ffn_bwd", grid=(nt,),
        in_specs=[pl.BlockSpec((tm, d), rev), pl.BlockSpec((tm, ff), rev), pl.BlockSpec((halo_rows, ff), halo_index),
                  pl.BlockSpec((tm, d), rev), _const((1, d)), _const(w_up.shape), _const(cw.shape), _const(w_down.shape)],
        out_specs=[pl.BlockSpec((tm, d), rev), pl.BlockSpec((tm, 2 * n), rev), pl.BlockSpec((tm, ff), rev),
                   pl.BlockSpec((tm, d), rev), _const((F32_ROWS, ff)), _const((F32_ROWS, d))],
        out_shape=[jax.ShapeDtypeStruct((t_len, d), F32), jax.ShapeDtypeStruct((t_len, 2 * n), ACT_DT),
                   jax.ShapeDtypeStruct((t_len, ff), ACT_DT), jax.ShapeDtypeStruct((t_len, d), ACT_DT),
                   jax.ShapeDtypeStruct((F32_ROWS, ff), F32), jax.ShapeDtypeStruct((F32_ROWS, d), F32)],
        scratch_shapes=[pltpu.VMEM((past + tm, 2 * n), F32), pltpu.VMEM((tm + past, 2 * n), F32), pltpu.VMEM((past, ff), F32)],
        ride=ride,
    )


V_CONV_B, V_B_A, V_B_X, V_LAMBDA = 0, 1, 2, 3
G_CONV_W, G_CONV_B, G_B_A, G_B_X, G_LAMBDA = 0, 4, 5, 6, 7


def _scan(a_ref, b_ref, edge, tm, reverse):
    nj = tm // F32_ROWS
    order = range(nj - 1, -1, -1) if reverse else range(nj)
    slab = lambda ref, j: ref[pl.ds(F32_ROWS * j, F32_ROWS), :]
    a_run = b_run = None
    for j in order:
        a_j, b_j = slab(a_ref, j), slab(b_ref, j)
        if a_run is not None:
            b_j = b_j + a_j * b_run
            a_j = a_j * a_run
            b_ref[pl.ds(F32_ROWS * j, F32_ROWS), :] = b_j
            a_ref[pl.ds(F32_ROWS * j, F32_ROWS), :] = a_j
        a_run, b_run = a_j, b_j
    sub = _sublane()
    shift = 1
    while shift < F32_ROWS:
        amount = F32_ROWS - shift if reverse else shift
        keep = (sub < F32_ROWS - shift) if reverse else (sub >= shift)
        b_run = jnp.where(keep, b_run + a_run * pltpu.roll(b_run, amount, 0), b_run)
        a_run = jnp.where(keep, a_run * pltpu.roll(a_run, amount, 0), a_run)
        shift *= 2
    outer = edge[0:1, :] if reverse else edge[F32_ROWS - 1:F32_ROWS, :]
    ends = b_run + a_run * outer
    if reverse:
        carry = jnp.where(sub == F32_ROWS - 1, outer, pltpu.roll(ends, F32_ROWS - 1, 0))
    else:
        carry = jnp.where(sub == 0, outer, pltpu.roll(ends, 1, 0))
    for j in range(nj):
        b_ref[pl.ds(F32_ROWS * j, F32_ROWS), :] = slab(b_ref, j) + slab(a_ref, j) * carry
    return slab(b_ref, 0 if reverse else nj - 1)


def _rg_gates(u, vec_ref, wa_ref, wx_ref, pre_scr, nb, bd):
    ub = u.astype(MXU_DT)
    for k in range(nb):
        blk = slice(k * bd, (k + 1) * bd)
        pre_scr[0, :, blk] = _dot(ub[:, blk], wa_ref[k])
        pre_scr[1, :, blk] = _dot(ub[:, blk], wx_ref[k])
    r_gate = _sigmoid(pre_scr[0] + vec_ref[V_B_A:V_B_A + 1, :])
    i_gate = _sigmoid(pre_scr[1] + vec_ref[V_B_X:V_B_X + 1, :])
    return r_gate, i_gate


def _rg_decay(r_gate, vec_ref):
    sp = _softplus(-vec_ref[V_LAMBDA:V_LAMBDA + 1, :])
    log_a = -RG_C * r_gate * sp
    one_minus_a2 = jnp.maximum(-_expm1_neg(2.0 * log_a), 1e-30)
    inv_mult = lax.rsqrt(one_minus_a2)
    return jnp.exp(log_a), one_minus_a2 * inv_mult, inv_mult, sp


def _rg_fwd(h, g, w_in, cw, vec, wa, wx, w_out, *, tm):
    t_len, d = h.shape
    nt = t_len // tm
    nq, _, n = w_in.shape
    dr = 2 * n
    width = cw.shape[0]
    past = _past_rows(width)
    nb, bd, _ = wa.shape

    def body(h_ref, g_ref, win_ref, cw_ref, vec_ref, wa_ref, wx_ref, wout_ref, out_ref, hh_ref, hs_ref, gates_ref,
             gbuf, rbuf, pre_scr, tail, edge):
        i = pl.program_id(0)

        @pl.when(i == 0)
        def _():
            tail[...] = jnp.zeros_like(tail)
            edge[...] = jnp.zeros_like(edge)

        h_in = h_ref[...]
        hn = _rms(h_in, g_ref[...])[0].astype(MXU_DT)
        rbuf[pl.ds(0, past), :] = tail[...]
        for q in range(2):
            gbuf[:, q * n:(q + 1) * n] = _dot(hn, win_ref[q])
            rbuf[pl.ds(past, tm), q * n:(q + 1) * n] = _dot(hn, win_ref[2 + q])
        hh_ref[:, 0:dr] = gbuf[...].astype(hh_ref.dtype)
        hh_ref[:, dr:2 * dr] = rbuf[pl.ds(past, tm), :].astype(hh_ref.dtype)
        tail[...] = rbuf[pl.ds(tm, past), :]
        _link_past(rbuf, slice(None), width, tm)
        taps = _conv_taps(rbuf, slice(None), width, tm)
        u = sum(cw_ref[k:k + 1, :] * taps[k] for k in range(width)) + vec_ref[V_CONV_B:V_CONV_B + 1, :]
        r_gate, i_gate = _rg_gates(u, vec_ref, wa_ref, wx_ref, pre_scr, nb, bd)
        for k, kept in enumerate((u, r_gate, i_gate)):
            gates_ref[:, k * dr:(k + 1) * dr] = kept.astype(gates_ref.dtype)
        a, mult, _, _ = _rg_decay(r_gate, vec_ref)
        pre_scr[0] = a
        pre_scr[1] = jnp.where(_valid_rows(i, tm), mult * (i_gate * u), 0.0)
        edge[...] = _scan(pre_scr.at[0], pre_scr.at[1], edge[...], tm, reverse=False)
        hs = pre_scr[1]
        hs_ref[...] = hs
        y = hs * _gelu(gbuf[...])[0]
        out_ref[...] = h_in + _dot(y.astype(MXU_DT), wout_ref[...])

    row = lambda i: (i, 0)
    return pl.pallas_call(
        body, name="rg_fwd", grid=(nt,),
        in_specs=[pl.BlockSpec((tm, d), row), _const((1, d)), _const(w_in.shape), _const(cw.shape), _const(vec.shape),
                  _const(wa.shape), _const(wx.shape), _const(w_out.shape)],
        out_specs=[pl.BlockSpec((tm, d), row), pl.BlockSpec((tm, 2 * dr), row), pl.BlockSpec((tm, dr), row),
                   pl.BlockSpec((tm, 3 * dr), row)],
        out_shape=[jax.ShapeDtypeStruct((t_len, d), F32), jax.ShapeDtypeStruct((t_len, 2 * dr), ACT_DT),
                   jax.ShapeDtypeStruct((t_len, dr), F32), jax.ShapeDtypeStruct((t_len, 3 * dr), ACT_DT)],
        scratch_shapes=[pltpu.VMEM((tm, dr), F32), pltpu.VMEM((past + tm, dr), F32), pltpu.VMEM((2, tm, dr), F32),
                        pltpu.VMEM((past, dr), F32), pltpu.VMEM((F32_ROWS, dr), F32)],
        compiler_params=_params(),
    )(h, g, w_in, cw, vec, wa, wx, w_out)


def _rg_bwd(dh, hh, hs, gates, h, g, w_in, cw, vec, wa, wx, w_out, *, tm, ride=None):
    t_len, d = dh.shape
    nt = t_len // tm
    dr = hs.shape[1]
    n = w_in.shape[2]
    width = cw.shape[0]
    nb, bd, _ = wa.shape
    past = _past_rows(width)
    halo_rows, halo_index = _halo_block(past, tm, nt)
    one = F32_ROWS

    def body(dh_ref, hh_ref, hhp_ref, hs_ref, hsp_ref, gates_ref, h_ref, g_ref, win_ref, cw_ref, vec_ref, wa_ref, wx_ref, wout_ref,
             dhin_ref, dhh_ref, y_ref, hn_ref, dvec_ref, dwa_ref, dwx_ref, dg_ref, rbuf, dbuf, pre_scr, hbuf, abuf, edge):
        i = pl.program_id(0)
        r = nt - 1 - i

        @pl.when(i == 0)
        def _():
            dbuf[pl.ds(tm, past), :] = jnp.zeros((past, dr), F32)
            abuf[pl.ds(tm, one), :] = jnp.zeros((one, dr), F32)
            edge[...] = jnp.zeros_like(edge)
            dvec_ref[...] = jnp.zeros_like(dvec_ref)
            dwa_ref[...] = jnp.zeros_like(dwa_ref)
            dwx_ref[...] = jnp.zeros_like(dwx_ref)
            dg_ref[...] = jnp.zeros_like(dg_ref)

        dh_out = dh_ref[...]
        gb = hh_ref[:, 0:dr].astype(F32)
        prev = hhp_ref[...].astype(F32)[halo_rows - past:, dr:2 * dr]
        rbuf[pl.ds(0, past), :] = jnp.where(r > 0, prev, 0.0)
        rbuf[pl.ds(past, tm), :] = hh_ref[:, dr:2 * dr].astype(F32)
        _link_past(rbuf, slice(None), width, tm)
        taps = _conv_taps(rbuf, slice(None), width, tm)
        ub = gates_ref[:, 0:dr].astype(MXU_DT)
        u, r_gate, i_gate = (gates_ref[:, k * dr:(k + 1) * dr].astype(F32) for k in range(3))
        a, mult, inv_mult, sp = _rg_decay(r_gate, vec_ref)
        hs_t = hs_ref[...]
        hbuf[pl.ds(0, one), :] = jnp.where(r > 0, hsp_ref[...], 0.0)
        hbuf[pl.ds(one, tm), :] = hs_t
        _link_past(hbuf, slice(None), 2, tm)
        h_prev = hbuf[pl.ds(0, tm), :]
        gate, th = _gelu(gb)
        y_ref[...] = (hs_t * gate).astype(y_ref.dtype)
        dy = _dot_nt(dh_out.astype(MXU_DT), wout_ref[...])
        d_gb = (dy * hs_t * _gelu_grad(gb, th)).astype(dhh_ref.dtype)
        dhh_ref[:, 0:dr] = d_gb
        dhn = sum(_dot_nt(d_gb[:, q * n:(q + 1) * n], win_ref[q]) for q in range(2))
        abuf[pl.ds(0, tm), :] = a
        _link_future(abuf, slice(None), 2, tm)
        pre_scr[0] = abuf[pl.ds(one, tm), :]
        pre_scr[1] = dy * gate
        edge[...] = _scan(pre_scr.at[0], pre_scr.at[1], edge[...], tm, reverse=True)
        abuf[pl.ds(tm, one), :] = abuf[pl.ds(0, one), :]
        d_hs = pre_scr[1]
        d_b = jnp.where(_valid_rows(r, tm), d_hs, 0.0)
        d_iu = d_b * mult
        d_log_a = d_hs * h_prev * a - d_b * (i_gate * u) * (a * a) * inv_mult
        dvec_ref[G_LAMBDA:G_LAMBDA + 1, :] += jnp.sum(d_log_a * r_gate, axis=0, keepdims=True) * (-RG_C)
        d_pre_r = d_log_a * (-RG_C * sp) * r_gate * (1.0 - r_gate)
        d_pre_i = d_iu * u * i_gate * (1.0 - i_gate)
        dvec_ref[G_B_A:G_B_A + 1, :] += jnp.sum(d_pre_r, axis=0, keepdims=True)
        dvec_ref[G_B_X:G_B_X + 1, :] += jnp.sum(d_pre_i, axis=0, keepdims=True)
        dbuf[pl.ds(0, tm), :] = d_iu * i_gate
        d_pre_r = d_pre_r.astype(MXU_DT)
        d_pre_i = d_pre_i.astype(MXU_DT)
        for k in range(nb):
            blk = slice(k * bd, (k + 1) * bd)
            dwa_ref[k] += _dot_tn(ub[:, blk], d_pre_r[:, blk])
            dwx_ref[k] += _dot_tn(ub[:, blk], d_pre_i[:, blk])
            dbuf[pl.ds(0, tm), blk] += _dot_nt(d_pre_r[:, blk], wa_ref[k]) + _dot_nt(d_pre_i[:, blk], wx_ref[k])
        du = dbuf[pl.ds(0, tm), :]
        dvec_ref[G_CONV_B:G_CONV_B + 1, :] += jnp.sum(du, axis=0, keepdims=True)
        for k in range(width):
            dvec_ref[G_CONV_W + k:G_CONV_W + k + 1, :] += jnp.sum(taps[k] * du, axis=0, keepdims=True)
        _link_future(dbuf, slice(None), width, tm)
        d_rb = _conv_back(dbuf, cw_ref, slice(None), width, tm)
        dbuf[pl.ds(tm, past), :] = dbuf[pl.ds(0, past), :]
        d_rb = d_rb.astype(dhh_ref.dtype)
        dhh_ref[:, dr:2 * dr] = d_rb
        dhn = dhn + sum(_dot_nt(d_rb[:, q * n:(q + 1) * n], win_ref[2 + q]) for q in range(2))
        dhin_ref[...] = _norm_bwd_tile(dhn, h_ref[...], dh_out, g_ref[...], _valid_rows(r, tm), hn_ref, dg_ref)

        @pl.when(i == nt - 1)
        def _():
            lam = vec_ref[V_LAMBDA:V_LAMBDA + 1, :]
            dvec_ref[G_LAMBDA:G_LAMBDA + 1, :] = dvec_ref[G_LAMBDA:G_LAMBDA + 1, :] * (-_sigmoid(-lam))

    rev = lambda i: (nt - 1 - i, 0)
    return _launch(
        body, [dh, hh, hh, hs, hs, gates, h, g, w_in, cw, vec, wa, wx, w_out], name="rg_bwd", grid=(nt,),
        in_specs=[pl.BlockSpec((tm, d), rev), pl.BlockSpec((tm, 2 * dr), rev), pl.BlockSpec((halo_rows, 2 * dr), halo_index),
                  pl.BlockSpec((tm, dr), rev),
                  pl.BlockSpec((one, dr), lambda i: (jnp.maximum((nt - 1 - i) * (tm // one) - 1, 0), 0)),
                  pl.BlockSpec((tm, 3 * dr), rev), pl.BlockSpec((tm, d), rev), _const((1, d)), _const(w_in.shape),
                  _const(cw.shape), _const(vec.shape), _const(wa.shape), _const(wx.shape), _const(w_out.shape)],
        out_specs=[pl.BlockSpec((tm, d), rev), pl.BlockSpec((tm, 2 * dr), rev), pl.BlockSpec((tm, dr), rev),
                   pl.BlockSpec((tm, d), rev), _const((F32_ROWS, dr)), _const(wa.shape), _const(wx.shape),
                   _const((F32_ROWS, d))],
        out_shape=[jax.ShapeDtypeStruct((t_len, d), F32), jax.ShapeDtypeStruct((t_len, 2 * dr), ACT_DT),
                   jax.ShapeDtypeStruct((t_len, dr), ACT_DT), jax.ShapeDtypeStruct((t_len, d), ACT_DT),
                   jax.ShapeDtypeStruct((F32_ROWS, dr), F32), jax.ShapeDtypeStruct(wa.shape, F32),
                   jax.ShapeDtypeStruct(wx.shape, F32), jax.ShapeDtypeStruct((F32_ROWS, d), F32)],
        scratch_shapes=[pltpu.VMEM((past + tm, dr), F32), pltpu.VMEM((tm + past, dr), F32), pltpu.VMEM((2, tm, dr), F32),
                        pltpu.VMEM((one + tm, dr), F32), pltpu.VMEM((tm + one, dr), F32), pltpu.VMEM((F32_ROWS, dr), F32)],
        ride=ride,
    )


def _weight_grad(a, b, nb, *, rows, ride=None):
    t_len, k_dim = a.shape
    n = b.shape[1] // nb
    nt = t_len // rows

    def body(a_ref, b_ref, out_ref, wire_ref):
        @pl.when(pl.program_id(1) == 0)
        def _():
            out_ref[...] = jnp.zeros_like(out_ref)

        out_ref[0] += _dot_tn(a_ref[...].astype(MXU_DT), b_ref[...].astype(MXU_DT))

        @pl.when(pl.program_id(1) == nt - 1)
        def _():
            wire_ref[...] = out_ref[...].astype(wire_ref.dtype)

    block = pl.BlockSpec((1, k_dim, n), lambda j, i: (j, 0, 0))
    return _launch(
        body, [a, b], name="weight_grad", grid=(nb, nt),
        in_specs=[pl.BlockSpec((rows, k_dim), lambda j, i: (i, 0)), pl.BlockSpec((rows, n), lambda j, i: (i, j))],
        out_specs=[block, block],
        out_shape=[jax.ShapeDtypeStruct((nb, k_dim, n), F32), jax.ShapeDtypeStruct((nb, k_dim, n), WIRE_DT)],
        ride=ride,
    )


def _adamw(w, m, v, parts, *, rows, layer=0, into=None):
    n_layers, n_rows, n_cols = w.shape
    nt = n_rows // rows
    n_parts = len(parts)

    def body(w_ref, m_ref, v_ref, *rest):
        part_refs, (g_ref, d_ref, nm_ref, nv_ref) = rest[:n_parts], rest[-4:]
        w_ref, m_ref, v_ref, g_ref, d_ref, nm_ref, nv_ref = (r.at[0] for r in (w_ref, m_ref, v_ref, g_ref, d_ref, nm_ref, nv_ref))
        grad = part_refs[0][...].astype(F32)
        for p in part_refs[1:]:
            grad = grad + p[...].astype(F32)
        new_m = ADAM_B1 * m_ref[...] + (1.0 - ADAM_B1) * grad
        new_v = ADAM_B2 * v_ref[...] + (1.0 - ADAM_B2) * (grad * grad)
        m_hat = new_m / (1.0 - ADAM_B1 ** ADAM_STEP)
        v_hat = new_v / (1.0 - ADAM_B2 ** ADAM_STEP)
        g_ref[...] = grad
        d_ref[...] = -ADAM_LR * (m_hat / (jnp.sqrt(v_hat) + ADAM_EPS) + ADAM_WD * w_ref[...])
        nm_ref[...] = new_m
        nv_ref[...] = new_v

    spec = pl.BlockSpec((rows, n_cols), lambda i: (i, 0))
    layer_spec = pl.BlockSpec((1, rows, n_cols), lambda i: (layer, i, 0))
    into = list(into or [])
    return pl.pallas_call(
        body, name="adamw", grid=(nt,),
        in_specs=[layer_spec] * 3 + [spec] * n_parts + [ANY] * len(into), out_specs=[layer_spec] * 4,
        out_shape=[jax.ShapeDtypeStruct(w.shape, F32)] * 4,
        input_output_aliases={3 + n_parts + k: k for k in range(len(into))},
        compiler_params=_params(),
    )(w, m, v, *parts, *into)


def _sum_stack(stack, *, rows):
    n_stack, n_rows, n_cols = stack.shape

    def body(stack_ref, out_ref):
        acc = stack_ref[0]
        for j in range(1, n_stack):
            acc = acc + stack_ref[j]
        out_ref[...] = acc

    return pl.pallas_call(
        body, name="sum_stack", grid=(n_rows // rows,),
        in_specs=[pl.BlockSpec((n_stack, rows, n_cols), lambda i: (0, i, 0))],
        out_specs=pl.BlockSpec((rows, n_cols), lambda i: (i, 0)),
        out_shape=jax.ShapeDtypeStruct((n_rows, n_cols), F32),
        compiler_params=_params(),
    )(stack)


def _sum_parts(own, recv, *, rows):
    n_rows, n_cols = own.shape
    n_recv = recv.shape[0]

    def body(own_ref, recv_ref, out_ref):
        acc = own_ref[...].astype(F32)
        for j in range(n_recv):
            acc = acc + recv_ref[j].astype(F32)
        out_ref[...] = acc

    return pl.pallas_call(
        body, name="sum_parts", grid=(n_rows // rows,),
        in_specs=[pl.BlockSpec((rows, n_cols), lambda i: (i, 0)), pl.BlockSpec((n_recv, rows, n_cols), lambda i: (0, i, 0))],
        out_specs=pl.BlockSpec((rows, n_cols), lambda i: (i, 0)),
        out_shape=jax.ShapeDtypeStruct(own.shape, F32),
        compiler_params=_params(),
    )(own, recv)


def _swap_cores(arrays):
    nk = len(arrays)

    def body(*refs):
        ins, outs, (send_sems, recv_sems) = refs[:nk], refs[nk:2 * nk], refs[2 * nk:]
        x, y, c = _place()
        sends = [pltpu.make_async_remote_copy(
            src_ref=ins[k], dst_ref=outs[k], send_sem=send_sems.at[k], recv_sem=recv_sems.at[k],
            device_id=(x, y, 1 - c), device_id_type=MESH_ID) for k in range(nk)]
        for cp in sends:
            cp.start()
        for cp in sends:
            cp.wait_recv()
        for cp in sends:
            cp.wait_send()

    return pl.pallas_call(
        body, name="swap_cores", in_specs=[ANY] * nk, out_specs=[ANY] * nk,
        out_shape=[jax.ShapeDtypeStruct(a.shape, a.dtype) for a in arrays],
        scratch_shapes=[pltpu.SemaphoreType.DMA((nk,)), pltpu.SemaphoreType.DMA((nk,))],
    )(*arrays)


class _AllDevices:
    def __init__(self, arrays):
        nk = len(arrays)
        self.arrays = list(arrays)
        self.out_shape = [jax.ShapeDtypeStruct((8,) + a.shape, a.dtype) for a in arrays]
        self.scratch = [pltpu.SemaphoreType.DMA((nk, 7)), pltpu.SemaphoreType.DMA((nk, 7)), pltpu.SemaphoreType.DMA((nk,))]

    def run(self, ins, outs, sems, start):
        send_sems, recv_sems, local_sems = sems
        x, y, c = _place()
        mine = 4 * x + 2 * y + c
        for k in range(len(ins)):
            local = pltpu.make_async_copy(ins[k], outs[k].at[mine], local_sems.at[k])
            local.start() if start else local.wait()
            for flip in range(1, 8):
                px, py, pc = x ^ (flip >> 2), y ^ ((flip >> 1) & 1), c ^ (flip & 1)
                sems_f = dict(send_sem=send_sems.at[k, flip - 1], recv_sem=recv_sems.at[k, flip - 1],
                              device_id=(px, py, pc), device_id_type=MESH_ID)
                send = pltpu.make_async_remote_copy(src_ref=ins[k], dst_ref=outs[k].at[mine], **sems_f)
                if start:
                    send.start()
                else:
                    pltpu.make_async_remote_copy(src_ref=ins[k], dst_ref=outs[k].at[4 * px + 2 * py + pc], **sems_f).wait_recv()
                    send.wait_send()


class _Both:
    def __init__(self, first, second):
        self.rides = (first, second)
        self.arrays = first.arrays + second.arrays
        self.out_shape = first.out_shape + second.out_shape
        self.scratch = first.scratch + second.scratch

    def run(self, ins, outs, sems, start):
        for ride in self.rides:
            n_in, n_out, n_sem = len(ride.arrays), len(ride.out_shape), len(ride.scratch)
            ride.run(ins[:n_in], outs[:n_out], sems[:n_sem], start)
            ins, outs, sems = ins[n_in:], outs[n_out:], sems[n_sem:]


def _pack(arrays, pad_rows=F32_ROWS):
    flat = jnp.concatenate([a.reshape(-1).astype(F32) for a in arrays])
    rows = -(-flat.shape[0] // (LANES * pad_rows)) * pad_rows
    return jnp.pad(flat, (0, rows * LANES - flat.shape[0])).reshape(rows, LANES)


def _unpack(packed, shapes):
    flat, out, off = packed.reshape(-1), [], 0
    for s in shapes:
        size = 1
        for dim in s:
            size *= dim
        out.append(flat[off:off + size].reshape(s))
        off += size
    return out


def _divisor_rows(n_rows, most=256):
    best = None
    for r in range(ACT_ROWS, most + 1, ACT_ROWS):
        if n_rows % r == 0:
            best = r
    return best or n_rows


def kernel(x, meta_tokens, norm_mix_g, norm_ffn_g, final_norm_g, sc_w_in, sc_conv_w, sc_w_out, rg_w_in, rg_conv_w, rg_conv_b, rg_w_gate_a, rg_b_gate_a, rg_w_gate_x, rg_b_gate_x, rg_lambda, rg_w_out, ffn_w_up, ffn_conv_w, ffn_w_down, loss_target, m_meta_tokens, m_norm_mix_g, m_norm_ffn_g, m_final_norm_g, m_sc_w_in, m_sc_conv_w, m_sc_w_out, m_rg_w_in, m_rg_conv_w, m_rg_conv_b, m_rg_w_gate_a, m_rg_b_gate_a, m_rg_w_gate_x, m_rg_b_gate_x, m_rg_lambda, m_rg_w_out, m_ffn_w_up, m_ffn_conv_w, m_ffn_w_down, v_meta_tokens, v_norm_mix_g, v_norm_ffn_g, v_final_norm_g, v_sc_w_in, v_sc_conv_w, v_sc_w_out, v_rg_w_in, v_rg_conv_w, v_rg_conv_b, v_rg_w_gate_a, v_rg_b_gate_a, v_rg_w_gate_x, v_rg_b_gate_x, v_rg_lambda, v_rg_w_out, v_ffn_w_up, v_ffn_conv_w, v_ffn_w_down):
    weights = dict(meta_tokens=meta_tokens, norm_mix_g=norm_mix_g, norm_ffn_g=norm_ffn_g, final_norm_g=final_norm_g, sc_w_in=sc_w_in, sc_conv_w=sc_conv_w, sc_w_out=sc_w_out, rg_w_in=rg_w_in, rg_conv_w=rg_conv_w, rg_conv_b=rg_conv_b, rg_w_gate_a=rg_w_gate_a, rg_b_gate_a=rg_b_gate_a, rg_w_gate_x=rg_w_gate_x, rg_b_gate_x=rg_b_gate_x, rg_lambda=rg_lambda, rg_w_out=rg_w_out, ffn_w_up=ffn_w_up, ffn_conv_w=ffn_conv_w, ffn_w_down=ffn_w_down)
    m_in = dict(meta_tokens=m_meta_tokens, norm_mix_g=m_norm_mix_g, norm_ffn_g=m_norm_ffn_g, final_norm_g=m_final_norm_g, sc_w_in=m_sc_w_in, sc_conv_w=m_sc_conv_w, sc_w_out=m_sc_w_out, rg_w_in=m_rg_w_in, rg_conv_w=m_rg_conv_w, rg_conv_b=m_rg_conv_b, rg_w_gate_a=m_rg_w_gate_a, rg_b_gate_a=m_rg_b_gate_a, rg_w_gate_x=m_rg_w_gate_x, rg_b_gate_x=m_rg_b_gate_x, rg_lambda=m_rg_lambda, rg_w_out=m_rg_w_out, ffn_w_up=m_ffn_w_up, ffn_conv_w=m_ffn_conv_w, ffn_w_down=m_ffn_w_down)
    v_in = dict(meta_tokens=v_meta_tokens, norm_mix_g=v_norm_mix_g, norm_ffn_g=v_norm_ffn_g, final_norm_g=v_final_norm_g, sc_w_in=v_sc_w_in, sc_conv_w=v_sc_conv_w, sc_w_out=v_sc_w_out, rg_w_in=v_rg_w_in, rg_conv_w=v_rg_conv_w, rg_conv_b=v_rg_conv_b, rg_w_gate_a=v_rg_w_gate_a, rg_b_gate_a=v_rg_b_gate_a, rg_w_gate_x=v_rg_w_gate_x, rg_b_gate_x=v_rg_b_gate_x, rg_lambda=v_rg_lambda, rg_w_out=v_rg_w_out, ffn_w_up=v_ffn_w_up, ffn_conv_w=v_ffn_conv_w, ffn_w_down=v_ffn_w_down)
    names = list(weights)

    seq, d = x.shape[1:]
    tm = _row_tile(seq)
    tokens, target = _tile_order(x[0], tm), _tile_order(loss_target[0], tm)
    t_len = seq + tm
    wg_rows = 5 * tm if t_len % (5 * tm) == 0 else tm
    wg_rows_in = 13 * tm if t_len % (13 * tm) == 0 else wg_rows
    xi, yi, _ = _place()
    chip = 2 * xi + yi
    mesh_axes = ("x", "y", "c")

    wire = lambda w: w.astype(WIRE_DT)
    small_sharded = ["meta_tokens", "sc_conv_w", "rg_conv_w", "rg_conv_b", "rg_b_gate_a", "rg_b_gate_x", "rg_lambda", "ffn_conv_w"]
    small_2d = {n: weights[n].reshape(-1, weights[n].shape[-1]) for n in small_sharded}
    w_sc_in, w_sc_out, small_by_chip = _exchange(
        _GatherHalves([wire(sc_w_in[0]), wire(sc_w_out[0]), _pack([small_2d[n] for n in small_sharded], 2 * ACT_ROWS)]),
        "gather_first")
    w_sc_out = w_sc_out.reshape(-1, d)
    gather_ffn0 = _Gather([wire(ffn_w_up[0]), wire(ffn_w_down[0])])
    gather_rest = _Gather([wire(rg_w_in[0]), wire(rg_w_out[0]), wire(ffn_w_up[1]), wire(ffn_w_down[1])])
    small_len = sum(a.size for a in small_2d.values())
    by_chip = small_by_chip.reshape(N_CHIPS, -1)[:, :small_len]
    full, off = {}, 0
    for n in small_sharded:
        rows, width = small_2d[n].shape
        full[n] = by_chip[:, off:off + rows * width].reshape(N_CHIPS, rows, width).transpose(1, 0, 2).reshape(rows, N_CHIPS * width)
        off += rows * width
    sc_cw, rg_cw = full["sc_conv_w"], full["rg_conv_w"]
    ffn_cw = [full["ffn_conv_w"][0:3], full["ffn_conv_w"][3:6]]
    d_rnn = rg_cw.shape[1]
    vec = jnp.concatenate([full["rg_conv_b"], full["rg_b_gate_a"], full["rg_b_gate_x"], full["rg_lambda"],
                           jnp.zeros((F32_ROWS - 4, d_rnn), F32)])
    wa, wx = rg_w_gate_a[0].astype(MXU_DT), rg_w_gate_x[0].astype(MXU_DT)
    first = _tile_order(jnp.concatenate([jnp.zeros((tm - N_META, d), F32), full["meta_tokens"]]), tm)
    g_mix = [norm_mix_g[0:1], norm_mix_g[1:2]]
    g_ffn = [norm_ffn_g[0:1], norm_ffn_g[1:2]]

    h1, hh0, w_up0, w_dn0 = _sc_fwd(tokens, first, g_mix[0], w_sc_in, sc_cw, w_sc_out, tm=tm, ride=gather_ffn0)
    h2, hu0, w_rg_in, w_rg_out, w_up1, w_dn1 = _ffn_fwd(h1, g_ffn[0], w_up0, ffn_cw[0], w_dn0.reshape(-1, d), tm=tm,
                                                         ride=gather_rest)
    w_up, w_dn, w_rg_out = [w_up0, w_up1], [w_dn0.reshape(-1, d), w_dn1.reshape(-1, d)], w_rg_out.reshape(-1, d)
    h3, hh1, hs, gates = _rg_fwd(h2, g_mix[1], w_rg_in, rg_cw, vec, wa, wx, w_rg_out, tm=tm)
    dh4, hu1, sq, d_final = _ffn_fwd(h3, g_ffn[1], w_up[1], ffn_cw[1], w_dn[1], tm=tm,
                                     loss=(target, final_norm_g.reshape(1, d)))
    loss = lax.psum(jnp.sum(sq[0]) * (0.5 / d), mesh_axes)

    def by_chip_rows(pair):
        return [p.reshape(N_CHIPS, -1, d) for p in pair]

    def ffn_backward(dh_out, h_in, hu, layer, ride):
        dh_in, act, dhu, hn, dcw, dg, *landed = _ffn_bwd(dh_out, hu, h_in, g_ffn[layer], w_up[layer], ffn_cw[layer],
                                                         w_dn[layer], tm=tm, ride=ride)
        d_up = _weight_grad(hn, dhu, N_CHIPS, rows=wg_rows_in)
        d_dn = by_chip_rows(_weight_grad(act, dh_out, 1, rows=wg_rows))
        return dh_in, d_up, d_dn, dcw[0:3], dg[0], landed

    dh3, d_up1, d_dn1, d_fcw1, d_gf1, _ = ffn_backward(dh4, h3, hu1, 1, None)
    dh2, dhh1, y_rg, hn_rg, d_vec, d_wa, d_wx, d_gm1, *landed_ffn1 = _rg_bwd(
        dh3, hh1, hs, gates, h2, g_mix[1], w_rg_in, rg_cw, vec, wa, wx, w_rg_out, tm=tm,
        ride=_Scatter([d_up1[1], d_dn1[1]]))
    d_rg_in = _weight_grad(hn_rg, dhh1, N_CHIPS, rows=wg_rows_in)
    d_rg_out = by_chip_rows(_weight_grad(y_rg, dh3, 1, rows=wg_rows))
    early = {"rg_conv_w": d_vec[G_CONV_W:G_CONV_W + 4], "rg_conv_b": d_vec[G_CONV_B:G_CONV_B + 1],
             "rg_b_gate_a": d_vec[G_B_A:G_B_A + 1], "rg_b_gate_x": d_vec[G_B_X:G_B_X + 1],
             "rg_lambda": d_vec[G_LAMBDA:G_LAMBDA + 1], "ffn_conv_w.1": d_fcw1, "norm_mix_g.1": d_gm1[0:1],
             "norm_ffn_g.1": d_gf1[None], "final_norm_g": d_final[0]}
    early_packed = _pack(list(early.values()))
    gate_names = ["rg_w_gate_a", "rg_w_gate_x"]
    to_all = _AllDevices([early_packed, d_wa.reshape(-1, LANES), d_wx.reshape(-1, LANES)])
    dh1, d_up0, d_dn0, d_fcw0, d_gf0, landed = ffn_backward(
        dh2, h1, hu0, 0, _Both(_Scatter([d_rg_in[1], d_rg_out[1]]), to_all))
    landed_rg, early_by_device, gates_by_device = landed[0:2], landed[2], landed[3:]
    grad_x, dhh0, z_sc, hn_sc, d_sccw, d_gm0, d_first, *landed_ffn0 = _sc_bwd(
        dh1, hh0, tokens, first, g_mix[0], w_sc_in, sc_cw, w_sc_out, tm=tm, ride=_Scatter([d_up0[1], d_dn0[1]]))
    d_sc_in = _weight_grad(hn_sc, dhh0, N_CHIPS, rows=wg_rows_in)
    *d_sc_out, landed_sc_in = _weight_grad(z_sc, dh1, 1, rows=wg_rows, ride=_Scatter([d_sc_in[1]]))
    d_sc_out = by_chip_rows(d_sc_out)
    landed_sc = [landed_sc_in, *_exchange(_Scatter([d_sc_out[1]]), "scatter_last")]
    grad_x = _time_order(grad_x, tm)[None]

    big = [("sc_w_in", 0, d_sc_in, landed_sc[0]), ("sc_w_out", 0, d_sc_out, landed_sc[1]),
           ("rg_w_in", 0, d_rg_in, landed_rg[0]), ("rg_w_out", 0, d_rg_out, landed_rg[1]),
           ("ffn_w_up", 0, d_up0, landed_ffn0[0]), ("ffn_w_up", 1, d_up1, landed_ffn1[0]),
           ("ffn_w_down", 0, d_dn0, landed_ffn0[1]), ("ffn_w_down", 1, d_dn1, landed_ffn1[1])]
    core_sum = []
    for _, _, (partial, _), received in big:
        own = lax.dynamic_index_in_dim(partial, chip, 0, keepdims=False)
        core_sum.append(_sum_parts(own, received, rows=_divisor_rows(own.shape[0])))
    other_sum = _swap_cores(core_sum)
    out = {k: {} for k in ("grad", "delta", "m", "v")}
    stacked = {}
    for (n, layer, _, _), mine, theirs in zip(big, core_sum, other_sum):
        stacked[n] = _adamw(weights[n], m_in[n], v_in[n], [mine, theirs], rows=_divisor_rows(mine.shape[0]), layer=layer,
                            into=stacked.get(n))
    for n, res in stacked.items():
        for k, key in enumerate(("grad", "delta", "m", "v")):
            out[key][n] = res[k]

    late = {"meta_tokens": _time_order(d_first, tm)[tm - N_META:], "sc_conv_w": d_sccw[0:3], "ffn_conv_w.0": d_fcw0,
            "norm_mix_g.0": d_gm0[0:1], "norm_ffn_g.0": d_gf0[None]}
    late_packed = _pack(list(late.values()))
    late_by_device, = _exchange(_AllDevices([late_packed]), "gather_devices")
    summed = {}
    for parts, packed, by_device in ((early, early_packed, early_by_device), (late, late_packed, late_by_device)):
        total = _sum_stack(by_device, rows=packed.shape[0])
        summed.update(zip(parts, _unpack(total, [p.shape for p in parts.values()])))
    for n in ("ffn_conv_w", "norm_mix_g", "norm_ffn_g"):
        summed[n] = jnp.concatenate([summed.pop(n + ".0"), summed.pop(n + ".1")])
    for n, by_device in zip(gate_names, gates_by_device):
        as_rows = lambda a: a.reshape(1, -1, LANES)
        res = _adamw(as_rows(weights[n]), as_rows(m_in[n]), as_rows(v_in[n]), [_sum_stack(by_device, rows=256)], rows=256)
        for k, key in enumerate(("grad", "delta", "m", "v")):
            out[key][n] = res[k].reshape(weights[n].shape)
    replicated = ["norm_mix_g", "norm_ffn_g", "final_norm_g"]
    small_names = small_sharded + replicated
    grads = {}
    for n in small_sharded:
        width = small_2d[n].shape[1]
        grads[n] = lax.dynamic_slice_in_dim(summed[n], chip * width, width, axis=1).reshape(weights[n].shape)
    for n in replicated:
        grads[n] = summed[n].reshape(weights[n].shape)
    shapes = [weights[n].shape for n in small_names]
    packed_w = _pack([weights[n] for n in small_names])
    res = _adamw(packed_w[None], _pack([m_in[n] for n in small_names])[None], _pack([v_in[n] for n in small_names])[None],
                 [_pack([grads[n] for n in small_names])], rows=packed_w.shape[0])
    for k, key in enumerate(("grad", "delta", "m", "v")):
        out[key].update(dict(zip(small_names, _unpack(res[k][0], shapes))))

    return (loss, grad_x, *[out["grad"][n] for n in names], *[out["delta"][n] for n in names],
            *[out["m"][n] for n in names], *[out["v"][n] for n in names])
```

```python
import functools

import jax
import jax.numpy as jnp
from jax import lax
from jax.experimental import pallas as pl
from jax.experimental.pallas import tpu as pltpu

F32 = jnp.float32
MXU_DT = jnp.bfloat16
ACT_DT = jnp.bfloat16
WIRE_DT = jnp.bfloat16
MESH_ID = pl.DeviceIdType.MESH

N_META = 16
RMS_EPS = 1e-6
RG_C = 8.0
ADAM_LR, ADAM_B1, ADAM_B2, ADAM_EPS, ADAM_WD, ADAM_STEP = 0.001, 0.9, 0.999, 1e-08, 0.01, 10
N_CHIPS = 4
VMEM_LIMIT = 60 * 1024 * 1024
F32_ROWS = 8
ACT_ROWS = 16
LANES = 128


def _row_tile(seq):
    for tm in (256, 128, 64, 32, 16):
        if seq % tm == 0:
            return tm
    raise ValueError(f"sequence length {seq} is not a multiple of 16")


def _params(n_axes=1, **kw):
    return pltpu.CompilerParams(dimension_semantics=("arbitrary",) * n_axes, vmem_limit_bytes=VMEM_LIMIT, **kw)


def _const(shape):
    return pl.BlockSpec(shape, lambda *_: (0,) * len(shape), pipeline_mode=pl.Buffered(1))


def _dot(a, b):
    return jnp.dot(a, b, preferred_element_type=F32)


def _dot_nt(a, b):
    return lax.dot_general(a, b, (((1,), (1,)), ((), ())), preferred_element_type=F32)


def _dot_tn(a, b):
    return lax.dot_general(a, b, (((0,), (0,)), ((), ())), preferred_element_type=F32)


def _sigmoid(x):
    return 1.0 / (1.0 + jnp.exp(-x))


def _rms(h, g):
    rstd = lax.rsqrt(jnp.mean(h * h, axis=-1, keepdims=True) + RMS_EPS)
    xhat = h * rstd
    return xhat * g, xhat, rstd


def _rms_bwd(dhn, xhat, rstd, g):
    dx = dhn * g
    return rstd * (dx - xhat * jnp.mean(dx * xhat, axis=-1, keepdims=True))


def _gelu(x):
    k = 0.7978845608028654
    t = jnp.tanh(k * (x + 0.044715 * x * x * x))
    return 0.5 * x * (1.0 + t), t


def _gelu_grad(x, t):
    k = 0.7978845608028654
    return 0.5 * (1.0 + t) + 0.5 * x * (1.0 - t * t) * k * (1.0 + 3 * 0.044715 * x * x)


def _softplus(x):
    e = jnp.exp(-jnp.abs(x))
    return jnp.maximum(x, 0.0) + jnp.where(e < 1e-4, e - 0.5 * e * e, jnp.log(1.0 + e))


def _expm1_neg(z):
    series = z * (1.0 + z * (0.5 + z * (1.0 / 6)))
    return jnp.where(z > -0.02, series, jnp.exp(z) - 1.0)


def _tile_order(a, tm):
    return a.reshape(-1, F32_ROWS, tm // F32_ROWS, a.shape[-1]).swapaxes(1, 2).reshape(a.shape)


def _time_order(a, tm):
    return a.reshape(-1, tm // F32_ROWS, F32_ROWS, a.shape[-1]).swapaxes(1, 2).reshape(a.shape)


def _valid_rows(tile, tm):
    row = lax.broadcasted_iota(jnp.int32, (tm, 1), 0)
    time = (row & (F32_ROWS - 1)) * (tm // F32_ROWS) + (row >> 3) + tile * tm
    return time >= tm - N_META


def _sublane():
    return lax.broadcasted_iota(jnp.int32, (F32_ROWS, 1), 0)


def _past_rows(width):
    return (width - 1) * F32_ROWS


def _halo_block(past, tm, nt):
    rows = -(-past // ACT_ROWS) * ACT_ROWS
    return rows, lambda i: (jnp.maximum((nt - 1 - i) * (tm // rows) - 1, 0), 0)


def _link_past(buf, cols, width, tm):
    past = _past_rows(width)
    for k in range(1, width):
        rows = pl.ds(past - F32_ROWS * k, F32_ROWS)
        before = pltpu.roll(buf[rows, cols], 1, 0)
        mine = pltpu.roll(buf[pl.ds(past + tm - F32_ROWS * k, F32_ROWS), cols], 1, 0)
        buf[rows, cols] = jnp.where(_sublane() == 0, before, mine)


def _link_future(buf, cols, width, tm):
    for k in range(1, width):
        rows = pl.ds(tm + F32_ROWS * (k - 1), F32_ROWS)
        after = pltpu.roll(buf[rows, cols], F32_ROWS - 1, 0)
        mine = pltpu.roll(buf[pl.ds(F32_ROWS * (k - 1), F32_ROWS), cols], F32_ROWS - 1, 0)
        buf[rows, cols] = jnp.where(_sublane() == F32_ROWS - 1, after, mine)


def _conv_taps(buf, cols, width, tm):
    return [buf[pl.ds(F32_ROWS * k, tm), cols] for k in range(width)]


def _conv_back(buf, cw_ref, cols, width, tm):
    return sum(cw_ref[k:k + 1, cols] * buf[pl.ds(F32_ROWS * (width - 1 - k), tm), cols] for k in range(width))


ANY = pl.BlockSpec(memory_space=pl.ANY)


def _place():
    return lax.axis_index("x"), lax.axis_index("y"), lax.axis_index("c")


def _other_chips(x, y):
    return [(1 - x, y), (x, 1 - y), (1 - x, 1 - y)]


class _Gather:
    def __init__(self, shards):
        nk = len(shards)
        self.arrays = list(shards)
        self.out_shape = [jax.ShapeDtypeStruct((N_CHIPS,) + s.shape, s.dtype) for s in shards]
        self.scratch = [pltpu.SemaphoreType.DMA((nk, 3)), pltpu.SemaphoreType.DMA((nk, 3)), pltpu.SemaphoreType.DMA((nk,))]

    def run(self, ins, outs, sems, start):
        send_sems, recv_sems, local_sems = sems
        x, y, c = _place()
        mine = 2 * x + y
        for k in range(len(ins)):
            local = pltpu.make_async_copy(ins[k], outs[k].at[mine], local_sems.at[k])
            local.start() if start else local.wait()
            for j, (px, py) in enumerate(_other_chips(x, y)):
                sems_kj = dict(send_sem=send_sems.at[k, j], recv_sem=recv_sems.at[k, j], device_id=(px, py, c),
                               device_id_type=MESH_ID)
                send = pltpu.make_async_remote_copy(src_ref=ins[k], dst_ref=outs[k].at[mine], **sems_kj)
                if start:
                    send.start()
                else:
                    pltpu.make_async_remote_copy(src_ref=ins[k], dst_ref=outs[k].at[2 * px + py], **sems_kj).wait_recv()
                    send.wait_send()


class _GatherHalves:
    def __init__(self, shards):
        nk = len(shards)
        self.arrays = list(shards)
        self.out_shape = [jax.ShapeDtypeStruct((N_CHIPS,) + s.shape, s.dtype) for s in shards]
        self.scratch = [pltpu.SemaphoreType.DMA((nk, 3)) for _ in range(4)] + [pltpu.SemaphoreType.DMA((nk,))]

    def run(self, ins, outs, sems, start):
        far_send, far_recv, near_send, near_recv, local_sems = sems
        x, y, c = _place()
        mine = 2 * x + y
        for phase in ((0,) if start else (1, 2)):
            for k in range(len(ins)):
                half = ins[k].shape[0] // 2
                my_half = pl.ds(pl.multiple_of(c * half, ACT_ROWS), half)
                other_half = pl.ds(pl.multiple_of((1 - c) * half, ACT_ROWS), half)
                if phase != 1:
                    local = pltpu.make_async_copy(ins[k], outs[k].at[mine], local_sems.at[k])
                    local.start() if phase == 0 else local.wait()
                for j, (px, py) in enumerate(_other_chips(x, y)):
                    theirs = 2 * px + py
                    far = dict(send_sem=far_send.at[k, j], recv_sem=far_recv.at[k, j], device_id=(px, py, c),
                               device_id_type=MESH_ID)
                    near = dict(send_sem=near_send.at[k, j], recv_sem=near_recv.at[k, j], device_id=(x, y, 1 - c),
                                device_id_type=MESH_ID)
                    landed = outs[k].at[theirs, my_half]
                    send = lambda: pltpu.make_async_remote_copy(src_ref=ins[k].at[my_half], dst_ref=outs[k].at[mine, my_half], **far)
                    pass_on = lambda: pltpu.make_async_remote_copy(src_ref=landed, dst_ref=landed, **near)
                    if phase == 0:
                        send().start()
                    elif phase == 1:
                        pltpu.make_async_remote_copy(src_ref=ins[k].at[my_half], dst_ref=landed, **far).wait_recv()
                        pass_on().start()
                    else:
                        pltpu.make_async_remote_copy(src_ref=landed, dst_ref=outs[k].at[theirs, other_half], **near).wait_recv()
                        pass_on().wait_send()
                        send().wait_send()


class _Scatter:
    def __init__(self, parts):
        nk = len(parts)
        self.arrays = list(parts)
        self.out_shape = [jax.ShapeDtypeStruct((3,) + p.shape[1:], p.dtype) for p in parts]
        self.scratch = [pltpu.SemaphoreType.DMA((nk, 3)), pltpu.SemaphoreType.DMA((nk, 3))]

    def run(self, ins, outs, sems, start):
        send_sems, recv_sems = sems
        x, y, c = _place()
        for k in range(len(ins)):
            for j, (px, py) in enumerate(_other_chips(x, y)):
                send = pltpu.make_async_remote_copy(
                    src_ref=ins[k].at[2 * px + py], dst_ref=outs[k].at[j], send_sem=send_sems.at[k, j],
                    recv_sem=recv_sems.at[k, j], device_id=(px, py, c), device_id_type=MESH_ID)
                if start:
                    send.start()
                else:
                    send.wait_recv()
                    send.wait_send()


def _exchange(ride, name):
    n_in, n_out = len(ride.arrays), len(ride.out_shape)

    def body(*refs):
        ride.run(refs[:n_in], refs[n_in:n_in + n_out], refs[n_in + n_out:], start=True)
        ride.run(refs[:n_in], refs[n_in:n_in + n_out], refs[n_in + n_out:], start=False)

    return pl.pallas_call(body, name=name, in_specs=[ANY] * n_in, out_specs=[ANY] * n_out, out_shape=ride.out_shape,
                          scratch_shapes=ride.scratch)(*ride.arrays)


def _launch(body, operands, *, name, grid, in_specs, out_specs, out_shape, scratch_shapes=(), ride=None):
    common = dict(name=name, grid=grid, compiler_params=_params(len(grid)))
    if ride is None:
        return pl.pallas_call(body, in_specs=in_specs, out_specs=out_specs, out_shape=out_shape,
                              scratch_shapes=list(scratch_shapes), **common)(*operands)
    n_in, n_out, n_scr = len(operands), len(out_shape), len(scratch_shapes)
    r_in, r_out = len(ride.arrays), len(ride.out_shape)

    def riding(*refs):
        ins, refs = refs[:n_in], refs[n_in:]
        r_ins, refs = refs[:r_in], refs[r_in:]
        outs, refs = refs[:n_out], refs[n_out:]
        r_outs, refs = refs[:r_out], refs[r_out:]
        scr, r_sems = refs[:n_scr], refs[n_scr:]
        step = [pl.program_id(axis) for axis in range(len(grid))]
        first = functools.reduce(jnp.logical_and, [s == 0 for s in step])
        last = functools.reduce(jnp.logical_and, [s == size - 1 for s, size in zip(step, grid)])

        @pl.when(first)
        def _():
            ride.run(r_ins, r_outs, r_sems, start=True)

        body(*ins, *outs, *scr)

        @pl.when(last)
        def _():
            ride.run(r_ins, r_outs, r_sems, start=False)

    return pl.pallas_call(
        riding, in_specs=list(in_specs) + [ANY] * r_in, out_specs=list(out_specs) + [ANY] * r_out,
        out_shape=list(out_shape) + ride.out_shape, scratch_shapes=list(scratch_shapes) + ride.scratch, **common,
    )(*operands, *ride.arrays)


def _sc_fwd(x, first, g, w_in, cw, w_out, *, tm, ride=None):
    seq, d = x.shape
    nt = seq // tm + 1
    nq, _, n = w_in.shape
    width = cw.shape[0]
    past = _past_rows(width)

    def body(x_ref, first_ref, g_ref, win_ref, cw_ref, wout_ref, h1_ref, hh_ref, hh_scr, cbuf):
        i = pl.program_id(0)

        @pl.when(i == 0)
        def _():
            cbuf[pl.ds(0, past), :] = jnp.zeros((past, d), F32)

        h = jnp.where(i == 0, first_ref[...], x_ref[...])
        hn = _rms(h, g_ref[...])[0].astype(MXU_DT)
        for q in range(nq):
            hh_scr[:, q * n:(q + 1) * n] = _dot(hn, win_ref[q])
        hh_ref[...] = hh_scr[...].astype(hh_ref.dtype)
        b = hh_scr[:, 0:d]
        cbuf[pl.ds(past, tm), :] = hh_scr[:, d:2 * d] * hh_scr[:, 2 * d:3 * d]
        last = cbuf[pl.ds(tm, past), :]
        _link_past(cbuf, slice(None), width, tm)
        u = sum(cw_ref[k:k + 1, :] * tap for k, tap in enumerate(_conv_taps(cbuf, slice(None), width, tm)))
        cbuf[pl.ds(0, past), :] = last
        h1_ref[...] = h + _dot((b * u).astype(MXU_DT), wout_ref[...])

    return _launch(
        body, [x, first, g, w_in, cw, w_out], name="sc_fwd", grid=(nt,),
        in_specs=[pl.BlockSpec((tm, d), lambda i: (jnp.maximum(i - 1, 0), 0)), _const((tm, d)), _const((1, d)),
                  _const(w_in.shape), _const(cw.shape), _const(w_out.shape)],
        out_specs=[pl.BlockSpec((tm, d), lambda i: (i, 0)), pl.BlockSpec((tm, nq * n), lambda i: (i, 0))],
        out_shape=[jax.ShapeDtypeStruct((nt * tm, d), F32), jax.ShapeDtypeStruct((nt * tm, nq * n), ACT_DT)],
        scratch_shapes=[pltpu.VMEM((tm, nq * n), F32), pltpu.VMEM((past + tm, d), F32)],
        ride=ride,
    )


def _sc_bwd(dh, hh, x, first, g, w_in, cw, w_out, *, tm, ride=None):
    t_len, d = dh.shape
    nt = t_len // tm
    nq, _, n = w_in.shape
    width = cw.shape[0]
    past = _past_rows(width)
    halo_rows, halo_index = _halo_block(past, tm, nt)

    def body(dh_ref, hh_ref, hhp_ref, x_ref, first_ref, g_ref, win_ref, cw_ref, wout_ref,
             dx_ref, dhh_ref, z_ref, hn_ref, dcw_ref, dg_ref, dfirst_ref, cbuf, dbuf):
        i = pl.program_id(0)
        r = nt - 1 - i

        @pl.when(i == 0)
        def _():
            dbuf[pl.ds(tm, past), :] = jnp.zeros((past, d), F32)
            dcw_ref[...] = jnp.zeros_like(dcw_ref)
            dg_ref[...] = jnp.zeros_like(dg_ref)

        dh_out = dh_ref[...]
        b = hh_ref[:, 0:d].astype(F32)
        c = hh_ref[:, d:2 * d].astype(F32)
        v = hh_ref[:, 2 * d:3 * d].astype(F32)
        prev = hhp_ref[...].astype(F32)[halo_rows - past:, :]
        cbuf[pl.ds(0, past), :] = jnp.where(r > 0, prev[:, d:2 * d] * prev[:, 2 * d:3 * d], 0.0)
        cbuf[pl.ds(past, tm), :] = c * v
        _link_past(cbuf, slice(None), width, tm)
        taps = _conv_taps(cbuf, slice(None), width, tm)
        u = sum(cw_ref[k:k + 1, :] * taps[k] for k in range(width))
        z_ref[...] = (b * u).astype(z_ref.dtype)
        dz = _dot_nt(dh_out.astype(MXU_DT), wout_ref[...])
        dhh_ref[:, 0:d] = (dz * u).astype(dhh_ref.dtype)
        du = dz * b
        for k in range(width):
            dcw_ref[k:k + 1, :] += jnp.sum(taps[k] * du, axis=0, keepdims=True)
        dbuf[pl.ds(0, tm), :] = du
        _link_future(dbuf, slice(None), width, tm)
        dcv = _conv_back(dbuf, cw_ref, slice(None), width, tm)
        dbuf[pl.ds(tm, past), :] = dbuf[pl.ds(0, past), :]
        dhh_ref[:, d:2 * d] = (dcv * v).astype(dhh_ref.dtype)
        dhh_ref[:, 2 * d:3 * d] = (dcv * c).astype(dhh_ref.dtype)
        parts = [_dot_nt(dhh_ref[:, q * n:(q + 1) * n], win_ref[q]) for q in range(nq)]
        dhn = functools.reduce(lambda a, b: a + b, parts)
        h_in = jnp.where(r == 0, first_ref[...], x_ref[...])
        dh_in = _norm_bwd_tile(dhn, h_in, dh_out, g_ref[...], _valid_rows(r, tm), hn_ref, dg_ref)

        @pl.when(r == 0)
        def _():
            dfirst_ref[...] = dh_in

        @pl.when(r > 0)
        def _():
            dx_ref[...] = dh_in

    rev = lambda i: (nt - 1 - i, 0)
    rev_x = lambda i: (jnp.maximum(nt - 2 - i, 0), 0)
    return _launch(
        body, [dh, hh, hh, x, first, g, w_in, cw, w_out], name="sc_bwd", grid=(nt,),
        in_specs=[pl.BlockSpec((tm, d), rev), pl.BlockSpec((tm, 3 * d), rev), pl.BlockSpec((halo_rows, 3 * d), halo_index),
                  pl.BlockSpec((tm, d), rev_x), _const((tm, d)), _const((1, d)), _const(w_in.shape), _const(cw.shape),
                  _const(w_out.shape)],
        out_specs=[pl.BlockSpec((tm, d), rev_x), pl.BlockSpec((tm, 3 * d), rev), pl.BlockSpec((tm, d), rev),
                   pl.BlockSpec((tm, d), rev), _const((F32_ROWS, d)), _const((F32_ROWS, d)), _const((tm, d))],
        out_shape=[jax.ShapeDtypeStruct((t_len - tm, d), F32), jax.ShapeDtypeStruct((t_len, 3 * d), ACT_DT),
                   jax.ShapeDtypeStruct((t_len, d), ACT_DT), jax.ShapeDtypeStruct((t_len, d), ACT_DT),
                   jax.ShapeDtypeStruct((F32_ROWS, d), F32), jax.ShapeDtypeStruct((F32_ROWS, d), F32),
                   jax.ShapeDtypeStruct((tm, d), F32)],
        scratch_shapes=[pltpu.VMEM((past + tm, d), F32), pltpu.VMEM((tm + past, d), F32)],
        ride=ride,
    )


def _ffn_fwd(h, g, w_up, cw, w_down, *, tm, ride=None, loss=None):
    t_len, d = h.shape
    nt = t_len // tm
    nq, _, n = w_up.shape
    width = cw.shape[0]
    past = _past_rows(width)

    def body(h_ref, g_ref, wup_ref, cw_ref, wdn_ref, *rest):
        if loss is None:
            out_ref, hu_ref, ubuf, tail = rest
        else:
            t_ref, gf_ref, out_ref, hu_ref, sq_ref, dgf_ref, ubuf, tail = rest
        i = pl.program_id(0)

        @pl.when(i == 0)
        def _():
            tail[...] = jnp.zeros_like(tail)

        h_in = h_ref[...]
        hn = _rms(h_in, g_ref[...])[0].astype(MXU_DT)
        ubuf[pl.ds(0, past), :] = tail[...]
        for q in range(nq):
            ubuf[pl.ds(past, tm), q * n:(q + 1) * n] = _dot(hn, wup_ref[q])
        hu_ref[...] = ubuf[pl.ds(past, tm), :].astype(hu_ref.dtype)
        tail[...] = ubuf[pl.ds(tm, past), :]
        _link_past(ubuf, slice(None), width, tm)
        acc = h_in
        for j in range(nq // 2):
            gcol, vcol = slice(j * n, (j + 1) * n), slice((nq // 2 + j) * n, (nq // 2 + j + 1) * n)
            conv = lambda cols: sum(cw_ref[k:k + 1, cols] * tap for k, tap in enumerate(_conv_taps(ubuf, cols, width, tm)))
            gj, vj = conv(gcol), conv(vcol)
            acc = acc + _dot((gj * _sigmoid(gj) * vj).astype(MXU_DT), wdn_ref[j * n:(j + 1) * n, :])
        if loss is None:
            out_ref[...] = acc
            return

        @pl.when(i == 0)
        def _():
            sq_ref[...] = jnp.zeros_like(sq_ref)
            dgf_ref[...] = jnp.zeros_like(dgf_ref)
            out_ref[...] = jnp.zeros_like(out_ref)

        @pl.when(i > 0)
        def _():
            gain = gf_ref[...]
            out, xhat, rstd = _rms(acc, gain)
            err = out - t_ref[...]
            sq_ref[0:1, :] += jnp.sum(err * err, axis=0, keepdims=True)
            dout = err * (1.0 / d)
            dgf_ref[0:1, :] += jnp.sum(dout * xhat, axis=0, keepdims=True)
            out_ref[...] = _rms_bwd(dout, xhat, rstd, gain)

    row = lambda i: (i, 0)
    stat = jax.ShapeDtypeStruct((F32_ROWS, d), F32)
    return _launch(
        body, [h, g, w_up, cw, w_down] + list(loss or ()), name="ffn_fwd", grid=(nt,),
        in_specs=[pl.BlockSpec((tm, d), row), _const((1, d)), _const(w_up.shape), _const(cw.shape), _const(w_down.shape)]
        + ([pl.BlockSpec((tm, d), lambda i: (jnp.maximum(i - 1, 0), 0)), _const((1, d))] if loss else []),
        out_specs=[pl.BlockSpec((tm, d), row), pl.BlockSpec((tm, nq * n), row)] + ([_const(stat.shape)] * 2 if loss else []),
        out_shape=[jax.ShapeDtypeStruct((t_len, d), F32), jax.ShapeDtypeStruct((t_len, nq * n), ACT_DT)]
        + ([stat, stat] if loss else []),
        scratch_shapes=[pltpu.VMEM((past + tm, nq * n), F32), pltpu.VMEM((past, nq * n), F32)],
        ride=ride,
    )


def _norm_bwd_tile(dhn, h_in, dh, gain, valid, hn_ref, dg_ref):
    hn, xhat, rstd = _rms(h_in, gain)
    hn_ref[...] = hn.astype(hn_ref.dtype)
    dg_ref[0:1, :] += jnp.sum(dhn * xhat, axis=0, keepdims=True)
    return jnp.where(valid, dh + _rms_bwd(dhn, xhat, rstd, gain), 0.0)


def _ffn_bwd(dh, hu, h, g, w_up, cw, w_down, *, tm, ride=None):
    t_len, d = dh.shape
    nt = t_len // tm
    ff = hu.shape[1]
    n = ff // 4
    width = cw.shape[0]
    past = _past_rows(width)
    halo_rows, halo_index = _halo_block(past, tm, nt)

    def body(dh_ref, hu_ref, hup_ref, h_ref, g_ref, wup_ref, cw_ref, wdn_ref,
             dhin_ref, a_ref, dhu_ref, hn_ref, dcw_ref, dg_ref, ubuf, dbuf, head):
        i = pl.program_id(0)
        r = nt - 1 - i

        @pl.when(i == 0)
        def _():
            head[...] = jnp.zeros_like(head)
            dcw_ref[...] = jnp.zeros_like(dcw_ref)
            dg_ref[...] = jnp.zeros_like(dg_ref)

        dh_out = dh_ref[...]
        dhb = dh_out.astype(MXU_DT)
        dhn_parts = []
        d_act = [_dot_nt(dhb, wdn_ref[j * n:(j + 1) * n, :]) for j in range(2)]
        for j in range(2):
            mine = slice(0, n), slice(n, 2 * n)
            full = slice(j * n, (j + 1) * n), slice((2 + j) * n, (3 + j) * n)
            for here, there in zip(mine, full):
                prev = hup_ref[:, there].astype(F32)[halo_rows - past:, :]
                ubuf[pl.ds(0, past), here] = jnp.where(r > 0, prev, 0.0)
                ubuf[pl.ds(past, tm), here] = hu_ref[:, there].astype(F32)
                dbuf[pl.ds(tm, past), here] = head[:, there]
            _link_past(ubuf, slice(None), width, tm)
            conv = lambda here, there: sum(cw_ref[k:k + 1, there] * tap
                                           for k, tap in enumerate(_conv_taps(ubuf, here, width, tm)))
            gj, vj = conv(mine[0], full[0]), conv(mine[1], full[1])
            sg = _sigmoid(gj)
            s = gj * sg
            a_ref[:, full[0]] = (s * vj).astype(a_ref.dtype)
            da = d_act[j]
            dbuf[pl.ds(0, tm), mine[1]] = da * s
            dbuf[pl.ds(0, tm), mine[0]] = da * vj * (sg * (1.0 + gj * (1.0 - sg)))
            for here, there in zip(mine, full):
                head[:, there] = dbuf[pl.ds(0, past), here]
            _link_future(dbuf, slice(None), width, tm)
            for here, there in zip(mine, full):
                dy = dbuf[pl.ds(0, tm), here]
                for k, tap in enumerate(_conv_taps(ubuf, here, width, tm)):
                    dcw_ref[k:k + 1, there] += jnp.sum(tap * dy, axis=0, keepdims=True)
                dhu = sum(cw_ref[k:k + 1, there] * dbuf[pl.ds(F32_ROWS * (width - 1 - k), tm), here]
                          for k in range(width)).astype(dhu_ref.dtype)
                dhu_ref[:, there] = dhu
                dhn_parts.append(_dot_nt(dhu, wup_ref[there.start // n]))
        dhn = (dhn_parts[0] + dhn_parts[1]) + (dhn_parts[2] + dhn_parts[3])
        dhin_ref[...] = _norm_bwd_tile(dhn, h_ref[...], dh_out, g_ref[...], _valid_rows(r, tm), hn_ref, dg_ref)

    rev = lambda i: (nt - 1 - i, 0)
    return _launch(
        body, [dh, hu, hu, h, g, w_up, cw, w_down], name="ffn_bwd", grid=(nt,),
        in_specs=[pl.BlockSpec((tm, d), rev), pl.BlockSpec((tm, ff), rev), pl.BlockSpec((halo_rows, ff), halo_index),
                  pl.BlockSpec((tm, d), rev), _const((1, d)), _const(w_up.shape), _const(cw.shape), _const(w_down.shape)],
        out_specs=[pl.BlockSpec((tm, d), rev), pl.BlockSpec((tm, 2 * n), rev), pl.BlockSpec((tm, ff), rev),
                   pl.BlockSpec((tm, d), rev), _const((F32_ROWS, ff)), _const((F32_ROWS, d))],
        out_shape=[jax.ShapeDtypeStruct((t_len, d), F32), jax.ShapeDtypeStruct((t_len, 2 * n), ACT_DT),
                   jax.ShapeDtypeStruct((t_len, ff), ACT_DT), jax.ShapeDtypeStruct((t_len, d), ACT_DT),
                   jax.ShapeDtypeStruct((F32_ROWS, ff), F32), jax.ShapeDtypeStruct((F32_ROWS, d), F32)],
        scratch_shapes=[pltpu.VMEM((past + tm, 2 * n), F32), pltpu.VMEM((tm + past, 2 * n), F32), pltpu.VMEM((past, ff), F32)],
        ride=ride,
    )


V_CONV_B, V_B_A, V_B_X, V_LAMBDA = 0, 1, 2, 3
G_CONV_W, G_CONV_B, G_B_A, G_B_X, G_LAMBDA = 0, 4, 5, 6, 7


def _scan(a_ref, b_ref, edge, tm, reverse):
    nj = tm // F32_ROWS
    order = range(nj - 1, -1, -1) if reverse else range(nj)
    slab = lambda ref, j: ref[pl.ds(F32_ROWS * j, F32_ROWS), :]
    a_run = b_run = None
    for j in order:
        a_j, b_j = slab(a_ref, j), slab(b_ref, j)
        if a_run is not None:
            b_j = b_j + a_j * b_run
            a_j = a_j * a_run
            b_ref[pl.ds(F32_ROWS * j, F32_ROWS), :] = b_j
            a_ref[pl.ds(F32_ROWS * j, F32_ROWS), :] = a_j
        a_run, b_run = a_j, b_j
    sub = _sublane()
    shift = 1
    while shift < F32_ROWS:
        amount = F32_ROWS - shift if reverse else shift
        keep = (sub < F32_ROWS - shift) if reverse else (sub >= shift)
        b_run = jnp.where(keep, b_run + a_run * pltpu.roll(b_run, amount, 0), b_run)
        a_run = jnp.where(keep, a_run * pltpu.roll(a_run, amount, 0), a_run)
        shift *= 2
    outer = edge[0:1, :] if reverse else edge[F32_ROWS - 1:F32_ROWS, :]
    ends = b_run + a_run * outer
    if reverse:
        carry = jnp.where(sub == F32_ROWS - 1, outer, pltpu.roll(ends, F32_ROWS - 1, 0))
    else:
        carry = jnp.where(sub == 0, outer, pltpu.roll(ends, 1, 0))
    for j in range(nj):
        b_ref[pl.ds(F32_ROWS * j, F32_ROWS), :] = slab(b_ref, j) + slab(a_ref, j) * carry
    return slab(b_ref, 0 if reverse else nj - 1)


def _rg_gates(u, vec_ref, wa_ref, wx_ref, pre_scr, nb, bd):
    ub = u.astype(MXU_DT)
    for k in range(nb):
        blk = slice(k * bd, (k + 1) * bd)
        pre_scr[0, :, blk] = _dot(ub[:, blk], wa_ref[k])
        pre_scr[1, :, blk] = _dot(ub[:, blk], wx_ref[k])
    r_gate = _sigmoid(pre_scr[0] + vec_ref[V_B_A:V_B_A + 1, :])
    i_gate = _sigmoid(pre_scr[1] + vec_ref[V_B_X:V_B_X + 1, :])
    return r_gate, i_gate


def _rg_decay(r_gate, vec_ref):
    sp = _softplus(-vec_ref[V_LAMBDA:V_LAMBDA + 1, :])
    log_a = -RG_C * r_gate * sp
    one_minus_a2 = jnp.maximum(-_expm1_neg(2.0 * log_a), 1e-30)
    inv_mult = lax.rsqrt(one_minus_a2)
    return jnp.exp(log_a), one_minus_a2 * inv_mult, inv_mult, sp


def _rg_fwd(h, g, w_in, cw, vec, wa, wx, w_out, *, tm):
    t_len, d = h.shape
    nt = t_len // tm
    nq, _, n = w_in.shape
    dr = 2 * n
    width = cw.shape[0]
    past = _past_rows(width)
    nb, bd, _ = wa.shape

    def body(h_ref, g_ref, win_ref, cw_ref, vec_ref, wa_ref, wx_ref, wout_ref, out_ref, hh_ref, hs_ref, gates_ref,
             gbuf, rbuf, pre_scr, tail, edge):
        i = pl.program_id(0)

        @pl.when(i == 0)
        def _():
            tail[...] = jnp.zeros_like(tail)
            edge[...] = jnp.zeros_like(edge)

        h_in = h_ref[...]
        hn = _rms(h_in, g_ref[...])[0].astype(MXU_DT)
        rbuf[pl.ds(0, past), :] = tail[...]
        for q in range(2):
            gbuf[:, q * n:(q + 1) * n] = _dot(hn, win_ref[q])
            rbuf[pl.ds(past, tm), q * n:(q + 1) * n] = _dot(hn, win_ref[2 + q])
        hh_ref[:, 0:dr] = gbuf[...].astype(hh_ref.dtype)
        hh_ref[:, dr:2 * dr] = rbuf[pl.ds(past, tm), :].astype(hh_ref.dtype)
        tail[...] = rbuf[pl.ds(tm, past), :]
        _link_past(rbuf, slice(None), width, tm)
        taps = _conv_taps(rbuf, slice(None), width, tm)
        u = sum(cw_ref[k:k + 1, :] * taps[k] for k in range(width)) + vec_ref[V_CONV_B:V_CONV_B + 1, :]
        r_gate, i_gate = _rg_gates(u, vec_ref, wa_ref, wx_ref, pre_scr, nb, bd)
        for k, kept in enumerate((u, r_gate, i_gate)):
            gates_ref[:, k * dr:(k + 1) * dr] = kept.astype(gates_ref.dtype)
        a, mult, _, _ = _rg_decay(r_gate, vec_ref)
        pre_scr[0] = a
        pre_scr[1] = jnp.where(_valid_rows(i, tm), mult * (i_gate * u), 0.0)
        edge[...] = _scan(pre_scr.at[0], pre_scr.at[1], edge[...], tm, reverse=False)
        hs = pre_scr[1]
        hs_ref[...] = hs
        y = hs * _gelu(gbuf[...])[0]
        out_ref[...] = h_in + _dot(y.astype(MXU_DT), wout_ref[...])

    row = lambda i: (i, 0)
    return pl.pallas_call(
        body, name="rg_fwd", grid=(nt,),
        in_specs=[pl.BlockSpec((tm, d), row), _const((1, d)), _const(w_in.shape), _const(cw.shape), _const(vec.shape),
                  _const(wa.shape), _const(wx.shape), _const(w_out.shape)],
        out_specs=[pl.BlockSpec((tm, d), row), pl.BlockSpec((tm, 2 * dr), row), pl.BlockSpec((tm, dr), row),
                   pl.BlockSpec((tm, 3 * dr), row)],
        out_shape=[jax.ShapeDtypeStruct((t_len, d), F32), jax.ShapeDtypeStruct((t_len, 2 * dr), ACT_DT),
                   jax.ShapeDtypeStruct((t_len, dr), F32), jax.ShapeDtypeStruct((t_len, 3 * dr), ACT_DT)],
        scratch_shapes=[pltpu.VMEM((tm, dr), F32), pltpu.VMEM((past + tm, dr), F32), pltpu.VMEM((2, tm, dr), F32),
                        pltpu.VMEM((past, dr), F32), pltpu.VMEM((F32_ROWS, dr), F32)],
        compiler_params=_params(),
    )(h, g, w_in, cw, vec, wa, wx, w_out)


def _rg_bwd(dh, hh, hs, gates, h, g, w_in, cw, vec, wa, wx, w_out, *, tm, ride=None):
    t_len, d = dh.shape
    nt = t_len // tm
    dr = hs.shape[1]
    n = w_in.shape[2]
    width = cw.shape[0]
    nb, bd, _ = wa.shape
    past = _past_rows(width)
    halo_rows, halo_index = _halo_block(past, tm, nt)
    one = F32_ROWS

    def body(dh_ref, hh_ref, hhp_ref, hs_ref, hsp_ref, gates_ref, h_ref, g_ref, win_ref, cw_ref, vec_ref, wa_ref, wx_ref, wout_ref,
             dhin_ref, dhh_ref, y_ref, hn_ref, dvec_ref, dwa_ref, dwx_ref, dg_ref, rbuf, dbuf, pre_scr, hbuf, abuf, edge):
        i = pl.program_id(0)
        r = nt - 1 - i

        @pl.when(i == 0)
        def _():
            dbuf[pl.ds(tm, past), :] = jnp.zeros((past, dr), F32)
            abuf[pl.ds(tm, one), :] = jnp.zeros((one, dr), F32)
            edge[...] = jnp.zeros_like(edge)
            dvec_ref[...] = jnp.zeros_like(dvec_ref)
            dwa_ref[...] = jnp.zeros_like(dwa_ref)
            dwx_ref[...] = jnp.zeros_like(dwx_ref)
            dg_ref[...] = jnp.zeros_like(dg_ref)

        dh_out = dh_ref[...]
        gb = hh_ref[:, 0:dr].astype(F32)
        prev = hhp_ref[...].astype(F32)[halo_rows - past:, dr:2 * dr]
        rbuf[pl.ds(0, past), :] = jnp.where(r > 0, prev, 0.0)
        rbuf[pl.ds(past, tm), :] = hh_ref[:, dr:2 * dr].astype(F32)
        _link_past(rbuf, slice(None), width, tm)
        taps = _conv_taps(rbuf, slice(None), width, tm)
        ub = gates_ref[:, 0:dr].astype(MXU_DT)
        u, r_gate, i_gate = (gates_ref[:, k * dr:(k + 1) * dr].astype(F32) for k in range(3))
        a, mult, inv_mult, sp = _rg_decay(r_gate, vec_ref)
        hs_t = hs_ref[...]
        hbuf[pl.ds(0, one), :] = jnp.where(r > 0, hsp_ref[...], 0.0)
        hbuf[pl.ds(one, tm), :] = hs_t
        _link_past(hbuf, slice(None), 2, tm)
        h_prev = hbuf[pl.ds(0, tm), :]
        gate, th = _gelu(gb)
        y_ref[...] = (hs_t * gate).astype(y_ref.dtype)
        dy = _dot_nt(dh_out.astype(MXU_DT), wout_ref[...])
        d_gb = (dy * hs_t * _gelu_grad(gb, th)).astype(dhh_ref.dtype)
        dhh_ref[:, 0:dr] = d_gb
        dhn = sum(_dot_nt(d_gb[:, q * n:(q + 1) * n], win_ref[q]) for q in range(2))
        abuf[pl.ds(0, tm), :] = a
        _link_future(abuf, slice(None), 2, tm)
        pre_scr[0] = abuf[pl.ds(one, tm), :]
        pre_scr[1] = dy * gate
        edge[...] = _scan(pre_scr.at[0], pre_scr.at[1], edge[...], tm, reverse=True)
        abuf[pl.ds(tm, one), :] = abuf[pl.ds(0, one), :]
        d_hs = pre_scr[1]
        d_b = jnp.where(_valid_rows(r, tm), d_hs, 0.0)
        d_iu = d_b * mult
        d_log_a = d_hs * h_prev * a - d_b * (i_gate * u) * (a * a) * inv_mult
        dvec_ref[G_LAMBDA:G_LAMBDA + 1, :] += jnp.sum(d_log_a * r_gate, axis=0, keepdims=True) * (-RG_C)
        d_pre_r = d_log_a * (-RG_C * sp) * r_gate * (1.0 - r_gate)
        d_pre_i = d_iu * u * i_gate * (1.0 - i_gate)
        dvec_ref[G_B_A:G_B_A + 1, :] += jnp.sum(d_pre_r, axis=0, keepdims=True)
        dvec_ref[G_B_X:G_B_X + 1, :] += jnp.sum(d_pre_i, axis=0, keepdims=True)
        dbuf[pl.ds(0, tm), :] = d_iu * i_gate
        d_pre_r = d_pre_r.astype(MXU_DT)
        d_pre_i = d_pre_i.astype(MXU_DT)
        for k in range(nb):
            blk = slice(k * bd, (k + 1) * bd)
            dwa_ref[k] += _dot_tn(ub[:, blk], d_pre_r[:, blk])
            dwx_ref[k] += _dot_tn(ub[:, blk], d_pre_i[:, blk])
            dbuf[pl.ds(0, tm), blk] += _dot_nt(d_pre_r[:, blk], wa_ref[k]) + _dot_nt(d_pre_i[:, blk], wx_ref[k])
        du = dbuf[pl.ds(0, tm), :]
        dvec_ref[G_CONV_B:G_CONV_B + 1, :] += jnp.sum(du, axis=0, keepdims=True)
        for k in range(width):
            dvec_ref[G_CONV_W + k:G_CONV_W + k + 1, :] += jnp.sum(taps[k] * du, axis=0, keepdims=True)
        _link_future(dbuf, slice(None), width, tm)
        d_rb = _conv_back(dbuf, cw_ref, slice(None), width, tm)
        dbuf[pl.ds(tm, past), :] = dbuf[pl.ds(0, past), :]
        d_rb = d_rb.astype(dhh_ref.dtype)
        dhh_ref[:, dr:2 * dr] = d_rb
        dhn = dhn + sum(_dot_nt(d_rb[:, q * n:(q + 1) * n], win_ref[2 + q]) for q in range(2))
        dhin_ref[...] = _norm_bwd_tile(dhn, h_ref[...], dh_out, g_ref[...], _valid_rows(r, tm), hn_ref, dg_ref)

        @pl.when(i == nt - 1)
        def _():
            lam = vec_ref[V_LAMBDA:V_LAMBDA + 1, :]
            dvec_ref[G_LAMBDA:G_LAMBDA + 1, :] = dvec_ref[G_LAMBDA:G_LAMBDA + 1, :] * (-_sigmoid(-lam))

    rev = lambda i: (nt - 1 - i, 0)
    return _launch(
        body, [dh, hh, hh, hs, hs, gates, h, g, w_in, cw, vec, wa, wx, w_out], name="rg_bwd", grid=(nt,),
        in_specs=[pl.BlockSpec((tm, d), rev), pl.BlockSpec((tm, 2 * dr), rev), pl.BlockSpec((halo_rows, 2 * dr), halo_index),
                  pl.BlockSpec((tm, dr), rev),
                  pl.BlockSpec((one, dr), lambda i: (jnp.maximum((nt - 1 - i) * (tm // one) - 1, 0), 0)),
                  pl.BlockSpec((tm, 3 * dr), rev), pl.BlockSpec((tm, d), rev), _const((1, d)), _const(w_in.shape),
                  _const(cw.shape), _const(vec.shape), _const(wa.shape), _const(wx.shape), _const(w_out.shape)],
        out_specs=[pl.BlockSpec((tm, d), rev), pl.BlockSpec((tm, 2 * dr), rev), pl.BlockSpec((tm, dr), rev),
                   pl.BlockSpec((tm, d), rev), _const((F32_ROWS, dr)), _const(wa.shape), _const(wx.shape),
                   _const((F32_ROWS, d))],
        out_shape=[jax.ShapeDtypeStruct((t_len, d), F32), jax.ShapeDtypeStruct((t_len, 2 * dr), ACT_DT),
                   jax.ShapeDtypeStruct((t_len, dr), ACT_DT), jax.ShapeDtypeStruct((t_len, d), ACT_DT),
                   jax.ShapeDtypeStruct((F32_ROWS, dr), F32), jax.ShapeDtypeStruct(wa.shape, F32),
                   jax.ShapeDtypeStruct(wx.shape, F32), jax.ShapeDtypeStruct((F32_ROWS, d), F32)],
        scratch_shapes=[pltpu.VMEM((past + tm, dr), F32), pltpu.VMEM((tm + past, dr), F32), pltpu.VMEM((2, tm, dr), F32),
                        pltpu.VMEM((one + tm, dr), F32), pltpu.VMEM((tm + one, dr), F32), pltpu.VMEM((F32_ROWS, dr), F32)],
        ride=ride,
    )


def _weight_grad(a, b, nb, *, rows, ride=None):
    t_len, k_dim = a.shape
    n = b.shape[1] // nb
    nt = t_len // rows

    def body(a_ref, b_ref, out_ref, wire_ref):
        @pl.when(pl.program_id(1) == 0)
        def _():
            out_ref[...] = jnp.zeros_like(out_ref)

        out_ref[0] += _dot_tn(a_ref[...].astype(MXU_DT), b_ref[...].astype(MXU_DT))

        @pl.when(pl.program_id(1) == nt - 1)
        def _():
            wire_ref[...] = out_ref[...].astype(wire_ref.dtype)

    block = pl.BlockSpec((1, k_dim, n), lambda j, i: (j, 0, 0))
    return _launch(
        body, [a, b], name="weight_grad", grid=(nb, nt),
        in_specs=[pl.BlockSpec((rows, k_dim), lambda j, i: (i, 0)), pl.BlockSpec((rows, n), lambda j, i: (i, j))],
        out_specs=[block, block],
        out_shape=[jax.ShapeDtypeStruct((nb, k_dim, n), F32), jax.ShapeDtypeStruct((nb, k_dim, n), WIRE_DT)],
        ride=ride,
    )


def _adamw(w, m, v, parts, *, rows, layer=0, into=None):
    n_layers, n_rows, n_cols = w.shape
    nt = n_rows // rows
    n_parts = len(parts)

    def body(w_ref, m_ref, v_ref, *rest):
        part_refs, (g_ref, d_ref, nm_ref, nv_ref) = rest[:n_parts], rest[-4:]
        w_ref, m_ref, v_ref, g_ref, d_ref, nm_ref, nv_ref = (r.at[0] for r in (w_ref, m_ref, v_ref, g_ref, d_ref, nm_ref, nv_ref))
        grad = part_refs[0][...].astype(F32)
        for p in part_refs[1:]:
            grad = grad + p[...].astype(F32)
        new_m = ADAM_B1 * m_ref[...] + (1.0 - ADAM_B1) * grad
        new_v = ADAM_B2 * v_ref[...] + (1.0 - ADAM_B2) * (grad * grad)
        m_hat = new_m / (1.0 - ADAM_B1 ** ADAM_STEP)
        v_hat = new_v / (1.0 - ADAM_B2 ** ADAM_STEP)
        g_ref[...] = grad
        d_ref[...] = -ADAM_LR * (m_hat / (jnp.sqrt(v_hat) + ADAM_EPS) + ADAM_WD * w_ref[...])
        nm_ref[...] = new_m
        nv_ref[...] = new_v

    spec = pl.BlockSpec((rows, n_cols), lambda i: (i, 0))
    layer_spec = pl.BlockSpec((1, rows, n_cols), lambda i: (layer, i, 0))
    into = list(into or [])
    return pl.pallas_call(
        body, name="adamw", grid=(nt,),
        in_specs=[layer_spec] * 3 + [spec] * n_parts + [ANY] * len(into), out_specs=[layer_spec] * 4,
        out_shape=[jax.ShapeDtypeStruct(w.shape, F32)] * 4,
        input_output_aliases={3 + n_parts + k: k for k in range(len(into))},
        compiler_params=_params(),
    )(w, m, v, *parts, *into)


def _sum_stack(stack, *, rows):
    n_stack, n_rows, n_cols = stack.shape

    def body(stack_ref, out_ref):
        acc = stack_ref[0]
        for j in range(1, n_stack):
            acc = acc + stack_ref[j]
        out_ref[...] = acc

    return pl.pallas_call(
        body, name="sum_stack", grid=(n_rows // rows,),
        in_specs=[pl.BlockSpec((n_stack, rows, n_cols), lambda i: (0, i, 0))],
        out_specs=pl.BlockSpec((rows, n_cols), lambda i: (i, 0)),
        out_shape=jax.ShapeDtypeStruct((n_rows, n_cols), F32),
        compiler_params=_params(),
    )(stack)


def _sum_parts(own, recv, *, rows):
    n_rows, n_cols = own.shape
    n_recv = recv.shape[0]

    def body(own_ref, recv_ref, out_ref):
        acc = own_ref[...].astype(F32)
        for j in range(n_recv):
            acc = acc + recv_ref[j].astype(F32)
        out_ref[...] = acc

    return pl.pallas_call(
        body, name="sum_parts", grid=(n_rows // rows,),
        in_specs=[pl.BlockSpec((rows, n_cols), lambda i: (i, 0)), pl.BlockSpec((n_recv, rows, n_cols), lambda i: (0, i, 0))],
        out_specs=pl.BlockSpec((rows, n_cols), lambda i: (i, 0)),
        out_shape=jax.ShapeDtypeStruct(own.shape, F32),
        compiler_params=_params(),
    )(own, recv)


class _Swap:
    def __init__(self, arrays):
        nk = len(arrays)
        self.arrays = list(arrays)
        self.out_shape = [jax.ShapeDtypeStruct(a.shape, a.dtype) for a in arrays]
        self.scratch = [pltpu.SemaphoreType.DMA((nk,)), pltpu.SemaphoreType.DMA((nk,))]

    def run(self, ins, outs, sems, start):
        send_sems, recv_sems = sems
        x, y, c = _place()
        for k in range(len(ins)):
            send = pltpu.make_async_remote_copy(src_ref=ins[k], dst_ref=outs[k], send_sem=send_sems.at[k],
                                                recv_sem=recv_sems.at[k], device_id=(x, y, 1 - c), device_id_type=MESH_ID)
            if start:
                send.start()
            else:
                send.wait_recv()
                send.wait_send()


class _AllDevices:
    def __init__(self, arrays):
        nk = len(arrays)
        self.arrays = list(arrays)
        self.out_shape = [jax.ShapeDtypeStruct((8,) + a.shape, a.dtype) for a in arrays]
        self.scratch = [pltpu.SemaphoreType.DMA((nk, 7)), pltpu.SemaphoreType.DMA((nk, 7)), pltpu.SemaphoreType.DMA((nk,))]

    def run(self, ins, outs, sems, start):
        send_sems, recv_sems, local_sems = sems
        x, y, c = _place()
        mine = 4 * x + 2 * y + c
        for k in range(len(ins)):
            local = pltpu.make_async_copy(ins[k], outs[k].at[mine], local_sems.at[k])
            local.start() if start else local.wait()
            for flip in range(1, 8):
                px, py, pc = x ^ (flip >> 2), y ^ ((flip >> 1) & 1), c ^ (flip & 1)
                sems_f = dict(send_sem=send_sems.at[k, flip - 1], recv_sem=recv_sems.at[k, flip - 1],
                              device_id=(px, py, pc), device_id_type=MESH_ID)
                send = pltpu.make_async_remote_copy(src_ref=ins[k], dst_ref=outs[k].at[mine], **sems_f)
                if start:
                    send.start()
                else:
                    pltpu.make_async_remote_copy(src_ref=ins[k], dst_ref=outs[k].at[4 * px + 2 * py + pc], **sems_f).wait_recv()
                    send.wait_send()


class _Both:
    def __init__(self, first, second):
        self.rides = (first, second)
        self.arrays = first.arrays + second.arrays
        self.out_shape = first.out_shape + second.out_shape
        self.scratch = first.scratch + second.scratch

    def run(self, ins, outs, sems, start):
        for ride in self.rides:
            n_in, n_out, n_sem = len(ride.arrays), len(ride.out_shape), len(ride.scratch)
            ride.run(ins[:n_in], outs[:n_out], sems[:n_sem], start)
            ins, outs, sems = ins[n_in:], outs[n_out:], sems[n_sem:]


def _pack(arrays, pad_rows=F32_ROWS):
    flat = jnp.concatenate([a.reshape(-1).astype(F32) for a in arrays])
    rows = -(-flat.shape[0] // (LANES * pad_rows)) * pad_rows
    return jnp.pad(flat, (0, rows * LANES - flat.shape[0])).reshape(rows, LANES)


def _unpack(packed, shapes):
    flat, out, off = packed.reshape(-1), [], 0
    for s in shapes:
        size = 1
        for dim in s:
            size *= dim
        out.append(flat[off:off + size].reshape(s))
        off += size
    return out


def _divisor_rows(n_rows, most=256):
    best = None
    for r in range(ACT_ROWS, most + 1, ACT_ROWS):
        if n_rows % r == 0:
            best = r
    return best or n_rows


def kernel(x, meta_tokens, norm_mix_g, norm_ffn_g, final_norm_g, sc_w_in, sc_conv_w, sc_w_out, rg_w_in, rg_conv_w, rg_conv_b, rg_w_gate_a, rg_b_gate_a, rg_w_gate_x, rg_b_gate_x, rg_lambda, rg_w_out, ffn_w_up, ffn_conv_w, ffn_w_down, loss_target, m_meta_tokens, m_norm_mix_g, m_norm_ffn_g, m_final_norm_g, m_sc_w_in, m_sc_conv_w, m_sc_w_out, m_rg_w_in, m_rg_conv_w, m_rg_conv_b, m_rg_w_gate_a, m_rg_b_gate_a, m_rg_w_gate_x, m_rg_b_gate_x, m_rg_lambda, m_rg_w_out, m_ffn_w_up, m_ffn_conv_w, m_ffn_w_down, v_meta_tokens, v_norm_mix_g, v_norm_ffn_g, v_final_norm_g, v_sc_w_in, v_sc_conv_w, v_sc_w_out, v_rg_w_in, v_rg_conv_w, v_rg_conv_b, v_rg_w_gate_a, v_rg_b_gate_a, v_rg_w_gate_x, v_rg_b_gate_x, v_rg_lambda, v_rg_w_out, v_ffn_w_up, v_ffn_conv_w, v_ffn_w_down):
    weights = dict(meta_tokens=meta_tokens, norm_mix_g=norm_mix_g, norm_ffn_g=norm_ffn_g, final_norm_g=final_norm_g, sc_w_in=sc_w_in, sc_conv_w=sc_conv_w, sc_w_out=sc_w_out, rg_w_in=rg_w_in, rg_conv_w=rg_conv_w, rg_conv_b=rg_conv_b, rg_w_gate_a=rg_w_gate_a, rg_b_gate_a=rg_b_gate_a, rg_w_gate_x=rg_w_gate_x, rg_b_gate_x=rg_b_gate_x, rg_lambda=rg_lambda, rg_w_out=rg_w_out, ffn_w_up=ffn_w_up, ffn_conv_w=ffn_conv_w, ffn_w_down=ffn_w_down)
    m_in = dict(meta_tokens=m_meta_tokens, norm_mix_g=m_norm_mix_g, norm_ffn_g=m_norm_ffn_g, final_norm_g=m_final_norm_g, sc_w_in=m_sc_w_in, sc_conv_w=m_sc_conv_w, sc_w_out=m_sc_w_out, rg_w_in=m_rg_w_in, rg_conv_w=m_rg_conv_w, rg_conv_b=m_rg_conv_b, rg_w_gate_a=m_rg_w_gate_a, rg_b_gate_a=m_rg_b_gate_a, rg_w_gate_x=m_rg_w_gate_x, rg_b_gate_x=m_rg_b_gate_x, rg_lambda=m_rg_lambda, rg_w_out=m_rg_w_out, ffn_w_up=m_ffn_w_up, ffn_conv_w=m_ffn_conv_w, ffn_w_down=m_ffn_w_down)
    v_in = dict(meta_tokens=v_meta_tokens, norm_mix_g=v_norm_mix_g, norm_ffn_g=v_norm_ffn_g, final_norm_g=v_final_norm_g, sc_w_in=v_sc_w_in, sc_conv_w=v_sc_conv_w, sc_w_out=v_sc_w_out, rg_w_in=v_rg_w_in, rg_conv_w=v_rg_conv_w, rg_conv_b=v_rg_conv_b, rg_w_gate_a=v_rg_w_gate_a, rg_b_gate_a=v_rg_b_gate_a, rg_w_gate_x=v_rg_w_gate_x, rg_b_gate_x=v_rg_b_gate_x, rg_lambda=v_rg_lambda, rg_w_out=v_rg_w_out, ffn_w_up=v_ffn_w_up, ffn_conv_w=v_ffn_conv_w, ffn_w_down=v_ffn_w_down)
    names = list(weights)

    seq, d = x.shape[1:]
    tm = _row_tile(seq)
    tokens, target = _tile_order(x[0], tm), _tile_order(loss_target[0], tm)
    t_len = seq + tm
    wg_rows = 5 * tm if t_len % (5 * tm) == 0 else tm
    wg_rows_in = 13 * tm if t_len % (13 * tm) == 0 else wg_rows
    xi, yi, _ = _place()
    chip = 2 * xi + yi
    mesh_axes = ("x", "y", "c")

    wire = lambda w: w.astype(WIRE_DT)
    small_sharded = ["meta_tokens", "sc_conv_w", "rg_conv_w", "rg_conv_b", "rg_b_gate_a", "rg_b_gate_x", "rg_lambda", "ffn_conv_w"]
    small_2d = {n: weights[n].reshape(-1, weights[n].shape[-1]) for n in small_sharded}
    w_sc_in, w_sc_out, small_by_chip = _exchange(
        _GatherHalves([wire(sc_w_in[0]), wire(sc_w_out[0]), _pack([small_2d[n] for n in small_sharded], 2 * ACT_ROWS)]),
        "gather_first")
    w_sc_out = w_sc_out.reshape(-1, d)
    gather_ffn0 = _Gather([wire(ffn_w_up[0]), wire(ffn_w_down[0])])
    gather_rest = _Gather([wire(rg_w_in[0]), wire(rg_w_out[0]), wire(ffn_w_up[1]), wire(ffn_w_down[1])])
    small_len = sum(a.size for a in small_2d.values())
    by_chip = small_by_chip.reshape(N_CHIPS, -1)[:, :small_len]
    full, off = {}, 0
    for n in small_sharded:
        rows, width = small_2d[n].shape
        full[n] = by_chip[:, off:off + rows * width].reshape(N_CHIPS, rows, width).transpose(1, 0, 2).reshape(rows, N_CHIPS * width)
        off += rows * width
    sc_cw, rg_cw = full["sc_conv_w"], full["rg_conv_w"]
    ffn_cw = [full["ffn_conv_w"][0:3], full["ffn_conv_w"][3:6]]
    d_rnn = rg_cw.shape[1]
    vec = jnp.concatenate([full["rg_conv_b"], full["rg_b_gate_a"], full["rg_b_gate_x"], full["rg_lambda"],
                           jnp.zeros((F32_ROWS - 4, d_rnn), F32)])
    wa, wx = rg_w_gate_a[0].astype(MXU_DT), rg_w_gate_x[0].astype(MXU_DT)
    first = _tile_order(jnp.concatenate([jnp.zeros((tm - N_META, d), F32), full["meta_tokens"]]), tm)
    g_mix = [norm_mix_g[0:1], norm_mix_g[1:2]]
    g_ffn = [norm_ffn_g[0:1], norm_ffn_g[1:2]]

    h1, hh0, w_up0, w_dn0 = _sc_fwd(tokens, first, g_mix[0], w_sc_in, sc_cw, w_sc_out, tm=tm, ride=gather_ffn0)
    h2, hu0, w_rg_in, w_rg_out, w_up1, w_dn1 = _ffn_fwd(h1, g_ffn[0], w_up0, ffn_cw[0], w_dn0.reshape(-1, d), tm=tm,
                                                         ride=gather_rest)
    w_up, w_dn, w_rg_out = [w_up0, w_up1], [w_dn0.reshape(-1, d), w_dn1.reshape(-1, d)], w_rg_out.reshape(-1, d)
    h3, hh1, hs, gates = _rg_fwd(h2, g_mix[1], w_rg_in, rg_cw, vec, wa, wx, w_rg_out, tm=tm)
    dh4, hu1, sq, d_final = _ffn_fwd(h3, g_ffn[1], w_up[1], ffn_cw[1], w_dn[1], tm=tm,
                                     loss=(target, final_norm_g.reshape(1, d)))
    loss = lax.psum(jnp.sum(sq[0]) * (0.5 / d), mesh_axes)

    def by_chip_rows(pair):
        return [p.reshape(N_CHIPS, -1, d) for p in pair]

    def ffn_backward(dh_out, h_in, hu, layer, ride):
        dh_in, act, dhu, hn, dcw, dg, *landed = _ffn_bwd(dh_out, hu, h_in, g_ffn[layer], w_up[layer], ffn_cw[layer],
                                                         w_dn[layer], tm=tm, ride=ride)
        d_up = _weight_grad(hn, dhu, N_CHIPS, rows=wg_rows_in)
        d_dn = by_chip_rows(_weight_grad(act, dh_out, 1, rows=wg_rows))
        return dh_in, d_up, d_dn, dcw[0:3], dg[0], landed

    dh3, d_up1, d_dn1, d_fcw1, d_gf1, _ = ffn_backward(dh4, h3, hu1, 1, None)
    dh2, dhh1, y_rg, hn_rg, d_vec, d_wa, d_wx, d_gm1, *landed_ffn1 = _rg_bwd(
        dh3, hh1, hs, gates, h2, g_mix[1], w_rg_in, rg_cw, vec, wa, wx, w_rg_out, tm=tm,
        ride=_Scatter([d_up1[1], d_dn1[1]]))
    d_rg_in = _weight_grad(hn_rg, dhh1, N_CHIPS, rows=wg_rows_in)
    d_rg_out = by_chip_rows(_weight_grad(y_rg, dh3, 1, rows=wg_rows))
    early = {"rg_conv_w": d_vec[G_CONV_W:G_CONV_W + 4], "rg_conv_b": d_vec[G_CONV_B:G_CONV_B + 1],
             "rg_b_gate_a": d_vec[G_B_A:G_B_A + 1], "rg_b_gate_x": d_vec[G_B_X:G_B_X + 1],
             "rg_lambda": d_vec[G_LAMBDA:G_LAMBDA + 1], "ffn_conv_w.1": d_fcw1, "norm_mix_g.1": d_gm1[0:1],
             "norm_ffn_g.1": d_gf1[None], "final_norm_g": d_final[0]}
    early_packed = _pack(list(early.values()))
    gate_names = ["rg_w_gate_a", "rg_w_gate_x"]
    to_all = _AllDevices([early_packed, d_wa.reshape(-1, LANES), d_wx.reshape(-1, LANES)])
    dh1, d_up0, d_dn0, d_fcw0, d_gf0, landed = ffn_backward(
        dh2, h1, hu0, 0, _Both(_Scatter([d_rg_in[1], d_rg_out[1]]), to_all))
    landed_rg, early_by_device, gates_by_device = landed[0:2], landed[2], landed[3:]

    def core_sum(pair, received):
        own = lax.dynamic_index_in_dim(pair[0], chip, 0, keepdims=False)
        return _sum_parts(own, received, rows=_divisor_rows(own.shape[0]))

    early_big = [("rg_w_in", 0), ("rg_w_out", 0), ("ffn_w_up", 1), ("ffn_w_down", 1)]
    early_sum = [core_sum(d_rg_in, landed_rg[0]), core_sum(d_rg_out, landed_rg[1]), core_sum(d_up1, landed_ffn1[0]),
                 core_sum(d_dn1, landed_ffn1[1])]
    grad_x, dhh0, z_sc, hn_sc, d_sccw, d_gm0, d_first, *landed = _sc_bwd(
        dh1, hh0, tokens, first, g_mix[0], w_sc_in, sc_cw, w_sc_out, tm=tm,
        ride=_Both(_Scatter([d_up0[1], d_dn0[1]]), _Swap(early_sum)))
    landed_ffn0, early_other = landed[0:2], landed[2:]
    d_sc_in = _weight_grad(hn_sc, dhh0, N_CHIPS, rows=wg_rows_in)
    *d_sc_out, landed_sc_in = _weight_grad(z_sc, dh1, 1, rows=wg_rows, ride=_Scatter([d_sc_in[1]]))
    d_sc_out = by_chip_rows(d_sc_out)
    landed_sc = [landed_sc_in, *_exchange(_Scatter([d_sc_out[1]]), "scatter_last")]
    grad_x = _time_order(grad_x, tm)[None]

    late_big = [("sc_w_in", 0), ("sc_w_out", 0), ("ffn_w_up", 0), ("ffn_w_down", 0)]
    late_sum = [core_sum(d_sc_in, landed_sc[0]), core_sum(d_sc_out, landed_sc[1]), core_sum(d_up0, landed_ffn0[0]),
                core_sum(d_dn0, landed_ffn0[1])]
    late_other = _exchange(_Swap(late_sum), "swap_cores")
    out = {k: {} for k in ("grad", "delta", "m", "v")}
    stacked = {}
    for (n, layer), mine, theirs in zip(late_big + early_big, late_sum + early_sum, list(late_other) + list(early_other)):
        stacked[n] = _adamw(weights[n], m_in[n], v_in[n], [mine, theirs], rows=_divisor_rows(mine.shape[0]), layer=layer,
                            into=stacked.get(n))
    for n, res in stacked.items():
        for k, key in enumerate(("grad", "delta", "m", "v")):
            out[key][n] = res[k]

    late = {"meta_tokens": _time_order(d_first, tm)[tm - N_META:], "sc_conv_w": d_sccw[0:3], "ffn_conv_w.0": d_fcw0,
            "norm_mix_g.0": d_gm0[0:1], "norm_ffn_g.0": d_gf0[None]}
    late_packed = _pack(list(late.values()))
    late_by_device, = _exchange(_AllDevices([late_packed]), "gather_devices")
    summed = {}
    for parts, packed, by_device in ((early, early_packed, early_by_device), (late, late_packed, late_by_device)):
        total = _sum_stack(by_device, rows=packed.shape[0])
        summed.update(zip(parts, _unpack(total, [p.shape for p in parts.values()])))
    for n in ("ffn_conv_w", "norm_mix_g", "norm_ffn_g"):
        summed[n] = jnp.concatenate([summed.pop(n + ".0"), summed.pop(n + ".1")])
    for n, by_device in zip(gate_names, gates_by_device):
        as_rows = lambda a: a.reshape(1, -1, LANES)
        res = _adamw(as_rows(weights[n]), as_rows(m_in[n]), as_rows(v_in[n]), [_sum_stack(by_device, rows=256)], rows=256)
        for k, key in enumerate(("grad", "delta", "m", "v")):
            out[key][n] = res[k].reshape(weights[n].shape)
    replicated = ["norm_mix_g", "norm_ffn_g", "final_norm_g"]
    small_names = small_sharded + replicated
    grads = {}
    for n in small_sharded:
        width = small_2d[n].shape[1]
        grads[n] = lax.dynamic_slice_in_dim(summed[n], chip * width, width, axis=1).reshape(weights[n].shape)
    for n in replicated:
        grads[n] = summed[n].reshape(weights[n].shape)
    shapes = [weights[n].shape for n in small_names]
    packed_w = _pack([weights[n] for n in small_names])
    res = _adamw(packed_w[None], _pack([m_in[n] for n in small_names])[None], _pack([v_in[n] for n in small_names])[None],
                 [_pack([grads[n] for n in small_names])], rows=packed_w.shape[0])
    for k, key in enumerate(("grad", "delta", "m", "v")):
        out[key].update(dict(zip(small_names, _unpack(res[k][0], shapes))))

    return (loss, grad_x, *[out["grad"][n] for n in names], *[out["delta"][n] for n in names],
            *[out["m"][n] for n in names], *[out["v"][n] for n in names])
```

```python
import functools

import jax
import jax.numpy as jnp
from jax import lax
from jax.experimental import pallas as pl
from jax.experimental.pallas import tpu as pltpu

F32 = jnp.float32
MXU_DT = jnp.bfloat16
ACT_DT = jnp.bfloat16
WIRE_DT = jnp.bfloat16
MESH_ID = pl.DeviceIdType.MESH

N_META = 16
RMS_EPS = 1e-6
RG_C = 8.0
ADAM_LR, ADAM_B1, ADAM_B2, ADAM_EPS, ADAM_WD, ADAM_STEP = 0.001, 0.9, 0.999, 1e-08, 0.01, 10
N_CHIPS = 4
VMEM_LIMIT = 60 * 1024 * 1024
F32_ROWS = 8
ACT_ROWS = 16
LANES = 128


def _row_tile(seq):
    for tm in (256, 128, 64, 32, 16):
        if seq % tm == 0:
            return tm
    raise ValueError(f"sequence length {seq} is not a multiple of 16")


def _params(n_axes=1, **kw):
    return pltpu.CompilerParams(dimension_semantics=("arbitrary",) * n_axes, vmem_limit_bytes=VMEM_LIMIT, **kw)


def _const(shape):
    return pl.BlockSpec(shape, lambda *_: (0,) * len(shape), pipeline_mode=pl.Buffered(1))


def _dot(a, b):
    return jnp.dot(a, b, preferred_element_type=F32)


def _dot_nt(a, b):
    return lax.dot_general(a, b, (((1,), (1,)), ((), ())), preferred_element_type=F32)


def _dot_tn(a, b):
    return lax.dot_general(a, b, (((0,), (0,)), ((), ())), preferred_element_type=F32)


def _sigmoid(x):
    return 0.5 + 0.5 * jnp.tanh(0.5 * x)


def _rms(h, g):
    rstd = lax.rsqrt(jnp.mean(h * h, axis=-1, keepdims=True) + RMS_EPS)
    xhat = h * rstd
    return xhat * g, xhat, rstd


def _rms_bwd(dhn, xhat, rstd, g):
    dx = dhn * g
    return rstd * (dx - xhat * jnp.mean(dx * xhat, axis=-1, keepdims=True))


def _gelu(x):
    k = 0.7978845608028654
    t = jnp.tanh(k * (x + 0.044715 * x * x * x))
    return 0.5 * x * (1.0 + t), t


def _gelu_grad(x, t):
    k = 0.7978845608028654
    return 0.5 * (1.0 + t) + 0.5 * x * (1.0 - t * t) * k * (1.0 + 3 * 0.044715 * x * x)


def _softplus(x):
    e = jnp.exp(-jnp.abs(x))
    return jnp.maximum(x, 0.0) + jnp.where(e < 1e-4, e - 0.5 * e * e, jnp.log(1.0 + e))


def _expm1_neg(z, exp_z):
    series = z * (1.0 + z * (0.5 + z * (1.0 / 6)))
    return jnp.where(z > -0.02, series, exp_z - 1.0)


def _tile_order(a, tm):
    return a.reshape(-1, F32_ROWS, tm // F32_ROWS, a.shape[-1]).swapaxes(1, 2).reshape(a.shape)


def _time_order(a, tm):
    return a.reshape(-1, tm // F32_ROWS, F32_ROWS, a.shape[-1]).swapaxes(1, 2).reshape(a.shape)


def _valid_rows(tile, tm):
    row = lax.broadcasted_iota(jnp.int32, (tm, 1), 0)
    time = (row & (F32_ROWS - 1)) * (tm // F32_ROWS) + (row >> 3) + tile * tm
    return time >= tm - N_META


def _sublane():
    return lax.broadcasted_iota(jnp.int32, (F32_ROWS, 1), 0)


def _past_rows(width):
    return (width - 1) * F32_ROWS


def _halo_block(past, tm, nt):
    rows = -(-past // ACT_ROWS) * ACT_ROWS
    return rows, lambda i: (jnp.maximum((nt - 1 - i) * (tm // rows) - 1, 0), 0)


def _link_past(buf, cols, width, tm):
    past = _past_rows(width)
    for k in range(1, width):
        rows = pl.ds(past - F32_ROWS * k, F32_ROWS)
        before = pltpu.roll(buf[rows, cols], 1, 0)
        mine = pltpu.roll(buf[pl.ds(past + tm - F32_ROWS * k, F32_ROWS), cols], 1, 0)
        buf[rows, cols] = jnp.where(_sublane() == 0, before, mine)


def _link_future(buf, cols, width, tm):
    for k in range(1, width):
        rows = pl.ds(tm + F32_ROWS * (k - 1), F32_ROWS)
        after = pltpu.roll(buf[rows, cols], F32_ROWS - 1, 0)
        mine = pltpu.roll(buf[pl.ds(F32_ROWS * (k - 1), F32_ROWS), cols], F32_ROWS - 1, 0)
        buf[rows, cols] = jnp.where(_sublane() == F32_ROWS - 1, after, mine)


def _conv_taps(buf, cols, width, tm):
    return [buf[pl.ds(F32_ROWS * k, tm), cols] for k in range(width)]


def _conv_back(buf, cw_ref, cols, width, tm):
    return sum(cw_ref[k:k + 1, cols] * buf[pl.ds(F32_ROWS * (width - 1 - k), tm), cols] for k in range(width))


ANY = pl.BlockSpec(memory_space=pl.ANY)


def _place():
    return lax.axis_index("x"), lax.axis_index("y"), lax.axis_index("c")


def _other_chips(x, y):
    return [(1 - x, y), (x, 1 - y), (1 - x, 1 - y)]


class _Gather:
    def __init__(self, shards):
        nk = len(shards)
        self.arrays = list(shards)
        self.out_shape = [jax.ShapeDtypeStruct((N_CHIPS,) + s.shape, s.dtype) for s in shards]
        self.scratch = [pltpu.SemaphoreType.DMA((nk, 3)), pltpu.SemaphoreType.DMA((nk, 3)), pltpu.SemaphoreType.DMA((nk,))]

    def run(self, ins, outs, sems, start):
        send_sems, recv_sems, local_sems = sems
        x, y, c = _place()
        mine = 2 * x + y
        for k in range(len(ins)):
            local = pltpu.make_async_copy(ins[k], outs[k].at[mine], local_sems.at[k])
            local.start() if start else local.wait()
            for j, (px, py) in enumerate(_other_chips(x, y)):
                sems_kj = dict(send_sem=send_sems.at[k, j], recv_sem=recv_sems.at[k, j], device_id=(px, py, c),
                               device_id_type=MESH_ID)
                send = pltpu.make_async_remote_copy(src_ref=ins[k], dst_ref=outs[k].at[mine], **sems_kj)
                if start:
                    send.start()
                else:
                    pltpu.make_async_remote_copy(src_ref=ins[k], dst_ref=outs[k].at[2 * px + py], **sems_kj).wait_recv()
                    send.wait_send()


class _GatherHalves:
    def __init__(self, shards):
        nk = len(shards)
        self.arrays = list(shards)
        self.out_shape = [jax.ShapeDtypeStruct((N_CHIPS,) + s.shape, s.dtype) for s in shards]
        self.scratch = [pltpu.SemaphoreType.DMA((nk, 3)) for _ in range(4)] + [pltpu.SemaphoreType.DMA((nk,))]

    def run(self, ins, outs, sems, start):
        far_send, far_recv, near_send, near_recv, local_sems = sems
        x, y, c = _place()
        mine = 2 * x + y
        for phase in ((0,) if start else (1, 2)):
            for k in range(len(ins)):
                half = ins[k].shape[0] // 2
                my_half = pl.ds(pl.multiple_of(c * half, ACT_ROWS), half)
                other_half = pl.ds(pl.multiple_of((1 - c) * half, ACT_ROWS), half)
                if phase != 1:
                    local = pltpu.make_async_copy(ins[k], outs[k].at[mine], local_sems.at[k])
                    local.start() if phase == 0 else local.wait()
                for j, (px, py) in enumerate(_other_chips(x, y)):
                    theirs = 2 * px + py
                    far = dict(send_sem=far_send.at[k, j], recv_sem=far_recv.at[k, j], device_id=(px, py, c),
                               device_id_type=MESH_ID)
                    near = dict(send_sem=near_send.at[k, j], recv_sem=near_recv.at[k, j], device_id=(x, y, 1 - c),
                                device_id_type=MESH_ID)
                    landed = outs[k].at[theirs, my_half]
                    send = lambda: pltpu.make_async_remote_copy(src_ref=ins[k].at[my_half], dst_ref=outs[k].at[mine, my_half], **far)
                    pass_on = lambda: pltpu.make_async_remote_copy(src_ref=landed, dst_ref=landed, **near)
                    if phase == 0:
                        send().start()
                    elif phase == 1:
                        pltpu.make_async_remote_copy(src_ref=ins[k].at[my_half], dst_ref=landed, **far).wait_recv()
                        pass_on().start()
                    else:
                        pltpu.make_async_remote_copy(src_ref=landed, dst_ref=outs[k].at[theirs, other_half], **near).wait_recv()
                        pass_on().wait_send()
                        send().wait_send()


class _Scatter:
    def __init__(self, parts):
        nk = len(parts)
        self.arrays = list(parts)
        self.out_shape = [jax.ShapeDtypeStruct((3,) + p.shape[1:], p.dtype) for p in parts]
        self.scratch = [pltpu.SemaphoreType.DMA((nk, 3)), pltpu.SemaphoreType.DMA((nk, 3))]

    def run(self, ins, outs, sems, start):
        send_sems, recv_sems = sems
        x, y, c = _place()
        for k in range(len(ins)):
            for j, (px, py) in enumerate(_other_chips(x, y)):
                send = pltpu.make_async_remote_copy(
                    src_ref=ins[k].at[2 * px + py], dst_ref=outs[k].at[j], send_sem=send_sems.at[k, j],
                    recv_sem=recv_sems.at[k, j], device_id=(px, py, c), device_id_type=MESH_ID)
                if start:
                    send.start()
                else:
                    send.wait_recv()
                    send.wait_send()


def _exchange(ride, name):
    n_in, n_out = len(ride.arrays), len(ride.out_shape)

    def body(*refs):
        ride.run(refs[:n_in], refs[n_in:n_in + n_out], refs[n_in + n_out:], start=True)
        ride.run(refs[:n_in], refs[n_in:n_in + n_out], refs[n_in + n_out:], start=False)

    return pl.pallas_call(body, name=name, in_specs=[ANY] * n_in, out_specs=[ANY] * n_out, out_shape=ride.out_shape,
                          scratch_shapes=ride.scratch)(*ride.arrays)


def _launch(body, operands, *, name, grid, in_specs, out_specs, out_shape, scratch_shapes=(), ride=None):
    common = dict(name=name, grid=grid, compiler_params=_params(len(grid)))
    if ride is None:
        return pl.pallas_call(body, in_specs=in_specs, out_specs=out_specs, out_shape=out_shape,
                              scratch_shapes=list(scratch_shapes), **common)(*operands)
    n_in, n_out, n_scr = len(operands), len(out_shape), len(scratch_shapes)
    r_in, r_out = len(ride.arrays), len(ride.out_shape)

    def riding(*refs):
        ins, refs = refs[:n_in], refs[n_in:]
        r_ins, refs = refs[:r_in], refs[r_in:]
        outs, refs = refs[:n_out], refs[n_out:]
        r_outs, refs = refs[:r_out], refs[r_out:]
        scr, r_sems = refs[:n_scr], refs[n_scr:]
        step = [pl.program_id(axis) for axis in range(len(grid))]
        first = functools.reduce(jnp.logical_and, [s == 0 for s in step])
        last = functools.reduce(jnp.logical_and, [s == size - 1 for s, size in zip(step, grid)])

        @pl.when(first)
        def _():
            ride.run(r_ins, r_outs, r_sems, start=True)

        body(*ins, *outs, *scr)

        @pl.when(last)
        def _():
            ride.run(r_ins, r_outs, r_sems, start=False)

    return pl.pallas_call(
        riding, in_specs=list(in_specs) + [ANY] * r_in, out_specs=list(out_specs) + [ANY] * r_out,
        out_shape=list(out_shape) + ride.out_shape, scratch_shapes=list(scratch_shapes) + ride.scratch, **common,
    )(*operands, *ride.arrays)


def _sc_fwd(x, first, g, w_in, cw, w_out, *, tm, ride=None):
    seq, d = x.shape
    nt = seq // tm + 1
    nq, _, n = w_in.shape
    width = cw.shape[0]
    past = _past_rows(width)

    def body(x_ref, first_ref, g_ref, win_ref, cw_ref, wout_ref, h1_ref, hh_ref, hh_scr, cbuf):
        i = pl.program_id(0)

        @pl.when(i == 0)
        def _():
            cbuf[pl.ds(0, past), :] = jnp.zeros((past, d), F32)

        h = jnp.where(i == 0, first_ref[...], x_ref[...])
        hn = _rms(h, g_ref[...])[0].astype(MXU_DT)
        for q in range(nq):
            hh_scr[:, q * n:(q + 1) * n] = _dot(hn, win_ref[q])
        hh_ref[...] = hh_scr[...].astype(hh_ref.dtype)
        b = hh_scr[:, 0:d]
        cbuf[pl.ds(past, tm), :] = hh_scr[:, d:2 * d] * hh_scr[:, 2 * d:3 * d]
        last = cbuf[pl.ds(tm, past), :]
        _link_past(cbuf, slice(None), width, tm)
        u = sum(cw_ref[k:k + 1, :] * tap for k, tap in enumerate(_conv_taps(cbuf, slice(None), width, tm)))
        cbuf[pl.ds(0, past), :] = last
        h1_ref[...] = h + _dot((b * u).astype(MXU_DT), wout_ref[...])

    return _launch(
        body, [x, first, g, w_in, cw, w_out], name="sc_fwd", grid=(nt,),
        in_specs=[pl.BlockSpec((tm, d), lambda i: (jnp.maximum(i - 1, 0), 0)), _const((tm, d)), _const((1, d)),
                  _const(w_in.shape), _const(cw.shape), _const(w_out.shape)],
        out_specs=[pl.BlockSpec((tm, d), lambda i: (i, 0)), pl.BlockSpec((tm, nq * n), lambda i: (i, 0))],
        out_shape=[jax.ShapeDtypeStruct((nt * tm, d), F32), jax.ShapeDtypeStruct((nt * tm, nq * n), ACT_DT)],
        scratch_shapes=[pltpu.VMEM((tm, nq * n), F32), pltpu.VMEM((past + tm, d), F32)],
        ride=ride,
    )


def _sc_bwd(dh, hh, x, first, g, w_in, cw, w_out, *, tm, ride=None):
    t_len, d = dh.shape
    nt = t_len // tm
    nq, _, n = w_in.shape
    width = cw.shape[0]
    past = _past_rows(width)
    halo_rows, halo_index = _halo_block(past, tm, nt)

    def body(dh_ref, hh_ref, hhp_ref, x_ref, first_ref, g_ref, win_ref, cw_ref, wout_ref,
             dx_ref, dhh_ref, z_ref, hn_ref, dcw_ref, dg_ref, dfirst_ref, cbuf, dbuf):
        i = pl.program_id(0)
        r = nt - 1 - i

        @pl.when(i == 0)
        def _():
            dbuf[pl.ds(tm, past), :] = jnp.zeros((past, d), F32)
            dcw_ref[...] = jnp.zeros_like(dcw_ref)
            dg_ref[...] = jnp.zeros_like(dg_ref)

        dh_out = dh_ref[...]
        b = hh_ref[:, 0:d].astype(F32)
        c = hh_ref[:, d:2 * d].astype(F32)
        v = hh_ref[:, 2 * d:3 * d].astype(F32)
        prev = hhp_ref[...].astype(F32)[halo_rows - past:, :]
        cbuf[pl.ds(0, past), :] = jnp.where(r > 0, prev[:, d:2 * d] * prev[:, 2 * d:3 * d], 0.0)
        cbuf[pl.ds(past, tm), :] = c * v
        _link_past(cbuf, slice(None), width, tm)
        taps = _conv_taps(cbuf, slice(None), width, tm)
        u = sum(cw_ref[k:k + 1, :] * taps[k] for k in range(width))
        z_ref[...] = (b * u).astype(z_ref.dtype)
        dz = _dot_nt(dh_out.astype(MXU_DT), wout_ref[...])
        dhh_ref[:, 0:d] = (dz * u).astype(dhh_ref.dtype)
        du = dz * b
        for k in range(width):
            dcw_ref[k:k + 1, :] += jnp.sum(taps[k] * du, axis=0, keepdims=True)
        dbuf[pl.ds(0, tm), :] = du
        _link_future(dbuf, slice(None), width, tm)
        dcv = _conv_back(dbuf, cw_ref, slice(None), width, tm)
        dbuf[pl.ds(tm, past), :] = dbuf[pl.ds(0, past), :]
        dhh_ref[:, d:2 * d] = (dcv * v).astype(dhh_ref.dtype)
        dhh_ref[:, 2 * d:3 * d] = (dcv * c).astype(dhh_ref.dtype)
        parts = [_dot_nt(dhh_ref[:, q * n:(q + 1) * n], win_ref[q]) for q in range(nq)]
        dhn = functools.reduce(lambda a, b: a + b, parts)
        h_in = jnp.where(r == 0, first_ref[...], x_ref[...])
        dh_in = _norm_bwd_tile(dhn, h_in, dh_out, g_ref[...], _valid_rows(r, tm), hn_ref, dg_ref)

        @pl.when(r == 0)
        def _():
            dfirst_ref[...] = dh_in

        @pl.when(r > 0)
        def _():
            dx_ref[...] = dh_in

    rev = lambda i: (nt - 1 - i, 0)
    rev_x = lambda i: (jnp.maximum(nt - 2 - i, 0), 0)
    return _launch(
        body, [dh, hh, hh, x, first, g, w_in, cw, w_out], name="sc_bwd", grid=(nt,),
        in_specs=[pl.BlockSpec((tm, d), rev), pl.BlockSpec((tm, 3 * d), rev), pl.BlockSpec((halo_rows, 3 * d), halo_index),
                  pl.BlockSpec((tm, d), rev_x), _const((tm, d)), _const((1, d)), _const(w_in.shape), _const(cw.shape),
                  _const(w_out.shape)],
        out_specs=[pl.BlockSpec((tm, d), rev_x), pl.BlockSpec((tm, 3 * d), rev), pl.BlockSpec((tm, d), rev),
                   pl.BlockSpec((tm, d), rev), _const((F32_ROWS, d)), _const((F32_ROWS, d)), _const((tm, d))],
        out_shape=[jax.ShapeDtypeStruct((t_len - tm, d), F32), jax.ShapeDtypeStruct((t_len, 3 * d), ACT_DT),
                   jax.ShapeDtypeStruct((t_len, d), ACT_DT), jax.ShapeDtypeStruct((t_len, d), ACT_DT),
                   jax.ShapeDtypeStruct((F32_ROWS, d), F32), jax.ShapeDtypeStruct((F32_ROWS, d), F32),
                   jax.ShapeDtypeStruct((tm, d), F32)],
        scratch_shapes=[pltpu.VMEM((past + tm, d), F32), pltpu.VMEM((tm + past, d), F32)],
        ride=ride,
    )


def _ffn_fwd(h, g, w_up, cw, w_down, *, tm, ride=None, loss=None):
    t_len, d = h.shape
    nt = t_len // tm
    nq, _, n = w_up.shape
    width = cw.shape[0]
    past = _past_rows(width)

    def body(h_ref, g_ref, wup_ref, cw_ref, wdn_ref, *rest):
        if loss is None:
            out_ref, hu_ref, ubuf, tail = rest
        else:
            t_ref, gf_ref, out_ref, hu_ref, sq_ref, dgf_ref, ubuf, tail = rest
        i = pl.program_id(0)

        @pl.when(i == 0)
        def _():
            tail[...] = jnp.zeros_like(tail)

        h_in = h_ref[...]
        hn = _rms(h_in, g_ref[...])[0].astype(MXU_DT)
        ubuf[pl.ds(0, past), :] = tail[...]
        for q in range(nq):
            ubuf[pl.ds(past, tm), q * n:(q + 1) * n] = _dot(hn, wup_ref[q])
        hu_ref[...] = ubuf[pl.ds(past, tm), :].astype(hu_ref.dtype)
        tail[...] = ubuf[pl.ds(tm, past), :]
        _link_past(ubuf, slice(None), width, tm)
        acc = h_in
        for j in range(nq // 2):
            gcol, vcol = slice(j * n, (j + 1) * n), slice((nq // 2 + j) * n, (nq // 2 + j + 1) * n)
            conv = lambda cols: sum(cw_ref[k:k + 1, cols] * tap for k, tap in enumerate(_conv_taps(ubuf, cols, width, tm)))
            gj, vj = conv(gcol), conv(vcol)
            acc = acc + _dot((gj * _sigmoid(gj) * vj).astype(MXU_DT), wdn_ref[j * n:(j + 1) * n, :])
        if loss is None:
            out_ref[...] = acc
            return

        @pl.when(i == 0)
        def _():
            sq_ref[...] = jnp.zeros_like(sq_ref)
            dgf_ref[...] = jnp.zeros_like(dgf_ref)
            out_ref[...] = jnp.zeros_like(out_ref)

        @pl.when(i > 0)
        def _():
            gain = gf_ref[...]
            out, xhat, rstd = _rms(acc, gain)
            err = out - t_ref[...]
            sq_ref[0:1, :] += jnp.sum(err * err, axis=0, keepdims=True)
            dout = err * (1.0 / d)
            dgf_ref[0:1, :] += jnp.sum(dout * xhat, axis=0, keepdims=True)
            out_ref[...] = _rms_bwd(dout, xhat, rstd, gain)

    row = lambda i: (i, 0)
    stat = jax.ShapeDtypeStruct((F32_ROWS, d), F32)
    return _launch(
        body, [h, g, w_up, cw, w_down] + list(loss or ()), name="ffn_fwd", grid=(nt,),
        in_specs=[pl.BlockSpec((tm, d), row), _const((1, d)), _const(w_up.shape), _const(cw.shape), _const(w_down.shape)]
        + ([pl.BlockSpec((tm, d), lambda i: (jnp.maximum(i - 1, 0), 0)), _const((1, d))] if loss else []),
        out_specs=[pl.BlockSpec((tm, d), row), pl.BlockSpec((tm, nq * n), row)] + ([_const(stat.shape)] * 2 if loss else []),
        out_shape=[jax.ShapeDtypeStruct((t_len, d), F32), jax.ShapeDtypeStruct((t_len, nq * n), ACT_DT)]
        + ([stat, stat] if loss else []),
        scratch_shapes=[pltpu.VMEM((past + tm, nq * n), F32), pltpu.VMEM((past, nq * n), F32)],
        ride=ride,
    )


def _norm_bwd_tile(dhn, h_in, dh, gain, valid, hn_ref, dg_ref):
    hn, xhat, rstd = _rms(h_in, gain)
    hn_ref[...] = hn.astype(hn_ref.dtype)
    dg_ref[0:1, :] += jnp.sum(dhn * xhat, axis=0, keepdims=True)
    return jnp.where(valid, dh + _rms_bwd(dhn, xhat, rstd, gain), 0.0)


def _ffn_bwd(dh, hu, h, g, w_up, cw, w_down, *, tm, ride=None):
    t_len, d = dh.shape
    nt = t_len // tm
    ff = hu.shape[1]
    n = ff // 4
    width = cw.shape[0]
    past = _past_rows(width)
    halo_rows, halo_index = _halo_block(past, tm, nt)

    def body(dh_ref, hu_ref, hup_ref, h_ref, g_ref, wup_ref, cw_ref, wdn_ref,
             dhin_ref, a_ref, dhu_ref, hn_ref, dcw_ref, dg_ref, ubuf, dbuf, head):
        i = pl.program_id(0)
        r = nt - 1 - i

        @pl.when(i == 0)
        def _():
            head[...] = jnp.zeros_like(head)
            dcw_ref[...] = jnp.zeros_like(dcw_ref)
            dg_ref[...] = jnp.zeros_like(dg_ref)

        dh_out = dh_ref[...]
        dhb = dh_out.astype(MXU_DT)
        dhn_parts = []
        d_act = [_dot_nt(dhb, wdn_ref[j * n:(j + 1) * n, :]) for j in range(2)]
        for j in range(2):
            mine = slice(0, n), slice(n, 2 * n)
            full = slice(j * n, (j + 1) * n), slice((2 + j) * n, (3 + j) * n)
            for here, there in zip(mine, full):
                prev = hup_ref[:, there].astype(F32)[halo_rows - past:, :]
                ubuf[pl.ds(0, past), here] = jnp.where(r > 0, prev, 0.0)
                ubuf[pl.ds(past, tm), here] = hu_ref[:, there].astype(F32)
                dbuf[pl.ds(tm, past), here] = head[:, there]
            _link_past(ubuf, slice(None), width, tm)
            conv = lambda here, there: sum(cw_ref[k:k + 1, there] * tap
                                           for k, tap in enumerate(_conv_taps(ubuf, here, width, tm)))
            gj, vj = conv(mine[0], full[0]), conv(mine[1], full[1])
            sg = _sigmoid(gj)
            s = gj * sg
            a_ref[:, full[0]] = (s * vj).astype(a_ref.dtype)
            da = d_act[j]
            dbuf[pl.ds(0, tm), mine[1]] = da * s
            dbuf[pl.ds(0, tm), mine[0]] = da * vj * (sg * (1.0 + gj * (1.0 - sg)))
            for here, there in zip(mine, full):
                head[:, there] = dbuf[pl.ds(0, past), here]
            _link_future(dbuf, slice(None), width, tm)
            for here, there in zip(mine, full):
                dy = dbuf[pl.ds(0, tm), here]
                for k, tap in enumerate(_conv_taps(ubuf, here, width, tm)):
                    dcw_ref[k:k + 1, there] += jnp.sum(tap * dy, axis=0, keepdims=True)
                dhu = sum(cw_ref[k:k + 1, there] * dbuf[pl.ds(F32_ROWS * (width - 1 - k), tm), here]
                          for k in range(width)).astype(dhu_ref.dtype)
                dhu_ref[:, there] = dhu
                dhn_parts.append(_dot_nt(dhu, wup_ref[there.start // n]))
        dhn = (dhn_parts[0] + dhn_parts[1]) + (dhn_parts[2] + dhn_parts[3])
        dhin_ref[...] = _norm_bwd_tile(dhn, h_ref[...], dh_out, g_ref[...], _valid_rows(r, tm), hn_ref, dg_ref)

    rev = lambda i: (nt - 1 - i, 0)
    return _launch(
        body, [dh, hu, hu, h, g, w_up, cw, w_down], name="ffn_bwd", grid=(nt,),
        in_specs=[pl.BlockSpec((tm, d), rev), pl.BlockSpec((tm, ff), rev), pl.BlockSpec((halo_rows, ff), halo_index),
                  pl.BlockSpec((tm, d), rev), _const((1, d)), _const(w_up.shape), _const(cw.shape), _const(w_down.shape)],
        out_specs=[pl.BlockSpec((tm, d), rev), pl.BlockSpec((tm, 2 * n), rev), pl.BlockSpec((tm, ff), rev),
                   pl.BlockSpec((tm, d), rev), _const((F32_ROWS, ff)), _const((F32_ROWS, d))],
        out_shape=[jax.ShapeDtypeStruct((t_len, d), F32), jax.ShapeDtypeStruct((t_len, 2 * n), ACT_DT),
                   jax.ShapeDtypeStruct((t_len, ff), ACT_DT), jax.ShapeDtypeStruct((t_len, d), ACT_DT),
                   jax.ShapeDtypeStruct((F32_ROWS, ff), F32), jax.ShapeDtypeStruct((F32_ROWS, d), F32)],
        scratch_shapes=[pltpu.VMEM((past + tm, 2 * n), F32), pltpu.VMEM((tm + past, 2 * n), F32), pltpu.VMEM((past, ff), F32)],
        ride=ride,
    )


V_CONV_B, V_B_A, V_B_X, V_LAMBDA = 0, 1, 2, 3
G_CONV_W, G_CONV_B, G_B_A, G_B_X, G_LAMBDA = 0, 4, 5, 6, 7


def _scan(a_ref, b_ref, edge, tm, reverse):
    nj = tm // F32_ROWS
    order = range(nj - 1, -1, -1) if reverse else range(nj)
    slab = lambda ref, j: ref[pl.ds(F32_ROWS * j, F32_ROWS), :]
    a_run = b_run = None
    for j in order:
        a_j, b_j = slab(a_ref, j), slab(b_ref, j)
        if a_run is not None:
            b_j = b_j + a_j * b_run
            a_j = a_j * a_run
            b_ref[pl.ds(F32_ROWS * j, F32_ROWS), :] = b_j
            a_ref[pl.ds(F32_ROWS * j, F32_ROWS), :] = a_j
        a_run, b_run = a_j, b_j
    sub = _sublane()
    shift = 1
    while shift < F32_ROWS:
        amount = F32_ROWS - shift if reverse else shift
        keep = (sub < F32_ROWS - shift) if reverse else (sub >= shift)
        b_run = jnp.where(keep, b_run + a_run * pltpu.roll(b_run, amount, 0), b_run)
        a_run = jnp.where(keep, a_run * pltpu.roll(a_run, amount, 0), a_run)
        shift *= 2
    outer = edge[0:1, :] if reverse else edge[F32_ROWS - 1:F32_ROWS, :]
    ends = b_run + a_run * outer
    if reverse:
        carry = jnp.where(sub == F32_ROWS - 1, outer, pltpu.roll(ends, F32_ROWS - 1, 0))
    else:
        carry = jnp.where(sub == 0, outer, pltpu.roll(ends, 1, 0))
    for j in range(nj):
        b_ref[pl.ds(F32_ROWS * j, F32_ROWS), :] = slab(b_ref, j) + slab(a_ref, j) * carry
    return slab(b_ref, 0 if reverse else nj - 1)


def _rg_gates(u, vec_ref, wa_ref, wx_ref, pre_scr, nb, bd):
    ub = u.astype(MXU_DT)
    for k in range(nb):
        blk = slice(k * bd, (k + 1) * bd)
        pre_scr[0, :, blk] = _dot(ub[:, blk], wa_ref[k])
        pre_scr[1, :, blk] = _dot(ub[:, blk], wx_ref[k])
    r_gate = _sigmoid(pre_scr[0] + vec_ref[V_B_A:V_B_A + 1, :])
    i_gate = _sigmoid(pre_scr[1] + vec_ref[V_B_X:V_B_X + 1, :])
    return r_gate, i_gate


def _rg_decay(r_gate, vec_ref):
    sp = _softplus(-vec_ref[V_LAMBDA:V_LAMBDA + 1, :])
    log_a = -RG_C * r_gate * sp
    a = jnp.exp(log_a)
    one_minus_a2 = jnp.maximum(-_expm1_neg(2.0 * log_a, a * a), 1e-30)
    inv_mult = lax.rsqrt(one_minus_a2)
    return a, one_minus_a2 * inv_mult, inv_mult, sp


def _rg_fwd(h, g, w_in, cw, vec, wa, wx, w_out, *, tm):
    t_len, d = h.shape
    nt = t_len // tm
    nq, _, n = w_in.shape
    dr = 2 * n
    width = cw.shape[0]
    past = _past_rows(width)
    nb, bd, _ = wa.shape

    def body(h_ref, g_ref, win_ref, cw_ref, vec_ref, wa_ref, wx_ref, wout_ref, out_ref, hh_ref, hs_ref, gates_ref,
             gbuf, rbuf, pre_scr, tail, edge):
        i = pl.program_id(0)

        @pl.when(i == 0)
        def _():
            tail[...] = jnp.zeros_like(tail)
            edge[...] = jnp.zeros_like(edge)

        h_in = h_ref[...]
        hn = _rms(h_in, g_ref[...])[0].astype(MXU_DT)
        rbuf[pl.ds(0, past), :] = tail[...]
        for q in range(2):
            gbuf[:, q * n:(q + 1) * n] = _dot(hn, win_ref[q])
            rbuf[pl.ds(past, tm), q * n:(q + 1) * n] = _dot(hn, win_ref[2 + q])
        hh_ref[:, 0:dr] = gbuf[...].astype(hh_ref.dtype)
        hh_ref[:, dr:2 * dr] = rbuf[pl.ds(past, tm), :].astype(hh_ref.dtype)
        tail[...] = rbuf[pl.ds(tm, past), :]
        _link_past(rbuf, slice(None), width, tm)
        taps = _conv_taps(rbuf, slice(None), width, tm)
        u = sum(cw_ref[k:k + 1, :] * taps[k] for k in range(width)) + vec_ref[V_CONV_B:V_CONV_B + 1, :]
        r_gate, i_gate = _rg_gates(u, vec_ref, wa_ref, wx_ref, pre_scr, nb, bd)
        for k, kept in enumerate((u, r_gate, i_gate)):
            gates_ref[:, k * dr:(k + 1) * dr] = kept.astype(gates_ref.dtype)
        a, mult, _, _ = _rg_decay(r_gate, vec_ref)
        pre_scr[0] = a
        pre_scr[1] = jnp.where(_valid_rows(i, tm), mult * (i_gate * u), 0.0)
        edge[...] = _scan(pre_scr.at[0], pre_scr.at[1], edge[...], tm, reverse=False)
        hs = pre_scr[1]
        hs_ref[...] = hs
        y = hs * _gelu(gbuf[...])[0]
        out_ref[...] = h_in + _dot(y.astype(MXU_DT), wout_ref[...])

    row = lambda i: (i, 0)
    return pl.pallas_call(
        body, name="rg_fwd", grid=(nt,),
        in_specs=[pl.BlockSpec((tm, d), row), _const((1, d)), _const(w_in.shape), _const(cw.shape), _const(vec.shape),
                  _const(wa.shape), _const(wx.shape), _const(w_out.shape)],
        out_specs=[pl.BlockSpec((tm, d), row), pl.BlockSpec((tm, 2 * dr), row), pl.BlockSpec((tm, dr), row),
                   pl.BlockSpec((tm, 3 * dr), row)],
        out_shape=[jax.ShapeDtypeStruct((t_len, d), F32), jax.ShapeDtypeStruct((t_len, 2 * dr), ACT_DT),
                   jax.ShapeDtypeStruct((t_len, dr), F32), jax.ShapeDtypeStruct((t_len, 3 * dr), ACT_DT)],
        scratch_shapes=[pltpu.VMEM((tm, dr), F32), pltpu.VMEM((past + tm, dr), F32), pltpu.VMEM((2, tm, dr), F32),
                        pltpu.VMEM((past, dr), F32), pltpu.VMEM((F32_ROWS, dr), F32)],
        compiler_params=_params(),
    )(h, g, w_in, cw, vec, wa, wx, w_out)


def _rg_bwd(dh, hh, hs, gates, h, g, w_in, cw, vec, wa, wx, w_out, *, tm, ride=None):
    t_len, d = dh.shape
    nt = t_len // tm
    dr = hs.shape[1]
    n = w_in.shape[2]
    width = cw.shape[0]
    nb, bd, _ = wa.shape
    past = _past_rows(width)
    halo_rows, halo_index = _halo_block(past, tm, nt)
    one = F32_ROWS

    def body(dh_ref, hh_ref, hhp_ref, hs_ref, hsp_ref, gates_ref, h_ref, g_ref, win_ref, cw_ref, vec_ref, wa_ref, wx_ref, wout_ref,
             dhin_ref, dhh_ref, y_ref, hn_ref, dvec_ref, dwa_ref, dwx_ref, dg_ref, rbuf, dbuf, pre_scr, hbuf, abuf, edge):
        i = pl.program_id(0)
        r = nt - 1 - i

        @pl.when(i == 0)
        def _():
            dbuf[pl.ds(tm, past), :] = jnp.zeros((past, dr), F32)
            abuf[pl.ds(tm, one), :] = jnp.zeros((one, dr), F32)
            edge[...] = jnp.zeros_like(edge)
            dvec_ref[...] = jnp.zeros_like(dvec_ref)
            dwa_ref[...] = jnp.zeros_like(dwa_ref)
            dwx_ref[...] = jnp.zeros_like(dwx_ref)
            dg_ref[...] = jnp.zeros_like(dg_ref)

        dh_out = dh_ref[...]
        gb = hh_ref[:, 0:dr].astype(F32)
        prev = hhp_ref[...].astype(F32)[halo_rows - past:, dr:2 * dr]
        rbuf[pl.ds(0, past), :] = jnp.where(r > 0, prev, 0.0)
        rbuf[pl.ds(past, tm), :] = hh_ref[:, dr:2 * dr].astype(F32)
        _link_past(rbuf, slice(None), width, tm)
        taps = _conv_taps(rbuf, slice(None), width, tm)
        ub = gates_ref[:, 0:dr].astype(MXU_DT)
        u, r_gate, i_gate = (gates_ref[:, k * dr:(k + 1) * dr].astype(F32) for k in range(3))
        a, mult, inv_mult, sp = _rg_decay(r_gate, vec_ref)
        hs_t = hs_ref[...]
        hbuf[pl.ds(0, one), :] = jnp.where(r > 0, hsp_ref[...], 0.0)
        hbuf[pl.ds(one, tm), :] = hs_t
        _link_past(hbuf, slice(None), 2, tm)
        h_prev = hbuf[pl.ds(0, tm), :]
        gate, th = _gelu(gb)
        y_ref[...] = (hs_t * gate).astype(y_ref.dtype)
        dy = _dot_nt(dh_out.astype(MXU_DT), wout_ref[...])
        d_gb = (dy * hs_t * _gelu_grad(gb, th)).astype(dhh_ref.dtype)
        dhh_ref[:, 0:dr] = d_gb
        dhn = sum(_dot_nt(d_gb[:, q * n:(q + 1) * n], win_ref[q]) for q in range(2))
        abuf[pl.ds(0, tm), :] = a
        _link_future(abuf, slice(None), 2, tm)
        pre_scr[0] = abuf[pl.ds(one, tm), :]
        pre_scr[1] = dy * gate
        edge[...] = _scan(pre_scr.at[0], pre_scr.at[1], edge[...], tm, reverse=True)
        abuf[pl.ds(tm, one), :] = abuf[pl.ds(0, one), :]
        d_hs = pre_scr[1]
        d_b = jnp.where(_valid_rows(r, tm), d_hs, 0.0)
        d_iu = d_b * mult
        d_log_a = d_hs * h_prev * a - d_b * (i_gate * u) * (a * a) * inv_mult
        dvec_ref[G_LAMBDA:G_LAMBDA + 1, :] += jnp.sum(d_log_a * r_gate, axis=0, keepdims=True) * (-RG_C)
        d_pre_r = d_log_a * (-RG_C * sp) * r_gate * (1.0 - r_gate)
        d_pre_i = d_iu * u * i_gate * (1.0 - i_gate)
        dvec_ref[G_B_A:G_B_A + 1, :] += jnp.sum(d_pre_r, axis=0, keepdims=True)
        dvec_ref[G_B_X:G_B_X + 1, :] += jnp.sum(d_pre_i, axis=0, keepdims=True)
        dbuf[pl.ds(0, tm), :] = d_iu * i_gate
        d_pre_r = d_pre_r.astype(MXU_DT)
        d_pre_i = d_pre_i.astype(MXU_DT)
        for k in range(nb):
            blk = slice(k * bd, (k + 1) * bd)
            dwa_ref[k] += _dot_tn(ub[:, blk], d_pre_r[:, blk])
            dwx_ref[k] += _dot_tn(ub[:, blk], d_pre_i[:, blk])
            dbuf[pl.ds(0, tm), blk] += _dot_nt(d_pre_r[:, blk], wa_ref[k]) + _dot_nt(d_pre_i[:, blk], wx_ref[k])
        du = dbuf[pl.ds(0, tm), :]
        dvec_ref[G_CONV_B:G_CONV_B + 1, :] += jnp.sum(du, axis=0, keepdims=True)
        for k in range(width):
            dvec_ref[G_CONV_W + k:G_CONV_W + k + 1, :] += jnp.sum(taps[k] * du, axis=0, keepdims=True)
        _link_future(dbuf, slice(None), width, tm)
        d_rb = _conv_back(dbuf, cw_ref, slice(None), width, tm)
        dbuf[pl.ds(tm, past), :] = dbuf[pl.ds(0, past), :]
        d_rb = d_rb.astype(dhh_ref.dtype)
        dhh_ref[:, dr:2 * dr] = d_rb
        dhn = dhn + sum(_dot_nt(d_rb[:, q * n:(q + 1) * n], win_ref[2 + q]) for q in range(2))
        dhin_ref[...] = _norm_bwd_tile(dhn, h_ref[...], dh_out, g_ref[...], _valid_rows(r, tm), hn_ref, dg_ref)

        @pl.when(i == nt - 1)
        def _():
            lam = vec_ref[V_LAMBDA:V_LAMBDA + 1, :]
            dvec_ref[G_LAMBDA:G_LAMBDA + 1, :] = dvec_ref[G_LAMBDA:G_LAMBDA + 1, :] * (-_sigmoid(-lam))

    rev = lambda i: (nt - 1 - i, 0)
    return _launch(
        body, [dh, hh, hh, hs, hs, gates, h, g, w_in, cw, vec, wa, wx, w_out], name="rg_bwd", grid=(nt,),
        in_specs=[pl.BlockSpec((tm, d), rev), pl.BlockSpec((tm, 2 * dr), rev), pl.BlockSpec((halo_rows, 2 * dr), halo_index),
                  pl.BlockSpec((tm, dr), rev),
                  pl.BlockSpec((one, dr), lambda i: (jnp.maximum((nt - 1 - i) * (tm // one) - 1, 0), 0)),
                  pl.BlockSpec((tm, 3 * dr), rev), pl.BlockSpec((tm, d), rev), _const((1, d)), _const(w_in.shape),
                  _const(cw.shape), _const(vec.shape), _const(wa.shape), _const(wx.shape), _const(w_out.shape)],
        out_specs=[pl.BlockSpec((tm, d), rev), pl.BlockSpec((tm, 2 * dr), rev), pl.BlockSpec((tm, dr), rev),
                   pl.BlockSpec((tm, d), rev), _const((F32_ROWS, dr)), _const(wa.shape), _const(wx.shape),
                   _const((F32_ROWS, d))],
        out_shape=[jax.ShapeDtypeStruct((t_len, d), F32), jax.ShapeDtypeStruct((t_len, 2 * dr), ACT_DT),
                   jax.ShapeDtypeStruct((t_len, dr), ACT_DT), jax.ShapeDtypeStruct((t_len, d), ACT_DT),
                   jax.ShapeDtypeStruct((F32_ROWS, dr), F32), jax.ShapeDtypeStruct(wa.shape, F32),
                   jax.ShapeDtypeStruct(wx.shape, F32), jax.ShapeDtypeStruct((F32_ROWS, d), F32)],
        scratch_shapes=[pltpu.VMEM((past + tm, dr), F32), pltpu.VMEM((tm + past, dr), F32), pltpu.VMEM((2, tm, dr), F32),
                        pltpu.VMEM((one + tm, dr), F32), pltpu.VMEM((tm + one, dr), F32), pltpu.VMEM((F32_ROWS, dr), F32)],
        ride=ride,
    )


def _weight_grad(a, b, nb, *, rows, ride=None):
    t_len, k_dim = a.shape
    n = b.shape[1] // nb
    nt = t_len // rows

    def body(a_ref, b_ref, out_ref, wire_ref):
        @pl.when(pl.program_id(1) == 0)
        def _():
            out_ref[...] = jnp.zeros_like(out_ref)

        out_ref[0] += _dot_tn(a_ref[...].astype(MXU_DT), b_ref[...].astype(MXU_DT))

        @pl.when(pl.program_id(1) == nt - 1)
        def _():
            wire_ref[...] = out_ref[...].astype(wire_ref.dtype)

    block = pl.BlockSpec((1, k_dim, n), lambda j, i: (j, 0, 0))
    return _launch(
        body, [a, b], name="weight_grad", grid=(nb, nt),
        in_specs=[pl.BlockSpec((rows, k_dim), lambda j, i: (i, 0)), pl.BlockSpec((rows, n), lambda j, i: (i, j))],
        out_specs=[block, block],
        out_shape=[jax.ShapeDtypeStruct((nb, k_dim, n), F32), jax.ShapeDtypeStruct((nb, k_dim, n), WIRE_DT)],
        ride=ride,
    )


def _adamw(w, m, v, parts, *, rows, layer=0, into=None):
    n_layers, n_rows, n_cols = w.shape
    nt = n_rows // rows
    n_parts = len(parts)

    def body(w_ref, m_ref, v_ref, *rest):
        part_refs, (g_ref, d_ref, nm_ref, nv_ref) = rest[:n_parts], rest[-4:]
        w_ref, m_ref, v_ref, g_ref, d_ref, nm_ref, nv_ref = (r.at[0] for r in (w_ref, m_ref, v_ref, g_ref, d_ref, nm_ref, nv_ref))
        grad = part_refs[0][...].astype(F32)
        for p in part_refs[1:]:
            grad = grad + p[...].astype(F32)
        new_m = ADAM_B1 * m_ref[...] + (1.0 - ADAM_B1) * grad
        new_v = ADAM_B2 * v_ref[...] + (1.0 - ADAM_B2) * (grad * grad)
        m_hat = new_m / (1.0 - ADAM_B1 ** ADAM_STEP)
        v_hat = new_v / (1.0 - ADAM_B2 ** ADAM_STEP)
        g_ref[...] = grad
        d_ref[...] = -ADAM_LR * (m_hat / (jnp.sqrt(v_hat) + ADAM_EPS) + ADAM_WD * w_ref[...])
        nm_ref[...] = new_m
        nv_ref[...] = new_v

    spec = pl.BlockSpec((rows, n_cols), lambda i: (i, 0))
    layer_spec = pl.BlockSpec((1, rows, n_cols), lambda i: (layer, i, 0))
    into = list(into or [])
    return pl.pallas_call(
        body, name="adamw", grid=(nt,),
        in_specs=[layer_spec] * 3 + [spec] * n_parts + [ANY] * len(into), out_specs=[layer_spec] * 4,
        out_shape=[jax.ShapeDtypeStruct(w.shape, F32)] * 4,
        input_output_aliases={3 + n_parts + k: k for k in range(len(into))},
        compiler_params=_params(),
    )(w, m, v, *parts, *into)


def _sum_stack(stack, *, rows):
    n_stack, n_rows, n_cols = stack.shape

    def body(stack_ref, out_ref):
        acc = stack_ref[0]
        for j in range(1, n_stack):
            acc = acc + stack_ref[j]
        out_ref[...] = acc

    return pl.pallas_call(
        body, name="sum_stack", grid=(n_rows // rows,),
        in_specs=[pl.BlockSpec((n_stack, rows, n_cols), lambda i: (0, i, 0))],
        out_specs=pl.BlockSpec((rows, n_cols), lambda i: (i, 0)),
        out_shape=jax.ShapeDtypeStruct((n_rows, n_cols), F32),
        compiler_params=_params(),
    )(stack)


def _sum_parts(own, recv, *, rows):
    n_rows, n_cols = own.shape
    n_recv = recv.shape[0]

    def body(own_ref, recv_ref, out_ref):
        acc = own_ref[...].astype(F32)
        for j in range(n_recv):
            acc = acc + recv_ref[j].astype(F32)
        out_ref[...] = acc

    return pl.pallas_call(
        body, name="sum_parts", grid=(n_rows // rows,),
        in_specs=[pl.BlockSpec((rows, n_cols), lambda i: (i, 0)), pl.BlockSpec((n_recv, rows, n_cols), lambda i: (0, i, 0))],
        out_specs=pl.BlockSpec((rows, n_cols), lambda i: (i, 0)),
        out_shape=jax.ShapeDtypeStruct(own.shape, F32),
        compiler_params=_params(),
    )(own, recv)


class _Swap:
    def __init__(self, arrays):
        nk = len(arrays)
        self.arrays = list(arrays)
        self.out_shape = [jax.ShapeDtypeStruct(a.shape, a.dtype) for a in arrays]
        self.scratch = [pltpu.SemaphoreType.DMA((nk,)), pltpu.SemaphoreType.DMA((nk,))]

    def run(self, ins, outs, sems, start):
        send_sems, recv_sems = sems
        x, y, c = _place()
        for k in range(len(ins)):
            send = pltpu.make_async_remote_copy(src_ref=ins[k], dst_ref=outs[k], send_sem=send_sems.at[k],
                                                recv_sem=recv_sems.at[k], device_id=(x, y, 1 - c), device_id_type=MESH_ID)
            if start:
                send.start()
            else:
                send.wait_recv()
                send.wait_send()


class _AllDevices:
    def __init__(self, arrays):
        nk = len(arrays)
        self.arrays = list(arrays)
        self.out_shape = [jax.ShapeDtypeStruct((8,) + a.shape, a.dtype) for a in arrays]
        self.scratch = [pltpu.SemaphoreType.DMA((nk, 7)), pltpu.SemaphoreType.DMA((nk, 7)), pltpu.SemaphoreType.DMA((nk,))]

    def run(self, ins, outs, sems, start):
        send_sems, recv_sems, local_sems = sems
        x, y, c = _place()
        mine = 4 * x + 2 * y + c
        for k in range(len(ins)):
            local = pltpu.make_async_copy(ins[k], outs[k].at[mine], local_sems.at[k])
            local.start() if start else local.wait()
            for flip in range(1, 8):
                px, py, pc = x ^ (flip >> 2), y ^ ((flip >> 1) & 1), c ^ (flip & 1)
                sems_f = dict(send_sem=send_sems.at[k, flip - 1], recv_sem=recv_sems.at[k, flip - 1],
                              device_id=(px, py, pc), device_id_type=MESH_ID)
                send = pltpu.make_async_remote_copy(src_ref=ins[k], dst_ref=outs[k].at[mine], **sems_f)
                if start:
                    send.start()
                else:
                    pltpu.make_async_remote_copy(src_ref=ins[k], dst_ref=outs[k].at[4 * px + 2 * py + pc], **sems_f).wait_recv()
                    send.wait_send()


class _Both:
    def __init__(self, first, second):
        self.rides = (first, second)
        self.arrays = first.arrays + second.arrays
        self.out_shape = first.out_shape + second.out_shape
        self.scratch = first.scratch + second.scratch

    def run(self, ins, outs, sems, start):
        for ride in self.rides:
            n_in, n_out, n_sem = len(ride.arrays), len(ride.out_shape), len(ride.scratch)
            ride.run(ins[:n_in], outs[:n_out], sems[:n_sem], start)
            ins, outs, sems = ins[n_in:], outs[n_out:], sems[n_sem:]


def _pack(arrays, pad_rows=F32_ROWS):
    flat = jnp.concatenate([a.reshape(-1).astype(F32) for a in arrays])
    rows = -(-flat.shape[0] // (LANES * pad_rows)) * pad_rows
    return jnp.pad(flat, (0, rows * LANES - flat.shape[0])).reshape(rows, LANES)


def _unpack(packed, shapes):
    flat, out, off = packed.reshape(-1), [], 0
    for s in shapes:
        size = 1
        for dim in s:
            size *= dim
        out.append(flat[off:off + size].reshape(s))
        off += size
    return out


def _divisor_rows(n_rows, most=256):
    best = None
    for r in range(ACT_ROWS, most + 1, ACT_ROWS):
        if n_rows % r == 0:
            best = r
    return best or n_rows


def kernel(x, meta_tokens, norm_mix_g, norm_ffn_g, final_norm_g, sc_w_in, sc_conv_w, sc_w_out, rg_w_in, rg_conv_w, rg_conv_b, rg_w_gate_a, rg_b_gate_a, rg_w_gate_x, rg_b_gate_x, rg_lambda, rg_w_out, ffn_w_up, ffn_conv_w, ffn_w_down, loss_target, m_meta_tokens, m_norm_mix_g, m_norm_ffn_g, m_final_norm_g, m_sc_w_in, m_sc_conv_w, m_sc_w_out, m_rg_w_in, m_rg_conv_w, m_rg_conv_b, m_rg_w_gate_a, m_rg_b_gate_a, m_rg_w_gate_x, m_rg_b_gate_x, m_rg_lambda, m_rg_w_out, m_ffn_w_up, m_ffn_conv_w, m_ffn_w_down, v_meta_tokens, v_norm_mix_g, v_norm_ffn_g, v_final_norm_g, v_sc_w_in, v_sc_conv_w, v_sc_w_out, v_rg_w_in, v_rg_conv_w, v_rg_conv_b, v_rg_w_gate_a, v_rg_b_gate_a, v_rg_w_gate_x, v_rg_b_gate_x, v_rg_lambda, v_rg_w_out, v_ffn_w_up, v_ffn_conv_w, v_ffn_w_down):
    weights = dict(meta_tokens=meta_tokens, norm_mix_g=norm_mix_g, norm_ffn_g=norm_ffn_g, final_norm_g=final_norm_g, sc_w_in=sc_w_in, sc_conv_w=sc_conv_w, sc_w_out=sc_w_out, rg_w_in=rg_w_in, rg_conv_w=rg_conv_w, rg_conv_b=rg_conv_b, rg_w_gate_a=rg_w_gate_a, rg_b_gate_a=rg_b_gate_a, rg_w_gate_x=rg_w_gate_x, rg_b_gate_x=rg_b_gate_x, rg_lambda=rg_lambda, rg_w_out=rg_w_out, ffn_w_up=ffn_w_up, ffn_conv_w=ffn_conv_w, ffn_w_down=ffn_w_down)
    m_in = dict(meta_tokens=m_meta_tokens, norm_mix_g=m_norm_mix_g, norm_ffn_g=m_norm_ffn_g, final_norm_g=m_final_norm_g, sc_w_in=m_sc_w_in, sc_conv_w=m_sc_conv_w, sc_w_out=m_sc_w_out, rg_w_in=m_rg_w_in, rg_conv_w=m_rg_conv_w, rg_conv_b=m_rg_conv_b, rg_w_gate_a=m_rg_w_gate_a, rg_b_gate_a=m_rg_b_gate_a, rg_w_gate_x=m_rg_w_gate_x, rg_b_gate_x=m_rg_b_gate_x, rg_lambda=m_rg_lambda, rg_w_out=m_rg_w_out, ffn_w_up=m_ffn_w_up, ffn_conv_w=m_ffn_conv_w, ffn_w_down=m_ffn_w_down)
    v_in = dict(meta_tokens=v_meta_tokens, norm_mix_g=v_norm_mix_g, norm_ffn_g=v_norm_ffn_g, final_norm_g=v_final_norm_g, sc_w_in=v_sc_w_in, sc_conv_w=v_sc_conv_w, sc_w_out=v_sc_w_out, rg_w_in=v_rg_w_in, rg_conv_w=v_rg_conv_w, rg_conv_b=v_rg_conv_b, rg_w_gate_a=v_rg_w_gate_a, rg_b_gate_a=v_rg_b_gate_a, rg_w_gate_x=v_rg_w_gate_x, rg_b_gate_x=v_rg_b_gate_x, rg_lambda=v_rg_lambda, rg_w_out=v_rg_w_out, ffn_w_up=v_ffn_w_up, ffn_conv_w=v_ffn_conv_w, ffn_w_down=v_ffn_w_down)
    names = list(weights)

    seq, d = x.shape[1:]
    tm = _row_tile(seq)
    tokens, target = _tile_order(x[0], tm), _tile_order(loss_target[0], tm)
    t_len = seq + tm
    wg_rows = 5 * tm if t_len % (5 * tm) == 0 else tm
    wg_rows_in = 13 * tm if t_len % (13 * tm) == 0 else wg_rows
    xi, yi, _ = _place()
    chip = 2 * xi + yi
    mesh_axes = ("x", "y", "c")

    wire = lambda w: w.astype(WIRE_DT)
    small_sharded = ["meta_tokens", "sc_conv_w", "rg_conv_w", "rg_conv_b", "rg_b_gate_a", "rg_b_gate_x", "rg_lambda", "ffn_conv_w"]
    small_2d = {n: weights[n].reshape(-1, weights[n].shape[-1]) for n in small_sharded}
    w_sc_in, w_sc_out, small_by_chip = _exchange(
        _GatherHalves([wire(sc_w_in[0]), wire(sc_w_out[0]), _pack([small_2d[n] for n in small_sharded], 2 * ACT_ROWS)]),
        "gather_first")
    w_sc_out = w_sc_out.reshape(-1, d)
    gather_ffn0 = _Gather([wire(ffn_w_up[0]), wire(ffn_w_down[0])])
    gather_rest = _Gather([wire(rg_w_in[0]), wire(rg_w_out[0]), wire(ffn_w_up[1]), wire(ffn_w_down[1])])
    small_len = sum(a.size for a in small_2d.values())
    by_chip = small_by_chip.reshape(N_CHIPS, -1)[:, :small_len]
    full, off = {}, 0
    for n in small_sharded:
        rows, width = small_2d[n].shape
        full[n] = by_chip[:, off:off + rows * width].reshape(N_CHIPS, rows, width).transpose(1, 0, 2).reshape(rows, N_CHIPS * width)
        off += rows * width
    sc_cw, rg_cw = full["sc_conv_w"], full["rg_conv_w"]
    ffn_cw = [full["ffn_conv_w"][0:3], full["ffn_conv_w"][3:6]]
    d_rnn = rg_cw.shape[1]
    vec = jnp.concatenate([full["rg_conv_b"], full["rg_b_gate_a"], full["rg_b_gate_x"], full["rg_lambda"],
                           jnp.zeros((F32_ROWS - 4, d_rnn), F32)])
    wa, wx = rg_w_gate_a[0].astype(MXU_DT), rg_w_gate_x[0].astype(MXU_DT)
    first = _tile_order(jnp.concatenate([jnp.zeros((tm - N_META, d), F32), full["meta_tokens"]]), tm)
    g_mix = [norm_mix_g[0:1], norm_mix_g[1:2]]
    g_ffn = [norm_ffn_g[0:1], norm_ffn_g[1:2]]

    h1, hh0, w_up0, w_dn0 = _sc_fwd(tokens, first, g_mix[0], w_sc_in, sc_cw, w_sc_out, tm=tm, ride=gather_ffn0)
    h2, hu0, w_rg_in, w_rg_out, w_up1, w_dn1 = _ffn_fwd(h1, g_ffn[0], w_up0, ffn_cw[0], w_dn0.reshape(-1, d), tm=tm,
                                                         ride=gather_rest)
    w_up, w_dn, w_rg_out = [w_up0, w_up1], [w_dn0.reshape(-1, d), w_dn1.reshape(-1, d)], w_rg_out.reshape(-1, d)
    h3, hh1, hs, gates = _rg_fwd(h2, g_mix[1], w_rg_in, rg_cw, vec, wa, wx, w_rg_out, tm=tm)
    dh4, hu1, sq, d_final = _ffn_fwd(h3, g_ffn[1], w_up[1], ffn_cw[1], w_dn[1], tm=tm,
                                     loss=(target, final_norm_g.reshape(1, d)))
    loss = lax.psum(jnp.sum(sq[0]) * (0.5 / d), mesh_axes)

    def by_chip_rows(pair):
        return [p.reshape(N_CHIPS, -1, d) for p in pair]

    def ffn_backward(dh_out, h_in, hu, layer, ride):
        dh_in, act, dhu, hn, dcw, dg, *landed = _ffn_bwd(dh_out, hu, h_in, g_ffn[layer], w_up[layer], ffn_cw[layer],
                                                         w_dn[layer], tm=tm, ride=ride)
        d_up = _weight_grad(hn, dhu, N_CHIPS, rows=wg_rows_in)
        d_dn = by_chip_rows(_weight_grad(act, dh_out, 1, rows=wg_rows))
        return dh_in, d_up, d_dn, dcw[0:3], dg[0], landed

    dh3, d_up1, d_dn1, d_fcw1, d_gf1, _ = ffn_backward(dh4, h3, hu1, 1, None)
    dh2, dhh1, y_rg, hn_rg, d_vec, d_wa, d_wx, d_gm1, *landed_ffn1 = _rg_bwd(
        dh3, hh1, hs, gates, h2, g_mix[1], w_rg_in, rg_cw, vec, wa, wx, w_rg_out, tm=tm,
        ride=_Scatter([d_up1[1], d_dn1[1]]))
    d_rg_in = _weight_grad(hn_rg, dhh1, N_CHIPS, rows=wg_rows_in)
    d_rg_out = by_chip_rows(_weight_grad(y_rg, dh3, 1, rows=wg_rows))
    early = {"rg_conv_w": d_vec[G_CONV_W:G_CONV_W + 4], "rg_conv_b": d_vec[G_CONV_B:G_CONV_B + 1],
             "rg_b_gate_a": d_vec[G_B_A:G_B_A + 1], "rg_b_gate_x": d_vec[G_B_X:G_B_X + 1],
             "rg_lambda": d_vec[G_LAMBDA:G_LAMBDA + 1], "ffn_conv_w.1": d_fcw1, "norm_mix_g.1": d_gm1[0:1],
             "norm_ffn_g.1": d_gf1[None], "final_norm_g": d_final[0]}
    early_packed = _pack(list(early.values()))
    gate_names = ["rg_w_gate_a", "rg_w_gate_x"]
    to_all = _AllDevices([early_packed, d_wa.reshape(-1, LANES), d_wx.reshape(-1, LANES)])
    dh1, d_up0, d_dn0, d_fcw0, d_gf0, landed = ffn_backward(
        dh2, h1, hu0, 0, _Both(_Scatter([d_rg_in[1], d_rg_out[1]]), to_all))
    landed_rg, early_by_device, gates_by_device = landed[0:2], landed[2], landed[3:]

    def core_sum(pair, received):
        own = lax.dynamic_index_in_dim(pair[0], chip, 0, keepdims=False)
        return _sum_parts(own, received, rows=_divisor_rows(own.shape[0]))

    early_big = [("rg_w_in", 0), ("rg_w_out", 0), ("ffn_w_up", 1), ("ffn_w_down", 1)]
    early_sum = [core_sum(d_rg_in, landed_rg[0]), core_sum(d_rg_out, landed_rg[1]), core_sum(d_up1, landed_ffn1[0]),
                 core_sum(d_dn1, landed_ffn1[1])]
    grad_x, dhh0, z_sc, hn_sc, d_sccw, d_gm0, d_first, *landed = _sc_bwd(
        dh1, hh0, tokens, first, g_mix[0], w_sc_in, sc_cw, w_sc_out, tm=tm,
        ride=_Both(_Scatter([d_up0[1], d_dn0[1]]), _Swap(early_sum)))
    landed_ffn0, early_other = landed[0:2], landed[2:]
    d_sc_in = _weight_grad(hn_sc, dhh0, N_CHIPS, rows=wg_rows_in)
    *d_sc_out, landed_sc_in = _weight_grad(z_sc, dh1, 1, rows=wg_rows, ride=_Scatter([d_sc_in[1]]))
    d_sc_out = by_chip_rows(d_sc_out)
    landed_sc = [landed_sc_in, *_exchange(_Scatter([d_sc_out[1]]), "scatter_last")]
    grad_x = _time_order(grad_x, tm)[None]

    late_big = [("sc_w_in", 0), ("sc_w_out", 0), ("ffn_w_up", 0), ("ffn_w_down", 0)]
    late_sum = [core_sum(d_sc_in, landed_sc[0]), core_sum(d_sc_out, landed_sc[1]), core_sum(d_up0, landed_ffn0[0]),
                core_sum(d_dn0, landed_ffn0[1])]
    late_other = _exchange(_Swap(late_sum), "swap_cores")
    out = {k: {} for k in ("grad", "delta", "m", "v")}
    stacked = {}
    for (n, layer), mine, theirs in zip(late_big + early_big, late_sum + early_sum, list(late_other) + list(early_other)):
        stacked[n] = _adamw(weights[n], m_in[n], v_in[n], [mine, theirs], rows=_divisor_rows(mine.shape[0]), layer=layer,
                            into=stacked.get(n))
    for n, res in stacked.items():
        for k, key in enumerate(("grad", "delta", "m", "v")):
            out[key][n] = res[k]

    late = {"meta_tokens": _time_order(d_first, tm)[tm - N_META:], "sc_conv_w": d_sccw[0:3], "ffn_conv_w.0": d_fcw0,
            "norm_mix_g.0": d_gm0[0:1], "norm_ffn_g.0": d_gf0[None]}
    late_packed = _pack(list(late.values()))
    late_by_device, = _exchange(_AllDevices([late_packed]), "gather_devices")
    summed = {}
    for parts, packed, by_device in ((early, early_packed, early_by_device), (late, late_packed, late_by_device)):
        total = _sum_stack(by_device, rows=packed.shape[0])
        summed.update(zip(parts, _unpack(total, [p.shape for p in parts.values()])))
    for n in ("ffn_conv_w", "norm_mix_g", "norm_ffn_g"):
        summed[n] = jnp.concatenate([summed.pop(n + ".0"), summed.pop(n + ".1")])
    for n, by_device in zip(gate_names, gates_by_device):
        as_rows = lambda a: a.reshape(1, -1, LANES)
        res = _adamw(as_rows(weights[n]), as_rows(m_in[n]), as_rows(v_in[n]), [_sum_stack(by_device, rows=256)], rows=256)
        for k, key in enumerate(("grad", "delta", "m", "v")):
            out[key][n] = res[k].reshape(weights[n].shape)
    replicated = ["norm_mix_g", "norm_ffn_g", "final_norm_g"]
    small_names = small_sharded + replicated
    grads = {}
    for n in small_sharded:
        width = small_2d[n].shape[1]
        grads[n] = lax.dynamic_slice_in_dim(summed[n], chip * width, width, axis=1).reshape(weights[n].shape)
    for n in replicated:
        grads[n] = summed[n].reshape(weights[n].shape)
    shapes = [weights[n].shape for n in small_names]
    packed_w = _pack([weights[n] for n in small_names])
    res = _adamw(packed_w[None], _pack([m_in[n] for n in small_names])[None], _pack([v_in[n] for n in small_names])[None],
                 [_pack([grads[n] for n in small_names])], rows=packed_w.shape[0])
    for k, key in enumerate(("grad", "delta", "m", "v")):
        out[key].update(dict(zip(small_names, _unpack(res[k][0], shapes))))

    return (loss, grad_x, *[out["grad"][n] for n in names], *[out["delta"][n] for n in names],
            *[out["m"][n] for n in names], *[out["v"][n] for n in names])
```

```python
import functools

import jax
import jax.numpy as jnp
from jax import lax
from jax.experimental import pallas as pl
from jax.experimental.pallas import tpu as pltpu

F32 = jnp.float32
MXU_DT = jnp.bfloat16
ACT_DT = jnp.bfloat16
WIRE_DT = jnp.bfloat16
MESH_ID = pl.DeviceIdType.MESH

N_META = 16
RMS_EPS = 1e-6
RG_C = 8.0
ADAM_LR, ADAM_B1, ADAM_B2, ADAM_EPS, ADAM_WD, ADAM_STEP = 0.001, 0.9, 0.999, 1e-08, 0.01, 10
N_CHIPS = 4
VMEM_LIMIT = 60 * 1024 * 1024
F32_ROWS = 8
ACT_ROWS = 16
LANES = 128


def _row_tile(seq):
    for tm in (256, 128, 64, 32, 16):
        if seq % tm == 0:
            return tm
    raise ValueError(f"sequence length {seq} is not a multiple of 16")


def _params(n_axes=1, **kw):
    return pltpu.CompilerParams(dimension_semantics=("arbitrary",) * n_axes, vmem_limit_bytes=VMEM_LIMIT, **kw)


def _const(shape):
    return pl.BlockSpec(shape, lambda *_: (0,) * len(shape), pipeline_mode=pl.Buffered(1))


def _dot(a, b):
    return jnp.dot(a, b, preferred_element_type=F32)


def _dot_nt(a, b):
    return lax.dot_general(a, b, (((1,), (1,)), ((), ())), preferred_element_type=F32)


def _dot_tn(a, b):
    return lax.dot_general(a, b, (((0,), (0,)), ((), ())), preferred_element_type=F32)


def _sigmoid(x):
    return 0.5 + 0.5 * jnp.tanh(0.5 * x)


def _rms(h, g):
    rstd = lax.rsqrt(jnp.mean(h * h, axis=-1, keepdims=True) + RMS_EPS)
    xhat = h * rstd
    return xhat * g, xhat, rstd


def _rms_bwd(dhn, xhat, rstd, g):
    dx = dhn * g
    return rstd * (dx - xhat * jnp.mean(dx * xhat, axis=-1, keepdims=True))


def _gelu(x):
    k = 0.7978845608028654
    t = jnp.tanh(k * (x + 0.044715 * x * x * x))
    return 0.5 * x * (1.0 + t), t


def _gelu_grad(x, t):
    k = 0.7978845608028654
    return 0.5 * (1.0 + t) + 0.5 * x * (1.0 - t * t) * k * (1.0 + 3 * 0.044715 * x * x)


def _softplus(x):
    e = jnp.exp(-jnp.abs(x))
    return jnp.maximum(x, 0.0) + jnp.where(e < 1e-4, e - 0.5 * e * e, jnp.log(1.0 + e))


def _expm1_neg(z, exp_z):
    series = z * (1.0 + z * (0.5 + z * (1.0 / 6)))
    return jnp.where(z > -0.02, series, exp_z - 1.0)


def _tile_order(a, tm):
    return a.reshape(-1, F32_ROWS, tm // F32_ROWS, a.shape[-1]).swapaxes(1, 2).reshape(a.shape)


def _time_order(a, tm):
    return a.reshape(-1, tm // F32_ROWS, F32_ROWS, a.shape[-1]).swapaxes(1, 2).reshape(a.shape)


def _valid_rows(tile, tm):
    row = lax.broadcasted_iota(jnp.int32, (tm, 1), 0)
    time = (row & (F32_ROWS - 1)) * (tm // F32_ROWS) + (row >> 3) + tile * tm
    return time >= tm - N_META


def _sublane():
    return lax.broadcasted_iota(jnp.int32, (F32_ROWS, 1), 0)


def _past_rows(width):
    return (width - 1) * F32_ROWS


def _halo_block(past, tm, nt):
    rows = -(-past // ACT_ROWS) * ACT_ROWS
    return rows, lambda i: (jnp.maximum((nt - 1 - i) * (tm // rows) - 1, 0), 0)


def _link_past(buf, cols, width, tm):
    past = _past_rows(width)
    for k in range(1, width):
        rows = pl.ds(past - F32_ROWS * k, F32_ROWS)
        before = pltpu.roll(buf[rows, cols], 1, 0)
        mine = pltpu.roll(buf[pl.ds(past + tm - F32_ROWS * k, F32_ROWS), cols], 1, 0)
        buf[rows, cols] = jnp.where(_sublane() == 0, before, mine)


def _link_future(buf, cols, width, tm):
    for k in range(1, width):
        rows = pl.ds(tm + F32_ROWS * (k - 1), F32_ROWS)
        after = pltpu.roll(buf[rows, cols], F32_ROWS - 1, 0)
        mine = pltpu.roll(buf[pl.ds(F32_ROWS * (k - 1), F32_ROWS), cols], F32_ROWS - 1, 0)
        buf[rows, cols] = jnp.where(_sublane() == F32_ROWS - 1, after, mine)


def _conv_taps(buf, cols, width, tm):
    return [buf[pl.ds(F32_ROWS * k, tm), cols] for k in range(width)]


def _conv_back(buf, cw_ref, cols, width, tm):
    return sum(cw_ref[k:k + 1, cols] * buf[pl.ds(F32_ROWS * (width - 1 - k), tm), cols] for k in range(width))


ANY = pl.BlockSpec(memory_space=pl.ANY)


def _place():
    return lax.axis_index("x"), lax.axis_index("y"), lax.axis_index("c")


def _other_chips(x, y):
    return [(1 - x, y), (x, 1 - y), (1 - x, 1 - y)]


class _Gather:
    def __init__(self, shards):
        nk = len(shards)
        self.arrays = list(shards)
        self.out_shape = [jax.ShapeDtypeStruct((N_CHIPS,) + s.shape, s.dtype) for s in shards]
        self.scratch = [pltpu.SemaphoreType.DMA((nk, 3)), pltpu.SemaphoreType.DMA((nk, 3)), pltpu.SemaphoreType.DMA((nk,))]

    def run(self, ins, outs, sems, start):
        send_sems, recv_sems, local_sems = sems
        x, y, c = _place()
        mine = 2 * x + y
        for k in range(len(ins)):
            local = pltpu.make_async_copy(ins[k], outs[k].at[mine], local_sems.at[k])
            local.start() if start else local.wait()
            for j, (px, py) in enumerate(_other_chips(x, y)):
                sems_kj = dict(send_sem=send_sems.at[k, j], recv_sem=recv_sems.at[k, j], device_id=(px, py, c),
                               device_id_type=MESH_ID)
                send = pltpu.make_async_remote_copy(src_ref=ins[k], dst_ref=outs[k].at[mine], **sems_kj)
                if start:
                    send.start()
                else:
                    pltpu.make_async_remote_copy(src_ref=ins[k], dst_ref=outs[k].at[2 * px + py], **sems_kj).wait_recv()
                    send.wait_send()


class _GatherHalves:
    def __init__(self, shards):
        nk = len(shards)
        self.arrays = list(shards)
        self.out_shape = [jax.ShapeDtypeStruct((N_CHIPS,) + s.shape, s.dtype) for s in shards]
        self.scratch = [pltpu.SemaphoreType.DMA((nk, 3)) for _ in range(4)] + [pltpu.SemaphoreType.DMA((nk,))]

    def run(self, ins, outs, sems, start):
        far_send, far_recv, near_send, near_recv, local_sems = sems
        x, y, c = _place()
        mine = 2 * x + y
        for phase in ((0,) if start else (1, 2)):
            for k in range(len(ins)):
                half = ins[k].shape[0] // 2
                my_half = pl.ds(pl.multiple_of(c * half, ACT_ROWS), half)
                other_half = pl.ds(pl.multiple_of((1 - c) * half, ACT_ROWS), half)
                if phase != 1:
                    local = pltpu.make_async_copy(ins[k], outs[k].at[mine], local_sems.at[k])
                    local.start() if phase == 0 else local.wait()
                for j, (px, py) in enumerate(_other_chips(x, y)):
                    theirs = 2 * px + py
                    far = dict(send_sem=far_send.at[k, j], recv_sem=far_recv.at[k, j], device_id=(px, py, c),
                               device_id_type=MESH_ID)
                    near = dict(send_sem=near_send.at[k, j], recv_sem=near_recv.at[k, j], device_id=(x, y, 1 - c),
                                device_id_type=MESH_ID)
                    landed = outs[k].at[theirs, my_half]
                    send = lambda: pltpu.make_async_remote_copy(src_ref=ins[k].at[my_half], dst_ref=outs[k].at[mine, my_half], **far)
                    pass_on = lambda: pltpu.make_async_remote_copy(src_ref=landed, dst_ref=landed, **near)
                    if phase == 0:
                        send().start()
                    elif phase == 1:
                        pltpu.make_async_remote_copy(src_ref=ins[k].at[my_half], dst_ref=landed, **far).wait_recv()
                        pass_on().start()
                    else:
                        pltpu.make_async_remote_copy(src_ref=landed, dst_ref=outs[k].at[theirs, other_half], **near).wait_recv()
                        pass_on().wait_send()
                        send().wait_send()


class _Scatter:
    def __init__(self, parts):
        nk = len(parts)
        self.arrays = list(parts)
        self.out_shape = [jax.ShapeDtypeStruct((3,) + p.shape[1:], p.dtype) for p in parts]
        self.scratch = [pltpu.SemaphoreType.DMA((nk, 3)), pltpu.SemaphoreType.DMA((nk, 3))]

    def run(self, ins, outs, sems, start):
        send_sems, recv_sems = sems
        x, y, c = _place()
        for k in range(len(ins)):
            for j, (px, py) in enumerate(_other_chips(x, y)):
                send = pltpu.make_async_remote_copy(
                    src_ref=ins[k].at[2 * px + py], dst_ref=outs[k].at[j], send_sem=send_sems.at[k, j],
                    recv_sem=recv_sems.at[k, j], device_id=(px, py, c), device_id_type=MESH_ID)
                if start:
                    send.start()
                else:
                    send.wait_recv()
                    send.wait_send()


def _exchange(ride, name):
    n_in, n_out = len(ride.arrays), len(ride.out_shape)

    def body(*refs):
        ride.run(refs[:n_in], refs[n_in:n_in + n_out], refs[n_in + n_out:], start=True)
        ride.run(refs[:n_in], refs[n_in:n_in + n_out], refs[n_in + n_out:], start=False)

    return pl.pallas_call(body, name=name, in_specs=[ANY] * n_in, out_specs=[ANY] * n_out, out_shape=ride.out_shape,
                          scratch_shapes=ride.scratch)(*ride.arrays)


def _launch(body, operands, *, name, grid, in_specs, out_specs, out_shape, scratch_shapes=(), ride=None):
    common = dict(name=name, grid=grid, compiler_params=_params(len(grid)))
    if ride is None:
        return pl.pallas_call(body, in_specs=in_specs, out_specs=out_specs, out_shape=out_shape,
                              scratch_shapes=list(scratch_shapes), **common)(*operands)
    n_in, n_out, n_scr = len(operands), len(out_shape), len(scratch_shapes)
    r_in, r_out = len(ride.arrays), len(ride.out_shape)

    def riding(*refs):
        ins, refs = refs[:n_in], refs[n_in:]
        r_ins, refs = refs[:r_in], refs[r_in:]
        outs, refs = refs[:n_out], refs[n_out:]
        r_outs, refs = refs[:r_out], refs[r_out:]
        scr, r_sems = refs[:n_scr], refs[n_scr:]
        step = [pl.program_id(axis) for axis in range(len(grid))]
        first = functools.reduce(jnp.logical_and, [s == 0 for s in step])
        last = functools.reduce(jnp.logical_and, [s == size - 1 for s, size in zip(step, grid)])

        @pl.when(first)
        def _():
            ride.run(r_ins, r_outs, r_sems, start=True)

        body(*ins, *outs, *scr)

        @pl.when(last)
        def _():
            ride.run(r_ins, r_outs, r_sems, start=False)

    return pl.pallas_call(
        riding, in_specs=list(in_specs) + [ANY] * r_in, out_specs=list(out_specs) + [ANY] * r_out,
        out_shape=list(out_shape) + ride.out_shape, scratch_shapes=list(scratch_shapes) + ride.scratch, **common,
    )(*operands, *ride.arrays)


def _sc_fwd(x, first, g, w_in, cw, w_out, *, tm, ride=None):
    seq, d = x.shape
    nt = seq // tm + 1
    nq, _, n = w_in.shape
    width = cw.shape[0]
    past = _past_rows(width)

    def body(x_ref, first_ref, g_ref, win_ref, cw_ref, wout_ref, h1_ref, hh_ref, hh_scr, cbuf):
        i = pl.program_id(0)

        @pl.when(i == 0)
        def _():
            cbuf[pl.ds(0, past), :] = jnp.zeros((past, d), F32)

        h = jnp.where(i == 0, first_ref[...], x_ref[...])
        hn = _rms(h, g_ref[...])[0].astype(MXU_DT)
        for q in range(nq):
            hh_scr[:, q * n:(q + 1) * n] = _dot(hn, win_ref[q])
        hh_ref[...] = hh_scr[...].astype(hh_ref.dtype)
        b = hh_scr[:, 0:d]
        cbuf[pl.ds(past, tm), :] = hh_scr[:, d:2 * d] * hh_scr[:, 2 * d:3 * d]
        last = cbuf[pl.ds(tm, past), :]
        _link_past(cbuf, slice(None), width, tm)
        u = sum(cw_ref[k:k + 1, :] * tap for k, tap in enumerate(_conv_taps(cbuf, slice(None), width, tm)))
        cbuf[pl.ds(0, past), :] = last
        h1_ref[...] = h + _dot((b * u).astype(MXU_DT), wout_ref[...])

    return _launch(
        body, [x, first, g, w_in, cw, w_out], name="sc_fwd", grid=(nt,),
        in_specs=[pl.BlockSpec((tm, d), lambda i: (jnp.maximum(i - 1, 0), 0)), _const((tm, d)), _const((1, d)),
                  _const(w_in.shape), _const(cw.shape), _const(w_out.shape)],
        out_specs=[pl.BlockSpec((tm, d), lambda i: (i, 0)), pl.BlockSpec((tm, nq * n), lambda i: (i, 0))],
        out_shape=[jax.ShapeDtypeStruct((nt * tm, d), F32), jax.ShapeDtypeStruct((nt * tm, nq * n), ACT_DT)],
        scratch_shapes=[pltpu.VMEM((tm, nq * n), F32), pltpu.VMEM((past + tm, d), F32)],
        ride=ride,
    )


def _sc_bwd(dh, hh, x, first, g, w_in, cw, w_out, *, tm, ride=None):
    t_len, d = dh.shape
    nt = t_len // tm
    nq, _, n = w_in.shape
    width = cw.shape[0]
    past = _past_rows(width)
    halo_rows, halo_index = _halo_block(past, tm, nt)

    def body(dh_ref, hh_ref, hhp_ref, x_ref, first_ref, g_ref, win_ref, cw_ref, wout_ref,
             dx_ref, dhh_ref, z_ref, hn_ref, dcw_ref, dg_ref, dfirst_ref, cbuf, dbuf):
        i = pl.program_id(0)
        r = nt - 1 - i

        @pl.when(i == 0)
        def _():
            dbuf[pl.ds(tm, past), :] = jnp.zeros((past, d), F32)
            dcw_ref[...] = jnp.zeros_like(dcw_ref)
            dg_ref[...] = jnp.zeros_like(dg_ref)

        dh_out = dh_ref[...]
        b = hh_ref[:, 0:d].astype(F32)
        c = hh_ref[:, d:2 * d].astype(F32)
        v = hh_ref[:, 2 * d:3 * d].astype(F32)
        prev = hhp_ref[...].astype(F32)[halo_rows - past:, :]
        cbuf[pl.ds(0, past), :] = jnp.where(r > 0, prev[:, d:2 * d] * prev[:, 2 * d:3 * d], 0.0)
        cbuf[pl.ds(past, tm), :] = c * v
        _link_past(cbuf, slice(None), width, tm)
        taps = _conv_taps(cbuf, slice(None), width, tm)
        u = sum(cw_ref[k:k + 1, :] * taps[k] for k in range(width))
        z_ref[...] = (b * u).astype(z_ref.dtype)
        dz = _dot_nt(dh_out.astype(MXU_DT), wout_ref[...])
        dhh_ref[:, 0:d] = (dz * u).astype(dhh_ref.dtype)
        du = dz * b
        for k in range(width):
            dcw_ref[k:k + 1, :] += jnp.sum(taps[k] * du, axis=0, keepdims=True)
        dbuf[pl.ds(0, tm), :] = du
        _link_future(dbuf, slice(None), width, tm)
        dcv = _conv_back(dbuf, cw_ref, slice(None), width, tm)
        dbuf[pl.ds(tm, past), :] = dbuf[pl.ds(0, past), :]
        dhh_ref[:, d:2 * d] = (dcv * v).astype(dhh_ref.dtype)
        dhh_ref[:, 2 * d:3 * d] = (dcv * c).astype(dhh_ref.dtype)
        parts = [_dot_nt(dhh_ref[:, q * n:(q + 1) * n], win_ref[q]) for q in range(nq)]
        dhn = functools.reduce(lambda a, b: a + b, parts)
        h_in = jnp.where(r == 0, first_ref[...], x_ref[...])
        dh_in = _norm_bwd_tile(dhn, h_in, dh_out, g_ref[...], _valid_rows(r, tm), hn_ref, dg_ref)

        @pl.when(r == 0)
        def _():
            dfirst_ref[...] = dh_in

        @pl.when(r > 0)
        def _():
            dx_ref[...] = dh_in

    rev = lambda i: (nt - 1 - i, 0)
    rev_x = lambda i: (jnp.maximum(nt - 2 - i, 0), 0)
    return _launch(
        body, [dh, hh, hh, x, first, g, w_in, cw, w_out], name="sc_bwd", grid=(nt,),
        in_specs=[pl.BlockSpec((tm, d), rev), pl.BlockSpec((tm, 3 * d), rev), pl.BlockSpec((halo_rows, 3 * d), halo_index),
                  pl.BlockSpec((tm, d), rev_x), _const((tm, d)), _const((1, d)), _const(w_in.shape), _const(cw.shape),
                  _const(w_out.shape)],
        out_specs=[pl.BlockSpec((tm, d), rev_x), pl.BlockSpec((tm, 3 * d), rev), pl.BlockSpec((tm, d), rev),
                   pl.BlockSpec((tm, d), rev), _const((F32_ROWS, d)), _const((F32_ROWS, d)), _const((tm, d))],
        out_shape=[jax.ShapeDtypeStruct((t_len - tm, d), F32), jax.ShapeDtypeStruct((t_len, 3 * d), ACT_DT),
                   jax.ShapeDtypeStruct((t_len, d), ACT_DT), jax.ShapeDtypeStruct((t_len, d), ACT_DT),
                   jax.ShapeDtypeStruct((F32_ROWS, d), F32), jax.ShapeDtypeStruct((F32_ROWS, d), F32),
                   jax.ShapeDtypeStruct((tm, d), F32)],
        scratch_shapes=[pltpu.VMEM((past + tm, d), F32), pltpu.VMEM((tm + past, d), F32)],
        ride=ride,
    )


def _ffn_fwd(h, g, w_up, cw, w_down, *, tm, ride=None, loss=None):
    t_len, d = h.shape
    nt = t_len // tm
    nq, _, n = w_up.shape
    width = cw.shape[0]
    past = _past_rows(width)

    def body(h_ref, g_ref, wup_ref, cw_ref, wdn_ref, *rest):
        if loss is None:
            out_ref, hu_ref, ubuf, tail = rest
        else:
            t_ref, gf_ref, out_ref, hu_ref, sq_ref, dgf_ref, ubuf, tail = rest
        i = pl.program_id(0)

        @pl.when(i == 0)
        def _():
            tail[...] = jnp.zeros_like(tail)

        h_in = h_ref[...]
        hn = _rms(h_in, g_ref[...])[0].astype(MXU_DT)
        ubuf[pl.ds(0, past), :] = tail[...]
        for q in range(nq):
            ubuf[pl.ds(past, tm), q * n:(q + 1) * n] = _dot(hn, wup_ref[q])
        hu_ref[...] = ubuf[pl.ds(past, tm), :].astype(hu_ref.dtype)
        tail[...] = ubuf[pl.ds(tm, past), :]
        _link_past(ubuf, slice(None), width, tm)
        acc = h_in
        for j in range(nq // 2):
            gcol, vcol = slice(j * n, (j + 1) * n), slice((nq // 2 + j) * n, (nq // 2 + j + 1) * n)
            conv = lambda cols: sum(cw_ref[k:k + 1, cols] * tap for k, tap in enumerate(_conv_taps(ubuf, cols, width, tm)))
            gj, vj = conv(gcol), conv(vcol)
            acc = acc + _dot((gj * _sigmoid(gj) * vj).astype(MXU_DT), wdn_ref[j * n:(j + 1) * n, :])
        if loss is None:
            out_ref[...] = acc
            return

        @pl.when(i == 0)
        def _():
            sq_ref[...] = jnp.zeros_like(sq_ref)
            dgf_ref[...] = jnp.zeros_like(dgf_ref)
            out_ref[...] = jnp.zeros_like(out_ref)

        @pl.when(i > 0)
        def _():
            gain = gf_ref[...]
            out, xhat, rstd = _rms(acc, gain)
            err = out - t_ref[...]
            sq_ref[0:1, :] += jnp.sum(err * err, axis=0, keepdims=True)
            dout = err * (1.0 / d)
            dgf_ref[0:1, :] += jnp.sum(dout * xhat, axis=0, keepdims=True)
            out_ref[...] = _rms_bwd(dout, xhat, rstd, gain)

    row = lambda i: (i, 0)
    stat = jax.ShapeDtypeStruct((F32_ROWS, d), F32)
    return _launch(
        body, [h, g, w_up, cw, w_down] + list(loss or ()), name="ffn_fwd", grid=(nt,),
        in_specs=[pl.BlockSpec((tm, d), row), _const((1, d)), _const(w_up.shape), _const(cw.shape), _const(w_down.shape)]
        + ([pl.BlockSpec((tm, d), lambda i: (jnp.maximum(i - 1, 0), 0)), _const((1, d))] if loss else []),
        out_specs=[pl.BlockSpec((tm, d), row), pl.BlockSpec((tm, nq * n), row)] + ([_const(stat.shape)] * 2 if loss else []),
        out_shape=[jax.ShapeDtypeStruct((t_len, d), F32), jax.ShapeDtypeStruct((t_len, nq * n), ACT_DT)]
        + ([stat, stat] if loss else []),
        scratch_shapes=[pltpu.VMEM((past + tm, nq * n), F32), pltpu.VMEM((past, nq * n), F32)],
        ride=ride,
    )


def _norm_bwd_tile(dhn, h_in, dh, gain, valid, hn_ref, dg_ref):
    hn, xhat, rstd = _rms(h_in, gain)
    hn_ref[...] = hn.astype(hn_ref.dtype)
    dg_ref[0:1, :] += jnp.sum(dhn * xhat, axis=0, keepdims=True)
    return jnp.where(valid, dh + _rms_bwd(dhn, xhat, rstd, gain), 0.0)


def _ffn_bwd(dh, hu, h, g, w_up, cw, w_down, *, tm, ride=None):
    t_len, d = dh.shape
    nt = t_len // tm
    ff = hu.shape[1]
    n = ff // 4
    width = cw.shape[0]
    past = _past_rows(width)
    halo_rows, halo_index = _halo_block(past, tm, nt)

    def body(dh_ref, hu_ref, hup_ref, h_ref, g_ref, wup_ref, cw_ref, wdn_ref,
             dhin_ref, a_ref, dhu_ref, hn_ref, dcw_ref, dg_ref, ubuf, dbuf, head):
        i = pl.program_id(0)
        r = nt - 1 - i

        @pl.when(i == 0)
        def _():
            head[...] = jnp.zeros_like(head)
            dcw_ref[...] = jnp.zeros_like(dcw_ref)
            dg_ref[...] = jnp.zeros_like(dg_ref)

        dh_out = dh_ref[...]
        dhb = dh_out.astype(MXU_DT)
        dhn_parts = []
        d_act = [_dot_nt(dhb, wdn_ref[j * n:(j + 1) * n, :]) for j in range(2)]
        for j in range(2):
            mine = slice(0, n), slice(n, 2 * n)
            full = slice(j * n, (j + 1) * n), slice((2 + j) * n, (3 + j) * n)
            for here, there in zip(mine, full):
                prev = hup_ref[:, there].astype(F32)[halo_rows - past:, :]
                ubuf[pl.ds(0, past), here] = jnp.where(r > 0, prev, 0.0)
                ubuf[pl.ds(past, tm), here] = hu_ref[:, there].astype(F32)
                dbuf[pl.ds(tm, past), here] = head[:, there]
            _link_past(ubuf, slice(None), width, tm)
            conv = lambda here, there: sum(cw_ref[k:k + 1, there] * tap
                                           for k, tap in enumerate(_conv_taps(ubuf, here, width, tm)))
            gj, vj = conv(mine[0], full[0]), conv(mine[1], full[1])
            sg = _sigmoid(gj)
            s = gj * sg
            a_ref[:, full[0]] = (s * vj).astype(a_ref.dtype)
            da = d_act[j]
            dbuf[pl.ds(0, tm), mine[1]] = da * s
            dbuf[pl.ds(0, tm), mine[0]] = da * vj * (sg * (1.0 + gj * (1.0 - sg)))
            for here, there in zip(mine, full):
                head[:, there] = dbuf[pl.ds(0, past), here]
            _link_future(dbuf, slice(None), width, tm)
            for here, there in zip(mine, full):
                dy = dbuf[pl.ds(0, tm), here]
                for k, tap in enumerate(_conv_taps(ubuf, here, width, tm)):
                    dcw_ref[k:k + 1, there] += jnp.sum(tap * dy, axis=0, keepdims=True)
                dhu = sum(cw_ref[k:k + 1, there] * dbuf[pl.ds(F32_ROWS * (width - 1 - k), tm), here]
                          for k in range(width)).astype(dhu_ref.dtype)
                dhu_ref[:, there] = dhu
                dhn_parts.append(_dot_nt(dhu, wup_ref[there.start // n]))
        dhn = (dhn_parts[0] + dhn_parts[1]) + (dhn_parts[2] + dhn_parts[3])
        dhin_ref[...] = _norm_bwd_tile(dhn, h_ref[...], dh_out, g_ref[...], _valid_rows(r, tm), hn_ref, dg_ref)

    rev = lambda i: (nt - 1 - i, 0)
    return _launch(
        body, [dh, hu, hu, h, g, w_up, cw, w_down], name="ffn_bwd", grid=(nt,),
        in_specs=[pl.BlockSpec((tm, d), rev), pl.BlockSpec((tm, ff), rev), pl.BlockSpec((halo_rows, ff), halo_index),
                  pl.BlockSpec((tm, d), rev), _const((1, d)), _const(w_up.shape), _const(cw.shape), _const(w_down.shape)],
        out_specs=[pl.BlockSpec((tm, d), rev), pl.BlockSpec((tm, 2 * n), rev), pl.BlockSpec((tm, ff), rev),
                   pl.BlockSpec((tm, d), rev), _const((F32_ROWS, ff)), _const((F32_ROWS, d))],
        out_shape=[jax.ShapeDtypeStruct((t_len, d), F32), jax.ShapeDtypeStruct((t_len, 2 * n), ACT_DT),
                   jax.ShapeDtypeStruct((t_len, ff), ACT_DT), jax.ShapeDtypeStruct((t_len, d), ACT_DT),
                   jax.ShapeDtypeStruct((F32_ROWS, ff), F32), jax.ShapeDtypeStruct((F32_ROWS, d), F32)],
        scratch_shapes=[pltpu.VMEM((past + tm, 2 * n), F32), pltpu.VMEM((tm + past, 2 * n), F32), pltpu.VMEM((past, ff), F32)],
        ride=ride,
    )


V_CONV_B, V_B_A, V_B_X, V_LAMBDA = 0, 1, 2, 3
G_CONV_W, G_CONV_B, G_B_A, G_B_X, G_LAMBDA = 0, 4, 5, 6, 7


def _scan(a_ref, b_ref, edge, tm, reverse):
    nj = tm // F32_ROWS
    order = range(nj - 1, -1, -1) if reverse else range(nj)
    slab = lambda ref, j: ref[pl.ds(F32_ROWS * j, F32_ROWS), :]
    a_run = b_run = None
    for j in order:
        a_j, b_j = slab(a_ref, j), slab(b_ref, j)
        if a_run is not None:
            b_j = b_j + a_j * b_run
            a_j = a_j * a_run
            b_ref[pl.ds(F32_ROWS * j, F32_ROWS), :] = b_j
            a_ref[pl.ds(F32_ROWS * j, F32_ROWS), :] = a_j
        a_run, b_run = a_j, b_j
    sub = _sublane()
    shift = 1
    while shift < F32_ROWS:
        amount = F32_ROWS - shift if reverse else shift
        keep = (sub < F32_ROWS - shift) if reverse else (sub >= shift)
        b_run = jnp.where(keep, b_run + a_run * pltpu.roll(b_run, amount, 0), b_run)
        a_run = jnp.where(keep, a_run * pltpu.roll(a_run, amount, 0), a_run)
        shift *= 2
    outer = edge[0:1, :] if reverse else edge[F32_ROWS - 1:F32_ROWS, :]
    ends = b_run + a_run * outer
    if reverse:
        carry = jnp.where(sub == F32_ROWS - 1, outer, pltpu.roll(ends, F32_ROWS - 1, 0))
    else:
        carry = jnp.where(sub == 0, outer, pltpu.roll(ends, 1, 0))
    for j in range(nj):
        b_ref[pl.ds(F32_ROWS * j, F32_ROWS), :] = slab(b_ref, j) + slab(a_ref, j) * carry
    return slab(b_ref, 0 if reverse else nj - 1)


def _rg_gates(u, vec_ref, wa_ref, wx_ref, pre_scr, nb, bd):
    ub = u.astype(MXU_DT)
    for k in range(nb):
        blk = slice(k * bd, (k + 1) * bd)
        pre_scr[0, :, blk] = _dot(ub[:, blk], wa_ref[k])
        pre_scr[1, :, blk] = _dot(ub[:, blk], wx_ref[k])
    r_gate = _sigmoid(pre_scr[0] + vec_ref[V_B_A:V_B_A + 1, :])
    i_gate = _sigmoid(pre_scr[1] + vec_ref[V_B_X:V_B_X + 1, :])
    return r_gate, i_gate


def _rg_decay(r_gate, vec_ref):
    sp = _softplus(-vec_ref[V_LAMBDA:V_LAMBDA + 1, :])
    log_a = -RG_C * r_gate * sp
    a = jnp.exp(log_a)
    one_minus_a2 = jnp.maximum(-_expm1_neg(2.0 * log_a, a * a), 1e-30)
    inv_mult = lax.rsqrt(one_minus_a2)
    return a, one_minus_a2 * inv_mult, inv_mult, sp


def _rg_fwd(h, g, w_in, cw, vec, wa, wx, w_out, *, tm):
    t_len, d = h.shape
    nt = t_len // tm
    nq, _, n = w_in.shape
    dr = 2 * n
    width = cw.shape[0]
    past = _past_rows(width)
    nb, bd, _ = wa.shape

    def body(h_ref, g_ref, win_ref, cw_ref, vec_ref, wa_ref, wx_ref, wout_ref, out_ref, hh_ref, hs_ref, gates_ref,
             gbuf, rbuf, pre_scr, tail, edge):
        i = pl.program_id(0)

        @pl.when(i == 0)
        def _():
            tail[...] = jnp.zeros_like(tail)
            edge[...] = jnp.zeros_like(edge)

        h_in = h_ref[...]
        hn = _rms(h_in, g_ref[...])[0].astype(MXU_DT)
        rbuf[pl.ds(0, past), :] = tail[...]
        for q in range(2):
            gbuf[:, q * n:(q + 1) * n] = _dot(hn, win_ref[q])
            rbuf[pl.ds(past, tm), q * n:(q + 1) * n] = _dot(hn, win_ref[2 + q])
        hh_ref[:, 0:dr] = gbuf[...].astype(hh_ref.dtype)
        hh_ref[:, dr:2 * dr] = rbuf[pl.ds(past, tm), :].astype(hh_ref.dtype)
        tail[...] = rbuf[pl.ds(tm, past), :]
        _link_past(rbuf, slice(None), width, tm)
        taps = _conv_taps(rbuf, slice(None), width, tm)
        u = sum(cw_ref[k:k + 1, :] * taps[k] for k in range(width)) + vec_ref[V_CONV_B:V_CONV_B + 1, :]
        r_gate, i_gate = _rg_gates(u, vec_ref, wa_ref, wx_ref, pre_scr, nb, bd)
        for k, kept in enumerate((u, r_gate, i_gate)):
            gates_ref[:, k * dr:(k + 1) * dr] = kept.astype(gates_ref.dtype)
        a, mult, _, _ = _rg_decay(r_gate, vec_ref)
        pre_scr[0] = a
        pre_scr[1] = jnp.where(_valid_rows(i, tm), mult * (i_gate * u), 0.0)
        edge[...] = _scan(pre_scr.at[0], pre_scr.at[1], edge[...], tm, reverse=False)
        hs = pre_scr[1]
        hs_ref[...] = hs
        y = hs * _gelu(gbuf[...])[0]
        out_ref[...] = h_in + _dot(y.astype(MXU_DT), wout_ref[...])

    row = lambda i: (i, 0)
    return pl.pallas_call(
        body, name="rg_fwd", grid=(nt,),
        in_specs=[pl.BlockSpec((tm, d), row), _const((1, d)), _const(w_in.shape), _const(cw.shape), _const(vec.shape),
                  _const(wa.shape), _const(wx.shape), _const(w_out.shape)],
        out_specs=[pl.BlockSpec((tm, d), row), pl.BlockSpec((tm, 2 * dr), row), pl.BlockSpec((tm, dr), row),
                   pl.BlockSpec((tm, 3 * dr), row)],
        out_shape=[jax.ShapeDtypeStruct((t_len, d), F32), jax.ShapeDtypeStruct((t_len, 2 * dr), ACT_DT),
                   jax.ShapeDtypeStruct((t_len, dr), F32), jax.ShapeDtypeStruct((t_len, 3 * dr), ACT_DT)],
        scratch_shapes=[pltpu.VMEM((tm, dr), F32), pltpu.VMEM((past + tm, dr), F32), pltpu.VMEM((2, tm, dr), F32),
                        pltpu.VMEM((past, dr), F32), pltpu.VMEM((F32_ROWS, dr), F32)],
        compiler_params=_params(),
    )(h, g, w_in, cw, vec, wa, wx, w_out)


def _rg_bwd(dh, hh, hs, gates, h, g, w_in, cw, vec, wa, wx, w_out, *, tm, ride=None):
    t_len, d = dh.shape
    nt = t_len // tm
    dr = hs.shape[1]
    n = w_in.shape[2]
    width = cw.shape[0]
    nb, bd, _ = wa.shape
    past = _past_rows(width)
    halo_rows, halo_index = _halo_block(past, tm, nt)
    one = F32_ROWS

    def body(dh_ref, hh_ref, hhp_ref, hs_ref, hsp_ref, gates_ref, h_ref, g_ref, win_ref, cw_ref, vec_ref, wa_ref, wx_ref, wout_ref,
             dhin_ref, dhh_ref, y_ref, hn_ref, dvec_ref, dwa_ref, dwx_ref, dg_ref, rbuf, dbuf, pre_scr, hbuf, abuf, edge):
        i = pl.program_id(0)
        r = nt - 1 - i

        @pl.when(i == 0)
        def _():
            dbuf[pl.ds(tm, past), :] = jnp.zeros((past, dr), F32)
            abuf[pl.ds(tm, one), :] = jnp.zeros((one, dr), F32)
            edge[...] = jnp.zeros_like(edge)
            dvec_ref[...] = jnp.zeros_like(dvec_ref)
            dwa_ref[...] = jnp.zeros_like(dwa_ref)
            dwx_ref[...] = jnp.zeros_like(dwx_ref)
            dg_ref[...] = jnp.zeros_like(dg_ref)

        dh_out = dh_ref[...]
        gb = hh_ref[:, 0:dr].astype(F32)
        prev = hhp_ref[...].astype(F32)[halo_rows - past:, dr:2 * dr]
        rbuf[pl.ds(0, past), :] = jnp.where(r > 0, prev, 0.0)
        rbuf[pl.ds(past, tm), :] = hh_ref[:, dr:2 * dr].astype(F32)
        _link_past(rbuf, slice(None), width, tm)
        taps = _conv_taps(rbuf, slice(None), width, tm)
        ub = gates_ref[:, 0:dr].astype(MXU_DT)
        u, r_gate, i_gate = (gates_ref[:, k * dr:(k + 1) * dr].astype(F32) for k in range(3))
        a, mult, inv_mult, sp = _rg_decay(r_gate, vec_ref)
        hs_t = hs_ref[...]
        hbuf[pl.ds(0, one), :] = jnp.where(r > 0, hsp_ref[...], 0.0)
        hbuf[pl.ds(one, tm), :] = hs_t
        _link_past(hbuf, slice(None), 2, tm)
        h_prev = hbuf[pl.ds(0, tm), :]
        gate, th = _gelu(gb)
        y_ref[...] = (hs_t * gate).astype(y_ref.dtype)
        dy = _dot_nt(dh_out.astype(MXU_DT), wout_ref[...])
        d_gb = (dy * hs_t * _gelu_grad(gb, th)).astype(dhh_ref.dtype)
        dhh_ref[:, 0:dr] = d_gb
        dhn = sum(_dot_nt(d_gb[:, q * n:(q + 1) * n], win_ref[q]) for q in range(2))
        abuf[pl.ds(0, tm), :] = a
        _link_future(abuf, slice(None), 2, tm)
        pre_scr[0] = abuf[pl.ds(one, tm), :]
        pre_scr[1] = dy * gate
        edge[...] = _scan(pre_scr.at[0], pre_scr.at[1], edge[...], tm, reverse=True)
        abuf[pl.ds(tm, one), :] = abuf[pl.ds(0, one), :]
        d_hs = pre_scr[1]
        d_b = jnp.where(_valid_rows(r, tm), d_hs, 0.0)
        d_iu = d_b * mult
        d_log_a = d_hs * h_prev * a - d_b * (i_gate * u) * (a * a) * inv_mult
        dvec_ref[G_LAMBDA:G_LAMBDA + 1, :] += jnp.sum(d_log_a * r_gate, axis=0, keepdims=True) * (-RG_C)
        d_pre_r = d_log_a * (-RG_C * sp) * r_gate * (1.0 - r_gate)
        d_pre_i = d_iu * u * i_gate * (1.0 - i_gate)
        dvec_ref[G_B_A:G_B_A + 1, :] += jnp.sum(d_pre_r, axis=0, keepdims=True)
        dvec_ref[G_B_X:G_B_X + 1, :] += jnp.sum(d_pre_i, axis=0, keepdims=True)
        dbuf[pl.ds(0, tm), :] = d_iu * i_gate
        d_pre_r = d_pre_r.astype(MXU_DT)
        d_pre_i = d_pre_i.astype(MXU_DT)
        for k in range(nb):
            blk = slice(k * bd, (k + 1) * bd)
            dwa_ref[k] += _dot_tn(ub[:, blk], d_pre_r[:, blk])
            dwx_ref[k] += _dot_tn(ub[:, blk], d_pre_i[:, blk])
            dbuf[pl.ds(0, tm), blk] += _dot_nt(d_pre_r[:, blk], wa_ref[k]) + _dot_nt(d_pre_i[:, blk], wx_ref[k])
        du = dbuf[pl.ds(0, tm), :]
        dvec_ref[G_CONV_B:G_CONV_B + 1, :] += jnp.sum(du, axis=0, keepdims=True)
        for k in range(width):
            dvec_ref[G_CONV_W + k:G_CONV_W + k + 1, :] += jnp.sum(taps[k] * du, axis=0, keepdims=True)
        _link_future(dbuf, slice(None), width, tm)
        d_rb = _conv_back(dbuf, cw_ref, slice(None), width, tm)
        dbuf[pl.ds(tm, past), :] = dbuf[pl.ds(0, past), :]
        d_rb = d_rb.astype(dhh_ref.dtype)
        dhh_ref[:, dr:2 * dr] = d_rb
        dhn = dhn + sum(_dot_nt(d_rb[:, q * n:(q + 1) * n], win_ref[2 + q]) for q in range(2))
        dhin_ref[...] = _norm_bwd_tile(dhn, h_ref[...], dh_out, g_ref[...], _valid_rows(r, tm), hn_ref, dg_ref)

        @pl.when(i == nt - 1)
        def _():
            lam = vec_ref[V_LAMBDA:V_LAMBDA + 1, :]
            dvec_ref[G_LAMBDA:G_LAMBDA + 1, :] = dvec_ref[G_LAMBDA:G_LAMBDA + 1, :] * (-_sigmoid(-lam))

    rev = lambda i: (nt - 1 - i, 0)
    return _launch(
        body, [dh, hh, hh, hs, hs, gates, h, g, w_in, cw, vec, wa, wx, w_out], name="rg_bwd", grid=(nt,),
        in_specs=[pl.BlockSpec((tm, d), rev), pl.BlockSpec((tm, 2 * dr), rev), pl.BlockSpec((halo_rows, 2 * dr), halo_index),
                  pl.BlockSpec((tm, dr), rev),
                  pl.BlockSpec((one, dr), lambda i: (jnp.maximum((nt - 1 - i) * (tm // one) - 1, 0), 0)),
                  pl.BlockSpec((tm, 3 * dr), rev), pl.BlockSpec((tm, d), rev), _const((1, d)), _const(w_in.shape),
                  _const(cw.shape), _const(vec.shape), _const(wa.shape), _const(wx.shape), _const(w_out.shape)],
        out_specs=[pl.BlockSpec((tm, d), rev), pl.BlockSpec((tm, 2 * dr), rev), pl.BlockSpec((tm, dr), rev),
                   pl.BlockSpec((tm, d), rev), _const((F32_ROWS, dr)), _const(wa.shape), _const(wx.shape),
                   _const((F32_ROWS, d))],
        out_shape=[jax.ShapeDtypeStruct((t_len, d), F32), jax.ShapeDtypeStruct((t_len, 2 * dr), ACT_DT),
                   jax.ShapeDtypeStruct((t_len, dr), ACT_DT), jax.ShapeDtypeStruct((t_len, d), ACT_DT),
                   jax.ShapeDtypeStruct((F32_ROWS, dr), F32), jax.ShapeDtypeStruct(wa.shape, F32),
                   jax.ShapeDtypeStruct(wx.shape, F32), jax.ShapeDtypeStruct((F32_ROWS, d), F32)],
        scratch_shapes=[pltpu.VMEM((past + tm, dr), F32), pltpu.VMEM((tm + past, dr), F32), pltpu.VMEM((2, tm, dr), F32),
                        pltpu.VMEM((one + tm, dr), F32), pltpu.VMEM((tm + one, dr), F32), pltpu.VMEM((F32_ROWS, dr), F32)],
        ride=ride,
    )


def _weight_grad(a, b, nb, *, rows, ride=None):
    t_len, k_dim = a.shape
    n = b.shape[1] // nb
    nt = t_len // rows

    def body(a_ref, b_ref, out_ref, wire_ref):
        @pl.when(pl.program_id(1) == 0)
        def _():
            out_ref[...] = jnp.zeros_like(out_ref)

        out_ref[0] += _dot_tn(a_ref[...].astype(MXU_DT), b_ref[...].astype(MXU_DT))

        @pl.when(pl.program_id(1) == nt - 1)
        def _():
            wire_ref[...] = out_ref[...].astype(wire_ref.dtype)

    block = pl.BlockSpec((1, k_dim, n), lambda j, i: (j, 0, 0))
    return _launch(
        body, [a, b], name="weight_grad", grid=(nb, nt),
        in_specs=[pl.BlockSpec((rows, k_dim), lambda j, i: (i, 0)), pl.BlockSpec((rows, n), lambda j, i: (i, j))],
        out_specs=[block, block],
        out_shape=[jax.ShapeDtypeStruct((nb, k_dim, n), F32), jax.ShapeDtypeStruct((nb, k_dim, n), WIRE_DT)],
        ride=ride,
    )


def _adamw(w, m, v, parts, *, rows, layer=0, into=None):
    n_layers, n_rows, n_cols = w.shape
    nt = n_rows // rows
    n_parts = len(parts)

    def body(w_ref, m_ref, v_ref, *rest):
        part_refs, (g_ref, d_ref, nm_ref, nv_ref) = rest[:n_parts], rest[-4:]
        w_ref, m_ref, v_ref, g_ref, d_ref, nm_ref, nv_ref = (r.at[0] for r in (w_ref, m_ref, v_ref, g_ref, d_ref, nm_ref, nv_ref))
        grad = part_refs[0][...].astype(F32)
        for p in part_refs[1:]:
            grad = grad + p[...].astype(F32)
        new_m = ADAM_B1 * m_ref[...] + (1.0 - ADAM_B1) * grad
        new_v = ADAM_B2 * v_ref[...] + (1.0 - ADAM_B2) * (grad * grad)
        m_hat = new_m / (1.0 - ADAM_B1 ** ADAM_STEP)
        v_hat = new_v / (1.0 - ADAM_B2 ** ADAM_STEP)
        g_ref[...] = grad
        d_ref[...] = -ADAM_LR * (m_hat / (jnp.sqrt(v_hat) + ADAM_EPS) + ADAM_WD * w_ref[...])
        nm_ref[...] = new_m
        nv_ref[...] = new_v

    spec = pl.BlockSpec((rows, n_cols), lambda i: (i, 0))
    layer_spec = pl.BlockSpec((1, rows, n_cols), lambda i: (layer, i, 0))
    into = list(into or [])
    return pl.pallas_call(
        body, name="adamw", grid=(nt,),
        in_specs=[layer_spec] * 3 + [spec] * n_parts + [ANY] * len(into), out_specs=[layer_spec] * 4,
        out_shape=[jax.ShapeDtypeStruct(w.shape, F32)] * 4,
        input_output_aliases={3 + n_parts + k: k for k in range(len(into))},
        compiler_params=_params(),
    )(w, m, v, *parts, *into)


def _sum_stack(stack, *, rows):
    n_stack, n_rows, n_cols = stack.shape

    def body(stack_ref, out_ref):
        acc = stack_ref[0]
        for j in range(1, n_stack):
            acc = acc + stack_ref[j]
        out_ref[...] = acc

    return pl.pallas_call(
        body, name="sum_stack", grid=(n_rows // rows,),
        in_specs=[pl.BlockSpec((n_stack, rows, n_cols), lambda i: (0, i, 0))],
        out_specs=pl.BlockSpec((rows, n_cols), lambda i: (i, 0)),
        out_shape=jax.ShapeDtypeStruct((n_rows, n_cols), F32),
        compiler_params=_params(),
    )(stack)


def _sum_parts(own, recv, *, rows):
    n_rows, n_cols = own.shape
    n_recv = recv.shape[0]

    def body(own_ref, recv_ref, out_ref):
        acc = own_ref[...].astype(F32)
        for j in range(n_recv):
            acc = acc + recv_ref[j].astype(F32)
        out_ref[...] = acc

    return pl.pallas_call(
        body, name="sum_parts", grid=(n_rows // rows,),
        in_specs=[pl.BlockSpec((rows, n_cols), lambda i: (i, 0)), pl.BlockSpec((n_recv, rows, n_cols), lambda i: (0, i, 0))],
        out_specs=pl.BlockSpec((rows, n_cols), lambda i: (i, 0)),
        out_shape=jax.ShapeDtypeStruct(own.shape, F32),
        compiler_params=_params(),
    )(own, recv)


class _Swap:
    def __init__(self, arrays):
        nk = len(arrays)
        self.arrays = list(arrays)
        self.out_shape = [jax.ShapeDtypeStruct(a.shape, a.dtype) for a in arrays]
        self.scratch = [pltpu.SemaphoreType.DMA((nk,)), pltpu.SemaphoreType.DMA((nk,))]

    def run(self, ins, outs, sems, start):
        send_sems, recv_sems = sems
        x, y, c = _place()
        for k in range(len(ins)):
            send = pltpu.make_async_remote_copy(src_ref=ins[k], dst_ref=outs[k], send_sem=send_sems.at[k],
                                                recv_sem=recv_sems.at[k], device_id=(x, y, 1 - c), device_id_type=MESH_ID)
            if start:
                send.start()
            else:
                send.wait_recv()
                send.wait_send()


class _AllDevices:
    def __init__(self, arrays):
        nk = len(arrays)
        self.arrays = list(arrays)
        self.out_shape = [jax.ShapeDtypeStruct((8,) + a.shape, a.dtype) for a in arrays]
        self.scratch = [pltpu.SemaphoreType.DMA((nk, 7)), pltpu.SemaphoreType.DMA((nk, 7)), pltpu.SemaphoreType.DMA((nk,))]

    def run(self, ins, outs, sems, start):
        send_sems, recv_sems, local_sems = sems
        x, y, c = _place()
        mine = 4 * x + 2 * y + c
        for k in range(len(ins)):
            local = pltpu.make_async_copy(ins[k], outs[k].at[mine], local_sems.at[k])
            local.start() if start else local.wait()
            for flip in range(1, 8):
                px, py, pc = x ^ (flip >> 2), y ^ ((flip >> 1) & 1), c ^ (flip & 1)
                sems_f = dict(send_sem=send_sems.at[k, flip - 1], recv_sem=recv_sems.at[k, flip - 1],
                              device_id=(px, py, pc), device_id_type=MESH_ID)
                send = pltpu.make_async_remote_copy(src_ref=ins[k], dst_ref=outs[k].at[mine], **sems_f)
                if start:
                    send.start()
                else:
                    pltpu.make_async_remote_copy(src_ref=ins[k], dst_ref=outs[k].at[4 * px + 2 * py + pc], **sems_f).wait_recv()
                    send.wait_send()


class _Both:
    def __init__(self, first, second):
        self.rides = (first, second)
        self.arrays = first.arrays + second.arrays
        self.out_shape = first.out_shape + second.out_shape
        self.scratch = first.scratch + second.scratch

    def run(self, ins, outs, sems, start):
        for ride in self.rides:
            n_in, n_out, n_sem = len(ride.arrays), len(ride.out_shape), len(ride.scratch)
            ride.run(ins[:n_in], outs[:n_out], sems[:n_sem], start)
            ins, outs, sems = ins[n_in:], outs[n_out:], sems[n_sem:]


def _pack(arrays, pad_rows=F32_ROWS):
    flat = jnp.concatenate([a.reshape(-1).astype(F32) for a in arrays])
    rows = -(-flat.shape[0] // (LANES * pad_rows)) * pad_rows
    return jnp.pad(flat, (0, rows * LANES - flat.shape[0])).reshape(rows, LANES)


def _unpack(packed, shapes):
    flat, out, off = packed.reshape(-1), [], 0
    for s in shapes:
        size = 1
        for dim in s:
            size *= dim
        out.append(flat[off:off + size].reshape(s))
        off += size
    return out


def _divisor_rows(n_rows, most=256):
    best = None
    for r in range(ACT_ROWS, most + 1, ACT_ROWS):
        if n_rows % r == 0:
            best = r
    return best or n_rows


def kernel(x, meta_tokens, norm_mix_g, norm_ffn_g, final_norm_g, sc_w_in, sc_conv_w, sc_w_out, rg_w_in, rg_conv_w, rg_conv_b, rg_w_gate_a, rg_b_gate_a, rg_w_gate_x, rg_b_gate_x, rg_lambda, rg_w_out, ffn_w_up, ffn_conv_w, ffn_w_down, loss_target, m_meta_tokens, m_norm_mix_g, m_norm_ffn_g, m_final_norm_g, m_sc_w_in, m_sc_conv_w, m_sc_w_out, m_rg_w_in, m_rg_conv_w, m_rg_conv_b, m_rg_w_gate_a, m_rg_b_gate_a, m_rg_w_gate_x, m_rg_b_gate_x, m_rg_lambda, m_rg_w_out, m_ffn_w_up, m_ffn_conv_w, m_ffn_w_down, v_meta_tokens, v_norm_mix_g, v_norm_ffn_g, v_final_norm_g, v_sc_w_in, v_sc_conv_w, v_sc_w_out, v_rg_w_in, v_rg_conv_w, v_rg_conv_b, v_rg_w_gate_a, v_rg_b_gate_a, v_rg_w_gate_x, v_rg_b_gate_x, v_rg_lambda, v_rg_w_out, v_ffn_w_up, v_ffn_conv_w, v_ffn_w_down):
    weights = dict(meta_tokens=meta_tokens, norm_mix_g=norm_mix_g, norm_ffn_g=norm_ffn_g, final_norm_g=final_norm_g, sc_w_in=sc_w_in, sc_conv_w=sc_conv_w, sc_w_out=sc_w_out, rg_w_in=rg_w_in, rg_conv_w=rg_conv_w, rg_conv_b=rg_conv_b, rg_w_gate_a=rg_w_gate_a, rg_b_gate_a=rg_b_gate_a, rg_w_gate_x=rg_w_gate_x, rg_b_gate_x=rg_b_gate_x, rg_lambda=rg_lambda, rg_w_out=rg_w_out, ffn_w_up=ffn_w_up, ffn_conv_w=ffn_conv_w, ffn_w_down=ffn_w_down)
    m_in = dict(meta_tokens=m_meta_tokens, norm_mix_g=m_norm_mix_g, norm_ffn_g=m_norm_ffn_g, final_norm_g=m_final_norm_g, sc_w_in=m_sc_w_in, sc_conv_w=m_sc_conv_w, sc_w_out=m_sc_w_out, rg_w_in=m_rg_w_in, rg_conv_w=m_rg_conv_w, rg_conv_b=m_rg_conv_b, rg_w_gate_a=m_rg_w_gate_a, rg_b_gate_a=m_rg_b_gate_a, rg_w_gate_x=m_rg_w_gate_x, rg_b_gate_x=m_rg_b_gate_x, rg_lambda=m_rg_lambda, rg_w_out=m_rg_w_out, ffn_w_up=m_ffn_w_up, ffn_conv_w=m_ffn_conv_w, ffn_w_down=m_ffn_w_down)
    v_in = dict(meta_tokens=v_meta_tokens, norm_mix_g=v_norm_mix_g, norm_ffn_g=v_norm_ffn_g, final_norm_g=v_final_norm_g, sc_w_in=v_sc_w_in, sc_conv_w=v_sc_conv_w, sc_w_out=v_sc_w_out, rg_w_in=v_rg_w_in, rg_conv_w=v_rg_conv_w, rg_conv_b=v_rg_conv_b, rg_w_gate_a=v_rg_w_gate_a, rg_b_gate_a=v_rg_b_gate_a, rg_w_gate_x=v_rg_w_gate_x, rg_b_gate_x=v_rg_b_gate_x, rg_lambda=v_rg_lambda, rg_w_out=v_rg_w_out, ffn_w_up=v_ffn_w_up, ffn_conv_w=v_ffn_conv_w, ffn_w_down=v_ffn_w_down)
    names = list(weights)

    seq, d = x.shape[1:]
    tm = _row_tile(seq)
    tokens, target = _tile_order(x[0], tm), _tile_order(loss_target[0], tm)
    t_len = seq + tm
    wg_rows = 5 * tm if t_len % (5 * tm) == 0 else tm
    wg_rows_in = 13 * tm if t_len % (13 * tm) == 0 else wg_rows
    xi, yi, _ = _place()
    chip = 2 * xi + yi
    mesh_axes = ("x", "y", "c")

    wire = lambda w: w.astype(WIRE_DT)
    small_sharded = ["meta_tokens", "sc_conv_w", "rg_conv_w", "rg_conv_b", "rg_b_gate_a", "rg_b_gate_x", "rg_lambda", "ffn_conv_w"]
    small_2d = {n: weights[n].reshape(-1, weights[n].shape[-1]) for n in small_sharded}
    w_sc_in, w_sc_out, small_by_chip = _exchange(
        _GatherHalves([wire(sc_w_in[0]), wire(sc_w_out[0]), _pack([small_2d[n] for n in small_sharded], 2 * ACT_ROWS)]),
        "gather_first")
    w_sc_out = w_sc_out.reshape(-1, d)
    gather_ffn0 = _Gather([wire(ffn_w_up[0]), wire(ffn_w_down[0])])
    gather_rest = _Gather([wire(rg_w_in[0]), wire(rg_w_out[0]), wire(ffn_w_up[1]), wire(ffn_w_down[1])])
    small_len = sum(a.size for a in small_2d.values())
    by_chip = small_by_chip.reshape(N_CHIPS, -1)[:, :small_len]
    full, off = {}, 0
    for n in small_sharded:
        rows, width = small_2d[n].shape
        full[n] = by_chip[:, off:off + rows * width].reshape(N_CHIPS, rows, width).transpose(1, 0, 2).reshape(rows, N_CHIPS * width)
        off += rows * width
    sc_cw, rg_cw = full["sc_conv_w"], full["rg_conv_w"]
    ffn_cw = [full["ffn_conv_w"][0:3], full["ffn_conv_w"][3:6]]
    d_rnn = rg_cw.shape[1]
    vec = jnp.concatenate([full["rg_conv_b"], full["rg_b_gate_a"], full["rg_b_gate_x"], full["rg_lambda"],
                           jnp.zeros((F32_ROWS - 4, d_rnn), F32)])
    wa, wx = rg_w_gate_a[0].astype(MXU_DT), rg_w_gate_x[0].astype(MXU_DT)
    first = _tile_order(jnp.concatenate([jnp.zeros((tm - N_META, d), F32), full["meta_tokens"]]), tm)
    g_mix = [norm_mix_g[0:1], norm_mix_g[1:2]]
    g_ffn = [norm_ffn_g[0:1], norm_ffn_g[1:2]]

    h1, hh0, w_up0, w_dn0 = _sc_fwd(tokens, first, g_mix[0], w_sc_in, sc_cw, w_sc_out, tm=tm, ride=gather_ffn0)
    h2, hu0, w_rg_in, w_rg_out, w_up1, w_dn1 = _ffn_fwd(h1, g_ffn[0], w_up0, ffn_cw[0], w_dn0.reshape(-1, d), tm=tm,
                                                         ride=gather_rest)
    w_up, w_dn, w_rg_out = [w_up0, w_up1], [w_dn0.reshape(-1, d), w_dn1.reshape(-1, d)], w_rg_out.reshape(-1, d)
    h3, hh1, hs, gates = _rg_fwd(h2, g_mix[1], w_rg_in, rg_cw, vec, wa, wx, w_rg_out, tm=tm)
    dh4, hu1, sq, d_final = _ffn_fwd(h3, g_ffn[1], w_up[1], ffn_cw[1], w_dn[1], tm=tm,
                                     loss=(target, final_norm_g.reshape(1, d)))
    loss = lax.psum(jnp.sum(sq[0]) * (0.5 / d), mesh_axes)

    def by_chip_rows(pair):
        return [p.reshape(N_CHIPS, -1, d) for p in pair]

    def ffn_backward(dh_out, h_in, hu, layer, ride):
        dh_in, act, dhu, hn, dcw, dg, *landed = _ffn_bwd(dh_out, hu, h_in, g_ffn[layer], w_up[layer], ffn_cw[layer],
                                                         w_dn[layer], tm=tm, ride=ride)
        d_up = _weight_grad(hn, dhu, N_CHIPS, rows=wg_rows_in)
        d_dn = by_chip_rows(_weight_grad(act, dh_out, 1, rows=wg_rows))
        return dh_in, d_up, d_dn, dcw[0:3], dg[0], landed

    dh3, d_up1, d_dn1, d_fcw1, d_gf1, _ = ffn_backward(dh4, h3, hu1, 1, None)
    dh2, dhh1, y_rg, hn_rg, d_vec, d_wa, d_wx, d_gm1, *landed_ffn1 = _rg_bwd(
        dh3, hh1, hs, gates, h2, g_mix[1], w_rg_in, rg_cw, vec, wa, wx, w_rg_out, tm=tm,
        ride=_Scatter([d_up1[1], d_dn1[1]]))
    d_rg_in = _weight_grad(hn_rg, dhh1, N_CHIPS, rows=wg_rows_in)
    d_rg_out = by_chip_rows(_weight_grad(y_rg, dh3, 1, rows=wg_rows))
    early = {"rg_conv_w": d_vec[G_CONV_W:G_CONV_W + 4], "rg_conv_b": d_vec[G_CONV_B:G_CONV_B + 1],
             "rg_b_gate_a": d_vec[G_B_A:G_B_A + 1], "rg_b_gate_x": d_vec[G_B_X:G_B_X + 1],
             "rg_lambda": d_vec[G_LAMBDA:G_LAMBDA + 1], "ffn_conv_w.1": d_fcw1, "norm_mix_g.1": d_gm1[0:1],
             "norm_ffn_g.1": d_gf1[None], "final_norm_g": d_final[0]}
    early_packed = _pack(list(early.values()))
    gate_names = ["rg_w_gate_a", "rg_w_gate_x"]
    to_all = _AllDevices([early_packed, d_wa.reshape(-1, LANES), d_wx.reshape(-1, LANES)])
    dh1, d_up0, d_dn0, d_fcw0, d_gf0, landed = ffn_backward(
        dh2, h1, hu0, 0, _Both(_Scatter([d_rg_in[1], d_rg_out[1]]), to_all))
    landed_rg, early_by_device, gates_by_device = landed[0:2], landed[2], landed[3:]

    def core_sum(pair, received):
        own = lax.dynamic_index_in_dim(pair[0], chip, 0, keepdims=False)
        return _sum_parts(own, received, rows=_divisor_rows(own.shape[0]))

    early_big = [("rg_w_in", 0), ("rg_w_out", 0), ("ffn_w_up", 1), ("ffn_w_down", 1)]
    early_sum = [core_sum(d_rg_in, landed_rg[0]), core_sum(d_rg_out, landed_rg[1]), core_sum(d_up1, landed_ffn1[0]),
                 core_sum(d_dn1, landed_ffn1[1])]
    grad_x, dhh0, z_sc, hn_sc, d_sccw, d_gm0, d_first, *landed = _sc_bwd(
        dh1, hh0, tokens, first, g_mix[0], w_sc_in, sc_cw, w_sc_out, tm=tm,
        ride=_Both(_Scatter([d_up0[1], d_dn0[1]]), _Swap(early_sum)))
    landed_ffn0, early_other = landed[0:2], landed[2:]
    late = {"meta_tokens": _time_order(d_first, tm)[tm - N_META:], "sc_conv_w": d_sccw[0:3], "ffn_conv_w.0": d_fcw0,
            "norm_mix_g.0": d_gm0[0:1], "norm_ffn_g.0": d_gf0[None]}
    late_packed = _pack(list(late.values()))
    ffn0_big = [("ffn_w_up", 0), ("ffn_w_down", 0)]
    ffn0_sum = [core_sum(d_up0, landed_ffn0[0]), core_sum(d_dn0, landed_ffn0[1])]
    *d_sc_in, ffn0_up_other, ffn0_dn_other, late_by_device = _weight_grad(
        hn_sc, dhh0, N_CHIPS, rows=wg_rows_in, ride=_Both(_Swap(ffn0_sum), _AllDevices([late_packed])))
    *d_sc_out, landed_sc_in = _weight_grad(z_sc, dh1, 1, rows=wg_rows, ride=_Scatter([d_sc_in[1]]))
    d_sc_out = by_chip_rows(d_sc_out)
    landed_sc = [landed_sc_in, *_exchange(_Scatter([d_sc_out[1]]), "scatter_last")]
    grad_x = _time_order(grad_x, tm)[None]

    sc_big = [("sc_w_in", 0), ("sc_w_out", 0)]
    sc_sum = [core_sum(d_sc_in, landed_sc[0]), core_sum(d_sc_out, landed_sc[1])]
    sc_other = _exchange(_Swap(sc_sum), "swap_cores")
    out = {k: {} for k in ("grad", "delta", "m", "v")}
    stacked = {}
    for (n, layer), mine, theirs in zip(sc_big + ffn0_big + early_big, sc_sum + ffn0_sum + early_sum,
                                        [*sc_other, ffn0_up_other, ffn0_dn_other, *early_other]):
        stacked[n] = _adamw(weights[n], m_in[n], v_in[n], [mine, theirs], rows=_divisor_rows(mine.shape[0]), layer=layer,
                            into=stacked.get(n))
    for n, res in stacked.items():
        for k, key in enumerate(("grad", "delta", "m", "v")):
            out[key][n] = res[k]

    summed = {}
    for parts, packed, by_device in ((early, early_packed, early_by_device), (late, late_packed, late_by_device)):
        total = _sum_stack(by_device, rows=packed.shape[0])
        summed.update(zip(parts, _unpack(total, [p.shape for p in parts.values()])))
    for n in ("ffn_conv_w", "norm_mix_g", "norm_ffn_g"):
        summed[n] = jnp.concatenate([summed.pop(n + ".0"), summed.pop(n + ".1")])
    for n, by_device in zip(gate_names, gates_by_device):
        as_rows = lambda a: a.reshape(1, -1, LANES)
        res = _adamw(as_rows(weights[n]), as_rows(m_in[n]), as_rows(v_in[n]), [_sum_stack(by_device, rows=256)], rows=256)
        for k, key in enumerate(("grad", "delta", "m", "v")):
            out[key][n] = res[k].reshape(weights[n].shape)
    replicated = ["norm_mix_g", "norm_ffn_g", "final_norm_g"]
    small_names = small_sharded + replicated
    grads = {}
    for n in small_sharded:
        width = small_2d[n].shape[1]
        grads[n] = lax.dynamic_slice_in_dim(summed[n], chip * width, width, axis=1).reshape(weights[n].shape)
    for n in replicated:
        grads[n] = summed[n].reshape(weights[n].shape)
    shapes = [weights[n].shape for n in small_names]
    packed_w = _pack([weights[n] for n in small_names])
    res = _adamw(packed_w[None], _pack([m_in[n] for n in small_names])[None], _pack([v_in[n] for n in small_names])[None],
                 [_pack([grads[n] for n in small_names])], rows=packed_w.shape[0])
    for k, key in enumerate(("grad", "delta", "m", "v")):
        out[key].update(dict(zip(small_names, _unpack(res[k][0], shapes))))

    return (loss, grad_x, *[out["grad"][n] for n in names], *[out["delta"][n] for n in names],
            *[out["m"][n] for n in names], *[out["v"][n] for n in names])
```

```python
import functools

import jax
import jax.numpy as jnp
from jax import lax
from jax.experimental import pallas as pl
from jax.experimental.pallas import tpu as pltpu

F32 = jnp.float32
MXU_DT = jnp.bfloat16
ACT_DT = jnp.bfloat16
WIRE_DT = jnp.bfloat16
MESH_ID = pl.DeviceIdType.MESH

N_META = 16
RMS_EPS = 1e-6
RG_C = 8.0
ADAM_LR, ADAM_B1, ADAM_B2, ADAM_EPS, ADAM_WD, ADAM_STEP = 0.001, 0.9, 0.999, 1e-08, 0.01, 10
N_CHIPS = 4
VMEM_LIMIT = 60 * 1024 * 1024
F32_ROWS = 8
ACT_ROWS = 16
LANES = 128


def _row_tile(seq):
    for tm in (256, 128, 64, 32, 16):
        if seq % tm == 0:
            return tm
    raise ValueError(f"sequence length {seq} is not a multiple of 16")


def _params(n_axes=1, **kw):
    return pltpu.CompilerParams(dimension_semantics=("arbitrary",) * n_axes, vmem_limit_bytes=VMEM_LIMIT, **kw)


def _const(shape):
    return pl.BlockSpec(shape, lambda *_: (0,) * len(shape), pipeline_mode=pl.Buffered(1))


def _dot(a, b):
    return jnp.dot(a, b, preferred_element_type=F32)


def _dot_nt(a, b):
    return lax.dot_general(a, b, (((1,), (1,)), ((), ())), preferred_element_type=F32)


def _dot_tn(a, b):
    return lax.dot_general(a, b, (((0,), (0,)), ((), ())), preferred_element_type=F32)


def _sigmoid(x):
    return 0.5 + 0.5 * jnp.tanh(0.5 * x)


def _rms(h, g):
    rstd = lax.rsqrt(jnp.mean(h * h, axis=-1, keepdims=True) + RMS_EPS)
    xhat = h * rstd
    return xhat * g, xhat, rstd


def _rms_bwd(dhn, xhat, rstd, g):
    dx = dhn * g
    return rstd * (dx - xhat * jnp.mean(dx * xhat, axis=-1, keepdims=True))


def _gelu(x):
    k = 0.7978845608028654
    t = jnp.tanh(k * (x + 0.044715 * x * x * x))
    return 0.5 * x * (1.0 + t), t


def _gelu_grad(x, t):
    k = 0.7978845608028654
    return 0.5 * (1.0 + t) + 0.5 * x * (1.0 - t * t) * k * (1.0 + 3 * 0.044715 * x * x)


def _softplus(x):
    e = jnp.exp(-jnp.abs(x))
    return jnp.maximum(x, 0.0) + jnp.where(e < 1e-4, e - 0.5 * e * e, jnp.log(1.0 + e))


def _expm1_neg(z, exp_z):
    series = z * (1.0 + z * (0.5 + z * (1.0 / 6)))
    return jnp.where(z > -0.02, series, exp_z - 1.0)


def _tile_order(a, tm):
    return a.reshape(-1, F32_ROWS, tm // F32_ROWS, a.shape[-1]).swapaxes(1, 2).reshape(a.shape)


def _time_order(a, tm):
    return a.reshape(-1, tm // F32_ROWS, F32_ROWS, a.shape[-1]).swapaxes(1, 2).reshape(a.shape)


def _valid_rows(tile, tm):
    row = lax.broadcasted_iota(jnp.int32, (tm, 1), 0)
    time = (row & (F32_ROWS - 1)) * (tm // F32_ROWS) + (row >> 3) + tile * tm
    return time >= tm - N_META


def _sublane():
    return lax.broadcasted_iota(jnp.int32, (F32_ROWS, 1), 0)


def _past_rows(width):
    return (width - 1) * F32_ROWS


def _halo_block(past, tm, nt):
    rows = -(-past // ACT_ROWS) * ACT_ROWS
    return rows, lambda i: (jnp.maximum((nt - 1 - i) * (tm // rows) - 1, 0), 0)


def _link_past(buf, cols, width, tm):
    past = _past_rows(width)
    for k in range(1, width):
        rows = pl.ds(past - F32_ROWS * k, F32_ROWS)
        before = pltpu.roll(buf[rows, cols], 1, 0)
        mine = pltpu.roll(buf[pl.ds(past + tm - F32_ROWS * k, F32_ROWS), cols], 1, 0)
        buf[rows, cols] = jnp.where(_sublane() == 0, before, mine)


def _link_future(buf, cols, width, tm):
    for k in range(1, width):
        rows = pl.ds(tm + F32_ROWS * (k - 1), F32_ROWS)
        after = pltpu.roll(buf[rows, cols], F32_ROWS - 1, 0)
        mine = pltpu.roll(buf[pl.ds(F32_ROWS * (k - 1), F32_ROWS), cols], F32_ROWS - 1, 0)
        buf[rows, cols] = jnp.where(_sublane() == F32_ROWS - 1, after, mine)


def _conv_taps(buf, cols, width, tm):
    return [buf[pl.ds(F32_ROWS * k, tm), cols] for k in range(width)]


def _conv_back(buf, cw_ref, cols, width, tm):
    return sum(cw_ref[k:k + 1, cols] * buf[pl.ds(F32_ROWS * (width - 1 - k), tm), cols] for k in range(width))


ANY = pl.BlockSpec(memory_space=pl.ANY)


def _place():
    return lax.axis_index("x"), lax.axis_index("y"), lax.axis_index("c")


def _other_chips(x, y):
    return [(1 - x, y), (x, 1 - y), (1 - x, 1 - y)]


class _Gather:
    def __init__(self, shards):
        nk = len(shards)
        self.arrays = list(shards)
        self.out_shape = [jax.ShapeDtypeStruct((N_CHIPS,) + s.shape, s.dtype) for s in shards]
        self.scratch = [pltpu.SemaphoreType.DMA((nk, 3)), pltpu.SemaphoreType.DMA((nk, 3)), pltpu.SemaphoreType.DMA((nk,))]

    def run(self, ins, outs, sems, start):
        send_sems, recv_sems, local_sems = sems
        x, y, c = _place()
        mine = 2 * x + y
        for k in range(len(ins)):
            local = pltpu.make_async_copy(ins[k], outs[k].at[mine], local_sems.at[k])
            local.start() if start else local.wait()
            for j, (px, py) in enumerate(_other_chips(x, y)):
                sems_kj = dict(send_sem=send_sems.at[k, j], recv_sem=recv_sems.at[k, j], device_id=(px, py, c),
                               device_id_type=MESH_ID)
                send = pltpu.make_async_remote_copy(src_ref=ins[k], dst_ref=outs[k].at[mine], **sems_kj)
                if start:
                    send.start()
                else:
                    pltpu.make_async_remote_copy(src_ref=ins[k], dst_ref=outs[k].at[2 * px + py], **sems_kj).wait_recv()
                    send.wait_send()


class _GatherHalves:
    def __init__(self, shards):
        nk = len(shards)
        self.arrays = list(shards)
        self.out_shape = [jax.ShapeDtypeStruct((N_CHIPS,) + s.shape, s.dtype) for s in shards]
        self.scratch = [pltpu.SemaphoreType.DMA((nk, 3)) for _ in range(4)] + [pltpu.SemaphoreType.DMA((nk,))]

    def run(self, ins, outs, sems, start):
        far_send, far_recv, near_send, near_recv, local_sems = sems
        x, y, c = _place()
        mine = 2 * x + y
        for phase in ((0,) if start else (1, 2)):
            for k in range(len(ins)):
                half = ins[k].shape[0] // 2
                my_half = pl.ds(pl.multiple_of(c * half, ACT_ROWS), half)
                other_half = pl.ds(pl.multiple_of((1 - c) * half, ACT_ROWS), half)
                if phase != 1:
                    local = pltpu.make_async_copy(ins[k], outs[k].at[mine], local_sems.at[k])
                    local.start() if phase == 0 else local.wait()
                for j, (px, py) in enumerate(_other_chips(x, y)):
                    theirs = 2 * px + py
                    far = dict(send_sem=far_send.at[k, j], recv_sem=far_recv.at[k, j], device_id=(px, py, c),
                               device_id_type=MESH_ID)
                    near = dict(send_sem=near_send.at[k, j], recv_sem=near_recv.at[k, j], device_id=(x, y, 1 - c),
                                device_id_type=MESH_ID)
                    landed = outs[k].at[theirs, my_half]
                    send = lambda: pltpu.make_async_remote_copy(src_ref=ins[k].at[my_half], dst_ref=outs[k].at[mine, my_half], **far)
                    pass_on = lambda: pltpu.make_async_remote_copy(src_ref=landed, dst_ref=landed, **near)
                    if phase == 0:
                        send().start()
                    elif phase == 1:
                        pltpu.make_async_remote_copy(src_ref=ins[k].at[my_half], dst_ref=landed, **far).wait_recv()
                        pass_on().start()
                    else:
                        pltpu.make_async_remote_copy(src_ref=landed, dst_ref=outs[k].at[theirs, other_half], **near).wait_recv()
                        pass_on().wait_send()
                        send().wait_send()


class _Scatter:
    def __init__(self, parts):
        nk = len(parts)
        self.arrays = list(parts)
        self.out_shape = [jax.ShapeDtypeStruct((3,) + p.shape[1:], p.dtype) for p in parts]
        self.scratch = [pltpu.SemaphoreType.DMA((nk, 3)), pltpu.SemaphoreType.DMA((nk, 3))]

    def run(self, ins, outs, sems, start):
        send_sems, recv_sems = sems
        x, y, c = _place()
        for k in range(len(ins)):
            for j, (px, py) in enumerate(_other_chips(x, y)):
                send = pltpu.make_async_remote_copy(
                    src_ref=ins[k].at[2 * px + py], dst_ref=outs[k].at[j], send_sem=send_sems.at[k, j],
                    recv_sem=recv_sems.at[k, j], device_id=(px, py, c), device_id_type=MESH_ID)
                if start:
                    send.start()
                else:
                    send.wait_recv()
                    send.wait_send()


def _exchange(ride, name):
    n_in, n_out = len(ride.arrays), len(ride.out_shape)

    def body(*refs):
        ride.run(refs[:n_in], refs[n_in:n_in + n_out], refs[n_in + n_out:], start=True)
        ride.run(refs[:n_in], refs[n_in:n_in + n_out], refs[n_in + n_out:], start=False)

    return pl.pallas_call(body, name=name, in_specs=[ANY] * n_in, out_specs=[ANY] * n_out, out_shape=ride.out_shape,
                          scratch_shapes=ride.scratch)(*ride.arrays)


def _launch(body, operands, *, name, grid, in_specs, out_specs, out_shape, scratch_shapes=(), ride=None):
    common = dict(name=name, grid=grid, compiler_params=_params(len(grid)))
    if ride is None:
        return pl.pallas_call(body, in_specs=in_specs, out_specs=out_specs, out_shape=out_shape,
                              scratch_shapes=list(scratch_shapes), **common)(*operands)
    n_in, n_out, n_scr = len(operands), len(out_shape), len(scratch_shapes)
    r_in, r_out = len(ride.arrays), len(ride.out_shape)

    def riding(*refs):
        ins, refs = refs[:n_in], refs[n_in:]
        r_ins, refs = refs[:r_in], refs[r_in:]
        outs, refs = refs[:n_out], refs[n_out:]
        r_outs, refs = refs[:r_out], refs[r_out:]
        scr, r_sems = refs[:n_scr], refs[n_scr:]
        step = [pl.program_id(axis) for axis in range(len(grid))]
        first = functools.reduce(jnp.logical_and, [s == 0 for s in step])
        last = functools.reduce(jnp.logical_and, [s == size - 1 for s, size in zip(step, grid)])

        @pl.when(first)
        def _():
            ride.run(r_ins, r_outs, r_sems, start=True)

        body(*ins, *outs, *scr)

        @pl.when(last)
        def _():
            ride.run(r_ins, r_outs, r_sems, start=False)

    return pl.pallas_call(
        riding, in_specs=list(in_specs) + [ANY] * r_in, out_specs=list(out_specs) + [ANY] * r_out,
        out_shape=list(out_shape) + ride.out_shape, scratch_shapes=list(scratch_shapes) + ride.scratch, **common,
    )(*operands, *ride.arrays)


def _sc_fwd(x, first, g, w_in, cw, w_out, *, tm, ride=None):
    seq, d = x.shape
    nt = seq // tm + 1
    nq, _, n = w_in.shape
    width = cw.shape[0]
    past = _past_rows(width)

    def body(x_ref, first_ref, g_ref, win_ref, cw_ref, wout_ref, h1_ref, hh_ref, hh_scr, cbuf):
        i = pl.program_id(0)

        @pl.when(i == 0)
        def _():
            cbuf[pl.ds(0, past), :] = jnp.zeros((past, d), F32)

        h = jnp.where(i == 0, first_ref[...], x_ref[...])
        hn = _rms(h, g_ref[...])[0].astype(MXU_DT)
        for q in range(nq):
            hh_scr[:, q * n:(q + 1) * n] = _dot(hn, win_ref[q])
        hh_ref[...] = hh_scr[...].astype(hh_ref.dtype)
        b = hh_scr[:, 0:d]
        cbuf[pl.ds(past, tm), :] = hh_scr[:, d:2 * d] * hh_scr[:, 2 * d:3 * d]
        last = cbuf[pl.ds(tm, past), :]
        _link_past(cbuf, slice(None), width, tm)
        u = sum(cw_ref[k:k + 1, :] * tap for k, tap in enumerate(_conv_taps(cbuf, slice(None), width, tm)))
        cbuf[pl.ds(0, past), :] = last
        h1_ref[...] = h + _dot((b * u).astype(MXU_DT), wout_ref[...])

    return _launch(
        body, [x, first, g, w_in, cw, w_out], name="sc_fwd", grid=(nt,),
        in_specs=[pl.BlockSpec((tm, d), lambda i: (jnp.maximum(i - 1, 0), 0)), _const((tm, d)), _const((1, d)),
                  _const(w_in.shape), _const(cw.shape), _const(w_out.shape)],
        out_specs=[pl.BlockSpec((tm, d), lambda i: (i, 0)), pl.BlockSpec((tm, nq * n), lambda i: (i, 0))],
        out_shape=[jax.ShapeDtypeStruct((nt * tm, d), F32), jax.ShapeDtypeStruct((nt * tm, nq * n), ACT_DT)],
        scratch_shapes=[pltpu.VMEM((tm, nq * n), F32), pltpu.VMEM((past + tm, d), F32)],
        ride=ride,
    )


def _sc_bwd(dh, hh, x, first, g, w_in, cw, w_out, *, tm, ride=None):
    t_len, d = dh.shape
    nt = t_len // tm
    nq, _, n = w_in.shape
    width = cw.shape[0]
    past = _past_rows(width)
    halo_rows, halo_index = _halo_block(past, tm, nt)

    def body(dh_ref, hh_ref, hhp_ref, x_ref, first_ref, g_ref, win_ref, cw_ref, wout_ref,
             dx_ref, dhh_ref, z_ref, hn_ref, dcw_ref, dg_ref, dfirst_ref, cbuf, dbuf):
        i = pl.program_id(0)
        r = nt - 1 - i

        @pl.when(i == 0)
        def _():
            dbuf[pl.ds(tm, past), :] = jnp.zeros((past, d), F32)
            dcw_ref[...] = jnp.zeros_like(dcw_ref)
            dg_ref[...] = jnp.zeros_like(dg_ref)

        dh_out = dh_ref[...]
        b = hh_ref[:, 0:d].astype(F32)
        c = hh_ref[:, d:2 * d].astype(F32)
        v = hh_ref[:, 2 * d:3 * d].astype(F32)
        prev = hhp_ref[...].astype(F32)[halo_rows - past:, :]
        cbuf[pl.ds(0, past), :] = jnp.where(r > 0, prev[:, d:2 * d] * prev[:, 2 * d:3 * d], 0.0)
        cbuf[pl.ds(past, tm), :] = c * v
        _link_past(cbuf, slice(None), width, tm)
        taps = _conv_taps(cbuf, slice(None), width, tm)
        u = sum(cw_ref[k:k + 1, :] * taps[k] for k in range(width))
        z_ref[...] = (b * u).astype(z_ref.dtype)
        dz = _dot_nt(dh_out.astype(MXU_DT), wout_ref[...])
        d_b = (dz * u).astype(dhh_ref.dtype)
        dhh_ref[:, 0:d] = d_b
        parts = [_dot_nt(d_b[:, 0:n], win_ref[0])]
        du = dz * b
        for k in range(width):
            dcw_ref[k:k + 1, :] += jnp.sum(taps[k] * du, axis=0, keepdims=True)
        dbuf[pl.ds(0, tm), :] = du
        _link_future(dbuf, slice(None), width, tm)
        dcv = _conv_back(dbuf, cw_ref, slice(None), width, tm)
        dbuf[pl.ds(tm, past), :] = dbuf[pl.ds(0, past), :]
        d_c, d_v = (dcv * v).astype(dhh_ref.dtype), (dcv * c).astype(dhh_ref.dtype)
        dhh_ref[:, d:2 * d] = d_c
        dhh_ref[:, 2 * d:3 * d] = d_v
        rest = jnp.concatenate([d_b[:, n:], d_c, d_v], axis=1)
        parts += [_dot_nt(rest[:, (q - 1) * n:q * n], win_ref[q]) for q in range(1, nq)]
        dhn = functools.reduce(lambda a, b: a + b, parts)
        h_in = jnp.where(r == 0, first_ref[...], x_ref[...])
        dh_in = _norm_bwd_tile(dhn, h_in, dh_out, g_ref[...], _valid_rows(r, tm), hn_ref, dg_ref)

        @pl.when(r == 0)
        def _():
            dfirst_ref[...] = dh_in

        @pl.when(r > 0)
        def _():
            dx_ref[...] = dh_in

    rev = lambda i: (nt - 1 - i, 0)
    rev_x = lambda i: (jnp.maximum(nt - 2 - i, 0), 0)
    return _launch(
        body, [dh, hh, hh, x, first, g, w_in, cw, w_out], name="sc_bwd", grid=(nt,),
        in_specs=[pl.BlockSpec((tm, d), rev), pl.BlockSpec((tm, 3 * d), rev), pl.BlockSpec((halo_rows, 3 * d), halo_index),
                  pl.BlockSpec((tm, d), rev_x), _const((tm, d)), _const((1, d)), _const(w_in.shape), _const(cw.shape),
                  _const(w_out.shape)],
        out_specs=[pl.BlockSpec((tm, d), rev_x), pl.BlockSpec((tm, 3 * d), rev), pl.BlockSpec((tm, d), rev),
                   pl.BlockSpec((tm, d), rev), _const((F32_ROWS, d)), _const((F32_ROWS, d)), _const((tm, d))],
        out_shape=[jax.ShapeDtypeStruct((t_len - tm, d), F32), jax.ShapeDtypeStruct((t_len, 3 * d), ACT_DT),
                   jax.ShapeDtypeStruct((t_len, d), ACT_DT), jax.ShapeDtypeStruct((t_len, d), ACT_DT),
                   jax.ShapeDtypeStruct((F32_ROWS, d), F32), jax.ShapeDtypeStruct((F32_ROWS, d), F32),
                   jax.ShapeDtypeStruct((tm, d), F32)],
        scratch_shapes=[pltpu.VMEM((past + tm, d), F32), pltpu.VMEM((tm + past, d), F32)],
        ride=ride,
    )


def _ffn_fwd(h, g, w_up, cw, w_down, *, tm, ride=None, loss=None):
    t_len, d = h.shape
    nt = t_len // tm
    nq, _, n = w_up.shape
    width = cw.shape[0]
    past = _past_rows(width)

    def body(h_ref, g_ref, wup_ref, cw_ref, wdn_ref, *rest):
        if loss is None:
            out_ref, hu_ref, hc_ref, ubuf, tail = rest
        else:
            t_ref, gf_ref, out_ref, hu_ref, hc_ref, sq_ref, dgf_ref, ubuf, tail = rest
        i = pl.program_id(0)

        @pl.when(i == 0)
        def _():
            tail[...] = jnp.zeros_like(tail)

        h_in = h_ref[...]
        hn = _rms(h_in, g_ref[...])[0].astype(MXU_DT)
        ubuf[pl.ds(0, past), :] = tail[...]
        for q in range(nq):
            ubuf[pl.ds(past, tm), q * n:(q + 1) * n] = _dot(hn, wup_ref[q])
        hu_ref[...] = ubuf[pl.ds(past, tm), :].astype(hu_ref.dtype)
        tail[...] = ubuf[pl.ds(tm, past), :]
        _link_past(ubuf, slice(None), width, tm)
        acc = h_in
        for j in range(nq // 2):
            gcol, vcol = slice(j * n, (j + 1) * n), slice((nq // 2 + j) * n, (nq // 2 + j + 1) * n)
            conv = lambda cols: sum(cw_ref[k:k + 1, cols] * tap for k, tap in enumerate(_conv_taps(ubuf, cols, width, tm)))
            gj, vj = conv(gcol), conv(vcol)
            hc_ref[:, gcol] = gj.astype(hc_ref.dtype)
            hc_ref[:, vcol] = vj.astype(hc_ref.dtype)
            acc = acc + _dot((gj * _sigmoid(gj) * vj).astype(MXU_DT), wdn_ref[j * n:(j + 1) * n, :])
        if loss is None:
            out_ref[...] = acc
            return

        @pl.when(i == 0)
        def _():
            sq_ref[...] = jnp.zeros_like(sq_ref)
            dgf_ref[...] = jnp.zeros_like(dgf_ref)
            out_ref[...] = jnp.zeros_like(out_ref)

        @pl.when(i > 0)
        def _():
            gain = gf_ref[...]
            out, xhat, rstd = _rms(acc, gain)
            err = out - t_ref[...]
            sq_ref[0:1, :] += jnp.sum(err * err, axis=0, keepdims=True)
            dout = err * (1.0 / d)
            dgf_ref[0:1, :] += jnp.sum(dout * xhat, axis=0, keepdims=True)
            out_ref[...] = _rms_bwd(dout, xhat, rstd, gain)

    row = lambda i: (i, 0)
    stat = jax.ShapeDtypeStruct((F32_ROWS, d), F32)
    return _launch(
        body, [h, g, w_up, cw, w_down] + list(loss or ()), name="ffn_fwd", grid=(nt,),
        in_specs=[pl.BlockSpec((tm, d), row), _const((1, d)), _const(w_up.shape), _const(cw.shape), _const(w_down.shape)]
        + ([pl.BlockSpec((tm, d), lambda i: (jnp.maximum(i - 1, 0), 0)), _const((1, d))] if loss else []),
        out_specs=[pl.BlockSpec((tm, d), row), pl.BlockSpec((tm, nq * n), row), pl.BlockSpec((tm, nq * n), row)]
        + ([_const(stat.shape)] * 2 if loss else []),
        out_shape=[jax.ShapeDtypeStruct((t_len, d), F32), jax.ShapeDtypeStruct((t_len, nq * n), ACT_DT),
                   jax.ShapeDtypeStruct((t_len, nq * n), ACT_DT)] + ([stat, stat] if loss else []),
        scratch_shapes=[pltpu.VMEM((past + tm, nq * n), F32), pltpu.VMEM((past, nq * n), F32)],
        ride=ride,
    )


def _norm_bwd_tile(dhn, h_in, dh, gain, valid, hn_ref, dg_ref):
    hn, xhat, rstd = _rms(h_in, gain)
    hn_ref[...] = hn.astype(hn_ref.dtype)
    dg_ref[0:1, :] += jnp.sum(dhn * xhat, axis=0, keepdims=True)
    return jnp.where(valid, dh + _rms_bwd(dhn, xhat, rstd, gain), 0.0)


def _ffn_bwd(dh, hu, hc, h, g, w_up, cw, w_down, *, tm, ride=None):
    t_len, d = dh.shape
    nt = t_len // tm
    ff = hu.shape[1]
    n = ff // 4
    width = cw.shape[0]
    past = _past_rows(width)
    halo_rows, halo_index = _halo_block(past, tm, nt)

    def body(dh_ref, hu_ref, hup_ref, hc_ref, h_ref, g_ref, wup_ref, cw_ref, wdn_ref,
             dhin_ref, a_ref, dhu_ref, hn_ref, dcw_ref, dg_ref, ubuf, dbuf, head):
        i = pl.program_id(0)
        r = nt - 1 - i

        @pl.when(i == 0)
        def _():
            head[...] = jnp.zeros_like(head)
            dcw_ref[...] = jnp.zeros_like(dcw_ref)
            dg_ref[...] = jnp.zeros_like(dg_ref)

        dh_out = dh_ref[...]
        dhb = dh_out.astype(MXU_DT)
        dhn_parts = []
        d_act = [_dot_nt(dhb, wdn_ref[j * n:(j + 1) * n, :]) for j in range(2)]
        for j in range(2):
            mine = slice(0, n), slice(n, 2 * n)
            full = slice(j * n, (j + 1) * n), slice((2 + j) * n, (3 + j) * n)
            for here, there in zip(mine, full):
                prev = hup_ref[:, there].astype(F32)[halo_rows - past:, :]
                ubuf[pl.ds(0, past), here] = jnp.where(r > 0, prev, 0.0)
                ubuf[pl.ds(past, tm), here] = hu_ref[:, there].astype(F32)
                dbuf[pl.ds(tm, past), here] = head[:, there]
            _link_past(ubuf, slice(None), width, tm)
            gj, vj = hc_ref[:, full[0]].astype(F32), hc_ref[:, full[1]].astype(F32)
            sg = _sigmoid(gj)
            s = gj * sg
            a_ref[:, full[0]] = (s * vj).astype(a_ref.dtype)
            da = d_act[j]
            dbuf[pl.ds(0, tm), mine[1]] = da * s
            dbuf[pl.ds(0, tm), mine[0]] = da * vj * (sg * (1.0 + gj * (1.0 - sg)))
            for here, there in zip(mine, full):
                head[:, there] = dbuf[pl.ds(0, past), here]
            _link_future(dbuf, slice(None), width, tm)
            for here, there in zip(mine, full):
                dy = dbuf[pl.ds(0, tm), here]
                for k, tap in enumerate(_conv_taps(ubuf, here, width, tm)):
                    dcw_ref[k:k + 1, there] += jnp.sum(tap * dy, axis=0, keepdims=True)
                dhu = sum(cw_ref[k:k + 1, there] * dbuf[pl.ds(F32_ROWS * (width - 1 - k), tm), here]
                          for k in range(width)).astype(dhu_ref.dtype)
                dhu_ref[:, there] = dhu
                dhn_parts.append(_dot_nt(dhu, wup_ref[there.start // n]))
        dhn = (dhn_parts[0] + dhn_parts[1]) + (dhn_parts[2] + dhn_parts[3])
        dhin_ref[...] = _norm_bwd_tile(dhn, h_ref[...], dh_out, g_ref[...], _valid_rows(r, tm), hn_ref, dg_ref)

    rev = lambda i: (nt - 1 - i, 0)
    return _launch(
        body, [dh, hu, hu, hc, h, g, w_up, cw, w_down], name="ffn_bwd", grid=(nt,),
        in_specs=[pl.BlockSpec((tm, d), rev), pl.BlockSpec((tm, ff), rev), pl.BlockSpec((halo_rows, ff), halo_index),
                  pl.BlockSpec((tm, ff), rev), pl.BlockSpec((tm, d), rev), _const((1, d)), _const(w_up.shape), _const(cw.shape), _const(w_down.shape)],
        out_specs=[pl.BlockSpec((tm, d), rev), pl.BlockSpec((tm, 2 * n), rev), pl.BlockSpec((tm, ff), rev),
                   pl.BlockSpec((tm, d), rev), _const((F32_ROWS, ff)), _const((F32_ROWS, d))],
        out_shape=[jax.ShapeDtypeStruct((t_len, d), F32), jax.ShapeDtypeStruct((t_len, 2 * n), ACT_DT),
                   jax.ShapeDtypeStruct((t_len, ff), ACT_DT), jax.ShapeDtypeStruct((t_len, d), ACT_DT),
                   jax.ShapeDtypeStruct((F32_ROWS, ff), F32), jax.ShapeDtypeStruct((F32_ROWS, d), F32)],
        scratch_shapes=[pltpu.VMEM((past + tm, 2 * n), F32), pltpu.VMEM((tm + past, 2 * n), F32), pltpu.VMEM((past, ff), F32)],
        ride=ride,
    )


V_CONV_B, V_B_A, V_B_X, V_LAMBDA = 0, 1, 2, 3
G_CONV_W, G_CONV_B, G_B_A, G_B_X, G_LAMBDA = 0, 4, 5, 6, 7


def _scan(a_ref, b_ref, edge, tm, reverse):
    nj = tm // F32_ROWS
    order = range(nj - 1, -1, -1) if reverse else range(nj)
    slab = lambda ref, j: ref[pl.ds(F32_ROWS * j, F32_ROWS), :]
    a_run = b_run = None
    for j in order:
        a_j, b_j = slab(a_ref, j), slab(b_ref, j)
        if a_run is not None:
            b_j = b_j + a_j * b_run
            a_j = a_j * a_run
            b_ref[pl.ds(F32_ROWS * j, F32_ROWS), :] = b_j
            a_ref[pl.ds(F32_ROWS * j, F32_ROWS), :] = a_j
        a_run, b_run = a_j, b_j
    sub = _sublane()
    shift = 1
    while shift < F32_ROWS:
        amount = F32_ROWS - shift if reverse else shift
        keep = (sub < F32_ROWS - shift) if reverse else (sub >= shift)
        b_run = jnp.where(keep, b_run + a_run * pltpu.roll(b_run, amount, 0), b_run)
        a_run = jnp.where(keep, a_run * pltpu.roll(a_run, amount, 0), a_run)
        shift *= 2
    outer = edge[0:1, :] if reverse else edge[F32_ROWS - 1:F32_ROWS, :]
    ends = b_run + a_run * outer
    if reverse:
        carry = jnp.where(sub == F32_ROWS - 1, outer, pltpu.roll(ends, F32_ROWS - 1, 0))
    else:
        carry = jnp.where(sub == 0, outer, pltpu.roll(ends, 1, 0))
    for j in range(nj):
        b_ref[pl.ds(F32_ROWS * j, F32_ROWS), :] = slab(b_ref, j) + slab(a_ref, j) * carry
    return slab(b_ref, 0 if reverse else nj - 1)


def _rg_gates(u, vec_ref, wa_ref, wx_ref, pre_scr, nb, bd):
    ub = u.astype(MXU_DT)
    for k in range(nb):
        blk = slice(k * bd, (k + 1) * bd)
        pre_scr[0, :, blk] = _dot(ub[:, blk], wa_ref[k])
        pre_scr[1, :, blk] = _dot(ub[:, blk], wx_ref[k])
    r_gate = _sigmoid(pre_scr[0] + vec_ref[V_B_A:V_B_A + 1, :])
    i_gate = _sigmoid(pre_scr[1] + vec_ref[V_B_X:V_B_X + 1, :])
    return r_gate, i_gate


def _rg_decay(r_gate, vec_ref):
    sp = _softplus(-vec_ref[V_LAMBDA:V_LAMBDA + 1, :])
    log_a = -RG_C * r_gate * sp
    a = jnp.exp(log_a)
    one_minus_a2 = jnp.maximum(-_expm1_neg(2.0 * log_a, a * a), 1e-30)
    inv_mult = lax.rsqrt(one_minus_a2)
    return a, one_minus_a2 * inv_mult, inv_mult, sp


def _rg_fwd(h, g, w_in, cw, vec, wa, wx, w_out, *, tm):
    t_len, d = h.shape
    nt = t_len // tm
    nq, _, n = w_in.shape
    dr = 2 * n
    width = cw.shape[0]
    past = _past_rows(width)
    nb, bd, _ = wa.shape

    def body(h_ref, g_ref, win_ref, cw_ref, vec_ref, wa_ref, wx_ref, wout_ref, out_ref, hh_ref, hs_ref, gates_ref,
             gbuf, rbuf, pre_scr, tail, edge):
        i = pl.program_id(0)

        @pl.when(i == 0)
        def _():
            tail[...] = jnp.zeros_like(tail)
            edge[...] = jnp.zeros_like(edge)

        h_in = h_ref[...]
        hn = _rms(h_in, g_ref[...])[0].astype(MXU_DT)
        rbuf[pl.ds(0, past), :] = tail[...]
        for q in range(2):
            gbuf[:, q * n:(q + 1) * n] = _dot(hn, win_ref[q])
            rbuf[pl.ds(past, tm), q * n:(q + 1) * n] = _dot(hn, win_ref[2 + q])
        hh_ref[:, 0:dr] = gbuf[...].astype(hh_ref.dtype)
        hh_ref[:, dr:2 * dr] = rbuf[pl.ds(past, tm), :].astype(hh_ref.dtype)
        tail[...] = rbuf[pl.ds(tm, past), :]
        _link_past(rbuf, slice(None), width, tm)
        taps = _conv_taps(rbuf, slice(None), width, tm)
        u = sum(cw_ref[k:k + 1, :] * taps[k] for k in range(width)) + vec_ref[V_CONV_B:V_CONV_B + 1, :]
        r_gate, i_gate = _rg_gates(u, vec_ref, wa_ref, wx_ref, pre_scr, nb, bd)
        for k, kept in enumerate((u, r_gate, i_gate)):
            gates_ref[:, k * dr:(k + 1) * dr] = kept.astype(gates_ref.dtype)
        a, mult, _, _ = _rg_decay(r_gate, vec_ref)
        pre_scr[0] = a
        pre_scr[1] = jnp.where(_valid_rows(i, tm), mult * (i_gate * u), 0.0)
        edge[...] = _scan(pre_scr.at[0], pre_scr.at[1], edge[...], tm, reverse=False)
        hs = pre_scr[1]
        hs_ref[...] = hs
        y = hs * _gelu(gbuf[...])[0]
        out_ref[...] = h_in + _dot(y.astype(MXU_DT), wout_ref[...])

    row = lambda i: (i, 0)
    return pl.pallas_call(
        body, name="rg_fwd", grid=(nt,),
        in_specs=[pl.BlockSpec((tm, d), row), _const((1, d)), _const(w_in.shape), _const(cw.shape), _const(vec.shape),
                  _const(wa.shape), _const(wx.shape), _const(w_out.shape)],
        out_specs=[pl.BlockSpec((tm, d), row), pl.BlockSpec((tm, 2 * dr), row), pl.BlockSpec((tm, dr), row),
                   pl.BlockSpec((tm, 3 * dr), row)],
        out_shape=[jax.ShapeDtypeStruct((t_len, d), F32), jax.ShapeDtypeStruct((t_len, 2 * dr), ACT_DT),
                   jax.ShapeDtypeStruct((t_len, dr), F32), jax.ShapeDtypeStruct((t_len, 3 * dr), ACT_DT)],
        scratch_shapes=[pltpu.VMEM((tm, dr), F32), pltpu.VMEM((past + tm, dr), F32), pltpu.VMEM((2, tm, dr), F32),
                        pltpu.VMEM((past, dr), F32), pltpu.VMEM((F32_ROWS, dr), F32)],
        compiler_params=_params(),
    )(h, g, w_in, cw, vec, wa, wx, w_out)


def _rg_bwd(dh, hh, hs, gates, h, g, w_in, cw, vec, wa, wx, w_out, *, tm, ride=None):
    t_len, d = dh.shape
    nt = t_len // tm
    dr = hs.shape[1]
    n = w_in.shape[2]
    width = cw.shape[0]
    nb, bd, _ = wa.shape
    past = _past_rows(width)
    halo_rows, halo_index = _halo_block(past, tm, nt)
    one = F32_ROWS

    def body(dh_ref, hh_ref, hhp_ref, hs_ref, hsp_ref, gates_ref, h_ref, g_ref, win_ref, cw_ref, vec_ref, wa_ref, wx_ref, wout_ref,
             dhin_ref, dhh_ref, y_ref, hn_ref, dvec_ref, dwa_ref, dwx_ref, dg_ref, rbuf, dbuf, pre_scr, hbuf, abuf, edge):
        i = pl.program_id(0)
        r = nt - 1 - i

        @pl.when(i == 0)
        def _():
            dbuf[pl.ds(tm, past), :] = jnp.zeros((past, dr), F32)
            abuf[pl.ds(tm, one), :] = jnp.zeros((one, dr), F32)
            edge[...] = jnp.zeros_like(edge)
            dvec_ref[...] = jnp.zeros_like(dvec_ref)
            dwa_ref[...] = jnp.zeros_like(dwa_ref)
            dwx_ref[...] = jnp.zeros_like(dwx_ref)
            dg_ref[...] = jnp.zeros_like(dg_ref)

        dh_out = dh_ref[...]
        gb = hh_ref[:, 0:dr].astype(F32)
        prev = hhp_ref[...].astype(F32)[halo_rows - past:, dr:2 * dr]
        rbuf[pl.ds(0, past), :] = jnp.where(r > 0, prev, 0.0)
        rbuf[pl.ds(past, tm), :] = hh_ref[:, dr:2 * dr].astype(F32)
        _link_past(rbuf, slice(None), width, tm)
        taps = _conv_taps(rbuf, slice(None), width, tm)
        ub = gates_ref[:, 0:dr].astype(MXU_DT)
        u, r_gate, i_gate = (gates_ref[:, k * dr:(k + 1) * dr].astype(F32) for k in range(3))
        a, mult, inv_mult, sp = _rg_decay(r_gate, vec_ref)
        hs_t = hs_ref[...]
        hbuf[pl.ds(0, one), :] = jnp.where(r > 0, hsp_ref[...], 0.0)
        hbuf[pl.ds(one, tm), :] = hs_t
        _link_past(hbuf, slice(None), 2, tm)
        h_prev = hbuf[pl.ds(0, tm), :]
        gate, th = _gelu(gb)
        y_ref[...] = (hs_t * gate).astype(y_ref.dtype)
        dy = _dot_nt(dh_out.astype(MXU_DT), wout_ref[...])
        d_gb = (dy * hs_t * _gelu_grad(gb, th)).astype(dhh_ref.dtype)
        dhh_ref[:, 0:dr] = d_gb
        dhn = sum(_dot_nt(d_gb[:, q * n:(q + 1) * n], win_ref[q]) for q in range(2))
        abuf[pl.ds(0, tm), :] = a
        _link_future(abuf, slice(None), 2, tm)
        pre_scr[0] = abuf[pl.ds(one, tm), :]
        pre_scr[1] = dy * gate
        edge[...] = _scan(pre_scr.at[0], pre_scr.at[1], edge[...], tm, reverse=True)
        abuf[pl.ds(tm, one), :] = abuf[pl.ds(0, one), :]
        d_hs = pre_scr[1]
        d_b = jnp.where(_valid_rows(r, tm), d_hs, 0.0)
        d_iu = d_b * mult
        d_log_a = d_hs * h_prev * a - d_b * (i_gate * u) * (a * a) * inv_mult
        dvec_ref[G_LAMBDA:G_LAMBDA + 1, :] += jnp.sum(d_log_a * r_gate, axis=0, keepdims=True) * (-RG_C)
        d_pre_r = d_log_a * (-RG_C * sp) * r_gate * (1.0 - r_gate)
        d_pre_i = d_iu * u * i_gate * (1.0 - i_gate)
        dvec_ref[G_B_A:G_B_A + 1, :] += jnp.sum(d_pre_r, axis=0, keepdims=True)
        dvec_ref[G_B_X:G_B_X + 1, :] += jnp.sum(d_pre_i, axis=0, keepdims=True)
        dbuf[pl.ds(0, tm), :] = d_iu * i_gate
        d_pre_r = d_pre_r.astype(MXU_DT)
        d_pre_i = d_pre_i.astype(MXU_DT)
        for k in range(nb):
            blk = slice(k * bd, (k + 1) * bd)
            dwa_ref[k] += _dot_tn(ub[:, blk], d_pre_r[:, blk])
            dwx_ref[k] += _dot_tn(ub[:, blk], d_pre_i[:, blk])
            dbuf[pl.ds(0, tm), blk] += _dot_nt(d_pre_r[:, blk], wa_ref[k]) + _dot_nt(d_pre_i[:, blk], wx_ref[k])
        du = dbuf[pl.ds(0, tm), :]
        dvec_ref[G_CONV_B:G_CONV_B + 1, :] += jnp.sum(du, axis=0, keepdims=True)
        for k in range(width):
            dvec_ref[G_CONV_W + k:G_CONV_W + k + 1, :] += jnp.sum(taps[k] * du, axis=0, keepdims=True)
        _link_future(dbuf, slice(None), width, tm)
        d_rb = _conv_back(dbuf, cw_ref, slice(None), width, tm)
        dbuf[pl.ds(tm, past), :] = dbuf[pl.ds(0, past), :]
        d_rb = d_rb.astype(dhh_ref.dtype)
        dhh_ref[:, dr:2 * dr] = d_rb
        dhn = dhn + sum(_dot_nt(d_rb[:, q * n:(q + 1) * n], win_ref[2 + q]) for q in range(2))
        dhin_ref[...] = _norm_bwd_tile(dhn, h_ref[...], dh_out, g_ref[...], _valid_rows(r, tm), hn_ref, dg_ref)

        @pl.when(i == nt - 1)
        def _():
            lam = vec_ref[V_LAMBDA:V_LAMBDA + 1, :]
            dvec_ref[G_LAMBDA:G_LAMBDA + 1, :] = dvec_ref[G_LAMBDA:G_LAMBDA + 1, :] * (-_sigmoid(-lam))

    rev = lambda i: (nt - 1 - i, 0)
    return _launch(
        body, [dh, hh, hh, hs, hs, gates, h, g, w_in, cw, vec, wa, wx, w_out], name="rg_bwd", grid=(nt,),
        in_specs=[pl.BlockSpec((tm, d), rev), pl.BlockSpec((tm, 2 * dr), rev), pl.BlockSpec((halo_rows, 2 * dr), halo_index),
                  pl.BlockSpec((tm, dr), rev),
                  pl.BlockSpec((one, dr), lambda i: (jnp.maximum((nt - 1 - i) * (tm // one) - 1, 0), 0)),
                  pl.BlockSpec((tm, 3 * dr), rev), pl.BlockSpec((tm, d), rev), _const((1, d)), _const(w_in.shape),
                  _const(cw.shape), _const(vec.shape), _const(wa.shape), _const(wx.shape), _const(w_out.shape)],
        out_specs=[pl.BlockSpec((tm, d), rev), pl.BlockSpec((tm, 2 * dr), rev), pl.BlockSpec((tm, dr), rev),
                   pl.BlockSpec((tm, d), rev), _const((F32_ROWS, dr)), _const(wa.shape), _const(wx.shape),
                   _const((F32_ROWS, d))],
        out_shape=[jax.ShapeDtypeStruct((t_len, d), F32), jax.ShapeDtypeStruct((t_len, 2 * dr), ACT_DT),
                   jax.ShapeDtypeStruct((t_len, dr), ACT_DT), jax.ShapeDtypeStruct((t_len, d), ACT_DT),
                   jax.ShapeDtypeStruct((F32_ROWS, dr), F32), jax.ShapeDtypeStruct(wa.shape, F32),
                   jax.ShapeDtypeStruct(wx.shape, F32), jax.ShapeDtypeStruct((F32_ROWS, d), F32)],
        scratch_shapes=[pltpu.VMEM((past + tm, dr), F32), pltpu.VMEM((tm + past, dr), F32), pltpu.VMEM((2, tm, dr), F32),
                        pltpu.VMEM((one + tm, dr), F32), pltpu.VMEM((tm + one, dr), F32), pltpu.VMEM((F32_ROWS, dr), F32)],
        ride=ride,
    )


def _weight_grad(a, b, nb, *, rows, ride=None):
    t_len, k_dim = a.shape
    n = b.shape[1] // nb
    nt = t_len // rows

    def body(a_ref, b_ref, out_ref, wire_ref):
        @pl.when(pl.program_id(1) == 0)
        def _():
            out_ref[...] = jnp.zeros_like(out_ref)

        out_ref[0] += _dot_tn(a_ref[...].astype(MXU_DT), b_ref[...].astype(MXU_DT))

        @pl.when(pl.program_id(1) == nt - 1)
        def _():
            wire_ref[...] = out_ref[...].astype(wire_ref.dtype)

    block = pl.BlockSpec((1, k_dim, n), lambda j, i: (j, 0, 0))
    return _launch(
        body, [a, b], name="weight_grad", grid=(nb, nt),
        in_specs=[pl.BlockSpec((rows, k_dim), lambda j, i: (i, 0)), pl.BlockSpec((rows, n), lambda j, i: (i, j))],
        out_specs=[block, block],
        out_shape=[jax.ShapeDtypeStruct((nb, k_dim, n), F32), jax.ShapeDtypeStruct((nb, k_dim, n), WIRE_DT)],
        ride=ride,
    )


def _adamw(w, m, v, parts, *, rows, layer=0, into=None):
    n_layers, n_rows, n_cols = w.shape
    nt = n_rows // rows
    n_parts = len(parts)

    def body(w_ref, m_ref, v_ref, *rest):
        part_refs, (g_ref, d_ref, nm_ref, nv_ref) = rest[:n_parts], rest[-4:]
        w_ref, m_ref, v_ref, g_ref, d_ref, nm_ref, nv_ref = (r.at[0] for r in (w_ref, m_ref, v_ref, g_ref, d_ref, nm_ref, nv_ref))
        grad = part_refs[0][...].astype(F32)
        for p in part_refs[1:]:
            grad = grad + p[...].astype(F32)
        new_m = ADAM_B1 * m_ref[...] + (1.0 - ADAM_B1) * grad
        new_v = ADAM_B2 * v_ref[...] + (1.0 - ADAM_B2) * (grad * grad)
        m_hat = new_m / (1.0 - ADAM_B1 ** ADAM_STEP)
        v_hat = new_v / (1.0 - ADAM_B2 ** ADAM_STEP)
        g_ref[...] = grad
        d_ref[...] = -ADAM_LR * (m_hat / (jnp.sqrt(v_hat) + ADAM_EPS) + ADAM_WD * w_ref[...])
        nm_ref[...] = new_m
        nv_ref[...] = new_v

    spec = pl.BlockSpec((rows, n_cols), lambda i: (i, 0))
    layer_spec = pl.BlockSpec((1, rows, n_cols), lambda i: (layer, i, 0))
    into = list(into or [])
    return pl.pallas_call(
        body, name="adamw", grid=(nt,),
        in_specs=[layer_spec] * 3 + [spec] * n_parts + [ANY] * len(into), out_specs=[layer_spec] * 4,
        out_shape=[jax.ShapeDtypeStruct(w.shape, F32)] * 4,
        input_output_aliases={3 + n_parts + k: k for k in range(len(into))},
        compiler_params=_params(),
    )(w, m, v, *parts, *into)


def _sum_stack(stack, *, rows):
    n_stack, n_rows, n_cols = stack.shape

    def body(stack_ref, out_ref):
        acc = stack_ref[0]
        for j in range(1, n_stack):
            acc = acc + stack_ref[j]
        out_ref[...] = acc

    return pl.pallas_call(
        body, name="sum_stack", grid=(n_rows // rows,),
        in_specs=[pl.BlockSpec((n_stack, rows, n_cols), lambda i: (0, i, 0))],
        out_specs=pl.BlockSpec((rows, n_cols), lambda i: (i, 0)),
        out_shape=jax.ShapeDtypeStruct((n_rows, n_cols), F32),
        compiler_params=_params(),
    )(stack)


def _sum_parts(own, recv, *, rows):
    n_rows, n_cols = own.shape
    n_recv = recv.shape[0]

    def body(own_ref, recv_ref, out_ref):
        acc = own_ref[...].astype(F32)
        for j in range(n_recv):
            acc = acc + recv_ref[j].astype(F32)
        out_ref[...] = acc

    return pl.pallas_call(
        body, name="sum_parts", grid=(n_rows // rows,),
        in_specs=[pl.BlockSpec((rows, n_cols), lambda i: (i, 0)), pl.BlockSpec((n_recv, rows, n_cols), lambda i: (0, i, 0))],
        out_specs=pl.BlockSpec((rows, n_cols), lambda i: (i, 0)),
        out_shape=jax.ShapeDtypeStruct(own.shape, F32),
        compiler_params=_params(),
    )(own, recv)


class _Swap:
    def __init__(self, arrays):
        nk = len(arrays)
        self.arrays = list(arrays)
        self.out_shape = [jax.ShapeDtypeStruct(a.shape, a.dtype) for a in arrays]
        self.scratch = [pltpu.SemaphoreType.DMA((nk,)), pltpu.SemaphoreType.DMA((nk,))]

    def run(self, ins, outs, sems, start):
        send_sems, recv_sems = sems
        x, y, c = _place()
        for k in range(len(ins)):
            send = pltpu.make_async_remote_copy(src_ref=ins[k], dst_ref=outs[k], send_sem=send_sems.at[k],
                                                recv_sem=recv_sems.at[k], device_id=(x, y, 1 - c), device_id_type=MESH_ID)
            if start:
                send.start()
            else:
                send.wait_recv()
                send.wait_send()


class _AllDevices:
    def __init__(self, arrays):
        nk = len(arrays)
        self.arrays = list(arrays)
        self.out_shape = [jax.ShapeDtypeStruct((8,) + a.shape, a.dtype) for a in arrays]
        self.scratch = [pltpu.SemaphoreType.DMA((nk, 7)), pltpu.SemaphoreType.DMA((nk, 7)), pltpu.SemaphoreType.DMA((nk,))]

    def run(self, ins, outs, sems, start):
        send_sems, recv_sems, local_sems = sems
        x, y, c = _place()
        mine = 4 * x + 2 * y + c
        for k in range(len(ins)):
            local = pltpu.make_async_copy(ins[k], outs[k].at[mine], local_sems.at[k])
            local.start() if start else local.wait()
            for flip in range(1, 8):
                px, py, pc = x ^ (flip >> 2), y ^ ((flip >> 1) & 1), c ^ (flip & 1)
                sems_f = dict(send_sem=send_sems.at[k, flip - 1], recv_sem=recv_sems.at[k, flip - 1],
                              device_id=(px, py, pc), device_id_type=MESH_ID)
                send = pltpu.make_async_remote_copy(src_ref=ins[k], dst_ref=outs[k].at[mine], **sems_f)
                if start:
                    send.start()
                else:
                    pltpu.make_async_remote_copy(src_ref=ins[k], dst_ref=outs[k].at[4 * px + 2 * py + pc], **sems_f).wait_recv()
                    send.wait_send()


class _Both:
    def __init__(self, first, second):
        self.rides = (first, second)
        self.arrays = first.arrays + second.arrays
        self.out_shape = first.out_shape + second.out_shape
        self.scratch = first.scratch + second.scratch

    def run(self, ins, outs, sems, start):
        for ride in self.rides:
            n_in, n_out, n_sem = len(ride.arrays), len(ride.out_shape), len(ride.scratch)
            ride.run(ins[:n_in], outs[:n_out], sems[:n_sem], start)
            ins, outs, sems = ins[n_in:], outs[n_out:], sems[n_sem:]


def _pack(arrays, pad_rows=F32_ROWS):
    flat = jnp.concatenate([a.reshape(-1).astype(F32) for a in arrays])
    rows = -(-flat.shape[0] // (LANES * pad_rows)) * pad_rows
    return jnp.pad(flat, (0, rows * LANES - flat.shape[0])).reshape(rows, LANES)


def _unpack(packed, shapes):
    flat, out, off = packed.reshape(-1), [], 0
    for s in shapes:
        size = 1
        for dim in s:
            size *= dim
        out.append(flat[off:off + size].reshape(s))
        off += size
    return out


def _divisor_rows(n_rows, most=256):
    best = None
    for r in range(ACT_ROWS, most + 1, ACT_ROWS):
        if n_rows % r == 0:
            best = r
    return best or n_rows


def kernel(x, meta_tokens, norm_mix_g, norm_ffn_g, final_norm_g, sc_w_in, sc_conv_w, sc_w_out, rg_w_in, rg_conv_w, rg_conv_b, rg_w_gate_a, rg_b_gate_a, rg_w_gate_x, rg_b_gate_x, rg_lambda, rg_w_out, ffn_w_up, ffn_conv_w, ffn_w_down, loss_target, m_meta_tokens, m_norm_mix_g, m_norm_ffn_g, m_final_norm_g, m_sc_w_in, m_sc_conv_w, m_sc_w_out, m_rg_w_in, m_rg_conv_w, m_rg_conv_b, m_rg_w_gate_a, m_rg_b_gate_a, m_rg_w_gate_x, m_rg_b_gate_x, m_rg_lambda, m_rg_w_out, m_ffn_w_up, m_ffn_conv_w, m_ffn_w_down, v_meta_tokens, v_norm_mix_g, v_norm_ffn_g, v_final_norm_g, v_sc_w_in, v_sc_conv_w, v_sc_w_out, v_rg_w_in, v_rg_conv_w, v_rg_conv_b, v_rg_w_gate_a, v_rg_b_gate_a, v_rg_w_gate_x, v_rg_b_gate_x, v_rg_lambda, v_rg_w_out, v_ffn_w_up, v_ffn_conv_w, v_ffn_w_down):
    weights = dict(meta_tokens=meta_tokens, norm_mix_g=norm_mix_g, norm_ffn_g=norm_ffn_g, final_norm_g=final_norm_g, sc_w_in=sc_w_in, sc_conv_w=sc_conv_w, sc_w_out=sc_w_out, rg_w_in=rg_w_in, rg_conv_w=rg_conv_w, rg_conv_b=rg_conv_b, rg_w_gate_a=rg_w_gate_a, rg_b_gate_a=rg_b_gate_a, rg_w_gate_x=rg_w_gate_x, rg_b_gate_x=rg_b_gate_x, rg_lambda=rg_lambda, rg_w_out=rg_w_out, ffn_w_up=ffn_w_up, ffn_conv_w=ffn_conv_w, ffn_w_down=ffn_w_down)
    m_in = dict(meta_tokens=m_meta_tokens, norm_mix_g=m_norm_mix_g, norm_ffn_g=m_norm_ffn_g, final_norm_g=m_final_norm_g, sc_w_in=m_sc_w_in, sc_conv_w=m_sc_conv_w, sc_w_out=m_sc_w_out, rg_w_in=m_rg_w_in, rg_conv_w=m_rg_conv_w, rg_conv_b=m_rg_conv_b, rg_w_gate_a=m_rg_w_gate_a, rg_b_gate_a=m_rg_b_gate_a, rg_w_gate_x=m_rg_w_gate_x, rg_b_gate_x=m_rg_b_gate_x, rg_lambda=m_rg_lambda, rg_w_out=m_rg_w_out, ffn_w_up=m_ffn_w_up, ffn_conv_w=m_ffn_conv_w, ffn_w_down=m_ffn_w_down)
    v_in = dict(meta_tokens=v_meta_tokens, norm_mix_g=v_norm_mix_g, norm_ffn_g=v_norm_ffn_g, final_norm_g=v_final_norm_g, sc_w_in=v_sc_w_in, sc_conv_w=v_sc_conv_w, sc_w_out=v_sc_w_out, rg_w_in=v_rg_w_in, rg_conv_w=v_rg_conv_w, rg_conv_b=v_rg_conv_b, rg_w_gate_a=v_rg_w_gate_a, rg_b_gate_a=v_rg_b_gate_a, rg_w_gate_x=v_rg_w_gate_x, rg_b_gate_x=v_rg_b_gate_x, rg_lambda=v_rg_lambda, rg_w_out=v_rg_w_out, ffn_w_up=v_ffn_w_up, ffn_conv_w=v_ffn_conv_w, ffn_w_down=v_ffn_w_down)
    names = list(weights)

    seq, d = x.shape[1:]
    tm = _row_tile(seq)
    tokens, target = _tile_order(x[0], tm), _tile_order(loss_target[0], tm)
    t_len = seq + tm
    wg_rows = 5 * tm if t_len % (5 * tm) == 0 else tm
    wg_rows_in = 13 * tm if t_len % (13 * tm) == 0 else wg_rows
    xi, yi, _ = _place()
    chip = 2 * xi + yi
    mesh_axes = ("x", "y", "c")

    wire = lambda w: w.astype(WIRE_DT)
    small_sharded = ["meta_tokens", "sc_conv_w", "rg_conv_w", "rg_conv_b", "rg_b_gate_a", "rg_b_gate_x", "rg_lambda", "ffn_conv_w"]
    small_2d = {n: weights[n].reshape(-1, weights[n].shape[-1]) for n in small_sharded}
    w_sc_in, w_sc_out, small_by_chip = _exchange(
        _GatherHalves([wire(sc_w_in[0]), wire(sc_w_out[0]), _pack([small_2d[n] for n in small_sharded], 2 * ACT_ROWS)]),
        "gather_first")
    w_sc_out = w_sc_out.reshape(-1, d)
    gather_ffn0 = _Gather([wire(ffn_w_up[0]), wire(ffn_w_down[0])])
    gather_rest = _Gather([wire(rg_w_in[0]), wire(rg_w_out[0]), wire(ffn_w_up[1]), wire(ffn_w_down[1])])
    small_len = sum(a.size for a in small_2d.values())
    by_chip = small_by_chip.reshape(N_CHIPS, -1)[:, :small_len]
    full, off = {}, 0
    for n in small_sharded:
        rows, width = small_2d[n].shape
        full[n] = by_chip[:, off:off + rows * width].reshape(N_CHIPS, rows, width).transpose(1, 0, 2).reshape(rows, N_CHIPS * width)
        off += rows * width
    sc_cw, rg_cw = full["sc_conv_w"], full["rg_conv_w"]
    ffn_cw = [full["ffn_conv_w"][0:3], full["ffn_conv_w"][3:6]]
    d_rnn = rg_cw.shape[1]
    vec = jnp.concatenate([full["rg_conv_b"], full["rg_b_gate_a"], full["rg_b_gate_x"], full["rg_lambda"],
                           jnp.zeros((F32_ROWS - 4, d_rnn), F32)])
    wa, wx = rg_w_gate_a[0].astype(MXU_DT), rg_w_gate_x[0].astype(MXU_DT)
    first = _tile_order(jnp.concatenate([jnp.zeros((tm - N_META, d), F32), full["meta_tokens"]]), tm)
    g_mix = [norm_mix_g[0:1], norm_mix_g[1:2]]
    g_ffn = [norm_ffn_g[0:1], norm_ffn_g[1:2]]

    h1, hh0, w_up0, w_dn0 = _sc_fwd(tokens, first, g_mix[0], w_sc_in, sc_cw, w_sc_out, tm=tm, ride=gather_ffn0)
    h2, hu0, hc0, w_rg_in, w_rg_out, w_up1, w_dn1 = _ffn_fwd(h1, g_ffn[0], w_up0, ffn_cw[0], w_dn0.reshape(-1, d), tm=tm,
                                                         ride=gather_rest)
    w_up, w_dn, w_rg_out = [w_up0, w_up1], [w_dn0.reshape(-1, d), w_dn1.reshape(-1, d)], w_rg_out.reshape(-1, d)
    h3, hh1, hs, gates = _rg_fwd(h2, g_mix[1], w_rg_in, rg_cw, vec, wa, wx, w_rg_out, tm=tm)
    dh4, hu1, hc1, sq, d_final = _ffn_fwd(h3, g_ffn[1], w_up[1], ffn_cw[1], w_dn[1], tm=tm,
                                     loss=(target, final_norm_g.reshape(1, d)))
    loss = lax.psum(jnp.sum(sq[0]) * (0.5 / d), mesh_axes)

    def by_chip_rows(pair):
        return [p.reshape(N_CHIPS, -1, d) for p in pair]

    def ffn_backward(dh_out, h_in, hu, hc, layer, ride):
        dh_in, act, dhu, hn, dcw, dg, *landed = _ffn_bwd(dh_out, hu, hc, h_in, g_ffn[layer], w_up[layer], ffn_cw[layer],
                                                         w_dn[layer], tm=tm, ride=ride)
        d_up = _weight_grad(hn, dhu, N_CHIPS, rows=wg_rows_in)
        d_dn = by_chip_rows(_weight_grad(act, dh_out, 1, rows=wg_rows))
        return dh_in, d_up, d_dn, dcw[0:3], dg[0], landed

    dh3, d_up1, d_dn1, d_fcw1, d_gf1, _ = ffn_backward(dh4, h3, hu1, hc1, 1, None)
    dh2, dhh1, y_rg, hn_rg, d_vec, d_wa, d_wx, d_gm1, *landed_ffn1 = _rg_bwd(
        dh3, hh1, hs, gates, h2, g_mix[1], w_rg_in, rg_cw, vec, wa, wx, w_rg_out, tm=tm,
        ride=_Scatter([d_up1[1], d_dn1[1]]))
    d_rg_in = _weight_grad(hn_rg, dhh1, N_CHIPS, rows=wg_rows_in)
    d_rg_out = by_chip_rows(_weight_grad(y_rg, dh3, 1, rows=wg_rows))
    early = {"rg_conv_w": d_vec[G_CONV_W:G_CONV_W + 4], "rg_conv_b": d_vec[G_CONV_B:G_CONV_B + 1],
             "rg_b_gate_a": d_vec[G_B_A:G_B_A + 1], "rg_b_gate_x": d_vec[G_B_X:G_B_X + 1],
             "rg_lambda": d_vec[G_LAMBDA:G_LAMBDA + 1], "ffn_conv_w.1": d_fcw1, "norm_mix_g.1": d_gm1[0:1],
             "norm_ffn_g.1": d_gf1[None], "final_norm_g": d_final[0]}
    early_packed = _pack(list(early.values()))
    gate_names = ["rg_w_gate_a", "rg_w_gate_x"]
    to_all = _AllDevices([early_packed, d_wa.reshape(-1, LANES), d_wx.reshape(-1, LANES)])
    dh1, d_up0, d_dn0, d_fcw0, d_gf0, landed = ffn_backward(
        dh2, h1, hu0, hc0, 0, _Both(_Scatter([d_rg_in[1], d_rg_out[1]]), to_all))
    landed_rg, early_by_device, gates_by_device = landed[0:2], landed[2], landed[3:]

    def core_sum(pair, received):
        own = lax.dynamic_index_in_dim(pair[0], chip, 0, keepdims=False)
        return _sum_parts(own, received, rows=_divisor_rows(own.shape[0]))

    early_big = [("rg_w_in", 0), ("rg_w_out", 0), ("ffn_w_up", 1), ("ffn_w_down", 1)]
    early_sum = [core_sum(d_rg_in, landed_rg[0]), core_sum(d_rg_out, landed_rg[1]), core_sum(d_up1, landed_ffn1[0]),
                 core_sum(d_dn1, landed_ffn1[1])]
    grad_x, dhh0, z_sc, hn_sc, d_sccw, d_gm0, d_first, *landed = _sc_bwd(
        dh1, hh0, tokens, first, g_mix[0], w_sc_in, sc_cw, w_sc_out, tm=tm,
        ride=_Both(_Scatter([d_up0[1], d_dn0[1]]), _Swap(early_sum)))
    landed_ffn0, early_other = landed[0:2], landed[2:]
    late = {"meta_tokens": _time_order(d_first, tm)[tm - N_META:], "sc_conv_w": d_sccw[0:3], "ffn_conv_w.0": d_fcw0,
            "norm_mix_g.0": d_gm0[0:1], "norm_ffn_g.0": d_gf0[None]}
    late_packed = _pack(list(late.values()))
    ffn0_big = [("ffn_w_up", 0), ("ffn_w_down", 0)]
    ffn0_sum = [core_sum(d_up0, landed_ffn0[0]), core_sum(d_dn0, landed_ffn0[1])]
    *d_sc_in, ffn0_up_other, ffn0_dn_other, late_by_device = _weight_grad(
        hn_sc, dhh0, N_CHIPS, rows=wg_rows_in, ride=_Both(_Swap(ffn0_sum), _AllDevices([late_packed])))
    *d_sc_out, landed_sc_in = _weight_grad(z_sc, dh1, 1, rows=wg_rows, ride=_Scatter([d_sc_in[1]]))
    d_sc_out = by_chip_rows(d_sc_out)
    landed_sc = [landed_sc_in, *_exchange(_Scatter([d_sc_out[1]]), "scatter_last")]
    grad_x = _time_order(grad_x, tm)[None]

    sc_big = [("sc_w_in", 0), ("sc_w_out", 0)]
    sc_sum = [core_sum(d_sc_in, landed_sc[0]), core_sum(d_sc_out, landed_sc[1])]
    sc_other = _exchange(_Swap(sc_sum), "swap_cores")
    out = {k: {} for k in ("grad", "delta", "m", "v")}
    stacked = {}
    for (n, layer), mine, theirs in zip(sc_big + ffn0_big + early_big, sc_sum + ffn0_sum + early_sum,
                                        [*sc_other, ffn0_up_other, ffn0_dn_other, *early_other]):
        stacked[n] = _adamw(weights[n], m_in[n], v_in[n], [mine, theirs], rows=_divisor_rows(mine.shape[0]), layer=layer,
                            into=stacked.get(n))
    for n, res in stacked.items():
        for k, key in enumerate(("grad", "delta", "m", "v")):
            out[key][n] = res[k]

    summed = {}
    for parts, packed, by_device in ((early, early_packed, early_by_device), (late, late_packed, late_by_device)):
        total = _sum_stack(by_device, rows=packed.shape[0])
        summed.update(zip(parts, _unpack(total, [p.shape for p in parts.values()])))
    for n in ("ffn_conv_w", "norm_mix_g", "norm_ffn_g"):
        summed[n] = jnp.concatenate([summed.pop(n + ".0"), summed.pop(n + ".1")])
    for n, by_device in zip(gate_names, gates_by_device):
        as_rows = lambda a: a.reshape(1, -1, LANES)
        res = _adamw(as_rows(weights[n]), as_rows(m_in[n]), as_rows(v_in[n]), [_sum_stack(by_device, rows=256)], rows=256)
        for k, key in enumerate(("grad", "delta", "m", "v")):
            out[key][n] = res[k].reshape(weights[n].shape)
    replicated = ["norm_mix_g", "norm_ffn_g", "final_norm_g"]
    small_names = small_sharded + replicated
    grads = {}
    for n in small_sharded:
        width = small_2d[n].shape[1]
        grads[n] = lax.dynamic_slice_in_dim(summed[n], chip * width, width, axis=1).reshape(weights[n].shape)
    for n in replicated:
        grads[n] = summed[n].reshape(weights[n].shape)
    shapes = [weights[n].shape for n in small_names]
    packed_w = _pack([weights[n] for n in small_names])
    res = _adamw(packed_w[None], _pack([m_in[n] for n in small_names])[None], _pack([v_in[n] for n in small_names])[None],
                 [_pack([grads[n] for n in small_names])], rows=packed_w.shape[0])
    for k, key in enumerate(("grad", "delta", "m", "v")):
        out[key].update(dict(zip(small_names, _unpack(res[k][0], shapes))))

    return (loss, grad_x, *[out["grad"][n] for n in names], *[out["delta"][n] for n in names],
            *[out["m"][n] for n in names], *[out["v"][n] for n in names])
```

```python
import functools

import jax
import jax.numpy as jnp
from jax import lax
from jax.experimental import pallas as pl
from jax.experimental.pallas import tpu as pltpu

F32 = jnp.float32
MXU_DT = jnp.bfloat16
ACT_DT = jnp.bfloat16
WIRE_DT = jnp.bfloat16
MESH_ID = pl.DeviceIdType.MESH

N_META = 16
RMS_EPS = 1e-6
RG_C = 8.0
ADAM_LR, ADAM_B1, ADAM_B2, ADAM_EPS, ADAM_WD, ADAM_STEP = 0.001, 0.9, 0.999, 1e-08, 0.01, 10
N_CHIPS = 4
VMEM_LIMIT = 60 * 1024 * 1024
F32_ROWS = 8
ACT_ROWS = 16
LANES = 128


def _row_tile(seq):
    for tm in (256, 128, 64, 32, 16):
        if seq % tm == 0:
            return tm
    raise ValueError(f"sequence length {seq} is not a multiple of 16")


def _params(n_axes=1, **kw):
    return pltpu.CompilerParams(dimension_semantics=("arbitrary",) * n_axes, vmem_limit_bytes=VMEM_LIMIT, **kw)


def _const(shape):
    return pl.BlockSpec(shape, lambda *_: (0,) * len(shape), pipeline_mode=pl.Buffered(1))


def _dot(a, b):
    return jnp.dot(a, b, preferred_element_type=F32)


def _dot_nt(a, b):
    return lax.dot_general(a, b, (((1,), (1,)), ((), ())), preferred_element_type=F32)


def _dot_tn(a, b):
    return lax.dot_general(a, b, (((0,), (0,)), ((), ())), preferred_element_type=F32)


def _sigmoid(x):
    return 0.5 + 0.5 * jnp.tanh(0.5 * x)


def _rms(h, g):
    rstd = lax.rsqrt(jnp.mean(h * h, axis=-1, keepdims=True) + RMS_EPS)
    xhat = h * rstd
    return xhat * g, xhat, rstd


def _rms_bwd(dhn, xhat, rstd, g):
    dx = dhn * g
    return rstd * (dx - xhat * jnp.mean(dx * xhat, axis=-1, keepdims=True))


def _gelu(x):
    k = 0.7978845608028654
    t = jnp.tanh(k * (x + 0.044715 * x * x * x))
    return 0.5 * x * (1.0 + t), t


def _gelu_grad(x, t):
    k = 0.7978845608028654
    return 0.5 * (1.0 + t) + 0.5 * x * (1.0 - t * t) * k * (1.0 + 3 * 0.044715 * x * x)


def _softplus(x):
    e = jnp.exp(-jnp.abs(x))
    return jnp.maximum(x, 0.0) + jnp.where(e < 1e-4, e - 0.5 * e * e, jnp.log(1.0 + e))


def _expm1_neg(z, exp_z):
    series = z * (1.0 + z * (0.5 + z * (1.0 / 6)))
    return jnp.where(z > -0.02, series, exp_z - 1.0)


def _tile_order(a, tm):
    return a.reshape(-1, F32_ROWS, tm // F32_ROWS, a.shape[-1]).swapaxes(1, 2).reshape(a.shape)


def _time_order(a, tm):
    return a.reshape(-1, tm // F32_ROWS, F32_ROWS, a.shape[-1]).swapaxes(1, 2).reshape(a.shape)


def _valid_rows(tile, tm):
    row = lax.broadcasted_iota(jnp.int32, (tm, 1), 0)
    time = (row & (F32_ROWS - 1)) * (tm // F32_ROWS) + (row >> 3) + tile * tm
    return time >= tm - N_META


def _sublane():
    return lax.broadcasted_iota(jnp.int32, (F32_ROWS, 1), 0)


def _past_rows(width):
    return (width - 1) * F32_ROWS


def _halo_block(past, tm, nt):
    rows = -(-past // ACT_ROWS) * ACT_ROWS
    return rows, lambda i: (jnp.maximum((nt - 1 - i) * (tm // rows) - 1, 0), 0)


def _link_past(buf, cols, width, tm):
    past = _past_rows(width)
    for k in range(1, width):
        rows = pl.ds(past - F32_ROWS * k, F32_ROWS)
        before = pltpu.roll(buf[rows, cols], 1, 0)
        mine = pltpu.roll(buf[pl.ds(past + tm - F32_ROWS * k, F32_ROWS), cols], 1, 0)
        buf[rows, cols] = jnp.where(_sublane() == 0, before, mine)


def _link_future(buf, cols, width, tm):
    for k in range(1, width):
        rows = pl.ds(tm + F32_ROWS * (k - 1), F32_ROWS)
        after = pltpu.roll(buf[rows, cols], F32_ROWS - 1, 0)
        mine = pltpu.roll(buf[pl.ds(F32_ROWS * (k - 1), F32_ROWS), cols], F32_ROWS - 1, 0)
        buf[rows, cols] = jnp.where(_sublane() == F32_ROWS - 1, after, mine)


def _conv_taps(buf, cols, width, tm):
    return [buf[pl.ds(F32_ROWS * k, tm), cols] for k in range(width)]


def _conv_back(buf, cw_ref, cols, width, tm):
    return sum(cw_ref[k:k + 1, cols] * buf[pl.ds(F32_ROWS * (width - 1 - k), tm), cols] for k in range(width))


ANY = pl.BlockSpec(memory_space=pl.ANY)


def _place():
    return lax.axis_index("x"), lax.axis_index("y"), lax.axis_index("c")


def _other_chips(x, y):
    return [(1 - x, y), (x, 1 - y), (1 - x, 1 - y)]


class _Gather:
    def __init__(self, shards):
        nk = len(shards)
        self.arrays = list(shards)
        self.out_shape = [jax.ShapeDtypeStruct((N_CHIPS,) + s.shape, s.dtype) for s in shards]
        self.scratch = [pltpu.SemaphoreType.DMA((nk, 3)), pltpu.SemaphoreType.DMA((nk, 3)), pltpu.SemaphoreType.DMA((nk,))]

    def run(self, ins, outs, sems, start):
        send_sems, recv_sems, local_sems = sems
        x, y, c = _place()
        mine = 2 * x + y
        for k in range(len(ins)):
            local = pltpu.make_async_copy(ins[k], outs[k].at[mine], local_sems.at[k])
            local.start() if start else local.wait()
            for j, (px, py) in enumerate(_other_chips(x, y)):
                sems_kj = dict(send_sem=send_sems.at[k, j], recv_sem=recv_sems.at[k, j], device_id=(px, py, c),
                               device_id_type=MESH_ID)
                send = pltpu.make_async_remote_copy(src_ref=ins[k], dst_ref=outs[k].at[mine], **sems_kj)
                if start:
                    send.start()
                else:
                    pltpu.make_async_remote_copy(src_ref=ins[k], dst_ref=outs[k].at[2 * px + py], **sems_kj).wait_recv()
                    send.wait_send()


class _GatherHalves:
    def __init__(self, shards):
        nk = len(shards)
        self.arrays = list(shards)
        self.out_shape = [jax.ShapeDtypeStruct((N_CHIPS,) + s.shape, s.dtype) for s in shards]
        self.scratch = [pltpu.SemaphoreType.DMA((nk, 3)) for _ in range(4)] + [pltpu.SemaphoreType.DMA((nk,))]

    def run(self, ins, outs, sems, start):
        far_send, far_recv, near_send, near_recv, local_sems = sems
        x, y, c = _place()
        mine = 2 * x + y
        for phase in ((0,) if start else (1, 2)):
            for k in range(len(ins)):
                half = ins[k].shape[0] // 2
                my_half = pl.ds(pl.multiple_of(c * half, ACT_ROWS), half)
                other_half = pl.ds(pl.multiple_of((1 - c) * half, ACT_ROWS), half)
                if phase != 1:
                    local = pltpu.make_async_copy(ins[k], outs[k].at[mine], local_sems.at[k])
                    local.start() if phase == 0 else local.wait()
                for j, (px, py) in enumerate(_other_chips(x, y)):
                    theirs = 2 * px + py
                    far = dict(send_sem=far_send.at[k, j], recv_sem=far_recv.at[k, j], device_id=(px, py, c),
                               device_id_type=MESH_ID)
                    near = dict(send_sem=near_send.at[k, j], recv_sem=near_recv.at[k, j], device_id=(x, y, 1 - c),
                                device_id_type=MESH_ID)
                    landed = outs[k].at[theirs, my_half]
                    send = lambda: pltpu.make_async_remote_copy(src_ref=ins[k].at[my_half], dst_ref=outs[k].at[mine, my_half], **far)
                    pass_on = lambda: pltpu.make_async_remote_copy(src_ref=landed, dst_ref=landed, **near)
                    if phase == 0:
                        send().start()
                    elif phase == 1:
                        pltpu.make_async_remote_copy(src_ref=ins[k].at[my_half], dst_ref=landed, **far).wait_recv()
                        pass_on().start()
                    else:
                        pltpu.make_async_remote_copy(src_ref=landed, dst_ref=outs[k].at[theirs, other_half], **near).wait_recv()
                        pass_on().wait_send()
                        send().wait_send()


class _Scatter:
    def __init__(self, parts):
        nk = len(parts)
        self.arrays = list(parts)
        self.out_shape = [jax.ShapeDtypeStruct((3,) + p.shape[1:], p.dtype) for p in parts]
        self.scratch = [pltpu.SemaphoreType.DMA((nk, 3)), pltpu.SemaphoreType.DMA((nk, 3))]

    def run(self, ins, outs, sems, start):
        send_sems, recv_sems = sems
        x, y, c = _place()
        for k in range(len(ins)):
            for j, (px, py) in enumerate(_other_chips(x, y)):
                send = pltpu.make_async_remote_copy(
                    src_ref=ins[k].at[2 * px + py], dst_ref=outs[k].at[j], send_sem=send_sems.at[k, j],
                    recv_sem=recv_sems.at[k, j], device_id=(px, py, c), device_id_type=MESH_ID)
                if start:
                    send.start()
                else:
                    send.wait_recv()
                    send.wait_send()


def _exchange(ride, name):
    n_in, n_out = len(ride.arrays), len(ride.out_shape)

    def body(*refs):
        ride.run(refs[:n_in], refs[n_in:n_in + n_out], refs[n_in + n_out:], start=True)
        ride.run(refs[:n_in], refs[n_in:n_in + n_out], refs[n_in + n_out:], start=False)

    return pl.pallas_call(body, name=name, in_specs=[ANY] * n_in, out_specs=[ANY] * n_out, out_shape=ride.out_shape,
                          scratch_shapes=ride.scratch)(*ride.arrays)


def _launch(body, operands, *, name, grid, in_specs, out_specs, out_shape, scratch_shapes=(), ride=None):
    common = dict(name=name, grid=grid, compiler_params=_params(len(grid)))
    if ride is None:
        return pl.pallas_call(body, in_specs=in_specs, out_specs=out_specs, out_shape=out_shape,
                              scratch_shapes=list(scratch_shapes), **common)(*operands)
    n_in, n_out, n_scr = len(operands), len(out_shape), len(scratch_shapes)
    r_in, r_out = len(ride.arrays), len(ride.out_shape)

    def riding(*refs):
        ins, refs = refs[:n_in], refs[n_in:]
        r_ins, refs = refs[:r_in], refs[r_in:]
        outs, refs = refs[:n_out], refs[n_out:]
        r_outs, refs = refs[:r_out], refs[r_out:]
        scr, r_sems = refs[:n_scr], refs[n_scr:]
        step = [pl.program_id(axis) for axis in range(len(grid))]
        first = functools.reduce(jnp.logical_and, [s == 0 for s in step])
        last = functools.reduce(jnp.logical_and, [s == size - 1 for s, size in zip(step, grid)])

        @pl.when(first)
        def _():
            ride.run(r_ins, r_outs, r_sems, start=True)

        body(*ins, *outs, *scr)

        @pl.when(last)
        def _():
            ride.run(r_ins, r_outs, r_sems, start=False)

    return pl.pallas_call(
        riding, in_specs=list(in_specs) + [ANY] * r_in, out_specs=list(out_specs) + [ANY] * r_out,
        out_shape=list(out_shape) + ride.out_shape, scratch_shapes=list(scratch_shapes) + ride.scratch, **common,
    )(*operands, *ride.arrays)


def _sc_fwd(x, first, g, w_in, cw, w_out, *, tm, ride=None):
    seq, d = x.shape
    nt = seq // tm + 1
    nq, _, n = w_in.shape
    width = cw.shape[0]
    past = _past_rows(width)

    def body(x_ref, first_ref, g_ref, win_ref, cw_ref, wout_ref, h1_ref, hh_ref, hh_scr, cbuf):
        i = pl.program_id(0)

        @pl.when(i == 0)
        def _():
            cbuf[pl.ds(0, past), :] = jnp.zeros((past, d), F32)

        h = jnp.where(i == 0, first_ref[...], x_ref[...])
        hn = _rms(h, g_ref[...])[0].astype(MXU_DT)
        for q in range(nq):
            hh_scr[:, q * n:(q + 1) * n] = _dot(hn, win_ref[q])
        hh_ref[...] = hh_scr[...].astype(hh_ref.dtype)
        b = hh_scr[:, 0:d]
        cbuf[pl.ds(past, tm), :] = hh_scr[:, d:2 * d] * hh_scr[:, 2 * d:3 * d]
        last = cbuf[pl.ds(tm, past), :]
        _link_past(cbuf, slice(None), width, tm)
        u = sum(cw_ref[k:k + 1, :] * tap for k, tap in enumerate(_conv_taps(cbuf, slice(None), width, tm)))
        cbuf[pl.ds(0, past), :] = last
        h1_ref[...] = h + _dot((b * u).astype(MXU_DT), wout_ref[...])

    return _launch(
        body, [x, first, g, w_in, cw, w_out], name="sc_fwd", grid=(nt,),
        in_specs=[pl.BlockSpec((tm, d), lambda i: (jnp.maximum(i - 1, 0), 0)), _const((tm, d)), _const((1, d)),
                  _const(w_in.shape), _const(cw.shape), _const(w_out.shape)],
        out_specs=[pl.BlockSpec((tm, d), lambda i: (i, 0)), pl.BlockSpec((tm, nq * n), lambda i: (i, 0))],
        out_shape=[jax.ShapeDtypeStruct((nt * tm, d), F32), jax.ShapeDtypeStruct((nt * tm, nq * n), ACT_DT)],
        scratch_shapes=[pltpu.VMEM((tm, nq * n), F32), pltpu.VMEM((past + tm, d), F32)],
        ride=ride,
    )


def _sc_bwd(dh, hh, x, first, g, w_in, cw, w_out, *, tm, ride=None):
    t_len, d = dh.shape
    nt = t_len // tm
    nq, _, n = w_in.shape
    width = cw.shape[0]
    past = _past_rows(width)
    halo_rows, halo_index = _halo_block(past, tm, nt)

    def body(dh_ref, hh_ref, hhp_ref, x_ref, first_ref, g_ref, win_ref, cw_ref, wout_ref,
             dx_ref, dhh_ref, z_ref, hn_ref, dcw_ref, dg_ref, dfirst_ref, cbuf, dbuf):
        i = pl.program_id(0)
        r = nt - 1 - i

        @pl.when(i == 0)
        def _():
            dbuf[pl.ds(tm, past), :] = jnp.zeros((past, d), F32)
            dcw_ref[...] = jnp.zeros_like(dcw_ref)
            dg_ref[...] = jnp.zeros_like(dg_ref)

        dh_out = dh_ref[...]
        b = hh_ref[:, 0:d].astype(F32)
        c = hh_ref[:, d:2 * d].astype(F32)
        v = hh_ref[:, 2 * d:3 * d].astype(F32)
        prev = hhp_ref[...].astype(F32)[halo_rows - past:, :]
        cbuf[pl.ds(0, past), :] = jnp.where(r > 0, prev[:, d:2 * d] * prev[:, 2 * d:3 * d], 0.0)
        cbuf[pl.ds(past, tm), :] = c * v
        _link_past(cbuf, slice(None), width, tm)
        taps = _conv_taps(cbuf, slice(None), width, tm)
        u = sum(cw_ref[k:k + 1, :] * taps[k] for k in range(width))
        z_ref[...] = (b * u).astype(z_ref.dtype)
        dz = _dot_nt(dh_out.astype(MXU_DT), wout_ref[...])
        d_b = (dz * u).astype(dhh_ref.dtype)
        dhh_ref[:, 0:d] = d_b
        parts = [_dot_nt(d_b[:, 0:n], win_ref[0])]
        du = dz * b
        for k in range(width):
            dcw_ref[k:k + 1, :] += jnp.sum(taps[k] * du, axis=0, keepdims=True)
        dbuf[pl.ds(0, tm), :] = du
        _link_future(dbuf, slice(None), width, tm)
        dcv = _conv_back(dbuf, cw_ref, slice(None), width, tm)
        dbuf[pl.ds(tm, past), :] = dbuf[pl.ds(0, past), :]
        d_c, d_v = (dcv * v).astype(dhh_ref.dtype), (dcv * c).astype(dhh_ref.dtype)
        dhh_ref[:, d:2 * d] = d_c
        dhh_ref[:, 2 * d:3 * d] = d_v
        rest = jnp.concatenate([d_b[:, n:], d_c, d_v], axis=1)
        parts += [_dot_nt(rest[:, (q - 1) * n:q * n], win_ref[q]) for q in range(1, nq)]
        dhn = functools.reduce(lambda a, b: a + b, parts)
        h_in = jnp.where(r == 0, first_ref[...], x_ref[...])
        dh_in = _norm_bwd_tile(dhn, h_in, dh_out, g_ref[...], _valid_rows(r, tm), hn_ref, dg_ref)

        @pl.when(r == 0)
        def _():
            dfirst_ref[...] = dh_in

        @pl.when(r > 0)
        def _():
            dx_ref[...] = dh_in

    rev = lambda i: (nt - 1 - i, 0)
    rev_x = lambda i: (jnp.maximum(nt - 2 - i, 0), 0)
    return _launch(
        body, [dh, hh, hh, x, first, g, w_in, cw, w_out], name="sc_bwd", grid=(nt,),
        in_specs=[pl.BlockSpec((tm, d), rev), pl.BlockSpec((tm, 3 * d), rev), pl.BlockSpec((halo_rows, 3 * d), halo_index),
                  pl.BlockSpec((tm, d), rev_x), _const((tm, d)), _const((1, d)), _const(w_in.shape), _const(cw.shape),
                  _const(w_out.shape)],
        out_specs=[pl.BlockSpec((tm, d), rev_x), pl.BlockSpec((tm, 3 * d), rev), pl.BlockSpec((tm, d), rev),
                   pl.BlockSpec((tm, d), rev), _const((F32_ROWS, d)), _const((F32_ROWS, d)), _const((tm, d))],
        out_shape=[jax.ShapeDtypeStruct((t_len - tm, d), F32), jax.ShapeDtypeStruct((t_len, 3 * d), ACT_DT),
                   jax.ShapeDtypeStruct((t_len, d), ACT_DT), jax.ShapeDtypeStruct((t_len, d), ACT_DT),
                   jax.ShapeDtypeStruct((F32_ROWS, d), F32), jax.ShapeDtypeStruct((F32_ROWS, d), F32),
                   jax.ShapeDtypeStruct((tm, d), F32)],
        scratch_shapes=[pltpu.VMEM((past + tm, d), F32), pltpu.VMEM((tm + past, d), F32)],
        ride=ride,
    )


def _ffn_fwd(h, g, w_up, cw, w_down, *, tm, ride=None, loss=None):
    t_len, d = h.shape
    nt = t_len // tm
    nq, _, n = w_up.shape
    width = cw.shape[0]
    past = _past_rows(width)

    def body(h_ref, g_ref, wup_ref, cw_ref, wdn_ref, *rest):
        if loss is None:
            out_ref, hu_ref, hc_ref, ubuf, tail = rest
        else:
            t_ref, gf_ref, out_ref, hu_ref, hc_ref, sq_ref, dgf_ref, ubuf, tail = rest
        i = pl.program_id(0)

        @pl.when(i == 0)
        def _():
            tail[...] = jnp.zeros_like(tail)

        h_in = h_ref[...]
        hn = _rms(h_in, g_ref[...])[0].astype(MXU_DT)
        ubuf[pl.ds(0, past), :] = tail[...]
        for q in range(nq):
            ubuf[pl.ds(past, tm), q * n:(q + 1) * n] = _dot(hn, wup_ref[q])
        hu_ref[...] = ubuf[pl.ds(past, tm), :].astype(hu_ref.dtype)
        tail[...] = ubuf[pl.ds(tm, past), :]
        _link_past(ubuf, slice(None), width, tm)
        acc = h_in
        for j in range(nq // 2):
            gcol, vcol = slice(j * n, (j + 1) * n), slice((nq // 2 + j) * n, (nq // 2 + j + 1) * n)
            conv = lambda cols: sum(cw_ref[k:k + 1, cols] * tap for k, tap in enumerate(_conv_taps(ubuf, cols, width, tm)))
            gj, vj = conv(gcol), conv(vcol)
            hc_ref[:, gcol] = gj.astype(hc_ref.dtype)
            hc_ref[:, vcol] = vj.astype(hc_ref.dtype)
            acc = acc + _dot((gj * _sigmoid(gj) * vj).astype(MXU_DT), wdn_ref[j * n:(j + 1) * n, :])
        if loss is None:
            out_ref[...] = acc
            return

        @pl.when(i == 0)
        def _():
            sq_ref[...] = jnp.zeros_like(sq_ref)
            dgf_ref[...] = jnp.zeros_like(dgf_ref)
            out_ref[...] = jnp.zeros_like(out_ref)

        @pl.when(i > 0)
        def _():
            gain = gf_ref[...]
            out, xhat, rstd = _rms(acc, gain)
            err = out - t_ref[...]
            sq_ref[0:1, :] += jnp.sum(err * err, axis=0, keepdims=True)
            dout = err * (1.0 / d)
            dgf_ref[0:1, :] += jnp.sum(dout * xhat, axis=0, keepdims=True)
            out_ref[...] = _rms_bwd(dout, xhat, rstd, gain)

    row = lambda i: (i, 0)
    stat = jax.ShapeDtypeStruct((F32_ROWS, d), F32)
    return _launch(
        body, [h, g, w_up, cw, w_down] + list(loss or ()), name="ffn_fwd", grid=(nt,),
        in_specs=[pl.BlockSpec((tm, d), row), _const((1, d)), _const(w_up.shape), _const(cw.shape), _const(w_down.shape)]
        + ([pl.BlockSpec((tm, d), lambda i: (jnp.maximum(i - 1, 0), 0)), _const((1, d))] if loss else []),
        out_specs=[pl.BlockSpec((tm, d), row), pl.BlockSpec((tm, nq * n), row), pl.BlockSpec((tm, nq * n), row)]
        + ([_const(stat.shape)] * 2 if loss else []),
        out_shape=[jax.ShapeDtypeStruct((t_len, d), F32), jax.ShapeDtypeStruct((t_len, nq * n), ACT_DT),
                   jax.ShapeDtypeStruct((t_len, nq * n), ACT_DT)] + ([stat, stat] if loss else []),
        scratch_shapes=[pltpu.VMEM((past + tm, nq * n), F32), pltpu.VMEM((past, nq * n), F32)],
        ride=ride,
    )


def _norm_bwd_tile(dhn, h_in, dh, gain, valid, hn_ref, dg_ref):
    hn, xhat, rstd = _rms(h_in, gain)
    hn_ref[...] = hn.astype(hn_ref.dtype)
    dg_ref[0:1, :] += jnp.sum(dhn * xhat, axis=0, keepdims=True)
    return jnp.where(valid, dh + _rms_bwd(dhn, xhat, rstd, gain), 0.0)


def _ffn_bwd(dh, hu, hc, h, g, w_up, cw, w_down, *, tm, ride=None):
    t_len, d = dh.shape
    nt = t_len // tm
    ff = hu.shape[1]
    n = ff // 4
    width = cw.shape[0]
    past = _past_rows(width)
    halo_rows, halo_index = _halo_block(past, tm, nt)

    def body(dh_ref, hu_ref, hup_ref, hc_ref, h_ref, g_ref, wup_ref, cw_ref, wdn_ref,
             dhin_ref, a_ref, dhu_ref, hn_ref, dcw_ref, dg_ref, ubuf, dbuf, head):
        i = pl.program_id(0)
        r = nt - 1 - i

        @pl.when(i == 0)
        def _():
            head[...] = jnp.zeros_like(head)
            dcw_ref[...] = jnp.zeros_like(dcw_ref)
            dg_ref[...] = jnp.zeros_like(dg_ref)

        dh_out = dh_ref[...]
        dhb = dh_out.astype(MXU_DT)
        dhn_parts = []
        d_act = [_dot_nt(dhb, wdn_ref[j * n:(j + 1) * n, :]) for j in range(2)]
        for j in range(2):
            mine = slice(0, n), slice(n, 2 * n)
            full = slice(j * n, (j + 1) * n), slice((2 + j) * n, (3 + j) * n)
            for here, there in zip(mine, full):
                prev = hup_ref[:, there].astype(F32)[halo_rows - past:, :]
                ubuf[pl.ds(0, past), here] = jnp.where(r > 0, prev, 0.0)
                ubuf[pl.ds(past, tm), here] = hu_ref[:, there].astype(F32)
                dbuf[pl.ds(tm, past), here] = head[:, there]
            _link_past(ubuf, slice(None), width, tm)
            gj, vj = hc_ref[:, full[0]].astype(F32), hc_ref[:, full[1]].astype(F32)
            sg = _sigmoid(gj)
            s = gj * sg
            a_ref[:, full[0]] = (s * vj).astype(a_ref.dtype)
            da = d_act[j]
            dbuf[pl.ds(0, tm), mine[1]] = da * s
            dbuf[pl.ds(0, tm), mine[0]] = da * vj * (sg * (1.0 + gj * (1.0 - sg)))
            for here, there in zip(mine, full):
                head[:, there] = dbuf[pl.ds(0, past), here]
            _link_future(dbuf, slice(None), width, tm)
            for here, there in zip(mine, full):
                dy = dbuf[pl.ds(0, tm), here]
                for k, tap in enumerate(_conv_taps(ubuf, here, width, tm)):
                    dcw_ref[k:k + 1, there] += jnp.sum(tap * dy, axis=0, keepdims=True)
                dhu = sum(cw_ref[k:k + 1, there] * dbuf[pl.ds(F32_ROWS * (width - 1 - k), tm), here]
                          for k in range(width)).astype(dhu_ref.dtype)
                dhu_ref[:, there] = dhu
                dhn_parts.append(_dot_nt(dhu, wup_ref[there.start // n]))
        dhn = (dhn_parts[0] + dhn_parts[1]) + (dhn_parts[2] + dhn_parts[3])
        dhin_ref[...] = _norm_bwd_tile(dhn, h_ref[...], dh_out, g_ref[...], _valid_rows(r, tm), hn_ref, dg_ref)

    rev = lambda i: (nt - 1 - i, 0)
    return _launch(
        body, [dh, hu, hu, hc, h, g, w_up, cw, w_down], name="ffn_bwd", grid=(nt,),
        in_specs=[pl.BlockSpec((tm, d), rev), pl.BlockSpec((tm, ff), rev), pl.BlockSpec((halo_rows, ff), halo_index),
                  pl.BlockSpec((tm, ff), rev), pl.BlockSpec((tm, d), rev), _const((1, d)), _const(w_up.shape), _const(cw.shape), _const(w_down.shape)],
        out_specs=[pl.BlockSpec((tm, d), rev), pl.BlockSpec((tm, 2 * n), rev), pl.BlockSpec((tm, ff), rev),
                   pl.BlockSpec((tm, d), rev), _const((F32_ROWS, ff)), _const((F32_ROWS, d))],
        out_shape=[jax.ShapeDtypeStruct((t_len, d), F32), jax.ShapeDtypeStruct((t_len, 2 * n), ACT_DT),
                   jax.ShapeDtypeStruct((t_len, ff), ACT_DT), jax.ShapeDtypeStruct((t_len, d), ACT_DT),
                   jax.ShapeDtypeStruct((F32_ROWS, ff), F32), jax.ShapeDtypeStruct((F32_ROWS, d), F32)],
        scratch_shapes=[pltpu.VMEM((past + tm, 2 * n), F32), pltpu.VMEM((tm + past, 2 * n), F32), pltpu.VMEM((past, ff), F32)],
        ride=ride,
    )


V_CONV_B, V_B_A, V_B_X, V_LAMBDA = 0, 1, 2, 3
G_CONV_W, G_CONV_B, G_B_A, G_B_X, G_LAMBDA = 0, 4, 5, 6, 7


def _scan(a_ref, b_ref, edge, tm, reverse):
    nj = tm // F32_ROWS
    order = range(nj - 1, -1, -1) if reverse else range(nj)
    slab = lambda ref, j: ref[pl.ds(F32_ROWS * j, F32_ROWS), :]
    a_run = b_run = None
    for j in order:
        a_j, b_j = slab(a_ref, j), slab(b_ref, j)
        if a_run is not None:
            b_j = b_j + a_j * b_run
            a_j = a_j * a_run
            b_ref[pl.ds(F32_ROWS * j, F32_ROWS), :] = b_j
            a_ref[pl.ds(F32_ROWS * j, F32_ROWS), :] = a_j
        a_run, b_run = a_j, b_j
    sub = _sublane()
    shift = 1
    while shift < F32_ROWS:
        amount = F32_ROWS - shift if reverse else shift
        keep = (sub < F32_ROWS - shift) if reverse else (sub >= shift)
        b_run = jnp.where(keep, b_run + a_run * pltpu.roll(b_run, amount, 0), b_run)
        a_run = jnp.where(keep, a_run * pltpu.roll(a_run, amount, 0), a_run)
        shift *= 2
    outer = edge[0:1, :] if reverse else edge[F32_ROWS - 1:F32_ROWS, :]
    ends = b_run + a_run * outer
    if reverse:
        carry = jnp.where(sub == F32_ROWS - 1, outer, pltpu.roll(ends, F32_ROWS - 1, 0))
    else:
        carry = jnp.where(sub == 0, outer, pltpu.roll(ends, 1, 0))
    for j in range(nj):
        b_ref[pl.ds(F32_ROWS * j, F32_ROWS), :] = slab(b_ref, j) + slab(a_ref, j) * carry
    return slab(b_ref, 0 if reverse else nj - 1)


def _rg_gates(u, vec_ref, wa_ref, wx_ref, pre_scr, nb, bd):
    ub = u.astype(MXU_DT)
    for k in range(nb):
        blk = slice(k * bd, (k + 1) * bd)
        pre_scr[0, :, blk] = _dot(ub[:, blk], wa_ref[k])
        pre_scr[1, :, blk] = _dot(ub[:, blk], wx_ref[k])
    r_gate = _sigmoid(pre_scr[0] + vec_ref[V_B_A:V_B_A + 1, :])
    i_gate = _sigmoid(pre_scr[1] + vec_ref[V_B_X:V_B_X + 1, :])
    return r_gate, i_gate


def _rg_decay(r_gate, vec_ref):
    sp = _softplus(-vec_ref[V_LAMBDA:V_LAMBDA + 1, :])
    log_a = -RG_C * r_gate * sp
    a = jnp.exp(log_a)
    one_minus_a2 = jnp.maximum(-_expm1_neg(2.0 * log_a, a * a), 1e-30)
    inv_mult = lax.rsqrt(one_minus_a2)
    return a, one_minus_a2 * inv_mult, inv_mult, sp


def _rg_fwd(h, g, w_in, cw, vec, wa, wx, w_out, *, tm):
    t_len, d = h.shape
    nt = t_len // tm
    nq, _, n = w_in.shape
    dr = 2 * n
    width = cw.shape[0]
    past = _past_rows(width)
    nb, bd, _ = wa.shape

    def body(h_ref, g_ref, win_ref, cw_ref, vec_ref, wa_ref, wx_ref, wout_ref, out_ref, hh_ref, hs_ref, gates_ref,
             gbuf, rbuf, pre_scr, tail, edge):
        i = pl.program_id(0)

        @pl.when(i == 0)
        def _():
            tail[...] = jnp.zeros_like(tail)
            edge[...] = jnp.zeros_like(edge)

        h_in = h_ref[...]
        hn = _rms(h_in, g_ref[...])[0].astype(MXU_DT)
        rbuf[pl.ds(0, past), :] = tail[...]
        for q in range(2):
            gbuf[:, q * n:(q + 1) * n] = _dot(hn, win_ref[q])
            rbuf[pl.ds(past, tm), q * n:(q + 1) * n] = _dot(hn, win_ref[2 + q])
        hh_ref[:, 0:dr] = gbuf[...].astype(hh_ref.dtype)
        hh_ref[:, dr:2 * dr] = rbuf[pl.ds(past, tm), :].astype(hh_ref.dtype)
        tail[...] = rbuf[pl.ds(tm, past), :]
        _link_past(rbuf, slice(None), width, tm)
        taps = _conv_taps(rbuf, slice(None), width, tm)
        u = sum(cw_ref[k:k + 1, :] * taps[k] for k in range(width)) + vec_ref[V_CONV_B:V_CONV_B + 1, :]
        r_gate, i_gate = _rg_gates(u, vec_ref, wa_ref, wx_ref, pre_scr, nb, bd)
        for k, kept in enumerate((u, r_gate, i_gate)):
            gates_ref[:, k * dr:(k + 1) * dr] = kept.astype(gates_ref.dtype)
        a, mult, _, _ = _rg_decay(r_gate, vec_ref)
        hs_ref[:, dr:2 * dr] = a
        hs_ref[:, 2 * dr:3 * dr] = mult
        pre_scr[0] = a
        pre_scr[1] = jnp.where(_valid_rows(i, tm), mult * (i_gate * u), 0.0)
        edge[...] = _scan(pre_scr.at[0], pre_scr.at[1], edge[...], tm, reverse=False)
        hs = pre_scr[1]
        hs_ref[:, 0:dr] = hs
        y = hs * _gelu(gbuf[...])[0]
        out_ref[...] = h_in + _dot(y.astype(MXU_DT), wout_ref[...])

    row = lambda i: (i, 0)
    return pl.pallas_call(
        body, name="rg_fwd", grid=(nt,),
        in_specs=[pl.BlockSpec((tm, d), row), _const((1, d)), _const(w_in.shape), _const(cw.shape), _const(vec.shape),
                  _const(wa.shape), _const(wx.shape), _const(w_out.shape)],
        out_specs=[pl.BlockSpec((tm, d), row), pl.BlockSpec((tm, 2 * dr), row), pl.BlockSpec((tm, 3 * dr), row),
                   pl.BlockSpec((tm, 3 * dr), row)],
        out_shape=[jax.ShapeDtypeStruct((t_len, d), F32), jax.ShapeDtypeStruct((t_len, 2 * dr), ACT_DT),
                   jax.ShapeDtypeStruct((t_len, 3 * dr), F32), jax.ShapeDtypeStruct((t_len, 3 * dr), ACT_DT)],
        scratch_shapes=[pltpu.VMEM((tm, dr), F32), pltpu.VMEM((past + tm, dr), F32), pltpu.VMEM((2, tm, dr), F32),
                        pltpu.VMEM((past, dr), F32), pltpu.VMEM((F32_ROWS, dr), F32)],
        compiler_params=_params(),
    )(h, g, w_in, cw, vec, wa, wx, w_out)


def _rg_bwd(dh, hh, hs, gates, h, g, w_in, cw, vec, wa, wx, w_out, *, tm, ride=None):
    t_len, d = dh.shape
    nt = t_len // tm
    dr = hs.shape[1] // 3
    n = w_in.shape[2]
    width = cw.shape[0]
    nb, bd, _ = wa.shape
    past = _past_rows(width)
    halo_rows, halo_index = _halo_block(past, tm, nt)
    one = F32_ROWS

    def body(dh_ref, hh_ref, hhp_ref, hs_ref, hsp_ref, gates_ref, h_ref, g_ref, win_ref, cw_ref, vec_ref, wa_ref, wx_ref, wout_ref,
             dhin_ref, dhh_ref, y_ref, hn_ref, dvec_ref, dwa_ref, dwx_ref, dg_ref, rbuf, dbuf, pre_scr, hbuf, abuf, edge):
        i = pl.program_id(0)
        r = nt - 1 - i

        @pl.when(i == 0)
        def _():
            dbuf[pl.ds(tm, past), :] = jnp.zeros((past, dr), F32)
            abuf[pl.ds(tm, one), :] = jnp.zeros((one, dr), F32)
            edge[...] = jnp.zeros_like(edge)
            dvec_ref[...] = jnp.zeros_like(dvec_ref)
            dwa_ref[...] = jnp.zeros_like(dwa_ref)
            dwx_ref[...] = jnp.zeros_like(dwx_ref)
            dg_ref[...] = jnp.zeros_like(dg_ref)

        dh_out = dh_ref[...]
        gb = hh_ref[:, 0:dr].astype(F32)
        prev = hhp_ref[...].astype(F32)[halo_rows - past:, dr:2 * dr]
        rbuf[pl.ds(0, past), :] = jnp.where(r > 0, prev, 0.0)
        rbuf[pl.ds(past, tm), :] = hh_ref[:, dr:2 * dr].astype(F32)
        _link_past(rbuf, slice(None), width, tm)
        taps = _conv_taps(rbuf, slice(None), width, tm)
        ub = gates_ref[:, 0:dr].astype(MXU_DT)
        u, r_gate, i_gate = (gates_ref[:, k * dr:(k + 1) * dr].astype(F32) for k in range(3))
        hs_t, a, mult = (hs_ref[:, k * dr:(k + 1) * dr] for k in range(3))
        inv_mult = 1.0 / mult
        sp = _softplus(-vec_ref[V_LAMBDA:V_LAMBDA + 1, :])
        hbuf[pl.ds(0, one), :] = jnp.where(r > 0, hsp_ref[...], 0.0)
        hbuf[pl.ds(one, tm), :] = hs_t
        _link_past(hbuf, slice(None), 2, tm)
        h_prev = hbuf[pl.ds(0, tm), :]
        gate, th = _gelu(gb)
        y_ref[...] = (hs_t * gate).astype(y_ref.dtype)
        dy = _dot_nt(dh_out.astype(MXU_DT), wout_ref[...])
        d_gb = (dy * hs_t * _gelu_grad(gb, th)).astype(dhh_ref.dtype)
        dhh_ref[:, 0:dr] = d_gb
        dhn = sum(_dot_nt(d_gb[:, q * n:(q + 1) * n], win_ref[q]) for q in range(2))
        abuf[pl.ds(0, tm), :] = a
        _link_future(abuf, slice(None), 2, tm)
        pre_scr[0] = abuf[pl.ds(one, tm), :]
        pre_scr[1] = dy * gate
        edge[...] = _scan(pre_scr.at[0], pre_scr.at[1], edge[...], tm, reverse=True)
        abuf[pl.ds(tm, one), :] = abuf[pl.ds(0, one), :]
        d_hs = pre_scr[1]
        d_b = jnp.where(_valid_rows(r, tm), d_hs, 0.0)
        d_iu = d_b * mult
        d_log_a = d_hs * h_prev * a - d_b * (i_gate * u) * (a * a) * inv_mult
        dvec_ref[G_LAMBDA:G_LAMBDA + 1, :] += jnp.sum(d_log_a * r_gate, axis=0, keepdims=True) * (-RG_C)
        d_pre_r = d_log_a * (-RG_C * sp) * r_gate * (1.0 - r_gate)
        d_pre_i = d_iu * u * i_gate * (1.0 - i_gate)
        dvec_ref[G_B_A:G_B_A + 1, :] += jnp.sum(d_pre_r, axis=0, keepdims=True)
        dvec_ref[G_B_X:G_B_X + 1, :] += jnp.sum(d_pre_i, axis=0, keepdims=True)
        dbuf[pl.ds(0, tm), :] = d_iu * i_gate
        d_pre_r = d_pre_r.astype(MXU_DT)
        d_pre_i = d_pre_i.astype(MXU_DT)
        for k in range(nb):
            blk = slice(k * bd, (k + 1) * bd)
            dwa_ref[k] += _dot_tn(ub[:, blk], d_pre_r[:, blk])
            dwx_ref[k] += _dot_tn(ub[:, blk], d_pre_i[:, blk])
            dbuf[pl.ds(0, tm), blk] += _dot_nt(d_pre_r[:, blk], wa_ref[k]) + _dot_nt(d_pre_i[:, blk], wx_ref[k])
        du = dbuf[pl.ds(0, tm), :]
        dvec_ref[G_CONV_B:G_CONV_B + 1, :] += jnp.sum(du, axis=0, keepdims=True)
        for k in range(width):
            dvec_ref[G_CONV_W + k:G_CONV_W + k + 1, :] += jnp.sum(taps[k] * du, axis=0, keepdims=True)
        _link_future(dbuf, slice(None), width, tm)
        d_rb = _conv_back(dbuf, cw_ref, slice(None), width, tm)
        dbuf[pl.ds(tm, past), :] = dbuf[pl.ds(0, past), :]
        d_rb = d_rb.astype(dhh_ref.dtype)
        dhh_ref[:, dr:2 * dr] = d_rb
        dhn = dhn + sum(_dot_nt(d_rb[:, q * n:(q + 1) * n], win_ref[2 + q]) for q in range(2))
        dhin_ref[...] = _norm_bwd_tile(dhn, h_ref[...], dh_out, g_ref[...], _valid_rows(r, tm), hn_ref, dg_ref)

        @pl.when(i == nt - 1)
        def _():
            lam = vec_ref[V_LAMBDA:V_LAMBDA + 1, :]
            dvec_ref[G_LAMBDA:G_LAMBDA + 1, :] = dvec_ref[G_LAMBDA:G_LAMBDA + 1, :] * (-_sigmoid(-lam))

    rev = lambda i: (nt - 1 - i, 0)
    return _launch(
        body, [dh, hh, hh, hs, hs, gates, h, g, w_in, cw, vec, wa, wx, w_out], name="rg_bwd", grid=(nt,),
        in_specs=[pl.BlockSpec((tm, d), rev), pl.BlockSpec((tm, 2 * dr), rev), pl.BlockSpec((halo_rows, 2 * dr), halo_index),
                  pl.BlockSpec((tm, 3 * dr), rev),
                  pl.BlockSpec((one, dr), lambda i: (jnp.maximum((nt - 1 - i) * (tm // one) - 1, 0), 0)),
                  pl.BlockSpec((tm, 3 * dr), rev), pl.BlockSpec((tm, d), rev), _const((1, d)), _const(w_in.shape),
                  _const(cw.shape), _const(vec.shape), _const(wa.shape), _const(wx.shape), _const(w_out.shape)],
        out_specs=[pl.BlockSpec((tm, d), rev), pl.BlockSpec((tm, 2 * dr), rev), pl.BlockSpec((tm, dr), rev),
                   pl.BlockSpec((tm, d), rev), _const((F32_ROWS, dr)), _const(wa.shape), _const(wx.shape),
                   _const((F32_ROWS, d))],
        out_shape=[jax.ShapeDtypeStruct((t_len, d), F32), jax.ShapeDtypeStruct((t_len, 2 * dr), ACT_DT),
                   jax.ShapeDtypeStruct((t_len, dr), ACT_DT), jax.ShapeDtypeStruct((t_len, d), ACT_DT),
                   jax.ShapeDtypeStruct((F32_ROWS, dr), F32), jax.ShapeDtypeStruct(wa.shape, F32),
                   jax.ShapeDtypeStruct(wx.shape, F32), jax.ShapeDtypeStruct((F32_ROWS, d), F32)],
        scratch_shapes=[pltpu.VMEM((past + tm, dr), F32), pltpu.VMEM((tm + past, dr), F32), pltpu.VMEM((2, tm, dr), F32),
                        pltpu.VMEM((one + tm, dr), F32), pltpu.VMEM((tm + one, dr), F32), pltpu.VMEM((F32_ROWS, dr), F32)],
        ride=ride,
    )


def _weight_grad(a, b, nb, *, rows, ride=None):
    t_len, k_dim = a.shape
    n = b.shape[1] // nb
    nt = t_len // rows

    def body(a_ref, b_ref, out_ref, wire_ref):
        @pl.when(pl.program_id(1) == 0)
        def _():
            out_ref[...] = jnp.zeros_like(out_ref)

        out_ref[0] += _dot_tn(a_ref[...].astype(MXU_DT), b_ref[...].astype(MXU_DT))

        @pl.when(pl.program_id(1) == nt - 1)
        def _():
            wire_ref[...] = out_ref[...].astype(wire_ref.dtype)

    block = pl.BlockSpec((1, k_dim, n), lambda j, i: (j, 0, 0))
    return _launch(
        body, [a, b], name="weight_grad", grid=(nb, nt),
        in_specs=[pl.BlockSpec((rows, k_dim), lambda j, i: (i, 0)), pl.BlockSpec((rows, n), lambda j, i: (i, j))],
        out_specs=[block, block],
        out_shape=[jax.ShapeDtypeStruct((nb, k_dim, n), F32), jax.ShapeDtypeStruct((nb, k_dim, n), WIRE_DT)],
        ride=ride,
    )


def _adamw(w, m, v, parts, *, rows, layer=0, into=None):
    n_layers, n_rows, n_cols = w.shape
    nt = n_rows // rows
    n_parts = len(parts)

    def body(w_ref, m_ref, v_ref, *rest):
        part_refs, (g_ref, d_ref, nm_ref, nv_ref) = rest[:n_parts], rest[-4:]
        w_ref, m_ref, v_ref, g_ref, d_ref, nm_ref, nv_ref = (r.at[0] for r in (w_ref, m_ref, v_ref, g_ref, d_ref, nm_ref, nv_ref))
        grad = part_refs[0][...].astype(F32)
        for p in part_refs[1:]:
            grad = grad + p[...].astype(F32)
        new_m = ADAM_B1 * m_ref[...] + (1.0 - ADAM_B1) * grad
        new_v = ADAM_B2 * v_ref[...] + (1.0 - ADAM_B2) * (grad * grad)
        m_hat = new_m / (1.0 - ADAM_B1 ** ADAM_STEP)
        v_hat = new_v / (1.0 - ADAM_B2 ** ADAM_STEP)
        g_ref[...] = grad
        d_ref[...] = -ADAM_LR * (m_hat / (jnp.sqrt(v_hat) + ADAM_EPS) + ADAM_WD * w_ref[...])
        nm_ref[...] = new_m
        nv_ref[...] = new_v

    spec = pl.BlockSpec((rows, n_cols), lambda i: (i, 0))
    layer_spec = pl.BlockSpec((1, rows, n_cols), lambda i: (layer, i, 0))
    into = list(into or [])
    return pl.pallas_call(
        body, name="adamw", grid=(nt,),
        in_specs=[layer_spec] * 3 + [spec] * n_parts + [ANY] * len(into), out_specs=[layer_spec] * 4,
        out_shape=[jax.ShapeDtypeStruct(w.shape, F32)] * 4,
        input_output_aliases={3 + n_parts + k: k for k in range(len(into))},
        compiler_params=_params(),
    )(w, m, v, *parts, *into)


def _sum_stack(stack, *, rows):
    n_stack, n_rows, n_cols = stack.shape

    def body(stack_ref, out_ref):
        acc = stack_ref[0]
        for j in range(1, n_stack):
            acc = acc + stack_ref[j]
        out_ref[...] = acc

    return pl.pallas_call(
        body, name="sum_stack", grid=(n_rows // rows,),
        in_specs=[pl.BlockSpec((n_stack, rows, n_cols), lambda i: (0, i, 0))],
        out_specs=pl.BlockSpec((rows, n_cols), lambda i: (i, 0)),
        out_shape=jax.ShapeDtypeStruct((n_rows, n_cols), F32),
        compiler_params=_params(),
    )(stack)


def _sum_parts(own, recv, *, rows):
    n_rows, n_cols = own.shape
    n_recv = recv.shape[0]

    def body(own_ref, recv_ref, out_ref):
        acc = own_ref[...].astype(F32)
        for j in range(n_recv):
            acc = acc + recv_ref[j].astype(F32)
        out_ref[...] = acc

    return pl.pallas_call(
        body, name="sum_parts", grid=(n_rows // rows,),
        in_specs=[pl.BlockSpec((rows, n_cols), lambda i: (i, 0)), pl.BlockSpec((n_recv, rows, n_cols), lambda i: (0, i, 0))],
        out_specs=pl.BlockSpec((rows, n_cols), lambda i: (i, 0)),
        out_shape=jax.ShapeDtypeStruct(own.shape, F32),
        compiler_params=_params(),
    )(own, recv)


class _Swap:
    def __init__(self, arrays):
        nk = len(arrays)
        self.arrays = list(arrays)
        self.out_shape = [jax.ShapeDtypeStruct(a.shape, a.dtype) for a in arrays]
        self.scratch = [pltpu.SemaphoreType.DMA((nk,)), pltpu.SemaphoreType.DMA((nk,))]

    def run(self, ins, outs, sems, start):
        send_sems, recv_sems = sems
        x, y, c = _place()
        for k in range(len(ins)):
            send = pltpu.make_async_remote_copy(src_ref=ins[k], dst_ref=outs[k], send_sem=send_sems.at[k],
                                                recv_sem=recv_sems.at[k], device_id=(x, y, 1 - c), device_id_type=MESH_ID)
            if start:
                send.start()
            else:
                send.wait_recv()
                send.wait_send()


class _AllDevices:
    def __init__(self, arrays):
        nk = len(arrays)
        self.arrays = list(arrays)
        self.out_shape = [jax.ShapeDtypeStruct((8,) + a.shape, a.dtype) for a in arrays]
        self.scratch = [pltpu.SemaphoreType.DMA((nk, 7)), pltpu.SemaphoreType.DMA((nk, 7)), pltpu.SemaphoreType.DMA((nk,))]

    def run(self, ins, outs, sems, start):
        send_sems, recv_sems, local_sems = sems
        x, y, c = _place()
        mine = 4 * x + 2 * y + c
        for k in range(len(ins)):
            local = pltpu.make_async_copy(ins[k], outs[k].at[mine], local_sems.at[k])
            local.start() if start else local.wait()
            for flip in range(1, 8):
                px, py, pc = x ^ (flip >> 2), y ^ ((flip >> 1) & 1), c ^ (flip & 1)
                sems_f = dict(send_sem=send_sems.at[k, flip - 1], recv_sem=recv_sems.at[k, flip - 1],
                              device_id=(px, py, pc), device_id_type=MESH_ID)
                send = pltpu.make_async_remote_copy(src_ref=ins[k], dst_ref=outs[k].at[mine], **sems_f)
                if start:
                    send.start()
                else:
                    pltpu.make_async_remote_copy(src_ref=ins[k], dst_ref=outs[k].at[4 * px + 2 * py + pc], **sems_f).wait_recv()
                    send.wait_send()


class _Both:
    def __init__(self, first, second):
        self.rides = (first, second)
        self.arrays = first.arrays + second.arrays
        self.out_shape = first.out_shape + second.out_shape
        self.scratch = first.scratch + second.scratch

    def run(self, ins, outs, sems, start):
        for ride in self.rides:
            n_in, n_out, n_sem = len(ride.arrays), len(ride.out_shape), len(ride.scratch)
            ride.run(ins[:n_in], outs[:n_out], sems[:n_sem], start)
            ins, outs, sems = ins[n_in:], outs[n_out:], sems[n_sem:]


def _pack(arrays, pad_rows=F32_ROWS):
    flat = jnp.concatenate([a.reshape(-1).astype(F32) for a in arrays])
    rows = -(-flat.shape[0] // (LANES * pad_rows)) * pad_rows
    return jnp.pad(flat, (0, rows * LANES - flat.shape[0])).reshape(rows, LANES)


def _unpack(packed, shapes):
    flat, out, off = packed.reshape(-1), [], 0
    for s in shapes:
        size = 1
        for dim in s:
            size *= dim
        out.append(flat[off:off + size].reshape(s))
        off += size
    return out


def _divisor_rows(n_rows, most=256):
    best = None
    for r in range(ACT_ROWS, most + 1, ACT_ROWS):
        if n_rows % r == 0:
            best = r
    return best or n_rows


def kernel(x, meta_tokens, norm_mix_g, norm_ffn_g, final_norm_g, sc_w_in, sc_conv_w, sc_w_out, rg_w_in, rg_conv_w, rg_conv_b, rg_w_gate_a, rg_b_gate_a, rg_w_gate_x, rg_b_gate_x, rg_lambda, rg_w_out, ffn_w_up, ffn_conv_w, ffn_w_down, loss_target, m_meta_tokens, m_norm_mix_g, m_norm_ffn_g, m_final_norm_g, m_sc_w_in, m_sc_conv_w, m_sc_w_out, m_rg_w_in, m_rg_conv_w, m_rg_conv_b, m_rg_w_gate_a, m_rg_b_gate_a, m_rg_w_gate_x, m_rg_b_gate_x, m_rg_lambda, m_rg_w_out, m_ffn_w_up, m_ffn_conv_w, m_ffn_w_down, v_meta_tokens, v_norm_mix_g, v_norm_ffn_g, v_final_norm_g, v_sc_w_in, v_sc_conv_w, v_sc_w_out, v_rg_w_in, v_rg_conv_w, v_rg_conv_b, v_rg_w_gate_a, v_rg_b_gate_a, v_rg_w_gate_x, v_rg_b_gate_x, v_rg_lambda, v_rg_w_out, v_ffn_w_up, v_ffn_conv_w, v_ffn_w_down):
    weights = dict(meta_tokens=meta_tokens, norm_mix_g=norm_mix_g, norm_ffn_g=norm_ffn_g, final_norm_g=final_norm_g, sc_w_in=sc_w_in, sc_conv_w=sc_conv_w, sc_w_out=sc_w_out, rg_w_in=rg_w_in, rg_conv_w=rg_conv_w, rg_conv_b=rg_conv_b, rg_w_gate_a=rg_w_gate_a, rg_b_gate_a=rg_b_gate_a, rg_w_gate_x=rg_w_gate_x, rg_b_gate_x=rg_b_gate_x, rg_lambda=rg_lambda, rg_w_out=rg_w_out, ffn_w_up=ffn_w_up, ffn_conv_w=ffn_conv_w, ffn_w_down=ffn_w_down)
    m_in = dict(meta_tokens=m_meta_tokens, norm_mix_g=m_norm_mix_g, norm_ffn_g=m_norm_ffn_g, final_norm_g=m_final_norm_g, sc_w_in=m_sc_w_in, sc_conv_w=m_sc_conv_w, sc_w_out=m_sc_w_out, rg_w_in=m_rg_w_in, rg_conv_w=m_rg_conv_w, rg_conv_b=m_rg_conv_b, rg_w_gate_a=m_rg_w_gate_a, rg_b_gate_a=m_rg_b_gate_a, rg_w_gate_x=m_rg_w_gate_x, rg_b_gate_x=m_rg_b_gate_x, rg_lambda=m_rg_lambda, rg_w_out=m_rg_w_out, ffn_w_up=m_ffn_w_up, ffn_conv_w=m_ffn_conv_w, ffn_w_down=m_ffn_w_down)
    v_in = dict(meta_tokens=v_meta_tokens, norm_mix_g=v_norm_mix_g, norm_ffn_g=v_norm_ffn_g, final_norm_g=v_final_norm_g, sc_w_in=v_sc_w_in, sc_conv_w=v_sc_conv_w, sc_w_out=v_sc_w_out, rg_w_in=v_rg_w_in, rg_conv_w=v_rg_conv_w, rg_conv_b=v_rg_conv_b, rg_w_gate_a=v_rg_w_gate_a, rg_b_gate_a=v_rg_b_gate_a, rg_w_gate_x=v_rg_w_gate_x, rg_b_gate_x=v_rg_b_gate_x, rg_lambda=v_rg_lambda, rg_w_out=v_rg_w_out, ffn_w_up=v_ffn_w_up, ffn_conv_w=v_ffn_conv_w, ffn_w_down=v_ffn_w_down)
    names = list(weights)

    seq, d = x.shape[1:]
    tm = _row_tile(seq)
    tokens, target = _tile_order(x[0], tm), _tile_order(loss_target[0], tm)
    t_len = seq + tm
    wg_rows = 5 * tm if t_len % (5 * tm) == 0 else tm
    wg_rows_in = 13 * tm if t_len % (13 * tm) == 0 else wg_rows
    xi, yi, _ = _place()
    chip = 2 * xi + yi
    mesh_axes = ("x", "y", "c")

    wire = lambda w: w.astype(WIRE_DT)
    small_sharded = ["meta_tokens", "sc_conv_w", "rg_conv_w", "rg_conv_b", "rg_b_gate_a", "rg_b_gate_x", "rg_lambda", "ffn_conv_w"]
    small_2d = {n: weights[n].reshape(-1, weights[n].shape[-1]) for n in small_sharded}
    w_sc_in, w_sc_out, small_by_chip = _exchange(
        _GatherHalves([wire(sc_w_in[0]), wire(sc_w_out[0]), _pack([small_2d[n] for n in small_sharded], 2 * ACT_ROWS)]),
        "gather_first")
    w_sc_out = w_sc_out.reshape(-1, d)
    gather_ffn0 = _Gather([wire(ffn_w_up[0]), wire(ffn_w_down[0])])
    gather_rest = _Gather([wire(rg_w_in[0]), wire(rg_w_out[0]), wire(ffn_w_up[1]), wire(ffn_w_down[1])])
    small_len = sum(a.size for a in small_2d.values())
    by_chip = small_by_chip.reshape(N_CHIPS, -1)[:, :small_len]
    full, off = {}, 0
    for n in small_sharded:
        rows, width = small_2d[n].shape
        full[n] = by_chip[:, off:off + rows * width].reshape(N_CHIPS, rows, width).transpose(1, 0, 2).reshape(rows, N_CHIPS * width)
        off += rows * width
    sc_cw, rg_cw = full["sc_conv_w"], full["rg_conv_w"]
    ffn_cw = [full["ffn_conv_w"][0:3], full["ffn_conv_w"][3:6]]
    d_rnn = rg_cw.shape[1]
    vec = jnp.concatenate([full["rg_conv_b"], full["rg_b_gate_a"], full["rg_b_gate_x"], full["rg_lambda"],
                           jnp.zeros((F32_ROWS - 4, d_rnn), F32)])
    wa, wx = rg_w_gate_a[0].astype(MXU_DT), rg_w_gate_x[0].astype(MXU_DT)
    first = _tile_order(jnp.concatenate([jnp.zeros((tm - N_META, d), F32), full["meta_tokens"]]), tm)
    g_mix = [norm_mix_g[0:1], norm_mix_g[1:2]]
    g_ffn = [norm_ffn_g[0:1], norm_ffn_g[1:2]]

    h1, hh0, w_up0, w_dn0 = _sc_fwd(tokens, first, g_mix[0], w_sc_in, sc_cw, w_sc_out, tm=tm, ride=gather_ffn0)
    h2, hu0, hc0, w_rg_in, w_rg_out, w_up1, w_dn1 = _ffn_fwd(h1, g_ffn[0], w_up0, ffn_cw[0], w_dn0.reshape(-1, d), tm=tm,
                                                         ride=gather_rest)
    w_up, w_dn, w_rg_out = [w_up0, w_up1], [w_dn0.reshape(-1, d), w_dn1.reshape(-1, d)], w_rg_out.reshape(-1, d)
    h3, hh1, hs, gates = _rg_fwd(h2, g_mix[1], w_rg_in, rg_cw, vec, wa, wx, w_rg_out, tm=tm)
    dh4, hu1, hc1, sq, d_final = _ffn_fwd(h3, g_ffn[1], w_up[1], ffn_cw[1], w_dn[1], tm=tm,
                                     loss=(target, final_norm_g.reshape(1, d)))
    loss = lax.psum(jnp.sum(sq[0]) * (0.5 / d), mesh_axes)

    def by_chip_rows(pair):
        return [p.reshape(N_CHIPS, -1, d) for p in pair]

    def ffn_backward(dh_out, h_in, hu, hc, layer, ride):
        dh_in, act, dhu, hn, dcw, dg, *landed = _ffn_bwd(dh_out, hu, hc, h_in, g_ffn[layer], w_up[layer], ffn_cw[layer],
                                                         w_dn[layer], tm=tm, ride=ride)
        d_up = _weight_grad(hn, dhu, N_CHIPS, rows=wg_rows_in)
        d_dn = by_chip_rows(_weight_grad(act, dh_out, 1, rows=wg_rows))
        return dh_in, d_up, d_dn, dcw[0:3], dg[0], landed

    dh3, d_up1, d_dn1, d_fcw1, d_gf1, _ = ffn_backward(dh4, h3, hu1, hc1, 1, None)
    dh2, dhh1, y_rg, hn_rg, d_vec, d_wa, d_wx, d_gm1, *landed_ffn1 = _rg_bwd(
        dh3, hh1, hs, gates, h2, g_mix[1], w_rg_in, rg_cw, vec, wa, wx, w_rg_out, tm=tm,
        ride=_Scatter([d_up1[1], d_dn1[1]]))
    d_rg_in = _weight_grad(hn_rg, dhh1, N_CHIPS, rows=wg_rows_in)
    d_rg_out = by_chip_rows(_weight_grad(y_rg, dh3, 1, rows=wg_rows))
    early = {"rg_conv_w": d_vec[G_CONV_W:G_CONV_W + 4], "rg_conv_b": d_vec[G_CONV_B:G_CONV_B + 1],
             "rg_b_gate_a": d_vec[G_B_A:G_B_A + 1], "rg_b_gate_x": d_vec[G_B_X:G_B_X + 1],
             "rg_lambda": d_vec[G_LAMBDA:G_LAMBDA + 1], "ffn_conv_w.1": d_fcw1, "norm_mix_g.1": d_gm1[0:1],
             "norm_ffn_g.1": d_gf1[None], "final_norm_g": d_final[0]}
    early_packed = _pack(list(early.values()))
    gate_names = ["rg_w_gate_a", "rg_w_gate_x"]
    to_all = _AllDevices([early_packed, d_wa.reshape(-1, LANES), d_wx.reshape(-1, LANES)])
    dh1, d_up0, d_dn0, d_fcw0, d_gf0, landed = ffn_backward(
        dh2, h1, hu0, hc0, 0, _Both(_Scatter([d_rg_in[1], d_rg_out[1]]), to_all))
    landed_rg, early_by_device, gates_by_device = landed[0:2], landed[2], landed[3:]

    def core_sum(pair, received):
        own = lax.dynamic_index_in_dim(pair[0], chip, 0, keepdims=False)
        return _sum_parts(own, received, rows=_divisor_rows(own.shape[0]))

    early_big = [("rg_w_in", 0), ("rg_w_out", 0), ("ffn_w_up", 1), ("ffn_w_down", 1)]
    early_sum = [core_sum(d_rg_in, landed_rg[0]), core_sum(d_rg_out, landed_rg[1]), core_sum(d_up1, landed_ffn1[0]),
                 core_sum(d_dn1, landed_ffn1[1])]
    grad_x, dhh0, z_sc, hn_sc, d_sccw, d_gm0, d_first, *landed = _sc_bwd(
        dh1, hh0, tokens, first, g_mix[0], w_sc_in, sc_cw, w_sc_out, tm=tm,
        ride=_Both(_Scatter([d_up0[1], d_dn0[1]]), _Swap(early_sum)))
    landed_ffn0, early_other = landed[0:2], landed[2:]
    late = {"meta_tokens": _time_order(d_first, tm)[tm - N_META:], "sc_conv_w": d_sccw[0:3], "ffn_conv_w.0": d_fcw0,
            "norm_mix_g.0": d_gm0[0:1], "norm_ffn_g.0": d_gf0[None]}
    late_packed = _pack(list(late.values()))
    ffn0_big = [("ffn_w_up", 0), ("ffn_w_down", 0)]
    ffn0_sum = [core_sum(d_up0, landed_ffn0[0]), core_sum(d_dn0, landed_ffn0[1])]
    *d_sc_in, ffn0_up_other, ffn0_dn_other, late_by_device = _weight_grad(
        hn_sc, dhh0, N_CHIPS, rows=wg_rows_in, ride=_Both(_Swap(ffn0_sum), _AllDevices([late_packed])))
    *d_sc_out, landed_sc_in = _weight_grad(z_sc, dh1, 1, rows=wg_rows, ride=_Scatter([d_sc_in[1]]))
    d_sc_out = by_chip_rows(d_sc_out)
    landed_sc = [landed_sc_in, *_exchange(_Scatter([d_sc_out[1]]), "scatter_last")]
    grad_x = _time_order(grad_x, tm)[None]

    sc_big = [("sc_w_in", 0), ("sc_w_out", 0)]
    sc_sum = [core_sum(d_sc_in, landed_sc[0]), core_sum(d_sc_out, landed_sc[1])]
    sc_other = _exchange(_Swap(sc_sum), "swap_cores")
    out = {k: {} for k in ("grad", "delta", "m", "v")}
    stacked = {}
    for (n, layer), mine, theirs in zip(sc_big + ffn0_big + early_big, sc_sum + ffn0_sum + early_sum,
                                        [*sc_other, ffn0_up_other, ffn0_dn_other, *early_other]):
        stacked[n] = _adamw(weights[n], m_in[n], v_in[n], [mine, theirs], rows=_divisor_rows(mine.shape[0]), layer=layer,
                            into=stacked.get(n))
    for n, res in stacked.items():
        for k, key in enumerate(("grad", "delta", "m", "v")):
            out[key][n] = res[k]

    summed = {}
    for parts, packed, by_device in ((early, early_packed, early_by_device), (late, late_packed, late_by_device)):
        total = _sum_stack(by_device, rows=packed.shape[0])
        summed.update(zip(parts, _unpack(total, [p.shape for p in parts.values()])))
    for n in ("ffn_conv_w", "norm_mix_g", "norm_ffn_g"):
        summed[n] = jnp.concatenate([summed.pop(n + ".0"), summed.pop(n + ".1")])
    for n, by_device in zip(gate_names, gates_by_device):
        as_rows = lambda a: a.reshape(1, -1, LANES)
        res = _adamw(as_rows(weights[n]), as_rows(m_in[n]), as_rows(v_in[n]), [_sum_stack(by_device, rows=256)], rows=256)
        for k, key in enumerate(("grad", "delta", "m", "v")):
            out[key][n] = res[k].reshape(weights[n].shape)
    replicated = ["norm_mix_g", "norm_ffn_g", "final_norm_g"]
    small_names = small_sharded + replicated
    grads = {}
    for n in small_sharded:
        width = small_2d[n].shape[1]
        grads[n] = lax.dynamic_slice_in_dim(summed[n], chip * width, width, axis=1).reshape(weights[n].shape)
    for n in replicated:
        grads[n] = summed[n].reshape(weights[n].shape)
    shapes = [weights[n].shape for n in small_names]
    packed_w = _pack([weights[n] for n in small_names])
    res = _adamw(packed_w[None], _pack([m_in[n] for n in small_names])[None], _pack([v_in[n] for n in small_names])[None],
                 [_pack([grads[n] for n in small_names])], rows=packed_w.shape[0])
    for k, key in enumerate(("grad", "delta", "m", "v")):
        out[key].update(dict(zip(small_names, _unpack(res[k][0], shapes))))

    return (loss, grad_x, *[out["grad"][n] for n in names], *[out["delta"][n] for n in names],
            *[out["m"][n] for n in names], *[out["v"][n] for n in names])
```

```python
import functools

import jax
import jax.numpy as jnp
from jax import lax
from jax.experimental import pallas as pl
from jax.experimental.pallas import tpu as pltpu

F32 = jnp.float32
MXU_DT = jnp.bfloat16
ACT_DT = jnp.bfloat16
WIRE_DT = jnp.bfloat16
MESH_ID = pl.DeviceIdType.MESH

N_META = 16
RMS_EPS = 1e-6
RG_C = 8.0
ADAM_LR, ADAM_B1, ADAM_B2, ADAM_EPS, ADAM_WD, ADAM_STEP = 0.001, 0.9, 0.999, 1e-08, 0.01, 10
N_CHIPS = 4
VMEM_LIMIT = 60 * 1024 * 1024
F32_ROWS = 8
ACT_ROWS = 16
LANES = 128


def _row_tile(seq):
    for tm in (256, 128, 64, 32, 16):
        if seq % tm == 0:
            return tm
    raise ValueError(f"sequence length {seq} is not a multiple of 16")


def _params(n_axes=1, **kw):
    return pltpu.CompilerParams(dimension_semantics=("arbitrary",) * n_axes, vmem_limit_bytes=VMEM_LIMIT, **kw)


def _const(shape):
    return pl.BlockSpec(shape, lambda *_: (0,) * len(shape), pipeline_mode=pl.Buffered(1))


def _dot(a, b):
    return jnp.dot(a, b, preferred_element_type=F32)


def _dot_nt(a, b):
    return lax.dot_general(a, b, (((1,), (1,)), ((), ())), preferred_element_type=F32)


def _dot_tn(a, b):
    return lax.dot_general(a, b, (((0,), (0,)), ((), ())), preferred_element_type=F32)


def _sigmoid(x):
    return 0.5 + 0.5 * jnp.tanh(0.5 * x)


def _rms(h, g):
    rstd = lax.rsqrt(jnp.mean(h * h, axis=-1, keepdims=True) + RMS_EPS)
    xhat = h * rstd
    return xhat * g, xhat, rstd


def _rms_bwd(dhn, xhat, rstd, g):
    dx = dhn * g
    return rstd * (dx - xhat * jnp.mean(dx * xhat, axis=-1, keepdims=True))


def _gelu(x):
    k = 0.7978845608028654
    t = jnp.tanh(k * (x + 0.044715 * x * x * x))
    return 0.5 * x * (1.0 + t), t


def _gelu_grad(x, t):
    k = 0.7978845608028654
    return 0.5 * (1.0 + t) + 0.5 * x * (1.0 - t * t) * k * (1.0 + 3 * 0.044715 * x * x)


def _softplus(x):
    e = jnp.exp(-jnp.abs(x))
    return jnp.maximum(x, 0.0) + jnp.where(e < 1e-4, e - 0.5 * e * e, jnp.log(1.0 + e))


def _expm1_neg(z, exp_z):
    series = z * (1.0 + z * (0.5 + z * (1.0 / 6)))
    return jnp.where(z > -0.02, series, exp_z - 1.0)


def _tile_order(a, tm):
    return a.reshape(-1, F32_ROWS, tm // F32_ROWS, a.shape[-1]).swapaxes(1, 2).reshape(a.shape)


def _time_order(a, tm):
    return a.reshape(-1, tm // F32_ROWS, F32_ROWS, a.shape[-1]).swapaxes(1, 2).reshape(a.shape)


def _valid_rows(tile, tm):
    row = lax.broadcasted_iota(jnp.int32, (tm, 1), 0)
    time = (row & (F32_ROWS - 1)) * (tm // F32_ROWS) + (row >> 3) + tile * tm
    return time >= tm - N_META


def _sublane():
    return lax.broadcasted_iota(jnp.int32, (F32_ROWS, 1), 0)


def _past_rows(width):
    return (width - 1) * F32_ROWS


def _halo_block(past, tm, nt):
    rows = -(-past // ACT_ROWS) * ACT_ROWS
    return rows, lambda i: (jnp.maximum((nt - 1 - i) * (tm // rows) - 1, 0), 0)


def _link_past(buf, cols, width, tm):
    past = _past_rows(width)
    for k in range(1, width):
        rows = pl.ds(past - F32_ROWS * k, F32_ROWS)
        before = pltpu.roll(buf[rows, cols], 1, 0)
        mine = pltpu.roll(buf[pl.ds(past + tm - F32_ROWS * k, F32_ROWS), cols], 1, 0)
        buf[rows, cols] = jnp.where(_sublane() == 0, before, mine)


def _link_future(buf, cols, width, tm):
    for k in range(1, width):
        rows = pl.ds(tm + F32_ROWS * (k - 1), F32_ROWS)
        after = pltpu.roll(buf[rows, cols], F32_ROWS - 1, 0)
        mine = pltpu.roll(buf[pl.ds(F32_ROWS * (k - 1), F32_ROWS), cols], F32_ROWS - 1, 0)
        buf[rows, cols] = jnp.where(_sublane() == F32_ROWS - 1, after, mine)


def _conv_taps(buf, cols, width, tm):
    return [buf[pl.ds(F32_ROWS * k, tm), cols] for k in range(width)]


def _conv_back(buf, cw_ref, cols, width, tm):
    return sum(cw_ref[k:k + 1, cols] * buf[pl.ds(F32_ROWS * (width - 1 - k), tm), cols] for k in range(width))


ANY = pl.BlockSpec(memory_space=pl.ANY)


def _place():
    return lax.axis_index("x"), lax.axis_index("y"), lax.axis_index("c")


def _other_chips(x, y):
    return [(1 - x, y), (x, 1 - y), (1 - x, 1 - y)]


class _Gather:
    def __init__(self, shards):
        nk = len(shards)
        self.arrays = list(shards)
        self.out_shape = [jax.ShapeDtypeStruct((N_CHIPS,) + s.shape, s.dtype) for s in shards]
        self.scratch = [pltpu.SemaphoreType.DMA((nk, 3)), pltpu.SemaphoreType.DMA((nk, 3)), pltpu.SemaphoreType.DMA((nk,))]

    def run(self, ins, outs, sems, start):
        send_sems, recv_sems, local_sems = sems
        x, y, c = _place()
        mine = 2 * x + y
        for k in range(len(ins)):
            local = pltpu.make_async_copy(ins[k], outs[k].at[mine], local_sems.at[k])
            local.start() if start else local.wait()
            for j, (px, py) in enumerate(_other_chips(x, y)):
                sems_kj = dict(send_sem=send_sems.at[k, j], recv_sem=recv_sems.at[k, j], device_id=(px, py, c),
                               device_id_type=MESH_ID)
                send = pltpu.make_async_remote_copy(src_ref=ins[k], dst_ref=outs[k].at[mine], **sems_kj)
                if start:
                    send.start()
                else:
                    pltpu.make_async_remote_copy(src_ref=ins[k], dst_ref=outs[k].at[2 * px + py], **sems_kj).wait_recv()
                    send.wait_send()


class _GatherHalves:
    def __init__(self, shards):
        nk = len(shards)
        self.arrays = list(shards)
        self.out_shape = [jax.ShapeDtypeStruct((N_CHIPS,) + s.shape, s.dtype) for s in shards]
        self.scratch = [pltpu.SemaphoreType.DMA((nk, 3)) for _ in range(4)] + [pltpu.SemaphoreType.DMA((nk,))]

    def run(self, ins, outs, sems, start):
        far_send, far_recv, near_send, near_recv, local_sems = sems
        x, y, c = _place()
        mine = 2 * x + y
        for phase in ((0,) if start else (1, 2)):
            for k in range(len(ins)):
                half = ins[k].shape[0] // 2
                my_half = pl.ds(pl.multiple_of(c * half, ACT_ROWS), half)
                other_half = pl.ds(pl.multiple_of((1 - c) * half, ACT_ROWS), half)
                if phase != 1:
                    local = pltpu.make_async_copy(ins[k], outs[k].at[mine], local_sems.at[k])
                    local.start() if phase == 0 else local.wait()
                for j, (px, py) in enumerate(_other_chips(x, y)):
                    theirs = 2 * px + py
                    far = dict(send_sem=far_send.at[k, j], recv_sem=far_recv.at[k, j], device_id=(px, py, c),
                               device_id_type=MESH_ID)
                    near = dict(send_sem=near_send.at[k, j], recv_sem=near_recv.at[k, j], device_id=(x, y, 1 - c),
                                device_id_type=MESH_ID)
                    landed = outs[k].at[theirs, my_half]
                    send = lambda: pltpu.make_async_remote_copy(src_ref=ins[k].at[my_half], dst_ref=outs[k].at[mine, my_half], **far)
                    pass_on = lambda: pltpu.make_async_remote_copy(src_ref=landed, dst_ref=landed, **near)
                    if phase == 0:
                        send().start()
                    elif phase == 1:
                        pltpu.make_async_remote_copy(src_ref=ins[k].at[my_half], dst_ref=landed, **far).wait_recv()
                        pass_on().start()
                    else:
                        pltpu.make_async_remote_copy(src_ref=landed, dst_ref=outs[k].at[theirs, other_half], **near).wait_recv()
                        pass_on().wait_send()
                        send().wait_send()


class _Scatter:
    def __init__(self, parts):
        nk = len(parts)
        self.arrays = list(parts)
        self.out_shape = [jax.ShapeDtypeStruct((3,) + p.shape[1:], p.dtype) for p in parts]
        self.scratch = [pltpu.SemaphoreType.DMA((nk, 3)), pltpu.SemaphoreType.DMA((nk, 3))]

    def run(self, ins, outs, sems, start):
        send_sems, recv_sems = sems
        x, y, c = _place()
        for k in range(len(ins)):
            for j, (px, py) in enumerate(_other_chips(x, y)):
                send = pltpu.make_async_remote_copy(
                    src_ref=ins[k].at[2 * px + py], dst_ref=outs[k].at[j], send_sem=send_sems.at[k, j],
                    recv_sem=recv_sems.at[k, j], device_id=(px, py, c), device_id_type=MESH_ID)
                if start:
                    send.start()
                else:
                    send.wait_recv()
                    send.wait_send()


def _exchange(ride, name):
    n_in, n_out = len(ride.arrays), len(ride.out_shape)

    def body(*refs):
        ride.run(refs[:n_in], refs[n_in:n_in + n_out], refs[n_in + n_out:], start=True)
        ride.run(refs[:n_in], refs[n_in:n_in + n_out], refs[n_in + n_out:], start=False)

    return pl.pallas_call(body, name=name, in_specs=[ANY] * n_in, out_specs=[ANY] * n_out, out_shape=ride.out_shape,
                          scratch_shapes=ride.scratch)(*ride.arrays)


def _launch(body, operands, *, name, grid, in_specs, out_specs, out_shape, scratch_shapes=(), ride=None):
    common = dict(name=name, grid=grid, compiler_params=_params(len(grid)))
    if ride is None:
        return pl.pallas_call(body, in_specs=in_specs, out_specs=out_specs, out_shape=out_shape,
                              scratch_shapes=list(scratch_shapes), **common)(*operands)
    n_in, n_out, n_scr = len(operands), len(out_shape), len(scratch_shapes)
    r_in, r_out = len(ride.arrays), len(ride.out_shape)

    def riding(*refs):
        ins, refs = refs[:n_in], refs[n_in:]
        r_ins, refs = refs[:r_in], refs[r_in:]
        outs, refs = refs[:n_out], refs[n_out:]
        r_outs, refs = refs[:r_out], refs[r_out:]
        scr, r_sems = refs[:n_scr], refs[n_scr:]
        step = [pl.program_id(axis) for axis in range(len(grid))]
        first = functools.reduce(jnp.logical_and, [s == 0 for s in step])
        last = functools.reduce(jnp.logical_and, [s == size - 1 for s, size in zip(step, grid)])

        @pl.when(first)
        def _():
            ride.run(r_ins, r_outs, r_sems, start=True)

        body(*ins, *outs, *scr)

        @pl.when(last)
        def _():
            ride.run(r_ins, r_outs, r_sems, start=False)

    return pl.pallas_call(
        riding, in_specs=list(in_specs) + [ANY] * r_in, out_specs=list(out_specs) + [ANY] * r_out,
        out_shape=list(out_shape) + ride.out_shape, scratch_shapes=list(scratch_shapes) + ride.scratch, **common,
    )(*operands, *ride.arrays)


def _sc_fwd(x, first, g, w_in, cw, w_out, *, tm, ride=None):
    seq, d = x.shape
    nt = seq // tm + 1
    nq, _, n = w_in.shape
    width = cw.shape[0]
    past = _past_rows(width)

    def body(x_ref, first_ref, g_ref, win_ref, cw_ref, wout_ref, h1_ref, hh_ref, hh_scr, cbuf):
        i = pl.program_id(0)

        @pl.when(i == 0)
        def _():
            cbuf[pl.ds(0, past), :] = jnp.zeros((past, d), F32)

        h = jnp.where(i == 0, first_ref[...], x_ref[...])
        hn = _rms(h, g_ref[...])[0].astype(MXU_DT)
        for q in range(nq):
            hh_scr[:, q * n:(q + 1) * n] = _dot(hn, win_ref[q])
        hh_ref[...] = hh_scr[...].astype(hh_ref.dtype)
        b = hh_scr[:, 0:d]
        cbuf[pl.ds(past, tm), :] = hh_scr[:, d:2 * d] * hh_scr[:, 2 * d:3 * d]
        last = cbuf[pl.ds(tm, past), :]
        _link_past(cbuf, slice(None), width, tm)
        u = sum(cw_ref[k:k + 1, :] * tap for k, tap in enumerate(_conv_taps(cbuf, slice(None), width, tm)))
        cbuf[pl.ds(0, past), :] = last
        h1_ref[...] = h + _dot((b * u).astype(MXU_DT), wout_ref[...])

    return _launch(
        body, [x, first, g, w_in, cw, w_out], name="sc_fwd", grid=(nt,),
        in_specs=[pl.BlockSpec((tm, d), lambda i: (jnp.maximum(i - 1, 0), 0)), _const((tm, d)), _const((1, d)),
                  _const(w_in.shape), _const(cw.shape), _const(w_out.shape)],
        out_specs=[pl.BlockSpec((tm, d), lambda i: (i, 0)), pl.BlockSpec((tm, nq * n), lambda i: (i, 0))],
        out_shape=[jax.ShapeDtypeStruct((nt * tm, d), F32), jax.ShapeDtypeStruct((nt * tm, nq * n), ACT_DT)],
        scratch_shapes=[pltpu.VMEM((tm, nq * n), F32), pltpu.VMEM((past + tm, d), F32)],
        ride=ride,
    )


def _sc_bwd(dh, hh, x, first, g, w_in, cw, w_out, *, tm, ride=None):
    t_len, d = dh.shape
    nt = t_len // tm
    nq, _, n = w_in.shape
    width = cw.shape[0]
    past = _past_rows(width)
    halo_rows, halo_index = _halo_block(past, tm, nt)

    def body(dh_ref, hh_ref, hhp_ref, x_ref, first_ref, g_ref, win_ref, cw_ref, wout_ref,
             dx_ref, dhh_ref, z_ref, hn_ref, dcw_ref, dg_ref, dfirst_ref, cbuf, dbuf):
        i = pl.program_id(0)
        r = nt - 1 - i

        @pl.when(i == 0)
        def _():
            dbuf[pl.ds(tm, past), :] = jnp.zeros((past, d), F32)
            dcw_ref[...] = jnp.zeros_like(dcw_ref)
            dg_ref[...] = jnp.zeros_like(dg_ref)

        dh_out = dh_ref[...]
        b = hh_ref[:, 0:d].astype(F32)
        c = hh_ref[:, d:2 * d].astype(F32)
        v = hh_ref[:, 2 * d:3 * d].astype(F32)
        prev = hhp_ref[...].astype(F32)[halo_rows - past:, :]
        cbuf[pl.ds(0, past), :] = jnp.where(r > 0, prev[:, d:2 * d] * prev[:, 2 * d:3 * d], 0.0)
        cbuf[pl.ds(past, tm), :] = c * v
        _link_past(cbuf, slice(None), width, tm)
        taps = _conv_taps(cbuf, slice(None), width, tm)
        u = sum(cw_ref[k:k + 1, :] * taps[k] for k in range(width))
        z_ref[...] = (b * u).astype(z_ref.dtype)
        dz = _dot_nt(dh_out.astype(MXU_DT), wout_ref[...])
        d_b = (dz * u).astype(dhh_ref.dtype)
        dhh_ref[:, 0:d] = d_b
        parts = [_dot_nt(d_b[:, 0:n], win_ref[0])]
        du = dz * b
        for k in range(width):
            dcw_ref[k:k + 1, :] += jnp.sum(taps[k] * du, axis=0, keepdims=True)
        dbuf[pl.ds(0, tm), :] = du
        _link_future(dbuf, slice(None), width, tm)
        dcv = _conv_back(dbuf, cw_ref, slice(None), width, tm)
        dbuf[pl.ds(tm, past), :] = dbuf[pl.ds(0, past), :]
        d_c, d_v = (dcv * v).astype(dhh_ref.dtype), (dcv * c).astype(dhh_ref.dtype)
        dhh_ref[:, d:2 * d] = d_c
        dhh_ref[:, 2 * d:3 * d] = d_v
        rest = jnp.concatenate([d_b[:, n:], d_c, d_v], axis=1)
        parts += [_dot_nt(rest[:, (q - 1) * n:q * n], win_ref[q]) for q in range(1, nq)]
        dhn = functools.reduce(lambda a, b: a + b, parts)
        h_in = jnp.where(r == 0, first_ref[...], x_ref[...])
        dh_in = _norm_bwd_tile(dhn, h_in, dh_out, g_ref[...], _valid_rows(r, tm), hn_ref, dg_ref)

        @pl.when(r == 0)
        def _():
            dfirst_ref[...] = dh_in

        @pl.when(r > 0)
        def _():
            dx_ref[...] = dh_in

    rev = lambda i: (nt - 1 - i, 0)
    rev_x = lambda i: (jnp.maximum(nt - 2 - i, 0), 0)
    return _launch(
        body, [dh, hh, hh, x, first, g, w_in, cw, w_out], name="sc_bwd", grid=(nt,),
        in_specs=[pl.BlockSpec((tm, d), rev), pl.BlockSpec((tm, 3 * d), rev), pl.BlockSpec((halo_rows, 3 * d), halo_index),
                  pl.BlockSpec((tm, d), rev_x), _const((tm, d)), _const((1, d)), _const(w_in.shape), _const(cw.shape),
                  _const(w_out.shape)],
        out_specs=[pl.BlockSpec((tm, d), rev_x), pl.BlockSpec((tm, 3 * d), rev), pl.BlockSpec((tm, d), rev),
                   pl.BlockSpec((tm, d), rev), _const((F32_ROWS, d)), _const((F32_ROWS, d)), _const((tm, d))],
        out_shape=[jax.ShapeDtypeStruct((t_len - tm, d), F32), jax.ShapeDtypeStruct((t_len, 3 * d), ACT_DT),
                   jax.ShapeDtypeStruct((t_len, d), ACT_DT), jax.ShapeDtypeStruct((t_len, d), ACT_DT),
                   jax.ShapeDtypeStruct((F32_ROWS, d), F32), jax.ShapeDtypeStruct((F32_ROWS, d), F32),
                   jax.ShapeDtypeStruct((tm, d), F32)],
        scratch_shapes=[pltpu.VMEM((past + tm, d), F32), pltpu.VMEM((tm + past, d), F32)],
        ride=ride,
    )


def _ffn_fwd(h, g, w_up, cw, w_down, *, tm, ride=None, loss=None):
    t_len, d = h.shape
    nt = t_len // tm
    nq, _, n = w_up.shape
    width = cw.shape[0]
    past = _past_rows(width)

    def body(h_ref, g_ref, wup_ref, cw_ref, wdn_ref, *rest):
        if loss is None:
            out_ref, hu_ref, hc_ref, ubuf, tail = rest
        else:
            t_ref, gf_ref, out_ref, hu_ref, hc_ref, sq_ref, dgf_ref, ubuf, tail = rest
        i = pl.program_id(0)

        @pl.when(i == 0)
        def _():
            tail[...] = jnp.zeros_like(tail)

        h_in = h_ref[...]
        hn = _rms(h_in, g_ref[...])[0].astype(MXU_DT)
        ubuf[pl.ds(0, past), :] = tail[...]
        for q in range(nq):
            ubuf[pl.ds(past, tm), q * n:(q + 1) * n] = _dot(hn, wup_ref[q])
        hu_ref[...] = ubuf[pl.ds(past, tm), :].astype(hu_ref.dtype)
        tail[...] = ubuf[pl.ds(tm, past), :]
        _link_past(ubuf, slice(None), width, tm)
        acc = h_in
        for j in range(nq // 2):
            gcol, vcol = slice(j * n, (j + 1) * n), slice((nq // 2 + j) * n, (nq // 2 + j + 1) * n)
            conv = lambda cols: sum(cw_ref[k:k + 1, cols] * tap for k, tap in enumerate(_conv_taps(ubuf, cols, width, tm)))
            gj, vj = conv(gcol), conv(vcol)
            hc_ref[:, gcol] = gj.astype(hc_ref.dtype)
            hc_ref[:, vcol] = vj.astype(hc_ref.dtype)
            acc = acc + _dot((gj * _sigmoid(gj) * vj).astype(MXU_DT), wdn_ref[j * n:(j + 1) * n, :])
        if loss is None:
            out_ref[...] = acc
            return

        @pl.when(i == 0)
        def _():
            sq_ref[...] = jnp.zeros_like(sq_ref)
            dgf_ref[...] = jnp.zeros_like(dgf_ref)
            out_ref[...] = jnp.zeros_like(out_ref)

        @pl.when(i > 0)
        def _():
            gain = gf_ref[...]
            out, xhat, rstd = _rms(acc, gain)
            err = out - t_ref[...]
            sq_ref[0:1, :] += jnp.sum(err * err, axis=0, keepdims=True)
            dout = err * (1.0 / d)
            dgf_ref[0:1, :] += jnp.sum(dout * xhat, axis=0, keepdims=True)
            out_ref[...] = _rms_bwd(dout, xhat, rstd, gain)

    row = lambda i: (i, 0)
    stat = jax.ShapeDtypeStruct((F32_ROWS, d), F32)
    return _launch(
        body, [h, g, w_up, cw, w_down] + list(loss or ()), name="ffn_fwd", grid=(nt,),
        in_specs=[pl.BlockSpec((tm, d), row), _const((1, d)), _const(w_up.shape), _const(cw.shape), _const(w_down.shape)]
        + ([pl.BlockSpec((tm, d), lambda i: (jnp.maximum(i - 1, 0), 0)), _const((1, d))] if loss else []),
        out_specs=[pl.BlockSpec((tm, d), row), pl.BlockSpec((tm, nq * n), row), pl.BlockSpec((tm, nq * n), row)]
        + ([_const(stat.shape)] * 2 if loss else []),
        out_shape=[jax.ShapeDtypeStruct((t_len, d), F32), jax.ShapeDtypeStruct((t_len, nq * n), ACT_DT),
                   jax.ShapeDtypeStruct((t_len, nq * n), ACT_DT)] + ([stat, stat] if loss else []),
        scratch_shapes=[pltpu.VMEM((past + tm, nq * n), F32), pltpu.VMEM((past, nq * n), F32)],
        ride=ride,
    )


def _norm_bwd_tile(dhn, h_in, dh, gain, valid, hn_ref, dg_ref):
    hn, xhat, rstd = _rms(h_in, gain)
    hn_ref[...] = hn.astype(hn_ref.dtype)
    dg_ref[0:1, :] += jnp.sum(dhn * xhat, axis=0, keepdims=True)
    return jnp.where(valid, dh + _rms_bwd(dhn, xhat, rstd, gain), 0.0)


def _ffn_bwd(dh, hu, hc, h, g, w_up, cw, w_down, *, tm, ride=None):
    t_len, d = dh.shape
    nt = t_len // tm
    ff = hu.shape[1]
    n = ff // 4
    width = cw.shape[0]
    past = _past_rows(width)
    halo_rows, halo_index = _halo_block(past, tm, nt)

    def body(dh_ref, hu_ref, hup_ref, hc_ref, h_ref, g_ref, wup_ref, cw_ref, wdn_ref,
             dhin_ref, a_ref, dhu_ref, hn_ref, dcw_ref, dg_ref, ubuf, dbuf, head):
        i = pl.program_id(0)
        r = nt - 1 - i

        @pl.when(i == 0)
        def _():
            head[...] = jnp.zeros_like(head)
            dcw_ref[...] = jnp.zeros_like(dcw_ref)
            dg_ref[...] = jnp.zeros_like(dg_ref)

        dh_out = dh_ref[...]
        dhb = dh_out.astype(MXU_DT)
        dhn_parts = []
        d_act = [_dot_nt(dhb, wdn_ref[j * n:(j + 1) * n, :]) for j in range(2)]
        for j in range(2):
            mine = slice(0, n), slice(n, 2 * n)
            full = slice(j * n, (j + 1) * n), slice((2 + j) * n, (3 + j) * n)
            for here, there in zip(mine, full):
                prev = hup_ref[:, there].astype(F32)[halo_rows - past:, :]
                ubuf[pl.ds(0, past), here] = jnp.where(r > 0, prev, 0.0)
                ubuf[pl.ds(past, tm), here] = hu_ref[:, there].astype(F32)
                dbuf[pl.ds(tm, past), here] = head[:, there]
            _link_past(ubuf, slice(None), width, tm)
            gj, vj = hc_ref[:, full[0]].astype(F32), hc_ref[:, full[1]].astype(F32)
            sg = _sigmoid(gj)
            s = gj * sg
            a_ref[:, full[0]] = (s * vj).astype(a_ref.dtype)
            da = d_act[j]
            dbuf[pl.ds(0, tm), mine[1]] = da * s
            dbuf[pl.ds(0, tm), mine[0]] = da * vj * (sg * (1.0 + gj * (1.0 - sg)))
            for here, there in zip(mine, full):
                head[:, there] = dbuf[pl.ds(0, past), here]
            _link_future(dbuf, slice(None), width, tm)
            for here, there in zip(mine, full):
                dy = dbuf[pl.ds(0, tm), here]
                for k, tap in enumerate(_conv_taps(ubuf, here, width, tm)):
                    dcw_ref[k:k + 1, there] += jnp.sum(tap * dy, axis=0, keepdims=True)
                dhu = sum(cw_ref[k:k + 1, there] * dbuf[pl.ds(F32_ROWS * (width - 1 - k), tm), here]
                          for k in range(width)).astype(dhu_ref.dtype)
                dhu_ref[:, there] = dhu
                dhn_parts.append(_dot_nt(dhu, wup_ref[there.start // n]))
        dhn = (dhn_parts[0] + dhn_parts[1]) + (dhn_parts[2] + dhn_parts[3])
        dhin_ref[...] = _norm_bwd_tile(dhn, h_ref[...], dh_out, g_ref[...], _valid_rows(r, tm), hn_ref, dg_ref)

    rev = lambda i: (nt - 1 - i, 0)
    return _launch(
        body, [dh, hu, hu, hc, h, g, w_up, cw, w_down], name="ffn_bwd", grid=(nt,),
        in_specs=[pl.BlockSpec((tm, d), rev), pl.BlockSpec((tm, ff), rev), pl.BlockSpec((halo_rows, ff), halo_index),
                  pl.BlockSpec((tm, ff), rev), pl.BlockSpec((tm, d), rev), _const((1, d)), _const(w_up.shape), _const(cw.shape), _const(w_down.shape)],
        out_specs=[pl.BlockSpec((tm, d), rev), pl.BlockSpec((tm, 2 * n), rev), pl.BlockSpec((tm, ff), rev),
                   pl.BlockSpec((tm, d), rev), _const((F32_ROWS, ff)), _const((F32_ROWS, d))],
        out_shape=[jax.ShapeDtypeStruct((t_len, d), F32), jax.ShapeDtypeStruct((t_len, 2 * n), ACT_DT),
                   jax.ShapeDtypeStruct((t_len, ff), ACT_DT), jax.ShapeDtypeStruct((t_len, d), ACT_DT),
                   jax.ShapeDtypeStruct((F32_ROWS, ff), F32), jax.ShapeDtypeStruct((F32_ROWS, d), F32)],
        scratch_shapes=[pltpu.VMEM((past + tm, 2 * n), F32), pltpu.VMEM((tm + past, 2 * n), F32), pltpu.VMEM((past, ff), F32)],
        ride=ride,
    )


V_CONV_B, V_B_A, V_B_X, V_LAMBDA = 0, 1, 2, 3
G_CONV_W, G_CONV_B, G_B_A, G_B_X, G_LAMBDA = 0, 4, 5, 6, 7


def _scan(a_ref, b_ref, edge, tm, reverse):
    nj = tm // F32_ROWS
    order = range(nj - 1, -1, -1) if reverse else range(nj)
    slab = lambda ref, j: ref[pl.ds(F32_ROWS * j, F32_ROWS), :]
    a_run = b_run = None
    for j in order:
        a_j, b_j = slab(a_ref, j), slab(b_ref, j)
        if a_run is not None:
            b_j = b_j + a_j * b_run
            a_j = a_j * a_run
            b_ref[pl.ds(F32_ROWS * j, F32_ROWS), :] = b_j
            a_ref[pl.ds(F32_ROWS * j, F32_ROWS), :] = a_j
        a_run, b_run = a_j, b_j
    sub = _sublane()
    shift = 1
    while shift < F32_ROWS:
        amount = F32_ROWS - shift if reverse else shift
        keep = (sub < F32_ROWS - shift) if reverse else (sub >= shift)
        b_run = jnp.where(keep, b_run + a_run * pltpu.roll(b_run, amount, 0), b_run)
        a_run = jnp.where(keep, a_run * pltpu.roll(a_run, amount, 0), a_run)
        shift *= 2
    outer = edge[0:1, :] if reverse else edge[F32_ROWS - 1:F32_ROWS, :]
    ends = b_run + a_run * outer
    if reverse:
        carry = jnp.where(sub == F32_ROWS - 1, outer, pltpu.roll(ends, F32_ROWS - 1, 0))
    else:
        carry = jnp.where(sub == 0, outer, pltpu.roll(ends, 1, 0))
    for j in range(nj):
        b_ref[pl.ds(F32_ROWS * j, F32_ROWS), :] = slab(b_ref, j) + slab(a_ref, j) * carry
    return slab(b_ref, 0 if reverse else nj - 1)


def _rg_gates(u, vec_ref, wa_ref, wx_ref, pre_scr, nb, bd):
    ub = u.astype(MXU_DT)
    for k in range(nb):
        blk = slice(k * bd, (k + 1) * bd)
        pre_scr[0, :, blk] = _dot(ub[:, blk], wa_ref[k])
        pre_scr[1, :, blk] = _dot(ub[:, blk], wx_ref[k])
    r_gate = _sigmoid(pre_scr[0] + vec_ref[V_B_A:V_B_A + 1, :])
    i_gate = _sigmoid(pre_scr[1] + vec_ref[V_B_X:V_B_X + 1, :])
    return r_gate, i_gate


def _rg_decay(r_gate, vec_ref):
    sp = _softplus(-vec_ref[V_LAMBDA:V_LAMBDA + 1, :])
    log_a = -RG_C * r_gate * sp
    a = jnp.exp(log_a)
    one_minus_a2 = jnp.maximum(-_expm1_neg(2.0 * log_a, a * a), 1e-30)
    inv_mult = lax.rsqrt(one_minus_a2)
    return a, one_minus_a2 * inv_mult, inv_mult, sp


def _rg_fwd(h, g, w_in, cw, vec, wa, wx, w_out, *, tm):
    t_len, d = h.shape
    nt = t_len // tm
    nq, _, n = w_in.shape
    dr = 2 * n
    width = cw.shape[0]
    past = _past_rows(width)
    nb, bd, _ = wa.shape

    def body(h_ref, g_ref, win_ref, cw_ref, vec_ref, wa_ref, wx_ref, wout_ref, out_ref, hh_ref, hs_ref, gates_ref,
             gbuf, rbuf, pre_scr, tail, edge):
        i = pl.program_id(0)

        @pl.when(i == 0)
        def _():
            tail[...] = jnp.zeros_like(tail)
            edge[...] = jnp.zeros_like(edge)

        h_in = h_ref[...]
        hn = _rms(h_in, g_ref[...])[0].astype(MXU_DT)
        rbuf[pl.ds(0, past), :] = tail[...]
        for q in range(2):
            gbuf[:, q * n:(q + 1) * n] = _dot(hn, win_ref[q])
            rbuf[pl.ds(past, tm), q * n:(q + 1) * n] = _dot(hn, win_ref[2 + q])
        hh_ref[:, 0:dr] = gbuf[...].astype(hh_ref.dtype)
        hh_ref[:, dr:2 * dr] = rbuf[pl.ds(past, tm), :].astype(hh_ref.dtype)
        tail[...] = rbuf[pl.ds(tm, past), :]
        _link_past(rbuf, slice(None), width, tm)
        taps = _conv_taps(rbuf, slice(None), width, tm)
        u = sum(cw_ref[k:k + 1, :] * taps[k] for k in range(width)) + vec_ref[V_CONV_B:V_CONV_B + 1, :]
        r_gate, i_gate = _rg_gates(u, vec_ref, wa_ref, wx_ref, pre_scr, nb, bd)
        for k, kept in enumerate((u, r_gate, i_gate)):
            gates_ref[:, k * dr:(k + 1) * dr] = kept.astype(gates_ref.dtype)
        a, mult, _, _ = _rg_decay(r_gate, vec_ref)
        hs_ref[:, dr:2 * dr] = a
        hs_ref[:, 2 * dr:3 * dr] = mult
        pre_scr[0] = a
        pre_scr[1] = jnp.where(_valid_rows(i, tm), mult * (i_gate * u), 0.0)
        edge[...] = _scan(pre_scr.at[0], pre_scr.at[1], edge[...], tm, reverse=False)
        hs = pre_scr[1]
        hs_ref[:, 0:dr] = hs
        y = hs * _gelu(gbuf[...])[0]
        out_ref[...] = h_in + _dot(y.astype(MXU_DT), wout_ref[...])

    row = lambda i: (i, 0)
    return pl.pallas_call(
        body, name="rg_fwd", grid=(nt,),
        in_specs=[pl.BlockSpec((tm, d), row), _const((1, d)), _const(w_in.shape), _const(cw.shape), _const(vec.shape),
                  _const(wa.shape), _const(wx.shape), _const(w_out.shape)],
        out_specs=[pl.BlockSpec((tm, d), row), pl.BlockSpec((tm, 2 * dr), row), pl.BlockSpec((tm, 3 * dr), row),
                   pl.BlockSpec((tm, 3 * dr), row)],
        out_shape=[jax.ShapeDtypeStruct((t_len, d), F32), jax.ShapeDtypeStruct((t_len, 2 * dr), ACT_DT),
                   jax.ShapeDtypeStruct((t_len, 3 * dr), F32), jax.ShapeDtypeStruct((t_len, 3 * dr), ACT_DT)],
        scratch_shapes=[pltpu.VMEM((tm, dr), F32), pltpu.VMEM((past + tm, dr), F32), pltpu.VMEM((2, tm, dr), F32),
                        pltpu.VMEM((past, dr), F32), pltpu.VMEM((F32_ROWS, dr), F32)],
        compiler_params=_params(),
    )(h, g, w_in, cw, vec, wa, wx, w_out)


def _rg_bwd(dh, hh, hs, gates, h, g, w_in, cw, vec, wa, wx, w_out, *, tm, ride=None):
    t_len, d = dh.shape
    nt = t_len // tm
    dr = hs.shape[1] // 3
    n = w_in.shape[2]
    width = cw.shape[0]
    nb, bd, _ = wa.shape
    past = _past_rows(width)
    halo_rows, halo_index = _halo_block(past, tm, nt)
    one = F32_ROWS

    def body(dh_ref, hh_ref, hhp_ref, hs_ref, hsp_ref, gates_ref, h_ref, g_ref, win_ref, cw_ref, vec_ref, wa_ref, wx_ref, wout_ref,
             dhin_ref, dhh_ref, y_ref, hn_ref, dvec_ref, dwa_ref, dwx_ref, dg_ref, rbuf, dbuf, pre_scr, hbuf, abuf, edge):
        i = pl.program_id(0)
        r = nt - 1 - i

        @pl.when(i == 0)
        def _():
            dbuf[pl.ds(tm, past), :] = jnp.zeros((past, dr), F32)
            abuf[pl.ds(tm, one), :] = jnp.zeros((one, dr), F32)
            edge[...] = jnp.zeros_like(edge)
            dvec_ref[...] = jnp.zeros_like(dvec_ref)
            dwa_ref[...] = jnp.zeros_like(dwa_ref)
            dwx_ref[...] = jnp.zeros_like(dwx_ref)
            dg_ref[...] = jnp.zeros_like(dg_ref)

        dh_out = dh_ref[...]
        gb = hh_ref[:, 0:dr].astype(F32)
        prev = hhp_ref[...].astype(F32)[halo_rows - past:, dr:2 * dr]
        rbuf[pl.ds(0, past), :] = jnp.where(r > 0, prev, 0.0)
        rbuf[pl.ds(past, tm), :] = hh_ref[:, dr:2 * dr].astype(F32)
        _link_past(rbuf, slice(None), width, tm)
        taps = _conv_taps(rbuf, slice(None), width, tm)
        ub = gates_ref[:, 0:dr].astype(MXU_DT)
        u, r_gate, i_gate = (gates_ref[:, k * dr:(k + 1) * dr].astype(F32) for k in range(3))
        hs_t, a, mult = (hs_ref[:, k * dr:(k + 1) * dr] for k in range(3))
        inv_mult = 1.0 / mult
        sp = _softplus(-vec_ref[V_LAMBDA:V_LAMBDA + 1, :])
        hbuf[pl.ds(0, one), :] = jnp.where(r > 0, hsp_ref[...], 0.0)
        hbuf[pl.ds(one, tm), :] = hs_t
        _link_past(hbuf, slice(None), 2, tm)
        h_prev = hbuf[pl.ds(0, tm), :]
        gate, th = _gelu(gb)
        y_ref[...] = (hs_t * gate).astype(y_ref.dtype)
        dy = _dot_nt(dh_out.astype(MXU_DT), wout_ref[...])
        d_gb = (dy * hs_t * _gelu_grad(gb, th)).astype(dhh_ref.dtype)
        dhh_ref[:, 0:dr] = d_gb
        dhn = sum(_dot_nt(d_gb[:, q * n:(q + 1) * n], win_ref[q]) for q in range(2))
        abuf[pl.ds(0, tm), :] = a
        _link_future(abuf, slice(None), 2, tm)
        pre_scr[0] = abuf[pl.ds(one, tm), :]
        pre_scr[1] = dy * gate
        edge[...] = _scan(pre_scr.at[0], pre_scr.at[1], edge[...], tm, reverse=True)
        abuf[pl.ds(tm, one), :] = abuf[pl.ds(0, one), :]
        d_hs = pre_scr[1]
        d_b = jnp.where(_valid_rows(r, tm), d_hs, 0.0)
        d_iu = d_b * mult
        d_log_a = d_hs * h_prev * a - d_b * (i_gate * u) * (a * a) * inv_mult
        dvec_ref[G_LAMBDA:G_LAMBDA + 1, :] += jnp.sum(d_log_a * r_gate, axis=0, keepdims=True) * (-RG_C)
        d_pre_r = d_log_a * (-RG_C * sp) * r_gate * (1.0 - r_gate)
        d_pre_i = d_iu * u * i_gate * (1.0 - i_gate)
        dvec_ref[G_B_A:G_B_A + 1, :] += jnp.sum(d_pre_r, axis=0, keepdims=True)
        dvec_ref[G_B_X:G_B_X + 1, :] += jnp.sum(d_pre_i, axis=0, keepdims=True)
        dbuf[pl.ds(0, tm), :] = d_iu * i_gate
        d_pre_r = d_pre_r.astype(MXU_DT)
        d_pre_i = d_pre_i.astype(MXU_DT)
        for k in range(nb):
            blk = slice(k * bd, (k + 1) * bd)
            dwa_ref[k] += _dot_tn(ub[:, blk], d_pre_r[:, blk])
            dwx_ref[k] += _dot_tn(ub[:, blk], d_pre_i[:, blk])
            dbuf[pl.ds(0, tm), blk] += _dot_nt(d_pre_r[:, blk], wa_ref[k]) + _dot_nt(d_pre_i[:, blk], wx_ref[k])
        du = dbuf[pl.ds(0, tm), :]
        dvec_ref[G_CONV_B:G_CONV_B + 1, :] += jnp.sum(du, axis=0, keepdims=True)
        for k in range(width):
            dvec_ref[G_CONV_W + k:G_CONV_W + k + 1, :] += jnp.sum(taps[k] * du, axis=0, keepdims=True)
        _link_future(dbuf, slice(None), width, tm)
        d_rb = _conv_back(dbuf, cw_ref, slice(None), width, tm)
        dbuf[pl.ds(tm, past), :] = dbuf[pl.ds(0, past), :]
        d_rb = d_rb.astype(dhh_ref.dtype)
        dhh_ref[:, dr:2 * dr] = d_rb
        dhn = dhn + sum(_dot_nt(d_rb[:, q * n:(q + 1) * n], win_ref[2 + q]) for q in range(2))
        dhin_ref[...] = _norm_bwd_tile(dhn, h_ref[...], dh_out, g_ref[...], _valid_rows(r, tm), hn_ref, dg_ref)

        @pl.when(i == nt - 1)
        def _():
            lam = vec_ref[V_LAMBDA:V_LAMBDA + 1, :]
            dvec_ref[G_LAMBDA:G_LAMBDA + 1, :] = dvec_ref[G_LAMBDA:G_LAMBDA + 1, :] * (-_sigmoid(-lam))

    rev = lambda i: (nt - 1 - i, 0)
    return _launch(
        body, [dh, hh, hh, hs, hs, gates, h, g, w_in, cw, vec, wa, wx, w_out], name="rg_bwd", grid=(nt,),
        in_specs=[pl.BlockSpec((tm, d), rev), pl.BlockSpec((tm, 2 * dr), rev), pl.BlockSpec((halo_rows, 2 * dr), halo_index),
                  pl.BlockSpec((tm, 3 * dr), rev),
                  pl.BlockSpec((one, dr), lambda i: (jnp.maximum((nt - 1 - i) * (tm // one) - 1, 0), 0)),
                  pl.BlockSpec((tm, 3 * dr), rev), pl.BlockSpec((tm, d), rev), _const((1, d)), _const(w_in.shape),
                  _const(cw.shape), _const(vec.shape), _const(wa.shape), _const(wx.shape), _const(w_out.shape)],
        out_specs=[pl.BlockSpec((tm, d), rev), pl.BlockSpec((tm, 2 * dr), rev), pl.BlockSpec((tm, dr), rev),
                   pl.BlockSpec((tm, d), rev), _const((F32_ROWS, dr)), _const(wa.shape), _const(wx.shape),
                   _const((F32_ROWS, d))],
        out_shape=[jax.ShapeDtypeStruct((t_len, d), F32), jax.ShapeDtypeStruct((t_len, 2 * dr), ACT_DT),
                   jax.ShapeDtypeStruct((t_len, dr), ACT_DT), jax.ShapeDtypeStruct((t_len, d), ACT_DT),
                   jax.ShapeDtypeStruct((F32_ROWS, dr), F32), jax.ShapeDtypeStruct(wa.shape, F32),
                   jax.ShapeDtypeStruct(wx.shape, F32), jax.ShapeDtypeStruct((F32_ROWS, d), F32)],
        scratch_shapes=[pltpu.VMEM((past + tm, dr), F32), pltpu.VMEM((tm + past, dr), F32), pltpu.VMEM((2, tm, dr), F32),
                        pltpu.VMEM((one + tm, dr), F32), pltpu.VMEM((tm + one, dr), F32), pltpu.VMEM((F32_ROWS, dr), F32)],
        ride=ride,
    )


def _weight_grad(a, b, nb, *, rows, ride=None):
    t_len, k_dim = a.shape
    n = b.shape[1] // nb
    nt = t_len // rows

    def body(a_ref, b_ref, out_ref, wire_ref):
        @pl.when(pl.program_id(1) == 0)
        def _():
            out_ref[...] = jnp.zeros_like(out_ref)

        out_ref[0] += _dot_tn(a_ref[...].astype(MXU_DT), b_ref[...].astype(MXU_DT))

        @pl.when(pl.program_id(1) == nt - 1)
        def _():
            wire_ref[...] = out_ref[...].astype(wire_ref.dtype)

    block = pl.BlockSpec((1, k_dim, n), lambda j, i: (j, 0, 0))
    return _launch(
        body, [a, b], name="weight_grad", grid=(nb, nt),
        in_specs=[pl.BlockSpec((rows, k_dim), lambda j, i: (i, 0)), pl.BlockSpec((rows, n), lambda j, i: (i, j))],
        out_specs=[block, block],
        out_shape=[jax.ShapeDtypeStruct((nb, k_dim, n), F32), jax.ShapeDtypeStruct((nb, k_dim, n), WIRE_DT)],
        ride=ride,
    )


def _adamw(w, m, v, parts, *, rows, layer=0, into=None):
    n_layers, n_rows, n_cols = w.shape
    nt = n_rows // rows
    n_parts = len(parts)

    def body(w_ref, m_ref, v_ref, *rest):
        part_refs, (g_ref, d_ref, nm_ref, nv_ref) = rest[:n_parts], rest[-4:]
        w_ref, m_ref, v_ref, g_ref, d_ref, nm_ref, nv_ref = (r.at[0] for r in (w_ref, m_ref, v_ref, g_ref, d_ref, nm_ref, nv_ref))
        grad = part_refs[0][...].astype(F32)
        for p in part_refs[1:]:
            grad = grad + p[...].astype(F32)
        new_m = ADAM_B1 * m_ref[...] + (1.0 - ADAM_B1) * grad
        new_v = ADAM_B2 * v_ref[...] + (1.0 - ADAM_B2) * (grad * grad)
        m_hat = new_m / (1.0 - ADAM_B1 ** ADAM_STEP)
        v_hat = new_v / (1.0 - ADAM_B2 ** ADAM_STEP)
        g_ref[...] = grad
        d_ref[...] = -ADAM_LR * (m_hat / (jnp.sqrt(v_hat) + ADAM_EPS) + ADAM_WD * w_ref[...])
        nm_ref[...] = new_m
        nv_ref[...] = new_v

    spec = pl.BlockSpec((rows, n_cols), lambda i: (i, 0))
    layer_spec = pl.BlockSpec((1, rows, n_cols), lambda i: (layer, i, 0))
    into = list(into or [])
    return pl.pallas_call(
        body, name="adamw", grid=(nt,),
        in_specs=[layer_spec] * 3 + [spec] * n_parts + [ANY] * len(into), out_specs=[layer_spec] * 4,
        out_shape=[jax.ShapeDtypeStruct(w.shape, F32)] * 4,
        input_output_aliases={3 + n_parts + k: k for k in range(len(into))},
        compiler_params=_params(),
    )(w, m, v, *parts, *into)


def _sum_stack(stack, *, rows):
    n_stack, n_rows, n_cols = stack.shape

    def body(stack_ref, out_ref):
        acc = stack_ref[0]
        for j in range(1, n_stack):
            acc = acc + stack_ref[j]
        out_ref[...] = acc

    return pl.pallas_call(
        body, name="sum_stack", grid=(n_rows // rows,),
        in_specs=[pl.BlockSpec((n_stack, rows, n_cols), lambda i: (0, i, 0))],
        out_specs=pl.BlockSpec((rows, n_cols), lambda i: (i, 0)),
        out_shape=jax.ShapeDtypeStruct((n_rows, n_cols), F32),
        compiler_params=_params(),
    )(stack)


def _sum_parts(own, recv, *, rows):
    n_rows, n_cols = own.shape
    n_recv = recv.shape[0]

    def body(own_ref, recv_ref, out_ref):
        acc = own_ref[...].astype(F32)
        for j in range(n_recv):
            acc = acc + recv_ref[j].astype(F32)
        out_ref[...] = acc

    return pl.pallas_call(
        body, name="sum_parts", grid=(n_rows // rows,),
        in_specs=[pl.BlockSpec((rows, n_cols), lambda i: (i, 0)), pl.BlockSpec((n_recv, rows, n_cols), lambda i: (0, i, 0))],
        out_specs=pl.BlockSpec((rows, n_cols), lambda i: (i, 0)),
        out_shape=jax.ShapeDtypeStruct(own.shape, F32),
        compiler_params=_params(),
    )(own, recv)


class _Swap:
    def __init__(self, arrays):
        nk = len(arrays)
        self.arrays = list(arrays)
        self.out_shape = [jax.ShapeDtypeStruct(a.shape, a.dtype) for a in arrays]
        self.scratch = [pltpu.SemaphoreType.DMA((nk,)), pltpu.SemaphoreType.DMA((nk,))]

    def run(self, ins, outs, sems, start):
        send_sems, recv_sems = sems
        x, y, c = _place()
        for k in range(len(ins)):
            send = pltpu.make_async_remote_copy(src_ref=ins[k], dst_ref=outs[k], send_sem=send_sems.at[k],
                                                recv_sem=recv_sems.at[k], device_id=(x, y, 1 - c), device_id_type=MESH_ID)
            if start:
                send.start()
            else:
                send.wait_recv()
                send.wait_send()


class _AllDevices:
    def __init__(self, arrays):
        nk = len(arrays)
        self.arrays = list(arrays)
        self.out_shape = [jax.ShapeDtypeStruct((8,) + a.shape, a.dtype) for a in arrays]
        self.scratch = [pltpu.SemaphoreType.DMA((nk, 7)), pltpu.SemaphoreType.DMA((nk, 7)), pltpu.SemaphoreType.DMA((nk,))]

    def run(self, ins, outs, sems, start):
        send_sems, recv_sems, local_sems = sems
        x, y, c = _place()
        mine = 4 * x + 2 * y + c
        for k in range(len(ins)):
            local = pltpu.make_async_copy(ins[k], outs[k].at[mine], local_sems.at[k])
            local.start() if start else local.wait()
            for flip in range(1, 8):
                px, py, pc = x ^ (flip >> 2), y ^ ((flip >> 1) & 1), c ^ (flip & 1)
                sems_f = dict(send_sem=send_sems.at[k, flip - 1], recv_sem=recv_sems.at[k, flip - 1],
                              device_id=(px, py, pc), device_id_type=MESH_ID)
                send = pltpu.make_async_remote_copy(src_ref=ins[k], dst_ref=outs[k].at[mine], **sems_f)
                if start:
                    send.start()
                else:
                    pltpu.make_async_remote_copy(src_ref=ins[k], dst_ref=outs[k].at[4 * px + 2 * py + pc], **sems_f).wait_recv()
                    send.wait_send()


class _Both:
    def __init__(self, first, second):
        self.rides = (first, second)
        self.arrays = first.arrays + second.arrays
        self.out_shape = first.out_shape + second.out_shape
        self.scratch = first.scratch + second.scratch

    def run(self, ins, outs, sems, start):
        for ride in self.rides:
            n_in, n_out, n_sem = len(ride.arrays), len(ride.out_shape), len(ride.scratch)
            ride.run(ins[:n_in], outs[:n_out], sems[:n_sem], start)
            ins, outs, sems = ins[n_in:], outs[n_out:], sems[n_sem:]


def _pack(arrays, pad_rows=F32_ROWS):
    flat = jnp.concatenate([a.reshape(-1).astype(F32) for a in arrays])
    rows = -(-flat.shape[0] // (LANES * pad_rows)) * pad_rows
    return jnp.pad(flat, (0, rows * LANES - flat.shape[0])).reshape(rows, LANES)


def _unpack(packed, shapes):
    flat, out, off = packed.reshape(-1), [], 0
    for s in shapes:
        size = 1
        for dim in s:
            size *= dim
        out.append(flat[off:off + size].reshape(s))
        off += size
    return out


def _divisor_rows(n_rows, most=256):
    best = None
    for r in range(ACT_ROWS, most + 1, ACT_ROWS):
        if n_rows % r == 0:
            best = r
    return best or n_rows


def kernel(x, meta_tokens, norm_mix_g, norm_ffn_g, final_norm_g, sc_w_in, sc_conv_w, sc_w_out, rg_w_in, rg_conv_w, rg_conv_b, rg_w_gate_a, rg_b_gate_a, rg_w_gate_x, rg_b_gate_x, rg_lambda, rg_w_out, ffn_w_up, ffn_conv_w, ffn_w_down, loss_target, m_meta_tokens, m_norm_mix_g, m_norm_ffn_g, m_final_norm_g, m_sc_w_in, m_sc_conv_w, m_sc_w_out, m_rg_w_in, m_rg_conv_w, m_rg_conv_b, m_rg_w_gate_a, m_rg_b_gate_a, m_rg_w_gate_x, m_rg_b_gate_x, m_rg_lambda, m_rg_w_out, m_ffn_w_up, m_ffn_conv_w, m_ffn_w_down, v_meta_tokens, v_norm_mix_g, v_norm_ffn_g, v_final_norm_g, v_sc_w_in, v_sc_conv_w, v_sc_w_out, v_rg_w_in, v_rg_conv_w, v_rg_conv_b, v_rg_w_gate_a, v_rg_b_gate_a, v_rg_w_gate_x, v_rg_b_gate_x, v_rg_lambda, v_rg_w_out, v_ffn_w_up, v_ffn_conv_w, v_ffn_w_down):
    weights = dict(meta_tokens=meta_tokens, norm_mix_g=norm_mix_g, norm_ffn_g=norm_ffn_g, final_norm_g=final_norm_g, sc_w_in=sc_w_in, sc_conv_w=sc_conv_w, sc_w_out=sc_w_out, rg_w_in=rg_w_in, rg_conv_w=rg_conv_w, rg_conv_b=rg_conv_b, rg_w_gate_a=rg_w_gate_a, rg_b_gate_a=rg_b_gate_a, rg_w_gate_x=rg_w_gate_x, rg_b_gate_x=rg_b_gate_x, rg_lambda=rg_lambda, rg_w_out=rg_w_out, ffn_w_up=ffn_w_up, ffn_conv_w=ffn_conv_w, ffn_w_down=ffn_w_down)
    m_in = dict(meta_tokens=m_meta_tokens, norm_mix_g=m_norm_mix_g, norm_ffn_g=m_norm_ffn_g, final_norm_g=m_final_norm_g, sc_w_in=m_sc_w_in, sc_conv_w=m_sc_conv_w, sc_w_out=m_sc_w_out, rg_w_in=m_rg_w_in, rg_conv_w=m_rg_conv_w, rg_conv_b=m_rg_conv_b, rg_w_gate_a=m_rg_w_gate_a, rg_b_gate_a=m_rg_b_gate_a, rg_w_gate_x=m_rg_w_gate_x, rg_b_gate_x=m_rg_b_gate_x, rg_lambda=m_rg_lambda, rg_w_out=m_rg_w_out, ffn_w_up=m_ffn_w_up, ffn_conv_w=m_ffn_conv_w, ffn_w_down=m_ffn_w_down)
    v_in = dict(meta_tokens=v_meta_tokens, norm_mix_g=v_norm_mix_g, norm_ffn_g=v_norm_ffn_g, final_norm_g=v_final_norm_g, sc_w_in=v_sc_w_in, sc_conv_w=v_sc_conv_w, sc_w_out=v_sc_w_out, rg_w_in=v_rg_w_in, rg_conv_w=v_rg_conv_w, rg_conv_b=v_rg_conv_b, rg_w_gate_a=v_rg_w_gate_a, rg_b_gate_a=v_rg_b_gate_a, rg_w_gate_x=v_rg_w_gate_x, rg_b_gate_x=v_rg_b_gate_x, rg_lambda=v_rg_lambda, rg_w_out=v_rg_w_out, ffn_w_up=v_ffn_w_up, ffn_conv_w=v_ffn_conv_w, ffn_w_down=v_ffn_w_down)
    names = list(weights)

    seq, d = x.shape[1:]
    tm = _row_tile(seq)
    tokens, target = _tile_order(x[0], tm), _tile_order(loss_target[0], tm)
    t_len = seq + tm
    wg_rows = 5 * tm if t_len % (5 * tm) == 0 else tm
    wg_rows_in = 13 * tm if t_len % (13 * tm) == 0 else wg_rows
    xi, yi, _ = _place()
    chip = 2 * xi + yi
    mesh_axes = ("x", "y", "c")

    wire = lambda w: w.astype(WIRE_DT)
    small_sharded = ["meta_tokens", "sc_conv_w", "rg_conv_w", "rg_conv_b", "rg_b_gate_a", "rg_b_gate_x", "rg_lambda", "ffn_conv_w"]
    small_2d = {n: weights[n].reshape(-1, weights[n].shape[-1]) for n in small_sharded}
    w_sc_in, w_sc_out, small_by_chip = _exchange(
        _GatherHalves([wire(sc_w_in[0]), wire(sc_w_out[0]), _pack([small_2d[n] for n in small_sharded], 2 * ACT_ROWS)]),
        "gather_first")
    w_sc_out = w_sc_out.reshape(-1, d)
    gather_ffn0 = _GatherHalves([wire(ffn_w_up[0]), wire(ffn_w_down[0])])
    gather_rest = _Gather([wire(rg_w_in[0]), wire(rg_w_out[0]), wire(ffn_w_up[1]), wire(ffn_w_down[1])])
    small_len = sum(a.size for a in small_2d.values())
    by_chip = small_by_chip.reshape(N_CHIPS, -1)[:, :small_len]
    full, off = {}, 0
    for n in small_sharded:
        rows, width = small_2d[n].shape
        full[n] = by_chip[:, off:off + rows * width].reshape(N_CHIPS, rows, width).transpose(1, 0, 2).reshape(rows, N_CHIPS * width)
        off += rows * width
    sc_cw, rg_cw = full["sc_conv_w"], full["rg_conv_w"]
    ffn_cw = [full["ffn_conv_w"][0:3], full["ffn_conv_w"][3:6]]
    d_rnn = rg_cw.shape[1]
    vec = jnp.concatenate([full["rg_conv_b"], full["rg_b_gate_a"], full["rg_b_gate_x"], full["rg_lambda"],
                           jnp.zeros((F32_ROWS - 4, d_rnn), F32)])
    wa, wx = rg_w_gate_a[0].astype(MXU_DT), rg_w_gate_x[0].astype(MXU_DT)
    first = _tile_order(jnp.concatenate([jnp.zeros((tm - N_META, d), F32), full["meta_tokens"]]), tm)
    g_mix = [norm_mix_g[0:1], norm_mix_g[1:2]]
    g_ffn = [norm_ffn_g[0:1], norm_ffn_g[1:2]]

    h1, hh0, w_up0, w_dn0 = _sc_fwd(tokens, first, g_mix[0], w_sc_in, sc_cw, w_sc_out, tm=tm, ride=gather_ffn0)
    h2, hu0, hc0, w_rg_in, w_rg_out, w_up1, w_dn1 = _ffn_fwd(h1, g_ffn[0], w_up0, ffn_cw[0], w_dn0.reshape(-1, d), tm=tm,
                                                         ride=gather_rest)
    w_up, w_dn, w_rg_out = [w_up0, w_up1], [w_dn0.reshape(-1, d), w_dn1.reshape(-1, d)], w_rg_out.reshape(-1, d)
    h3, hh1, hs, gates = _rg_fwd(h2, g_mix[1], w_rg_in, rg_cw, vec, wa, wx, w_rg_out, tm=tm)
    dh4, hu1, hc1, sq, d_final = _ffn_fwd(h3, g_ffn[1], w_up[1], ffn_cw[1], w_dn[1], tm=tm,
                                     loss=(target, final_norm_g.reshape(1, d)))
    loss = lax.psum(jnp.sum(sq[0]) * (0.5 / d), mesh_axes)

    def by_chip_rows(pair):
        return [p.reshape(N_CHIPS, -1, d) for p in pair]

    def ffn_backward(dh_out, h_in, hu, hc, layer, ride):
        dh_in, act, dhu, hn, dcw, dg, *landed = _ffn_bwd(dh_out, hu, hc, h_in, g_ffn[layer], w_up[layer], ffn_cw[layer],
                                                         w_dn[layer], tm=tm, ride=ride)
        d_up = _weight_grad(hn, dhu, N_CHIPS, rows=wg_rows_in)
        d_dn = by_chip_rows(_weight_grad(act, dh_out, 1, rows=wg_rows))
        return dh_in, d_up, d_dn, dcw[0:3], dg[0], landed

    dh3, d_up1, d_dn1, d_fcw1, d_gf1, _ = ffn_backward(dh4, h3, hu1, hc1, 1, None)
    dh2, dhh1, y_rg, hn_rg, d_vec, d_wa, d_wx, d_gm1, *landed_ffn1 = _rg_bwd(
        dh3, hh1, hs, gates, h2, g_mix[1], w_rg_in, rg_cw, vec, wa, wx, w_rg_out, tm=tm,
        ride=_Scatter([d_up1[1], d_dn1[1]]))
    d_rg_in = _weight_grad(hn_rg, dhh1, N_CHIPS, rows=wg_rows_in)
    d_rg_out = by_chip_rows(_weight_grad(y_rg, dh3, 1, rows=wg_rows))
    early = {"rg_conv_w": d_vec[G_CONV_W:G_CONV_W + 4], "rg_conv_b": d_vec[G_CONV_B:G_CONV_B + 1],
             "rg_b_gate_a": d_vec[G_B_A:G_B_A + 1], "rg_b_gate_x": d_vec[G_B_X:G_B_X + 1],
             "rg_lambda": d_vec[G_LAMBDA:G_LAMBDA + 1], "ffn_conv_w.1": d_fcw1, "norm_mix_g.1": d_gm1[0:1],
             "norm_ffn_g.1": d_gf1[None], "final_norm_g": d_final[0]}
    early_packed = _pack(list(early.values()))
    gate_names = ["rg_w_gate_a", "rg_w_gate_x"]
    to_all = _AllDevices([early_packed, d_wa.reshape(-1, LANES), d_wx.reshape(-1, LANES)])
    dh1, d_up0, d_dn0, d_fcw0, d_gf0, landed = ffn_backward(
        dh2, h1, hu0, hc0, 0, _Both(_Scatter([d_rg_in[1], d_rg_out[1]]), to_all))
    landed_rg, early_by_device, gates_by_device = landed[0:2], landed[2], landed[3:]

    def core_sum(pair, received):
        own = lax.dynamic_index_in_dim(pair[0], chip, 0, keepdims=False)
        return _sum_parts(own, received, rows=_divisor_rows(own.shape[0]))

    early_big = [("rg_w_in", 0), ("rg_w_out", 0), ("ffn_w_up", 1), ("ffn_w_down", 1)]
    early_sum = [core_sum(d_rg_in, landed_rg[0]), core_sum(d_rg_out, landed_rg[1]), core_sum(d_up1, landed_ffn1[0]),
                 core_sum(d_dn1, landed_ffn1[1])]
    grad_x, dhh0, z_sc, hn_sc, d_sccw, d_gm0, d_first, *landed = _sc_bwd(
        dh1, hh0, tokens, first, g_mix[0], w_sc_in, sc_cw, w_sc_out, tm=tm,
        ride=_Both(_Scatter([d_up0[1], d_dn0[1]]), _Swap(early_sum)))
    landed_ffn0, early_other = landed[0:2], landed[2:]
    late = {"meta_tokens": _time_order(d_first, tm)[tm - N_META:], "sc_conv_w": d_sccw[0:3], "ffn_conv_w.0": d_fcw0,
            "norm_mix_g.0": d_gm0[0:1], "norm_ffn_g.0": d_gf0[None]}
    late_packed = _pack(list(late.values()))
    ffn0_big = [("ffn_w_up", 0), ("ffn_w_down", 0)]
    ffn0_sum = [core_sum(d_up0, landed_ffn0[0]), core_sum(d_dn0, landed_ffn0[1])]
    *d_sc_in, ffn0_up_other, ffn0_dn_other, late_by_device = _weight_grad(
        hn_sc, dhh0, N_CHIPS, rows=wg_rows_in, ride=_Both(_Swap(ffn0_sum), _AllDevices([late_packed])))
    *d_sc_out, landed_sc_in = _weight_grad(z_sc, dh1, 1, rows=wg_rows, ride=_Scatter([d_sc_in[1]]))
    d_sc_out = by_chip_rows(d_sc_out)
    landed_sc = [landed_sc_in, *_exchange(_Scatter([d_sc_out[1]]), "scatter_last")]
    grad_x = _time_order(grad_x, tm)[None]

    sc_big = [("sc_w_in", 0), ("sc_w_out", 0)]
    sc_sum = [core_sum(d_sc_in, landed_sc[0]), core_sum(d_sc_out, landed_sc[1])]
    sc_other = _exchange(_Swap(sc_sum), "swap_cores")
    out = {k: {} for k in ("grad", "delta", "m", "v")}
    stacked = {}
    for (n, layer), mine, theirs in zip(sc_big + ffn0_big + early_big, sc_sum + ffn0_sum + early_sum,
                                        [*sc_other, ffn0_up_other, ffn0_dn_other, *early_other]):
        stacked[n] = _adamw(weights[n], m_in[n], v_in[n], [mine, theirs], rows=_divisor_rows(mine.shape[0]), layer=layer,
                            into=stacked.get(n))
    for n, res in stacked.items():
        for k, key in enumerate(("grad", "delta", "m", "v")):
            out[key][n] = res[k]

    summed = {}
    for parts, packed, by_device in ((early, early_packed, early_by_device), (late, late_packed, late_by_device)):
        total = _sum_stack(by_device, rows=packed.shape[0])
        summed.update(zip(parts, _unpack(total, [p.shape for p in parts.values()])))
    for n in ("ffn_conv_w", "norm_mix_g", "norm_ffn_g"):
        summed[n] = jnp.concatenate([summed.pop(n + ".0"), summed.pop(n + ".1")])
    for n, by_device in zip(gate_names, gates_by_device):
        as_rows = lambda a: a.reshape(1, -1, LANES)
        res = _adamw(as_rows(weights[n]), as_rows(m_in[n]), as_rows(v_in[n]), [_sum_stack(by_device, rows=256)], rows=256)
        for k, key in enumerate(("grad", "delta", "m", "v")):
            out[key][n] = res[k].reshape(weights[n].shape)
    replicated = ["norm_mix_g", "norm_ffn_g", "final_norm_g"]
    small_names = small_sharded + replicated
    grads = {}
    for n in small_sharded:
        width = small_2d[n].shape[1]
        grads[n] = lax.dynamic_slice_in_dim(summed[n], chip * width, width, axis=1).reshape(weights[n].shape)
    for n in replicated:
        grads[n] = summed[n].reshape(weights[n].shape)
    shapes = [weights[n].shape for n in small_names]
    packed_w = _pack([weights[n] for n in small_names])
    res = _adamw(packed_w[None], _pack([m_in[n] for n in small_names])[None], _pack([v_in[n] for n in small_names])[None],
                 [_pack([grads[n] for n in small_names])], rows=packed_w.shape[0])
    for k, key in enumerate(("grad", "delta", "m", "v")):
        out[key].update(dict(zip(small_names, _unpack(res[k][0], shapes))))

    return (loss, grad_x, *[out["grad"][n] for n in names], *[out["delta"][n] for n in names],
            *[out["m"][n] for n in names], *[out["v"][n] for n in names])
```

```python
import functools

import jax
import jax.numpy as jnp
from jax import lax
from jax.experimental import pallas as pl
from jax.experimental.pallas import tpu as pltpu

F32 = jnp.float32
MXU_DT = jnp.bfloat16
ACT_DT = jnp.bfloat16
WIRE_DT = jnp.bfloat16
MESH_ID = pl.DeviceIdType.MESH

N_META = 16
RMS_EPS = 1e-6
RG_C = 8.0
ADAM_LR, ADAM_B1, ADAM_B2, ADAM_EPS, ADAM_WD, ADAM_STEP = 0.001, 0.9, 0.999, 1e-08, 0.01, 10
N_CHIPS = 4
VMEM_LIMIT = 60 * 1024 * 1024
F32_ROWS = 8
ACT_ROWS = 16
LANES = 128


def _row_tile(seq):
    for tm in (256, 128, 64, 32, 16):
        if seq % tm == 0:
            return tm
    raise ValueError(f"sequence length {seq} is not a multiple of 16")


def _params(n_axes=1, **kw):
    return pltpu.CompilerParams(dimension_semantics=("arbitrary",) * n_axes, vmem_limit_bytes=VMEM_LIMIT, **kw)


def _const(shape):
    return pl.BlockSpec(shape, lambda *_: (0,) * len(shape), pipeline_mode=pl.Buffered(1))


def _dot(a, b):
    return jnp.dot(a, b, preferred_element_type=F32)


def _dot_nt(a, b):
    return lax.dot_general(a, b, (((1,), (1,)), ((), ())), preferred_element_type=F32)


def _dot_tn(a, b):
    return lax.dot_general(a, b, (((0,), (0,)), ((), ())), preferred_element_type=F32)


def _sigmoid(x):
    return 0.5 + 0.5 * jnp.tanh(0.5 * x)


def _rms(h, g):
    rstd = lax.rsqrt(jnp.mean(h * h, axis=-1, keepdims=True) + RMS_EPS)
    xhat = h * rstd
    return xhat * g, xhat, rstd


def _rms_bwd(dhn, xhat, rstd, g):
    dx = dhn * g
    return rstd * (dx - xhat * jnp.mean(dx * xhat, axis=-1, keepdims=True))


def _gelu(x):
    k = 0.7978845608028654
    t = jnp.tanh(k * (x + 0.044715 * x * x * x))
    return 0.5 * x * (1.0 + t), t


def _gelu_grad(x, t):
    k = 0.7978845608028654
    return 0.5 * (1.0 + t) + 0.5 * x * (1.0 - t * t) * k * (1.0 + 3 * 0.044715 * x * x)


def _softplus(x):
    e = jnp.exp(-jnp.abs(x))
    return jnp.maximum(x, 0.0) + jnp.where(e < 1e-4, e - 0.5 * e * e, jnp.log(1.0 + e))


def _expm1_neg(z, exp_z):
    series = z * (1.0 + z * (0.5 + z * (1.0 / 6)))
    return jnp.where(z > -0.02, series, exp_z - 1.0)


def _tile_order(a, tm):
    return a.reshape(-1, F32_ROWS, tm // F32_ROWS, a.shape[-1]).swapaxes(1, 2).reshape(a.shape)


def _time_order(a, tm):
    return a.reshape(-1, tm // F32_ROWS, F32_ROWS, a.shape[-1]).swapaxes(1, 2).reshape(a.shape)


def _valid_rows(tile, tm):
    row = lax.broadcasted_iota(jnp.int32, (tm, 1), 0)
    time = (row & (F32_ROWS - 1)) * (tm // F32_ROWS) + (row >> 3) + tile * tm
    return time >= tm - N_META


def _sublane():
    return lax.broadcasted_iota(jnp.int32, (F32_ROWS, 1), 0)


def _past_rows(width):
    return (width - 1) * F32_ROWS


def _halo_block(past, tm, nt):
    rows = -(-past // ACT_ROWS) * ACT_ROWS
    return rows, lambda i: (jnp.maximum((nt - 1 - i) * (tm // rows) - 1, 0), 0)


def _link_past(buf, cols, width, tm):
    past = _past_rows(width)
    for k in range(1, width):
        rows = pl.ds(past - F32_ROWS * k, F32_ROWS)
        before = pltpu.roll(buf[rows, cols], 1, 0)
        mine = pltpu.roll(buf[pl.ds(past + tm - F32_ROWS * k, F32_ROWS), cols], 1, 0)
        buf[rows, cols] = jnp.where(_sublane() == 0, before, mine)


def _link_future(buf, cols, width, tm):
    for k in range(1, width):
        rows = pl.ds(tm + F32_ROWS * (k - 1), F32_ROWS)
        after = pltpu.roll(buf[rows, cols], F32_ROWS - 1, 0)
        mine = pltpu.roll(buf[pl.ds(F32_ROWS * (k - 1), F32_ROWS), cols], F32_ROWS - 1, 0)
        buf[rows, cols] = jnp.where(_sublane() == F32_ROWS - 1, after, mine)


def _conv_taps(buf, cols, width, tm):
    return [buf[pl.ds(F32_ROWS * k, tm), cols] for k in range(width)]


def _conv_back(buf, cw_ref, cols, width, tm):
    return sum(cw_ref[k:k + 1, cols] * buf[pl.ds(F32_ROWS * (width - 1 - k), tm), cols] for k in range(width))


ANY = pl.BlockSpec(memory_space=pl.ANY)


def _place():
    return lax.axis_index("x"), lax.axis_index("y"), lax.axis_index("c")


def _other_chips(x, y):
    return [(1 - x, y), (x, 1 - y), (1 - x, 1 - y)]


class _Gather:
    def __init__(self, shards):
        nk = len(shards)
        self.arrays = list(shards)
        self.out_shape = [jax.ShapeDtypeStruct((N_CHIPS,) + s.shape, s.dtype) for s in shards]
        self.scratch = [pltpu.SemaphoreType.DMA((nk, 3)), pltpu.SemaphoreType.DMA((nk, 3)), pltpu.SemaphoreType.DMA((nk,))]

    def run(self, ins, outs, sems, start):
        send_sems, recv_sems, local_sems = sems
        x, y, c = _place()
        mine = 2 * x + y
        for k in range(len(ins)):
            local = pltpu.make_async_copy(ins[k], outs[k].at[mine], local_sems.at[k])
            local.start() if start else local.wait()
            for j, (px, py) in enumerate(_other_chips(x, y)):
                sems_kj = dict(send_sem=send_sems.at[k, j], recv_sem=recv_sems.at[k, j], device_id=(px, py, c),
                               device_id_type=MESH_ID)
                send = pltpu.make_async_remote_copy(src_ref=ins[k], dst_ref=outs[k].at[mine], **sems_kj)
                if start:
                    send.start()
                else:
                    pltpu.make_async_remote_copy(src_ref=ins[k], dst_ref=outs[k].at[2 * px + py], **sems_kj).wait_recv()
                    send.wait_send()


class _GatherHalves:
    def __init__(self, shards):
        nk = len(shards)
        self.arrays = list(shards)
        self.out_shape = [jax.ShapeDtypeStruct((N_CHIPS,) + s.shape, s.dtype) for s in shards]
        self.scratch = [pltpu.SemaphoreType.DMA((nk, 3)) for _ in range(4)] + [pltpu.SemaphoreType.DMA((nk,))]

    def run(self, ins, outs, sems, start):
        far_send, far_recv, near_send, near_recv, local_sems = sems
        x, y, c = _place()
        mine = 2 * x + y
        for phase in ((0,) if start else (1, 2)):
            for k in range(len(ins)):
                half = ins[k].shape[0] // 2
                my_half = pl.ds(pl.multiple_of(c * half, ACT_ROWS), half)
                other_half = pl.ds(pl.multiple_of((1 - c) * half, ACT_ROWS), half)
                if phase != 1:
                    local = pltpu.make_async_copy(ins[k], outs[k].at[mine], local_sems.at[k])
                    local.start() if phase == 0 else local.wait()
                for j, (px, py) in enumerate(_other_chips(x, y)):
                    theirs = 2 * px + py
                    far = dict(send_sem=far_send.at[k, j], recv_sem=far_recv.at[k, j], device_id=(px, py, c),
                               device_id_type=MESH_ID)
                    near = dict(send_sem=near_send.at[k, j], recv_sem=near_recv.at[k, j], device_id=(x, y, 1 - c),
                                device_id_type=MESH_ID)
                    landed = outs[k].at[theirs, my_half]
                    send = lambda: pltpu.make_async_remote_copy(src_ref=ins[k].at[my_half], dst_ref=outs[k].at[mine, my_half], **far)
                    pass_on = lambda: pltpu.make_async_remote_copy(src_ref=landed, dst_ref=landed, **near)
                    if phase == 0:
                        send().start()
                    elif phase == 1:
                        pltpu.make_async_remote_copy(src_ref=ins[k].at[my_half], dst_ref=landed, **far).wait_recv()
                        pass_on().start()
                    else:
                        pltpu.make_async_remote_copy(src_ref=landed, dst_ref=outs[k].at[theirs, other_half], **near).wait_recv()
                        pass_on().wait_send()
                        send().wait_send()


class _Scatter:
    def __init__(self, parts):
        nk = len(parts)
        self.arrays = list(parts)
        self.out_shape = [jax.ShapeDtypeStruct((3,) + p.shape[1:], p.dtype) for p in parts]
        self.scratch = [pltpu.SemaphoreType.DMA((nk, 3)), pltpu.SemaphoreType.DMA((nk, 3))]

    def run(self, ins, outs, sems, start):
        send_sems, recv_sems = sems
        x, y, c = _place()
        for k in range(len(ins)):
            for j, (px, py) in enumerate(_other_chips(x, y)):
                send = pltpu.make_async_remote_copy(
                    src_ref=ins[k].at[2 * px + py], dst_ref=outs[k].at[j], send_sem=send_sems.at[k, j],
                    recv_sem=recv_sems.at[k, j], device_id=(px, py, c), device_id_type=MESH_ID)
                if start:
                    send.start()
                else:
                    send.wait_recv()
                    send.wait_send()


def _exchange(ride, name):
    n_in, n_out = len(ride.arrays), len(ride.out_shape)

    def body(*refs):
        ride.run(refs[:n_in], refs[n_in:n_in + n_out], refs[n_in + n_out:], start=True)
        ride.run(refs[:n_in], refs[n_in:n_in + n_out], refs[n_in + n_out:], start=False)

    return pl.pallas_call(body, name=name, in_specs=[ANY] * n_in, out_specs=[ANY] * n_out, out_shape=ride.out_shape,
                          scratch_shapes=ride.scratch)(*ride.arrays)


def _launch(body, operands, *, name, grid, in_specs, out_specs, out_shape, scratch_shapes=(), ride=None):
    common = dict(name=name, grid=grid, compiler_params=_params(len(grid)))
    if ride is None:
        return pl.pallas_call(body, in_specs=in_specs, out_specs=out_specs, out_shape=out_shape,
                              scratch_shapes=list(scratch_shapes), **common)(*operands)
    n_in, n_out, n_scr = len(operands), len(out_shape), len(scratch_shapes)
    r_in, r_out = len(ride.arrays), len(ride.out_shape)

    def riding(*refs):
        ins, refs = refs[:n_in], refs[n_in:]
        r_ins, refs = refs[:r_in], refs[r_in:]
        outs, refs = refs[:n_out], refs[n_out:]
        r_outs, refs = refs[:r_out], refs[r_out:]
        scr, r_sems = refs[:n_scr], refs[n_scr:]
        step = [pl.program_id(axis) for axis in range(len(grid))]
        first = functools.reduce(jnp.logical_and, [s == 0 for s in step])
        last = functools.reduce(jnp.logical_and, [s == size - 1 for s, size in zip(step, grid)])

        @pl.when(first)
        def _():
            ride.run(r_ins, r_outs, r_sems, start=True)

        body(*ins, *outs, *scr)

        @pl.when(last)
        def _():
            ride.run(r_ins, r_outs, r_sems, start=False)

    return pl.pallas_call(
        riding, in_specs=list(in_specs) + [ANY] * r_in, out_specs=list(out_specs) + [ANY] * r_out,
        out_shape=list(out_shape) + ride.out_shape, scratch_shapes=list(scratch_shapes) + ride.scratch, **common,
    )(*operands, *ride.arrays)


def _sc_fwd(x, first, g, w_in, cw, w_out, *, tm, ride=None):
    seq, d = x.shape
    nt = seq // tm + 1
    nq, _, n = w_in.shape
    width = cw.shape[0]
    past = _past_rows(width)

    def body(x_ref, first_ref, g_ref, win_ref, cw_ref, wout_ref, h1_ref, hh_ref, hh_scr, cbuf):
        i = pl.program_id(0)

        @pl.when(i == 0)
        def _():
            cbuf[pl.ds(0, past), :] = jnp.zeros((past, d), F32)

        h = jnp.where(i == 0, first_ref[...], x_ref[...])
        hn = _rms(h, g_ref[...])[0].astype(MXU_DT)
        for q in range(nq):
            hh_scr[:, q * n:(q + 1) * n] = _dot(hn, win_ref[q])
        hh_ref[...] = hh_scr[...].astype(hh_ref.dtype)
        b = hh_scr[:, 0:d]
        cbuf[pl.ds(past, tm), :] = hh_scr[:, d:2 * d] * hh_scr[:, 2 * d:3 * d]
        last = cbuf[pl.ds(tm, past), :]
        _link_past(cbuf, slice(None), width, tm)
        u = sum(cw_ref[k:k + 1, :] * tap for k, tap in enumerate(_conv_taps(cbuf, slice(None), width, tm)))
        cbuf[pl.ds(0, past), :] = last
        h1_ref[...] = h + _dot((b * u).astype(MXU_DT), wout_ref[...])

    return _launch(
        body, [x, first, g, w_in, cw, w_out], name="sc_fwd", grid=(nt,),
        in_specs=[pl.BlockSpec((tm, d), lambda i: (jnp.maximum(i - 1, 0), 0)), _const((tm, d)), _const((1, d)),
                  _const(w_in.shape), _const(cw.shape), _const(w_out.shape)],
        out_specs=[pl.BlockSpec((tm, d), lambda i: (i, 0)), pl.BlockSpec((tm, nq * n), lambda i: (i, 0))],
        out_shape=[jax.ShapeDtypeStruct((nt * tm, d), F32), jax.ShapeDtypeStruct((nt * tm, nq * n), ACT_DT)],
        scratch_shapes=[pltpu.VMEM((tm, nq * n), F32), pltpu.VMEM((past + tm, d), F32)],
        ride=ride,
    )


def _sc_bwd(dh, hh, x, first, g, w_in, cw, w_out, *, tm, ride=None):
    t_len, d = dh.shape
    nt = t_len // tm
    nq, _, n = w_in.shape
    width = cw.shape[0]
    past = _past_rows(width)
    halo_rows, halo_index = _halo_block(past, tm, nt)

    def body(dh_ref, hh_ref, hhp_ref, x_ref, first_ref, g_ref, win_ref, cw_ref, wout_ref,
             dx_ref, dhh_ref, z_ref, hn_ref, dcw_ref, dg_ref, dfirst_ref, cbuf, dbuf):
        i = pl.program_id(0)
        r = nt - 1 - i

        @pl.when(i == 0)
        def _():
            dbuf[pl.ds(tm, past), :] = jnp.zeros((past, d), F32)
            dcw_ref[...] = jnp.zeros_like(dcw_ref)
            dg_ref[...] = jnp.zeros_like(dg_ref)

        dh_out = dh_ref[...]
        b = hh_ref[:, 0:d].astype(F32)
        c = hh_ref[:, d:2 * d].astype(F32)
        v = hh_ref[:, 2 * d:3 * d].astype(F32)
        prev = hhp_ref[...].astype(F32)[halo_rows - past:, :]
        cbuf[pl.ds(0, past), :] = jnp.where(r > 0, prev[:, d:2 * d] * prev[:, 2 * d:3 * d], 0.0)
        cbuf[pl.ds(past, tm), :] = c * v
        _link_past(cbuf, slice(None), width, tm)
        taps = _conv_taps(cbuf, slice(None), width, tm)
        u = sum(cw_ref[k:k + 1, :] * taps[k] for k in range(width))
        z_ref[...] = (b * u).astype(z_ref.dtype)
        dz = _dot_nt(dh_out.astype(MXU_DT), wout_ref[...])
        d_b = (dz * u).astype(dhh_ref.dtype)
        dhh_ref[:, 0:d] = d_b
        parts = [_dot_nt(d_b[:, 0:n], win_ref[0])]
        du = dz * b
        for k in range(width):
            dcw_ref[k:k + 1, :] += jnp.sum(taps[k] * du, axis=0, keepdims=True)
        dbuf[pl.ds(0, tm), :] = du
        _link_future(dbuf, slice(None), width, tm)
        dcv = _conv_back(dbuf, cw_ref, slice(None), width, tm)
        dbuf[pl.ds(tm, past), :] = dbuf[pl.ds(0, past), :]
        d_c, d_v = (dcv * v).astype(dhh_ref.dtype), (dcv * c).astype(dhh_ref.dtype)
        dhh_ref[:, d:2 * d] = d_c
        dhh_ref[:, 2 * d:3 * d] = d_v
        rest = jnp.concatenate([d_b[:, n:], d_c, d_v], axis=1)
        parts += [_dot_nt(rest[:, (q - 1) * n:q * n], win_ref[q]) for q in range(1, nq)]
        dhn = functools.reduce(lambda a, b: a + b, parts)
        h_in = jnp.where(r == 0, first_ref[...], x_ref[...])
        dh_in = _norm_bwd_tile(dhn, h_in, dh_out, g_ref[...], _valid_rows(r, tm), hn_ref, dg_ref)

        @pl.when(r == 0)
        def _():
            dfirst_ref[...] = dh_in

        @pl.when(r > 0)
        def _():
            dx_ref[...] = dh_in

    rev = lambda i: (nt - 1 - i, 0)
    rev_x = lambda i: (jnp.maximum(nt - 2 - i, 0), 0)
    return _launch(
        body, [dh, hh, hh, x, first, g, w_in, cw, w_out], name="sc_bwd", grid=(nt,),
        in_specs=[pl.BlockSpec((tm, d), rev), pl.BlockSpec((tm, 3 * d), rev), pl.BlockSpec((halo_rows, 3 * d), halo_index),
                  pl.BlockSpec((tm, d), rev_x), _const((tm, d)), _const((1, d)), _const(w_in.shape), _const(cw.shape),
                  _const(w_out.shape)],
        out_specs=[pl.BlockSpec((tm, d), rev_x), pl.BlockSpec((tm, 3 * d), rev), pl.BlockSpec((tm, d), rev),
                   pl.BlockSpec((tm, d), rev), _const((F32_ROWS, d)), _const((F32_ROWS, d)), _const((tm, d))],
        out_shape=[jax.ShapeDtypeStruct((t_len - tm, d), F32), jax.ShapeDtypeStruct((t_len, 3 * d), ACT_DT),
                   jax.ShapeDtypeStruct((t_len, d), ACT_DT), jax.ShapeDtypeStruct((t_len, d), ACT_DT),
                   jax.ShapeDtypeStruct((F32_ROWS, d), F32), jax.ShapeDtypeStruct((F32_ROWS, d), F32),
                   jax.ShapeDtypeStruct((tm, d), F32)],
        scratch_shapes=[pltpu.VMEM((past + tm, d), F32), pltpu.VMEM((tm + past, d), F32)],
        ride=ride,
    )


def _ffn_fwd(h, g, w_up, cw, w_down, *, tm, ride=None, loss=None):
    t_len, d = h.shape
    nt = t_len // tm
    nq, _, n = w_up.shape
    width = cw.shape[0]
    past = _past_rows(width)

    def body(h_ref, g_ref, wup_ref, cw_ref, wdn_ref, *rest):
        if loss is None:
            out_ref, hu_ref, hc_ref, act_ref, ubuf, tail = rest
        else:
            t_ref, gf_ref, out_ref, hu_ref, hc_ref, act_ref, sq_ref, dgf_ref, ubuf, tail = rest
        i = pl.program_id(0)

        @pl.when(i == 0)
        def _():
            tail[...] = jnp.zeros_like(tail)

        h_in = h_ref[...]
        hn = _rms(h_in, g_ref[...])[0].astype(MXU_DT)
        ubuf[pl.ds(0, past), :] = tail[...]
        for q in range(nq):
            ubuf[pl.ds(past, tm), q * n:(q + 1) * n] = _dot(hn, wup_ref[q])
        hu_ref[...] = ubuf[pl.ds(past, tm), :].astype(hu_ref.dtype)
        tail[...] = ubuf[pl.ds(tm, past), :]
        _link_past(ubuf, slice(None), width, tm)
        acc = h_in
        for j in range(nq // 2):
            gcol, vcol = slice(j * n, (j + 1) * n), slice((nq // 2 + j) * n, (nq // 2 + j + 1) * n)
            conv = lambda cols: sum(cw_ref[k:k + 1, cols] * tap for k, tap in enumerate(_conv_taps(ubuf, cols, width, tm)))
            gj, vj = conv(gcol), conv(vcol)
            hc_ref[:, gcol] = gj.astype(hc_ref.dtype)
            hc_ref[:, vcol] = vj.astype(hc_ref.dtype)
            act = (gj * _sigmoid(gj) * vj).astype(MXU_DT)
            act_ref[:, gcol] = act.astype(act_ref.dtype)
            acc = acc + _dot(act, wdn_ref[j * n:(j + 1) * n, :])
        if loss is None:
            out_ref[...] = acc
            return

        @pl.when(i == 0)
        def _():
            sq_ref[...] = jnp.zeros_like(sq_ref)
            dgf_ref[...] = jnp.zeros_like(dgf_ref)
            out_ref[...] = jnp.zeros_like(out_ref)

        @pl.when(i > 0)
        def _():
            gain = gf_ref[...]
            out, xhat, rstd = _rms(acc, gain)
            err = out - t_ref[...]
            sq_ref[0:1, :] += jnp.sum(err * err, axis=0, keepdims=True)
            dout = err * (1.0 / d)
            dgf_ref[0:1, :] += jnp.sum(dout * xhat, axis=0, keepdims=True)
            out_ref[...] = _rms_bwd(dout, xhat, rstd, gain)

    row = lambda i: (i, 0)
    stat = jax.ShapeDtypeStruct((F32_ROWS, d), F32)
    return _launch(
        body, [h, g, w_up, cw, w_down] + list(loss or ()), name="ffn_fwd", grid=(nt,),
        in_specs=[pl.BlockSpec((tm, d), row), _const((1, d)), _const(w_up.shape), _const(cw.shape), _const(w_down.shape)]
        + ([pl.BlockSpec((tm, d), lambda i: (jnp.maximum(i - 1, 0), 0)), _const((1, d))] if loss else []),
        out_specs=[pl.BlockSpec((tm, d), row), pl.BlockSpec((tm, nq * n), row), pl.BlockSpec((tm, nq * n), row),
                   pl.BlockSpec((tm, nq // 2 * n), row)] + ([_const(stat.shape)] * 2 if loss else []),
        out_shape=[jax.ShapeDtypeStruct((t_len, d), F32), jax.ShapeDtypeStruct((t_len, nq * n), ACT_DT),
                   jax.ShapeDtypeStruct((t_len, nq * n), ACT_DT), jax.ShapeDtypeStruct((t_len, nq // 2 * n), ACT_DT)]
        + ([stat, stat] if loss else []),
        scratch_shapes=[pltpu.VMEM((past + tm, nq * n), F32), pltpu.VMEM((past, nq * n), F32)],
        ride=ride,
    )


def _norm_bwd_tile(dhn, h_in, dh, gain, valid, hn_ref, dg_ref):
    hn, xhat, rstd = _rms(h_in, gain)
    hn_ref[...] = hn.astype(hn_ref.dtype)
    dg_ref[0:1, :] += jnp.sum(dhn * xhat, axis=0, keepdims=True)
    return jnp.where(valid, dh + _rms_bwd(dhn, xhat, rstd, gain), 0.0)


def _ffn_bwd(dh, hu, hc, h, g, w_up, cw, w_down, *, tm, ride=None):
    t_len, d = dh.shape
    nt = t_len // tm
    ff = hu.shape[1]
    n = ff // 4
    width = cw.shape[0]
    past = _past_rows(width)
    halo_rows, halo_index = _halo_block(past, tm, nt)

    def body(dh_ref, hu_ref, hup_ref, hc_ref, h_ref, g_ref, wup_ref, cw_ref, wdn_ref,
             dhin_ref, dhu_ref, hn_ref, dcw_ref, dg_ref, ubuf, dbuf, head):
        i = pl.program_id(0)
        r = nt - 1 - i

        @pl.when(i == 0)
        def _():
            head[...] = jnp.zeros_like(head)
            dcw_ref[...] = jnp.zeros_like(dcw_ref)
            dg_ref[...] = jnp.zeros_like(dg_ref)

        dh_out = dh_ref[...]
        dhb = dh_out.astype(MXU_DT)
        dhn_parts = []
        d_act = [_dot_nt(dhb, wdn_ref[j * n:(j + 1) * n, :]) for j in range(2)]
        for j in range(2):
            mine = slice(0, n), slice(n, 2 * n)
            full = slice(j * n, (j + 1) * n), slice((2 + j) * n, (3 + j) * n)
            for here, there in zip(mine, full):
                prev = hup_ref[:, there].astype(F32)[halo_rows - past:, :]
                ubuf[pl.ds(0, past), here] = jnp.where(r > 0, prev, 0.0)
                ubuf[pl.ds(past, tm), here] = hu_ref[:, there].astype(F32)
                dbuf[pl.ds(tm, past), here] = head[:, there]
            _link_past(ubuf, slice(None), width, tm)
            gj, vj = hc_ref[:, full[0]].astype(F32), hc_ref[:, full[1]].astype(F32)
            sg = _sigmoid(gj)
            s = gj * sg
            da = d_act[j]
            dbuf[pl.ds(0, tm), mine[1]] = da * s
            dbuf[pl.ds(0, tm), mine[0]] = da * vj * (sg * (1.0 + gj * (1.0 - sg)))
            for here, there in zip(mine, full):
                head[:, there] = dbuf[pl.ds(0, past), here]
            _link_future(dbuf, slice(None), width, tm)
            for here, there in zip(mine, full):
                dy = dbuf[pl.ds(0, tm), here]
                for k, tap in enumerate(_conv_taps(ubuf, here, width, tm)):
                    dcw_ref[k:k + 1, there] += jnp.sum(tap * dy, axis=0, keepdims=True)
                dhu = sum(cw_ref[k:k + 1, there] * dbuf[pl.ds(F32_ROWS * (width - 1 - k), tm), here]
                          for k in range(width)).astype(dhu_ref.dtype)
                dhu_ref[:, there] = dhu
                dhn_parts.append(_dot_nt(dhu, wup_ref[there.start // n]))
        dhn = (dhn_parts[0] + dhn_parts[1]) + (dhn_parts[2] + dhn_parts[3])
        dhin_ref[...] = _norm_bwd_tile(dhn, h_ref[...], dh_out, g_ref[...], _valid_rows(r, tm), hn_ref, dg_ref)

    rev = lambda i: (nt - 1 - i, 0)
    return _launch(
        body, [dh, hu, hu, hc, h, g, w_up, cw, w_down], name="ffn_bwd", grid=(nt,),
        in_specs=[pl.BlockSpec((tm, d), rev), pl.BlockSpec((tm, ff), rev), pl.BlockSpec((halo_rows, ff), halo_index),
                  pl.BlockSpec((tm, ff), rev), pl.BlockSpec((tm, d), rev), _const((1, d)), _const(w_up.shape), _const(cw.shape), _const(w_down.shape)],
        out_specs=[pl.BlockSpec((tm, d), rev), pl.BlockSpec((tm, ff), rev),
                   pl.BlockSpec((tm, d), rev), _const((F32_ROWS, ff)), _const((F32_ROWS, d))],
        out_shape=[jax.ShapeDtypeStruct((t_len, d), F32),
                   jax.ShapeDtypeStruct((t_len, ff), ACT_DT), jax.ShapeDtypeStruct((t_len, d), ACT_DT),
                   jax.ShapeDtypeStruct((F32_ROWS, ff), F32), jax.ShapeDtypeStruct((F32_ROWS, d), F32)],
        scratch_shapes=[pltpu.VMEM((past + tm, 2 * n), F32), pltpu.VMEM((tm + past, 2 * n), F32), pltpu.VMEM((past, ff), F32)],
        ride=ride,
    )


V_CONV_B, V_B_A, V_B_X, V_LAMBDA = 0, 1, 2, 3
G_CONV_W, G_CONV_B, G_B_A, G_B_X, G_LAMBDA = 0, 4, 5, 6, 7


def _scan(a_ref, b_ref, edge, tm, reverse):
    nj = tm // F32_ROWS
    order = range(nj - 1, -1, -1) if reverse else range(nj)
    slab = lambda ref, j: ref[pl.ds(F32_ROWS * j, F32_ROWS), :]
    a_run = b_run = None
    for j in order:
        a_j, b_j = slab(a_ref, j), slab(b_ref, j)
        if a_run is not None:
            b_j = b_j + a_j * b_run
            a_j = a_j * a_run
            b_ref[pl.ds(F32_ROWS * j, F32_ROWS), :] = b_j
            a_ref[pl.ds(F32_ROWS * j, F32_ROWS), :] = a_j
        a_run, b_run = a_j, b_j
    sub = _sublane()
    shift = 1
    while shift < F32_ROWS:
        amount = F32_ROWS - shift if reverse else shift
        keep = (sub < F32_ROWS - shift) if reverse else (sub >= shift)
        b_run = jnp.where(keep, b_run + a_run * pltpu.roll(b_run, amount, 0), b_run)
        a_run = jnp.where(keep, a_run * pltpu.roll(a_run, amount, 0), a_run)
        shift *= 2
    outer = edge[0:1, :] if reverse else edge[F32_ROWS - 1:F32_ROWS, :]
    ends = b_run + a_run * outer
    if reverse:
        carry = jnp.where(sub == F32_ROWS - 1, outer, pltpu.roll(ends, F32_ROWS - 1, 0))
    else:
        carry = jnp.where(sub == 0, outer, pltpu.roll(ends, 1, 0))
    for j in range(nj):
        b_ref[pl.ds(F32_ROWS * j, F32_ROWS), :] = slab(b_ref, j) + slab(a_ref, j) * carry
    return slab(b_ref, 0 if reverse else nj - 1)


def _rg_gates(u, vec_ref, wa_ref, wx_ref, pre_scr, nb, bd):
    ub = u.astype(MXU_DT)
    for k in range(nb):
        blk = slice(k * bd, (k + 1) * bd)
        pre_scr[0, :, blk] = _dot(ub[:, blk], wa_ref[k])
        pre_scr[1, :, blk] = _dot(ub[:, blk], wx_ref[k])
    r_gate = _sigmoid(pre_scr[0] + vec_ref[V_B_A:V_B_A + 1, :])
    i_gate = _sigmoid(pre_scr[1] + vec_ref[V_B_X:V_B_X + 1, :])
    return r_gate, i_gate


def _rg_decay(r_gate, vec_ref):
    sp = _softplus(-vec_ref[V_LAMBDA:V_LAMBDA + 1, :])
    log_a = -RG_C * r_gate * sp
    a = jnp.exp(log_a)
    one_minus_a2 = jnp.maximum(-_expm1_neg(2.0 * log_a, a * a), 1e-30)
    inv_mult = lax.rsqrt(one_minus_a2)
    return a, one_minus_a2 * inv_mult, inv_mult, sp


def _rg_fwd(h, g, w_in, cw, vec, wa, wx, w_out, *, tm):
    t_len, d = h.shape
    nt = t_len // tm
    nq, _, n = w_in.shape
    dr = 2 * n
    width = cw.shape[0]
    past = _past_rows(width)
    nb, bd, _ = wa.shape

    def body(h_ref, g_ref, win_ref, cw_ref, vec_ref, wa_ref, wx_ref, wout_ref, out_ref, hh_ref, hs_ref, gates_ref,
             gbuf, rbuf, pre_scr, tail, edge):
        i = pl.program_id(0)

        @pl.when(i == 0)
        def _():
            tail[...] = jnp.zeros_like(tail)
            edge[...] = jnp.zeros_like(edge)

        h_in = h_ref[...]
        hn = _rms(h_in, g_ref[...])[0].astype(MXU_DT)
        rbuf[pl.ds(0, past), :] = tail[...]
        for q in range(2):
            gbuf[:, q * n:(q + 1) * n] = _dot(hn, win_ref[q])
            rbuf[pl.ds(past, tm), q * n:(q + 1) * n] = _dot(hn, win_ref[2 + q])
        hh_ref[:, 0:dr] = gbuf[...].astype(hh_ref.dtype)
        hh_ref[:, dr:2 * dr] = rbuf[pl.ds(past, tm), :].astype(hh_ref.dtype)
        tail[...] = rbuf[pl.ds(tm, past), :]
        _link_past(rbuf, slice(None), width, tm)
        taps = _conv_taps(rbuf, slice(None), width, tm)
        u = sum(cw_ref[k:k + 1, :] * taps[k] for k in range(width)) + vec_ref[V_CONV_B:V_CONV_B + 1, :]
        r_gate, i_gate = _rg_gates(u, vec_ref, wa_ref, wx_ref, pre_scr, nb, bd)
        for k, kept in enumerate((u, r_gate, i_gate)):
            gates_ref[:, k * dr:(k + 1) * dr] = kept.astype(gates_ref.dtype)
        a, mult, _, _ = _rg_decay(r_gate, vec_ref)
        hs_ref[:, dr:2 * dr] = a
        hs_ref[:, 2 * dr:3 * dr] = mult
        pre_scr[0] = a
        pre_scr[1] = jnp.where(_valid_rows(i, tm), mult * (i_gate * u), 0.0)
        edge[...] = _scan(pre_scr.at[0], pre_scr.at[1], edge[...], tm, reverse=False)
        hs = pre_scr[1]
        hs_ref[:, 0:dr] = hs
        gate, th = _gelu(gbuf[...])
        gates_ref[:, 3 * dr:4 * dr] = th.astype(gates_ref.dtype)
        y = hs * gate
        out_ref[...] = h_in + _dot(y.astype(MXU_DT), wout_ref[...])

    row = lambda i: (i, 0)
    return pl.pallas_call(
        body, name="rg_fwd", grid=(nt,),
        in_specs=[pl.BlockSpec((tm, d), row), _const((1, d)), _const(w_in.shape), _const(cw.shape), _const(vec.shape),
                  _const(wa.shape), _const(wx.shape), _const(w_out.shape)],
        out_specs=[pl.BlockSpec((tm, d), row), pl.BlockSpec((tm, 2 * dr), row), pl.BlockSpec((tm, 3 * dr), row),
                   pl.BlockSpec((tm, 4 * dr), row)],
        out_shape=[jax.ShapeDtypeStruct((t_len, d), F32), jax.ShapeDtypeStruct((t_len, 2 * dr), ACT_DT),
                   jax.ShapeDtypeStruct((t_len, 3 * dr), F32), jax.ShapeDtypeStruct((t_len, 4 * dr), ACT_DT)],
        scratch_shapes=[pltpu.VMEM((tm, dr), F32), pltpu.VMEM((past + tm, dr), F32), pltpu.VMEM((2, tm, dr), F32),
                        pltpu.VMEM((past, dr), F32), pltpu.VMEM((F32_ROWS, dr), F32)],
        compiler_params=_params(),
    )(h, g, w_in, cw, vec, wa, wx, w_out)


def _rg_bwd(dh, hh, hs, gates, h, g, w_in, cw, vec, wa, wx, w_out, *, tm, ride=None):
    t_len, d = dh.shape
    nt = t_len // tm
    dr = hs.shape[1] // 3
    n = w_in.shape[2]
    width = cw.shape[0]
    nb, bd, _ = wa.shape
    past = _past_rows(width)
    halo_rows, halo_index = _halo_block(past, tm, nt)
    one = F32_ROWS

    def body(dh_ref, hh_ref, hhp_ref, hs_ref, hsp_ref, gates_ref, h_ref, g_ref, win_ref, cw_ref, vec_ref, wa_ref, wx_ref, wout_ref,
             dhin_ref, dhh_ref, y_ref, hn_ref, dvec_ref, dwa_ref, dwx_ref, dg_ref, rbuf, dbuf, pre_scr, hbuf, abuf, edge):
        i = pl.program_id(0)
        r = nt - 1 - i

        @pl.when(i == 0)
        def _():
            dbuf[pl.ds(tm, past), :] = jnp.zeros((past, dr), F32)
            abuf[pl.ds(tm, one), :] = jnp.zeros((one, dr), F32)
            edge[...] = jnp.zeros_like(edge)
            dvec_ref[...] = jnp.zeros_like(dvec_ref)
            dwa_ref[...] = jnp.zeros_like(dwa_ref)
            dwx_ref[...] = jnp.zeros_like(dwx_ref)
            dg_ref[...] = jnp.zeros_like(dg_ref)

        dh_out = dh_ref[...]
        gb = hh_ref[:, 0:dr].astype(F32)
        prev = hhp_ref[...].astype(F32)[halo_rows - past:, dr:2 * dr]
        rbuf[pl.ds(0, past), :] = jnp.where(r > 0, prev, 0.0)
        rbuf[pl.ds(past, tm), :] = hh_ref[:, dr:2 * dr].astype(F32)
        _link_past(rbuf, slice(None), width, tm)
        taps = _conv_taps(rbuf, slice(None), width, tm)
        ub = gates_ref[:, 0:dr].astype(MXU_DT)
        u, r_gate, i_gate = (gates_ref[:, k * dr:(k + 1) * dr].astype(F32) for k in range(3))
        hs_t, a, mult = (hs_ref[:, k * dr:(k + 1) * dr] for k in range(3))
        inv_mult = 1.0 / mult
        sp = _softplus(-vec_ref[V_LAMBDA:V_LAMBDA + 1, :])
        hbuf[pl.ds(0, one), :] = jnp.where(r > 0, hsp_ref[...], 0.0)
        hbuf[pl.ds(one, tm), :] = hs_t
        _link_past(hbuf, slice(None), 2, tm)
        h_prev = hbuf[pl.ds(0, tm), :]
        th = gates_ref[:, 3 * dr:4 * dr].astype(F32)
        gate = 0.5 * gb * (1.0 + th)
        y_ref[...] = (hs_t * gate).astype(y_ref.dtype)
        dy = _dot_nt(dh_out.astype(MXU_DT), wout_ref[...])
        d_gb = (dy * hs_t * _gelu_grad(gb, th)).astype(dhh_ref.dtype)
        dhh_ref[:, 0:dr] = d_gb
        dhn = sum(_dot_nt(d_gb[:, q * n:(q + 1) * n], win_ref[q]) for q in range(2))
        abuf[pl.ds(0, tm), :] = a
        _link_future(abuf, slice(None), 2, tm)
        pre_scr[0] = abuf[pl.ds(one, tm), :]
        pre_scr[1] = dy * gate
        edge[...] = _scan(pre_scr.at[0], pre_scr.at[1], edge[...], tm, reverse=True)
        abuf[pl.ds(tm, one), :] = abuf[pl.ds(0, one), :]
        d_hs = pre_scr[1]
        d_b = jnp.where(_valid_rows(r, tm), d_hs, 0.0)
        d_iu = d_b * mult
        d_log_a = d_hs * h_prev * a - d_b * (i_gate * u) * (a * a) * inv_mult
        dvec_ref[G_LAMBDA:G_LAMBDA + 1, :] += jnp.sum(d_log_a * r_gate, axis=0, keepdims=True) * (-RG_C)
        d_pre_r = d_log_a * (-RG_C * sp) * r_gate * (1.0 - r_gate)
        d_pre_i = d_iu * u * i_gate * (1.0 - i_gate)
        dvec_ref[G_B_A:G_B_A + 1, :] += jnp.sum(d_pre_r, axis=0, keepdims=True)
        dvec_ref[G_B_X:G_B_X + 1, :] += jnp.sum(d_pre_i, axis=0, keepdims=True)
        dbuf[pl.ds(0, tm), :] = d_iu * i_gate
        d_pre_r = d_pre_r.astype(MXU_DT)
        d_pre_i = d_pre_i.astype(MXU_DT)
        for k in range(nb):
            blk = slice(k * bd, (k + 1) * bd)
            dwa_ref[k] += _dot_tn(ub[:, blk], d_pre_r[:, blk])
            dwx_ref[k] += _dot_tn(ub[:, blk], d_pre_i[:, blk])
            dbuf[pl.ds(0, tm), blk] += _dot_nt(d_pre_r[:, blk], wa_ref[k]) + _dot_nt(d_pre_i[:, blk], wx_ref[k])
        du = dbuf[pl.ds(0, tm), :]
        dvec_ref[G_CONV_B:G_CONV_B + 1, :] += jnp.sum(du, axis=0, keepdims=True)
        for k in range(width):
            dvec_ref[G_CONV_W + k:G_CONV_W + k + 1, :] += jnp.sum(taps[k] * du, axis=0, keepdims=True)
        _link_future(dbuf, slice(None), width, tm)
        d_rb = _conv_back(dbuf, cw_ref, slice(None), width, tm)
        dbuf[pl.ds(tm, past), :] = dbuf[pl.ds(0, past), :]
        d_rb = d_rb.astype(dhh_ref.dtype)
        dhh_ref[:, dr:2 * dr] = d_rb
        dhn = dhn + sum(_dot_nt(d_rb[:, q * n:(q + 1) * n], win_ref[2 + q]) for q in range(2))
        dhin_ref[...] = _norm_bwd_tile(dhn, h_ref[...], dh_out, g_ref[...], _valid_rows(r, tm), hn_ref, dg_ref)

        @pl.when(i == nt - 1)
        def _():
            lam = vec_ref[V_LAMBDA:V_LAMBDA + 1, :]
            dvec_ref[G_LAMBDA:G_LAMBDA + 1, :] = dvec_ref[G_LAMBDA:G_LAMBDA + 1, :] * (-_sigmoid(-lam))

    rev = lambda i: (nt - 1 - i, 0)
    return _launch(
        body, [dh, hh, hh, hs, hs, gates, h, g, w_in, cw, vec, wa, wx, w_out], name="rg_bwd", grid=(nt,),
        in_specs=[pl.BlockSpec((tm, d), rev), pl.BlockSpec((tm, 2 * dr), rev), pl.BlockSpec((halo_rows, 2 * dr), halo_index),
                  pl.BlockSpec((tm, 3 * dr), rev),
                  pl.BlockSpec((one, dr), lambda i: (jnp.maximum((nt - 1 - i) * (tm // one) - 1, 0), 0)),
                  pl.BlockSpec((tm, 4 * dr), rev), pl.BlockSpec((tm, d), rev), _const((1, d)), _const(w_in.shape),
                  _const(cw.shape), _const(vec.shape), _const(wa.shape), _const(wx.shape), _const(w_out.shape)],
        out_specs=[pl.BlockSpec((tm, d), rev), pl.BlockSpec((tm, 2 * dr), rev), pl.BlockSpec((tm, dr), rev),
                   pl.BlockSpec((tm, d), rev), _const((F32_ROWS, dr)), _const(wa.shape), _const(wx.shape),
                   _const((F32_ROWS, d))],
        out_shape=[jax.ShapeDtypeStruct((t_len, d), F32), jax.ShapeDtypeStruct((t_len, 2 * dr), ACT_DT),
                   jax.ShapeDtypeStruct((t_len, dr), ACT_DT), jax.ShapeDtypeStruct((t_len, d), ACT_DT),
                   jax.ShapeDtypeStruct((F32_ROWS, dr), F32), jax.ShapeDtypeStruct(wa.shape, F32),
                   jax.ShapeDtypeStruct(wx.shape, F32), jax.ShapeDtypeStruct((F32_ROWS, d), F32)],
        scratch_shapes=[pltpu.VMEM((past + tm, dr), F32), pltpu.VMEM((tm + past, dr), F32), pltpu.VMEM((2, tm, dr), F32),
                        pltpu.VMEM((one + tm, dr), F32), pltpu.VMEM((tm + one, dr), F32), pltpu.VMEM((F32_ROWS, dr), F32)],
        ride=ride,
    )


def _weight_grad(a, b, nb, *, rows, ride=None):
    t_len, k_dim = a.shape
    n = b.shape[1] // nb
    nt = t_len // rows

    def body(a_ref, b_ref, out_ref, wire_ref):
        @pl.when(pl.program_id(1) == 0)
        def _():
            out_ref[...] = jnp.zeros_like(out_ref)

        out_ref[0] += _dot_tn(a_ref[...].astype(MXU_DT), b_ref[...].astype(MXU_DT))

        @pl.when(pl.program_id(1) == nt - 1)
        def _():
            wire_ref[...] = out_ref[...].astype(wire_ref.dtype)

    block = pl.BlockSpec((1, k_dim, n), lambda j, i: (j, 0, 0))
    return _launch(
        body, [a, b], name="weight_grad", grid=(nb, nt),
        in_specs=[pl.BlockSpec((rows, k_dim), lambda j, i: (i, 0)), pl.BlockSpec((rows, n), lambda j, i: (i, j))],
        out_specs=[block, block],
        out_shape=[jax.ShapeDtypeStruct((nb, k_dim, n), F32), jax.ShapeDtypeStruct((nb, k_dim, n), WIRE_DT)],
        ride=ride,
    )


def _adamw(w, m, v, parts, *, rows, layer=0, into=None):
    n_layers, n_rows, n_cols = w.shape
    nt = n_rows // rows
    n_parts = len(parts)

    def body(w_ref, m_ref, v_ref, *rest):
        part_refs, (g_ref, d_ref, nm_ref, nv_ref) = rest[:n_parts], rest[-4:]
        w_ref, m_ref, v_ref, g_ref, d_ref, nm_ref, nv_ref = (r.at[0] for r in (w_ref, m_ref, v_ref, g_ref, d_ref, nm_ref, nv_ref))
        grad = part_refs[0][...].astype(F32)
        for p in part_refs[1:]:
            grad = grad + p[...].astype(F32)
        new_m = ADAM_B1 * m_ref[...] + (1.0 - ADAM_B1) * grad
        new_v = ADAM_B2 * v_ref[...] + (1.0 - ADAM_B2) * (grad * grad)
        m_hat = new_m / (1.0 - ADAM_B1 ** ADAM_STEP)
        v_hat = new_v / (1.0 - ADAM_B2 ** ADAM_STEP)
        g_ref[...] = grad
        d_ref[...] = -ADAM_LR * (m_hat / (jnp.sqrt(v_hat) + ADAM_EPS) + ADAM_WD * w_ref[...])
        nm_ref[...] = new_m
        nv_ref[...] = new_v

    spec = pl.BlockSpec((rows, n_cols), lambda i: (i, 0))
    layer_spec = pl.BlockSpec((1, rows, n_cols), lambda i: (layer, i, 0))
    into = list(into or [])
    return pl.pallas_call(
        body, name="adamw", grid=(nt,),
        in_specs=[layer_spec] * 3 + [spec] * n_parts + [ANY] * len(into), out_specs=[layer_spec] * 4,
        out_shape=[jax.ShapeDtypeStruct(w.shape, F32)] * 4,
        input_output_aliases={3 + n_parts + k: k for k in range(len(into))},
        compiler_params=_params(),
    )(w, m, v, *parts, *into)


def _sum_stack(stack, *, rows):
    n_stack, n_rows, n_cols = stack.shape

    def body(stack_ref, out_ref):
        acc = stack_ref[0]
        for j in range(1, n_stack):
            acc = acc + stack_ref[j]
        out_ref[...] = acc

    return pl.pallas_call(
        body, name="sum_stack", grid=(n_rows // rows,),
        in_specs=[pl.BlockSpec((n_stack, rows, n_cols), lambda i: (0, i, 0))],
        out_specs=pl.BlockSpec((rows, n_cols), lambda i: (i, 0)),
        out_shape=jax.ShapeDtypeStruct((n_rows, n_cols), F32),
        compiler_params=_params(),
    )(stack)


def _sum_parts(own, recv, *, rows):
    n_rows, n_cols = own.shape
    n_recv = recv.shape[0]

    def body(own_ref, recv_ref, out_ref):
        acc = own_ref[...].astype(F32)
        for j in range(n_recv):
            acc = acc + recv_ref[j].astype(F32)
        out_ref[...] = acc

    return pl.pallas_call(
        body, name="sum_parts", grid=(n_rows // rows,),
        in_specs=[pl.BlockSpec((rows, n_cols), lambda i: (i, 0)), pl.BlockSpec((n_recv, rows, n_cols), lambda i: (0, i, 0))],
        out_specs=pl.BlockSpec((rows, n_cols), lambda i: (i, 0)),
        out_shape=jax.ShapeDtypeStruct(own.shape, F32),
        compiler_params=_params(),
    )(own, recv)


class _Swap:
    def __init__(self, arrays):
        nk = len(arrays)
        self.arrays = list(arrays)
        self.out_shape = [jax.ShapeDtypeStruct(a.shape, a.dtype) for a in arrays]
        self.scratch = [pltpu.SemaphoreType.DMA((nk,)), pltpu.SemaphoreType.DMA((nk,))]

    def run(self, ins, outs, sems, start):
        send_sems, recv_sems = sems
        x, y, c = _place()
        for k in range(len(ins)):
            send = pltpu.make_async_remote_copy(src_ref=ins[k], dst_ref=outs[k], send_sem=send_sems.at[k],
                                                recv_sem=recv_sems.at[k], device_id=(x, y, 1 - c), device_id_type=MESH_ID)
            if start:
                send.start()
            else:
                send.wait_recv()
                send.wait_send()


class _AllDevices:
    def __init__(self, arrays):
        nk = len(arrays)
        self.arrays = list(arrays)
        self.out_shape = [jax.ShapeDtypeStruct((8,) + a.shape, a.dtype) for a in arrays]
        self.scratch = [pltpu.SemaphoreType.DMA((nk, 7)), pltpu.SemaphoreType.DMA((nk, 7)), pltpu.SemaphoreType.DMA((nk,))]

    def run(self, ins, outs, sems, start):
        send_sems, recv_sems, local_sems = sems
        x, y, c = _place()
        mine = 4 * x + 2 * y + c
        for k in range(len(ins)):
            local = pltpu.make_async_copy(ins[k], outs[k].at[mine], local_sems.at[k])
            local.start() if start else local.wait()
            for flip in range(1, 8):
                px, py, pc = x ^ (flip >> 2), y ^ ((flip >> 1) & 1), c ^ (flip & 1)
                sems_f = dict(send_sem=send_sems.at[k, flip - 1], recv_sem=recv_sems.at[k, flip - 1],
                              device_id=(px, py, pc), device_id_type=MESH_ID)
                send = pltpu.make_async_remote_copy(src_ref=ins[k], dst_ref=outs[k].at[mine], **sems_f)
                if start:
                    send.start()
                else:
                    pltpu.make_async_remote_copy(src_ref=ins[k], dst_ref=outs[k].at[4 * px + 2 * py + pc], **sems_f).wait_recv()
                    send.wait_send()


class _Both:
    def __init__(self, first, second):
        self.rides = (first, second)
        self.arrays = first.arrays + second.arrays
        self.out_shape = first.out_shape + second.out_shape
        self.scratch = first.scratch + second.scratch

    def run(self, ins, outs, sems, start):
        for ride in self.rides:
            n_in, n_out, n_sem = len(ride.arrays), len(ride.out_shape), len(ride.scratch)
            ride.run(ins[:n_in], outs[:n_out], sems[:n_sem], start)
            ins, outs, sems = ins[n_in:], outs[n_out:], sems[n_sem:]


def _pack(arrays, pad_rows=F32_ROWS):
    flat = jnp.concatenate([a.reshape(-1).astype(F32) for a in arrays])
    rows = -(-flat.shape[0] // (LANES * pad_rows)) * pad_rows
    return jnp.pad(flat, (0, rows * LANES - flat.shape[0])).reshape(rows, LANES)


def _unpack(packed, shapes):
    flat, out, off = packed.reshape(-1), [], 0
    for s in shapes:
        size = 1
        for dim in s:
            size *= dim
        out.append(flat[off:off + size].reshape(s))
        off += size
    return out


def _divisor_rows(n_rows, most=256):
    best = None
    for r in range(ACT_ROWS, most + 1, ACT_ROWS):
        if n_rows % r == 0:
            best = r
    return best or n_rows


def kernel(x, meta_tokens, norm_mix_g, norm_ffn_g, final_norm_g, sc_w_in, sc_conv_w, sc_w_out, rg_w_in, rg_conv_w, rg_conv_b, rg_w_gate_a, rg_b_gate_a, rg_w_gate_x, rg_b_gate_x, rg_lambda, rg_w_out, ffn_w_up, ffn_conv_w, ffn_w_down, loss_target, m_meta_tokens, m_norm_mix_g, m_norm_ffn_g, m_final_norm_g, m_sc_w_in, m_sc_conv_w, m_sc_w_out, m_rg_w_in, m_rg_conv_w, m_rg_conv_b, m_rg_w_gate_a, m_rg_b_gate_a, m_rg_w_gate_x, m_rg_b_gate_x, m_rg_lambda, m_rg_w_out, m_ffn_w_up, m_ffn_conv_w, m_ffn_w_down, v_meta_tokens, v_norm_mix_g, v_norm_ffn_g, v_final_norm_g, v_sc_w_in, v_sc_conv_w, v_sc_w_out, v_rg_w_in, v_rg_conv_w, v_rg_conv_b, v_rg_w_gate_a, v_rg_b_gate_a, v_rg_w_gate_x, v_rg_b_gate_x, v_rg_lambda, v_rg_w_out, v_ffn_w_up, v_ffn_conv_w, v_ffn_w_down):
    weights = dict(meta_tokens=meta_tokens, norm_mix_g=norm_mix_g, norm_ffn_g=norm_ffn_g, final_norm_g=final_norm_g, sc_w_in=sc_w_in, sc_conv_w=sc_conv_w, sc_w_out=sc_w_out, rg_w_in=rg_w_in, rg_conv_w=rg_conv_w, rg_conv_b=rg_conv_b, rg_w_gate_a=rg_w_gate_a, rg_b_gate_a=rg_b_gate_a, rg_w_gate_x=rg_w_gate_x, rg_b_gate_x=rg_b_gate_x, rg_lambda=rg_lambda, rg_w_out=rg_w_out, ffn_w_up=ffn_w_up, ffn_conv_w=ffn_conv_w, ffn_w_down=ffn_w_down)
    m_in = dict(meta_tokens=m_meta_tokens, norm_mix_g=m_norm_mix_g, norm_ffn_g=m_norm_ffn_g, final_norm_g=m_final_norm_g, sc_w_in=m_sc_w_in, sc_conv_w=m_sc_conv_w, sc_w_out=m_sc_w_out, rg_w_in=m_rg_w_in, rg_conv_w=m_rg_conv_w, rg_conv_b=m_rg_conv_b, rg_w_gate_a=m_rg_w_gate_a, rg_b_gate_a=m_rg_b_gate_a, rg_w_gate_x=m_rg_w_gate_x, rg_b_gate_x=m_rg_b_gate_x, rg_lambda=m_rg_lambda, rg_w_out=m_rg_w_out, ffn_w_up=m_ffn_w_up, ffn_conv_w=m_ffn_conv_w, ffn_w_down=m_ffn_w_down)
    v_in = dict(meta_tokens=v_meta_tokens, norm_mix_g=v_norm_mix_g, norm_ffn_g=v_norm_ffn_g, final_norm_g=v_final_norm_g, sc_w_in=v_sc_w_in, sc_conv_w=v_sc_conv_w, sc_w_out=v_sc_w_out, rg_w_in=v_rg_w_in, rg_conv_w=v_rg_conv_w, rg_conv_b=v_rg_conv_b, rg_w_gate_a=v_rg_w_gate_a, rg_b_gate_a=v_rg_b_gate_a, rg_w_gate_x=v_rg_w_gate_x, rg_b_gate_x=v_rg_b_gate_x, rg_lambda=v_rg_lambda, rg_w_out=v_rg_w_out, ffn_w_up=v_ffn_w_up, ffn_conv_w=v_ffn_conv_w, ffn_w_down=v_ffn_w_down)
    names = list(weights)

    seq, d = x.shape[1:]
    tm = _row_tile(seq)
    tokens, target = _tile_order(x[0], tm), _tile_order(loss_target[0], tm)
    t_len = seq + tm
    wg_rows = 5 * tm if t_len % (5 * tm) == 0 else tm
    wg_rows_in = 13 * tm if t_len % (13 * tm) == 0 else wg_rows
    xi, yi, _ = _place()
    chip = 2 * xi + yi
    mesh_axes = ("x", "y", "c")

    wire = lambda w: w.astype(WIRE_DT)
    small_sharded = ["meta_tokens", "sc_conv_w", "rg_conv_w", "rg_conv_b", "rg_b_gate_a", "rg_b_gate_x", "rg_lambda", "ffn_conv_w"]
    small_2d = {n: weights[n].reshape(-1, weights[n].shape[-1]) for n in small_sharded}
    w_sc_in, w_sc_out, small_by_chip = _exchange(
        _GatherHalves([wire(sc_w_in[0]), wire(sc_w_out[0]), _pack([small_2d[n] for n in small_sharded], 2 * ACT_ROWS)]),
        "gather_first")
    w_sc_out = w_sc_out.reshape(-1, d)
    gather_ffn0 = _Gather([wire(ffn_w_up[0]), wire(ffn_w_down[0])])
    gather_rest = _Gather([wire(rg_w_in[0]), wire(rg_w_out[0]), wire(ffn_w_up[1]), wire(ffn_w_down[1])])
    small_len = sum(a.size for a in small_2d.values())
    by_chip = small_by_chip.reshape(N_CHIPS, -1)[:, :small_len]
    full, off = {}, 0
    for n in small_sharded:
        rows, width = small_2d[n].shape
        full[n] = by_chip[:, off:off + rows * width].reshape(N_CHIPS, rows, width).transpose(1, 0, 2).reshape(rows, N_CHIPS * width)
        off += rows * width
    sc_cw, rg_cw = full["sc_conv_w"], full["rg_conv_w"]
    ffn_cw = [full["ffn_conv_w"][0:3], full["ffn_conv_w"][3:6]]
    d_rnn = rg_cw.shape[1]
    vec = jnp.concatenate([full["rg_conv_b"], full["rg_b_gate_a"], full["rg_b_gate_x"], full["rg_lambda"],
                           jnp.zeros((F32_ROWS - 4, d_rnn), F32)])
    wa, wx = rg_w_gate_a[0].astype(MXU_DT), rg_w_gate_x[0].astype(MXU_DT)
    first = _tile_order(jnp.concatenate([jnp.zeros((tm - N_META, d), F32), full["meta_tokens"]]), tm)
    g_mix = [norm_mix_g[0:1], norm_mix_g[1:2]]
    g_ffn = [norm_ffn_g[0:1], norm_ffn_g[1:2]]

    h1, hh0, w_up0, w_dn0 = _sc_fwd(tokens, first, g_mix[0], w_sc_in, sc_cw, w_sc_out, tm=tm, ride=gather_ffn0)
    h2, hu0, hc0, act0, w_rg_in, w_rg_out, w_up1, w_dn1 = _ffn_fwd(h1, g_ffn[0], w_up0, ffn_cw[0], w_dn0.reshape(-1, d), tm=tm,
                                                         ride=gather_rest)
    w_up, w_dn, w_rg_out = [w_up0, w_up1], [w_dn0.reshape(-1, d), w_dn1.reshape(-1, d)], w_rg_out.reshape(-1, d)
    h3, hh1, hs, gates = _rg_fwd(h2, g_mix[1], w_rg_in, rg_cw, vec, wa, wx, w_rg_out, tm=tm)
    dh4, hu1, hc1, act1, sq, d_final = _ffn_fwd(h3, g_ffn[1], w_up[1], ffn_cw[1], w_dn[1], tm=tm,
                                     loss=(target, final_norm_g.reshape(1, d)))
    loss = lax.psum(jnp.sum(sq[0]) * (0.5 / d), mesh_axes)

    def by_chip_rows(pair):
        return [p.reshape(N_CHIPS, -1, d) for p in pair]

    def ffn_backward(dh_out, h_in, hu, hc, act, layer, ride):
        dh_in, dhu, hn, dcw, dg, *landed = _ffn_bwd(dh_out, hu, hc, h_in, g_ffn[layer], w_up[layer], ffn_cw[layer],
                                                         w_dn[layer], tm=tm, ride=ride)
        d_up = _weight_grad(hn, dhu, N_CHIPS, rows=wg_rows_in)
        d_dn = by_chip_rows(_weight_grad(act, dh_out, 1, rows=wg_rows))
        return dh_in, d_up, d_dn, dcw[0:3], dg[0], landed

    dh3, d_up1, d_dn1, d_fcw1, d_gf1, _ = ffn_backward(dh4, h3, hu1, hc1, act1, 1, None)
    dh2, dhh1, y_rg, hn_rg, d_vec, d_wa, d_wx, d_gm1, *landed_ffn1 = _rg_bwd(
        dh3, hh1, hs, gates, h2, g_mix[1], w_rg_in, rg_cw, vec, wa, wx, w_rg_out, tm=tm,
        ride=_Scatter([d_up1[1], d_dn1[1]]))
    d_rg_in = _weight_grad(hn_rg, dhh1, N_CHIPS, rows=wg_rows_in)
    d_rg_out = by_chip_rows(_weight_grad(y_rg, dh3, 1, rows=wg_rows))
    early = {"rg_conv_w": d_vec[G_CONV_W:G_CONV_W + 4], "rg_conv_b": d_vec[G_CONV_B:G_CONV_B + 1],
             "rg_b_gate_a": d_vec[G_B_A:G_B_A + 1], "rg_b_gate_x": d_vec[G_B_X:G_B_X + 1],
             "rg_lambda": d_vec[G_LAMBDA:G_LAMBDA + 1], "ffn_conv_w.1": d_fcw1, "norm_mix_g.1": d_gm1[0:1],
             "norm_ffn_g.1": d_gf1[None], "final_norm_g": d_final[0]}
    early_packed = _pack(list(early.values()))
    gate_names = ["rg_w_gate_a", "rg_w_gate_x"]
    to_all = _AllDevices([early_packed, d_wa.reshape(-1, LANES), d_wx.reshape(-1, LANES)])
    dh1, d_up0, d_dn0, d_fcw0, d_gf0, landed = ffn_backward(
        dh2, h1, hu0, hc0, act0, 0, _Both(_Scatter([d_rg_in[1], d_rg_out[1]]), to_all))
    landed_rg, early_by_device, gates_by_device = landed[0:2], landed[2], landed[3:]

    def core_sum(pair, received):
        own = lax.dynamic_index_in_dim(pair[0], chip, 0, keepdims=False)
        return _sum_parts(own, received, rows=_divisor_rows(own.shape[0]))

    early_big = [("rg_w_in", 0), ("rg_w_out", 0), ("ffn_w_up", 1), ("ffn_w_down", 1)]
    early_sum = [core_sum(d_rg_in, landed_rg[0]), core_sum(d_rg_out, landed_rg[1]), core_sum(d_up1, landed_ffn1[0]),
                 core_sum(d_dn1, landed_ffn1[1])]
    grad_x, dhh0, z_sc, hn_sc, d_sccw, d_gm0, d_first, *landed = _sc_bwd(
        dh1, hh0, tokens, first, g_mix[0], w_sc_in, sc_cw, w_sc_out, tm=tm,
        ride=_Both(_Scatter([d_up0[1], d_dn0[1]]), _Swap(early_sum)))
    landed_ffn0, early_other = landed[0:2], landed[2:]
    late = {"meta_tokens": _time_order(d_first, tm)[tm - N_META:], "sc_conv_w": d_sccw[0:3], "ffn_conv_w.0": d_fcw0,
            "norm_mix_g.0": d_gm0[0:1], "norm_ffn_g.0": d_gf0[None]}
    late_packed = _pack(list(late.values()))
    ffn0_big = [("ffn_w_up", 0), ("ffn_w_down", 0)]
    ffn0_sum = [core_sum(d_up0, landed_ffn0[0]), core_sum(d_dn0, landed_ffn0[1])]
    *d_sc_in, ffn0_up_other, ffn0_dn_other, late_by_device = _weight_grad(
        hn_sc, dhh0, N_CHIPS, rows=wg_rows_in, ride=_Both(_Swap(ffn0_sum), _AllDevices([late_packed])))
    *d_sc_out, landed_sc_in = _weight_grad(z_sc, dh1, 1, rows=wg_rows, ride=_Scatter([d_sc_in[1]]))
    d_sc_out = by_chip_rows(d_sc_out)
    landed_sc = [landed_sc_in, *_exchange(_Scatter([d_sc_out[1]]), "scatter_last")]
    grad_x = _time_order(grad_x, tm)[None]

    sc_big = [("sc_w_in", 0), ("sc_w_out", 0)]
    sc_sum = [core_sum(d_sc_in, landed_sc[0]), core_sum(d_sc_out, landed_sc[1])]
    sc_other = _exchange(_Swap(sc_sum), "swap_cores")
    out = {k: {} for k in ("grad", "delta", "m", "v")}
    stacked = {}
    for (n, layer), mine, theirs in zip(sc_big + ffn0_big + early_big, sc_sum + ffn0_sum + early_sum,
                                        [*sc_other, ffn0_up_other, ffn0_dn_other, *early_other]):
        stacked[n] = _adamw(weights[n], m_in[n], v_in[n], [mine, theirs], rows=_divisor_rows(mine.shape[0]), layer=layer,
                            into=stacked.get(n))
    for n, res in stacked.items():
        for k, key in enumerate(("grad", "delta", "m", "v")):
            out[key][n] = res[k]

    summed = {}
    for parts, packed, by_device in ((early, early_packed, early_by_device), (late, late_packed, late_by_device)):
        total = _sum_stack(by_device, rows=packed.shape[0])
        summed.update(zip(parts, _unpack(total, [p.shape for p in parts.values()])))
    for n in ("ffn_conv_w", "norm_mix_g", "norm_ffn_g"):
        summed[n] = jnp.concatenate([summed.pop(n + ".0"), summed.pop(n + ".1")])
    for n, by_device in zip(gate_names, gates_by_device):
        as_rows = lambda a: a.reshape(1, -1, LANES)
        res = _adamw(as_rows(weights[n]), as_rows(m_in[n]), as_rows(v_in[n]), [_sum_stack(by_device, rows=256)], rows=256)
        for k, key in enumerate(("grad", "delta", "m", "v")):
            out[key][n] = res[k].reshape(weights[n].shape)
    replicated = ["norm_mix_g", "norm_ffn_g", "final_norm_g"]
    small_names = small_sharded + replicated
    grads = {}
    for n in small_sharded:
        width = small_2d[n].shape[1]
        grads[n] = lax.dynamic_slice_in_dim(summed[n], chip * width, width, axis=1).reshape(weights[n].shape)
    for n in replicated:
        grads[n] = summed[n].reshape(weights[n].shape)
    shapes = [weights[n].shape for n in small_names]
    packed_w = _pack([weights[n] for n in small_names])
    res = _adamw(packed_w[None], _pack([m_in[n] for n in small_names])[None], _pack([v_in[n] for n in small_names])[None],
                 [_pack([grads[n] for n in small_names])], rows=packed_w.shape[0])
    for k, key in enumerate(("grad", "delta", "m", "v")):
        out[key].update(dict(zip(small_names, _unpack(res[k][0], shapes))))

    return (loss, grad_x, *[out["grad"][n] for n in names], *[out["delta"][n] for n in names],
            *[out["m"][n] for n in names], *[out["v"][n] for n in names])
```

```python
import functools

import jax
import jax.numpy as jnp
from jax import lax
from jax.experimental import pallas as pl
from jax.experimental.pallas import tpu as pltpu

F32 = jnp.float32
MXU_DT = jnp.bfloat16
ACT_DT = jnp.bfloat16
WIRE_DT = jnp.bfloat16
MESH_ID = pl.DeviceIdType.MESH

N_META = 16
RMS_EPS = 1e-6
RG_C = 8.0
ADAM_LR, ADAM_B1, ADAM_B2, ADAM_EPS, ADAM_WD, ADAM_STEP = 0.001, 0.9, 0.999, 1e-08, 0.01, 10
N_CHIPS = 4
VMEM_LIMIT = 60 * 1024 * 1024
F32_ROWS = 8
ACT_ROWS = 16
LANES = 128


def _row_tile(seq):
    for tm in (256, 128, 64, 32, 16):
        if seq % tm == 0:
            return tm
    raise ValueError(f"sequence length {seq} is not a multiple of 16")


def _params(n_axes=1, **kw):
    return pltpu.CompilerParams(dimension_semantics=("arbitrary",) * n_axes, vmem_limit_bytes=VMEM_LIMIT, **kw)


def _const(shape):
    return pl.BlockSpec(shape, lambda *_: (0,) * len(shape), pipeline_mode=pl.Buffered(1))


def _dot(a, b):
    return jnp.dot(a, b, preferred_element_type=F32)


def _dot_nt(a, b):
    return lax.dot_general(a, b, (((1,), (1,)), ((), ())), preferred_element_type=F32)


def _dot_tn(a, b):
    return lax.dot_general(a, b, (((0,), (0,)), ((), ())), preferred_element_type=F32)


def _sigmoid(x):
    return 0.5 + 0.5 * jnp.tanh(0.5 * x)


def _rms(h, g):
    rstd = lax.rsqrt(jnp.mean(h * h, axis=-1, keepdims=True) + RMS_EPS)
    xhat = h * rstd
    return xhat * g, xhat, rstd


def _rms_bwd(dhn, xhat, rstd, g):
    dx = dhn * g
    return rstd * (dx - xhat * jnp.mean(dx * xhat, axis=-1, keepdims=True))


def _gelu(x):
    k = 0.7978845608028654
    t = jnp.tanh(k * (x + 0.044715 * x * x * x))
    return 0.5 * x * (1.0 + t), t


def _gelu_grad(x, t):
    k = 0.7978845608028654
    return 0.5 * (1.0 + t) + 0.5 * x * (1.0 - t * t) * k * (1.0 + 3 * 0.044715 * x * x)


def _softplus(x):
    e = jnp.exp(-jnp.abs(x))
    return jnp.maximum(x, 0.0) + jnp.where(e < 1e-4, e - 0.5 * e * e, jnp.log(1.0 + e))


def _expm1_neg(z, exp_z):
    series = z * (1.0 + z * (0.5 + z * (1.0 / 6)))
    return jnp.where(z > -0.02, series, exp_z - 1.0)


def _tile_order(a, tm):
    return a.reshape(-1, F32_ROWS, tm // F32_ROWS, a.shape[-1]).swapaxes(1, 2).reshape(a.shape)


def _time_order(a, tm):
    return a.reshape(-1, tm // F32_ROWS, F32_ROWS, a.shape[-1]).swapaxes(1, 2).reshape(a.shape)


def _valid_rows(tile, tm):
    row = lax.broadcasted_iota(jnp.int32, (tm, 1), 0)
    time = (row & (F32_ROWS - 1)) * (tm // F32_ROWS) + (row >> 3) + tile * tm
    return time >= tm - N_META


def _sublane():
    return lax.broadcasted_iota(jnp.int32, (F32_ROWS, 1), 0)


def _past_rows(width):
    return (width - 1) * F32_ROWS


def _halo_block(past, tm, nt):
    rows = -(-past // ACT_ROWS) * ACT_ROWS
    return rows, lambda i: (jnp.maximum((nt - 1 - i) * (tm // rows) - 1, 0), 0)


def _link_past(buf, cols, width, tm):
    past = _past_rows(width)
    for k in range(1, width):
        rows = pl.ds(past - F32_ROWS * k, F32_ROWS)
        before = pltpu.roll(buf[rows, cols], 1, 0)
        mine = pltpu.roll(buf[pl.ds(past + tm - F32_ROWS * k, F32_ROWS), cols], 1, 0)
        buf[rows, cols] = jnp.where(_sublane() == 0, before, mine)


def _link_future(buf, cols, width, tm):
    for k in range(1, width):
        rows = pl.ds(tm + F32_ROWS * (k - 1), F32_ROWS)
        after = pltpu.roll(buf[rows, cols], F32_ROWS - 1, 0)
        mine = pltpu.roll(buf[pl.ds(F32_ROWS * (k - 1), F32_ROWS), cols], F32_ROWS - 1, 0)
        buf[rows, cols] = jnp.where(_sublane() == F32_ROWS - 1, after, mine)


def _conv_taps(buf, cols, width, tm):
    return [buf[pl.ds(F32_ROWS * k, tm), cols] for k in range(width)]


def _conv_back(buf, cw_ref, cols, width, tm):
    return sum(cw_ref[k:k + 1, cols] * buf[pl.ds(F32_ROWS * (width - 1 - k), tm), cols] for k in range(width))


ANY = pl.BlockSpec(memory_space=pl.ANY)


def _place():
    return lax.axis_index("x"), lax.axis_index("y"), lax.axis_index("c")


def _other_chips(x, y):
    return [(1 - x, y), (x, 1 - y), (1 - x, 1 - y)]


class _Gather:
    def __init__(self, shards):
        nk = len(shards)
        self.arrays = list(shards)
        self.out_shape = [jax.ShapeDtypeStruct((N_CHIPS,) + s.shape, s.dtype) for s in shards]
        self.scratch = [pltpu.SemaphoreType.DMA((nk, 3)), pltpu.SemaphoreType.DMA((nk, 3)), pltpu.SemaphoreType.DMA((nk,))]

    def run(self, ins, outs, sems, start):
        send_sems, recv_sems, local_sems = sems
        x, y, c = _place()
        mine = 2 * x + y
        for k in range(len(ins)):
            local = pltpu.make_async_copy(ins[k], outs[k].at[mine], local_sems.at[k])
            local.start() if start else local.wait()
            for j, (px, py) in enumerate(_other_chips(x, y)):
                sems_kj = dict(send_sem=send_sems.at[k, j], recv_sem=recv_sems.at[k, j], device_id=(px, py, c),
                               device_id_type=MESH_ID)
                send = pltpu.make_async_remote_copy(src_ref=ins[k], dst_ref=outs[k].at[mine], **sems_kj)
                if start:
                    send.start()
                else:
                    pltpu.make_async_remote_copy(src_ref=ins[k], dst_ref=outs[k].at[2 * px + py], **sems_kj).wait_recv()
                    send.wait_send()


class _GatherHalves:
    def __init__(self, shards):
        nk = len(shards)
        self.arrays = list(shards)
        self.out_shape = [jax.ShapeDtypeStruct((N_CHIPS,) + s.shape, s.dtype) for s in shards]
        self.scratch = [pltpu.SemaphoreType.DMA((nk, 3)) for _ in range(4)] + [pltpu.SemaphoreType.DMA((nk,))]

    def run(self, ins, outs, sems, start):
        far_send, far_recv, near_send, near_recv, local_sems = sems
        x, y, c = _place()
        mine = 2 * x + y
        for phase in ((0,) if start else (1, 2)):
            for k in range(len(ins)):
                half = ins[k].shape[0] // 2
                my_half = pl.ds(pl.multiple_of(c * half, ACT_ROWS), half)
                other_half = pl.ds(pl.multiple_of((1 - c) * half, ACT_ROWS), half)
                if phase != 1:
                    local = pltpu.make_async_copy(ins[k], outs[k].at[mine], local_sems.at[k])
                    local.start() if phase == 0 else local.wait()
                for j, (px, py) in enumerate(_other_chips(x, y)):
                    theirs = 2 * px + py
                    far = dict(send_sem=far_send.at[k, j], recv_sem=far_recv.at[k, j], device_id=(px, py, c),
                               device_id_type=MESH_ID)
                    near = dict(send_sem=near_send.at[k, j], recv_sem=near_recv.at[k, j], device_id=(x, y, 1 - c),
                                device_id_type=MESH_ID)
                    landed = outs[k].at[theirs, my_half]
                    send = lambda: pltpu.make_async_remote_copy(src_ref=ins[k].at[my_half], dst_ref=outs[k].at[mine, my_half], **far)
                    pass_on = lambda: pltpu.make_async_remote_copy(src_ref=landed, dst_ref=landed, **near)
                    if phase == 0:
                        send().start()
                    elif phase == 1:
                        pltpu.make_async_remote_copy(src_ref=ins[k].at[my_half], dst_ref=landed, **far).wait_recv()
                        pass_on().start()
                    else:
                        pltpu.make_async_remote_copy(src_ref=landed, dst_ref=outs[k].at[theirs, other_half], **near).wait_recv()
                        pass_on().wait_send()
                        send().wait_send()


class _Scatter:
    def __init__(self, parts):
        nk = len(parts)
        self.arrays = list(parts)
        self.out_shape = [jax.ShapeDtypeStruct((3,) + p.shape[1:], p.dtype) for p in parts]
        self.scratch = [pltpu.SemaphoreType.DMA((nk, 3)), pltpu.SemaphoreType.DMA((nk, 3))]

    def run(self, ins, outs, sems, start):
        send_sems, recv_sems = sems
        x, y, c = _place()
        for k in range(len(ins)):
            for j, (px, py) in enumerate(_other_chips(x, y)):
                send = pltpu.make_async_remote_copy(
                    src_ref=ins[k].at[2 * px + py], dst_ref=outs[k].at[j], send_sem=send_sems.at[k, j],
                    recv_sem=recv_sems.at[k, j], device_id=(px, py, c), device_id_type=MESH_ID)
                if start:
                    send.start()
                else:
                    send.wait_recv()
                    send.wait_send()


def _exchange(ride, name):
    n_in, n_out = len(ride.arrays), len(ride.out_shape)

    def body(*refs):
        ride.run(refs[:n_in], refs[n_in:n_in + n_out], refs[n_in + n_out:], start=True)
        ride.run(refs[:n_in], refs[n_in:n_in + n_out], refs[n_in + n_out:], start=False)

    return pl.pallas_call(body, name=name, in_specs=[ANY] * n_in, out_specs=[ANY] * n_out, out_shape=ride.out_shape,
                          scratch_shapes=ride.scratch)(*ride.arrays)


def _launch(body, operands, *, name, grid, in_specs, out_specs, out_shape, scratch_shapes=(), ride=None):
    common = dict(name=name, grid=grid, compiler_params=_params(len(grid)))
    if ride is None:
        return pl.pallas_call(body, in_specs=in_specs, out_specs=out_specs, out_shape=out_shape,
                              scratch_shapes=list(scratch_shapes), **common)(*operands)
    n_in, n_out, n_scr = len(operands), len(out_shape), len(scratch_shapes)
    r_in, r_out = len(ride.arrays), len(ride.out_shape)

    def riding(*refs):
        ins, refs = refs[:n_in], refs[n_in:]
        r_ins, refs = refs[:r_in], refs[r_in:]
        outs, refs = refs[:n_out], refs[n_out:]
        r_outs, refs = refs[:r_out], refs[r_out:]
        scr, r_sems = refs[:n_scr], refs[n_scr:]
        step = [pl.program_id(axis) for axis in range(len(grid))]
        first = functools.reduce(jnp.logical_and, [s == 0 for s in step])
        last = functools.reduce(jnp.logical_and, [s == size - 1 for s, size in zip(step, grid)])

        @pl.when(first)
        def _():
            ride.run(r_ins, r_outs, r_sems, start=True)

        body(*ins, *outs, *scr)

        @pl.when(last)
        def _():
            ride.run(r_ins, r_outs, r_sems, start=False)

    return pl.pallas_call(
        riding, in_specs=list(in_specs) + [ANY] * r_in, out_specs=list(out_specs) + [ANY] * r_out,
        out_shape=list(out_shape) + ride.out_shape, scratch_shapes=list(scratch_shapes) + ride.scratch, **common,
    )(*operands, *ride.arrays)


def _sc_fwd(x, first, g, w_in, cw, w_out, *, tm, ride=None):
    seq, d = x.shape
    nt = seq // tm + 1
    nq, _, n = w_in.shape
    width = cw.shape[0]
    past = _past_rows(width)

    def body(x_ref, first_ref, g_ref, win_ref, cw_ref, wout_ref, h1_ref, hh_ref, hh_scr, cbuf):
        i = pl.program_id(0)

        @pl.when(i == 0)
        def _():
            cbuf[pl.ds(0, past), :] = jnp.zeros((past, d), F32)

        h = jnp.where(i == 0, first_ref[...], x_ref[...])
        hn = _rms(h, g_ref[...])[0].astype(MXU_DT)
        for q in range(nq):
            hh_scr[:, q * n:(q + 1) * n] = _dot(hn, win_ref[q])
        hh_ref[...] = hh_scr[...].astype(hh_ref.dtype)
        b = hh_scr[:, 0:d]
        cbuf[pl.ds(past, tm), :] = hh_scr[:, d:2 * d] * hh_scr[:, 2 * d:3 * d]
        last = cbuf[pl.ds(tm, past), :]
        _link_past(cbuf, slice(None), width, tm)
        u = sum(cw_ref[k:k + 1, :] * tap for k, tap in enumerate(_conv_taps(cbuf, slice(None), width, tm)))
        cbuf[pl.ds(0, past), :] = last
        h1_ref[...] = h + _dot((b * u).astype(MXU_DT), wout_ref[...])

    return _launch(
        body, [x, first, g, w_in, cw, w_out], name="sc_fwd", grid=(nt,),
        in_specs=[pl.BlockSpec((tm, d), lambda i: (jnp.maximum(i - 1, 0), 0)), _const((tm, d)), _const((1, d)),
                  _const(w_in.shape), _const(cw.shape), _const(w_out.shape)],
        out_specs=[pl.BlockSpec((tm, d), lambda i: (i, 0)), pl.BlockSpec((tm, nq * n), lambda i: (i, 0))],
        out_shape=[jax.ShapeDtypeStruct((nt * tm, d), F32), jax.ShapeDtypeStruct((nt * tm, nq * n), ACT_DT)],
        scratch_shapes=[pltpu.VMEM((tm, nq * n), F32), pltpu.VMEM((past + tm, d), F32)],
        ride=ride,
    )


def _sc_bwd(dh, hh, x, first, g, w_in, cw, w_out, *, tm, ride=None):
    t_len, d = dh.shape
    nt = t_len // tm
    nq, _, n = w_in.shape
    width = cw.shape[0]
    past = _past_rows(width)
    halo_rows, halo_index = _halo_block(past, tm, nt)

    def body(dh_ref, hh_ref, hhp_ref, x_ref, first_ref, g_ref, win_ref, cw_ref, wout_ref,
             dx_ref, dhh_ref, z_ref, hn_ref, dcw_ref, dg_ref, dfirst_ref, cbuf, dbuf):
        i = pl.program_id(0)
        r = nt - 1 - i

        @pl.when(i == 0)
        def _():
            dbuf[pl.ds(tm, past), :] = jnp.zeros((past, d), F32)
            dcw_ref[...] = jnp.zeros_like(dcw_ref)
            dg_ref[...] = jnp.zeros_like(dg_ref)

        dh_out = dh_ref[...]
        b = hh_ref[:, 0:d].astype(F32)
        c = hh_ref[:, d:2 * d].astype(F32)
        v = hh_ref[:, 2 * d:3 * d].astype(F32)
        prev = hhp_ref[...].astype(F32)[halo_rows - past:, :]
        cbuf[pl.ds(0, past), :] = jnp.where(r > 0, prev[:, d:2 * d] * prev[:, 2 * d:3 * d], 0.0)
        cbuf[pl.ds(past, tm), :] = c * v
        _link_past(cbuf, slice(None), width, tm)
        taps = _conv_taps(cbuf, slice(None), width, tm)
        u = sum(cw_ref[k:k + 1, :] * taps[k] for k in range(width))
        z_ref[...] = (b * u).astype(z_ref.dtype)
        dz = _dot_nt(dh_out.astype(MXU_DT), wout_ref[...])
        d_b = (dz * u).astype(dhh_ref.dtype)
        dhh_ref[:, 0:d] = d_b
        parts = [_dot_nt(d_b[:, 0:n], win_ref[0])]
        du = dz * b
        for k in range(width):
            dcw_ref[k:k + 1, :] += jnp.sum(taps[k] * du, axis=0, keepdims=True)
        dbuf[pl.ds(0, tm), :] = du
        _link_future(dbuf, slice(None), width, tm)
        dcv = _conv_back(dbuf, cw_ref, slice(None), width, tm)
        dbuf[pl.ds(tm, past), :] = dbuf[pl.ds(0, past), :]
        d_c, d_v = (dcv * v).astype(dhh_ref.dtype), (dcv * c).astype(dhh_ref.dtype)
        dhh_ref[:, d:2 * d] = d_c
        dhh_ref[:, 2 * d:3 * d] = d_v
        rest = jnp.concatenate([d_b[:, n:], d_c, d_v], axis=1)
        parts += [_dot_nt(rest[:, (q - 1) * n:q * n], win_ref[q]) for q in range(1, nq)]
        dhn = functools.reduce(lambda a, b: a + b, parts)
        h_in = jnp.where(r == 0, first_ref[...], x_ref[...])
        dh_in = _norm_bwd_tile(dhn, h_in, dh_out, g_ref[...], _valid_rows(r, tm), hn_ref, dg_ref)

        @pl.when(r == 0)
        def _():
            dfirst_ref[...] = dh_in

        @pl.when(r > 0)
        def _():
            dx_ref[...] = dh_in

    rev = lambda i: (nt - 1 - i, 0)
    rev_x = lambda i: (jnp.maximum(nt - 2 - i, 0), 0)
    return _launch(
        body, [dh, hh, hh, x, first, g, w_in, cw, w_out], name="sc_bwd", grid=(nt,),
        in_specs=[pl.BlockSpec((tm, d), rev), pl.BlockSpec((tm, 3 * d), rev), pl.BlockSpec((halo_rows, 3 * d), halo_index),
                  pl.BlockSpec((tm, d), rev_x), _const((tm, d)), _const((1, d)), _const(w_in.shape), _const(cw.shape),
                  _const(w_out.shape)],
        out_specs=[pl.BlockSpec((tm, d), rev_x), pl.BlockSpec((tm, 3 * d), rev), pl.BlockSpec((tm, d), rev),
                   pl.BlockSpec((tm, d), rev), _const((F32_ROWS, d)), _const((F32_ROWS, d)), _const((tm, d))],
        out_shape=[jax.ShapeDtypeStruct((t_len - tm, d), F32), jax.ShapeDtypeStruct((t_len, 3 * d), ACT_DT),
                   jax.ShapeDtypeStruct((t_len, d), ACT_DT), jax.ShapeDtypeStruct((t_len, d), ACT_DT),
                   jax.ShapeDtypeStruct((F32_ROWS, d), F32), jax.ShapeDtypeStruct((F32_ROWS, d), F32),
                   jax.ShapeDtypeStruct((tm, d), F32)],
        scratch_shapes=[pltpu.VMEM((past + tm, d), F32), pltpu.VMEM((tm + past, d), F32)],
        ride=ride,
    )


def _ffn_fwd(h, g, w_up, cw, w_down, *, tm, ride=None, loss=None):
    t_len, d = h.shape
    nt = t_len // tm
    nq, _, n = w_up.shape
    width = cw.shape[0]
    past = _past_rows(width)

    def body(h_ref, g_ref, wup_ref, cw_ref, wdn_ref, *rest):
        if loss is None:
            out_ref, hu_ref, hc_ref, act_ref, ubuf, tail = rest
        else:
            t_ref, gf_ref, out_ref, hu_ref, hc_ref, act_ref, sq_ref, dgf_ref, ubuf, tail = rest
        i = pl.program_id(0)

        @pl.when(i == 0)
        def _():
            tail[...] = jnp.zeros_like(tail)

        h_in = h_ref[...]
        hn = _rms(h_in, g_ref[...])[0].astype(MXU_DT)
        ubuf[pl.ds(0, past), :] = tail[...]
        for q in range(nq):
            ubuf[pl.ds(past, tm), q * n:(q + 1) * n] = _dot(hn, wup_ref[q])
        hu_ref[...] = ubuf[pl.ds(past, tm), :].astype(hu_ref.dtype)
        tail[...] = ubuf[pl.ds(tm, past), :]
        _link_past(ubuf, slice(None), width, tm)
        acc = h_in
        for j in range(nq // 2):
            gcol, vcol = slice(j * n, (j + 1) * n), slice((nq // 2 + j) * n, (nq // 2 + j + 1) * n)
            conv = lambda cols: sum(cw_ref[k:k + 1, cols] * tap for k, tap in enumerate(_conv_taps(ubuf, cols, width, tm)))
            gj, vj = conv(gcol), conv(vcol)
            hc_ref[:, gcol] = gj.astype(hc_ref.dtype)
            hc_ref[:, vcol] = vj.astype(hc_ref.dtype)
            act = (gj * _sigmoid(gj) * vj).astype(MXU_DT)
            act_ref[:, gcol] = act.astype(act_ref.dtype)
            acc = acc + _dot(act, wdn_ref[j * n:(j + 1) * n, :])
        if loss is None:
            out_ref[...] = acc
            return

        @pl.when(i == 0)
        def _():
            sq_ref[...] = jnp.zeros_like(sq_ref)
            dgf_ref[...] = jnp.zeros_like(dgf_ref)
            out_ref[...] = jnp.zeros_like(out_ref)

        @pl.when(i > 0)
        def _():
            gain = gf_ref[...]
            out, xhat, rstd = _rms(acc, gain)
            err = out - t_ref[...]
            sq_ref[0:1, :] += jnp.sum(err * err, axis=0, keepdims=True)
            dout = err * (1.0 / d)
            dgf_ref[0:1, :] += jnp.sum(dout * xhat, axis=0, keepdims=True)
            out_ref[...] = _rms_bwd(dout, xhat, rstd, gain)

    row = lambda i: (i, 0)
    stat = jax.ShapeDtypeStruct((F32_ROWS, d), F32)
    return _launch(
        body, [h, g, w_up, cw, w_down] + list(loss or ()), name="ffn_fwd", grid=(nt,),
        in_specs=[pl.BlockSpec((tm, d), row), _const((1, d)), _const(w_up.shape), _const(cw.shape), _const(w_down.shape)]
        + ([pl.BlockSpec((tm, d), lambda i: (jnp.maximum(i - 1, 0), 0)), _const((1, d))] if loss else []),
        out_specs=[pl.BlockSpec((tm, d), row), pl.BlockSpec((tm, nq * n), row), pl.BlockSpec((tm, nq * n), row),
                   pl.BlockSpec((tm, nq // 2 * n), row)] + ([_const(stat.shape)] * 2 if loss else []),
        out_shape=[jax.ShapeDtypeStruct((t_len, d), F32), jax.ShapeDtypeStruct((t_len, nq * n), ACT_DT),
                   jax.ShapeDtypeStruct((t_len, nq * n), ACT_DT), jax.ShapeDtypeStruct((t_len, nq // 2 * n), ACT_DT)]
        + ([stat, stat] if loss else []),
        scratch_shapes=[pltpu.VMEM((past + tm, nq * n), F32), pltpu.VMEM((past, nq * n), F32)],
        ride=ride,
    )


def _norm_bwd_tile(dhn, h_in, dh, gain, valid, hn_ref, dg_ref):
    hn, xhat, rstd = _rms(h_in, gain)
    hn_ref[...] = hn.astype(hn_ref.dtype)
    dg_ref[0:1, :] += jnp.sum(dhn * xhat, axis=0, keepdims=True)
    return jnp.where(valid, dh + _rms_bwd(dhn, xhat, rstd, gain), 0.0)


def _ffn_bwd(dh, hu, hc, h, g, w_up, cw, w_down, *, tm, ride=None):
    t_len, d = dh.shape
    nt = t_len // tm
    ff = hu.shape[1]
    n = ff // 4
    width = cw.shape[0]
    past = _past_rows(width)
    halo_rows, halo_index = _halo_block(past, tm, nt)

    def body(dh_ref, hu_ref, hup_ref, hc_ref, h_ref, g_ref, wup_ref, cw_ref, wdn_ref,
             dhin_ref, dhu_ref, hn_ref, dcw_ref, dg_ref, ubuf, dbuf, head):
        i = pl.program_id(0)
        r = nt - 1 - i

        @pl.when(i == 0)
        def _():
            head[...] = jnp.zeros_like(head)
            dcw_ref[...] = jnp.zeros_like(dcw_ref)
            dg_ref[...] = jnp.zeros_like(dg_ref)

        dh_out = dh_ref[...]
        dhb = dh_out.astype(MXU_DT)
        dhn_parts = []
        d_act = [_dot_nt(dhb, wdn_ref[j * n:(j + 1) * n, :]) for j in range(2)]
        for j in range(2):
            mine = slice(0, n), slice(n, 2 * n)
            full = slice(j * n, (j + 1) * n), slice((2 + j) * n, (3 + j) * n)
            for here, there in zip(mine, full):
                prev = hup_ref[:, there].astype(F32)[halo_rows - past:, :]
                ubuf[pl.ds(0, past), here] = jnp.where(r > 0, prev, 0.0)
                ubuf[pl.ds(past, tm), here] = hu_ref[:, there].astype(F32)
                dbuf[pl.ds(tm, past), here] = head[:, there]
            _link_past(ubuf, slice(None), width, tm)
            gj, vj = hc_ref[:, full[0]].astype(F32), hc_ref[:, full[1]].astype(F32)
            sg = _sigmoid(gj)
            s = gj * sg
            da = d_act[j]
            dbuf[pl.ds(0, tm), mine[1]] = da * s
            dbuf[pl.ds(0, tm), mine[0]] = da * vj * (sg * (1.0 + gj * (1.0 - sg)))
            for here, there in zip(mine, full):
                head[:, there] = dbuf[pl.ds(0, past), here]
            _link_future(dbuf, slice(None), width, tm)
            for here, there in zip(mine, full):
                dy = dbuf[pl.ds(0, tm), here]
                for k, tap in enumerate(_conv_taps(ubuf, here, width, tm)):
                    dcw_ref[k:k + 1, there] += jnp.sum(tap * dy, axis=0, keepdims=True)
                dhu = sum(cw_ref[k:k + 1, there] * dbuf[pl.ds(F32_ROWS * (width - 1 - k), tm), here]
                          for k in range(width)).astype(dhu_ref.dtype)
                dhu_ref[:, there] = dhu
                dhn_parts.append(_dot_nt(dhu, wup_ref[there.start // n]))
        dhn = (dhn_parts[0] + dhn_parts[1]) + (dhn_parts[2] + dhn_parts[3])
        dhin_ref[...] = _norm_bwd_tile(dhn, h_ref[...], dh_out, g_ref[...], _valid_rows(r, tm), hn_ref, dg_ref)

    rev = lambda i: (nt - 1 - i, 0)
    return _launch(
        body, [dh, hu, hu, hc, h, g, w_up, cw, w_down], name="ffn_bwd", grid=(nt,),
        in_specs=[pl.BlockSpec((tm, d), rev), pl.BlockSpec((tm, ff), rev), pl.BlockSpec((halo_rows, ff), halo_index),
                  pl.BlockSpec((tm, ff), rev), pl.BlockSpec((tm, d), rev), _const((1, d)), _const(w_up.shape), _const(cw.shape), _const(w_down.shape)],
        out_specs=[pl.BlockSpec((tm, d), rev), pl.BlockSpec((tm, ff), rev),
                   pl.BlockSpec((tm, d), rev), _const((F32_ROWS, ff)), _const((F32_ROWS, d))],
        out_shape=[jax.ShapeDtypeStruct((t_len, d), F32),
                   jax.ShapeDtypeStruct((t_len, ff), ACT_DT), jax.ShapeDtypeStruct((t_len, d), ACT_DT),
                   jax.ShapeDtypeStruct((F32_ROWS, ff), F32), jax.ShapeDtypeStruct((F32_ROWS, d), F32)],
        scratch_shapes=[pltpu.VMEM((past + tm, 2 * n), F32), pltpu.VMEM((tm + past, 2 * n), F32), pltpu.VMEM((past, ff), F32)],
        ride=ride,
    )


V_CONV_B, V_B_A, V_B_X, V_LAMBDA = 0, 1, 2, 3
G_CONV_W, G_CONV_B, G_B_A, G_B_X, G_LAMBDA = 0, 4, 5, 6, 7


def _scan(a_ref, b_ref, edge, tm, reverse):
    nj = tm // F32_ROWS
    order = range(nj - 1, -1, -1) if reverse else range(nj)
    slab = lambda ref, j: ref[pl.ds(F32_ROWS * j, F32_ROWS), :]
    a_run = b_run = None
    for j in order:
        a_j, b_j = slab(a_ref, j), slab(b_ref, j)
        if a_run is not None:
            b_j = b_j + a_j * b_run
            a_j = a_j * a_run
            b_ref[pl.ds(F32_ROWS * j, F32_ROWS), :] = b_j
            a_ref[pl.ds(F32_ROWS * j, F32_ROWS), :] = a_j
        a_run, b_run = a_j, b_j
    sub = _sublane()
    shift = 1
    while shift < F32_ROWS:
        amount = F32_ROWS - shift if reverse else shift
        keep = (sub < F32_ROWS - shift) if reverse else (sub >= shift)
        b_run = jnp.where(keep, b_run + a_run * pltpu.roll(b_run, amount, 0), b_run)
        a_run = jnp.where(keep, a_run * pltpu.roll(a_run, amount, 0), a_run)
        shift *= 2
    outer = edge[0:1, :] if reverse else edge[F32_ROWS - 1:F32_ROWS, :]
    ends = b_run + a_run * outer
    if reverse:
        carry = jnp.where(sub == F32_ROWS - 1, outer, pltpu.roll(ends, F32_ROWS - 1, 0))
    else:
        carry = jnp.where(sub == 0, outer, pltpu.roll(ends, 1, 0))
    for j in range(nj):
        b_ref[pl.ds(F32_ROWS * j, F32_ROWS), :] = slab(b_ref, j) + slab(a_ref, j) * carry
    return slab(b_ref, 0 if reverse else nj - 1)


def _rg_gates(u, vec_ref, wa_ref, wx_ref, pre_scr, nb, bd):
    ub = u.astype(MXU_DT)
    for k in range(nb):
        blk = slice(k * bd, (k + 1) * bd)
        pre_scr[0, :, blk] = _dot(ub[:, blk], wa_ref[k])
        pre_scr[1, :, blk] = _dot(ub[:, blk], wx_ref[k])
    r_gate = _sigmoid(pre_scr[0] + vec_ref[V_B_A:V_B_A + 1, :])
    i_gate = _sigmoid(pre_scr[1] + vec_ref[V_B_X:V_B_X + 1, :])
    return r_gate, i_gate


def _rg_decay(r_gate, vec_ref):
    sp = _softplus(-vec_ref[V_LAMBDA:V_LAMBDA + 1, :])
    log_a = -RG_C * r_gate * sp
    a = jnp.exp(log_a)
    one_minus_a2 = jnp.maximum(-_expm1_neg(2.0 * log_a, a * a), 1e-30)
    inv_mult = lax.rsqrt(one_minus_a2)
    return a, one_minus_a2 * inv_mult, inv_mult, sp


def _rg_fwd(h, g, w_in, cw, vec, wa, wx, w_out, *, tm):
    t_len, d = h.shape
    nt = t_len // tm
    nq, _, n = w_in.shape
    dr = 2 * n
    width = cw.shape[0]
    past = _past_rows(width)
    nb, bd, _ = wa.shape

    def body(h_ref, g_ref, win_ref, cw_ref, vec_ref, wa_ref, wx_ref, wout_ref, out_ref, hh_ref, hs_ref, gates_ref,
             gbuf, rbuf, pre_scr, tail, edge):
        i = pl.program_id(0)

        @pl.when(i == 0)
        def _():
            tail[...] = jnp.zeros_like(tail)
            edge[...] = jnp.zeros_like(edge)

        h_in = h_ref[...]
        hn = _rms(h_in, g_ref[...])[0].astype(MXU_DT)
        rbuf[pl.ds(0, past), :] = tail[...]
        for q in range(2):
            gbuf[:, q * n:(q + 1) * n] = _dot(hn, win_ref[q])
            rbuf[pl.ds(past, tm), q * n:(q + 1) * n] = _dot(hn, win_ref[2 + q])
        hh_ref[:, 0:dr] = gbuf[...].astype(hh_ref.dtype)
        hh_ref[:, dr:2 * dr] = rbuf[pl.ds(past, tm), :].astype(hh_ref.dtype)
        tail[...] = rbuf[pl.ds(tm, past), :]
        _link_past(rbuf, slice(None), width, tm)
        taps = _conv_taps(rbuf, slice(None), width, tm)
        u = sum(cw_ref[k:k + 1, :] * taps[k] for k in range(width)) + vec_ref[V_CONV_B:V_CONV_B + 1, :]
        r_gate, i_gate = _rg_gates(u, vec_ref, wa_ref, wx_ref, pre_scr, nb, bd)
        for k, kept in enumerate((u, r_gate, i_gate)):
            gates_ref[:, k * dr:(k + 1) * dr] = kept.astype(gates_ref.dtype)
        a, mult, _, _ = _rg_decay(r_gate, vec_ref)
        hs_ref[:, dr:2 * dr] = a
        hs_ref[:, 2 * dr:3 * dr] = mult
        pre_scr[0] = a
        pre_scr[1] = jnp.where(_valid_rows(i, tm), mult * (i_gate * u), 0.0)
        edge[...] = _scan(pre_scr.at[0], pre_scr.at[1], edge[...], tm, reverse=False)
        hs = pre_scr[1]
        hs_ref[:, 0:dr] = hs
        gate, th = _gelu(gbuf[...])
        gates_ref[:, 3 * dr:4 * dr] = th.astype(gates_ref.dtype)
        y = hs * gate
        out_ref[...] = h_in + _dot(y.astype(MXU_DT), wout_ref[...])

    row = lambda i: (i, 0)
    return pl.pallas_call(
        body, name="rg_fwd", grid=(nt,),
        in_specs=[pl.BlockSpec((tm, d), row), _const((1, d)), _const(w_in.shape), _const(cw.shape), _const(vec.shape),
                  _const(wa.shape), _const(wx.shape), _const(w_out.shape)],
        out_specs=[pl.BlockSpec((tm, d), row), pl.BlockSpec((tm, 2 * dr), row), pl.BlockSpec((tm, 3 * dr), row),
                   pl.BlockSpec((tm, 4 * dr), row)],
        out_shape=[jax.ShapeDtypeStruct((t_len, d), F32), jax.ShapeDtypeStruct((t_len, 2 * dr), ACT_DT),
                   jax.ShapeDtypeStruct((t_len, 3 * dr), F32), jax.ShapeDtypeStruct((t_len, 4 * dr), ACT_DT)],
        scratch_shapes=[pltpu.VMEM((tm, dr), F32), pltpu.VMEM((past + tm, dr), F32), pltpu.VMEM((2, tm, dr), F32),
                        pltpu.VMEM((past, dr), F32), pltpu.VMEM((F32_ROWS, dr), F32)],
        compiler_params=_params(),
    )(h, g, w_in, cw, vec, wa, wx, w_out)


def _rg_bwd(dh, hh, hs, gates, h, g, w_in, cw, vec, wa, wx, w_out, *, tm, ride=None):
    t_len, d = dh.shape
    nt = t_len // tm
    dr = hs.shape[1] // 3
    n = w_in.shape[2]
    width = cw.shape[0]
    nb, bd, _ = wa.shape
    past = _past_rows(width)
    halo_rows, halo_index = _halo_block(past, tm, nt)
    one = F32_ROWS

    def body(dh_ref, hh_ref, hhp_ref, hs_ref, hsp_ref, gates_ref, h_ref, g_ref, win_ref, cw_ref, vec_ref, wa_ref, wx_ref, wout_ref,
             dhin_ref, dhh_ref, y_ref, hn_ref, dvec_ref, dwa_ref, dwx_ref, dg_ref, rbuf, dbuf, pre_scr, hbuf, abuf, edge):
        i = pl.program_id(0)
        r = nt - 1 - i

        @pl.when(i == 0)
        def _():
            dbuf[pl.ds(tm, past), :] = jnp.zeros((past, dr), F32)
            abuf[pl.ds(tm, one), :] = jnp.zeros((one, dr), F32)
            edge[...] = jnp.zeros_like(edge)
            dvec_ref[...] = jnp.zeros_like(dvec_ref)
            dwa_ref[...] = jnp.zeros_like(dwa_ref)
            dwx_ref[...] = jnp.zeros_like(dwx_ref)
            dg_ref[...] = jnp.zeros_like(dg_ref)

        dh_out = dh_ref[...]
        gb = hh_ref[:, 0:dr].astype(F32)
        prev = hhp_ref[...].astype(F32)[halo_rows - past:, dr:2 * dr]
        rbuf[pl.ds(0, past), :] = jnp.where(r > 0, prev, 0.0)
        rbuf[pl.ds(past, tm), :] = hh_ref[:, dr:2 * dr].astype(F32)
        _link_past(rbuf, slice(None), width, tm)
        taps = _conv_taps(rbuf, slice(None), width, tm)
        ub = gates_ref[:, 0:dr].astype(MXU_DT)
        u, r_gate, i_gate = (gates_ref[:, k * dr:(k + 1) * dr].astype(F32) for k in range(3))
        hs_t, a, mult = (hs_ref[:, k * dr:(k + 1) * dr] for k in range(3))
        inv_mult = 1.0 / mult
        sp = _softplus(-vec_ref[V_LAMBDA:V_LAMBDA + 1, :])
        hbuf[pl.ds(0, one), :] = jnp.where(r > 0, hsp_ref[...], 0.0)
        hbuf[pl.ds(one, tm), :] = hs_t
        _link_past(hbuf, slice(None), 2, tm)
        h_prev = hbuf[pl.ds(0, tm), :]
        th = gates_ref[:, 3 * dr:4 * dr].astype(F32)
        gate = 0.5 * gb * (1.0 + th)
        y_ref[...] = (hs_t * gate).astype(y_ref.dtype)
        dy = _dot_nt(dh_out.astype(MXU_DT), wout_ref[...])
        d_gb = (dy * hs_t * _gelu_grad(gb, th)).astype(dhh_ref.dtype)
        dhh_ref[:, 0:dr] = d_gb
        dhn = sum(_dot_nt(d_gb[:, q * n:(q + 1) * n], win_ref[q]) for q in range(2))
        abuf[pl.ds(0, tm), :] = a
        _link_future(abuf, slice(None), 2, tm)
        pre_scr[0] = abuf[pl.ds(one, tm), :]
        pre_scr[1] = dy * gate
        edge[...] = _scan(pre_scr.at[0], pre_scr.at[1], edge[...], tm, reverse=True)
        abuf[pl.ds(tm, one), :] = abuf[pl.ds(0, one), :]
        d_hs = pre_scr[1]
        d_b = jnp.where(_valid_rows(r, tm), d_hs, 0.0)
        d_iu = d_b * mult
        d_log_a = d_hs * h_prev * a - d_b * (i_gate * u) * (a * a) * inv_mult
        dvec_ref[G_LAMBDA:G_LAMBDA + 1, :] += jnp.sum(d_log_a * r_gate, axis=0, keepdims=True) * (-RG_C)
        d_pre_r = d_log_a * (-RG_C * sp) * r_gate * (1.0 - r_gate)
        d_pre_i = d_iu * u * i_gate * (1.0 - i_gate)
        dvec_ref[G_B_A:G_B_A + 1, :] += jnp.sum(d_pre_r, axis=0, keepdims=True)
        dvec_ref[G_B_X:G_B_X + 1, :] += jnp.sum(d_pre_i, axis=0, keepdims=True)
        dbuf[pl.ds(0, tm), :] = d_iu * i_gate
        d_pre_r = d_pre_r.astype(MXU_DT)
        d_pre_i = d_pre_i.astype(MXU_DT)
        for k in range(nb):
            blk = slice(k * bd, (k + 1) * bd)
            dwa_ref[k] += _dot_tn(ub[:, blk], d_pre_r[:, blk])
            dwx_ref[k] += _dot_tn(ub[:, blk], d_pre_i[:, blk])
            dbuf[pl.ds(0, tm), blk] += _dot_nt(d_pre_r[:, blk], wa_ref[k]) + _dot_nt(d_pre_i[:, blk], wx_ref[k])
        du = dbuf[pl.ds(0, tm), :]
        dvec_ref[G_CONV_B:G_CONV_B + 1, :] += jnp.sum(du, axis=0, keepdims=True)
        for k in range(width):
            dvec_ref[G_CONV_W + k:G_CONV_W + k + 1, :] += jnp.sum(taps[k] * du, axis=0, keepdims=True)
        _link_future(dbuf, slice(None), width, tm)
        d_rb = _conv_back(dbuf, cw_ref, slice(None), width, tm)
        dbuf[pl.ds(tm, past), :] = dbuf[pl.ds(0, past), :]
        d_rb = d_rb.astype(dhh_ref.dtype)
        dhh_ref[:, dr:2 * dr] = d_rb
        dhn = dhn + sum(_dot_nt(d_rb[:, q * n:(q + 1) * n], win_ref[2 + q]) for q in range(2))
        dhin_ref[...] = _norm_bwd_tile(dhn, h_ref[...], dh_out, g_ref[...], _valid_rows(r, tm), hn_ref, dg_ref)

        @pl.when(i == nt - 1)
        def _():
            lam = vec_ref[V_LAMBDA:V_LAMBDA + 1, :]
            dvec_ref[G_LAMBDA:G_LAMBDA + 1, :] = dvec_ref[G_LAMBDA:G_LAMBDA + 1, :] * (-_sigmoid(-lam))

    rev = lambda i: (nt - 1 - i, 0)
    return _launch(
        body, [dh, hh, hh, hs, hs, gates, h, g, w_in, cw, vec, wa, wx, w_out], name="rg_bwd", grid=(nt,),
        in_specs=[pl.BlockSpec((tm, d), rev), pl.BlockSpec((tm, 2 * dr), rev), pl.BlockSpec((halo_rows, 2 * dr), halo_index),
                  pl.BlockSpec((tm, 3 * dr), rev),
                  pl.BlockSpec((one, dr), lambda i: (jnp.maximum((nt - 1 - i) * (tm // one) - 1, 0), 0)),
                  pl.BlockSpec((tm, 4 * dr), rev), pl.BlockSpec((tm, d), rev), _const((1, d)), _const(w_in.shape),
                  _const(cw.shape), _const(vec.shape), _const(wa.shape), _const(wx.shape), _const(w_out.shape)],
        out_specs=[pl.BlockSpec((tm, d), rev), pl.BlockSpec((tm, 2 * dr), rev), pl.BlockSpec((tm, dr), rev),
                   pl.BlockSpec((tm, d), rev), _const((F32_ROWS, dr)), _const(wa.shape), _const(wx.shape),
                   _const((F32_ROWS, d))],
        out_shape=[jax.ShapeDtypeStruct((t_len, d), F32), jax.ShapeDtypeStruct((t_len, 2 * dr), ACT_DT),
                   jax.ShapeDtypeStruct((t_len, dr), ACT_DT), jax.ShapeDtypeStruct((t_len, d), ACT_DT),
                   jax.ShapeDtypeStruct((F32_ROWS, dr), F32), jax.ShapeDtypeStruct(wa.shape, F32),
                   jax.ShapeDtypeStruct(wx.shape, F32), jax.ShapeDtypeStruct((F32_ROWS, d), F32)],
        scratch_shapes=[pltpu.VMEM((past + tm, dr), F32), pltpu.VMEM((tm + past, dr), F32), pltpu.VMEM((2, tm, dr), F32),
                        pltpu.VMEM((one + tm, dr), F32), pltpu.VMEM((tm + one, dr), F32), pltpu.VMEM((F32_ROWS, dr), F32)],
        ride=ride,
    )


def _weight_grad(a, b, nb, *, rows, ride=None):
    t_len, k_dim = a.shape
    n = b.shape[1] // nb
    nt = t_len // rows

    def body(a_ref, b_ref, out_ref, wire_ref):
        @pl.when(pl.program_id(1) == 0)
        def _():
            out_ref[...] = jnp.zeros_like(out_ref)

        out_ref[0] += _dot_tn(a_ref[...].astype(MXU_DT), b_ref[...].astype(MXU_DT))

        @pl.when(pl.program_id(1) == nt - 1)
        def _():
            wire_ref[...] = out_ref[...].astype(wire_ref.dtype)

    block = pl.BlockSpec((1, k_dim, n), lambda j, i: (j, 0, 0))
    return _launch(
        body, [a, b], name="weight_grad", grid=(nb, nt),
        in_specs=[pl.BlockSpec((rows, k_dim), lambda j, i: (i, 0)), pl.BlockSpec((rows, n), lambda j, i: (i, j))],
        out_specs=[block, block],
        out_shape=[jax.ShapeDtypeStruct((nb, k_dim, n), F32), jax.ShapeDtypeStruct((nb, k_dim, n), WIRE_DT)],
        ride=ride,
    )


def _adamw(w, m, v, parts, *, rows, layer=0, into=None):
    n_layers, n_rows, n_cols = w.shape
    nt = n_rows // rows
    n_parts = len(parts)

    def body(w_ref, m_ref, v_ref, *rest):
        part_refs, (g_ref, d_ref, nm_ref, nv_ref) = rest[:n_parts], rest[-4:]
        w_ref, m_ref, v_ref, g_ref, d_ref, nm_ref, nv_ref = (r.at[0] for r in (w_ref, m_ref, v_ref, g_ref, d_ref, nm_ref, nv_ref))
        grad = part_refs[0][...].astype(F32)
        for p in part_refs[1:]:
            grad = grad + p[...].astype(F32)
        new_m = ADAM_B1 * m_ref[...] + (1.0 - ADAM_B1) * grad
        new_v = ADAM_B2 * v_ref[...] + (1.0 - ADAM_B2) * (grad * grad)
        m_hat = new_m / (1.0 - ADAM_B1 ** ADAM_STEP)
        v_hat = new_v / (1.0 - ADAM_B2 ** ADAM_STEP)
        g_ref[...] = grad
        d_ref[...] = -ADAM_LR * (m_hat / (jnp.sqrt(v_hat) + ADAM_EPS) + ADAM_WD * w_ref[...])
        nm_ref[...] = new_m
        nv_ref[...] = new_v

    spec = pl.BlockSpec((rows, n_cols), lambda i: (i, 0))
    layer_spec = pl.BlockSpec((1, rows, n_cols), lambda i: (layer, i, 0))
    into = list(into or [])
    return pl.pallas_call(
        body, name="adamw", grid=(nt,),
        in_specs=[layer_spec] * 3 + [spec] * n_parts + [ANY] * len(into), out_specs=[layer_spec] * 4,
        out_shape=[jax.ShapeDtypeStruct(w.shape, F32)] * 4,
        input_output_aliases={3 + n_parts + k: k for k in range(len(into))},
        compiler_params=_params(),
    )(w, m, v, *parts, *into)


def _to_wire(w, layer, *, ride=None):
    _, n_rows, n_cols = w.shape
    rows = _divisor_rows(n_rows)

    def body(w_ref, out_ref):
        out_ref[...] = w_ref[0].astype(out_ref.dtype)

    return _launch(
        body, [w], name="to_wire", grid=(n_rows // rows,),
        in_specs=[pl.BlockSpec((1, rows, n_cols), lambda i: (layer, i, 0))],
        out_specs=[pl.BlockSpec((rows, n_cols), lambda i: (i, 0))],
        out_shape=[jax.ShapeDtypeStruct((n_rows, n_cols), WIRE_DT)], ride=ride)


def _sum_stack(stack, *, rows):
    n_stack, n_rows, n_cols = stack.shape

    def body(stack_ref, out_ref):
        acc = stack_ref[0]
        for j in range(1, n_stack):
            acc = acc + stack_ref[j]
        out_ref[...] = acc

    return pl.pallas_call(
        body, name="sum_stack", grid=(n_rows // rows,),
        in_specs=[pl.BlockSpec((n_stack, rows, n_cols), lambda i: (0, i, 0))],
        out_specs=pl.BlockSpec((rows, n_cols), lambda i: (i, 0)),
        out_shape=jax.ShapeDtypeStruct((n_rows, n_cols), F32),
        compiler_params=_params(),
    )(stack)


def _sum_parts(own, recv, *, rows):
    n_rows, n_cols = own.shape
    n_recv = recv.shape[0]

    def body(own_ref, recv_ref, out_ref):
        acc = own_ref[...].astype(F32)
        for j in range(n_recv):
            acc = acc + recv_ref[j].astype(F32)
        out_ref[...] = acc

    return pl.pallas_call(
        body, name="sum_parts", grid=(n_rows // rows,),
        in_specs=[pl.BlockSpec((rows, n_cols), lambda i: (i, 0)), pl.BlockSpec((n_recv, rows, n_cols), lambda i: (0, i, 0))],
        out_specs=pl.BlockSpec((rows, n_cols), lambda i: (i, 0)),
        out_shape=jax.ShapeDtypeStruct(own.shape, F32),
        compiler_params=_params(),
    )(own, recv)


class _Swap:
    def __init__(self, arrays):
        nk = len(arrays)
        self.arrays = list(arrays)
        self.out_shape = [jax.ShapeDtypeStruct(a.shape, a.dtype) for a in arrays]
        self.scratch = [pltpu.SemaphoreType.DMA((nk,)), pltpu.SemaphoreType.DMA((nk,))]

    def run(self, ins, outs, sems, start):
        send_sems, recv_sems = sems
        x, y, c = _place()
        for k in range(len(ins)):
            send = pltpu.make_async_remote_copy(src_ref=ins[k], dst_ref=outs[k], send_sem=send_sems.at[k],
                                                recv_sem=recv_sems.at[k], device_id=(x, y, 1 - c), device_id_type=MESH_ID)
            if start:
                send.start()
            else:
                send.wait_recv()
                send.wait_send()


class _AllDevices:
    def __init__(self, arrays):
        nk = len(arrays)
        self.arrays = list(arrays)
        self.out_shape = [jax.ShapeDtypeStruct((8,) + a.shape, a.dtype) for a in arrays]
        self.scratch = [pltpu.SemaphoreType.DMA((nk, 7)), pltpu.SemaphoreType.DMA((nk, 7)), pltpu.SemaphoreType.DMA((nk,))]

    def run(self, ins, outs, sems, start):
        send_sems, recv_sems, local_sems = sems
        x, y, c = _place()
        mine = 4 * x + 2 * y + c
        for k in range(len(ins)):
            local = pltpu.make_async_copy(ins[k], outs[k].at[mine], local_sems.at[k])
            local.start() if start else local.wait()
            for flip in range(1, 8):
                px, py, pc = x ^ (flip >> 2), y ^ ((flip >> 1) & 1), c ^ (flip & 1)
                sems_f = dict(send_sem=send_sems.at[k, flip - 1], recv_sem=recv_sems.at[k, flip - 1],
                              device_id=(px, py, pc), device_id_type=MESH_ID)
                send = pltpu.make_async_remote_copy(src_ref=ins[k], dst_ref=outs[k].at[mine], **sems_f)
                if start:
                    send.start()
                else:
                    pltpu.make_async_remote_copy(src_ref=ins[k], dst_ref=outs[k].at[4 * px + 2 * py + pc], **sems_f).wait_recv()
                    send.wait_send()


class _Both:
    def __init__(self, first, second):
        self.rides = (first, second)
        self.arrays = first.arrays + second.arrays
        self.out_shape = first.out_shape + second.out_shape
        self.scratch = first.scratch + second.scratch

    def run(self, ins, outs, sems, start):
        for ride in self.rides:
            n_in, n_out, n_sem = len(ride.arrays), len(ride.out_shape), len(ride.scratch)
            ride.run(ins[:n_in], outs[:n_out], sems[:n_sem], start)
            ins, outs, sems = ins[n_in:], outs[n_out:], sems[n_sem:]


def _pack(arrays, pad_rows=F32_ROWS):
    flat = jnp.concatenate([a.reshape(-1).astype(F32) for a in arrays])
    rows = -(-flat.shape[0] // (LANES * pad_rows)) * pad_rows
    return jnp.pad(flat, (0, rows * LANES - flat.shape[0])).reshape(rows, LANES)


def _unpack(packed, shapes):
    flat, out, off = packed.reshape(-1), [], 0
    for s in shapes:
        size = 1
        for dim in s:
            size *= dim
        out.append(flat[off:off + size].reshape(s))
        off += size
    return out


def _divisor_rows(n_rows, most=256):
    best = None
    for r in range(ACT_ROWS, most + 1, ACT_ROWS):
        if n_rows % r == 0:
            best = r
    return best or n_rows


def kernel(x, meta_tokens, norm_mix_g, norm_ffn_g, final_norm_g, sc_w_in, sc_conv_w, sc_w_out, rg_w_in, rg_conv_w, rg_conv_b, rg_w_gate_a, rg_b_gate_a, rg_w_gate_x, rg_b_gate_x, rg_lambda, rg_w_out, ffn_w_up, ffn_conv_w, ffn_w_down, loss_target, m_meta_tokens, m_norm_mix_g, m_norm_ffn_g, m_final_norm_g, m_sc_w_in, m_sc_conv_w, m_sc_w_out, m_rg_w_in, m_rg_conv_w, m_rg_conv_b, m_rg_w_gate_a, m_rg_b_gate_a, m_rg_w_gate_x, m_rg_b_gate_x, m_rg_lambda, m_rg_w_out, m_ffn_w_up, m_ffn_conv_w, m_ffn_w_down, v_meta_tokens, v_norm_mix_g, v_norm_ffn_g, v_final_norm_g, v_sc_w_in, v_sc_conv_w, v_sc_w_out, v_rg_w_in, v_rg_conv_w, v_rg_conv_b, v_rg_w_gate_a, v_rg_b_gate_a, v_rg_w_gate_x, v_rg_b_gate_x, v_rg_lambda, v_rg_w_out, v_ffn_w_up, v_ffn_conv_w, v_ffn_w_down):
    weights = dict(meta_tokens=meta_tokens, norm_mix_g=norm_mix_g, norm_ffn_g=norm_ffn_g, final_norm_g=final_norm_g, sc_w_in=sc_w_in, sc_conv_w=sc_conv_w, sc_w_out=sc_w_out, rg_w_in=rg_w_in, rg_conv_w=rg_conv_w, rg_conv_b=rg_conv_b, rg_w_gate_a=rg_w_gate_a, rg_b_gate_a=rg_b_gate_a, rg_w_gate_x=rg_w_gate_x, rg_b_gate_x=rg_b_gate_x, rg_lambda=rg_lambda, rg_w_out=rg_w_out, ffn_w_up=ffn_w_up, ffn_conv_w=ffn_conv_w, ffn_w_down=ffn_w_down)
    m_in = dict(meta_tokens=m_meta_tokens, norm_mix_g=m_norm_mix_g, norm_ffn_g=m_norm_ffn_g, final_norm_g=m_final_norm_g, sc_w_in=m_sc_w_in, sc_conv_w=m_sc_conv_w, sc_w_out=m_sc_w_out, rg_w_in=m_rg_w_in, rg_conv_w=m_rg_conv_w, rg_conv_b=m_rg_conv_b, rg_w_gate_a=m_rg_w_gate_a, rg_b_gate_a=m_rg_b_gate_a, rg_w_gate_x=m_rg_w_gate_x, rg_b_gate_x=m_rg_b_gate_x, rg_lambda=m_rg_lambda, rg_w_out=m_rg_w_out, ffn_w_up=m_ffn_w_up, ffn_conv_w=m_ffn_conv_w, ffn_w_down=m_ffn_w_down)
    v_in = dict(meta_tokens=v_meta_tokens, norm_mix_g=v_norm_mix_g, norm_ffn_g=v_norm_ffn_g, final_norm_g=v_final_norm_g, sc_w_in=v_sc_w_in, sc_conv_w=v_sc_conv_w, sc_w_out=v_sc_w_out, rg_w_in=v_rg_w_in, rg_conv_w=v_rg_conv_w, rg_conv_b=v_rg_conv_b, rg_w_gate_a=v_rg_w_gate_a, rg_b_gate_a=v_rg_b_gate_a, rg_w_gate_x=v_rg_w_gate_x, rg_b_gate_x=v_rg_b_gate_x, rg_lambda=v_rg_lambda, rg_w_out=v_rg_w_out, ffn_w_up=v_ffn_w_up, ffn_conv_w=v_ffn_conv_w, ffn_w_down=v_ffn_w_down)
    names = list(weights)

    seq, d = x.shape[1:]
    tm = _row_tile(seq)
    tokens, target = _tile_order(x[0], tm), _tile_order(loss_target[0], tm)
    t_len = seq + tm
    wg_rows = 5 * tm if t_len % (5 * tm) == 0 else tm
    wg_rows_in = 13 * tm if t_len % (13 * tm) == 0 else wg_rows
    xi, yi, _ = _place()
    chip = 2 * xi + yi
    mesh_axes = ("x", "y", "c")

    wire = lambda w: w.astype(WIRE_DT)
    small_sharded = ["meta_tokens", "sc_conv_w", "rg_conv_w", "rg_conv_b", "rg_b_gate_a", "rg_b_gate_x", "rg_lambda", "ffn_conv_w"]
    small_2d = {n: weights[n].reshape(-1, weights[n].shape[-1]) for n in small_sharded}
    up0_wire, w_sc_in, w_sc_out, small_by_chip = _to_wire(ffn_w_up, 0, ride=_GatherHalves(
        [wire(sc_w_in[0]), wire(sc_w_out[0]), _pack([small_2d[n] for n in small_sharded], 2 * ACT_ROWS)]))
    w_sc_out = w_sc_out.reshape(-1, d)
    gather_ffn0 = _Gather([up0_wire, *_to_wire(ffn_w_down, 0)])
    gather_rest = _Gather([*_to_wire(rg_w_in, 0), *_to_wire(rg_w_out, 0), *_to_wire(ffn_w_up, 1), *_to_wire(ffn_w_down, 1)])
    small_len = sum(a.size for a in small_2d.values())
    by_chip = small_by_chip.reshape(N_CHIPS, -1)[:, :small_len]
    full, off = {}, 0
    for n in small_sharded:
        rows, width = small_2d[n].shape
        full[n] = by_chip[:, off:off + rows * width].reshape(N_CHIPS, rows, width).transpose(1, 0, 2).reshape(rows, N_CHIPS * width)
        off += rows * width
    sc_cw, rg_cw = full["sc_conv_w"], full["rg_conv_w"]
    ffn_cw = [full["ffn_conv_w"][0:3], full["ffn_conv_w"][3:6]]
    d_rnn = rg_cw.shape[1]
    vec = jnp.concatenate([full["rg_conv_b"], full["rg_b_gate_a"], full["rg_b_gate_x"], full["rg_lambda"],
                           jnp.zeros((F32_ROWS - 4, d_rnn), F32)])
    wa, wx = rg_w_gate_a[0].astype(MXU_DT), rg_w_gate_x[0].astype(MXU_DT)
    first = _tile_order(jnp.concatenate([jnp.zeros((tm - N_META, d), F32), full["meta_tokens"]]), tm)
    g_mix = [norm_mix_g[0:1], norm_mix_g[1:2]]
    g_ffn = [norm_ffn_g[0:1], norm_ffn_g[1:2]]

    h1, hh0, w_up0, w_dn0 = _sc_fwd(tokens, first, g_mix[0], w_sc_in, sc_cw, w_sc_out, tm=tm, ride=gather_ffn0)
    h2, hu0, hc0, act0, w_rg_in, w_rg_out, w_up1, w_dn1 = _ffn_fwd(h1, g_ffn[0], w_up0, ffn_cw[0], w_dn0.reshape(-1, d), tm=tm,
                                                         ride=gather_rest)
    w_up, w_dn, w_rg_out = [w_up0, w_up1], [w_dn0.reshape(-1, d), w_dn1.reshape(-1, d)], w_rg_out.reshape(-1, d)
    h3, hh1, hs, gates = _rg_fwd(h2, g_mix[1], w_rg_in, rg_cw, vec, wa, wx, w_rg_out, tm=tm)
    dh4, hu1, hc1, act1, sq, d_final = _ffn_fwd(h3, g_ffn[1], w_up[1], ffn_cw[1], w_dn[1], tm=tm,
                                     loss=(target, final_norm_g.reshape(1, d)))
    loss = lax.psum(jnp.sum(sq[0]) * (0.5 / d), mesh_axes)

    def by_chip_rows(pair):
        return [p.reshape(N_CHIPS, -1, d) for p in pair]

    def ffn_backward(dh_out, h_in, hu, hc, act, layer, ride):
        dh_in, dhu, hn, dcw, dg, *landed = _ffn_bwd(dh_out, hu, hc, h_in, g_ffn[layer], w_up[layer], ffn_cw[layer],
                                                         w_dn[layer], tm=tm, ride=ride)
        d_up = _weight_grad(hn, dhu, N_CHIPS, rows=wg_rows_in)
        d_dn = by_chip_rows(_weight_grad(act, dh_out, 1, rows=wg_rows))
        return dh_in, d_up, d_dn, dcw[0:3], dg[0], landed

    dh3, d_up1, d_dn1, d_fcw1, d_gf1, _ = ffn_backward(dh4, h3, hu1, hc1, act1, 1, None)
    dh2, dhh1, y_rg, hn_rg, d_vec, d_wa, d_wx, d_gm1, *landed_ffn1 = _rg_bwd(
        dh3, hh1, hs, gates, h2, g_mix[1], w_rg_in, rg_cw, vec, wa, wx, w_rg_out, tm=tm,
        ride=_Scatter([d_up1[1], d_dn1[1]]))
    d_rg_in = _weight_grad(hn_rg, dhh1, N_CHIPS, rows=wg_rows_in)
    d_rg_out = by_chip_rows(_weight_grad(y_rg, dh3, 1, rows=wg_rows))
    early = {"rg_conv_w": d_vec[G_CONV_W:G_CONV_W + 4], "rg_conv_b": d_vec[G_CONV_B:G_CONV_B + 1],
             "rg_b_gate_a": d_vec[G_B_A:G_B_A + 1], "rg_b_gate_x": d_vec[G_B_X:G_B_X + 1],
             "rg_lambda": d_vec[G_LAMBDA:G_LAMBDA + 1], "ffn_conv_w.1": d_fcw1, "norm_mix_g.1": d_gm1[0:1],
             "norm_ffn_g.1": d_gf1[None], "final_norm_g": d_final[0]}
    early_packed = _pack(list(early.values()))
    gate_names = ["rg_w_gate_a", "rg_w_gate_x"]
    to_all = _AllDevices([early_packed, d_wa.reshape(-1, LANES), d_wx.reshape(-1, LANES)])
    dh1, d_up0, d_dn0, d_fcw0, d_gf0, landed = ffn_backward(
        dh2, h1, hu0, hc0, act0, 0, _Both(_Scatter([d_rg_in[1], d_rg_out[1]]), to_all))
    landed_rg, early_by_device, gates_by_device = landed[0:2], landed[2], landed[3:]

    def core_sum(pair, received):
        own = lax.dynamic_index_in_dim(pair[0], chip, 0, keepdims=False)
        return _sum_parts(own, received, rows=_divisor_rows(own.shape[0]))

    early_big = [("rg_w_in", 0), ("rg_w_out", 0), ("ffn_w_up", 1), ("ffn_w_down", 1)]
    early_sum = [core_sum(d_rg_in, landed_rg[0]), core_sum(d_rg_out, landed_rg[1]), core_sum(d_up1, landed_ffn1[0]),
                 core_sum(d_dn1, landed_ffn1[1])]
    grad_x, dhh0, z_sc, hn_sc, d_sccw, d_gm0, d_first, *landed = _sc_bwd(
        dh1, hh0, tokens, first, g_mix[0], w_sc_in, sc_cw, w_sc_out, tm=tm,
        ride=_Both(_Scatter([d_up0[1], d_dn0[1]]), _Swap(early_sum)))
    landed_ffn0, early_other = landed[0:2], landed[2:]
    late = {"meta_tokens": _time_order(d_first, tm)[tm - N_META:], "sc_conv_w": d_sccw[0:3], "ffn_conv_w.0": d_fcw0,
            "norm_mix_g.0": d_gm0[0:1], "norm_ffn_g.0": d_gf0[None]}
    late_packed = _pack(list(late.values()))
    ffn0_big = [("ffn_w_up", 0), ("ffn_w_down", 0)]
    ffn0_sum = [core_sum(d_up0, landed_ffn0[0]), core_sum(d_dn0, landed_ffn0[1])]
    *d_sc_in, ffn0_up_other, ffn0_dn_other, late_by_device = _weight_grad(
        hn_sc, dhh0, N_CHIPS, rows=wg_rows_in, ride=_Both(_Swap(ffn0_sum), _AllDevices([late_packed])))
    *d_sc_out, landed_sc_in = _weight_grad(z_sc, dh1, 1, rows=wg_rows, ride=_Scatter([d_sc_in[1]]))
    d_sc_out = by_chip_rows(d_sc_out)
    landed_sc = [landed_sc_in, *_exchange(_Scatter([d_sc_out[1]]), "scatter_last")]
    grad_x = _time_order(grad_x, tm)[None]

    sc_big = [("sc_w_in", 0), ("sc_w_out", 0)]
    sc_sum = [core_sum(d_sc_in, landed_sc[0]), core_sum(d_sc_out, landed_sc[1])]
    sc_other = _exchange(_Swap(sc_sum), "swap_cores")
    out = {k: {} for k in ("grad", "delta", "m", "v")}
    stacked = {}
    for (n, layer), mine, theirs in zip(sc_big + ffn0_big + early_big, sc_sum + ffn0_sum + early_sum,
                                        [*sc_other, ffn0_up_other, ffn0_dn_other, *early_other]):
        stacked[n] = _adamw(weights[n], m_in[n], v_in[n], [mine, theirs], rows=_divisor_rows(mine.shape[0]), layer=layer,
                            into=stacked.get(n))
    for n, res in stacked.items():
        for k, key in enumerate(("grad", "delta", "m", "v")):
            out[key][n] = res[k]

    summed = {}
    for parts, packed, by_device in ((early, early_packed, early_by_device), (late, late_packed, late_by_device)):
        total = _sum_stack(by_device, rows=packed.shape[0])
        summed.update(zip(parts, _unpack(total, [p.shape for p in parts.values()])))
    for n in ("ffn_conv_w", "norm_mix_g", "norm_ffn_g"):
        summed[n] = jnp.concatenate([summed.pop(n + ".0"), summed.pop(n + ".1")])
    for n, by_device in zip(gate_names, gates_by_device):
        as_rows = lambda a: a.reshape(1, -1, LANES)
        res = _adamw(as_rows(weights[n]), as_rows(m_in[n]), as_rows(v_in[n]), [_sum_stack(by_device, rows=256)], rows=256)
        for k, key in enumerate(("grad", "delta", "m", "v")):
            out[key][n] = res[k].reshape(weights[n].shape)
    replicated = ["norm_mix_g", "norm_ffn_g", "final_norm_g"]
    small_names = small_sharded + replicated
    grads = {}
    for n in small_sharded:
        width = small_2d[n].shape[1]
        grads[n] = lax.dynamic_slice_in_dim(summed[n], chip * width, width, axis=1).reshape(weights[n].shape)
    for n in replicated:
        grads[n] = summed[n].reshape(weights[n].shape)
    shapes = [weights[n].shape for n in small_names]
    packed_w = _pack([weights[n] for n in small_names])
    res = _adamw(packed_w[None], _pack([m_in[n] for n in small_names])[None], _pack([v_in[n] for n in small_names])[None],
                 [_pack([grads[n] for n in small_names])], rows=packed_w.shape[0])
    for k, key in enumerate(("grad", "delta", "m", "v")):
        out[key].update(dict(zip(small_names, _unpack(res[k][0], shapes))))

    return (loss, grad_x, *[out["grad"][n] for n in names], *[out["delta"][n] for n in names],
            *[out["m"][n] for n in names], *[out["v"][n] for n in names])
```

```python
import functools

import jax
import jax.numpy as jnp
from jax import lax
from jax.experimental import pallas as pl
from jax.experimental.pallas import tpu as pltpu

F32 = jnp.float32
MXU_DT = jnp.bfloat16
ACT_DT = jnp.bfloat16
WIRE_DT = jnp.bfloat16
MESH_ID = pl.DeviceIdType.MESH

N_META = 16
RMS_EPS = 1e-6
RG_C = 8.0
ADAM_LR, ADAM_B1, ADAM_B2, ADAM_EPS, ADAM_WD, ADAM_STEP = 0.001, 0.9, 0.999, 1e-08, 0.01, 10
N_CHIPS = 4
VMEM_LIMIT = 60 * 1024 * 1024
F32_ROWS = 8
ACT_ROWS = 16
LANES = 128


def _row_tile(seq):
    for tm in (256, 128, 64, 32, 16):
        if seq % tm == 0:
            return tm
    raise ValueError(f"sequence length {seq} is not a multiple of 16")


def _params(n_axes=1, **kw):
    return pltpu.CompilerParams(dimension_semantics=("arbitrary",) * n_axes, vmem_limit_bytes=VMEM_LIMIT, **kw)


def _const(shape):
    return pl.BlockSpec(shape, lambda *_: (0,) * len(shape), pipeline_mode=pl.Buffered(1))


def _dot(a, b):
    return jnp.dot(a, b, preferred_element_type=F32)


def _dot_nt(a, b):
    return lax.dot_general(a, b, (((1,), (1,)), ((), ())), preferred_element_type=F32)


def _dot_tn(a, b):
    return lax.dot_general(a, b, (((0,), (0,)), ((), ())), preferred_element_type=F32)


def _sigmoid(x):
    return 0.5 + 0.5 * jnp.tanh(0.5 * x)


def _rms(h, g):
    rstd = lax.rsqrt(jnp.mean(h * h, axis=-1, keepdims=True) + RMS_EPS)
    xhat = h * rstd
    return xhat * g, xhat, rstd


def _rms_bwd(dhn, xhat, rstd, g):
    dx = dhn * g
    return rstd * (dx - xhat * jnp.mean(dx * xhat, axis=-1, keepdims=True))


def _gelu(x):
    k = 0.7978845608028654
    t = jnp.tanh(k * (x + 0.044715 * x * x * x))
    return 0.5 * x * (1.0 + t), t


def _gelu_grad(x, t):
    k = 0.7978845608028654
    return 0.5 * (1.0 + t) + 0.5 * x * (1.0 - t * t) * k * (1.0 + 3 * 0.044715 * x * x)


def _softplus(x):
    e = jnp.exp(-jnp.abs(x))
    return jnp.maximum(x, 0.0) + jnp.where(e < 1e-4, e - 0.5 * e * e, jnp.log(1.0 + e))


def _expm1_neg(z, exp_z):
    series = z * (1.0 + z * (0.5 + z * (1.0 / 6)))
    return jnp.where(z > -0.02, series, exp_z - 1.0)


def _tile_order(a, tm):
    return a.reshape(-1, F32_ROWS, tm // F32_ROWS, a.shape[-1]).swapaxes(1, 2).reshape(a.shape)


def _time_order(a, tm):
    return a.reshape(-1, tm // F32_ROWS, F32_ROWS, a.shape[-1]).swapaxes(1, 2).reshape(a.shape)


def _valid_rows(tile, tm):
    row = lax.broadcasted_iota(jnp.int32, (tm, 1), 0)
    time = (row & (F32_ROWS - 1)) * (tm // F32_ROWS) + (row >> 3) + tile * tm
    return time >= tm - N_META


def _sublane():
    return lax.broadcasted_iota(jnp.int32, (F32_ROWS, 1), 0)


def _past_rows(width):
    return (width - 1) * F32_ROWS


def _halo_block(past, tm, nt):
    rows = -(-past // ACT_ROWS) * ACT_ROWS
    return rows, lambda i: (jnp.maximum((nt - 1 - i) * (tm // rows) - 1, 0), 0)


def _link_past(buf, cols, width, tm):
    past = _past_rows(width)
    for k in range(1, width):
        rows = pl.ds(past - F32_ROWS * k, F32_ROWS)
        before = pltpu.roll(buf[rows, cols], 1, 0)
        mine = pltpu.roll(buf[pl.ds(past + tm - F32_ROWS * k, F32_ROWS), cols], 1, 0)
        buf[rows, cols] = jnp.where(_sublane() == 0, before, mine)


def _link_future(buf, cols, width, tm):
    for k in range(1, width):
        rows = pl.ds(tm + F32_ROWS * (k - 1), F32_ROWS)
        after = pltpu.roll(buf[rows, cols], F32_ROWS - 1, 0)
        mine = pltpu.roll(buf[pl.ds(F32_ROWS * (k - 1), F32_ROWS), cols], F32_ROWS - 1, 0)
        buf[rows, cols] = jnp.where(_sublane() == F32_ROWS - 1, after, mine)


def _conv_taps(buf, cols, width, tm):
    return [buf[pl.ds(F32_ROWS * k, tm), cols] for k in range(width)]


def _conv_back(buf, cw_ref, cols, width, tm):
    return sum(cw_ref[k:k + 1, cols] * buf[pl.ds(F32_ROWS * (width - 1 - k), tm), cols] for k in range(width))


ANY = pl.BlockSpec(memory_space=pl.ANY)


def _place():
    return lax.axis_index("x"), lax.axis_index("y"), lax.axis_index("c")


def _other_chips(x, y):
    return [(1 - x, y), (x, 1 - y), (1 - x, 1 - y)]


class _Gather:
    def __init__(self, shards):
        nk = len(shards)
        self.arrays = list(shards)
        self.out_shape = [jax.ShapeDtypeStruct((N_CHIPS,) + s.shape, s.dtype) for s in shards]
        self.scratch = [pltpu.SemaphoreType.DMA((nk, 3)), pltpu.SemaphoreType.DMA((nk, 3)), pltpu.SemaphoreType.DMA((nk,))]

    def run(self, ins, outs, sems, start):
        send_sems, recv_sems, local_sems = sems
        x, y, c = _place()
        mine = 2 * x + y
        for k in range(len(ins)):
            local = pltpu.make_async_copy(ins[k], outs[k].at[mine], local_sems.at[k])
            local.start() if start else local.wait()
            for j, (px, py) in enumerate(_other_chips(x, y)):
                sems_kj = dict(send_sem=send_sems.at[k, j], recv_sem=recv_sems.at[k, j], device_id=(px, py, c),
                               device_id_type=MESH_ID)
                send = pltpu.make_async_remote_copy(src_ref=ins[k], dst_ref=outs[k].at[mine], **sems_kj)
                if start:
                    send.start()
                else:
                    pltpu.make_async_remote_copy(src_ref=ins[k], dst_ref=outs[k].at[2 * px + py], **sems_kj).wait_recv()
                    send.wait_send()


class _GatherHalves:
    def __init__(self, shards):
        nk = len(shards)
        self.arrays = list(shards)
        self.out_shape = [jax.ShapeDtypeStruct((N_CHIPS,) + s.shape, s.dtype) for s in shards]
        self.scratch = [pltpu.SemaphoreType.DMA((nk, 3)) for _ in range(4)] + [pltpu.SemaphoreType.DMA((nk,))]

    def run(self, ins, outs, sems, start):
        far_send, far_recv, near_send, near_recv, local_sems = sems
        x, y, c = _place()
        mine = 2 * x + y
        for phase in ((0,) if start else (1, 2)):
            for k in range(len(ins)):
                half = ins[k].shape[0] // 2
                my_half = pl.ds(pl.multiple_of(c * half, ACT_ROWS), half)
                other_half = pl.ds(pl.multiple_of((1 - c) * half, ACT_ROWS), half)
                if phase != 1:
                    local = pltpu.make_async_copy(ins[k], outs[k].at[mine], local_sems.at[k])
                    local.start() if phase == 0 else local.wait()
                for j, (px, py) in enumerate(_other_chips(x, y)):
                    theirs = 2 * px + py
                    far = dict(send_sem=far_send.at[k, j], recv_sem=far_recv.at[k, j], device_id=(px, py, c),
                               device_id_type=MESH_ID)
                    near = dict(send_sem=near_send.at[k, j], recv_sem=near_recv.at[k, j], device_id=(x, y, 1 - c),
                                device_id_type=MESH_ID)
                    landed = outs[k].at[theirs, my_half]
                    send = lambda: pltpu.make_async_remote_copy(src_ref=ins[k].at[my_half], dst_ref=outs[k].at[mine, my_half], **far)
                    pass_on = lambda: pltpu.make_async_remote_copy(src_ref=landed, dst_ref=landed, **near)
                    if phase == 0:
                        send().start()
                    elif phase == 1:
                        pltpu.make_async_remote_copy(src_ref=ins[k].at[my_half], dst_ref=landed, **far).wait_recv()
                        pass_on().start()
                    else:
                        pltpu.make_async_remote_copy(src_ref=landed, dst_ref=outs[k].at[theirs, other_half], **near).wait_recv()
                        pass_on().wait_send()
                        send().wait_send()


class _Scatter:
    def __init__(self, parts):
        nk = len(parts)
        self.arrays = list(parts)
        self.out_shape = [jax.ShapeDtypeStruct((3,) + p.shape[1:], p.dtype) for p in parts]
        self.scratch = [pltpu.SemaphoreType.DMA((nk, 3)), pltpu.SemaphoreType.DMA((nk, 3))]

    def run(self, ins, outs, sems, start):
        send_sems, recv_sems = sems
        x, y, c = _place()
        for k in range(len(ins)):
            for j, (px, py) in enumerate(_other_chips(x, y)):
                send = pltpu.make_async_remote_copy(
                    src_ref=ins[k].at[2 * px + py], dst_ref=outs[k].at[j], send_sem=send_sems.at[k, j],
                    recv_sem=recv_sems.at[k, j], device_id=(px, py, c), device_id_type=MESH_ID)
                if start:
                    send.start()
                else:
                    send.wait_recv()
                    send.wait_send()


def _exchange(ride, name):
    n_in, n_out = len(ride.arrays), len(ride.out_shape)

    def body(*refs):
        ride.run(refs[:n_in], refs[n_in:n_in + n_out], refs[n_in + n_out:], start=True)
        ride.run(refs[:n_in], refs[n_in:n_in + n_out], refs[n_in + n_out:], start=False)

    return pl.pallas_call(body, name=name, in_specs=[ANY] * n_in, out_specs=[ANY] * n_out, out_shape=ride.out_shape,
                          scratch_shapes=ride.scratch)(*ride.arrays)


def _launch(body, operands, *, name, grid, in_specs, out_specs, out_shape, scratch_shapes=(), ride=None):
    common = dict(name=name, grid=grid, compiler_params=_params(len(grid)))
    if ride is None:
        return pl.pallas_call(body, in_specs=in_specs, out_specs=out_specs, out_shape=out_shape,
                              scratch_shapes=list(scratch_shapes), **common)(*operands)
    n_in, n_out, n_scr = len(operands), len(out_shape), len(scratch_shapes)
    r_in, r_out = len(ride.arrays), len(ride.out_shape)

    def riding(*refs):
        ins, refs = refs[:n_in], refs[n_in:]
        r_ins, refs = refs[:r_in], refs[r_in:]
        outs, refs = refs[:n_out], refs[n_out:]
        r_outs, refs = refs[:r_out], refs[r_out:]
        scr, r_sems = refs[:n_scr], refs[n_scr:]
        step = [pl.program_id(axis) for axis in range(len(grid))]
        first = functools.reduce(jnp.logical_and, [s == 0 for s in step])
        last = functools.reduce(jnp.logical_and, [s == size - 1 for s, size in zip(step, grid)])

        @pl.when(first)
        def _():
            ride.run(r_ins, r_outs, r_sems, start=True)

        body(*ins, *outs, *scr)

        @pl.when(last)
        def _():
            ride.run(r_ins, r_outs, r_sems, start=False)

    return pl.pallas_call(
        riding, in_specs=list(in_specs) + [ANY] * r_in, out_specs=list(out_specs) + [ANY] * r_out,
        out_shape=list(out_shape) + ride.out_shape, scratch_shapes=list(scratch_shapes) + ride.scratch, **common,
    )(*operands, *ride.arrays)


def _sc_fwd(x, first, g, w_in, cw, w_out, *, tm, ride=None):
    seq, d = x.shape
    nt = seq // tm + 1
    nq, _, n = w_in.shape
    width = cw.shape[0]
    past = _past_rows(width)

    def body(x_ref, first_ref, g_ref, win_ref, cw_ref, wout_ref, h1_ref, hh_ref, hh_scr, cbuf):
        i = pl.program_id(0)

        @pl.when(i == 0)
        def _():
            cbuf[pl.ds(0, past), :] = jnp.zeros((past, d), F32)

        h = jnp.where(i == 0, first_ref[...], x_ref[...])
        hn = _rms(h, g_ref[...])[0].astype(MXU_DT)
        for q in range(nq):
            hh_scr[:, q * n:(q + 1) * n] = _dot(hn, win_ref[q])
        hh_ref[...] = hh_scr[...].astype(hh_ref.dtype)
        b = hh_scr[:, 0:d]
        cbuf[pl.ds(past, tm), :] = hh_scr[:, d:2 * d] * hh_scr[:, 2 * d:3 * d]
        last = cbuf[pl.ds(tm, past), :]
        _link_past(cbuf, slice(None), width, tm)
        u = sum(cw_ref[k:k + 1, :] * tap for k, tap in enumerate(_conv_taps(cbuf, slice(None), width, tm)))
        cbuf[pl.ds(0, past), :] = last
        h1_ref[...] = h + _dot((b * u).astype(MXU_DT), wout_ref[...])

    return _launch(
        body, [x, first, g, w_in, cw, w_out], name="sc_fwd", grid=(nt,),
        in_specs=[pl.BlockSpec((tm, d), lambda i: (jnp.maximum(i - 1, 0), 0)), _const((tm, d)), _const((1, d)),
                  _const(w_in.shape), _const(cw.shape), _const(w_out.shape)],
        out_specs=[pl.BlockSpec((tm, d), lambda i: (i, 0)), pl.BlockSpec((tm, nq * n), lambda i: (i, 0))],
        out_shape=[jax.ShapeDtypeStruct((nt * tm, d), F32), jax.ShapeDtypeStruct((nt * tm, nq * n), ACT_DT)],
        scratch_shapes=[pltpu.VMEM((tm, nq * n), F32), pltpu.VMEM((past + tm, d), F32)],
        ride=ride,
    )


def _sc_bwd(dh, hh, x, first, g, w_in, cw, w_out, *, tm, ride=None):
    t_len, d = dh.shape
    nt = t_len // tm
    nq, _, n = w_in.shape
    width = cw.shape[0]
    past = _past_rows(width)
    halo_rows, halo_index = _halo_block(past, tm, nt)

    def body(dh_ref, hh_ref, hhp_ref, x_ref, first_ref, g_ref, win_ref, cw_ref, wout_ref,
             dx_ref, dhh_ref, z_ref, hn_ref, dcw_ref, dg_ref, dfirst_ref, cbuf, dbuf):
        i = pl.program_id(0)
        r = nt - 1 - i

        @pl.when(i == 0)
        def _():
            dbuf[pl.ds(tm, past), :] = jnp.zeros((past, d), F32)
            dcw_ref[...] = jnp.zeros_like(dcw_ref)
            dg_ref[...] = jnp.zeros_like(dg_ref)

        dh_out = dh_ref[...]
        b = hh_ref[:, 0:d].astype(F32)
        c = hh_ref[:, d:2 * d].astype(F32)
        v = hh_ref[:, 2 * d:3 * d].astype(F32)
        prev = hhp_ref[...].astype(F32)[halo_rows - past:, :]
        cbuf[pl.ds(0, past), :] = jnp.where(r > 0, prev[:, d:2 * d] * prev[:, 2 * d:3 * d], 0.0)
        cbuf[pl.ds(past, tm), :] = c * v
        _link_past(cbuf, slice(None), width, tm)
        taps = _conv_taps(cbuf, slice(None), width, tm)
        u = sum(cw_ref[k:k + 1, :] * taps[k] for k in range(width))
        z_ref[...] = (b * u).astype(z_ref.dtype)
        dz = _dot_nt(dh_out.astype(MXU_DT), wout_ref[...])
        d_b = (dz * u).astype(dhh_ref.dtype)
        dhh_ref[:, 0:d] = d_b
        parts = [_dot_nt(d_b[:, 0:n], win_ref[0])]
        du = dz * b
        for k in range(width):
            dcw_ref[k:k + 1, :] += jnp.sum(taps[k] * du, axis=0, keepdims=True)
        dbuf[pl.ds(0, tm), :] = du
        _link_future(dbuf, slice(None), width, tm)
        dcv = _conv_back(dbuf, cw_ref, slice(None), width, tm)
        dbuf[pl.ds(tm, past), :] = dbuf[pl.ds(0, past), :]
        d_c, d_v = (dcv * v).astype(dhh_ref.dtype), (dcv * c).astype(dhh_ref.dtype)
        dhh_ref[:, d:2 * d] = d_c
        dhh_ref[:, 2 * d:3 * d] = d_v
        rest = jnp.concatenate([d_b[:, n:], d_c, d_v], axis=1)
        parts += [_dot_nt(rest[:, (q - 1) * n:q * n], win_ref[q]) for q in range(1, nq)]
        dhn = functools.reduce(lambda a, b: a + b, parts)
        h_in = jnp.where(r == 0, first_ref[...], x_ref[...])
        dh_in = _norm_bwd_tile(dhn, h_in, dh_out, g_ref[...], _valid_rows(r, tm), hn_ref, dg_ref)

        @pl.when(r == 0)
        def _():
            dfirst_ref[...] = dh_in

        @pl.when(r > 0)
        def _():
            dx_ref[...] = dh_in

    rev = lambda i: (nt - 1 - i, 0)
    rev_x = lambda i: (jnp.maximum(nt - 2 - i, 0), 0)
    return _launch(
        body, [dh, hh, hh, x, first, g, w_in, cw, w_out], name="sc_bwd", grid=(nt,),
        in_specs=[pl.BlockSpec((tm, d), rev), pl.BlockSpec((tm, 3 * d), rev), pl.BlockSpec((halo_rows, 3 * d), halo_index),
                  pl.BlockSpec((tm, d), rev_x), _const((tm, d)), _const((1, d)), _const(w_in.shape), _const(cw.shape),
                  _const(w_out.shape)],
        out_specs=[pl.BlockSpec((tm, d), rev_x), pl.BlockSpec((tm, 3 * d), rev), pl.BlockSpec((tm, d), rev),
                   pl.BlockSpec((tm, d), rev), _const((F32_ROWS, d)), _const((F32_ROWS, d)), _const((tm, d))],
        out_shape=[jax.ShapeDtypeStruct((t_len - tm, d), F32), jax.ShapeDtypeStruct((t_len, 3 * d), ACT_DT),
                   jax.ShapeDtypeStruct((t_len, d), ACT_DT), jax.ShapeDtypeStruct((t_len, d), ACT_DT),
                   jax.ShapeDtypeStruct((F32_ROWS, d), F32), jax.ShapeDtypeStruct((F32_ROWS, d), F32),
                   jax.ShapeDtypeStruct((tm, d), F32)],
        scratch_shapes=[pltpu.VMEM((past + tm, d), F32), pltpu.VMEM((tm + past, d), F32)],
        ride=ride,
    )


def _ffn_fwd(h, g, w_up, cw, w_down, *, tm, ride=None, loss=None):
    t_len, d = h.shape
    nt = t_len // tm
    nq, _, n = w_up.shape
    width = cw.shape[0]
    past = _past_rows(width)

    def body(h_ref, g_ref, wup_ref, cw_ref, wdn_ref, *rest):
        if loss is None:
            out_ref, hu_ref, hc_ref, act_ref, ubuf, tail = rest
        else:
            t_ref, gf_ref, out_ref, hu_ref, hc_ref, act_ref, sq_ref, dgf_ref, ubuf, tail = rest
        i = pl.program_id(0)

        @pl.when(i == 0)
        def _():
            tail[...] = jnp.zeros_like(tail)

        h_in = h_ref[...]
        hn = _rms(h_in, g_ref[...])[0].astype(MXU_DT)
        ubuf[pl.ds(0, past), :] = tail[...]
        for q in range(nq):
            ubuf[pl.ds(past, tm), q * n:(q + 1) * n] = _dot(hn, wup_ref[q])
        hu_ref[...] = ubuf[pl.ds(past, tm), :].astype(hu_ref.dtype)
        tail[...] = ubuf[pl.ds(tm, past), :]
        _link_past(ubuf, slice(None), width, tm)
        acc = h_in
        for j in range(nq // 2):
            gcol, vcol = slice(j * n, (j + 1) * n), slice((nq // 2 + j) * n, (nq // 2 + j + 1) * n)
            conv = lambda cols: sum(cw_ref[k:k + 1, cols] * tap for k, tap in enumerate(_conv_taps(ubuf, cols, width, tm)))
            gj, vj = conv(gcol), conv(vcol)
            hc_ref[:, gcol] = gj.astype(hc_ref.dtype)
            hc_ref[:, vcol] = vj.astype(hc_ref.dtype)
            act = (gj * _sigmoid(gj) * vj).astype(MXU_DT)
            act_ref[:, gcol] = act.astype(act_ref.dtype)
            acc = acc + _dot(act, wdn_ref[j * n:(j + 1) * n, :])
        if loss is None:
            out_ref[...] = acc
            return

        @pl.when(i == 0)
        def _():
            sq_ref[...] = jnp.zeros_like(sq_ref)
            dgf_ref[...] = jnp.zeros_like(dgf_ref)
            out_ref[...] = jnp.zeros_like(out_ref)

        @pl.when(i > 0)
        def _():
            gain = gf_ref[...]
            out, xhat, rstd = _rms(acc, gain)
            err = out - t_ref[...]
            sq_ref[0:1, :] += jnp.sum(err * err, axis=0, keepdims=True)
            dout = err * (1.0 / d)
            dgf_ref[0:1, :] += jnp.sum(dout * xhat, axis=0, keepdims=True)
            out_ref[...] = _rms_bwd(dout, xhat, rstd, gain)

    row = lambda i: (i, 0)
    stat = jax.ShapeDtypeStruct((F32_ROWS, d), F32)
    return _launch(
        body, [h, g, w_up, cw, w_down] + list(loss or ()), name="ffn_fwd", grid=(nt,),
        in_specs=[pl.BlockSpec((tm, d), row), _const((1, d)), _const(w_up.shape), _const(cw.shape), _const(w_down.shape)]
        + ([pl.BlockSpec((tm, d), lambda i: (jnp.maximum(i - 1, 0), 0)), _const((1, d))] if loss else []),
        out_specs=[pl.BlockSpec((tm, d), row), pl.BlockSpec((tm, nq * n), row), pl.BlockSpec((tm, nq * n), row),
                   pl.BlockSpec((tm, nq // 2 * n), row)] + ([_const(stat.shape)] * 2 if loss else []),
        out_shape=[jax.ShapeDtypeStruct((t_len, d), F32), jax.ShapeDtypeStruct((t_len, nq * n), ACT_DT),
                   jax.ShapeDtypeStruct((t_len, nq * n), ACT_DT), jax.ShapeDtypeStruct((t_len, nq // 2 * n), ACT_DT)]
        + ([stat, stat] if loss else []),
        scratch_shapes=[pltpu.VMEM((past + tm, nq * n), F32), pltpu.VMEM((past, nq * n), F32)],
        ride=ride,
    )


def _norm_bwd_tile(dhn, h_in, dh, gain, valid, hn_ref, dg_ref):
    hn, xhat, rstd = _rms(h_in, gain)
    hn_ref[...] = hn.astype(hn_ref.dtype)
    dg_ref[0:1, :] += jnp.sum(dhn * xhat, axis=0, keepdims=True)
    return jnp.where(valid, dh + _rms_bwd(dhn, xhat, rstd, gain), 0.0)


def _ffn_bwd(dh, hu, hc, h, g, w_up, cw, w_down, *, tm, ride=None):
    t_len, d = dh.shape
    nt = t_len // tm
    ff = hu.shape[1]
    n = ff // 4
    width = cw.shape[0]
    past = _past_rows(width)
    halo_rows, halo_index = _halo_block(past, tm, nt)

    def body(dh_ref, hu_ref, hup_ref, hc_ref, h_ref, g_ref, wup_ref, cw_ref, wdn_ref,
             dhin_ref, dhu_ref, hn_ref, dcw_ref, dg_ref, ubuf, dbuf, head):
        i = pl.program_id(0)
        r = nt - 1 - i

        @pl.when(i == 0)
        def _():
            head[...] = jnp.zeros_like(head)
            dcw_ref[...] = jnp.zeros_like(dcw_ref)
            dg_ref[...] = jnp.zeros_like(dg_ref)

        dh_out = dh_ref[...]
        dhb = dh_out.astype(MXU_DT)
        dhn_parts = []
        d_act = [_dot_nt(dhb, wdn_ref[j * n:(j + 1) * n, :]) for j in range(2)]
        for j in range(2):
            mine = slice(0, n), slice(n, 2 * n)
            full = slice(j * n, (j + 1) * n), slice((2 + j) * n, (3 + j) * n)
            for here, there in zip(mine, full):
                prev = hup_ref[:, there].astype(F32)[halo_rows - past:, :]
                ubuf[pl.ds(0, past), here] = jnp.where(r > 0, prev, 0.0)
                ubuf[pl.ds(past, tm), here] = hu_ref[:, there].astype(F32)
                dbuf[pl.ds(tm, past), here] = head[:, there]
            _link_past(ubuf, slice(None), width, tm)
            gj, vj = hc_ref[:, full[0]].astype(F32), hc_ref[:, full[1]].astype(F32)
            sg = _sigmoid(gj)
            s = gj * sg
            da = d_act[j]
            dbuf[pl.ds(0, tm), mine[1]] = da * s
            dbuf[pl.ds(0, tm), mine[0]] = da * vj * (sg * (1.0 + gj * (1.0 - sg)))
            for here, there in zip(mine, full):
                head[:, there] = dbuf[pl.ds(0, past), here]
            _link_future(dbuf, slice(None), width, tm)
            for here, there in zip(mine, full):
                dy = dbuf[pl.ds(0, tm), here]
                for k, tap in enumerate(_conv_taps(ubuf, here, width, tm)):
                    dcw_ref[k:k + 1, there] += jnp.sum(tap * dy, axis=0, keepdims=True)
                dhu = sum(cw_ref[k:k + 1, there] * dbuf[pl.ds(F32_ROWS * (width - 1 - k), tm), here]
                          for k in range(width)).astype(dhu_ref.dtype)
                dhu_ref[:, there] = dhu
                dhn_parts.append(_dot_nt(dhu, wup_ref[there.start // n]))
        dhn = (dhn_parts[0] + dhn_parts[1]) + (dhn_parts[2] + dhn_parts[3])
        dhin_ref[...] = _norm_bwd_tile(dhn, h_ref[...], dh_out, g_ref[...], _valid_rows(r, tm), hn_ref, dg_ref)

    rev = lambda i: (nt - 1 - i, 0)
    return _launch(
        body, [dh, hu, hu, hc, h, g, w_up, cw, w_down], name="ffn_bwd", grid=(nt,),
        in_specs=[pl.BlockSpec((tm, d), rev), pl.BlockSpec((tm, ff), rev), pl.BlockSpec((halo_rows, ff), halo_index),
                  pl.BlockSpec((tm, ff), rev), pl.BlockSpec((tm, d), rev), _const((1, d)), _const(w_up.shape), _const(cw.shape), _const(w_down.shape)],
        out_specs=[pl.BlockSpec((tm, d), rev), pl.BlockSpec((tm, ff), rev),
                   pl.BlockSpec((tm, d), rev), _const((F32_ROWS, ff)), _const((F32_ROWS, d))],
        out_shape=[jax.ShapeDtypeStruct((t_len, d), F32),
                   jax.ShapeDtypeStruct((t_len, ff), ACT_DT), jax.ShapeDtypeStruct((t_len, d), ACT_DT),
                   jax.ShapeDtypeStruct((F32_ROWS, ff), F32), jax.ShapeDtypeStruct((F32_ROWS, d), F32)],
        scratch_shapes=[pltpu.VMEM((past + tm, 2 * n), F32), pltpu.VMEM((tm + past, 2 * n), F32), pltpu.VMEM((past, ff), F32)],
        ride=ride,
    )


V_CONV_B, V_B_A, V_B_X, V_LAMBDA = 0, 1, 2, 3
G_CONV_W, G_CONV_B, G_B_A, G_B_X, G_LAMBDA = 0, 4, 5, 6, 7


def _scan(a_ref, b_ref, edge, tm, reverse):
    nj = tm // F32_ROWS
    order = range(nj - 1, -1, -1) if reverse else range(nj)
    slab = lambda ref, j: ref[pl.ds(F32_ROWS * j, F32_ROWS), :]
    a_run = b_run = None
    for j in order:
        a_j, b_j = slab(a_ref, j), slab(b_ref, j)
        if a_run is not None:
            b_j = b_j + a_j * b_run
            a_j = a_j * a_run
            b_ref[pl.ds(F32_ROWS * j, F32_ROWS), :] = b_j
            a_ref[pl.ds(F32_ROWS * j, F32_ROWS), :] = a_j
        a_run, b_run = a_j, b_j
    sub = _sublane()
    shift = 1
    while shift < F32_ROWS:
        amount = F32_ROWS - shift if reverse else shift
        keep = (sub < F32_ROWS - shift) if reverse else (sub >= shift)
        b_run = jnp.where(keep, b_run + a_run * pltpu.roll(b_run, amount, 0), b_run)
        a_run = jnp.where(keep, a_run * pltpu.roll(a_run, amount, 0), a_run)
        shift *= 2
    outer = edge[0:1, :] if reverse else edge[F32_ROWS - 1:F32_ROWS, :]
    ends = b_run + a_run * outer
    if reverse:
        carry = jnp.where(sub == F32_ROWS - 1, outer, pltpu.roll(ends, F32_ROWS - 1, 0))
    else:
        carry = jnp.where(sub == 0, outer, pltpu.roll(ends, 1, 0))
    for j in range(nj):
        b_ref[pl.ds(F32_ROWS * j, F32_ROWS), :] = slab(b_ref, j) + slab(a_ref, j) * carry
    return slab(b_ref, 0 if reverse else nj - 1)


def _rg_gates(u, vec_ref, wa_ref, wx_ref, pre_scr, nb, bd):
    ub = u.astype(MXU_DT)
    for k in range(nb):
        blk = slice(k * bd, (k + 1) * bd)
        pre_scr[0, :, blk] = _dot(ub[:, blk], wa_ref[k])
        pre_scr[1, :, blk] = _dot(ub[:, blk], wx_ref[k])
    r_gate = _sigmoid(pre_scr[0] + vec_ref[V_B_A:V_B_A + 1, :])
    i_gate = _sigmoid(pre_scr[1] + vec_ref[V_B_X:V_B_X + 1, :])
    return r_gate, i_gate


def _rg_decay(r_gate, vec_ref):
    sp = _softplus(-vec_ref[V_LAMBDA:V_LAMBDA + 1, :])
    log_a = -RG_C * r_gate * sp
    a = jnp.exp(log_a)
    one_minus_a2 = jnp.maximum(-_expm1_neg(2.0 * log_a, a * a), 1e-30)
    inv_mult = lax.rsqrt(one_minus_a2)
    return a, one_minus_a2 * inv_mult, inv_mult, sp


def _rg_fwd(h, g, w_in, cw, vec, wa, wx, w_out, *, tm):
    t_len, d = h.shape
    nt = t_len // tm
    nq, _, n = w_in.shape
    dr = 2 * n
    width = cw.shape[0]
    past = _past_rows(width)
    nb, bd, _ = wa.shape

    def body(h_ref, g_ref, win_ref, cw_ref, vec_ref, wa_ref, wx_ref, wout_ref, out_ref, hh_ref, hs_ref, gates_ref,
             gbuf, rbuf, pre_scr, tail, edge):
        i = pl.program_id(0)

        @pl.when(i == 0)
        def _():
            tail[...] = jnp.zeros_like(tail)
            edge[...] = jnp.zeros_like(edge)

        h_in = h_ref[...]
        hn = _rms(h_in, g_ref[...])[0].astype(MXU_DT)
        rbuf[pl.ds(0, past), :] = tail[...]
        for q in range(2):
            gbuf[:, q * n:(q + 1) * n] = _dot(hn, win_ref[q])
            rbuf[pl.ds(past, tm), q * n:(q + 1) * n] = _dot(hn, win_ref[2 + q])
        hh_ref[:, 0:dr] = gbuf[...].astype(hh_ref.dtype)
        hh_ref[:, dr:2 * dr] = rbuf[pl.ds(past, tm), :].astype(hh_ref.dtype)
        tail[...] = rbuf[pl.ds(tm, past), :]
        _link_past(rbuf, slice(None), width, tm)
        taps = _conv_taps(rbuf, slice(None), width, tm)
        u = sum(cw_ref[k:k + 1, :] * taps[k] for k in range(width)) + vec_ref[V_CONV_B:V_CONV_B + 1, :]
        r_gate, i_gate = _rg_gates(u, vec_ref, wa_ref, wx_ref, pre_scr, nb, bd)
        for k, kept in enumerate((u, r_gate, i_gate)):
            gates_ref[:, k * dr:(k + 1) * dr] = kept.astype(gates_ref.dtype)
        a, mult, _, _ = _rg_decay(r_gate, vec_ref)
        hs_ref[:, dr:2 * dr] = a
        hs_ref[:, 2 * dr:3 * dr] = mult
        pre_scr[0] = a
        pre_scr[1] = jnp.where(_valid_rows(i, tm), mult * (i_gate * u), 0.0)
        edge[...] = _scan(pre_scr.at[0], pre_scr.at[1], edge[...], tm, reverse=False)
        hs = pre_scr[1]
        hs_ref[:, 0:dr] = hs
        gate, th = _gelu(gbuf[...])
        gates_ref[:, 3 * dr:4 * dr] = th.astype(gates_ref.dtype)
        y = hs * gate
        out_ref[...] = h_in + _dot(y.astype(MXU_DT), wout_ref[...])

    row = lambda i: (i, 0)
    return pl.pallas_call(
        body, name="rg_fwd", grid=(nt,),
        in_specs=[pl.BlockSpec((tm, d), row), _const((1, d)), _const(w_in.shape), _const(cw.shape), _const(vec.shape),
                  _const(wa.shape), _const(wx.shape), _const(w_out.shape)],
        out_specs=[pl.BlockSpec((tm, d), row), pl.BlockSpec((tm, 2 * dr), row), pl.BlockSpec((tm, 3 * dr), row),
                   pl.BlockSpec((tm, 4 * dr), row)],
        out_shape=[jax.ShapeDtypeStruct((t_len, d), F32), jax.ShapeDtypeStruct((t_len, 2 * dr), ACT_DT),
                   jax.ShapeDtypeStruct((t_len, 3 * dr), F32), jax.ShapeDtypeStruct((t_len, 4 * dr), ACT_DT)],
        scratch_shapes=[pltpu.VMEM((tm, dr), F32), pltpu.VMEM((past + tm, dr), F32), pltpu.VMEM((2, tm, dr), F32),
                        pltpu.VMEM((past, dr), F32), pltpu.VMEM((F32_ROWS, dr), F32)],
        compiler_params=_params(),
    )(h, g, w_in, cw, vec, wa, wx, w_out)


def _rg_bwd(dh, hh, hs, gates, h, g, w_in, cw, vec, wa, wx, w_out, *, tm, ride=None):
    t_len, d = dh.shape
    nt = t_len // tm
    dr = hs.shape[1] // 3
    n = w_in.shape[2]
    width = cw.shape[0]
    nb, bd, _ = wa.shape
    past = _past_rows(width)
    halo_rows, halo_index = _halo_block(past, tm, nt)
    one = F32_ROWS

    def body(dh_ref, hh_ref, hhp_ref, hs_ref, hsp_ref, gates_ref, h_ref, g_ref, win_ref, cw_ref, vec_ref, wa_ref, wx_ref, wout_ref,
             dhin_ref, dhh_ref, y_ref, hn_ref, dvec_ref, dwa_ref, dwx_ref, dg_ref, rbuf, dbuf, pre_scr, hbuf, abuf, edge):
        i = pl.program_id(0)
        r = nt - 1 - i

        @pl.when(i == 0)
        def _():
            dbuf[pl.ds(tm, past), :] = jnp.zeros((past, dr), F32)
            abuf[pl.ds(tm, one), :] = jnp.zeros((one, dr), F32)
            edge[...] = jnp.zeros_like(edge)
            dvec_ref[...] = jnp.zeros_like(dvec_ref)
            dwa_ref[...] = jnp.zeros_like(dwa_ref)
            dwx_ref[...] = jnp.zeros_like(dwx_ref)
            dg_ref[...] = jnp.zeros_like(dg_ref)

        dh_out = dh_ref[...]
        gb = hh_ref[:, 0:dr].astype(F32)
        prev = hhp_ref[...].astype(F32)[halo_rows - past:, dr:2 * dr]
        rbuf[pl.ds(0, past), :] = jnp.where(r > 0, prev, 0.0)
        rbuf[pl.ds(past, tm), :] = hh_ref[:, dr:2 * dr].astype(F32)
        _link_past(rbuf, slice(None), width, tm)
        taps = _conv_taps(rbuf, slice(None), width, tm)
        ub = gates_ref[:, 0:dr].astype(MXU_DT)
        u, r_gate, i_gate = (gates_ref[:, k * dr:(k + 1) * dr].astype(F32) for k in range(3))
        hs_t, a, mult = (hs_ref[:, k * dr:(k + 1) * dr] for k in range(3))
        inv_mult = 1.0 / mult
        sp = _softplus(-vec_ref[V_LAMBDA:V_LAMBDA + 1, :])
        hbuf[pl.ds(0, one), :] = jnp.where(r > 0, hsp_ref[...], 0.0)
        hbuf[pl.ds(one, tm), :] = hs_t
        _link_past(hbuf, slice(None), 2, tm)
        h_prev = hbuf[pl.ds(0, tm), :]
        th = gates_ref[:, 3 * dr:4 * dr].astype(F32)
        gate = 0.5 * gb * (1.0 + th)
        y_ref[...] = (hs_t * gate).astype(y_ref.dtype)
        dy = _dot_nt(dh_out.astype(MXU_DT), wout_ref[...])
        d_gb = (dy * hs_t * _gelu_grad(gb, th)).astype(dhh_ref.dtype)
        dhh_ref[:, 0:dr] = d_gb
        dhn = sum(_dot_nt(d_gb[:, q * n:(q + 1) * n], win_ref[q]) for q in range(2))
        abuf[pl.ds(0, tm), :] = a
        _link_future(abuf, slice(None), 2, tm)
        pre_scr[0] = abuf[pl.ds(one, tm), :]
        pre_scr[1] = dy * gate
        edge[...] = _scan(pre_scr.at[0], pre_scr.at[1], edge[...], tm, reverse=True)
        abuf[pl.ds(tm, one), :] = abuf[pl.ds(0, one), :]
        d_hs = pre_scr[1]
        d_b = jnp.where(_valid_rows(r, tm), d_hs, 0.0)
        d_iu = d_b * mult
        d_log_a = d_hs * h_prev * a - d_b * (i_gate * u) * (a * a) * inv_mult
        dvec_ref[G_LAMBDA:G_LAMBDA + 1, :] += jnp.sum(d_log_a * r_gate, axis=0, keepdims=True) * (-RG_C)
        d_pre_r = d_log_a * (-RG_C * sp) * r_gate * (1.0 - r_gate)
        d_pre_i = d_iu * u * i_gate * (1.0 - i_gate)
        dvec_ref[G_B_A:G_B_A + 1, :] += jnp.sum(d_pre_r, axis=0, keepdims=True)
        dvec_ref[G_B_X:G_B_X + 1, :] += jnp.sum(d_pre_i, axis=0, keepdims=True)
        dbuf[pl.ds(0, tm), :] = d_iu * i_gate
        d_pre_r = d_pre_r.astype(MXU_DT)
        d_pre_i = d_pre_i.astype(MXU_DT)
        for k in range(nb):
            blk = slice(k * bd, (k + 1) * bd)
            dwa_ref[k] += _dot_tn(ub[:, blk], d_pre_r[:, blk])
            dwx_ref[k] += _dot_tn(ub[:, blk], d_pre_i[:, blk])
            dbuf[pl.ds(0, tm), blk] += _dot_nt(d_pre_r[:, blk], wa_ref[k]) + _dot_nt(d_pre_i[:, blk], wx_ref[k])
        du = dbuf[pl.ds(0, tm), :]
        dvec_ref[G_CONV_B:G_CONV_B + 1, :] += jnp.sum(du, axis=0, keepdims=True)
        for k in range(width):
            dvec_ref[G_CONV_W + k:G_CONV_W + k + 1, :] += jnp.sum(taps[k] * du, axis=0, keepdims=True)
        _link_future(dbuf, slice(None), width, tm)
        d_rb = _conv_back(dbuf, cw_ref, slice(None), width, tm)
        dbuf[pl.ds(tm, past), :] = dbuf[pl.ds(0, past), :]
        d_rb = d_rb.astype(dhh_ref.dtype)
        dhh_ref[:, dr:2 * dr] = d_rb
        dhn = dhn + sum(_dot_nt(d_rb[:, q * n:(q + 1) * n], win_ref[2 + q]) for q in range(2))
        dhin_ref[...] = _norm_bwd_tile(dhn, h_ref[...], dh_out, g_ref[...], _valid_rows(r, tm), hn_ref, dg_ref)

        @pl.when(i == nt - 1)
        def _():
            lam = vec_ref[V_LAMBDA:V_LAMBDA + 1, :]
            dvec_ref[G_LAMBDA:G_LAMBDA + 1, :] = dvec_ref[G_LAMBDA:G_LAMBDA + 1, :] * (-_sigmoid(-lam))

    rev = lambda i: (nt - 1 - i, 0)
    return _launch(
        body, [dh, hh, hh, hs, hs, gates, h, g, w_in, cw, vec, wa, wx, w_out], name="rg_bwd", grid=(nt,),
        in_specs=[pl.BlockSpec((tm, d), rev), pl.BlockSpec((tm, 2 * dr), rev), pl.BlockSpec((halo_rows, 2 * dr), halo_index),
                  pl.BlockSpec((tm, 3 * dr), rev),
                  pl.BlockSpec((one, dr), lambda i: (jnp.maximum((nt - 1 - i) * (tm // one) - 1, 0), 0)),
                  pl.BlockSpec((tm, 4 * dr), rev), pl.BlockSpec((tm, d), rev), _const((1, d)), _const(w_in.shape),
                  _const(cw.shape), _const(vec.shape), _const(wa.shape), _const(wx.shape), _const(w_out.shape)],
        out_specs=[pl.BlockSpec((tm, d), rev), pl.BlockSpec((tm, 2 * dr), rev), pl.BlockSpec((tm, dr), rev),
                   pl.BlockSpec((tm, d), rev), _const((F32_ROWS, dr)), _const(wa.shape), _const(wx.shape),
                   _const((F32_ROWS, d))],
        out_shape=[jax.ShapeDtypeStruct((t_len, d), F32), jax.ShapeDtypeStruct((t_len, 2 * dr), ACT_DT),
                   jax.ShapeDtypeStruct((t_len, dr), ACT_DT), jax.ShapeDtypeStruct((t_len, d), ACT_DT),
                   jax.ShapeDtypeStruct((F32_ROWS, dr), F32), jax.ShapeDtypeStruct(wa.shape, F32),
                   jax.ShapeDtypeStruct(wx.shape, F32), jax.ShapeDtypeStruct((F32_ROWS, d), F32)],
        scratch_shapes=[pltpu.VMEM((past + tm, dr), F32), pltpu.VMEM((tm + past, dr), F32), pltpu.VMEM((2, tm, dr), F32),
                        pltpu.VMEM((one + tm, dr), F32), pltpu.VMEM((tm + one, dr), F32), pltpu.VMEM((F32_ROWS, dr), F32)],
        ride=ride,
    )


def _weight_grad(a, b, nb, *, rows, ride=None):
    t_len, k_dim = a.shape
    n = b.shape[1] // nb
    nt = t_len // rows

    def body(a_ref, b_ref, out_ref, wire_ref):
        @pl.when(pl.program_id(1) == 0)
        def _():
            out_ref[...] = jnp.zeros_like(out_ref)

        out_ref[0] += _dot_tn(a_ref[...].astype(MXU_DT), b_ref[...].astype(MXU_DT))

        @pl.when(pl.program_id(1) == nt - 1)
        def _():
            wire_ref[...] = out_ref[...].astype(wire_ref.dtype)

    block = pl.BlockSpec((1, k_dim, n), lambda j, i: (j, 0, 0))
    return _launch(
        body, [a, b], name="weight_grad", grid=(nb, nt),
        in_specs=[pl.BlockSpec((rows, k_dim), lambda j, i: (i, 0)), pl.BlockSpec((rows, n), lambda j, i: (i, j))],
        out_specs=[block, block],
        out_shape=[jax.ShapeDtypeStruct((nb, k_dim, n), F32), jax.ShapeDtypeStruct((nb, k_dim, n), WIRE_DT)],
        ride=ride,
    )


def _adamw(w, m, v, parts, *, rows, layer=0, into=None):
    n_layers, n_rows, n_cols = w.shape
    nt = n_rows // rows
    n_parts = len(parts)

    def body(w_ref, m_ref, v_ref, *rest):
        part_refs, (g_ref, d_ref, nm_ref, nv_ref) = rest[:n_parts], rest[-4:]
        w_ref, m_ref, v_ref, g_ref, d_ref, nm_ref, nv_ref = (r.at[0] for r in (w_ref, m_ref, v_ref, g_ref, d_ref, nm_ref, nv_ref))
        grad = part_refs[0][...].astype(F32)
        for p in part_refs[1:]:
            grad = grad + p[...].astype(F32)
        new_m = ADAM_B1 * m_ref[...] + (1.0 - ADAM_B1) * grad
        new_v = ADAM_B2 * v_ref[...] + (1.0 - ADAM_B2) * (grad * grad)
        m_hat = new_m / (1.0 - ADAM_B1 ** ADAM_STEP)
        v_hat = new_v / (1.0 - ADAM_B2 ** ADAM_STEP)
        g_ref[...] = grad
        d_ref[...] = -ADAM_LR * (m_hat / (jnp.sqrt(v_hat) + ADAM_EPS) + ADAM_WD * w_ref[...])
        nm_ref[...] = new_m
        nv_ref[...] = new_v

    spec = pl.BlockSpec((rows, n_cols), lambda i: (i, 0))
    layer_spec = pl.BlockSpec((1, rows, n_cols), lambda i: (layer, i, 0))
    into = list(into or [])
    return pl.pallas_call(
        body, name="adamw", grid=(nt,),
        in_specs=[layer_spec] * 3 + [spec] * n_parts + [ANY] * len(into), out_specs=[layer_spec] * 4,
        out_shape=[jax.ShapeDtypeStruct(w.shape, F32)] * 4,
        input_output_aliases={3 + n_parts + k: k for k in range(len(into))},
        compiler_params=_params(),
    )(w, m, v, *parts, *into)


def _to_wire(layers, *, steps=4, ride=None):
    n_arrays = len(layers)

    def body(*refs):
        for w_ref, out_ref in zip(refs[:n_arrays], refs[n_arrays:]):
            out_ref[...] = w_ref[0].astype(out_ref.dtype)

    def spec(layer):
        return lambda i: (layer, i, 0)

    return _launch(
        body, [w for w, _ in layers], name="to_wire", grid=(steps,),
        in_specs=[pl.BlockSpec((1, w.shape[1] // steps, w.shape[2]), spec(layer)) for w, layer in layers],
        out_specs=[pl.BlockSpec((w.shape[1] // steps, w.shape[2]), lambda i: (i, 0)) for w, _ in layers],
        out_shape=[jax.ShapeDtypeStruct(w.shape[1:], WIRE_DT) for w, _ in layers], ride=ride)


def _sum_stack(stack, *, rows):
    n_stack, n_rows, n_cols = stack.shape

    def body(stack_ref, out_ref):
        acc = stack_ref[0]
        for j in range(1, n_stack):
            acc = acc + stack_ref[j]
        out_ref[...] = acc

    return pl.pallas_call(
        body, name="sum_stack", grid=(n_rows // rows,),
        in_specs=[pl.BlockSpec((n_stack, rows, n_cols), lambda i: (0, i, 0))],
        out_specs=pl.BlockSpec((rows, n_cols), lambda i: (i, 0)),
        out_shape=jax.ShapeDtypeStruct((n_rows, n_cols), F32),
        compiler_params=_params(),
    )(stack)


def _sum_parts(own, recv, *, rows):
    n_rows, n_cols = own.shape
    n_recv = recv.shape[0]

    def body(own_ref, recv_ref, out_ref):
        acc = own_ref[...].astype(F32)
        for j in range(n_recv):
            acc = acc + recv_ref[j].astype(F32)
        out_ref[...] = acc

    return pl.pallas_call(
        body, name="sum_parts", grid=(n_rows // rows,),
        in_specs=[pl.BlockSpec((rows, n_cols), lambda i: (i, 0)), pl.BlockSpec((n_recv, rows, n_cols), lambda i: (0, i, 0))],
        out_specs=pl.BlockSpec((rows, n_cols), lambda i: (i, 0)),
        out_shape=jax.ShapeDtypeStruct(own.shape, F32),
        compiler_params=_params(),
    )(own, recv)


class _Swap:
    def __init__(self, arrays):
        nk = len(arrays)
        self.arrays = list(arrays)
        self.out_shape = [jax.ShapeDtypeStruct(a.shape, a.dtype) for a in arrays]
        self.scratch = [pltpu.SemaphoreType.DMA((nk,)), pltpu.SemaphoreType.DMA((nk,))]

    def run(self, ins, outs, sems, start):
        send_sems, recv_sems = sems
        x, y, c = _place()
        for k in range(len(ins)):
            send = pltpu.make_async_remote_copy(src_ref=ins[k], dst_ref=outs[k], send_sem=send_sems.at[k],
                                                recv_sem=recv_sems.at[k], device_id=(x, y, 1 - c), device_id_type=MESH_ID)
            if start:
                send.start()
            else:
                send.wait_recv()
                send.wait_send()


class _AllDevices:
    def __init__(self, arrays):
        nk = len(arrays)
        self.arrays = list(arrays)
        self.out_shape = [jax.ShapeDtypeStruct((8,) + a.shape, a.dtype) for a in arrays]
        self.scratch = [pltpu.SemaphoreType.DMA((nk, 7)), pltpu.SemaphoreType.DMA((nk, 7)), pltpu.SemaphoreType.DMA((nk,))]

    def run(self, ins, outs, sems, start):
        send_sems, recv_sems, local_sems = sems
        x, y, c = _place()
        mine = 4 * x + 2 * y + c
        for k in range(len(ins)):
            local = pltpu.make_async_copy(ins[k], outs[k].at[mine], local_sems.at[k])
            local.start() if start else local.wait()
            for flip in range(1, 8):
                px, py, pc = x ^ (flip >> 2), y ^ ((flip >> 1) & 1), c ^ (flip & 1)
                sems_f = dict(send_sem=send_sems.at[k, flip - 1], recv_sem=recv_sems.at[k, flip - 1],
                              device_id=(px, py, pc), device_id_type=MESH_ID)
                send = pltpu.make_async_remote_copy(src_ref=ins[k], dst_ref=outs[k].at[mine], **sems_f)
                if start:
                    send.start()
                else:
                    pltpu.make_async_remote_copy(src_ref=ins[k], dst_ref=outs[k].at[4 * px + 2 * py + pc], **sems_f).wait_recv()
                    send.wait_send()


class _Both:
    def __init__(self, first, second):
        self.rides = (first, second)
        self.arrays = first.arrays + second.arrays
        self.out_shape = first.out_shape + second.out_shape
        self.scratch = first.scratch + second.scratch

    def run(self, ins, outs, sems, start):
        for ride in self.rides:
            n_in, n_out, n_sem = len(ride.arrays), len(ride.out_shape), len(ride.scratch)
            ride.run(ins[:n_in], outs[:n_out], sems[:n_sem], start)
            ins, outs, sems = ins[n_in:], outs[n_out:], sems[n_sem:]


def _pack(arrays, pad_rows=F32_ROWS):
    flat = jnp.concatenate([a.reshape(-1).astype(F32) for a in arrays])
    rows = -(-flat.shape[0] // (LANES * pad_rows)) * pad_rows
    return jnp.pad(flat, (0, rows * LANES - flat.shape[0])).reshape(rows, LANES)


def _unpack(packed, shapes):
    flat, out, off = packed.reshape(-1), [], 0
    for s in shapes:
        size = 1
        for dim in s:
            size *= dim
        out.append(flat[off:off + size].reshape(s))
        off += size
    return out


def _divisor_rows(n_rows, most=256):
    best = None
    for r in range(ACT_ROWS, most + 1, ACT_ROWS):
        if n_rows % r == 0:
            best = r
    return best or n_rows


def kernel(x, meta_tokens, norm_mix_g, norm_ffn_g, final_norm_g, sc_w_in, sc_conv_w, sc_w_out, rg_w_in, rg_conv_w, rg_conv_b, rg_w_gate_a, rg_b_gate_a, rg_w_gate_x, rg_b_gate_x, rg_lambda, rg_w_out, ffn_w_up, ffn_conv_w, ffn_w_down, loss_target, m_meta_tokens, m_norm_mix_g, m_norm_ffn_g, m_final_norm_g, m_sc_w_in, m_sc_conv_w, m_sc_w_out, m_rg_w_in, m_rg_conv_w, m_rg_conv_b, m_rg_w_gate_a, m_rg_b_gate_a, m_rg_w_gate_x, m_rg_b_gate_x, m_rg_lambda, m_rg_w_out, m_ffn_w_up, m_ffn_conv_w, m_ffn_w_down, v_meta_tokens, v_norm_mix_g, v_norm_ffn_g, v_final_norm_g, v_sc_w_in, v_sc_conv_w, v_sc_w_out, v_rg_w_in, v_rg_conv_w, v_rg_conv_b, v_rg_w_gate_a, v_rg_b_gate_a, v_rg_w_gate_x, v_rg_b_gate_x, v_rg_lambda, v_rg_w_out, v_ffn_w_up, v_ffn_conv_w, v_ffn_w_down):
    weights = dict(meta_tokens=meta_tokens, norm_mix_g=norm_mix_g, norm_ffn_g=norm_ffn_g, final_norm_g=final_norm_g, sc_w_in=sc_w_in, sc_conv_w=sc_conv_w, sc_w_out=sc_w_out, rg_w_in=rg_w_in, rg_conv_w=rg_conv_w, rg_conv_b=rg_conv_b, rg_w_gate_a=rg_w_gate_a, rg_b_gate_a=rg_b_gate_a, rg_w_gate_x=rg_w_gate_x, rg_b_gate_x=rg_b_gate_x, rg_lambda=rg_lambda, rg_w_out=rg_w_out, ffn_w_up=ffn_w_up, ffn_conv_w=ffn_conv_w, ffn_w_down=ffn_w_down)
    m_in = dict(meta_tokens=m_meta_tokens, norm_mix_g=m_norm_mix_g, norm_ffn_g=m_norm_ffn_g, final_norm_g=m_final_norm_g, sc_w_in=m_sc_w_in, sc_conv_w=m_sc_conv_w, sc_w_out=m_sc_w_out, rg_w_in=m_rg_w_in, rg_conv_w=m_rg_conv_w, rg_conv_b=m_rg_conv_b, rg_w_gate_a=m_rg_w_gate_a, rg_b_gate_a=m_rg_b_gate_a, rg_w_gate_x=m_rg_w_gate_x, rg_b_gate_x=m_rg_b_gate_x, rg_lambda=m_rg_lambda, rg_w_out=m_rg_w_out, ffn_w_up=m_ffn_w_up, ffn_conv_w=m_ffn_conv_w, ffn_w_down=m_ffn_w_down)
    v_in = dict(meta_tokens=v_meta_tokens, norm_mix_g=v_norm_mix_g, norm_ffn_g=v_norm_ffn_g, final_norm_g=v_final_norm_g, sc_w_in=v_sc_w_in, sc_conv_w=v_sc_conv_w, sc_w_out=v_sc_w_out, rg_w_in=v_rg_w_in, rg_conv_w=v_rg_conv_w, rg_conv_b=v_rg_conv_b, rg_w_gate_a=v_rg_w_gate_a, rg_b_gate_a=v_rg_b_gate_a, rg_w_gate_x=v_rg_w_gate_x, rg_b_gate_x=v_rg_b_gate_x, rg_lambda=v_rg_lambda, rg_w_out=v_rg_w_out, ffn_w_up=v_ffn_w_up, ffn_conv_w=v_ffn_conv_w, ffn_w_down=v_ffn_w_down)
    names = list(weights)

    seq, d = x.shape[1:]
    tm = _row_tile(seq)
    tokens, target = _tile_order(x[0], tm), _tile_order(loss_target[0], tm)
    t_len = seq + tm
    wg_rows = 5 * tm if t_len % (5 * tm) == 0 else tm
    wg_rows_in = 13 * tm if t_len % (13 * tm) == 0 else wg_rows
    xi, yi, _ = _place()
    chip = 2 * xi + yi
    mesh_axes = ("x", "y", "c")

    wire = lambda w: w.astype(WIRE_DT)
    small_sharded = ["meta_tokens", "sc_conv_w", "rg_conv_w", "rg_conv_b", "rg_b_gate_a", "rg_b_gate_x", "rg_lambda", "ffn_conv_w"]
    small_2d = {n: weights[n].reshape(-1, weights[n].shape[-1]) for n in small_sharded}
    up0, dn0, rg_in, rg_out, up1, dn1, w_sc_in, w_sc_out, small_by_chip = _to_wire(
        [(ffn_w_up, 0), (ffn_w_down, 0), (rg_w_in, 0), (rg_w_out, 0), (ffn_w_up, 1), (ffn_w_down, 1)],
        ride=_GatherHalves([wire(sc_w_in[0]), wire(sc_w_out[0]), _pack([small_2d[n] for n in small_sharded], 2 * ACT_ROWS)]))
    w_sc_out = w_sc_out.reshape(-1, d)
    gather_ffn0 = _Gather([up0, dn0])
    gather_rest = _Gather([rg_in, rg_out, up1, dn1])
    small_len = sum(a.size for a in small_2d.values())
    by_chip = small_by_chip.reshape(N_CHIPS, -1)[:, :small_len]
    full, off = {}, 0
    for n in small_sharded:
        rows, width = small_2d[n].shape
        full[n] = by_chip[:, off:off + rows * width].reshape(N_CHIPS, rows, width).transpose(1, 0, 2).reshape(rows, N_CHIPS * width)
        off += rows * width
    sc_cw, rg_cw = full["sc_conv_w"], full["rg_conv_w"]
    ffn_cw = [full["ffn_conv_w"][0:3], full["ffn_conv_w"][3:6]]
    d_rnn = rg_cw.shape[1]
    vec = jnp.concatenate([full["rg_conv_b"], full["rg_b_gate_a"], full["rg_b_gate_x"], full["rg_lambda"],
                           jnp.zeros((F32_ROWS - 4, d_rnn), F32)])
    wa, wx = rg_w_gate_a[0].astype(MXU_DT), rg_w_gate_x[0].astype(MXU_DT)
    first = _tile_order(jnp.concatenate([jnp.zeros((tm - N_META, d), F32), full["meta_tokens"]]), tm)
    g_mix = [norm_mix_g[0:1], norm_mix_g[1:2]]
    g_ffn = [norm_ffn_g[0:1], norm_ffn_g[1:2]]

    h1, hh0, w_up0, w_dn0 = _sc_fwd(tokens, first, g_mix[0], w_sc_in, sc_cw, w_sc_out, tm=tm, ride=gather_ffn0)
    h2, hu0, hc0, act0, w_rg_in, w_rg_out, w_up1, w_dn1 = _ffn_fwd(h1, g_ffn[0], w_up0, ffn_cw[0], w_dn0.reshape(-1, d), tm=tm,
                                                         ride=gather_rest)
    w_up, w_dn, w_rg_out = [w_up0, w_up1], [w_dn0.reshape(-1, d), w_dn1.reshape(-1, d)], w_rg_out.reshape(-1, d)
    h3, hh1, hs, gates = _rg_fwd(h2, g_mix[1], w_rg_in, rg_cw, vec, wa, wx, w_rg_out, tm=tm)
    dh4, hu1, hc1, act1, sq, d_final = _ffn_fwd(h3, g_ffn[1], w_up[1], ffn_cw[1], w_dn[1], tm=tm,
                                     loss=(target, final_norm_g.reshape(1, d)))
    loss = lax.psum(jnp.sum(sq[0]) * (0.5 / d), mesh_axes)

    def by_chip_rows(pair):
        return [p.reshape(N_CHIPS, -1, d) for p in pair]

    def ffn_backward(dh_out, h_in, hu, hc, act, layer, ride):
        dh_in, dhu, hn, dcw, dg, *landed = _ffn_bwd(dh_out, hu, hc, h_in, g_ffn[layer], w_up[layer], ffn_cw[layer],
                                                         w_dn[layer], tm=tm, ride=ride)
        d_up = _weight_grad(hn, dhu, N_CHIPS, rows=wg_rows_in)
        d_dn = by_chip_rows(_weight_grad(act, dh_out, 1, rows=wg_rows))
        return dh_in, d_up, d_dn, dcw[0:3], dg[0], landed

    dh3, d_up1, d_dn1, d_fcw1, d_gf1, _ = ffn_backward(dh4, h3, hu1, hc1, act1, 1, None)
    dh2, dhh1, y_rg, hn_rg, d_vec, d_wa, d_wx, d_gm1, *landed_ffn1 = _rg_bwd(
        dh3, hh1, hs, gates, h2, g_mix[1], w_rg_in, rg_cw, vec, wa, wx, w_rg_out, tm=tm,
        ride=_Scatter([d_up1[1], d_dn1[1]]))
    d_rg_in = _weight_grad(hn_rg, dhh1, N_CHIPS, rows=wg_rows_in)
    d_rg_out = by_chip_rows(_weight_grad(y_rg, dh3, 1, rows=wg_rows))
    early = {"rg_conv_w": d_vec[G_CONV_W:G_CONV_W + 4], "rg_conv_b": d_vec[G_CONV_B:G_CONV_B + 1],
             "rg_b_gate_a": d_vec[G_B_A:G_B_A + 1], "rg_b_gate_x": d_vec[G_B_X:G_B_X + 1],
             "rg_lambda": d_vec[G_LAMBDA:G_LAMBDA + 1], "ffn_conv_w.1": d_fcw1, "norm_mix_g.1": d_gm1[0:1],
             "norm_ffn_g.1": d_gf1[None], "final_norm_g": d_final[0]}
    early_packed = _pack(list(early.values()))
    gate_names = ["rg_w_gate_a", "rg_w_gate_x"]
    to_all = _AllDevices([early_packed, d_wa.reshape(-1, LANES), d_wx.reshape(-1, LANES)])
    dh1, d_up0, d_dn0, d_fcw0, d_gf0, landed = ffn_backward(
        dh2, h1, hu0, hc0, act0, 0, _Both(_Scatter([d_rg_in[1], d_rg_out[1]]), to_all))
    landed_rg, early_by_device, gates_by_device = landed[0:2], landed[2], landed[3:]

    def core_sum(pair, received):
        own = lax.dynamic_index_in_dim(pair[0], chip, 0, keepdims=False)
        return _sum_parts(own, received, rows=_divisor_rows(own.shape[0]))

    early_big = [("rg_w_in", 0), ("rg_w_out", 0), ("ffn_w_up", 1), ("ffn_w_down", 1)]
    early_sum = [core_sum(d_rg_in, landed_rg[0]), core_sum(d_rg_out, landed_rg[1]), core_sum(d_up1, landed_ffn1[0]),
                 core_sum(d_dn1, landed_ffn1[1])]
    grad_x, dhh0, z_sc, hn_sc, d_sccw, d_gm0, d_first, *landed = _sc_bwd(
        dh1, hh0, tokens, first, g_mix[0], w_sc_in, sc_cw, w_sc_out, tm=tm,
        ride=_Both(_Scatter([d_up0[1], d_dn0[1]]), _Swap(early_sum)))
    landed_ffn0, early_other = landed[0:2], landed[2:]
    late = {"meta_tokens": _time_order(d_first, tm)[tm - N_META:], "sc_conv_w": d_sccw[0:3], "ffn_conv_w.0": d_fcw0,
            "norm_mix_g.0": d_gm0[0:1], "norm_ffn_g.0": d_gf0[None]}
    late_packed = _pack(list(late.values()))
    ffn0_big = [("ffn_w_up", 0), ("ffn_w_down", 0)]
    ffn0_sum = [core_sum(d_up0, landed_ffn0[0]), core_sum(d_dn0, landed_ffn0[1])]
    *d_sc_in, ffn0_up_other, ffn0_dn_other, late_by_device = _weight_grad(
        hn_sc, dhh0, N_CHIPS, rows=wg_rows_in, ride=_Both(_Swap(ffn0_sum), _AllDevices([late_packed])))
    *d_sc_out, landed_sc_in = _weight_grad(z_sc, dh1, 1, rows=wg_rows, ride=_Scatter([d_sc_in[1]]))
    d_sc_out = by_chip_rows(d_sc_out)
    landed_sc = [landed_sc_in, *_exchange(_Scatter([d_sc_out[1]]), "scatter_last")]
    grad_x = _time_order(grad_x, tm)[None]

    sc_big = [("sc_w_in", 0), ("sc_w_out", 0)]
    sc_sum = [core_sum(d_sc_in, landed_sc[0]), core_sum(d_sc_out, landed_sc[1])]
    sc_other = _exchange(_Swap(sc_sum), "swap_cores")
    out = {k: {} for k in ("grad", "delta", "m", "v")}
    stacked = {}
    for (n, layer), mine, theirs in zip(sc_big + ffn0_big + early_big, sc_sum + ffn0_sum + early_sum,
                                        [*sc_other, ffn0_up_other, ffn0_dn_other, *early_other]):
        stacked[n] = _adamw(weights[n], m_in[n], v_in[n], [mine, theirs], rows=_divisor_rows(mine.shape[0]), layer=layer,
                            into=stacked.get(n))
    for n, res in stacked.items():
        for k, key in enumerate(("grad", "delta", "m", "v")):
            out[key][n] = res[k]

    summed = {}
    for parts, packed, by_device in ((early, early_packed, early_by_device), (late, late_packed, late_by_device)):
        total = _sum_stack(by_device, rows=packed.shape[0])
        summed.update(zip(parts, _unpack(total, [p.shape for p in parts.values()])))
    for n in ("ffn_conv_w", "norm_mix_g", "norm_ffn_g"):
        summed[n] = jnp.concatenate([summed.pop(n + ".0"), summed.pop(n + ".1")])
    for n, by_device in zip(gate_names, gates_by_device):
        as_rows = lambda a: a.reshape(1, -1, LANES)
        res = _adamw(as_rows(weights[n]), as_rows(m_in[n]), as_rows(v_in[n]), [_sum_stack(by_device, rows=256)], rows=256)
        for k, key in enumerate(("grad", "delta", "m", "v")):
            out[key][n] = res[k].reshape(weights[n].shape)
    replicated = ["norm_mix_g", "norm_ffn_g", "final_norm_g"]
    small_names = small_sharded + replicated
    grads = {}
    for n in small_sharded:
        width = small_2d[n].shape[1]
        grads[n] = lax.dynamic_slice_in_dim(summed[n], chip * width, width, axis=1).reshape(weights[n].shape)
    for n in replicated:
        grads[n] = summed[n].reshape(weights[n].shape)
    shapes = [weights[n].shape for n in small_names]
    packed_w = _pack([weights[n] for n in small_names])
    res = _adamw(packed_w[None], _pack([m_in[n] for n in small_names])[None], _pack([v_in[n] for n in small_names])[None],
                 [_pack([grads[n] for n in small_names])], rows=packed_w.shape[0])
    for k, key in enumerate(("grad", "delta", "m", "v")):
        out[key].update(dict(zip(small_names, _unpack(res[k][0], shapes))))

    return (loss, grad_x, *[out["grad"][n] for n in names], *[out["delta"][n] for n in names],
            *[out["m"][n] for n in names], *[out["v"][n] for n in names])
```

```python
import functools

import jax
import jax.numpy as jnp
from jax import lax
from jax.experimental import pallas as pl
from jax.experimental.pallas import tpu as pltpu

F32 = jnp.float32
MXU_DT = jnp.bfloat16
ACT_DT = jnp.bfloat16
WIRE_DT = jnp.bfloat16
MESH_ID = pl.DeviceIdType.MESH

N_META = 16
RMS_EPS = 1e-6
RG_C = 8.0
ADAM_LR, ADAM_B1, ADAM_B2, ADAM_EPS, ADAM_WD, ADAM_STEP = 0.001, 0.9, 0.999, 1e-08, 0.01, 10
N_CHIPS = 4
VMEM_LIMIT = 60 * 1024 * 1024
F32_ROWS = 8
ACT_ROWS = 16
LANES = 128


def _row_tile(seq):
    for tm in (256, 128, 64, 32, 16):
        if seq % tm == 0:
            return tm
    raise ValueError(f"sequence length {seq} is not a multiple of 16")


def _params(n_axes=1, **kw):
    return pltpu.CompilerParams(dimension_semantics=("arbitrary",) * n_axes, vmem_limit_bytes=VMEM_LIMIT, **kw)


def _const(shape):
    return pl.BlockSpec(shape, lambda *_: (0,) * len(shape), pipeline_mode=pl.Buffered(1))


def _dot(a, b):
    return jnp.dot(a, b, preferred_element_type=F32)


def _dot_nt(a, b):
    return lax.dot_general(a, b, (((1,), (1,)), ((), ())), preferred_element_type=F32)


def _dot_tn(a, b):
    return lax.dot_general(a, b, (((0,), (0,)), ((), ())), preferred_element_type=F32)


def _sigmoid(x):
    return 0.5 + 0.5 * jnp.tanh(0.5 * x)


def _rms(h, g):
    rstd = lax.rsqrt(jnp.mean(h * h, axis=-1, keepdims=True) + RMS_EPS)
    xhat = h * rstd
    return xhat * g, xhat, rstd


def _rms_bwd(dhn, xhat, rstd, g):
    dx = dhn * g
    return rstd * (dx - xhat * jnp.mean(dx * xhat, axis=-1, keepdims=True))


def _gelu(x):
    k = 0.7978845608028654
    t = jnp.tanh(k * (x + 0.044715 * x * x * x))
    return 0.5 * x * (1.0 + t), t


def _gelu_grad(x, t):
    k = 0.7978845608028654
    return 0.5 * (1.0 + t) + 0.5 * x * (1.0 - t * t) * k * (1.0 + 3 * 0.044715 * x * x)


def _softplus(x):
    e = jnp.exp(-jnp.abs(x))
    return jnp.maximum(x, 0.0) + jnp.where(e < 1e-4, e - 0.5 * e * e, jnp.log(1.0 + e))


def _expm1_neg(z, exp_z):
    series = z * (1.0 + z * (0.5 + z * (1.0 / 6)))
    return jnp.where(z > -0.02, series, exp_z - 1.0)


def _tile_order(a, tm):
    return a.reshape(-1, F32_ROWS, tm // F32_ROWS, a.shape[-1]).swapaxes(1, 2).reshape(a.shape)


def _time_order(a, tm):
    return a.reshape(-1, tm // F32_ROWS, F32_ROWS, a.shape[-1]).swapaxes(1, 2).reshape(a.shape)


def _valid_rows(tile, tm):
    row = lax.broadcasted_iota(jnp.int32, (tm, 1), 0)
    time = (row & (F32_ROWS - 1)) * (tm // F32_ROWS) + (row >> 3) + tile * tm
    return time >= tm - N_META


def _sublane():
    return lax.broadcasted_iota(jnp.int32, (F32_ROWS, 1), 0)


def _past_rows(width):
    return (width - 1) * F32_ROWS


def _halo_block(past, tm, nt):
    rows = -(-past // ACT_ROWS) * ACT_ROWS
    return rows, lambda i: (jnp.maximum((nt - 1 - i) * (tm // rows) - 1, 0), 0)


def _link_past(buf, cols, width, tm):
    past = _past_rows(width)
    for k in range(1, width):
        rows = pl.ds(past - F32_ROWS * k, F32_ROWS)
        before = pltpu.roll(buf[rows, cols], 1, 0)
        mine = pltpu.roll(buf[pl.ds(past + tm - F32_ROWS * k, F32_ROWS), cols], 1, 0)
        buf[rows, cols] = jnp.where(_sublane() == 0, before, mine)


def _link_future(buf, cols, width, tm):
    for k in range(1, width):
        rows = pl.ds(tm + F32_ROWS * (k - 1), F32_ROWS)
        after = pltpu.roll(buf[rows, cols], F32_ROWS - 1, 0)
        mine = pltpu.roll(buf[pl.ds(F32_ROWS * (k - 1), F32_ROWS), cols], F32_ROWS - 1, 0)
        buf[rows, cols] = jnp.where(_sublane() == F32_ROWS - 1, after, mine)


def _conv_taps(buf, cols, width, tm):
    return [buf[pl.ds(F32_ROWS * k, tm), cols] for k in range(width)]


def _conv_back(buf, cw_ref, cols, width, tm):
    return sum(cw_ref[k:k + 1, cols] * buf[pl.ds(F32_ROWS * (width - 1 - k), tm), cols] for k in range(width))


ANY = pl.BlockSpec(memory_space=pl.ANY)


def _place():
    return lax.axis_index("x"), lax.axis_index("y"), lax.axis_index("c")


def _other_chips(x, y):
    return [(1 - x, y), (x, 1 - y), (1 - x, 1 - y)]


class _Gather:
    def __init__(self, shards):
        nk = len(shards)
        self.arrays = list(shards)
        self.out_shape = [jax.ShapeDtypeStruct((N_CHIPS,) + s.shape, s.dtype) for s in shards]
        self.scratch = [pltpu.SemaphoreType.DMA((nk, 3)), pltpu.SemaphoreType.DMA((nk, 3)), pltpu.SemaphoreType.DMA((nk,))]

    def run(self, ins, outs, sems, start):
        send_sems, recv_sems, local_sems = sems
        x, y, c = _place()
        mine = 2 * x + y
        for k in range(len(ins)):
            local = pltpu.make_async_copy(ins[k], outs[k].at[mine], local_sems.at[k])
            local.start() if start else local.wait()
            for j, (px, py) in enumerate(_other_chips(x, y)):
                sems_kj = dict(send_sem=send_sems.at[k, j], recv_sem=recv_sems.at[k, j], device_id=(px, py, c),
                               device_id_type=MESH_ID)
                send = pltpu.make_async_remote_copy(src_ref=ins[k], dst_ref=outs[k].at[mine], **sems_kj)
                if start:
                    send.start()
                else:
                    pltpu.make_async_remote_copy(src_ref=ins[k], dst_ref=outs[k].at[2 * px + py], **sems_kj).wait_recv()
                    send.wait_send()


class _GatherHalves:
    def __init__(self, shards):
        nk = len(shards)
        self.arrays = list(shards)
        self.out_shape = [jax.ShapeDtypeStruct((N_CHIPS,) + s.shape, s.dtype) for s in shards]
        self.scratch = [pltpu.SemaphoreType.DMA((nk, 3)) for _ in range(4)] + [pltpu.SemaphoreType.DMA((nk,))]

    def run(self, ins, outs, sems, start):
        far_send, far_recv, near_send, near_recv, local_sems = sems
        x, y, c = _place()
        mine = 2 * x + y
        for phase in ((0,) if start else (1, 2)):
            for k in range(len(ins)):
                half = ins[k].shape[0] // 2
                my_half = pl.ds(pl.multiple_of(c * half, ACT_ROWS), half)
                other_half = pl.ds(pl.multiple_of((1 - c) * half, ACT_ROWS), half)
                if phase != 1:
                    local = pltpu.make_async_copy(ins[k], outs[k].at[mine], local_sems.at[k])
                    local.start() if phase == 0 else local.wait()
                for j, (px, py) in enumerate(_other_chips(x, y)):
                    theirs = 2 * px + py
                    far = dict(send_sem=far_send.at[k, j], recv_sem=far_recv.at[k, j], device_id=(px, py, c),
                               device_id_type=MESH_ID)
                    near = dict(send_sem=near_send.at[k, j], recv_sem=near_recv.at[k, j], device_id=(x, y, 1 - c),
                                device_id_type=MESH_ID)
                    landed = outs[k].at[theirs, my_half]
                    send = lambda: pltpu.make_async_remote_copy(src_ref=ins[k].at[my_half], dst_ref=outs[k].at[mine, my_half], **far)
                    pass_on = lambda: pltpu.make_async_remote_copy(src_ref=landed, dst_ref=landed, **near)
                    if phase == 0:
                        send().start()
                    elif phase == 1:
                        pltpu.make_async_remote_copy(src_ref=ins[k].at[my_half], dst_ref=landed, **far).wait_recv()
                        pass_on().start()
                    else:
                        pltpu.make_async_remote_copy(src_ref=landed, dst_ref=outs[k].at[theirs, other_half], **near).wait_recv()
                        pass_on().wait_send()
                        send().wait_send()


class _Scatter:
    def __init__(self, parts):
        nk = len(parts)
        self.arrays = list(parts)
        self.out_shape = [jax.ShapeDtypeStruct((3,) + p.shape[1:], p.dtype) for p in parts]
        self.scratch = [pltpu.SemaphoreType.DMA((nk, 3)), pltpu.SemaphoreType.DMA((nk, 3))]

    def run(self, ins, outs, sems, start):
        send_sems, recv_sems = sems
        x, y, c = _place()
        for k in range(len(ins)):
            for j, (px, py) in enumerate(_other_chips(x, y)):
                send = pltpu.make_async_remote_copy(
                    src_ref=ins[k].at[2 * px + py], dst_ref=outs[k].at[j], send_sem=send_sems.at[k, j],
                    recv_sem=recv_sems.at[k, j], device_id=(px, py, c), device_id_type=MESH_ID)
                if start:
                    send.start()
                else:
                    send.wait_recv()
                    send.wait_send()


def _exchange(ride, name):
    n_in, n_out = len(ride.arrays), len(ride.out_shape)

    def body(*refs):
        ride.run(refs[:n_in], refs[n_in:n_in + n_out], refs[n_in + n_out:], start=True)
        ride.run(refs[:n_in], refs[n_in:n_in + n_out], refs[n_in + n_out:], start=False)

    return pl.pallas_call(body, name=name, in_specs=[ANY] * n_in, out_specs=[ANY] * n_out, out_shape=ride.out_shape,
                          scratch_shapes=ride.scratch)(*ride.arrays)


def _launch(body, operands, *, name, grid, in_specs, out_specs, out_shape, scratch_shapes=(), ride=None):
    common = dict(name=name, grid=grid, compiler_params=_params(len(grid)))
    if ride is None:
        return pl.pallas_call(body, in_specs=in_specs, out_specs=out_specs, out_shape=out_shape,
                              scratch_shapes=list(scratch_shapes), **common)(*operands)
    n_in, n_out, n_scr = len(operands), len(out_shape), len(scratch_shapes)
    r_in, r_out = len(ride.arrays), len(ride.out_shape)

    def riding(*refs):
        ins, refs = refs[:n_in], refs[n_in:]
        r_ins, refs = refs[:r_in], refs[r_in:]
        outs, refs = refs[:n_out], refs[n_out:]
        r_outs, refs = refs[:r_out], refs[r_out:]
        scr, r_sems = refs[:n_scr], refs[n_scr:]
        step = [pl.program_id(axis) for axis in range(len(grid))]
        first = functools.reduce(jnp.logical_and, [s == 0 for s in step])
        last = functools.reduce(jnp.logical_and, [s == size - 1 for s, size in zip(step, grid)])

        @pl.when(first)
        def _():
            ride.run(r_ins, r_outs, r_sems, start=True)

        body(*ins, *outs, *scr)

        @pl.when(last)
        def _():
            ride.run(r_ins, r_outs, r_sems, start=False)

    return pl.pallas_call(
        riding, in_specs=list(in_specs) + [ANY] * r_in, out_specs=list(out_specs) + [ANY] * r_out,
        out_shape=list(out_shape) + ride.out_shape, scratch_shapes=list(scratch_shapes) + ride.scratch, **common,
    )(*operands, *ride.arrays)


def _sc_fwd(x, first, g, w_in, cw, w_out, *, tm, ride=None):
    seq, d = x.shape
    nt = seq // tm + 1
    nq, _, n = w_in.shape
    width = cw.shape[0]
    past = _past_rows(width)

    def body(x_ref, first_ref, g_ref, win_ref, cw_ref, wout_ref, h1_ref, hh_ref, hh_scr, cbuf):
        i = pl.program_id(0)

        @pl.when(i == 0)
        def _():
            cbuf[pl.ds(0, past), :] = jnp.zeros((past, d), F32)

        h = jnp.where(i == 0, first_ref[...], x_ref[...])
        hn = _rms(h, g_ref[...])[0].astype(MXU_DT)
        for q in range(nq):
            hh_scr[:, q * n:(q + 1) * n] = _dot(hn, win_ref[q])
        hh_ref[...] = hh_scr[...].astype(hh_ref.dtype)
        b = hh_scr[:, 0:d]
        cbuf[pl.ds(past, tm), :] = hh_scr[:, d:2 * d] * hh_scr[:, 2 * d:3 * d]
        last = cbuf[pl.ds(tm, past), :]
        _link_past(cbuf, slice(None), width, tm)
        u = sum(cw_ref[k:k + 1, :] * tap for k, tap in enumerate(_conv_taps(cbuf, slice(None), width, tm)))
        cbuf[pl.ds(0, past), :] = last
        h1_ref[...] = h + _dot((b * u).astype(MXU_DT), wout_ref[...])

    return _launch(
        body, [x, first, g, w_in, cw, w_out], name="sc_fwd", grid=(nt,),
        in_specs=[pl.BlockSpec((tm, d), lambda i: (jnp.maximum(i - 1, 0), 0)), _const((tm, d)), _const((1, d)),
                  _const(w_in.shape), _const(cw.shape), _const(w_out.shape)],
        out_specs=[pl.BlockSpec((tm, d), lambda i: (i, 0)), pl.BlockSpec((tm, nq * n), lambda i: (i, 0))],
        out_shape=[jax.ShapeDtypeStruct((nt * tm, d), F32), jax.ShapeDtypeStruct((nt * tm, nq * n), ACT_DT)],
        scratch_shapes=[pltpu.VMEM((tm, nq * n), F32), pltpu.VMEM((past + tm, d), F32)],
        ride=ride,
    )


def _sc_bwd(dh, hh, x, first, g, w_in, cw, w_out, *, tm, ride=None):
    t_len, d = dh.shape
    nt = t_len // tm
    nq, _, n = w_in.shape
    width = cw.shape[0]
    past = _past_rows(width)
    halo_rows, halo_index = _halo_block(past, tm, nt)

    def body(dh_ref, hh_ref, hhp_ref, x_ref, first_ref, g_ref, win_ref, cw_ref, wout_ref,
             dx_ref, dhh_ref, z_ref, hn_ref, dcw_ref, dg_ref, dfirst_ref, cbuf, dbuf):
        i = pl.program_id(0)
        r = nt - 1 - i

        @pl.when(i == 0)
        def _():
            dbuf[pl.ds(tm, past), :] = jnp.zeros((past, d), F32)
            dcw_ref[...] = jnp.zeros_like(dcw_ref)
            dg_ref[...] = jnp.zeros_like(dg_ref)

        dh_out = dh_ref[...]
        b = hh_ref[:, 0:d].astype(F32)
        c = hh_ref[:, d:2 * d].astype(F32)
        v = hh_ref[:, 2 * d:3 * d].astype(F32)
        prev = hhp_ref[...].astype(F32)[halo_rows - past:, :]
        cbuf[pl.ds(0, past), :] = jnp.where(r > 0, prev[:, d:2 * d] * prev[:, 2 * d:3 * d], 0.0)
        cbuf[pl.ds(past, tm), :] = c * v
        _link_past(cbuf, slice(None), width, tm)
        taps = _conv_taps(cbuf, slice(None), width, tm)
        u = sum(cw_ref[k:k + 1, :] * taps[k] for k in range(width))
        z_ref[...] = (b * u).astype(z_ref.dtype)
        dz = _dot_nt(dh_out.astype(MXU_DT), wout_ref[...])
        d_b = (dz * u).astype(dhh_ref.dtype)
        dhh_ref[:, 0:d] = d_b
        parts = [_dot_nt(d_b[:, 0:n], win_ref[0])]
        du = dz * b
        for k in range(width):
            dcw_ref[k:k + 1, :] += jnp.sum(taps[k] * du, axis=0, keepdims=True)
        dbuf[pl.ds(0, tm), :] = du
        _link_future(dbuf, slice(None), width, tm)
        dcv = _conv_back(dbuf, cw_ref, slice(None), width, tm)
        dbuf[pl.ds(tm, past), :] = dbuf[pl.ds(0, past), :]
        d_c, d_v = (dcv * v).astype(dhh_ref.dtype), (dcv * c).astype(dhh_ref.dtype)
        dhh_ref[:, d:2 * d] = d_c
        dhh_ref[:, 2 * d:3 * d] = d_v
        rest = jnp.concatenate([d_b[:, n:], d_c, d_v], axis=1)
        parts += [_dot_nt(rest[:, (q - 1) * n:q * n], win_ref[q]) for q in range(1, nq)]
        dhn = functools.reduce(lambda a, b: a + b, parts)
        h_in = jnp.where(r == 0, first_ref[...], x_ref[...])
        dh_in = _norm_bwd_tile(dhn, h_in, dh_out, g_ref[...], _valid_rows(r, tm), hn_ref, dg_ref)

        @pl.when(r == 0)
        def _():
            dfirst_ref[...] = dh_in

        @pl.when(r > 0)
        def _():
            dx_ref[...] = dh_in

    rev = lambda i: (nt - 1 - i, 0)
    rev_x = lambda i: (jnp.maximum(nt - 2 - i, 0), 0)
    return _launch(
        body, [dh, hh, hh, x, first, g, w_in, cw, w_out], name="sc_bwd", grid=(nt,),
        in_specs=[pl.BlockSpec((tm, d), rev), pl.BlockSpec((tm, 3 * d), rev), pl.BlockSpec((halo_rows, 3 * d), halo_index),
                  pl.BlockSpec((tm, d), rev_x), _const((tm, d)), _const((1, d)), _const(w_in.shape), _const(cw.shape),
                  _const(w_out.shape)],
        out_specs=[pl.BlockSpec((tm, d), rev_x), pl.BlockSpec((tm, 3 * d), rev), pl.BlockSpec((tm, d), rev),
                   pl.BlockSpec((tm, d), rev), _const((F32_ROWS, d)), _const((F32_ROWS, d)), _const((tm, d))],
        out_shape=[jax.ShapeDtypeStruct((t_len - tm, d), F32), jax.ShapeDtypeStruct((t_len, 3 * d), ACT_DT),
                   jax.ShapeDtypeStruct((t_len, d), ACT_DT), jax.ShapeDtypeStruct((t_len, d), ACT_DT),
                   jax.ShapeDtypeStruct((F32_ROWS, d), F32), jax.ShapeDtypeStruct((F32_ROWS, d), F32),
                   jax.ShapeDtypeStruct((tm, d), F32)],
        scratch_shapes=[pltpu.VMEM((past + tm, d), F32), pltpu.VMEM((tm + past, d), F32)],
        ride=ride,
    )


def _ffn_fwd(h, g, w_up, cw, w_down, *, tm, ride=None, loss=None):
    t_len, d = h.shape
    nt = t_len // tm
    nq, _, n = w_up.shape
    width = cw.shape[0]
    past = _past_rows(width)

    def body(h_ref, g_ref, wup_ref, cw_ref, wdn_ref, *rest):
        if loss is None:
            out_ref, hu_ref, hc_ref, act_ref, ubuf, tail = rest
        else:
            t_ref, gf_ref, out_ref, hu_ref, hc_ref, act_ref, sq_ref, dgf_ref, ubuf, tail = rest
        i = pl.program_id(0)

        @pl.when(i == 0)
        def _():
            tail[...] = jnp.zeros_like(tail)

        h_in = h_ref[...]
        hn = _rms(h_in, g_ref[...])[0].astype(MXU_DT)
        ubuf[pl.ds(0, past), :] = tail[...]
        for q in range(nq):
            ubuf[pl.ds(past, tm), q * n:(q + 1) * n] = _dot(hn, wup_ref[q])
        hu_ref[...] = ubuf[pl.ds(past, tm), :].astype(hu_ref.dtype)
        tail[...] = ubuf[pl.ds(tm, past), :]
        _link_past(ubuf, slice(None), width, tm)
        acc = h_in
        for j in range(nq // 2):
            gcol, vcol = slice(j * n, (j + 1) * n), slice((nq // 2 + j) * n, (nq // 2 + j + 1) * n)
            conv = lambda cols: sum(cw_ref[k:k + 1, cols] * tap for k, tap in enumerate(_conv_taps(ubuf, cols, width, tm)))
            gj, vj = conv(gcol), conv(vcol)
            hc_ref[:, gcol] = gj.astype(hc_ref.dtype)
            hc_ref[:, vcol] = vj.astype(hc_ref.dtype)
            act = (gj * _sigmoid(gj) * vj).astype(MXU_DT)
            act_ref[:, gcol] = act.astype(act_ref.dtype)
            acc = acc + _dot(act, wdn_ref[j * n:(j + 1) * n, :])
        if loss is None:
            out_ref[...] = acc
            return

        @pl.when(i == 0)
        def _():
            sq_ref[...] = jnp.zeros_like(sq_ref)
            dgf_ref[...] = jnp.zeros_like(dgf_ref)

        gain = gf_ref[...]
        out, xhat, rstd = _rms(acc, gain)
        err = jnp.where(i > 0, out - t_ref[...], 0.0)
        sq_ref[0:1, :] += jnp.sum(err * err, axis=0, keepdims=True)
        dout = err * (1.0 / d)
        dgf_ref[0:1, :] += jnp.sum(dout * xhat, axis=0, keepdims=True)
        out_ref[...] = _rms_bwd(dout, xhat, rstd, gain)

    row = lambda i: (i, 0)
    stat = jax.ShapeDtypeStruct((F32_ROWS, d), F32)
    return _launch(
        body, [h, g, w_up, cw, w_down] + list(loss or ()), name="ffn_fwd", grid=(nt,),
        in_specs=[pl.BlockSpec((tm, d), row), _const((1, d)), _const(w_up.shape), _const(cw.shape), _const(w_down.shape)]
        + ([pl.BlockSpec((tm, d), lambda i: (jnp.maximum(i - 1, 0), 0)), _const((1, d))] if loss else []),
        out_specs=[pl.BlockSpec((tm, d), row), pl.BlockSpec((tm, nq * n), row), pl.BlockSpec((tm, nq * n), row),
                   pl.BlockSpec((tm, nq // 2 * n), row)] + ([_const(stat.shape)] * 2 if loss else []),
        out_shape=[jax.ShapeDtypeStruct((t_len, d), F32), jax.ShapeDtypeStruct((t_len, nq * n), ACT_DT),
                   jax.ShapeDtypeStruct((t_len, nq * n), ACT_DT), jax.ShapeDtypeStruct((t_len, nq // 2 * n), ACT_DT)]
        + ([stat, stat] if loss else []),
        scratch_shapes=[pltpu.VMEM((past + tm, nq * n), F32), pltpu.VMEM((past, nq * n), F32)],
        ride=ride,
    )


def _norm_bwd_tile(dhn, h_in, dh, gain, valid, hn_ref, dg_ref):
    hn, xhat, rstd = _rms(h_in, gain)
    hn_ref[...] = hn.astype(hn_ref.dtype)
    dg_ref[0:1, :] += jnp.sum(dhn * xhat, axis=0, keepdims=True)
    return jnp.where(valid, dh + _rms_bwd(dhn, xhat, rstd, gain), 0.0)


def _ffn_bwd(dh, hu, hc, h, g, w_up, cw, w_down, *, tm, ride=None):
    t_len, d = dh.shape
    nt = t_len // tm
    ff = hu.shape[1]
    n = ff // 4
    width = cw.shape[0]
    past = _past_rows(width)
    halo_rows, halo_index = _halo_block(past, tm, nt)

    def body(dh_ref, hu_ref, hup_ref, hc_ref, h_ref, g_ref, wup_ref, cw_ref, wdn_ref,
             dhin_ref, dhu_ref, hn_ref, dcw_ref, dg_ref, ubuf, dbuf, head):
        i = pl.program_id(0)
        r = nt - 1 - i

        @pl.when(i == 0)
        def _():
            head[...] = jnp.zeros_like(head)
            dcw_ref[...] = jnp.zeros_like(dcw_ref)
            dg_ref[...] = jnp.zeros_like(dg_ref)

        dh_out = dh_ref[...]
        dhb = dh_out.astype(MXU_DT)
        dhn_parts = []
        d_act = [_dot_nt(dhb, wdn_ref[j * n:(j + 1) * n, :]) for j in range(2)]
        for j in range(2):
            mine = slice(0, n), slice(n, 2 * n)
            full = slice(j * n, (j + 1) * n), slice((2 + j) * n, (3 + j) * n)
            for here, there in zip(mine, full):
                prev = hup_ref[:, there].astype(F32)[halo_rows - past:, :]
                ubuf[pl.ds(0, past), here] = jnp.where(r > 0, prev, 0.0)
                ubuf[pl.ds(past, tm), here] = hu_ref[:, there].astype(F32)
                dbuf[pl.ds(tm, past), here] = head[:, there]
            _link_past(ubuf, slice(None), width, tm)
            gj, vj = hc_ref[:, full[0]].astype(F32), hc_ref[:, full[1]].astype(F32)
            sg = _sigmoid(gj)
            s = gj * sg
            da = d_act[j]
            dbuf[pl.ds(0, tm), mine[1]] = da * s
            dbuf[pl.ds(0, tm), mine[0]] = da * vj * (sg * (1.0 + gj * (1.0 - sg)))
            for here, there in zip(mine, full):
                head[:, there] = dbuf[pl.ds(0, past), here]
            _link_future(dbuf, slice(None), width, tm)
            for here, there in zip(mine, full):
                dy = dbuf[pl.ds(0, tm), here]
                for k, tap in enumerate(_conv_taps(ubuf, here, width, tm)):
                    dcw_ref[k:k + 1, there] += jnp.sum(tap * dy, axis=0, keepdims=True)
                dhu = sum(cw_ref[k:k + 1, there] * dbuf[pl.ds(F32_ROWS * (width - 1 - k), tm), here]
                          for k in range(width)).astype(dhu_ref.dtype)
                dhu_ref[:, there] = dhu
                dhn_parts.append(_dot_nt(dhu, wup_ref[there.start // n]))
        dhn = (dhn_parts[0] + dhn_parts[1]) + (dhn_parts[2] + dhn_parts[3])
        dhin_ref[...] = _norm_bwd_tile(dhn, h_ref[...], dh_out, g_ref[...], _valid_rows(r, tm), hn_ref, dg_ref)

    rev = lambda i: (nt - 1 - i, 0)
    return _launch(
        body, [dh, hu, hu, hc, h, g, w_up, cw, w_down], name="ffn_bwd", grid=(nt,),
        in_specs=[pl.BlockSpec((tm, d), rev), pl.BlockSpec((tm, ff), rev), pl.BlockSpec((halo_rows, ff), halo_index),
                  pl.BlockSpec((tm, ff), rev), pl.BlockSpec((tm, d), rev), _const((1, d)), _const(w_up.shape), _const(cw.shape), _const(w_down.shape)],
        out_specs=[pl.BlockSpec((tm, d), rev), pl.BlockSpec((tm, ff), rev),
                   pl.BlockSpec((tm, d), rev), _const((F32_ROWS, ff)), _const((F32_ROWS, d))],
        out_shape=[jax.ShapeDtypeStruct((t_len, d), F32),
                   jax.ShapeDtypeStruct((t_len, ff), ACT_DT), jax.ShapeDtypeStruct((t_len, d), ACT_DT),
                   jax.ShapeDtypeStruct((F32_ROWS, ff), F32), jax.ShapeDtypeStruct((F32_ROWS, d), F32)],
        scratch_shapes=[pltpu.VMEM((past + tm, 2 * n), F32), pltpu.VMEM((tm + past, 2 * n), F32), pltpu.VMEM((past, ff), F32)],
        ride=ride,
    )


V_CONV_B, V_B_A, V_B_X, V_LAMBDA = 0, 1, 2, 3
G_CONV_W, G_CONV_B, G_B_A, G_B_X, G_LAMBDA = 0, 4, 5, 6, 7


def _scan(a_ref, b_ref, edge, tm, reverse):
    nj = tm // F32_ROWS
    order = range(nj - 1, -1, -1) if reverse else range(nj)
    slab = lambda ref, j: ref[pl.ds(F32_ROWS * j, F32_ROWS), :]
    a_run = b_run = None
    for j in order:
        a_j, b_j = slab(a_ref, j), slab(b_ref, j)
        if a_run is not None:
            b_j = b_j + a_j * b_run
            a_j = a_j * a_run
            b_ref[pl.ds(F32_ROWS * j, F32_ROWS), :] = b_j
            a_ref[pl.ds(F32_ROWS * j, F32_ROWS), :] = a_j
        a_run, b_run = a_j, b_j
    sub = _sublane()
    shift = 1
    while shift < F32_ROWS:
        amount = F32_ROWS - shift if reverse else shift
        keep = (sub < F32_ROWS - shift) if reverse else (sub >= shift)
        b_run = jnp.where(keep, b_run + a_run * pltpu.roll(b_run, amount, 0), b_run)
        a_run = jnp.where(keep, a_run * pltpu.roll(a_run, amount, 0), a_run)
        shift *= 2
    outer = edge[0:1, :] if reverse else edge[F32_ROWS - 1:F32_ROWS, :]
    ends = b_run + a_run * outer
    if reverse:
        carry = jnp.where(sub == F32_ROWS - 1, outer, pltpu.roll(ends, F32_ROWS - 1, 0))
    else:
        carry = jnp.where(sub == 0, outer, pltpu.roll(ends, 1, 0))
    for j in range(nj):
        b_ref[pl.ds(F32_ROWS * j, F32_ROWS), :] = slab(b_ref, j) + slab(a_ref, j) * carry
    return slab(b_ref, 0 if reverse else nj - 1)


def _rg_gates(u, vec_ref, wa_ref, wx_ref, pre_scr, nb, bd):
    ub = u.astype(MXU_DT)
    for k in range(nb):
        blk = slice(k * bd, (k + 1) * bd)
        pre_scr[0, :, blk] = _dot(ub[:, blk], wa_ref[k])
        pre_scr[1, :, blk] = _dot(ub[:, blk], wx_ref[k])
    r_gate = _sigmoid(pre_scr[0] + vec_ref[V_B_A:V_B_A + 1, :])
    i_gate = _sigmoid(pre_scr[1] + vec_ref[V_B_X:V_B_X + 1, :])
    return r_gate, i_gate


def _rg_decay(r_gate, vec_ref):
    sp = _softplus(-vec_ref[V_LAMBDA:V_LAMBDA + 1, :])
    log_a = -RG_C * r_gate * sp
    a = jnp.exp(log_a)
    one_minus_a2 = jnp.maximum(-_expm1_neg(2.0 * log_a, a * a), 1e-30)
    inv_mult = lax.rsqrt(one_minus_a2)
    return a, one_minus_a2 * inv_mult, inv_mult, sp


def _rg_fwd(h, g, w_in, cw, vec, wa, wx, w_out, *, tm):
    t_len, d = h.shape
    nt = t_len // tm
    nq, _, n = w_in.shape
    dr = 2 * n
    width = cw.shape[0]
    past = _past_rows(width)
    nb, bd, _ = wa.shape

    def body(h_ref, g_ref, win_ref, cw_ref, vec_ref, wa_ref, wx_ref, wout_ref, out_ref, hh_ref, hs_ref, gates_ref,
             gbuf, rbuf, pre_scr, tail, edge):
        i = pl.program_id(0)

        @pl.when(i == 0)
        def _():
            tail[...] = jnp.zeros_like(tail)
            edge[...] = jnp.zeros_like(edge)

        h_in = h_ref[...]
        hn = _rms(h_in, g_ref[...])[0].astype(MXU_DT)
        rbuf[pl.ds(0, past), :] = tail[...]
        for q in range(2):
            gbuf[:, q * n:(q + 1) * n] = _dot(hn, win_ref[q])
            rbuf[pl.ds(past, tm), q * n:(q + 1) * n] = _dot(hn, win_ref[2 + q])
        hh_ref[:, 0:dr] = gbuf[...].astype(hh_ref.dtype)
        hh_ref[:, dr:2 * dr] = rbuf[pl.ds(past, tm), :].astype(hh_ref.dtype)
        tail[...] = rbuf[pl.ds(tm, past), :]
        _link_past(rbuf, slice(None), width, tm)
        taps = _conv_taps(rbuf, slice(None), width, tm)
        u = sum(cw_ref[k:k + 1, :] * taps[k] for k in range(width)) + vec_ref[V_CONV_B:V_CONV_B + 1, :]
        r_gate, i_gate = _rg_gates(u, vec_ref, wa_ref, wx_ref, pre_scr, nb, bd)
        for k, kept in enumerate((u, r_gate, i_gate)):
            gates_ref[:, k * dr:(k + 1) * dr] = kept.astype(gates_ref.dtype)
        a, mult, _, _ = _rg_decay(r_gate, vec_ref)
        hs_ref[:, dr:2 * dr] = a
        hs_ref[:, 2 * dr:3 * dr] = mult
        pre_scr[0] = a
        pre_scr[1] = jnp.where(_valid_rows(i, tm), mult * (i_gate * u), 0.0)
        edge[...] = _scan(pre_scr.at[0], pre_scr.at[1], edge[...], tm, reverse=False)
        hs = pre_scr[1]
        hs_ref[:, 0:dr] = hs
        gate, th = _gelu(gbuf[...])
        gates_ref[:, 3 * dr:4 * dr] = th.astype(gates_ref.dtype)
        y = hs * gate
        out_ref[...] = h_in + _dot(y.astype(MXU_DT), wout_ref[...])

    row = lambda i: (i, 0)
    return pl.pallas_call(
        body, name="rg_fwd", grid=(nt,),
        in_specs=[pl.BlockSpec((tm, d), row), _const((1, d)), _const(w_in.shape), _const(cw.shape), _const(vec.shape),
                  _const(wa.shape), _const(wx.shape), _const(w_out.shape)],
        out_specs=[pl.BlockSpec((tm, d), row), pl.BlockSpec((tm, 2 * dr), row), pl.BlockSpec((tm, 3 * dr), row),
                   pl.BlockSpec((tm, 4 * dr), row)],
        out_shape=[jax.ShapeDtypeStruct((t_len, d), F32), jax.ShapeDtypeStruct((t_len, 2 * dr), ACT_DT),
                   jax.ShapeDtypeStruct((t_len, 3 * dr), F32), jax.ShapeDtypeStruct((t_len, 4 * dr), ACT_DT)],
        scratch_shapes=[pltpu.VMEM((tm, dr), F32), pltpu.VMEM((past + tm, dr), F32), pltpu.VMEM((2, tm, dr), F32),
                        pltpu.VMEM((past, dr), F32), pltpu.VMEM((F32_ROWS, dr), F32)],
        compiler_params=_params(),
    )(h, g, w_in, cw, vec, wa, wx, w_out)


def _rg_bwd(dh, hh, hs, gates, h, g, w_in, cw, vec, wa, wx, w_out, *, tm, ride=None):
    t_len, d = dh.shape
    nt = t_len // tm
    dr = hs.shape[1] // 3
    n = w_in.shape[2]
    width = cw.shape[0]
    nb, bd, _ = wa.shape
    past = _past_rows(width)
    halo_rows, halo_index = _halo_block(past, tm, nt)
    one = F32_ROWS

    def body(dh_ref, hh_ref, hhp_ref, hs_ref, hsp_ref, gates_ref, h_ref, g_ref, win_ref, cw_ref, vec_ref, wa_ref, wx_ref, wout_ref,
             dhin_ref, dhh_ref, y_ref, hn_ref, dvec_ref, dwa_ref, dwx_ref, dg_ref, rbuf, dbuf, pre_scr, hbuf, abuf, edge):
        i = pl.program_id(0)
        r = nt - 1 - i

        @pl.when(i == 0)
        def _():
            dbuf[pl.ds(tm, past), :] = jnp.zeros((past, dr), F32)
            abuf[pl.ds(tm, one), :] = jnp.zeros((one, dr), F32)
            edge[...] = jnp.zeros_like(edge)
            dvec_ref[...] = jnp.zeros_like(dvec_ref)
            dwa_ref[...] = jnp.zeros_like(dwa_ref)
            dwx_ref[...] = jnp.zeros_like(dwx_ref)
            dg_ref[...] = jnp.zeros_like(dg_ref)

        dh_out = dh_ref[...]
        gb = hh_ref[:, 0:dr].astype(F32)
        prev = hhp_ref[...].astype(F32)[halo_rows - past:, dr:2 * dr]
        rbuf[pl.ds(0, past), :] = jnp.where(r > 0, prev, 0.0)
        rbuf[pl.ds(past, tm), :] = hh_ref[:, dr:2 * dr].astype(F32)
        _link_past(rbuf, slice(None), width, tm)
        taps = _conv_taps(rbuf, slice(None), width, tm)
        ub = gates_ref[:, 0:dr].astype(MXU_DT)
        u, r_gate, i_gate = (gates_ref[:, k * dr:(k + 1) * dr].astype(F32) for k in range(3))
        hs_t, a, mult = (hs_ref[:, k * dr:(k + 1) * dr] for k in range(3))
        inv_mult = 1.0 / mult
        sp = _softplus(-vec_ref[V_LAMBDA:V_LAMBDA + 1, :])
        hbuf[pl.ds(0, one), :] = jnp.where(r > 0, hsp_ref[...], 0.0)
        hbuf[pl.ds(one, tm), :] = hs_t
        _link_past(hbuf, slice(None), 2, tm)
        h_prev = hbuf[pl.ds(0, tm), :]
        th = gates_ref[:, 3 * dr:4 * dr].astype(F32)
        gate = 0.5 * gb * (1.0 + th)
        y_ref[...] = (hs_t * gate).astype(y_ref.dtype)
        dy = _dot_nt(dh_out.astype(MXU_DT), wout_ref[...])
        d_gb = (dy * hs_t * _gelu_grad(gb, th)).astype(dhh_ref.dtype)
        dhh_ref[:, 0:dr] = d_gb
        dhn = sum(_dot_nt(d_gb[:, q * n:(q + 1) * n], win_ref[q]) for q in range(2))
        abuf[pl.ds(0, tm), :] = a
        _link_future(abuf, slice(None), 2, tm)
        pre_scr[0] = abuf[pl.ds(one, tm), :]
        pre_scr[1] = dy * gate
        edge[...] = _scan(pre_scr.at[0], pre_scr.at[1], edge[...], tm, reverse=True)
        abuf[pl.ds(tm, one), :] = abuf[pl.ds(0, one), :]
        d_hs = pre_scr[1]
        d_b = jnp.where(_valid_rows(r, tm), d_hs, 0.0)
        d_iu = d_b * mult
        d_log_a = d_hs * h_prev * a - d_b * (i_gate * u) * (a * a) * inv_mult
        dvec_ref[G_LAMBDA:G_LAMBDA + 1, :] += jnp.sum(d_log_a * r_gate, axis=0, keepdims=True) * (-RG_C)
        d_pre_r = d_log_a * (-RG_C * sp) * r_gate * (1.0 - r_gate)
        d_pre_i = d_iu * u * i_gate * (1.0 - i_gate)
        dvec_ref[G_B_A:G_B_A + 1, :] += jnp.sum(d_pre_r, axis=0, keepdims=True)
        dvec_ref[G_B_X:G_B_X + 1, :] += jnp.sum(d_pre_i, axis=0, keepdims=True)
        dbuf[pl.ds(0, tm), :] = d_iu * i_gate
        d_pre_r = d_pre_r.astype(MXU_DT)
        d_pre_i = d_pre_i.astype(MXU_DT)
        for k in range(nb):
            blk = slice(k * bd, (k + 1) * bd)
            dwa_ref[k] += _dot_tn(ub[:, blk], d_pre_r[:, blk])
            dwx_ref[k] += _dot_tn(ub[:, blk], d_pre_i[:, blk])
            dbuf[pl.ds(0, tm), blk] += _dot_nt(d_pre_r[:, blk], wa_ref[k]) + _dot_nt(d_pre_i[:, blk], wx_ref[k])
        du = dbuf[pl.ds(0, tm), :]
        dvec_ref[G_CONV_B:G_CONV_B + 1, :] += jnp.sum(du, axis=0, keepdims=True)
        for k in range(width):
            dvec_ref[G_CONV_W + k:G_CONV_W + k + 1, :] += jnp.sum(taps[k] * du, axis=0, keepdims=True)
        _link_future(dbuf, slice(None), width, tm)
        d_rb = _conv_back(dbuf, cw_ref, slice(None), width, tm)
        dbuf[pl.ds(tm, past), :] = dbuf[pl.ds(0, past), :]
        d_rb = d_rb.astype(dhh_ref.dtype)
        dhh_ref[:, dr:2 * dr] = d_rb
        dhn = dhn + sum(_dot_nt(d_rb[:, q * n:(q + 1) * n], win_ref[2 + q]) for q in range(2))
        dhin_ref[...] = _norm_bwd_tile(dhn, h_ref[...], dh_out, g_ref[...], _valid_rows(r, tm), hn_ref, dg_ref)

        @pl.when(i == nt - 1)
        def _():
            lam = vec_ref[V_LAMBDA:V_LAMBDA + 1, :]
            dvec_ref[G_LAMBDA:G_LAMBDA + 1, :] = dvec_ref[G_LAMBDA:G_LAMBDA + 1, :] * (-_sigmoid(-lam))

    rev = lambda i: (nt - 1 - i, 0)
    return _launch(
        body, [dh, hh, hh, hs, hs, gates, h, g, w_in, cw, vec, wa, wx, w_out], name="rg_bwd", grid=(nt,),
        in_specs=[pl.BlockSpec((tm, d), rev), pl.BlockSpec((tm, 2 * dr), rev), pl.BlockSpec((halo_rows, 2 * dr), halo_index),
                  pl.BlockSpec((tm, 3 * dr), rev),
                  pl.BlockSpec((one, dr), lambda i: (jnp.maximum((nt - 1 - i) * (tm // one) - 1, 0), 0)),
                  pl.BlockSpec((tm, 4 * dr), rev), pl.BlockSpec((tm, d), rev), _const((1, d)), _const(w_in.shape),
                  _const(cw.shape), _const(vec.shape), _const(wa.shape), _const(wx.shape), _const(w_out.shape)],
        out_specs=[pl.BlockSpec((tm, d), rev), pl.BlockSpec((tm, 2 * dr), rev), pl.BlockSpec((tm, dr), rev),
                   pl.BlockSpec((tm, d), rev), _const((F32_ROWS, dr)), _const(wa.shape), _const(wx.shape),
                   _const((F32_ROWS, d))],
        out_shape=[jax.ShapeDtypeStruct((t_len, d), F32), jax.ShapeDtypeStruct((t_len, 2 * dr), ACT_DT),
                   jax.ShapeDtypeStruct((t_len, dr), ACT_DT), jax.ShapeDtypeStruct((t_len, d), ACT_DT),
                   jax.ShapeDtypeStruct((F32_ROWS, dr), F32), jax.ShapeDtypeStruct(wa.shape, F32),
                   jax.ShapeDtypeStruct(wx.shape, F32), jax.ShapeDtypeStruct((F32_ROWS, d), F32)],
        scratch_shapes=[pltpu.VMEM((past + tm, dr), F32), pltpu.VMEM((tm + past, dr), F32), pltpu.VMEM((2, tm, dr), F32),
                        pltpu.VMEM((one + tm, dr), F32), pltpu.VMEM((tm + one, dr), F32), pltpu.VMEM((F32_ROWS, dr), F32)],
        ride=ride,
    )


def _weight_grad(a, b, nb, *, rows, ride=None):
    t_len, k_dim = a.shape
    n = b.shape[1] // nb
    nt = t_len // rows

    def body(a_ref, b_ref, out_ref, wire_ref):
        @pl.when(pl.program_id(1) == 0)
        def _():
            out_ref[...] = jnp.zeros_like(out_ref)

        out_ref[0] += _dot_tn(a_ref[...].astype(MXU_DT), b_ref[...].astype(MXU_DT))

        @pl.when(pl.program_id(1) == nt - 1)
        def _():
            wire_ref[...] = out_ref[...].astype(wire_ref.dtype)

    block = pl.BlockSpec((1, k_dim, n), lambda j, i: (j, 0, 0))
    return _launch(
        body, [a, b], name="weight_grad", grid=(nb, nt),
        in_specs=[pl.BlockSpec((rows, k_dim), lambda j, i: (i, 0)), pl.BlockSpec((rows, n), lambda j, i: (i, j))],
        out_specs=[block, block],
        out_shape=[jax.ShapeDtypeStruct((nb, k_dim, n), F32), jax.ShapeDtypeStruct((nb, k_dim, n), WIRE_DT)],
        ride=ride,
    )


def _adamw(w, m, v, parts, *, rows, layer=0, into=None):
    n_layers, n_rows, n_cols = w.shape
    nt = n_rows // rows
    n_parts = len(parts)

    def body(w_ref, m_ref, v_ref, *rest):
        part_refs, (g_ref, d_ref, nm_ref, nv_ref) = rest[:n_parts], rest[-4:]
        w_ref, m_ref, v_ref, g_ref, d_ref, nm_ref, nv_ref = (r.at[0] for r in (w_ref, m_ref, v_ref, g_ref, d_ref, nm_ref, nv_ref))
        grad = part_refs[0][...].astype(F32)
        for p in part_refs[1:]:
            grad = grad + p[...].astype(F32)
        new_m = ADAM_B1 * m_ref[...] + (1.0 - ADAM_B1) * grad
        new_v = ADAM_B2 * v_ref[...] + (1.0 - ADAM_B2) * (grad * grad)
        m_hat = new_m / (1.0 - ADAM_B1 ** ADAM_STEP)
        v_hat = new_v / (1.0 - ADAM_B2 ** ADAM_STEP)
        g_ref[...] = grad
        d_ref[...] = -ADAM_LR * (m_hat / (jnp.sqrt(v_hat) + ADAM_EPS) + ADAM_WD * w_ref[...])
        nm_ref[...] = new_m
        nv_ref[...] = new_v

    spec = pl.BlockSpec((rows, n_cols), lambda i: (i, 0))
    layer_spec = pl.BlockSpec((1, rows, n_cols), lambda i: (layer, i, 0))
    into = list(into or [])
    return pl.pallas_call(
        body, name="adamw", grid=(nt,),
        in_specs=[layer_spec] * 3 + [spec] * n_parts + [ANY] * len(into), out_specs=[layer_spec] * 4,
        out_shape=[jax.ShapeDtypeStruct(w.shape, F32)] * 4,
        input_output_aliases={3 + n_parts + k: k for k in range(len(into))},
        compiler_params=_params(),
    )(w, m, v, *parts, *into)


def _to_wire(layers, *, steps=4, ride=None):
    n_arrays = len(layers)

    def body(*refs):
        for w_ref, out_ref in zip(refs[:n_arrays], refs[n_arrays:]):
            out_ref[...] = w_ref[0].astype(out_ref.dtype)

    def spec(layer):
        return lambda i: (layer, i, 0)

    return _launch(
        body, [w for w, _ in layers], name="to_wire", grid=(steps,),
        in_specs=[pl.BlockSpec((1, w.shape[1] // steps, w.shape[2]), spec(layer)) for w, layer in layers],
        out_specs=[pl.BlockSpec((w.shape[1] // steps, w.shape[2]), lambda i: (i, 0)) for w, _ in layers],
        out_shape=[jax.ShapeDtypeStruct(w.shape[1:], WIRE_DT) for w, _ in layers], ride=ride)


def _sum_stack(stack, *, rows):
    n_stack, n_rows, n_cols = stack.shape

    def body(stack_ref, out_ref):
        acc = stack_ref[0]
        for j in range(1, n_stack):
            acc = acc + stack_ref[j]
        out_ref[...] = acc

    return pl.pallas_call(
        body, name="sum_stack", grid=(n_rows // rows,),
        in_specs=[pl.BlockSpec((n_stack, rows, n_cols), lambda i: (0, i, 0))],
        out_specs=pl.BlockSpec((rows, n_cols), lambda i: (i, 0)),
        out_shape=jax.ShapeDtypeStruct((n_rows, n_cols), F32),
        compiler_params=_params(),
    )(stack)


def _sum_parts(own, recv, *, rows):
    n_rows, n_cols = own.shape
    n_recv = recv.shape[0]

    def body(own_ref, recv_ref, out_ref):
        acc = own_ref[...].astype(F32)
        for j in range(n_recv):
            acc = acc + recv_ref[j].astype(F32)
        out_ref[...] = acc

    return pl.pallas_call(
        body, name="sum_parts", grid=(n_rows // rows,),
        in_specs=[pl.BlockSpec((rows, n_cols), lambda i: (i, 0)), pl.BlockSpec((n_recv, rows, n_cols), lambda i: (0, i, 0))],
        out_specs=pl.BlockSpec((rows, n_cols), lambda i: (i, 0)),
        out_shape=jax.ShapeDtypeStruct(own.shape, F32),
        compiler_params=_params(),
    )(own, recv)


class _Swap:
    def __init__(self, arrays):
        nk = len(arrays)
        self.arrays = list(arrays)
        self.out_shape = [jax.ShapeDtypeStruct(a.shape, a.dtype) for a in arrays]
        self.scratch = [pltpu.SemaphoreType.DMA((nk,)), pltpu.SemaphoreType.DMA((nk,))]

    def run(self, ins, outs, sems, start):
        send_sems, recv_sems = sems
        x, y, c = _place()
        for k in range(len(ins)):
            send = pltpu.make_async_remote_copy(src_ref=ins[k], dst_ref=outs[k], send_sem=send_sems.at[k],
                                                recv_sem=recv_sems.at[k], device_id=(x, y, 1 - c), device_id_type=MESH_ID)
            if start:
                send.start()
            else:
                send.wait_recv()
                send.wait_send()


class _AllDevices:
    def __init__(self, arrays):
        nk = len(arrays)
        self.arrays = list(arrays)
        self.out_shape = [jax.ShapeDtypeStruct((8,) + a.shape, a.dtype) for a in arrays]
        self.scratch = [pltpu.SemaphoreType.DMA((nk, 7)), pltpu.SemaphoreType.DMA((nk, 7)), pltpu.SemaphoreType.DMA((nk,))]

    def run(self, ins, outs, sems, start):
        send_sems, recv_sems, local_sems = sems
        x, y, c = _place()
        mine = 4 * x + 2 * y + c
        for k in range(len(ins)):
            local = pltpu.make_async_copy(ins[k], outs[k].at[mine], local_sems.at[k])
            local.start() if start else local.wait()
            for flip in range(1, 8):
                px, py, pc = x ^ (flip >> 2), y ^ ((flip >> 1) & 1), c ^ (flip & 1)
                sems_f = dict(send_sem=send_sems.at[k, flip - 1], recv_sem=recv_sems.at[k, flip - 1],
                              device_id=(px, py, pc), device_id_type=MESH_ID)
                send = pltpu.make_async_remote_copy(src_ref=ins[k], dst_ref=outs[k].at[mine], **sems_f)
                if start:
                    send.start()
                else:
                    pltpu.make_async_remote_copy(src_ref=ins[k], dst_ref=outs[k].at[4 * px + 2 * py + pc], **sems_f).wait_recv()
                    send.wait_send()


class _Both:
    def __init__(self, first, second):
        self.rides = (first, second)
        self.arrays = first.arrays + second.arrays
        self.out_shape = first.out_shape + second.out_shape
        self.scratch = first.scratch + second.scratch

    def run(self, ins, outs, sems, start):
        for ride in self.rides:
            n_in, n_out, n_sem = len(ride.arrays), len(ride.out_shape), len(ride.scratch)
            ride.run(ins[:n_in], outs[:n_out], sems[:n_sem], start)
            ins, outs, sems = ins[n_in:], outs[n_out:], sems[n_sem:]


def _pack(arrays, pad_rows=F32_ROWS):
    flat = jnp.concatenate([a.reshape(-1).astype(F32) for a in arrays])
    rows = -(-flat.shape[0] // (LANES * pad_rows)) * pad_rows
    return jnp.pad(flat, (0, rows * LANES - flat.shape[0])).reshape(rows, LANES)


def _unpack(packed, shapes):
    flat, out, off = packed.reshape(-1), [], 0
    for s in shapes:
        size = 1
        for dim in s:
            size *= dim
        out.append(flat[off:off + size].reshape(s))
        off += size
    return out


def _divisor_rows(n_rows, most=256):
    best = None
    for r in range(ACT_ROWS, most + 1, ACT_ROWS):
        if n_rows % r == 0:
            best = r
    return best or n_rows


def kernel(x, meta_tokens, norm_mix_g, norm_ffn_g, final_norm_g, sc_w_in, sc_conv_w, sc_w_out, rg_w_in, rg_conv_w, rg_conv_b, rg_w_gate_a, rg_b_gate_a, rg_w_gate_x, rg_b_gate_x, rg_lambda, rg_w_out, ffn_w_up, ffn_conv_w, ffn_w_down, loss_target, m_meta_tokens, m_norm_mix_g, m_norm_ffn_g, m_final_norm_g, m_sc_w_in, m_sc_conv_w, m_sc_w_out, m_rg_w_in, m_rg_conv_w, m_rg_conv_b, m_rg_w_gate_a, m_rg_b_gate_a, m_rg_w_gate_x, m_rg_b_gate_x, m_rg_lambda, m_rg_w_out, m_ffn_w_up, m_ffn_conv_w, m_ffn_w_down, v_meta_tokens, v_norm_mix_g, v_norm_ffn_g, v_final_norm_g, v_sc_w_in, v_sc_conv_w, v_sc_w_out, v_rg_w_in, v_rg_conv_w, v_rg_conv_b, v_rg_w_gate_a, v_rg_b_gate_a, v_rg_w_gate_x, v_rg_b_gate_x, v_rg_lambda, v_rg_w_out, v_ffn_w_up, v_ffn_conv_w, v_ffn_w_down):
    weights = dict(meta_tokens=meta_tokens, norm_mix_g=norm_mix_g, norm_ffn_g=norm_ffn_g, final_norm_g=final_norm_g, sc_w_in=sc_w_in, sc_conv_w=sc_conv_w, sc_w_out=sc_w_out, rg_w_in=rg_w_in, rg_conv_w=rg_conv_w, rg_conv_b=rg_conv_b, rg_w_gate_a=rg_w_gate_a, rg_b_gate_a=rg_b_gate_a, rg_w_gate_x=rg_w_gate_x, rg_b_gate_x=rg_b_gate_x, rg_lambda=rg_lambda, rg_w_out=rg_w_out, ffn_w_up=ffn_w_up, ffn_conv_w=ffn_conv_w, ffn_w_down=ffn_w_down)
    m_in = dict(meta_tokens=m_meta_tokens, norm_mix_g=m_norm_mix_g, norm_ffn_g=m_norm_ffn_g, final_norm_g=m_final_norm_g, sc_w_in=m_sc_w_in, sc_conv_w=m_sc_conv_w, sc_w_out=m_sc_w_out, rg_w_in=m_rg_w_in, rg_conv_w=m_rg_conv_w, rg_conv_b=m_rg_conv_b, rg_w_gate_a=m_rg_w_gate_a, rg_b_gate_a=m_rg_b_gate_a, rg_w_gate_x=m_rg_w_gate_x, rg_b_gate_x=m_rg_b_gate_x, rg_lambda=m_rg_lambda, rg_w_out=m_rg_w_out, ffn_w_up=m_ffn_w_up, ffn_conv_w=m_ffn_conv_w, ffn_w_down=m_ffn_w_down)
    v_in = dict(meta_tokens=v_meta_tokens, norm_mix_g=v_norm_mix_g, norm_ffn_g=v_norm_ffn_g, final_norm_g=v_final_norm_g, sc_w_in=v_sc_w_in, sc_conv_w=v_sc_conv_w, sc_w_out=v_sc_w_out, rg_w_in=v_rg_w_in, rg_conv_w=v_rg_conv_w, rg_conv_b=v_rg_conv_b, rg_w_gate_a=v_rg_w_gate_a, rg_b_gate_a=v_rg_b_gate_a, rg_w_gate_x=v_rg_w_gate_x, rg_b_gate_x=v_rg_b_gate_x, rg_lambda=v_rg_lambda, rg_w_out=v_rg_w_out, ffn_w_up=v_ffn_w_up, ffn_conv_w=v_ffn_conv_w, ffn_w_down=v_ffn_w_down)
    names = list(weights)

    seq, d = x.shape[1:]
    tm = _row_tile(seq)
    tokens, target = _tile_order(x[0], tm), _tile_order(loss_target[0], tm)
    t_len = seq + tm
    wg_rows = 5 * tm if t_len % (5 * tm) == 0 else tm
    wg_rows_in = 13 * tm if t_len % (13 * tm) == 0 else wg_rows
    xi, yi, _ = _place()
    chip = 2 * xi + yi
    mesh_axes = ("x", "y", "c")

    wire = lambda w: w.astype(WIRE_DT)
    small_sharded = ["meta_tokens", "sc_conv_w", "rg_conv_w", "rg_conv_b", "rg_b_gate_a", "rg_b_gate_x", "rg_lambda", "ffn_conv_w"]
    small_2d = {n: weights[n].reshape(-1, weights[n].shape[-1]) for n in small_sharded}
    up0, dn0, rg_in, rg_out, up1, dn1, w_sc_in, w_sc_out, small_by_chip = _to_wire(
        [(ffn_w_up, 0), (ffn_w_down, 0), (rg_w_in, 0), (rg_w_out, 0), (ffn_w_up, 1), (ffn_w_down, 1)],
        ride=_GatherHalves([wire(sc_w_in[0]), wire(sc_w_out[0]), _pack([small_2d[n] for n in small_sharded], 2 * ACT_ROWS)]))
    w_sc_out = w_sc_out.reshape(-1, d)
    gather_ffn0 = _Gather([up0, dn0])
    gather_rest = _Gather([rg_in, rg_out, up1, dn1])
    small_len = sum(a.size for a in small_2d.values())
    by_chip = small_by_chip.reshape(N_CHIPS, -1)[:, :small_len]
    full, off = {}, 0
    for n in small_sharded:
        rows, width = small_2d[n].shape
        full[n] = by_chip[:, off:off + rows * width].reshape(N_CHIPS, rows, width).transpose(1, 0, 2).reshape(rows, N_CHIPS * width)
        off += rows * width
    sc_cw, rg_cw = full["sc_conv_w"], full["rg_conv_w"]
    ffn_cw = [full["ffn_conv_w"][0:3], full["ffn_conv_w"][3:6]]
    d_rnn = rg_cw.shape[1]
    vec = jnp.concatenate([full["rg_conv_b"], full["rg_b_gate_a"], full["rg_b_gate_x"], full["rg_lambda"],
                           jnp.zeros((F32_ROWS - 4, d_rnn), F32)])
    wa, wx = rg_w_gate_a[0].astype(MXU_DT), rg_w_gate_x[0].astype(MXU_DT)
    first = _tile_order(jnp.concatenate([jnp.zeros((tm - N_META, d), F32), full["meta_tokens"]]), tm)
    g_mix = [norm_mix_g[0:1], norm_mix_g[1:2]]
    g_ffn = [norm_ffn_g[0:1], norm_ffn_g[1:2]]

    h1, hh0, w_up0, w_dn0 = _sc_fwd(tokens, first, g_mix[0], w_sc_in, sc_cw, w_sc_out, tm=tm, ride=gather_ffn0)
    h2, hu0, hc0, act0, w_rg_in, w_rg_out, w_up1, w_dn1 = _ffn_fwd(h1, g_ffn[0], w_up0, ffn_cw[0], w_dn0.reshape(-1, d), tm=tm,
                                                         ride=gather_rest)
    w_up, w_dn, w_rg_out = [w_up0, w_up1], [w_dn0.reshape(-1, d), w_dn1.reshape(-1, d)], w_rg_out.reshape(-1, d)
    h3, hh1, hs, gates = _rg_fwd(h2, g_mix[1], w_rg_in, rg_cw, vec, wa, wx, w_rg_out, tm=tm)
    dh4, hu1, hc1, act1, sq, d_final = _ffn_fwd(h3, g_ffn[1], w_up[1], ffn_cw[1], w_dn[1], tm=tm,
                                     loss=(target, final_norm_g.reshape(1, d)))
    loss = lax.psum(jnp.sum(sq[0]) * (0.5 / d), mesh_axes)

    def by_chip_rows(pair):
        return [p.reshape(N_CHIPS, -1, d) for p in pair]

    def ffn_backward(dh_out, h_in, hu, hc, act, layer, ride):
        dh_in, dhu, hn, dcw, dg, *landed = _ffn_bwd(dh_out, hu, hc, h_in, g_ffn[layer], w_up[layer], ffn_cw[layer],
                                                         w_dn[layer], tm=tm, ride=ride)
        d_up = _weight_grad(hn, dhu, N_CHIPS, rows=wg_rows_in)
        d_dn = by_chip_rows(_weight_grad(act, dh_out, 1, rows=wg_rows))
        return dh_in, d_up, d_dn, dcw[0:3], dg[0], landed

    dh3, d_up1, d_dn1, d_fcw1, d_gf1, _ = ffn_backward(dh4, h3, hu1, hc1, act1, 1, None)
    dh2, dhh1, y_rg, hn_rg, d_vec, d_wa, d_wx, d_gm1, *landed_ffn1 = _rg_bwd(
        dh3, hh1, hs, gates, h2, g_mix[1], w_rg_in, rg_cw, vec, wa, wx, w_rg_out, tm=tm,
        ride=_Scatter([d_up1[1], d_dn1[1]]))
    d_rg_in = _weight_grad(hn_rg, dhh1, N_CHIPS, rows=wg_rows_in)
    d_rg_out = by_chip_rows(_weight_grad(y_rg, dh3, 1, rows=wg_rows))
    early = {"rg_conv_w": d_vec[G_CONV_W:G_CONV_W + 4], "rg_conv_b": d_vec[G_CONV_B:G_CONV_B + 1],
             "rg_b_gate_a": d_vec[G_B_A:G_B_A + 1], "rg_b_gate_x": d_vec[G_B_X:G_B_X + 1],
             "rg_lambda": d_vec[G_LAMBDA:G_LAMBDA + 1], "ffn_conv_w.1": d_fcw1, "norm_mix_g.1": d_gm1[0:1],
             "norm_ffn_g.1": d_gf1[None], "final_norm_g": d_final[0]}
    early_packed = _pack(list(early.values()))
    gate_names = ["rg_w_gate_a", "rg_w_gate_x"]
    to_all = _AllDevices([early_packed, d_wa.reshape(-1, LANES), d_wx.reshape(-1, LANES)])
    dh1, d_up0, d_dn0, d_fcw0, d_gf0, landed = ffn_backward(
        dh2, h1, hu0, hc0, act0, 0, _Both(_Scatter([d_rg_in[1], d_rg_out[1]]), to_all))
    landed_rg, early_by_device, gates_by_device = landed[0:2], landed[2], landed[3:]

    def core_sum(pair, received):
        own = lax.dynamic_index_in_dim(pair[0], chip, 0, keepdims=False)
        return _sum_parts(own, received, rows=_divisor_rows(own.shape[0]))

    early_big = [("rg_w_in", 0), ("rg_w_out", 0), ("ffn_w_up", 1), ("ffn_w_down", 1)]
    early_sum = [core_sum(d_rg_in, landed_rg[0]), core_sum(d_rg_out, landed_rg[1]), core_sum(d_up1, landed_ffn1[0]),
                 core_sum(d_dn1, landed_ffn1[1])]
    grad_x, dhh0, z_sc, hn_sc, d_sccw, d_gm0, d_first, *landed = _sc_bwd(
        dh1, hh0, tokens, first, g_mix[0], w_sc_in, sc_cw, w_sc_out, tm=tm,
        ride=_Both(_Scatter([d_up0[1], d_dn0[1]]), _Swap(early_sum)))
    landed_ffn0, early_other = landed[0:2], landed[2:]
    late = {"meta_tokens": _time_order(d_first, tm)[tm - N_META:], "sc_conv_w": d_sccw[0:3], "ffn_conv_w.0": d_fcw0,
            "norm_mix_g.0": d_gm0[0:1], "norm_ffn_g.0": d_gf0[None]}
    late_packed = _pack(list(late.values()))
    ffn0_big = [("ffn_w_up", 0), ("ffn_w_down", 0)]
    ffn0_sum = [core_sum(d_up0, landed_ffn0[0]), core_sum(d_dn0, landed_ffn0[1])]
    *d_sc_in, ffn0_up_other, ffn0_dn_other, late_by_device = _weight_grad(
        hn_sc, dhh0, N_CHIPS, rows=wg_rows_in, ride=_Both(_Swap(ffn0_sum), _AllDevices([late_packed])))
    *d_sc_out, landed_sc_in = _weight_grad(z_sc, dh1, 1, rows=wg_rows, ride=_Scatter([d_sc_in[1]]))
    d_sc_out = by_chip_rows(d_sc_out)
    landed_sc = [landed_sc_in, *_exchange(_Scatter([d_sc_out[1]]), "scatter_last")]
    grad_x = _time_order(grad_x, tm)[None]

    sc_big = [("sc_w_in", 0), ("sc_w_out", 0)]
    sc_sum = [core_sum(d_sc_in, landed_sc[0]), core_sum(d_sc_out, landed_sc[1])]
    sc_other = _exchange(_Swap(sc_sum), "swap_cores")
    out = {k: {} for k in ("grad", "delta", "m", "v")}
    stacked = {}
    for (n, layer), mine, theirs in zip(sc_big + ffn0_big + early_big, sc_sum + ffn0_sum + early_sum,
                                        [*sc_other, ffn0_up_other, ffn0_dn_other, *early_other]):
        stacked[n] = _adamw(weights[n], m_in[n], v_in[n], [mine, theirs], rows=_divisor_rows(mine.shape[0]), layer=layer,
                            into=stacked.get(n))
    for n, res in stacked.items():
        for k, key in enumerate(("grad", "delta", "m", "v")):
            out[key][n] = res[k]

    summed = {}
    for parts, packed, by_device in ((early, early_packed, early_by_device), (late, late_packed, late_by_device)):
        total = _sum_stack(by_device, rows=packed.shape[0])
        summed.update(zip(parts, _unpack(total, [p.shape for p in parts.values()])))
    for n in ("ffn_conv_w", "norm_mix_g", "norm_ffn_g"):
        summed[n] = jnp.concatenate([summed.pop(n + ".0"), summed.pop(n + ".1")])
    for n, by_device in zip(gate_names, gates_by_device):
        as_rows = lambda a: a.reshape(1, -1, LANES)
        res = _adamw(as_rows(weights[n]), as_rows(m_in[n]), as_rows(v_in[n]), [_sum_stack(by_device, rows=256)], rows=256)
        for k, key in enumerate(("grad", "delta", "m", "v")):
            out[key][n] = res[k].reshape(weights[n].shape)
    replicated = ["norm_mix_g", "norm_ffn_g", "final_norm_g"]
    small_names = small_sharded + replicated
    grads = {}
    for n in small_sharded:
        width = small_2d[n].shape[1]
        grads[n] = lax.dynamic_slice_in_dim(summed[n], chip * width, width, axis=1).reshape(weights[n].shape)
    for n in replicated:
        grads[n] = summed[n].reshape(weights[n].shape)
    shapes = [weights[n].shape for n in small_names]
    packed_w = _pack([weights[n] for n in small_names])
    res = _adamw(packed_w[None], _pack([m_in[n] for n in small_names])[None], _pack([v_in[n] for n in small_names])[None],
                 [_pack([grads[n] for n in small_names])], rows=packed_w.shape[0])
    for k, key in enumerate(("grad", "delta", "m", "v")):
        out[key].update(dict(zip(small_names, _unpack(res[k][0], shapes))))

    return (loss, grad_x, *[out["grad"][n] for n in names], *[out["delta"][n] for n in names],
            *[out["m"][n] for n in names], *[out["v"][n] for n in names])
```
